```python
import jax, jax.numpy as jnp
from jax import lax
import numpy as np

D_MODEL = 1024
BATCH = 32
SEQ = 2048
DEPTH = 1

CHUNK = 64
CONV_WIDTH = D_MODEL // 2
CONV_GROUPS = 8
CONV_K = 3
ATTN_HEADS = 8
HEAD_DIM = 64
ATTN_WIDTH = ATTN_HEADS * HEAD_DIM
N_BRANCH = 2
Q_BLOCK = 128
FFN_HIDDEN = 2816
FFN_CONV_K = 3
EPS = 1e-6
IN_SPLITS = (CONV_WIDTH, CONV_WIDTH, CONV_WIDTH, ATTN_WIDTH, ATTN_WIDTH, ATTN_WIDTH, ATTN_HEADS, N_BRANCH * D_MODEL)
IN_WIDTH = sum(IN_SPLITS)

kernel_name = "hybrid_gated_conv_fox_convffn"


def rms_norm(x, g):
    x32 = x.astype(jnp.float32)
    y = x32 * lax.rsqrt(jnp.mean(x32 * x32, axis=-1, keepdims=True) + EPS)
    return y.astype(x.dtype) * g


def causal_dwconv(x, w):
    K = w.shape[0]
    S = x.shape[1]
    xp = jnp.pad(x, ((0, 0), (K - 1, 0), (0, 0)))
    y = xp[:, K - 1:K - 1 + S, :] * w[K - 1]
    for k in range(K - 1):
        y = y + xp[:, k:k + S, :] * w[k]
    return y


def forgetting_attention(q, k, v, log_f):
    B, S, H, hd = q.shape
    scale = 1.0 / np.sqrt(hd).astype(np.float32)
    F = jnp.cumsum(log_f, axis=1).transpose(0, 2, 1)
    qh = q.transpose(0, 2, 1, 3).astype(jnp.float32) * scale
    kh = k.transpose(0, 2, 1, 3).astype(jnp.float32)
    vh = v.transpose(0, 2, 1, 3).astype(jnp.float32)
    nb = S // Q_BLOCK
    q_blocks = qh.reshape(B, H, nb, Q_BLOCK, hd).transpose(2, 0, 1, 3, 4)
    fq_blocks = F.reshape(B, H, nb, Q_BLOCK).transpose(2, 0, 1, 3)
    k_pos = jnp.arange(S)

    def one_block(args):
        q_blk, fq_blk, i = args
        q_pos = i * Q_BLOCK + jnp.arange(Q_BLOCK)
        s = jnp.einsum('bhqd,bhkd->bhqk', q_blk, kh) + fq_blk[..., None] - F[:, :, None, :]
        s = jnp.where(k_pos[None, :] <= q_pos[:, None], s, -jnp.inf)
        p = jax.nn.softmax(s, axis=-1)
        return jnp.einsum('bhqk,bhkd->bhqd', p, vh)

    o = lax.map(one_block, (q_blocks, fq_blocks, jnp.arange(nb)))
    o = o.transpose(1, 0, 3, 2, 4).reshape(B, S, H * hd)
    return o.astype(q.dtype)


def _fwd_setup_inputs(seed: int = 0) -> dict:
    key = jax.random.key(seed)
    ks = jax.random.split(key, 16)
    f32 = jnp.float32
    L = DEPTH
    x = jax.random.normal(ks[0], (BATCH, SEQ, D_MODEL), f32)
    norm_mix_g = 1.0 + 0.02 * jax.random.normal(ks[1], (L, D_MODEL), f32)
    w_in = jax.random.normal(ks[2], (L, D_MODEL, IN_WIDTH), f32) * D_MODEL ** -0.5
    b_f = 2.0 + 0.5 * jax.random.normal(ks[3], (L, ATTN_HEADS), f32)
    b_gate = 0.02 * jax.random.normal(ks[4], (L, N_BRANCH * D_MODEL), f32)
    conv_mix_w = jax.random.normal(ks[5], (L, CONV_K, CONV_WIDTH), f32) * CONV_K ** -0.5
    w_out_conv = jax.random.normal(ks[6], (L, CONV_WIDTH, D_MODEL), f32) * CONV_WIDTH ** -0.5
    w_out_attn = jax.random.normal(ks[7], (L, ATTN_WIDTH, D_MODEL), f32) * ATTN_WIDTH ** -0.5
    w_o = jax.random.normal(ks[8], (L, D_MODEL, D_MODEL), f32) * D_MODEL ** -0.5
    norm_ffn_g = 1.0 + 0.02 * jax.random.normal(ks[9], (L, D_MODEL), f32)
    w_up = jax.random.normal(ks[10], (L, D_MODEL, 2 * FFN_HIDDEN), f32) * D_MODEL ** -0.5
    conv_ffn_w = jax.random.normal(ks[11], (L, FFN_CONV_K, 2 * FFN_HIDDEN), f32) * FFN_CONV_K ** -0.5
    w_down = jax.random.normal(ks[12], (L, FFN_HIDDEN, D_MODEL), f32) * FFN_HIDDEN ** -0.5
    norm_f_g = 1.0 + 0.02 * jax.random.normal(ks[13], (D_MODEL,), f32)
    return {"x": x, "norm_mix_g": norm_mix_g, "w_in": w_in, "b_f": b_f, "b_gate": b_gate,
            "conv_mix_w": conv_mix_w, "w_out_conv": w_out_conv, "w_out_attn": w_out_attn,
            "w_o": w_o, "norm_ffn_g": norm_ffn_g, "w_up": w_up, "conv_ffn_w": conv_ffn_w,
            "w_down": w_down, "norm_f_g": norm_f_g}


def _fwd_reference(x, norm_mix_g, w_in, b_f, b_gate, conv_mix_w, w_out_conv, w_out_attn,
              w_o, norm_ffn_g, w_up, conv_ffn_w, w_down, norm_f_g):
    B, S, _ = x.shape
    split_idx = list(np.cumsum(IN_SPLITS)[:-1])
    for l in range(DEPTH):
        h = rms_norm(x, norm_mix_g[l])
        proj = h @ w_in[l]
        cb, cc, cin, q, k, v, f_logit, g_logit = jnp.split(proj, split_idx, axis=-1)
        u = causal_dwconv(cc * cin, conv_mix_w[l])
        y_conv = (cb * u) @ w_out_conv[l]
        log_f = jax.nn.log_sigmoid((f_logit + b_f[l]).astype(jnp.float32))
        o = forgetting_attention(q.reshape(B, S, ATTN_HEADS, HEAD_DIM),
                                 k.reshape(B, S, ATTN_HEADS, HEAD_DIM),
                                 v.reshape(B, S, ATTN_HEADS, HEAD_DIM), log_f)
        y_attn = o @ w_out_attn[l]
        gates = jax.nn.sigmoid(g_logit + b_gate[l])
        g_conv, g_attn = jnp.split(gates, 2, axis=-1)
        x = x + (g_conv * y_conv + g_attn * y_attn) @ w_o[l]
        h = rms_norm(x, norm_ffn_g[l])
        up = causal_dwconv(h @ w_up[l], conv_ffn_w[l])
        a, b = jnp.split(up, 2, axis=-1)
        x = x + (jax.nn.silu(a) * b) @ w_down[l]
    return rms_norm(x, norm_f_g)


import jax as _jax
import jax.numpy as _jnp

TWIN_FORMAT = 'train_step'
FWD_PARAMS = ['x', 'norm_mix_g', 'w_in', 'b_f', 'b_gate', 'conv_mix_w', 'w_out_conv', 'w_out_attn', 'w_o', 'norm_ffn_g', 'w_up', 'conv_ffn_w', 'w_down', 'norm_f_g']
TWIN_WEIGHTS = ['norm_mix_g', 'w_in', 'b_f', 'b_gate', 'conv_mix_w', 'w_out_conv', 'w_out_attn', 'w_o', 'norm_ffn_g', 'w_up', 'conv_ffn_w', 'w_down', 'norm_f_g']
TWIN_DIFF_INPUT = 'x'
TWIN_INPUTS = ['x', 'norm_mix_g', 'w_in', 'b_f', 'b_gate', 'conv_mix_w', 'w_out_conv', 'w_out_attn', 'w_o', 'norm_ffn_g', 'w_up', 'conv_ffn_w', 'w_down', 'norm_f_g', 'loss_target', 'm_norm_mix_g', 'm_w_in', 'm_b_f', 'm_b_gate', 'm_conv_mix_w', 'm_w_out_conv', 'm_w_out_attn', 'm_w_o', 'm_norm_ffn_g', 'm_w_up', 'm_conv_ffn_w', 'm_w_down', 'm_norm_f_g', 'v_norm_mix_g', 'v_w_in', 'v_b_f', 'v_b_gate', 'v_conv_mix_w', 'v_w_out_conv', 'v_w_out_attn', 'v_w_o', 'v_norm_ffn_g', 'v_w_up', 'v_conv_ffn_w', 'v_w_down', 'v_norm_f_g']
TWIN_OUTPUTS = ['loss', 'grad_x', 'grad_norm_mix_g', 'grad_w_in', 'grad_b_f', 'grad_b_gate', 'grad_conv_mix_w', 'grad_w_out_conv', 'grad_w_out_attn', 'grad_w_o', 'grad_norm_ffn_g', 'grad_w_up', 'grad_conv_ffn_w', 'grad_w_down', 'grad_norm_f_g', 'delta_norm_mix_g', 'delta_w_in', 'delta_b_f', 'delta_b_gate', 'delta_conv_mix_w', 'delta_w_out_conv', 'delta_w_out_attn', 'delta_w_o', 'delta_norm_ffn_g', 'delta_w_up', 'delta_conv_ffn_w', 'delta_w_down', 'delta_norm_f_g', 'new_m_norm_mix_g', 'new_m_w_in', 'new_m_b_f', 'new_m_b_gate', 'new_m_conv_mix_w', 'new_m_w_out_conv', 'new_m_w_out_attn', 'new_m_w_o', 'new_m_norm_ffn_g', 'new_m_w_up', 'new_m_conv_ffn_w', 'new_m_w_down', 'new_m_norm_f_g', 'new_v_norm_mix_g', 'new_v_w_in', 'new_v_b_f', 'new_v_b_gate', 'new_v_conv_mix_w', 'new_v_w_out_conv', 'new_v_w_out_attn', 'new_v_w_o', 'new_v_norm_ffn_g', 'new_v_w_up', 'new_v_conv_ffn_w', 'new_v_w_down', 'new_v_norm_f_g']
TWIN_LEAF_KINDS = {'loss': 'loss', 'grad_x': 'grad_x', 'grad_norm_mix_g': 'grad_w', 'grad_w_in': 'grad_w', 'grad_b_f': 'grad_w', 'grad_b_gate': 'grad_w', 'grad_conv_mix_w': 'grad_w', 'grad_w_out_conv': 'grad_w', 'grad_w_out_attn': 'grad_w', 'grad_w_o': 'grad_w', 'grad_norm_ffn_g': 'grad_w', 'grad_w_up': 'grad_w', 'grad_conv_ffn_w': 'grad_w', 'grad_w_down': 'grad_w', 'grad_norm_f_g': 'grad_w', 'delta_norm_mix_g': 'delta_w', 'delta_w_in': 'delta_w', 'delta_b_f': 'delta_w', 'delta_b_gate': 'delta_w', 'delta_conv_mix_w': 'delta_w', 'delta_w_out_conv': 'delta_w', 'delta_w_out_attn': 'delta_w', 'delta_w_o': 'delta_w', 'delta_norm_ffn_g': 'delta_w', 'delta_w_up': 'delta_w', 'delta_conv_ffn_w': 'delta_w', 'delta_w_down': 'delta_w', 'delta_norm_f_g': 'delta_w', 'new_m_norm_mix_g': 'new_m', 'new_m_w_in': 'new_m', 'new_m_b_f': 'new_m', 'new_m_b_gate': 'new_m', 'new_m_conv_mix_w': 'new_m', 'new_m_w_out_conv': 'new_m', 'new_m_w_out_attn': 'new_m', 'new_m_w_o': 'new_m', 'new_m_norm_ffn_g': 'new_m', 'new_m_w_up': 'new_m', 'new_m_conv_ffn_w': 'new_m', 'new_m_w_down': 'new_m', 'new_m_norm_f_g': 'new_m', 'new_v_norm_mix_g': 'new_v', 'new_v_w_in': 'new_v', 'new_v_b_f': 'new_v', 'new_v_b_gate': 'new_v', 'new_v_conv_mix_w': 'new_v', 'new_v_w_out_conv': 'new_v', 'new_v_w_out_attn': 'new_v', 'new_v_w_o': 'new_v', 'new_v_norm_ffn_g': 'new_v', 'new_v_w_up': 'new_v', 'new_v_conv_ffn_w': 'new_v', 'new_v_w_down': 'new_v', 'new_v_norm_f_g': 'new_v'}


def _forward(args):
    return _fwd_reference(*[args[k] for k in FWD_PARAMS])


def _output_shape():
    out = _jax.eval_shape(lambda: _forward(_fwd_setup_inputs(0)))
    return out.shape, out.dtype

N_MICROBATCH = 1
ADAM_LR = 0.001
ADAM_B1 = 0.9
ADAM_B2 = 0.999
ADAM_EPS = 1e-08
ADAM_WD = 0.01
ADAM_STEP = 10
PER_EXAMPLE_BATCH_AXIS = {'x': 0, 'loss_target': 0}
SHARED_INPUTS = []
_WEIGHT_DTYPES = {'norm_mix_g': _jnp.float32, 'w_in': _jnp.float32, 'b_f': _jnp.float32, 'b_gate': _jnp.float32, 'conv_mix_w': _jnp.float32, 'w_out_conv': _jnp.float32, 'w_out_attn': _jnp.float32, 'w_o': _jnp.float32, 'norm_ffn_g': _jnp.float32, 'w_up': _jnp.float32, 'conv_ffn_w': _jnp.float32, 'w_down': _jnp.float32, 'norm_f_g': _jnp.float32}
MOMENT_SCALE = {'norm_mix_g': 2.511110e-01, 'w_in': 1.107635e-01, 'b_f': 4.746007e-01, 'b_gate': 4.399731e-02, 'conv_mix_w': 1.813746e-01, 'w_out_conv': 1.347128e-01, 'w_out_attn': 5.526267e-02, 'w_o': 1.426192e-01, 'norm_ffn_g': 1.734522e-01, 'w_up': 7.282455e-02, 'conv_ffn_w': 7.176429e-02, 'w_down': 1.188763e-01, 'norm_f_g': 6.402243e+01}


def _to_microbatches(a, axis):
    t = _jnp.moveaxis(a, axis, 0)
    t = t.reshape((N_MICROBATCH, t.shape[0] // N_MICROBATCH) + t.shape[1:])
    return _jnp.moveaxis(t, 1, axis + 1)


def setup_inputs(seed: int = 0) -> dict:
    inp = _fwd_setup_inputs(seed)
    key = _jax.random.fold_in(_jax.random.key(seed), 7919)
    shape, _ = _output_shape()
    out = dict(inp)
    out["loss_target"] = _jax.random.normal(_jax.random.fold_in(key, 0), shape, _jnp.float32)
    for i, name in enumerate(TWIN_WEIGHTS):
        w = inp[name].astype(_jnp.float32)
        if MOMENT_SCALE is None:
            s = _jnp.sqrt(_jnp.mean(_jnp.square(w)) + 1e-30)
        else:
            s = MOMENT_SCALE[name]
        km, kv = _jax.random.split(_jax.random.fold_in(key, i + 1))
        out[name] = w
        out["m_" + name] = s * _jax.random.normal(km, w.shape, _jnp.float32)
        out["v_" + name] = (s * s) * _jax.random.uniform(kv, w.shape, _jnp.float32, 0.5, 1.5)
    if N_MICROBATCH > 1:
        for name, axis in PER_EXAMPLE_BATCH_AXIS.items():
            out[name] = _to_microbatches(out[name], axis)
    return {'x': out['x'], 'norm_mix_g': out['norm_mix_g'], 'w_in': out['w_in'], 'b_f': out['b_f'], 'b_gate': out['b_gate'], 'conv_mix_w': out['conv_mix_w'], 'w_out_conv': out['w_out_conv'], 'w_out_attn': out['w_out_attn'], 'w_o': out['w_o'], 'norm_ffn_g': out['norm_ffn_g'], 'w_up': out['w_up'], 'conv_ffn_w': out['conv_ffn_w'], 'w_down': out['w_down'], 'norm_f_g': out['norm_f_g'], 'loss_target': out['loss_target'], 'm_norm_mix_g': out['m_norm_mix_g'], 'm_w_in': out['m_w_in'], 'm_b_f': out['m_b_f'], 'm_b_gate': out['m_b_gate'], 'm_conv_mix_w': out['m_conv_mix_w'], 'm_w_out_conv': out['m_w_out_conv'], 'm_w_out_attn': out['m_w_out_attn'], 'm_w_o': out['m_w_o'], 'm_norm_ffn_g': out['m_norm_ffn_g'], 'm_w_up': out['m_w_up'], 'm_conv_ffn_w': out['m_conv_ffn_w'], 'm_w_down': out['m_w_down'], 'm_norm_f_g': out['m_norm_f_g'], 'v_norm_mix_g': out['v_norm_mix_g'], 'v_w_in': out['v_w_in'], 'v_b_f': out['v_b_f'], 'v_b_gate': out['v_b_gate'], 'v_conv_mix_w': out['v_conv_mix_w'], 'v_w_out_conv': out['v_w_out_conv'], 'v_w_out_attn': out['v_w_out_attn'], 'v_w_o': out['v_w_o'], 'v_norm_ffn_g': out['v_norm_ffn_g'], 'v_w_up': out['v_w_up'], 'v_conv_ffn_w': out['v_conv_ffn_w'], 'v_w_down': out['v_w_down'], 'v_norm_f_g': out['v_norm_f_g']}


def _loss(weights, diff, rest, loss_target):
    with _jax.named_scope("forward"):
        args = {**rest, TWIN_DIFF_INPUT: diff, **{k: w.astype(_WEIGHT_DTYPES[k]) for k, w in weights.items()}}
        y = _forward(args)
    with _jax.named_scope("loss_head"):
        err = _jnp.square(y.astype(_jnp.float32) - loss_target)
        return 0.5 * _jnp.sum(_jnp.mean(err, axis=-1)) if err.ndim else 0.5 * err


def _adamw(w, g, m, v):
    m = ADAM_B1 * m + (1.0 - ADAM_B1) * g
    v = ADAM_B2 * v + (1.0 - ADAM_B2) * _jnp.square(g)
    m_hat = m / (1.0 - ADAM_B1 ** ADAM_STEP)
    v_hat = v / (1.0 - ADAM_B2 ** ADAM_STEP)
    delta = -ADAM_LR * (m_hat / (_jnp.sqrt(v_hat) + ADAM_EPS) + ADAM_WD * w)
    return delta, m, v


def reference(x, norm_mix_g, w_in, b_f, b_gate, conv_mix_w, w_out_conv, w_out_attn, w_o, norm_ffn_g, w_up, conv_ffn_w, w_down, norm_f_g, loss_target, m_norm_mix_g, m_w_in, m_b_f, m_b_gate, m_conv_mix_w, m_w_out_conv, m_w_out_attn, m_w_o, m_norm_ffn_g, m_w_up, m_conv_ffn_w, m_w_down, m_norm_f_g, v_norm_mix_g, v_w_in, v_b_f, v_b_gate, v_conv_mix_w, v_w_out_conv, v_w_out_attn, v_w_o, v_norm_ffn_g, v_w_up, v_conv_ffn_w, v_w_down, v_norm_f_g):
    given = dict(x=x, norm_mix_g=norm_mix_g, w_in=w_in, b_f=b_f, b_gate=b_gate, conv_mix_w=conv_mix_w, w_out_conv=w_out_conv, w_out_attn=w_out_attn, w_o=w_o, norm_ffn_g=norm_ffn_g, w_up=w_up, conv_ffn_w=conv_ffn_w, w_down=w_down, norm_f_g=norm_f_g, loss_target=loss_target, m_norm_mix_g=m_norm_mix_g, m_w_in=m_w_in, m_b_f=m_b_f, m_b_gate=m_b_gate, m_conv_mix_w=m_conv_mix_w, m_w_out_conv=m_w_out_conv, m_w_out_attn=m_w_out_attn, m_w_o=m_w_o, m_norm_ffn_g=m_norm_ffn_g, m_w_up=m_w_up, m_conv_ffn_w=m_conv_ffn_w, m_w_down=m_w_down, m_norm_f_g=m_norm_f_g, v_norm_mix_g=v_norm_mix_g, v_w_in=v_w_in, v_b_f=v_b_f, v_b_gate=v_b_gate, v_conv_mix_w=v_conv_mix_w, v_w_out_conv=v_w_out_conv, v_w_out_attn=v_w_out_attn, v_w_o=v_w_o, v_norm_ffn_g=v_norm_ffn_g, v_w_up=v_w_up, v_conv_ffn_w=v_conv_ffn_w, v_w_down=v_w_down, v_norm_f_g=v_norm_f_g)
    weights = {n: given[n] for n in TWIN_WEIGHTS}
    shared = {n: given[n] for n in SHARED_INPUTS}
    per_example = {n: given[n] for n in ['x']}
    grad_fn = _jax.value_and_grad(_loss, argnums=(0, 1))

    def one_microbatch(ex, loss_target):
        ex = dict(ex)
        diff = ex.pop(TWIN_DIFF_INPUT)
        return grad_fn(weights, diff, {**shared, **ex}, loss_target)

    if N_MICROBATCH == 1:
        loss, (grad_w, grad_x) = one_microbatch(per_example, given["loss_target"])
    else:
        def body(carry, xs):
            loss_sum, grad_sum = carry
            l_k, (gw_k, gx_k) = one_microbatch(xs[0], xs[1])
            with _jax.named_scope("update"):
                return (loss_sum + l_k, _jax.tree.map(_jnp.add, grad_sum, gw_k)), gx_k

        init = (_jnp.zeros((), _jnp.float32), _jax.tree.map(_jnp.zeros_like, weights))
        (loss, grad_w), grad_x = _jax.lax.scan(body, init, (per_example, given["loss_target"]))
    with _jax.named_scope("update"):
        delta_w, new_m, new_v = {}, {}, {}
        for n in TWIN_WEIGHTS:
            delta_w[n], new_m[n], new_v[n] = _adamw(weights[n], grad_w[n], given["m_" + n], given["v_" + n])
    return (loss, grad_x, *[grad_w[n] for n in TWIN_WEIGHTS], *[delta_w[n] for n in TWIN_WEIGHTS],
            *[new_m[n] for n in TWIN_WEIGHTS], *[new_v[n] for n in TWIN_WEIGHTS])
```

```python
import functools
import math

import jax
import jax.numpy as jnp
from jax import lax
from jax.experimental import pallas as pl
from jax.experimental.pallas import tpu as pltpu

F32 = jnp.float32
BF16 = jnp.bfloat16
MESH = pl.DeviceIdType.MESH

EPS = 1e-6
HEADS = 8
HEAD_DIM = 64
ATTN_WIDTH = HEADS * HEAD_DIM
HEAD_PAIRS = HEADS // 2
LANES = 128
F_PAD = 2 * LANES
NEG_BIG = -1e30
N_CHIPS = 4
N_DEV = 8

ADAM_LR = 0.001
ADAM_B1 = 0.9
ADAM_B2 = 0.999
ADAM_EPS = 1e-08
ADAM_WD = 0.01
ADAM_STEP = 10

_DIMS = {
    "nn": (((1,), (0,)), ((), ())),
    "nt": (((1,), (1,)), ((), ())),
    "tn": (((0,), (0,)), ((), ())),
}


def _tile(n, target, mult, also=()):
    best = None
    for t in range(mult, n + 1, mult):
        if n % t == 0 and t <= target and all(o % t == 0 for o in also):
            best = t
    if best is None:
        assert all(o == 0 for o in also), (n, target, mult, also)
        return n
    return best


def _sds(shape, dtype):
    return jax.ShapeDtypeStruct(shape, dtype)


def _mm(name, a, b, mode, out_dtype, *, m, n, k, a_off=0, b_off=0, out=None, o_off=0, o_width=None,
        add=None, tm=1024, tn=2048, tk=2048):
    if mode == "nn":
        tm = _tile(m, tm, 16)
        tk = _tile(k, tk, LANES, (a_off,))
        tn = _tile(n, tn, LANES, (b_off, o_off))
        a_spec = pl.BlockSpec((tm, tk), lambda i, j, kk: (i, a_off // tk + kk))
        b_spec = pl.BlockSpec((tk, tn), lambda i, j, kk: (kk, b_off // tn + j))
    elif mode == "nt":
        tm = _tile(m, tm, 16)
        tk = _tile(k, tk, LANES, (a_off, b_off))
        tn = _tile(n, tn, LANES, (o_off,))
        a_spec = pl.BlockSpec((tm, tk), lambda i, j, kk: (i, a_off // tk + kk))
        b_spec = pl.BlockSpec((tn, tk), lambda i, j, kk: (j, b_off // tk + kk))
    else:
        tm = _tile(m, tm, LANES, (a_off,))
        tk = _tile(k, tk, 16)
        tn = _tile(n, tn, LANES, (b_off, o_off))
        a_spec = pl.BlockSpec((tk, tm), lambda i, j, kk: (kk, a_off // tm + i))
        b_spec = pl.BlockSpec((tk, tn), lambda i, j, kk: (kk, b_off // tn + j))
    nk = k // tk
    o_spec = pl.BlockSpec((tm, tn), lambda i, j, kk: (i, o_off // tn + j))
    width = o_width if o_width is not None else (out.shape[1] if out is not None else n)
    use_acc = nk > 1 and out_dtype != F32
    dims = _DIMS[mode]
    has_add, has_out = add is not None, out is not None

    def body(*refs):
        a_ref, b_ref = refs[0], refs[1]
        pos = 2
        add_ref = None
        if has_add:
            add_ref = refs[pos]
            pos += 1
        if has_out:
            pos += 1
        o_ref = refs[pos]
        acc_ref = refs[pos + 1] if use_acc else None
        part = lax.dot_general(a_ref[...].astype(BF16), b_ref[...].astype(BF16), dims,
                               preferred_element_type=F32)
        if nk == 1:
            if has_add:
                part = part + add_ref[...]
            o_ref[...] = part.astype(o_ref.dtype)
            return
        kk = pl.program_id(2)
        tgt = acc_ref if use_acc else o_ref

        @pl.when(kk == 0)
        def _():
            tgt[...] = part + add_ref[...] if has_add else part

        @pl.when(kk > 0)
        def _():
            tgt[...] += part

        if use_acc:
            @pl.when(kk == nk - 1)
            def _():
                o_ref[...] = acc_ref[...].astype(o_ref.dtype)

    operands, in_specs = [a, b], [a_spec, b_spec]
    if has_add:
        operands.append(add)
        in_specs.append(pl.BlockSpec((tm, tn), lambda i, j, kk: (i, j)))
    aliases = {}
    if has_out:
        aliases = {len(operands): 0}
        operands.append(out)
        in_specs.append(pl.BlockSpec(memory_space=pl.ANY))
    return pl.pallas_call(
        body,
        out_shape=_sds((m, width), out_dtype),
        grid=(m // tm, n // tn, nk),
        in_specs=in_specs,
        out_specs=o_spec,
        scratch_shapes=[pltpu.VMEM((tm, tn), F32)] if use_acc else [],
        input_output_aliases=aliases,
        compiler_params=pltpu.CompilerParams(dimension_semantics=("parallel", "parallel", "arbitrary")),
        name=name,
    )(*operands)


def _rms_fwd(name, x, g):
    t, d = x.shape
    tm = _tile(t, 512, 16)

    def body(x_ref, g_ref, o_ref):
        xv = x_ref[...]
        r = lax.rsqrt(jnp.mean(xv * xv, axis=-1, keepdims=True) + EPS)
        o_ref[...] = ((xv * r) * g_ref[...]).astype(o_ref.dtype)

    return pl.pallas_call(
        body,
        out_shape=_sds((t, d), BF16),
        grid=(t // tm,),
        in_specs=[pl.BlockSpec((tm, d), lambda i: (i, 0)), pl.BlockSpec((1, d), lambda i: (0, 0))],
        out_specs=pl.BlockSpec((tm, d), lambda i: (i, 0)),
        compiler_params=pltpu.CompilerParams(dimension_semantics=("parallel",)),
        name=name,
    )(x, g)


def _rms_bwd(name, x, dh, g, res):
    t, d = x.shape
    tm = _tile(t, 512, 16)

    def body(x_ref, dh_ref, g_ref, res_ref, dx_ref, dg_ref):
        xv = x_ref[...]
        r = lax.rsqrt(jnp.mean(xv * xv, axis=-1, keepdims=True) + EPS)
        xh = xv * r
        dhv = dh_ref[...]
        dxh = dhv * g_ref[...]
        dx_ref[...] = res_ref[...] + r * (dxh - xh * jnp.mean(dxh * xh, axis=-1, keepdims=True))

        @pl.when(pl.program_id(0) == 0)
        def _():
            dg_ref[...] = jnp.zeros_like(dg_ref)

        dg_ref[...] += jnp.sum(dhv * xh, axis=0, keepdims=True)

    row = pl.BlockSpec((tm, d), lambda i: (i, 0))
    vec = pl.BlockSpec((1, d), lambda i: (0, 0))
    return pl.pallas_call(
        body,
        out_shape=(_sds((t, d), F32), _sds((1, d), F32)),
        grid=(t // tm,),
        in_specs=[row, row, vec, row],
        out_specs=(row, vec),
        compiler_params=pltpu.CompilerParams(dimension_semantics=("arbitrary",)),
        name=name,
    )(x, dh, g, res)


def _final_loss(name, x, g, target):
    t, d = x.shape
    tm = _tile(t, 512, 16)

    def body(x_ref, g_ref, t_ref, dx_ref, loss_ref, dg_ref):
        xv = x_ref[...]
        gv = g_ref[...]
        r = lax.rsqrt(jnp.mean(xv * xv, axis=-1, keepdims=True) + EPS)
        xh = xv * r
        err = xh * gv - t_ref[...]
        dy = err * (1.0 / d)
        dxh = dy * gv
        dx_ref[...] = r * (dxh - xh * jnp.mean(dxh * xh, axis=-1, keepdims=True))
        per_row = jnp.sum(err * err, axis=-1, keepdims=True) * (0.5 / d)

        @pl.when(pl.program_id(0) == 0)
        def _():
            dg_ref[...] = jnp.zeros_like(dg_ref)
            loss_ref[...] = jnp.zeros_like(loss_ref)

        dg_ref[...] += jnp.sum(dy * xh, axis=0, keepdims=True)
        loss_ref[...] += jnp.sum(per_row, axis=0, keepdims=True)

    row = pl.BlockSpec((tm, d), lambda i: (i, 0))
    vec = pl.BlockSpec((1, d), lambda i: (0, 0))
    return pl.pallas_call(
        body,
        out_shape=(_sds((t, d), F32), _sds((1, LANES), F32), _sds((1, d), F32)),
        grid=(t // tm,),
        in_specs=[row, vec, row],
        out_specs=(row, pl.BlockSpec((1, LANES), lambda i: (0, 0)), vec),
        compiler_params=pltpu.CompilerParams(dimension_semantics=("arbitrary",)),
        name=name,
    )(x, g, target)


def _shift_down(z, k):
    row = lax.broadcasted_iota(jnp.int32, z.shape, 0)
    return jnp.where(row >= k, pltpu.roll(z, k, axis=0), 0.0)


def _shift_up(z, k):
    s = z.shape[0]
    row = lax.broadcasted_iota(jnp.int32, z.shape, 0)
    return jnp.where(row < s - k, pltpu.roll(z, s - k, axis=0), 0.0)


def _conv3(z, w):
    return (w[2:3] * z + w[0:1] * _shift_down(z, 2)) + w[1:2] * _shift_down(z, 1)


def _conv3_t(dz, w):
    return (w[2:3] * dz + w[0:1] * _shift_up(dz, 2)) + w[1:2] * _shift_up(dz, 1)


def _conv_fwd(name, pc, w, batch, seq, tc):
    cw = w.shape[1]
    nct = cw // tc

    def body(pc_ref, w_ref, o_ref):
        cb = pc_ref[:, 0:tc]
        z = pc_ref[:, tc:2 * tc] * pc_ref[:, 2 * tc:3 * tc]
        o_ref[...] = (cb * _conv3(z, w_ref[...])).astype(o_ref.dtype)

    return pl.pallas_call(
        body,
        out_shape=_sds((batch * seq, cw), BF16),
        grid=(batch, nct),
        in_specs=[pl.BlockSpec((seq, 3 * tc), lambda b, j: (b, j)), pl.BlockSpec((3, tc), lambda b, j: (0, j))],
        out_specs=pl.BlockSpec((seq, tc), lambda b, j: (b, j)),
        compiler_params=pltpu.CompilerParams(dimension_semantics=("parallel", "parallel")),
        name=name,
    )(pc, w)


def _conv_bwd(name, da, pc, w, dproj, batch, seq, tc):
    cw = w.shape[1]
    nct = cw // tc

    def body(da_ref, pc_ref, w_ref, _, dpc_ref, dw_ref):
        wv = w_ref[...]
        cb = pc_ref[:, 0:tc]
        cc = pc_ref[:, tc:2 * tc]
        cin = pc_ref[:, 2 * tc:3 * tc]
        z = cc * cin
        dav = da_ref[...]
        du = dav * cb
        dz = _conv3_t(du, wv)
        dpc_ref[:, 0:tc] = (dav * _conv3(z, wv)).astype(dpc_ref.dtype)
        dpc_ref[:, tc:2 * tc] = (dz * cin).astype(dpc_ref.dtype)
        dpc_ref[:, 2 * tc:3 * tc] = (dz * cc).astype(dpc_ref.dtype)

        @pl.when(pl.program_id(1) == 0)
        def _():
            dw_ref[...] = jnp.zeros_like(dw_ref)

        dw_ref[0:1, :] += jnp.sum(du * _shift_down(z, 2), axis=0, keepdims=True)
        dw_ref[1:2, :] += jnp.sum(du * _shift_down(z, 1), axis=0, keepdims=True)
        dw_ref[2:3, :] += jnp.sum(du * z, axis=0, keepdims=True)

    return pl.pallas_call(
        body,
        out_shape=(_sds(dproj.shape, dproj.dtype), _sds((3, cw), F32)),
        grid=(nct, batch),
        in_specs=[
            pl.BlockSpec((seq, tc), lambda j, b: (b, j)),
            pl.BlockSpec((seq, 3 * tc), lambda j, b: (b, j)),
            pl.BlockSpec((3, tc), lambda j, b: (0, j)),
            pl.BlockSpec(memory_space=pl.ANY),
        ],
        out_specs=(pl.BlockSpec((seq, 3 * tc), lambda j, b: (b, j)), pl.BlockSpec((3, tc), lambda j, b: (0, j))),
        input_output_aliases={3: 0},
        compiler_params=pltpu.CompilerParams(dimension_semantics=("parallel", "arbitrary")),
        name=name,
    )(da, pc, w, dproj)


def _ffn_fwd(name, upre, w, batch, seq, tc):
    fh = w.shape[1] // 2
    nf = fh // tc

    def body(ua_ref, ub_ref, wa_ref, wb_ref, o_ref):
        a = _conv3(ua_ref[...], wa_ref[...])
        b = _conv3(ub_ref[...], wb_ref[...])
        o_ref[...] = (a * jax.nn.sigmoid(a) * b).astype(o_ref.dtype)

    return pl.pallas_call(
        body,
        out_shape=_sds((batch * seq, fh), BF16),
        grid=(batch, nf),
        in_specs=[
            pl.BlockSpec((seq, tc), lambda b, j: (b, j)),
            pl.BlockSpec((seq, tc), lambda b, j: (b, nf + j)),
            pl.BlockSpec((3, tc), lambda b, j: (0, j)),
            pl.BlockSpec((3, tc), lambda b, j: (0, nf + j)),
        ],
        out_specs=pl.BlockSpec((seq, tc), lambda b, j: (b, j)),
        compiler_params=pltpu.CompilerParams(dimension_semantics=("parallel", "parallel")),
        name=name,
    )(upre, upre, w, w)


def _ffn_bwd(name, dh, upre, w, batch, seq, tc):
    fh = w.shape[1] // 2
    nf = fh // tc

    def body(dh_ref, ua_ref, ub_ref, wa_ref, wb_ref, dua_ref, dub_ref, dwa_ref, dwb_ref):
        ua, ub, wa, wb = ua_ref[...], ub_ref[...], wa_ref[...], wb_ref[...]
        a = _conv3(ua, wa)
        b = _conv3(ub, wb)
        sg = jax.nn.sigmoid(a)
        dhv = dh_ref[...]
        da = dhv * b * (sg * (1.0 + a * (1.0 - sg)))
        db = dhv * (a * sg)
        dua_ref[...] = _conv3_t(da, wa).astype(dua_ref.dtype)
        dub_ref[...] = _conv3_t(db, wb).astype(dub_ref.dtype)

        @pl.when(pl.program_id(1) == 0)
        def _():
            dwa_ref[...] = jnp.zeros_like(dwa_ref)
            dwb_ref[...] = jnp.zeros_like(dwb_ref)

        for d_ref, dv, uv in ((dwa_ref, da, ua), (dwb_ref, db, ub)):
            d_ref[0:1, :] += jnp.sum(dv * _shift_down(uv, 2), axis=0, keepdims=True)
            d_ref[1:2, :] += jnp.sum(dv * _shift_down(uv, 1), axis=0, keepdims=True)
            d_ref[2:3, :] += jnp.sum(dv * uv, axis=0, keepdims=True)

    act = pl.BlockSpec((seq, tc), lambda j, b: (b, j))
    wsp = pl.BlockSpec((3, tc), lambda j, b: (0, j))
    return pl.pallas_call(
        body,
        out_shape=(_sds((batch * seq, fh), BF16), _sds((batch * seq, fh), BF16), _sds((3, fh), F32), _sds((3, fh), F32)),
        grid=(nf, batch),
        in_specs=[
            act,
            act,
            pl.BlockSpec((seq, tc), lambda j, b: (b, nf + j)),
            wsp,
            pl.BlockSpec((3, tc), lambda j, b: (0, nf + j)),
        ],
        out_specs=(act, act, wsp, wsp),
        compiler_params=pltpu.CompilerParams(dimension_semantics=("parallel", "arbitrary")),
        name=name,
    )(dh, upre, upre, w, w)


def _merge_fwd(name, ycat, gl, bg):
    t, d2 = ycat.shape
    d = d2 // 2
    tm = _tile(t, 256, 16)

    def body(y_ref, gl_ref, bg_ref, o_ref):
        g = jax.nn.sigmoid(gl_ref[...] + bg_ref[...])
        prod = g * y_ref[...]
        o_ref[...] = (prod[:, 0:d] + prod[:, d:d2]).astype(o_ref.dtype)

    row = pl.BlockSpec((tm, d2), lambda i: (i, 0))
    return pl.pallas_call(
        body,
        out_shape=_sds((t, d), BF16),
        grid=(t // tm,),
        in_specs=[row, row, pl.BlockSpec((1, d2), lambda i: (0, 0))],
        out_specs=pl.BlockSpec((tm, d), lambda i: (i, 0)),
        compiler_params=pltpu.CompilerParams(dimension_semantics=("parallel",)),
        name=name,
    )(ycat, gl, bg)


def _merge_bwd(name, dm, ycat, gl, bg, width, gl_off):
    t, d2 = ycat.shape
    d = d2 // 2
    tm = _tile(t, 512, 16)
    wb = math.gcd(gl_off, d)
    nw = d // wb

    def body(dm_ref, y_ref, gl_ref, bg_ref, dgl_ref, dy_ref, dbg_ref):
        g = jax.nn.sigmoid(gl_ref[...] + bg_ref[...])
        dmv = dm_ref[...]
        dgl = dmv * y_ref[...] * (g * (1.0 - g))
        dgl_ref[...] = dgl.astype(dgl_ref.dtype)
        dy_ref[...] = (dmv * g).astype(dy_ref.dtype)

        @pl.when(pl.program_id(2) == 0)
        def _():
            dbg_ref[...] = jnp.zeros_like(dbg_ref)

        dbg_ref[...] += jnp.sum(dgl, axis=0, keepdims=True)

    half = pl.BlockSpec((tm, wb), lambda h, j, i: (i, h * nw + j))
    vec = pl.BlockSpec((1, wb), lambda h, j, i: (0, h * nw + j))
    return pl.pallas_call(
        body,
        out_shape=(_sds((t, width), BF16), _sds((t, d2), BF16), _sds((1, d2), F32)),
        grid=(2, nw, t // tm),
        in_specs=[pl.BlockSpec((tm, wb), lambda h, j, i: (i, j)), half, half, vec],
        out_specs=(pl.BlockSpec((tm, wb), lambda h, j, i: (i, gl_off // wb + h * nw + j)), half, vec),
        compiler_params=pltpu.CompilerParams(dimension_semantics=("parallel", "parallel", "arbitrary")),
        name=name,
    )(dm, ycat, gl, bg)


def _log_sigmoid(z):
    return jnp.minimum(z, 0.0) - jnp.log1p(jnp.exp(-jnp.abs(z)))


def _forget_fwd(name, fl, bf, batch, seq):
    def body(fl_ref, bf_ref, o_ref):
        lf = _log_sigmoid(fl_ref[:, 0:LANES] + bf_ref[:, 0:LANES])
        acc = lf.T[0:HEADS, :]
        lane = lax.broadcasted_iota(jnp.int32, acc.shape, 1)
        k = 1
        while k < seq:
            acc = acc + jnp.where(lane >= k, pltpu.roll(acc, k, axis=1), 0.0)
            k *= 2
        o_ref[...] = acc

    return pl.pallas_call(
        body,
        out_shape=_sds((batch, HEADS, seq), F32),
        grid=(batch,),
        in_specs=[pl.BlockSpec((seq, F_PAD), lambda b: (b, 0)), pl.BlockSpec((1, F_PAD), lambda b: (0, 0))],
        out_specs=pl.BlockSpec((None, HEADS, seq), lambda b: (b, 0, 0)),
        compiler_params=pltpu.CompilerParams(dimension_semantics=("parallel",)),
        name=name,
    )(fl, bf)


def _forget_bwd(name, d_key, d_query, fl, bf, dproj, f_off, batch, seq):
    nfb = F_PAD // LANES

    def body(dk_ref, dq_ref, fl_ref, bf_ref, _, df_ref, dbf_ref):
        jj = pl.program_id(1)
        key_t = jnp.concatenate([dk_ref[...], jnp.zeros((LANES - HEADS, seq), F32)], axis=0).T
        acc = dq_ref[...] - key_t
        row = lax.broadcasted_iota(jnp.int32, acc.shape, 0)
        k = 1
        while k < seq:
            acc = acc + jnp.where(row < seq - k, pltpu.roll(acc, seq - k, axis=0), 0.0)
            k *= 2
        z = fl_ref[:, 0:LANES] + bf_ref[:, 0:LANES]
        col = lax.broadcasted_iota(jnp.int32, acc.shape, 1)
        df = jnp.where(col < HEADS, acc * jax.nn.sigmoid(-z), 0.0)
        df = jnp.where(jj == 0, df, 0.0)
        df_ref[...] = df.astype(df_ref.dtype)

        @pl.when((pl.program_id(0) == 0) & (jj == 0))
        def _():
            dbf_ref[...] = jnp.zeros_like(dbf_ref)

        dbf_ref[...] += jnp.sum(df, axis=0, keepdims=True)

    return pl.pallas_call(
        body,
        out_shape=(_sds(dproj.shape, dproj.dtype), _sds((1, LANES), F32)),
        grid=(batch, nfb),
        in_specs=[
            pl.BlockSpec((None, HEADS, seq), lambda b, j: (b, 0, 0)),
            pl.BlockSpec((seq, LANES), lambda b, j: (b, 0)),
            pl.BlockSpec((seq, F_PAD), lambda b, j: (b, 0)),
            pl.BlockSpec((1, F_PAD), lambda b, j: (0, 0)),
            pl.BlockSpec(memory_space=pl.ANY),
        ],
        out_specs=(pl.BlockSpec((seq, LANES), lambda b, j: (b, f_off // LANES + j)),
                   pl.BlockSpec((1, LANES), lambda b, j: (0, 0))),
        input_output_aliases={4: 0},
        compiler_params=pltpu.CompilerParams(dimension_semantics=("arbitrary", "arbitrary")),
        name=name,
    )(d_key, d_query, fl, bf, dproj)


def _dot(a, b, mode):
    return lax.dot_general(a, b, _DIMS[mode], preferred_element_type=F32)


def _attn_fwd(name, qkv, frow, batch, seq, tq):
    nq = seq // tq
    scale = 1.0 / math.sqrt(HEAD_DIM)

    def body(q_ref, k_ref, v_ref, f_ref, o_ref, lse_ref):
        i = pl.program_id(2)
        lane = lax.broadcasted_iota(jnp.int32, (1, LANES), 1)
        lo = lane < HEAD_DIM
        qs = q_ref[...] * scale
        qh = (jnp.where(lo, qs, 0.0).astype(BF16), jnp.where(lo, 0.0, qs).astype(BF16))
        row = lax.broadcasted_iota(jnp.int32, (tq, tq), 0)
        col = lax.broadcasted_iota(jnp.int32, (tq, tq), 1)

        def step(j, carry, diag):
            m0, l0, m1, l1, acc = carry
            start = pl.multiple_of(j * tq, tq)
            kj = k_ref[pl.ds(start, tq), :]
            vj = v_ref[pl.ds(start, tq), :]
            ms, ls, pvs, alphas = [], [], [], []
            for h, (m_old, l_old) in enumerate(((m0, l0), (m1, l1))):
                s = _dot(qh[h], kj, "nt") - f_ref[h:h + 1, pl.ds(start, tq)]
                if diag:
                    s = jnp.where(col <= row, s, NEG_BIG)
                m_new = jnp.maximum(m_old, jnp.max(s, axis=1, keepdims=True))
                p = jnp.exp(s - m_new)
                alpha = jnp.exp(m_old - m_new)
                ls.append(alpha * l_old + jnp.sum(p, axis=1, keepdims=True))
                ms.append(m_new)
                alphas.append(alpha)
                vh = jnp.where(lo, vj, 0.0) if h == 0 else jnp.where(lo, 0.0, vj)
                pvs.append(_dot(p.astype(BF16), vh.astype(BF16), "nn"))
            acc = acc * jnp.where(lo, alphas[0], alphas[1]) + (pvs[0] + pvs[1])
            return ms[0], ls[0], ms[1], ls[1], acc

        neg = jnp.full((tq, 1), NEG_BIG, F32)
        zero = jnp.zeros((tq, 1), F32)
        init = (neg, zero, neg, zero, jnp.zeros((tq, LANES), F32))
        carry = lax.fori_loop(0, i, lambda j, c: step(j, c, False), init)
        m0, l0, m1, l1, acc = step(i, carry, True)
        o_ref[...] = (acc / jnp.where(lo, l0, l1)).astype(o_ref.dtype)
        lse_ref[:, 0:1] = m0 + jnp.log(l0)
        lse_ref[:, 1:2] = m1 + jnp.log(l1)

    return pl.pallas_call(
        body,
        out_shape=(_sds((batch * seq, ATTN_WIDTH), BF16), _sds((HEAD_PAIRS, batch * seq, 2), F32)),
        grid=(batch, HEAD_PAIRS, nq),
        in_specs=[
            pl.BlockSpec((tq, LANES), lambda b, hp, i: (b * nq + i, 3 * hp)),
            pl.BlockSpec((seq, LANES), lambda b, hp, i: (b, 3 * hp + 1)),
            pl.BlockSpec((seq, LANES), lambda b, hp, i: (b, 3 * hp + 2)),
            pl.BlockSpec((None, None, 2, seq), lambda b, hp, i: (b, hp, 0, 0)),
        ],
        out_specs=(
            pl.BlockSpec((tq, LANES), lambda b, hp, i: (b * nq + i, hp)),
            pl.BlockSpec((None, tq, 2), lambda b, hp, i: (hp, b * nq + i, 0)),
        ),
        compiler_params=pltpu.CompilerParams(dimension_semantics=("parallel", "parallel", "parallel")),
        name=name,
    )(qkv, qkv, qkv, frow)


def _attn_bwd(name, qkv, do, o, lse, frow, dproj, qkv_off, batch, seq, tq):
    nq = seq // tq
    scale = 1.0 / math.sqrt(HEAD_DIM)

    def body(q_ref, k_ref, v_ref, do_ref, o_ref, lse_ref, f_ref, _, dqkv_ref, df_ref, drow_ref,
             dq_acc, dk_acc, dv_acc, df_acc):
        j = pl.program_id(2)
        lane = lax.broadcasted_iota(jnp.int32, (1, LANES), 1)
        lo = lane < HEAD_DIM
        masks = (lo, jnp.logical_not(lo))
        row = lax.broadcasted_iota(jnp.int32, (tq, tq), 0)
        col = lax.broadcasted_iota(jnp.int32, (tq, tq), 1)

        @pl.when(j == 0)
        def _():
            dq_acc[...] = jnp.zeros_like(dq_acc)
            drow_ref[...] = jnp.zeros_like(drow_ref)

        dk_acc[...] = jnp.zeros_like(dk_acc)
        dv_acc[...] = jnp.zeros_like(dv_acc)
        df_acc[...] = jnp.zeros_like(df_acc)
        kj = k_ref[...]
        vj = v_ref[...]
        kstart = pl.multiple_of(j * tq, tq)
        kh = tuple(jnp.where(mk, kj, 0.0).astype(BF16) for mk in masks)

        def step(i, diag):
            start = pl.multiple_of(i * tq, tq)
            rows = pl.ds(start, tq)
            qi = q_ref[rows, :] * scale
            doi = do_ref[rows, :]
            prod = doi.astype(F32) * o_ref[rows, :].astype(F32)
            lse_i = lse_ref[rows, :]
            dq_i = jnp.zeros((tq, LANES), F32)
            for h, mk in enumerate(masks):
                q_h = jnp.where(mk, qi, 0.0).astype(BF16)
                do_h = jnp.where(mk, doi, 0.0).astype(BF16)
                delta = jnp.sum(jnp.where(mk, prod, 0.0), axis=1, keepdims=True)
                s = _dot(q_h, kj, "nt") - f_ref[h:h + 1, pl.ds(kstart, tq)]
                p = jnp.exp(s - lse_i[:, h:h + 1])
                if diag:
                    p = jnp.where(col <= row, p, 0.0)
                ds = p * (_dot(do_h, vj, "nt") - delta)
                df_acc[h:h + 1, :] += jnp.sum(ds, axis=0, keepdims=True)
                drow_ref[rows, h:h + 1] += jnp.sum(ds, axis=1, keepdims=True)
                dsb = ds.astype(BF16)
                dv_acc[...] += _dot(p.astype(BF16), do_h, "tn")
                dk_acc[...] += _dot(dsb, q_h, "tn")
                dq_i = dq_i + _dot(dsb, kh[h], "nn")
            dq_acc[rows, :] += dq_i

        step(j, True)
        lax.fori_loop(j + 1, nq, lambda i, c: (step(i, False), c)[1], 0)
        dqkv_ref[:, 0:LANES] = (dq_acc[pl.ds(kstart, tq), :] * scale).astype(dqkv_ref.dtype)
        dqkv_ref[:, LANES:2 * LANES] = dk_acc[...].astype(dqkv_ref.dtype)
        dqkv_ref[:, 2 * LANES:3 * LANES] = dv_acc[...].astype(dqkv_ref.dtype)
        df_ref[...] = df_acc[...]

    full = lambda c: pl.BlockSpec((seq, LANES), lambda b, hp, j: (b, c(hp)))
    blk = lambda c: pl.BlockSpec((tq, LANES), lambda b, hp, j: (b * nq + j, c(hp)))
    return pl.pallas_call(
        body,
        out_shape=(_sds(dproj.shape, dproj.dtype), _sds((batch, HEAD_PAIRS, 2, seq), F32),
                   _sds((HEAD_PAIRS, batch * seq, 2), F32)),
        grid=(batch, HEAD_PAIRS, nq),
        in_specs=[
            full(lambda hp: 3 * hp),
            blk(lambda hp: 3 * hp + 1),
            blk(lambda hp: 3 * hp + 2),
            full(lambda hp: hp),
            full(lambda hp: hp),
            pl.BlockSpec((None, seq, 2), lambda b, hp, j: (hp, b, 0)),
            pl.BlockSpec((None, None, 2, seq), lambda b, hp, j: (b, hp, 0, 0)),
            pl.BlockSpec(memory_space=pl.ANY),
        ],
        out_specs=(
            pl.BlockSpec((tq, 3 * LANES), lambda b, hp, j: (b * nq + j, qkv_off // (3 * LANES) + hp)),
            pl.BlockSpec((None, None, 2, tq), lambda b, hp, j: (b, hp, 0, j)),
            pl.BlockSpec((None, seq, 2), lambda b, hp, j: (hp, b, 0)),
        ),
        scratch_shapes=[
            pltpu.VMEM((seq, LANES), F32),
            pltpu.VMEM((tq, LANES), F32),
            pltpu.VMEM((tq, LANES), F32),
            pltpu.VMEM((2, tq), F32),
        ],
        input_output_aliases={7: 0},
        compiler_params=pltpu.CompilerParams(dimension_semantics=("parallel", "parallel", "arbitrary")),
        name=name,
    )(qkv, qkv, qkv, do, o, lse, frow, dproj)


def _mesh_place():
    x, y, c = lax.axis_index("x"), lax.axis_index("y"), lax.axis_index("c")
    chips = [(1 - x, y), (x, 1 - y), (1 - x, 1 - y)]
    return x, y, c, chips


def _hbm_specs(n):
    return [pl.BlockSpec(memory_space=pl.ANY)] * n


def _gather_weights(bigs, smalls):
    nb, ns = len(bigs), len(smalls)
    arrays = list(bigs) + list(smalls)
    n = nb + ns

    def body(*refs):
        ins, outs = refs[:n], refs[n:2 * n]
        send_sems, recv_sems, local_sems = refs[2 * n:]
        x, y, c, chips = _mesh_place()
        me = 2 * x + y
        sibling = (x, y, 1 - c)

        def half(a, which):
            r2 = arrays[a].shape[0] // 2
            return pl.ds(pl.multiple_of(which * r2, 16), r2)

        def copy(a, k, src, dst, to):
            return pltpu.make_async_remote_copy(src_ref=src, dst_ref=dst, send_sem=send_sems.at[a, k],
                                                recv_sem=recv_sems.at[a, k], device_id=to, device_id_type=MESH)

        local = [pltpu.make_async_copy(ins[a], outs[a].at[me], local_sems.at[a]) for a in range(n)]
        for cp in local:
            cp.start()
        sends = []
        for a in range(n):
            for j, chip in enumerate(chips):
                if a < nb:
                    cp = copy(a, j, ins[a].at[half(a, c)], outs[a].at[me, half(a, c)], (*chip, c))
                else:
                    cp = copy(a, j, ins[a], outs[a].at[me], (*chip, c))
                cp.start()
                sends.append(cp)
        for a in range(nb):
            for j, (px, py) in enumerate(chips):
                blk = outs[a].at[2 * px + py, half(a, c)]
                copy(a, j, blk, blk, (px, py, c)).wait_recv()
                fwd = copy(a, 3 + j, blk, blk, sibling)
                fwd.start()
                sends.append(fwd)
        for a in range(nb, n):
            for j, (px, py) in enumerate(chips):
                blk = outs[a].at[2 * px + py]
                copy(a, j, blk, blk, (px, py, c)).wait_recv()
        for a in range(nb):
            for j, (px, py) in enumerate(chips):
                blk = outs[a].at[2 * px + py, half(a, 1 - c)]
                copy(a, 3 + j, blk, blk, sibling).wait_recv()
        for cp in sends:
            cp.wait_send()
        for cp in local:
            cp.wait()

    return pl.pallas_call(
        body,
        out_shape=tuple(_sds((N_CHIPS,) + a.shape, a.dtype) for a in arrays),
        in_specs=_hbm_specs(n),
        out_specs=tuple(_hbm_specs(n)),
        scratch_shapes=[pltpu.SemaphoreType.DMA((n, 6)), pltpu.SemaphoreType.DMA((n, 6)), pltpu.SemaphoreType.DMA((n,))],
        name="gather_weights",
    )(*arrays)


def _gather_small(v):
    m_per, ncol = v.shape

    def body(x_ref, out_ref, send_sems, recv_sems, local_sem):
        x, y, c, chips = _mesh_place()
        me, sibling = (x, y, c), (x, y, 1 - c)

        def rows(px, py, pc):
            return out_ref.at[pl.ds((4 * px + 2 * py + pc) * m_per, m_per), :]

        def copy(k, block, to, src=None):
            return pltpu.make_async_remote_copy(src_ref=rows(*block) if src is None else src, dst_ref=rows(*block),
                                                send_sem=send_sems.at[k], recv_sem=recv_sems.at[k],
                                                device_id=to, device_id_type=MESH)

        mine = pltpu.make_async_copy(x_ref, rows(*me), local_sem)
        mine.start()
        first = [copy(0, me, sibling, src=x_ref)]
        first += [copy(1 + j, me, (*chip, c), src=x_ref) for j, chip in enumerate(chips)]
        for cp in first:
            cp.start()
        passed = [copy(4 + j, (*chip, c), sibling) for j, chip in enumerate(chips)]
        for j, chip in enumerate(chips):
            copy(1 + j, (*chip, c), me).wait_recv()
            passed[j].start()
        copy(0, sibling, me).wait_recv()
        for j, chip in enumerate(chips):
            copy(4 + j, (*chip, 1 - c), me).wait_recv()
        for cp in first + passed:
            cp.wait_send()
        mine.wait()

    return pl.pallas_call(
        body,
        out_shape=_sds((N_DEV * m_per, ncol), v.dtype),
        in_specs=[pl.BlockSpec(memory_space=pltpu.VMEM)],
        out_specs=pl.BlockSpec(memory_space=pltpu.VMEM),
        scratch_shapes=[pltpu.SemaphoreType.DMA((7,)), pltpu.SemaphoreType.DMA((7,)), pltpu.SemaphoreType.DMA],
        name="gather_small",
    )(v)


def _exchange_sibling(grads):
    n = len(grads)

    def body(*refs):
        ins, outs = refs[:n], refs[n:2 * n]
        send_sems, recv_sems = refs[2 * n:]
        x, y, c, _ = _mesh_place()
        copies = []
        for a in range(n):
            r2 = grads[a].shape[1] // 2
            src = ins[a].at[:, pl.ds(pl.multiple_of((1 - c) * r2, 8), r2), :]
            cp = pltpu.make_async_remote_copy(src_ref=src, dst_ref=outs[a], send_sem=send_sems.at[a],
                                              recv_sem=recv_sems.at[a], device_id=(x, y, 1 - c), device_id_type=MESH)
            cp.start()
            copies.append(cp)
        for cp in copies:
            cp.wait()

    return pl.pallas_call(
        body,
        out_shape=tuple(_sds((N_CHIPS, g.shape[1] // 2, g.shape[2]), g.dtype) for g in grads),
        in_specs=_hbm_specs(n),
        out_specs=tuple(_hbm_specs(n)),
        scratch_shapes=[pltpu.SemaphoreType.DMA((n,)), pltpu.SemaphoreType.DMA((n,))],
        name="exchange_sibling",
    )(*grads)


def _exchange_chips(sums):
    n = len(sums)

    def body(*refs):
        ins, outs = refs[:n], refs[n:2 * n]
        send_sems, recv_sems = refs[2 * n:]
        _, _, c, chips = _mesh_place()
        copies = []
        for a in range(n):
            for j, (px, py) in enumerate(chips):
                cp = pltpu.make_async_remote_copy(src_ref=ins[a].at[2 * px + py], dst_ref=outs[a].at[j],
                                                  send_sem=send_sems.at[a, j], recv_sem=recv_sems.at[a, j],
                                                  device_id=(px, py, c), device_id_type=MESH)
                cp.start()
                copies.append(cp)
        for cp in copies:
            cp.wait()

    return pl.pallas_call(
        body,
        out_shape=tuple(_sds((3,) + s.shape[1:], s.dtype) for s in sums),
        in_specs=_hbm_specs(n),
        out_specs=tuple(_hbm_specs(n)),
        scratch_shapes=[pltpu.SemaphoreType.DMA((n, 3)), pltpu.SemaphoreType.DMA((n, 3))],
        name="exchange_chips",
    )(*sums)


def _share_sibling(halves):
    n = len(halves)

    def body(*refs):
        ins, outs = refs[:n], refs[n:2 * n]
        send_sems, recv_sems, local_sems = refs[2 * n:]
        x, y, c, _ = _mesh_place()
        started = []
        for a in range(n):
            r2 = halves[a].shape[0]
            mine = outs[a].at[pl.ds(pl.multiple_of(c * r2, 8), r2), :]
            theirs = outs[a].at[pl.ds(pl.multiple_of((1 - c) * r2, 8), r2), :]
            loc = pltpu.make_async_copy(ins[a], mine, local_sems.at[a])
            loc.start()
            cp = pltpu.make_async_remote_copy(src_ref=ins[a], dst_ref=mine, send_sem=send_sems.at[a],
                                              recv_sem=recv_sems.at[a], device_id=(x, y, 1 - c), device_id_type=MESH)
            cp.start()
            arrival = pltpu.make_async_remote_copy(src_ref=ins[a], dst_ref=theirs, send_sem=send_sems.at[a],
                                                   recv_sem=recv_sems.at[a], device_id=(x, y, 1 - c), device_id_type=MESH)
            started.append((loc, cp, arrival))
        for loc, cp, arrival in started:
            arrival.wait_recv()
            cp.wait_send()
            loc.wait()

    return pl.pallas_call(
        body,
        out_shape=tuple(_sds((2 * h.shape[0], h.shape[1]), h.dtype) for h in halves),
        in_specs=_hbm_specs(n),
        out_specs=tuple(_hbm_specs(n)),
        scratch_shapes=[pltpu.SemaphoreType.DMA((n,)), pltpu.SemaphoreType.DMA((n,)), pltpu.SemaphoreType.DMA((n,))],
        name="share_sibling",
    )(*halves)


def _pair_sum(name, place, g, got):
    _, r, cdim = g.shape
    r2 = r // 2

    def body(place_ref, g_ref, got_ref, o_ref):
        o_ref[...] = (g_ref[...] + got_ref[...]).astype(o_ref.dtype)

    blk = (None, r2, cdim)
    return pl.pallas_call(
        body,
        out_shape=_sds((N_CHIPS, r2, cdim), BF16),
        grid_spec=pltpu.PrefetchScalarGridSpec(
            num_scalar_prefetch=1,
            grid=(N_CHIPS,),
            in_specs=[pl.BlockSpec(blk, lambda k, pr: (k, pr[1], 0)), pl.BlockSpec(blk, lambda k, pr: (k, 0, 0))],
            out_specs=pl.BlockSpec(blk, lambda k, pr: (k, 0, 0)),
        ),
        compiler_params=pltpu.CompilerParams(dimension_semantics=("parallel",)),
        name=name,
    )(place, g, got)


def _chip_sum(name, place, g, got, arrivals):
    _, r, cdim = g.shape
    r2 = r // 2

    def body(place_ref, g_ref, got_ref, arr_ref, o_ref):
        acc = g_ref[...] + got_ref[...]
        for j in range(3):
            acc = acc + arr_ref[j].astype(F32)
        o_ref[...] = acc

    blk = (None, r2, cdim)
    return pl.pallas_call(
        body,
        out_shape=_sds((r2, cdim), F32),
        grid_spec=pltpu.PrefetchScalarGridSpec(
            num_scalar_prefetch=1,
            grid=(1,),
            in_specs=[
                pl.BlockSpec(blk, lambda i, pr: (pr[0], pr[1], 0)),
                pl.BlockSpec(blk, lambda i, pr: (pr[0], 0, 0)),
                pl.BlockSpec((3, r2, cdim), lambda i, pr: (0, 0, 0)),
            ],
            out_specs=pl.BlockSpec((r2, cdim), lambda i, pr: (0, 0)),
        ),
        compiler_params=pltpu.CompilerParams(dimension_semantics=("arbitrary",)),
        name=name,
    )(place, g, got, arrivals)


def _device_sum(name, gathered):
    m_per = gathered.shape[0] // N_DEV

    def body(g_ref, o_ref):
        acc = g_ref[0:m_per, :]
        for dev in range(1, N_DEV):
            acc = acc + g_ref[dev * m_per:(dev + 1) * m_per, :]
        o_ref[...] = acc

    return pl.pallas_call(body, out_shape=_sds((m_per, gathered.shape[1]), F32), name=name)(gathered)


def _adamw(name, w, g, m, v):
    r, cdim = w.shape
    tr = _tile(r, 256, 8) if r % 8 == 0 else r
    bc1 = 1.0 - ADAM_B1 ** ADAM_STEP
    bc2 = 1.0 - ADAM_B2 ** ADAM_STEP

    def body(w_ref, g_ref, m_ref, v_ref, d_ref, nm_ref, nv_ref):
        gv = g_ref[...]
        nm = ADAM_B1 * m_ref[...] + (1.0 - ADAM_B1) * gv
        nv = ADAM_B2 * v_ref[...] + (1.0 - ADAM_B2) * (gv * gv)
        d_ref[...] = -ADAM_LR * ((nm / bc1) / (jnp.sqrt(nv / bc2) + ADAM_EPS) + ADAM_WD * w_ref[...])
        nm_ref[...] = nm
        nv_ref[...] = nv

    blk = pl.BlockSpec((tr, cdim), lambda i: (i, 0))
    shape = _sds((r, cdim), F32)
    return pl.pallas_call(
        body,
        out_shape=(shape, shape, shape),
        grid=(r // tr,),
        in_specs=[blk] * 4,
        out_specs=(blk, blk, blk),
        compiler_params=pltpu.CompilerParams(dimension_semantics=("parallel",)),
        name=name,
    )(w, g, m, v)


def _cat_cols(g):
    return jnp.transpose(g, (1, 0, 2)).reshape(g.shape[1], N_CHIPS * g.shape[2])


def _split_cols(a):
    r, c4 = a.shape
    return jnp.transpose(a.reshape(r, N_CHIPS, c4 // N_CHIPS), (1, 0, 2))


def _local_step(x, target, w_in, w_oc, w_oa, w_o, w_up, w_down, cmw, cfw, g1, b_f, b_gate, g2, gf):
    batch, seq, d = x.shape
    t = batch * seq
    cw = d // 2
    fh = w_down.shape[0]
    tc = LANES
    nct = cw // tc
    tq = min(256, seq)
    pc_w, qkv_w, gl_w = 3 * cw, 3 * ATTN_WIDTH, 2 * d
    qkv_off, gl_off, f_off = pc_w, pc_w + qkv_w, pc_w + qkv_w + gl_w
    width = f_off + F_PAD
    f_col = pc_w + qkv_w

    w_pc = w_in[:, :pc_w].reshape(d, 3, nct, tc).transpose(0, 2, 1, 3).reshape(d, pc_w)
    w_qkv = w_in[:, pc_w:f_col].reshape(d, 3, HEAD_PAIRS, LANES).transpose(0, 2, 1, 3).reshape(d, qkv_w)
    w_f = jnp.pad(w_in[:, f_col:f_col + HEADS], ((0, 0), (0, F_PAD - HEADS)))
    w_inp = jnp.concatenate([w_pc, w_qkv, w_in[:, f_col + HEADS:], w_f], axis=1)
    bf_pad = jnp.pad(b_f, ((0, 0), (0, F_PAD - HEADS)))

    x2d = x.reshape(t, d)
    tgt2d = target.reshape(t, d)

    h1 = _rms_fwd("norm_mix", x2d, g1)
    pc = _mm("proj_conv", h1, w_inp, "nn", F32, m=t, n=pc_w, k=d, b_off=0)
    qkv = _mm("proj_qkv", h1, w_inp, "nn", BF16, m=t, n=qkv_w, k=d, b_off=qkv_off)
    gl = _mm("proj_gate", h1, w_inp, "nn", F32, m=t, n=gl_w, k=d, b_off=gl_off)
    fl = _mm("proj_forget", h1, w_inp, "nn", F32, m=t, n=F_PAD, k=d, b_off=f_off)
    a_c = _conv_fwd("conv_mix", pc, cmw, batch, seq, tc)
    f_cum = _forget_fwd("forget_cumsum", fl, bf_pad, batch, seq)
    frow = f_cum.reshape(batch, HEAD_PAIRS, 2, seq)
    o, lse = _attn_fwd("attn_fwd", qkv, frow, batch, seq, tq)
    ycat = _mm("out_conv", a_c, w_oc, "nn", F32, m=t, n=d, k=cw, o_off=0, o_width=2 * d)
    ycat = _mm("out_attn", o, w_oa, "nn", F32, m=t, n=d, k=ATTN_WIDTH, out=ycat, o_off=d)
    mg = _merge_fwd("gate_merge", ycat, gl, b_gate)
    x2 = _mm("mix_out", mg, w_o, "nn", F32, m=t, n=d, k=d, add=x2d)
    h2 = _rms_fwd("norm_ffn", x2, g2)
    upre = _mm("ffn_up", h2, w_up, "nn", F32, m=t, n=2 * fh, k=d, tn=1408)
    hmid = _ffn_fwd("ffn_act", upre, cfw, batch, seq, tc)
    x3 = _mm("ffn_down", hmid, w_down, "nn", F32, m=t, n=d, k=fh, add=x2, tk=4096)

    dx3, loss_row, d_gf = _final_loss("final_loss", x3, gf.reshape(1, d), tgt2d)
    d_hmid = _mm("d_ffn_act", dx3, w_down, "nt", F32, m=t, n=fh, k=d, tm=512, tn=4096)
    dw_down = _mm("dw_down", hmid, dx3, "tn", F32, m=fh, n=d, k=t, tm=1408, tk=512)
    du_a, du_b, d_cfw_a, d_cfw_b = _ffn_bwd("d_ffn_conv", d_hmid, upre, cfw, batch, seq, tc)
    dw_up_a = _mm("dw_up_a", h2, du_a, "tn", F32, m=d, n=fh, k=t, tn=1408, tk=512)
    dw_up_b = _mm("dw_up_b", h2, du_b, "tn", F32, m=d, n=fh, k=t, tn=1408, tk=512)
    dh2 = _mm("d_norm_ffn_a", du_a, w_up, "nt", F32, m=t, n=d, k=fh, b_off=0, tk=1408)
    dh2 = _mm("d_norm_ffn_b", du_b, w_up, "nt", F32, m=t, n=d, k=fh, b_off=fh, add=dh2, tk=1408)
    dx2, d_g2 = _rms_bwd("d_norm_ffn", x2, dh2, g2, dx3)
    dm = _mm("d_merge", dx2, w_o, "nt", F32, m=t, n=d, k=d)
    dw_o = _mm("dw_o", mg, dx2, "tn", F32, m=d, n=d, k=t, tk=512)
    dproj, dycat, d_bg = _merge_bwd("d_gate_merge", dm, ycat, gl, b_gate, width, gl_off)
    da_c = _mm("d_conv_out", dycat, w_oc, "nt", F32, m=t, n=cw, k=d, a_off=0)
    do = _mm("d_attn_out", dycat, w_oa, "nt", BF16, m=t, n=ATTN_WIDTH, k=d, a_off=d)
    dw_oc = _mm("dw_out_conv", a_c, dycat, "tn", F32, m=cw, n=d, k=t, b_off=0, tk=512)
    dw_oa = _mm("dw_out_attn", o, dycat, "tn", F32, m=ATTN_WIDTH, n=d, k=t, b_off=d, tk=512)
    dproj, d_cmw = _conv_bwd("d_conv_mix", da_c, pc, cmw, dproj, batch, seq, tc)
    dproj, d_fkey, d_fquery = _attn_bwd("attn_bwd", qkv, do, o, lse, frow, dproj, qkv_off, batch, seq, tq)
    d_fquery = jnp.pad(jnp.transpose(d_fquery, (1, 0, 2)).reshape(t, HEADS), ((0, 0), (0, LANES - HEADS)))
    dproj, d_bf = _forget_bwd("d_forget", d_fkey.reshape(batch, HEADS, seq), d_fquery, fl, bf_pad, dproj, f_off,
                              batch, seq)
    dw_inp = _mm("dw_in", h1, dproj, "tn", F32, m=d, n=width, k=t, tn=1792, tk=512)
    dh1 = _mm("d_norm_mix", dproj, w_inp, "nt", F32, m=t, n=d, k=width, tk=1792)
    grad_x, d_g1 = _rms_bwd("d_norm_mix_x", x2d, dh1, g1, dx2)

    d_pc = dw_inp[:, :pc_w].reshape(d, nct, 3, tc).transpose(0, 2, 1, 3).reshape(d, pc_w)
    d_qkv = dw_inp[:, qkv_off:gl_off].reshape(d, HEAD_PAIRS, 3, LANES).transpose(0, 2, 1, 3).reshape(d, qkv_w)
    dw_in = jnp.concatenate([d_pc, d_qkv, dw_inp[:, f_off:f_off + HEADS], dw_inp[:, gl_off:f_off]], axis=1)
    dw_up = jnp.concatenate([dw_up_a, dw_up_b], axis=1)
    d_cfw = jnp.concatenate([d_cfw_a, d_cfw_b], axis=1)
    mats = (dw_in, dw_oc, dw_oa, dw_o, dw_up, dw_down)
    smalls = (d_g1, d_g2, d_gf, d_bg, d_bf, d_cmw, d_cfw)
    return loss_row[0, 0], grad_x.reshape(batch, seq, d), mats, smalls


def _pack_small(parts):
    flat = [p.reshape(-1) for p in parts]
    sizes = [f.shape[0] for f in flat]
    total = sum(sizes)
    padded = -(-total // (8 * LANES)) * (8 * LANES)
    vec = jnp.concatenate(flat + [jnp.zeros((padded - total,), F32)])
    offsets = [sum(sizes[:i]) for i in range(len(sizes))]
    return vec.reshape(padded // LANES, LANES), offsets


def kernel(x, norm_mix_g, w_in, b_f, b_gate, conv_mix_w, w_out_conv, w_out_attn, w_o, norm_ffn_g, w_up, conv_ffn_w, w_down, norm_f_g, loss_target, m_norm_mix_g, m_w_in, m_b_f, m_b_gate, m_conv_mix_w, m_w_out_conv, m_w_out_attn, m_w_o, m_norm_ffn_g, m_w_up, m_conv_ffn_w, m_w_down, m_norm_f_g, v_norm_mix_g, v_w_in, v_b_f, v_b_gate, v_conv_mix_w, v_w_out_conv, v_w_out_attn, v_w_o, v_norm_ffn_g, v_w_up, v_conv_ffn_w, v_w_down, v_norm_f_g):
    d = x.shape[-1]
    chip = 2 * lax.axis_index("x") + lax.axis_index("y")
    place = jnp.stack([chip, lax.axis_index("c")]).astype(jnp.int32)

    bigs = [w[0].astype(BF16) for w in (w_in, w_out_conv, w_out_attn, w_o, w_up, w_down)]
    gathered = _gather_weights(bigs, [conv_mix_w[0], conv_ffn_w[0]])
    a_in, a_oc, a_oa, a_o, a_up, a_down, a_cmw, a_cfw = gathered
    full_o = a_o.reshape(N_CHIPS * a_o.shape[1], a_o.shape[2])
    full_down = a_down.reshape(N_CHIPS * a_down.shape[1], a_down.shape[2])

    loss_local, grad_x, mats, smalls = _local_step(
        x, loss_target, _cat_cols(a_in), _cat_cols(a_oc), _cat_cols(a_oa), full_o, _cat_cols(a_up), full_down,
        _cat_cols(a_cmw), _cat_cols(a_cfw), norm_mix_g, b_f, b_gate, norm_ffn_g, norm_f_g)
    dw_in, dw_oc, dw_oa, dw_o, dw_up, dw_down = mats

    grads = [
        _split_cols(dw_in), _split_cols(dw_oc), _split_cols(dw_oa),
        dw_o.reshape(N_CHIPS, dw_o.shape[0] // N_CHIPS, dw_o.shape[1]),
        _split_cols(dw_up),
        dw_down.reshape(N_CHIPS, dw_down.shape[0] // N_CHIPS, dw_down.shape[1]),
    ]
    names = ("w_in", "w_out_conv", "w_out_attn", "w_o", "w_up", "w_down")
    got = _exchange_sibling(grads)
    sums = [_pair_sum("pair_sum_" + nm, place, g, r) for nm, g, r in zip(names, grads, got)]
    arrivals = _exchange_chips(sums)
    halves = [_chip_sum("chip_sum_" + nm, place, g, r, arr) for nm, g, r, arr in zip(names, grads, got, arrivals)]
    g_in, g_oc, g_oa, g_o, g_up, g_down = _share_sibling(halves)

    packed, offs = _pack_small(smalls)
    total = _device_sum("device_sum", _gather_small(packed)).reshape(-1)
    shapes = [s.shape for s in smalls]
    d_g1, d_g2, d_gf, d_bg, d_bf, d_cmw, d_cfw = [
        total[o:o + math.prod(sh)].reshape(sh) for o, sh in zip(offs, shapes)]
    d_bf = d_bf[:, :HEADS]
    cw_s, cf_s = conv_mix_w.shape[2], conv_ffn_w.shape[2]
    d_cmw = lax.dynamic_slice(d_cmw, (0, chip * cw_s), (3, cw_s))
    d_cfw = lax.dynamic_slice(d_cfw, (0, chip * cf_s), (3, cf_s))

    loss = lax.psum(loss_local, ("x", "y", "c"))
    order = [
        ("norm_mix_g", norm_mix_g[0:1], d_g1, m_norm_mix_g, v_norm_mix_g),
        ("w_in", w_in[0], g_in, m_w_in[0], v_w_in[0]),
        ("b_f", b_f, d_bf, m_b_f, v_b_f),
        ("b_gate", b_gate, d_bg, m_b_gate, v_b_gate),
        ("conv_mix_w", conv_mix_w[0], d_cmw, m_conv_mix_w[0], v_conv_mix_w[0]),
        ("w_out_conv", w_out_conv[0], g_oc, m_w_out_conv[0], v_w_out_conv[0]),
        ("w_out_attn", w_out_attn[0], g_oa, m_w_out_attn[0], v_w_out_attn[0]),
        ("w_o", w_o[0], g_o, m_w_o[0], v_w_o[0]),
        ("norm_ffn_g", norm_ffn_g, d_g2, m_norm_ffn_g, v_norm_ffn_g),
        ("w_up", w_up[0], g_up, m_w_up[0], v_w_up[0]),
        ("conv_ffn_w", conv_ffn_w[0], d_cfw, m_conv_ffn_w[0], v_conv_ffn_w[0]),
        ("w_down", w_down[0], g_down, m_w_down[0], v_w_down[0]),
        ("norm_f_g", norm_f_g.reshape(1, d), d_gf, m_norm_f_g.reshape(1, d), v_norm_f_g.reshape(1, d)),
    ]
    out_shapes = [norm_mix_g.shape, w_in.shape, b_f.shape, b_gate.shape, conv_mix_w.shape, w_out_conv.shape,
                  w_out_attn.shape, w_o.shape, norm_ffn_g.shape, w_up.shape, conv_ffn_w.shape, w_down.shape,
                  norm_f_g.shape]
    g_out, d_out, m_out, v_out = [], [], [], []
    for (nm, w, g, m, v), sh in zip(order, out_shapes):
        g = g.reshape(w.shape)
        delta, new_m, new_v = _adamw("adamw_" + nm, w, g, m.reshape(w.shape), v.reshape(w.shape))
        g_out.append(g.reshape(sh))
        d_out.append(delta.reshape(sh))
        m_out.append(new_m.reshape(sh))
        v_out.append(new_v.reshape(sh))
    return (loss, grad_x, *g_out, *d_out, *m_out, *v_out)
```

```python
import functools
import math

import jax
import jax.numpy as jnp
from jax import lax
from jax.experimental import pallas as pl
from jax.experimental.pallas import tpu as pltpu

F32 = jnp.float32
BF16 = jnp.bfloat16
MESH = pl.DeviceIdType.MESH

EPS = 1e-6
HEADS = 8
HEAD_DIM = 64
ATTN_WIDTH = HEADS * HEAD_DIM
HEAD_PAIRS = HEADS // 2
LANES = 128
F_PAD = 2 * LANES
NEG_BIG = -1e30
N_CHIPS = 4
N_DEV = 8

ADAM_LR = 0.001
ADAM_B1 = 0.9
ADAM_B2 = 0.999
ADAM_EPS = 1e-08
ADAM_WD = 0.01
ADAM_STEP = 10

_DIMS = {
    "nn": (((1,), (0,)), ((), ())),
    "nt": (((1,), (1,)), ((), ())),
    "tn": (((0,), (0,)), ((), ())),
}


def _tile(n, target, mult, also=()):
    best = None
    for t in range(mult, n + 1, mult):
        if n % t == 0 and t <= target and all(o % t == 0 for o in also):
            best = t
    if best is None:
        assert all(o == 0 for o in also), (n, target, mult, also)
        return n
    return best


def _sds(shape, dtype):
    return jax.ShapeDtypeStruct(shape, dtype)


def _mm(name, a, b, mode, out_dtype, *, m, n, k, a_off=0, b_off=0, b_roff=0, b3=False, out=None, o_off=0,
        o_width=None, o3=None, add=None, tm=1024, tn=2048, tk=2048):
    wb = b.shape[2] if b3 else None
    if mode == "nn":
        tm = _tile(m, tm, 16)
        tk = _tile(k, tk, LANES, (a_off,))
        tn = wb if b3 else _tile(n, tn, LANES, (b_off, o_off))
        a_spec = pl.BlockSpec((tm, tk), lambda i, j, kk: (i, a_off // tk + kk))
        if b3:
            b_spec = pl.BlockSpec((None, tk, tn), lambda i, j, kk: (b_off // tn + j, kk, 0))
        else:
            b_spec = pl.BlockSpec((tk, tn), lambda i, j, kk: (kk, b_off // tn + j))
    elif mode == "nt":
        tm = _tile(m, tm, 16)
        tk = wb if b3 else _tile(k, tk, LANES, (a_off, b_off))
        tn = _tile(n, tn, LANES, (o_off, b_roff))
        a_spec = pl.BlockSpec((tm, tk), lambda i, j, kk: (i, a_off // tk + kk))
        if b3:
            b_spec = pl.BlockSpec((None, tn, tk), lambda i, j, kk: (b_off // tk + kk, b_roff // tn + j, 0))
        else:
            b_spec = pl.BlockSpec((tn, tk), lambda i, j, kk: (b_roff // tn + j, b_off // tk + kk))
    else:
        tm = _tile(m, tm, LANES, (a_off,))
        tk = _tile(k, tk, 16)
        tn = _tile(n, tn, LANES, (b_off, o_off))
        a_spec = pl.BlockSpec((tk, tm), lambda i, j, kk: (kk, a_off // tm + i))
        b_spec = pl.BlockSpec((tk, tn), lambda i, j, kk: (kk, b_off // tn + j))
    assert m % tm == 0 and n % tn == 0 and k % tk == 0, (name, tm, tn, tk)
    nk = k // tk
    if o3 is not None:
        o_spec = pl.BlockSpec((None, tm, tn), lambda i, j, kk: (o_off // tn + j, i, 0))
        out_sds = _sds((o3, m, tn), out_dtype)
    else:
        o_spec = pl.BlockSpec((tm, tn), lambda i, j, kk: (i, o_off // tn + j))
        width = o_width if o_width is not None else (out.shape[1] if out is not None else n)
        out_sds = _sds((m, width), out_dtype)
    use_acc = nk > 1 and out_dtype != F32
    dims = _DIMS[mode]
    has_add, has_out = add is not None, out is not None

    def body(*refs):
        a_ref, b_ref = refs[0], refs[1]
        pos = 2
        add_ref = None
        if has_add:
            add_ref = refs[pos]
            pos += 1
        if has_out:
            pos += 1
        o_ref = refs[pos]
        acc_ref = refs[pos + 1] if use_acc else None
        part = lax.dot_general(a_ref[...].astype(BF16), b_ref[...].astype(BF16), dims,
                               preferred_element_type=F32)
        if nk == 1:
            if has_add:
                part = part + add_ref[...]
            o_ref[...] = part.astype(o_ref.dtype)
            return
        kk = pl.program_id(2)
        tgt = acc_ref if use_acc else o_ref

        @pl.when(kk == 0)
        def _():
            tgt[...] = part + add_ref[...] if has_add else part

        @pl.when(kk > 0)
        def _():
            tgt[...] += part

        if use_acc:
            @pl.when(kk == nk - 1)
            def _():
                o_ref[...] = acc_ref[...].astype(o_ref.dtype)

    operands, in_specs = [a, b], [a_spec, b_spec]
    if has_add:
        operands.append(add)
        in_specs.append(pl.BlockSpec((tm, tn), lambda i, j, kk: (i, j)))
    aliases = {}
    if has_out:
        aliases = {len(operands): 0}
        operands.append(out)
        in_specs.append(pl.BlockSpec(memory_space=pl.ANY))
    return pl.pallas_call(
        body,
        out_shape=out_sds,
        grid=(m // tm, n // tn, nk),
        in_specs=in_specs,
        out_specs=o_spec,
        scratch_shapes=[pltpu.VMEM((tm, tn), F32)] if use_acc else [],
        input_output_aliases=aliases,
        compiler_params=pltpu.CompilerParams(dimension_semantics=("parallel", "parallel", "arbitrary")),
        name=name,
    )(*operands)


def _rms_fwd(name, x, g):
    t, d = x.shape
    tm = _tile(t, 512, 16)

    def body(x_ref, g_ref, o_ref):
        xv = x_ref[...]
        r = lax.rsqrt(jnp.mean(xv * xv, axis=-1, keepdims=True) + EPS)
        o_ref[...] = ((xv * r) * g_ref[...]).astype(o_ref.dtype)

    return pl.pallas_call(
        body,
        out_shape=_sds((t, d), BF16),
        grid=(t // tm,),
        in_specs=[pl.BlockSpec((tm, d), lambda i: (i, 0)), pl.BlockSpec((1, d), lambda i: (0, 0))],
        out_specs=pl.BlockSpec((tm, d), lambda i: (i, 0)),
        compiler_params=pltpu.CompilerParams(dimension_semantics=("parallel",)),
        name=name,
    )(x, g)


def _rms_bwd(name, x, dh, g, res):
    t, d = x.shape
    tm = _tile(t, 512, 16)

    def body(x_ref, dh_ref, g_ref, res_ref, dx_ref, dg_ref):
        xv = x_ref[...]
        r = lax.rsqrt(jnp.mean(xv * xv, axis=-1, keepdims=True) + EPS)
        xh = xv * r
        dhv = dh_ref[...]
        dxh = dhv * g_ref[...]
        dx_ref[...] = res_ref[...] + r * (dxh - xh * jnp.mean(dxh * xh, axis=-1, keepdims=True))

        @pl.when(pl.program_id(0) == 0)
        def _():
            dg_ref[...] = jnp.zeros_like(dg_ref)

        dg_ref[...] += jnp.sum(dhv * xh, axis=0, keepdims=True)

    row = pl.BlockSpec((tm, d), lambda i: (i, 0))
    vec = pl.BlockSpec((1, d), lambda i: (0, 0))
    return pl.pallas_call(
        body,
        out_shape=(_sds((t, d), F32), _sds((1, d), F32)),
        grid=(t // tm,),
        in_specs=[row, row, vec, row],
        out_specs=(row, vec),
        compiler_params=pltpu.CompilerParams(dimension_semantics=("arbitrary",)),
        name=name,
    )(x, dh, g, res)


def _final_loss(name, x, g, target):
    t, d = x.shape
    tm = _tile(t, 512, 16)

    def body(x_ref, g_ref, t_ref, dx_ref, loss_ref, dg_ref):
        xv = x_ref[...]
        gv = g_ref[...]
        r = lax.rsqrt(jnp.mean(xv * xv, axis=-1, keepdims=True) + EPS)
        xh = xv * r
        err = xh * gv - t_ref[...]
        dy = err * (1.0 / d)
        dxh = dy * gv
        dx_ref[...] = r * (dxh - xh * jnp.mean(dxh * xh, axis=-1, keepdims=True))
        per_row = jnp.sum(err * err, axis=-1, keepdims=True) * (0.5 / d)

        @pl.when(pl.program_id(0) == 0)
        def _():
            dg_ref[...] = jnp.zeros_like(dg_ref)
            loss_ref[...] = jnp.zeros_like(loss_ref)

        dg_ref[...] += jnp.sum(dy * xh, axis=0, keepdims=True)
        loss_ref[...] += jnp.sum(per_row, axis=0, keepdims=True)

    row = pl.BlockSpec((tm, d), lambda i: (i, 0))
    vec = pl.BlockSpec((1, d), lambda i: (0, 0))
    return pl.pallas_call(
        body,
        out_shape=(_sds((t, d), F32), _sds((1, LANES), F32), _sds((1, d), F32)),
        grid=(t // tm,),
        in_specs=[row, vec, row],
        out_specs=(row, pl.BlockSpec((1, LANES), lambda i: (0, 0)), vec),
        compiler_params=pltpu.CompilerParams(dimension_semantics=("arbitrary",)),
        name=name,
    )(x, g, target)


def _shift_down(z, k):
    row = lax.broadcasted_iota(jnp.int32, z.shape, 0)
    return jnp.where(row >= k, pltpu.roll(z, k, axis=0), 0.0)


def _shift_up(z, k):
    s = z.shape[0]
    row = lax.broadcasted_iota(jnp.int32, z.shape, 0)
    return jnp.where(row < s - k, pltpu.roll(z, s - k, axis=0), 0.0)


def _conv3(z, w):
    return (w[2:3] * z + w[0:1] * _shift_down(z, 2)) + w[1:2] * _shift_down(z, 1)


def _conv3_t(dz, w):
    return (w[2:3] * dz + w[0:1] * _shift_up(dz, 2)) + w[1:2] * _shift_up(dz, 1)


def _conv_fwd(name, pc, w, batch, seq, tc):
    cw = w.shape[1]
    nct = cw // tc

    def body(pc_ref, w_ref, o_ref):
        cb = pc_ref[:, 0:tc]
        z = pc_ref[:, tc:2 * tc] * pc_ref[:, 2 * tc:3 * tc]
        o_ref[...] = (cb * _conv3(z, w_ref[...])).astype(o_ref.dtype)

    return pl.pallas_call(
        body,
        out_shape=_sds((batch * seq, cw), BF16),
        grid=(batch, nct),
        in_specs=[pl.BlockSpec((seq, 3 * tc), lambda b, j: (b, j)), pl.BlockSpec((3, tc), lambda b, j: (0, j))],
        out_specs=pl.BlockSpec((seq, tc), lambda b, j: (b, j)),
        compiler_params=pltpu.CompilerParams(dimension_semantics=("parallel", "parallel")),
        name=name,
    )(pc, w)


def _conv_bwd(name, da, pc, w, dproj, batch, seq, tc):
    cw = w.shape[1]
    nct = cw // tc

    def body(da_ref, pc_ref, w_ref, _, dpc_ref, dw_ref):
        wv = w_ref[...]
        cb = pc_ref[:, 0:tc]
        cc = pc_ref[:, tc:2 * tc]
        cin = pc_ref[:, 2 * tc:3 * tc]
        z = cc * cin
        dav = da_ref[...]
        du = dav * cb
        dz = _conv3_t(du, wv)
        dpc_ref[:, 0:tc] = (dav * _conv3(z, wv)).astype(dpc_ref.dtype)
        dpc_ref[:, tc:2 * tc] = (dz * cin).astype(dpc_ref.dtype)
        dpc_ref[:, 2 * tc:3 * tc] = (dz * cc).astype(dpc_ref.dtype)

        @pl.when(pl.program_id(1) == 0)
        def _():
            dw_ref[...] = jnp.zeros_like(dw_ref)

        dw_ref[0:1, :] += jnp.sum(du * _shift_down(z, 2), axis=0, keepdims=True)
        dw_ref[1:2, :] += jnp.sum(du * _shift_down(z, 1), axis=0, keepdims=True)
        dw_ref[2:3, :] += jnp.sum(du * z, axis=0, keepdims=True)

    return pl.pallas_call(
        body,
        out_shape=(_sds(dproj.shape, dproj.dtype), _sds((3, cw), F32)),
        grid=(nct, batch),
        in_specs=[
            pl.BlockSpec((seq, tc), lambda j, b: (b, j)),
            pl.BlockSpec((seq, 3 * tc), lambda j, b: (b, j)),
            pl.BlockSpec((3, tc), lambda j, b: (0, j)),
            pl.BlockSpec(memory_space=pl.ANY),
        ],
        out_specs=(pl.BlockSpec((seq, 3 * tc), lambda j, b: (b, j)), pl.BlockSpec((3, tc), lambda j, b: (0, j))),
        input_output_aliases={3: 0},
        compiler_params=pltpu.CompilerParams(dimension_semantics=("parallel", "arbitrary")),
        name=name,
    )(da, pc, w, dproj)


def _ffn_fwd(name, upre, w, batch, seq, tc):
    fh = w.shape[1] // 2
    nf = fh // tc

    def body(ua_ref, ub_ref, wa_ref, wb_ref, o_ref):
        a = _conv3(ua_ref[...], wa_ref[...])
        b = _conv3(ub_ref[...], wb_ref[...])
        o_ref[...] = (a * jax.nn.sigmoid(a) * b).astype(o_ref.dtype)

    return pl.pallas_call(
        body,
        out_shape=_sds((batch * seq, fh), BF16),
        grid=(batch, nf),
        in_specs=[
            pl.BlockSpec((seq, tc), lambda b, j: (b, j)),
            pl.BlockSpec((seq, tc), lambda b, j: (b, nf + j)),
            pl.BlockSpec((3, tc), lambda b, j: (0, j)),
            pl.BlockSpec((3, tc), lambda b, j: (0, nf + j)),
        ],
        out_specs=pl.BlockSpec((seq, tc), lambda b, j: (b, j)),
        compiler_params=pltpu.CompilerParams(dimension_semantics=("parallel", "parallel")),
        name=name,
    )(upre, upre, w, w)


def _ffn_bwd(name, dh, upre, w, batch, seq, tc):
    fh = w.shape[1] // 2
    nf = fh // tc

    def body(dh_ref, ua_ref, ub_ref, wa_ref, wb_ref, dua_ref, dub_ref, dwa_ref, dwb_ref):
        ua, ub, wa, wb = ua_ref[...], ub_ref[...], wa_ref[...], wb_ref[...]
        a = _conv3(ua, wa)
        b = _conv3(ub, wb)
        sg = jax.nn.sigmoid(a)
        dhv = dh_ref[...]
        da = dhv * b * (sg * (1.0 + a * (1.0 - sg)))
        db = dhv * (a * sg)
        dua_ref[...] = _conv3_t(da, wa).astype(dua_ref.dtype)
        dub_ref[...] = _conv3_t(db, wb).astype(dub_ref.dtype)

        @pl.when(pl.program_id(1) == 0)
        def _():
            dwa_ref[...] = jnp.zeros_like(dwa_ref)
            dwb_ref[...] = jnp.zeros_like(dwb_ref)

        for d_ref, dv, uv in ((dwa_ref, da, ua), (dwb_ref, db, ub)):
            d_ref[0:1, :] += jnp.sum(dv * _shift_down(uv, 2), axis=0, keepdims=True)
            d_ref[1:2, :] += jnp.sum(dv * _shift_down(uv, 1), axis=0, keepdims=True)
            d_ref[2:3, :] += jnp.sum(dv * uv, axis=0, keepdims=True)

    act = pl.BlockSpec((seq, tc), lambda j, b: (b, j))
    wsp = pl.BlockSpec((3, tc), lambda j, b: (0, j))
    return pl.pallas_call(
        body,
        out_shape=(_sds((batch * seq, fh), BF16), _sds((batch * seq, fh), BF16), _sds((3, fh), F32), _sds((3, fh), F32)),
        grid=(nf, batch),
        in_specs=[
            act,
            act,
            pl.BlockSpec((seq, tc), lambda j, b: (b, nf + j)),
            wsp,
            pl.BlockSpec((3, tc), lambda j, b: (0, nf + j)),
        ],
        out_specs=(act, act, wsp, wsp),
        compiler_params=pltpu.CompilerParams(dimension_semantics=("parallel", "arbitrary")),
        name=name,
    )(dh, upre, upre, w, w)


def _merge_fwd(name, ycat, gl, bg):
    t, d2 = ycat.shape
    d = d2 // 2
    tm = _tile(t, 256, 16)

    def body(y_ref, gl_ref, bg_ref, o_ref):
        g = jax.nn.sigmoid(gl_ref[...] + bg_ref[...])
        prod = g * y_ref[...]
        o_ref[...] = (prod[:, 0:d] + prod[:, d:d2]).astype(o_ref.dtype)

    row = pl.BlockSpec((tm, d2), lambda i: (i, 0))
    return pl.pallas_call(
        body,
        out_shape=_sds((t, d), BF16),
        grid=(t // tm,),
        in_specs=[row, row, pl.BlockSpec((1, d2), lambda i: (0, 0))],
        out_specs=pl.BlockSpec((tm, d), lambda i: (i, 0)),
        compiler_params=pltpu.CompilerParams(dimension_semantics=("parallel",)),
        name=name,
    )(ycat, gl, bg)


def _merge_bwd(name, dm, ycat, gl, bg, width, gl_off):
    t, d2 = ycat.shape
    d = d2 // 2
    tm = _tile(t, 512, 16)
    wb = math.gcd(gl_off, d)
    nw = d // wb

    def body(dm_ref, y_ref, gl_ref, bg_ref, dgl_ref, dy_ref, dbg_ref):
        g = jax.nn.sigmoid(gl_ref[...] + bg_ref[...])
        dmv = dm_ref[...]
        dgl = dmv * y_ref[...] * (g * (1.0 - g))
        dgl_ref[...] = dgl.astype(dgl_ref.dtype)
        dy_ref[...] = (dmv * g).astype(dy_ref.dtype)

        @pl.when(pl.program_id(2) == 0)
        def _():
            dbg_ref[...] = jnp.zeros_like(dbg_ref)

        dbg_ref[...] += jnp.sum(dgl, axis=0, keepdims=True)

    half = pl.BlockSpec((tm, wb), lambda h, j, i: (i, h * nw + j))
    vec = pl.BlockSpec((1, wb), lambda h, j, i: (0, h * nw + j))
    return pl.pallas_call(
        body,
        out_shape=(_sds((t, width), BF16), _sds((t, d2), BF16), _sds((1, d2), F32)),
        grid=(2, nw, t // tm),
        in_specs=[pl.BlockSpec((tm, wb), lambda h, j, i: (i, j)), half, half, vec],
        out_specs=(pl.BlockSpec((tm, wb), lambda h, j, i: (i, gl_off // wb + h * nw + j)), half, vec),
        compiler_params=pltpu.CompilerParams(dimension_semantics=("parallel", "parallel", "arbitrary")),
        name=name,
    )(dm, ycat, gl, bg)


def _log_sigmoid(z):
    return jnp.minimum(z, 0.0) - jnp.log1p(jnp.exp(-jnp.abs(z)))


def _forget_fwd(name, fl, bf, batch, seq):
    def body(fl_ref, bf_ref, o_ref):
        lf = _log_sigmoid(fl_ref[:, 0:LANES] + bf_ref[:, 0:LANES])
        acc = lf.T[0:HEADS, :]
        lane = lax.broadcasted_iota(jnp.int32, acc.shape, 1)
        k = 1
        while k < seq:
            acc = acc + jnp.where(lane >= k, pltpu.roll(acc, k, axis=1), 0.0)
            k *= 2
        o_ref[...] = acc

    return pl.pallas_call(
        body,
        out_shape=_sds((batch, HEADS, seq), F32),
        grid=(batch,),
        in_specs=[pl.BlockSpec((seq, F_PAD), lambda b: (b, 0)), pl.BlockSpec((1, F_PAD), lambda b: (0, 0))],
        out_specs=pl.BlockSpec((None, HEADS, seq), lambda b: (b, 0, 0)),
        compiler_params=pltpu.CompilerParams(dimension_semantics=("parallel",)),
        name=name,
    )(fl, bf)


def _forget_bwd(name, d_key, d_query, fl, bf, dproj, f_off, batch, seq):
    nfb = F_PAD // LANES

    def body(dk_ref, dq_ref, fl_ref, bf_ref, _, df_ref, dbf_ref):
        jj = pl.program_id(1)
        key_t = jnp.concatenate([dk_ref[...], jnp.zeros((LANES - HEADS, seq), F32)], axis=0).T
        acc = dq_ref[...] - key_t
        row = lax.broadcasted_iota(jnp.int32, acc.shape, 0)
        k = 1
        while k < seq:
            acc = acc + jnp.where(row < seq - k, pltpu.roll(acc, seq - k, axis=0), 0.0)
            k *= 2
        z = fl_ref[:, 0:LANES] + bf_ref[:, 0:LANES]
        col = lax.broadcasted_iota(jnp.int32, acc.shape, 1)
        df = jnp.where(col < HEADS, acc * jax.nn.sigmoid(-z), 0.0)
        df = jnp.where(jj == 0, df, 0.0)
        df_ref[...] = df.astype(df_ref.dtype)

        @pl.when((pl.program_id(0) == 0) & (jj == 0))
        def _():
            dbf_ref[...] = jnp.zeros_like(dbf_ref)

        dbf_ref[...] += jnp.sum(df, axis=0, keepdims=True)

    return pl.pallas_call(
        body,
        out_shape=(_sds(dproj.shape, dproj.dtype), _sds((1, LANES), F32)),
        grid=(batch, nfb),
        in_specs=[
            pl.BlockSpec((None, HEADS, seq), lambda b, j: (b, 0, 0)),
            pl.BlockSpec((seq, LANES), lambda b, j: (b, 0)),
            pl.BlockSpec((seq, F_PAD), lambda b, j: (b, 0)),
            pl.BlockSpec((1, F_PAD), lambda b, j: (0, 0)),
            pl.BlockSpec(memory_space=pl.ANY),
        ],
        out_specs=(pl.BlockSpec((seq, LANES), lambda b, j: (b, f_off // LANES + j)),
                   pl.BlockSpec((1, LANES), lambda b, j: (0, 0))),
        input_output_aliases={4: 0},
        compiler_params=pltpu.CompilerParams(dimension_semantics=("arbitrary", "arbitrary")),
        name=name,
    )(d_key, d_query, fl, bf, dproj)


def _dot(a, b, mode):
    return lax.dot_general(a, b, _DIMS[mode], preferred_element_type=F32)


def _attn_fwd(name, qkv, frow, batch, seq, tq):
    nq = seq // tq
    scale = 1.0 / math.sqrt(HEAD_DIM)

    def body(q_ref, k_ref, v_ref, f_ref, o_ref, lse_ref):
        i = pl.program_id(2)
        lane = lax.broadcasted_iota(jnp.int32, (1, LANES), 1)
        lo = lane < HEAD_DIM
        qs = q_ref[...] * scale
        qh = (jnp.where(lo, qs, 0.0).astype(BF16), jnp.where(lo, 0.0, qs).astype(BF16))
        row = lax.broadcasted_iota(jnp.int32, (tq, tq), 0)
        col = lax.broadcasted_iota(jnp.int32, (tq, tq), 1)

        def step(j, carry, diag):
            m0, l0, m1, l1, acc = carry
            start = pl.multiple_of(j * tq, tq)
            kj = k_ref[pl.ds(start, tq), :]
            vj = v_ref[pl.ds(start, tq), :]
            ms, ls, pvs, alphas = [], [], [], []
            for h, (m_old, l_old) in enumerate(((m0, l0), (m1, l1))):
                s = _dot(qh[h], kj, "nt") - f_ref[h:h + 1, pl.ds(start, tq)]
                if diag:
                    s = jnp.where(col <= row, s, NEG_BIG)
                m_new = jnp.maximum(m_old, jnp.max(s, axis=1, keepdims=True))
                p = jnp.exp(s - m_new)
                alpha = jnp.exp(m_old - m_new)
                ls.append(alpha * l_old + jnp.sum(p, axis=1, keepdims=True))
                ms.append(m_new)
                alphas.append(alpha)
                vh = jnp.where(lo, vj, 0.0) if h == 0 else jnp.where(lo, 0.0, vj)
                pvs.append(_dot(p.astype(BF16), vh.astype(BF16), "nn"))
            acc = acc * jnp.where(lo, alphas[0], alphas[1]) + (pvs[0] + pvs[1])
            return ms[0], ls[0], ms[1], ls[1], acc

        neg = jnp.full((tq, 1), NEG_BIG, F32)
        zero = jnp.zeros((tq, 1), F32)
        init = (neg, zero, neg, zero, jnp.zeros((tq, LANES), F32))
        carry = lax.fori_loop(0, i, lambda j, c: step(j, c, False), init)
        m0, l0, m1, l1, acc = step(i, carry, True)
        o_ref[...] = (acc / jnp.where(lo, l0, l1)).astype(o_ref.dtype)
        lse_ref[:, 0:1] = m0 + jnp.log(l0)
        lse_ref[:, 1:2] = m1 + jnp.log(l1)

    return pl.pallas_call(
        body,
        out_shape=(_sds((batch * seq, ATTN_WIDTH), BF16), _sds((HEAD_PAIRS, batch * seq, 2), F32)),
        grid=(batch, HEAD_PAIRS, nq),
        in_specs=[
            pl.BlockSpec((tq, LANES), lambda b, hp, i: (b * nq + i, 3 * hp)),
            pl.BlockSpec((seq, LANES), lambda b, hp, i: (b, 3 * hp + 1)),
            pl.BlockSpec((seq, LANES), lambda b, hp, i: (b, 3 * hp + 2)),
            pl.BlockSpec((None, None, 2, seq), lambda b, hp, i: (b, hp, 0, 0)),
        ],
        out_specs=(
            pl.BlockSpec((tq, LANES), lambda b, hp, i: (b * nq + i, hp)),
            pl.BlockSpec((None, tq, 2), lambda b, hp, i: (hp, b * nq + i, 0)),
        ),
        compiler_params=pltpu.CompilerParams(dimension_semantics=("parallel", "parallel", "parallel")),
        name=name,
    )(qkv, qkv, qkv, frow)


def _attn_bwd(name, qkv, do, o, lse, frow, dproj, qkv_off, batch, seq, tq):
    nq = seq // tq
    scale = 1.0 / math.sqrt(HEAD_DIM)

    def body(q_ref, k_ref, v_ref, do_ref, o_ref, lse_ref, f_ref, _, dqkv_ref, df_ref, drow_ref,
             dq_acc, dk_acc, dv_acc, df_acc):
        j = pl.program_id(2)
        lane = lax.broadcasted_iota(jnp.int32, (1, LANES), 1)
        lo = lane < HEAD_DIM
        masks = (lo, jnp.logical_not(lo))
        row = lax.broadcasted_iota(jnp.int32, (tq, tq), 0)
        col = lax.broadcasted_iota(jnp.int32, (tq, tq), 1)

        @pl.when(j == 0)
        def _():
            dq_acc[...] = jnp.zeros_like(dq_acc)
            drow_ref[...] = jnp.zeros_like(drow_ref)

        dk_acc[...] = jnp.zeros_like(dk_acc)
        dv_acc[...] = jnp.zeros_like(dv_acc)
        df_acc[...] = jnp.zeros_like(df_acc)
        kj = k_ref[...]
        vj = v_ref[...]
        kstart = pl.multiple_of(j * tq, tq)
        kh = tuple(jnp.where(mk, kj, 0.0).astype(BF16) for mk in masks)

        def step(i, diag):
            start = pl.multiple_of(i * tq, tq)
            rows = pl.ds(start, tq)
            qi = q_ref[rows, :] * scale
            doi = do_ref[rows, :]
            prod = doi.astype(F32) * o_ref[rows, :].astype(F32)
            lse_i = lse_ref[rows, :]
            dq_i = jnp.zeros((tq, LANES), F32)
            for h, mk in enumerate(masks):
                q_h = jnp.where(mk, qi, 0.0).astype(BF16)
                do_h = jnp.where(mk, doi, 0.0).astype(BF16)
                delta = jnp.sum(jnp.where(mk, prod, 0.0), axis=1, keepdims=True)
                s = _dot(q_h, kj, "nt") - f_ref[h:h + 1, pl.ds(kstart, tq)]
                p = jnp.exp(s - lse_i[:, h:h + 1])
                if diag:
                    p = jnp.where(col <= row, p, 0.0)
                ds = p * (_dot(do_h, vj, "nt") - delta)
                df_acc[h:h + 1, :] += jnp.sum(ds, axis=0, keepdims=True)
                drow_ref[rows, h:h + 1] += jnp.sum(ds, axis=1, keepdims=True)
                dsb = ds.astype(BF16)
                dv_acc[...] += _dot(p.astype(BF16), do_h, "tn")
                dk_acc[...] += _dot(dsb, q_h, "tn")
                dq_i = dq_i + _dot(dsb, kh[h], "nn")
            dq_acc[rows, :] += dq_i

        step(j, True)
        lax.fori_loop(j + 1, nq, lambda i, c: (step(i, False), c)[1], 0)
        dqkv_ref[:, 0:LANES] = (dq_acc[pl.ds(kstart, tq), :] * scale).astype(dqkv_ref.dtype)
        dqkv_ref[:, LANES:2 * LANES] = dk_acc[...].astype(dqkv_ref.dtype)
        dqkv_ref[:, 2 * LANES:3 * LANES] = dv_acc[...].astype(dqkv_ref.dtype)
        df_ref[...] = df_acc[...]

    full = lambda c: pl.BlockSpec((seq, LANES), lambda b, hp, j: (b, c(hp)))
    blk = lambda c: pl.BlockSpec((tq, LANES), lambda b, hp, j: (b * nq + j, c(hp)))
    return pl.pallas_call(
        body,
        out_shape=(_sds(dproj.shape, dproj.dtype), _sds((batch, HEAD_PAIRS, 2, seq), F32),
                   _sds((HEAD_PAIRS, batch * seq, 2), F32)),
        grid=(batch, HEAD_PAIRS, nq),
        in_specs=[
            full(lambda hp: 3 * hp),
            blk(lambda hp: 3 * hp + 1),
            blk(lambda hp: 3 * hp + 2),
            full(lambda hp: hp),
            full(lambda hp: hp),
            pl.BlockSpec((None, seq, 2), lambda b, hp, j: (hp, b, 0)),
            pl.BlockSpec((None, None, 2, seq), lambda b, hp, j: (b, hp, 0, 0)),
            pl.BlockSpec(memory_space=pl.ANY),
        ],
        out_specs=(
            pl.BlockSpec((tq, 3 * LANES), lambda b, hp, j: (b * nq + j, qkv_off // (3 * LANES) + hp)),
            pl.BlockSpec((None, None, 2, tq), lambda b, hp, j: (b, hp, 0, j)),
            pl.BlockSpec((None, seq, 2), lambda b, hp, j: (hp, b, 0)),
        ),
        scratch_shapes=[
            pltpu.VMEM((seq, LANES), F32),
            pltpu.VMEM((tq, LANES), F32),
            pltpu.VMEM((tq, LANES), F32),
            pltpu.VMEM((2, tq), F32),
        ],
        input_output_aliases={7: 0},
        compiler_params=pltpu.CompilerParams(dimension_semantics=("parallel", "parallel", "arbitrary")),
        name=name,
    )(qkv, qkv, qkv, do, o, lse, frow, dproj)


def _mesh_place():
    x, y, c = lax.axis_index("x"), lax.axis_index("y"), lax.axis_index("c")
    chips = [(1 - x, y), (x, 1 - y), (1 - x, 1 - y)]
    return x, y, c, chips


def _hbm_specs(n):
    return [pl.BlockSpec(memory_space=pl.ANY)] * n


def _half(shape2d, axis, which):
    size = shape2d[axis] // 2
    sl = pl.ds(pl.multiple_of(which * size, 16 if axis == 0 else LANES), size)
    return (sl, slice(None)) if axis == 0 else (slice(None), sl)


def _gather_weights(bigs, axes, smalls):
    nb, ns = len(bigs), len(smalls)
    arrays = list(bigs) + list(smalls)
    n = nb + ns

    def body(*refs):
        ins, outs = refs[:n], refs[n:2 * n]
        send_sems, recv_sems = refs[2 * n:]
        x, y, c, chips = _mesh_place()
        me = 2 * x + y
        sibling = (x, y, 1 - c)

        def half(a, which):
            return _half(arrays[a].shape, axes[a], which)

        def copy(a, k, src, dst, to):
            return pltpu.make_async_remote_copy(src_ref=src, dst_ref=dst, send_sem=send_sems.at[a, k],
                                                recv_sem=recv_sems.at[a, k], device_id=to, device_id_type=MESH)

        sends = []
        for a in range(n):
            for j, chip in enumerate(chips):
                if a < nb:
                    cp = copy(a, j, ins[a].at[half(a, c)], outs[a].at[(me,) + half(a, c)], (*chip, c))
                else:
                    cp = copy(a, j, ins[a], outs[a].at[me], (*chip, c))
                cp.start()
                sends.append(cp)
        for a in range(nb):
            for j, (px, py) in enumerate(chips):
                blk = outs[a].at[(2 * px + py,) + half(a, c)]
                copy(a, j, blk, blk, (px, py, c)).wait_recv()
                fwd = copy(a, 3 + j, blk, blk, sibling)
                fwd.start()
                sends.append(fwd)
        for a in range(nb, n):
            for j, (px, py) in enumerate(chips):
                blk = outs[a].at[2 * px + py]
                copy(a, j, blk, blk, (px, py, c)).wait_recv()
        for a in range(nb):
            for j, (px, py) in enumerate(chips):
                blk = outs[a].at[(2 * px + py,) + half(a, 1 - c)]
                copy(a, 3 + j, blk, blk, sibling).wait_recv()
        for cp in sends:
            cp.wait_send()

    outs = pl.pallas_call(
        body,
        out_shape=tuple(_sds((N_CHIPS,) + a.shape, a.dtype) for a in arrays),
        in_specs=_hbm_specs(n),
        out_specs=tuple(_hbm_specs(n)),
        scratch_shapes=[pltpu.SemaphoreType.DMA((n, 6)), pltpu.SemaphoreType.DMA((n, 6))],
        name="gather_weights",
    )(*arrays)
    me = 2 * lax.axis_index("x") + lax.axis_index("y")
    return tuple(lax.dynamic_update_index_in_dim(o, a, me, 0) for o, a in zip(outs, arrays))


def _gather_small(v):
    m_per, ncol = v.shape

    def body(x_ref, out_ref, send_sems, recv_sems, local_sem):
        x, y, c, chips = _mesh_place()
        me, sibling = (x, y, c), (x, y, 1 - c)

        def rows(px, py, pc):
            return out_ref.at[pl.ds((4 * px + 2 * py + pc) * m_per, m_per), :]

        def copy(k, block, to, src=None):
            return pltpu.make_async_remote_copy(src_ref=rows(*block) if src is None else src, dst_ref=rows(*block),
                                                send_sem=send_sems.at[k], recv_sem=recv_sems.at[k],
                                                device_id=to, device_id_type=MESH)

        mine = pltpu.make_async_copy(x_ref, rows(*me), local_sem)
        mine.start()
        first = [copy(0, me, sibling, src=x_ref)]
        first += [copy(1 + j, me, (*chip, c), src=x_ref) for j, chip in enumerate(chips)]
        for cp in first:
            cp.start()
        passed = [copy(4 + j, (*chip, c), sibling) for j, chip in enumerate(chips)]
        for j, chip in enumerate(chips):
            copy(1 + j, (*chip, c), me).wait_recv()
            passed[j].start()
        copy(0, sibling, me).wait_recv()
        for j, chip in enumerate(chips):
            copy(4 + j, (*chip, 1 - c), me).wait_recv()
        for cp in first + passed:
            cp.wait_send()
        mine.wait()

    return pl.pallas_call(
        body,
        out_shape=_sds((N_DEV * m_per, ncol), v.dtype),
        in_specs=[pl.BlockSpec(memory_space=pltpu.VMEM)],
        out_specs=pl.BlockSpec(memory_space=pltpu.VMEM),
        scratch_shapes=[pltpu.SemaphoreType.DMA((7,)), pltpu.SemaphoreType.DMA((7,)), pltpu.SemaphoreType.DMA],
        name="gather_small",
    )(v)


def _half_shape(shape2d, axis):
    return (shape2d[0] // 2, shape2d[1]) if axis == 0 else (shape2d[0], shape2d[1] // 2)


def _exchange_sibling(grads, axes):
    n = len(grads)

    def body(*refs):
        ins, outs = refs[:n], refs[n:2 * n]
        send_sems, recv_sems = refs[2 * n:]
        x, y, c, _ = _mesh_place()
        copies = []
        for a in range(n):
            src = ins[a].at[(slice(None),) + _half(grads[a].shape[1:], axes[a], 1 - c)]
            cp = pltpu.make_async_remote_copy(src_ref=src, dst_ref=outs[a], send_sem=send_sems.at[a],
                                              recv_sem=recv_sems.at[a], device_id=(x, y, 1 - c), device_id_type=MESH)
            cp.start()
            copies.append(cp)
        for cp in copies:
            cp.wait()

    return pl.pallas_call(
        body,
        out_shape=tuple(_sds((N_CHIPS,) + _half_shape(g.shape[1:], ax), g.dtype) for g, ax in zip(grads, axes)),
        in_specs=_hbm_specs(n),
        out_specs=tuple(_hbm_specs(n)),
        scratch_shapes=[pltpu.SemaphoreType.DMA((n,)), pltpu.SemaphoreType.DMA((n,))],
        name="exchange_sibling",
    )(*grads)


def _exchange_chips(sums):
    n = len(sums)

    def body(*refs):
        ins, outs = refs[:n], refs[n:2 * n]
        send_sems, recv_sems = refs[2 * n:]
        _, _, c, chips = _mesh_place()
        copies = []
        for a in range(n):
            for j, (px, py) in enumerate(chips):
                cp = pltpu.make_async_remote_copy(src_ref=ins[a].at[2 * px + py], dst_ref=outs[a].at[j],
                                                  send_sem=send_sems.at[a, j], recv_sem=recv_sems.at[a, j],
                                                  device_id=(px, py, c), device_id_type=MESH)
                cp.start()
                copies.append(cp)
        for cp in copies:
            cp.wait()

    return pl.pallas_call(
        body,
        out_shape=tuple(_sds((3,) + s.shape[1:], s.dtype) for s in sums),
        in_specs=_hbm_specs(n),
        out_specs=tuple(_hbm_specs(n)),
        scratch_shapes=[pltpu.SemaphoreType.DMA((n, 3)), pltpu.SemaphoreType.DMA((n, 3))],
        name="exchange_chips",
    )(*sums)


def _share_sibling(shards, axes):
    n = len(shards)

    def body(*refs):
        ins, outs = refs[:n], refs[n:2 * n]
        send_sems, recv_sems = refs[2 * n:]
        x, y, c, _ = _mesh_place()
        started = []
        for a in range(n):
            mine = _half(shards[a].shape, axes[a], c)
            theirs = _half(shards[a].shape, axes[a], 1 - c)
            cp = pltpu.make_async_remote_copy(src_ref=ins[a].at[mine], dst_ref=outs[a].at[mine],
                                              send_sem=send_sems.at[a], recv_sem=recv_sems.at[a],
                                              device_id=(x, y, 1 - c), device_id_type=MESH)
            cp.start()
            arrival = pltpu.make_async_remote_copy(src_ref=ins[a].at[theirs], dst_ref=outs[a].at[theirs],
                                                   send_sem=send_sems.at[a], recv_sem=recv_sems.at[a],
                                                   device_id=(x, y, 1 - c), device_id_type=MESH)
            started.append((cp, arrival))
        for cp, arrival in started:
            arrival.wait_recv()
            cp.wait_send()

    return pl.pallas_call(
        body,
        out_shape=tuple(_sds(s.shape, s.dtype) for s in shards),
        in_specs=_hbm_specs(n),
        out_specs=tuple(_hbm_specs(n)),
        scratch_shapes=[pltpu.SemaphoreType.DMA((n,)), pltpu.SemaphoreType.DMA((n,))],
        input_output_aliases={a: a for a in range(n)},
        name="share_sibling",
    )(*shards)


def _pair_sum(name, place, g, got, axis):
    hr, hc = got.shape[1:]

    def body(place_ref, g_ref, got_ref, o_ref):
        o_ref[...] = (g_ref[...] + got_ref[...]).astype(o_ref.dtype)

    blk = (None, hr, hc)
    mine = (lambda k, pr: (k, pr[1], 0)) if axis == 0 else (lambda k, pr: (k, 0, pr[1]))
    return pl.pallas_call(
        body,
        out_shape=_sds((N_CHIPS, hr, hc), BF16),
        grid_spec=pltpu.PrefetchScalarGridSpec(
            num_scalar_prefetch=1,
            grid=(N_CHIPS,),
            in_specs=[pl.BlockSpec(blk, mine), pl.BlockSpec(blk, lambda k, pr: (k, 0, 0))],
            out_specs=pl.BlockSpec(blk, lambda k, pr: (k, 0, 0)),
        ),
        compiler_params=pltpu.CompilerParams(dimension_semantics=("parallel",)),
        name=name,
    )(place, g, got)


def _chip_sum(name, place, g, got, arrivals, axis):
    _, r, cdim = g.shape
    hr, hc = got.shape[1:]

    def body(place_ref, g_ref, got_ref, arr_ref, o_ref):
        acc = g_ref[...] + got_ref[...]
        for j in range(3):
            acc = acc + arr_ref[j].astype(F32)
        o_ref[...] = acc

    blk = (None, hr, hc)
    mine = (lambda i, pr: (pr[0], pr[1], 0)) if axis == 0 else (lambda i, pr: (pr[0], 0, pr[1]))
    dest = (lambda i, pr: (pr[1], 0)) if axis == 0 else (lambda i, pr: (0, pr[1]))
    return pl.pallas_call(
        body,
        out_shape=_sds((r, cdim), F32),
        grid_spec=pltpu.PrefetchScalarGridSpec(
            num_scalar_prefetch=1,
            grid=(1,),
            in_specs=[
                pl.BlockSpec(blk, mine),
                pl.BlockSpec(blk, lambda i, pr: (pr[0], 0, 0)),
                pl.BlockSpec((3, hr, hc), lambda i, pr: (0, 0, 0)),
            ],
            out_specs=pl.BlockSpec((hr, hc), dest),
        ),
        compiler_params=pltpu.CompilerParams(dimension_semantics=("arbitrary",)),
        name=name,
    )(place, g, got, arrivals)


def _device_sum(name, gathered):
    m_per = gathered.shape[0] // N_DEV

    def body(g_ref, o_ref):
        acc = g_ref[0:m_per, :]
        for dev in range(1, N_DEV):
            acc = acc + g_ref[dev * m_per:(dev + 1) * m_per, :]
        o_ref[...] = acc

    return pl.pallas_call(body, out_shape=_sds((m_per, gathered.shape[1]), F32), name=name)(gathered)


def _adamw(name, w, g, m, v):
    r, cdim = w.shape
    if r % 8 == 0:
        tr, tcol = _tile(r, 256, 8), cdim
    else:
        tr, tcol = r, (_tile(cdim, 256, LANES) if cdim % LANES == 0 else cdim)
    bc1 = 1.0 - ADAM_B1 ** ADAM_STEP
    bc2 = 1.0 - ADAM_B2 ** ADAM_STEP

    def body(w_ref, g_ref, m_ref, v_ref, d_ref, nm_ref, nv_ref):
        gv = g_ref[...]
        nm = ADAM_B1 * m_ref[...] + (1.0 - ADAM_B1) * gv
        nv = ADAM_B2 * v_ref[...] + (1.0 - ADAM_B2) * (gv * gv)
        d_ref[...] = -ADAM_LR * ((nm / bc1) / (jnp.sqrt(nv / bc2) + ADAM_EPS) + ADAM_WD * w_ref[...])
        nm_ref[...] = nm
        nv_ref[...] = nv

    blk = pl.BlockSpec((tr, tcol), lambda i, j: (i, j))
    shape = _sds((r, cdim), F32)
    return pl.pallas_call(
        body,
        out_shape=(shape, shape, shape),
        grid=(r // tr, cdim // tcol),
        in_specs=[blk] * 4,
        out_specs=(blk, blk, blk),
        compiler_params=pltpu.CompilerParams(dimension_semantics=("parallel", "parallel")),
        name=name,
    )(w, g, m, v)


def _cat_cols(g):
    return jnp.transpose(g, (1, 0, 2)).reshape(g.shape[1], N_CHIPS * g.shape[2])


def _split_cols(a):
    r, c4 = a.shape
    return jnp.transpose(a.reshape(r, N_CHIPS, c4 // N_CHIPS), (1, 0, 2))


def _local_step(x, target, w_int, w_oc, w_oa, w_o, w_up, w_down, cmw, cfw, g1, b_f, b_gate, g2, gf):
    batch, seq, d = x.shape
    t = batch * seq
    cw = d // 2
    fh = w_down.shape[0]
    tc = LANES
    nct = cw // tc
    tq = min(256, seq)
    pc_w, qkv_w, gl_w = 3 * cw, 3 * ATTN_WIDTH, 2 * d
    qkv_off, gl_off, f_off = pc_w, pc_w + qkv_w, pc_w + qkv_w + gl_w
    width = f_off + F_PAD
    f_col = pc_w + qkv_w

    w_pc = w_int[:pc_w].reshape(3, nct, tc, d).transpose(1, 0, 2, 3).reshape(pc_w, d)
    w_qkv = w_int[pc_w:f_col].reshape(3, HEAD_PAIRS, LANES, d).transpose(1, 0, 2, 3).reshape(qkv_w, d)
    w_f = jnp.pad(w_int[f_col:f_col + HEADS], ((0, F_PAD - HEADS), (0, 0)))
    w_inp = jnp.concatenate([w_pc, w_qkv, w_int[f_col + HEADS:], w_f], axis=0)
    bf_pad = jnp.pad(b_f, ((0, 0), (0, F_PAD - HEADS)))

    x2d = x.reshape(t, d)
    tgt2d = target.reshape(t, d)

    h1 = _rms_fwd("norm_mix", x2d, g1)
    pc = _mm("proj_conv", h1, w_inp, "nt", F32, m=t, n=pc_w, k=d, b_roff=0)
    qkv = _mm("proj_qkv", h1, w_inp, "nt", BF16, m=t, n=qkv_w, k=d, b_roff=qkv_off)
    gl = _mm("proj_gate", h1, w_inp, "nt", F32, m=t, n=gl_w, k=d, b_roff=gl_off)
    fl = _mm("proj_forget", h1, w_inp, "nt", F32, m=t, n=F_PAD, k=d, b_roff=f_off)
    a_c = _conv_fwd("conv_mix", pc, cmw, batch, seq, tc)
    f_cum = _forget_fwd("forget_cumsum", fl, bf_pad, batch, seq)
    frow = f_cum.reshape(batch, HEAD_PAIRS, 2, seq)
    o, lse = _attn_fwd("attn_fwd", qkv, frow, batch, seq, tq)
    ycat = _mm("out_conv", a_c, w_oc, "nn", F32, m=t, n=d, k=cw, o_off=0, o_width=2 * d)
    ycat = _mm("out_attn", o, w_oa, "nn", F32, m=t, n=d, k=ATTN_WIDTH, out=ycat, o_off=d)
    mg = _merge_fwd("gate_merge", ycat, gl, b_gate)
    x2 = _mm("mix_out", mg, w_o, "nn", F32, m=t, n=d, k=d, add=x2d)
    h2 = _rms_fwd("norm_ffn", x2, g2)
    upre = _mm("ffn_up", h2, w_up, "nn", F32, m=t, n=2 * fh, k=d, b3=True)
    hmid = _ffn_fwd("ffn_act", upre, cfw, batch, seq, tc)
    x3 = _mm("ffn_down", hmid, w_down, "nn", F32, m=t, n=d, k=fh, add=x2, tk=4096)

    dx3, loss_row, d_gf = _final_loss("final_loss", x3, gf.reshape(1, d), tgt2d)
    d_hmid = _mm("d_ffn_act", dx3, w_down, "nt", F32, m=t, n=fh, k=d, tm=512, tn=4096)
    dw_down = _mm("dw_down", hmid, dx3, "tn", F32, m=fh, n=d, k=t, tm=1408, tk=512)
    du_a, du_b, d_cfw_a, d_cfw_b = _ffn_bwd("d_ffn_conv", d_hmid, upre, cfw, batch, seq, tc)
    ws = w_up.shape[2]
    dw_up = _mm("dw_up_a", h2, du_a, "tn", F32, m=d, n=fh, k=t, tn=ws, tk=512, o3=N_CHIPS)
    dw_up = _mm("dw_up_b", h2, du_b, "tn", F32, m=d, n=fh, k=t, tn=ws, tk=512, o3=N_CHIPS, out=dw_up, o_off=fh)
    dh2 = _mm("d_norm_ffn_a", du_a, w_up, "nt", F32, m=t, n=d, k=fh, b_off=0, b3=True)
    dh2 = _mm("d_norm_ffn_b", du_b, w_up, "nt", F32, m=t, n=d, k=fh, b_off=fh, b3=True, add=dh2)
    dx2, d_g2 = _rms_bwd("d_norm_ffn", x2, dh2, g2, dx3)
    dm = _mm("d_merge", dx2, w_o, "nt", F32, m=t, n=d, k=d)
    dw_o = _mm("dw_o", mg, dx2, "tn", F32, m=d, n=d, k=t, tk=512)
    dproj, dycat, d_bg = _merge_bwd("d_gate_merge", dm, ycat, gl, b_gate, width, gl_off)
    da_c = _mm("d_conv_out", dycat, w_oc, "nt", F32, m=t, n=cw, k=d, a_off=0)
    do = _mm("d_attn_out", dycat, w_oa, "nt", BF16, m=t, n=ATTN_WIDTH, k=d, a_off=d)
    dw_oc = _mm("dw_out_conv", a_c, dycat, "tn", F32, m=cw, n=d, k=t, b_off=0, tk=512)
    dw_oa = _mm("dw_out_attn", o, dycat, "tn", F32, m=ATTN_WIDTH, n=d, k=t, b_off=d, tk=512)
    dproj, d_cmw = _conv_bwd("d_conv_mix", da_c, pc, cmw, dproj, batch, seq, tc)
    dproj, d_fkey, d_fquery = _attn_bwd("attn_bwd", qkv, do, o, lse, frow, dproj, qkv_off, batch, seq, tq)
    d_fquery = jnp.pad(jnp.transpose(d_fquery, (1, 0, 2)).reshape(t, HEADS), ((0, 0), (0, LANES - HEADS)))
    dproj, d_bf = _forget_bwd("d_forget", d_fkey.reshape(batch, HEADS, seq), d_fquery, fl, bf_pad, dproj, f_off,
                              batch, seq)
    dw_inp = _mm("dw_in", dproj, h1, "tn", F32, m=width, n=d, k=t, tm=1792, tk=512)
    dh1 = _mm("d_norm_mix", dproj, w_inp, "nn", F32, m=t, n=d, k=width, tk=1792)
    grad_x, d_g1 = _rms_bwd("d_norm_mix_x", x2d, dh1, g1, dx2)

    d_pc = dw_inp[:pc_w].reshape(nct, 3, tc, d).transpose(1, 0, 2, 3).reshape(pc_w, d)
    d_qkv = dw_inp[qkv_off:gl_off].reshape(HEAD_PAIRS, 3, LANES, d).transpose(1, 0, 2, 3).reshape(qkv_w, d)
    dw_int = jnp.concatenate([d_pc, d_qkv, dw_inp[f_off:f_off + HEADS], dw_inp[gl_off:f_off]], axis=0)
    d_cfw = jnp.concatenate([d_cfw_a, d_cfw_b], axis=1)
    mats = (dw_int, dw_oc, dw_oa, dw_o, dw_up, dw_down)
    smalls = (d_g1, d_g2, d_gf, d_bg, d_bf, d_cmw, d_cfw)
    return loss_row[0, 0], grad_x.reshape(batch, seq, d), mats, smalls


def _pack_small(parts):
    flat = [p.reshape(-1) for p in parts]
    sizes = [f.shape[0] for f in flat]
    total = sum(sizes)
    padded = -(-total // (8 * LANES)) * (8 * LANES)
    vec = jnp.concatenate(flat + [jnp.zeros((padded - total,), F32)])
    offsets = [sum(sizes[:i]) for i in range(len(sizes))]
    return vec.reshape(padded // LANES, LANES), offsets


def kernel(x, norm_mix_g, w_in, b_f, b_gate, conv_mix_w, w_out_conv, w_out_attn, w_o, norm_ffn_g, w_up, conv_ffn_w, w_down, norm_f_g, loss_target, m_norm_mix_g, m_w_in, m_b_f, m_b_gate, m_conv_mix_w, m_w_out_conv, m_w_out_attn, m_w_o, m_norm_ffn_g, m_w_up, m_conv_ffn_w, m_w_down, m_norm_f_g, v_norm_mix_g, v_w_in, v_b_f, v_b_gate, v_conv_mix_w, v_w_out_conv, v_w_out_attn, v_w_o, v_norm_ffn_g, v_w_up, v_conv_ffn_w, v_w_down, v_norm_f_g):
    d = x.shape[-1]
    chip = 2 * lax.axis_index("x") + lax.axis_index("y")
    place = jnp.stack([chip, lax.axis_index("c")]).astype(jnp.int32)

    t_in, t_m_in, t_v_in = (jnp.transpose(w[0]) for w in (w_in, m_w_in, v_w_in))
    axes = (1, 0, 0, 0, 0, 0)

    bigs = [t_in.astype(BF16)] + [w[0].astype(BF16) for w in (w_out_conv, w_out_attn, w_o, w_up, w_down)]
    gathered = _gather_weights(bigs, axes, [conv_mix_w[0], conv_ffn_w[0]])
    a_in, a_oc, a_oa, a_o, a_up, a_down, a_cmw, a_cfw = gathered
    full_in = a_in.reshape(N_CHIPS * a_in.shape[1], a_in.shape[2])
    full_o = a_o.reshape(N_CHIPS * a_o.shape[1], a_o.shape[2])
    full_down = a_down.reshape(N_CHIPS * a_down.shape[1], a_down.shape[2])

    loss_local, grad_x, mats, smalls = _local_step(
        x, loss_target, full_in, _cat_cols(a_oc), _cat_cols(a_oa), full_o, a_up, full_down,
        _cat_cols(a_cmw), _cat_cols(a_cfw), norm_mix_g, b_f, b_gate, norm_ffn_g, norm_f_g)
    dw_int, dw_oc, dw_oa, dw_o, dw_up, dw_down = mats

    grads = [
        dw_int.reshape(N_CHIPS, dw_int.shape[0] // N_CHIPS, dw_int.shape[1]),
        _split_cols(dw_oc), _split_cols(dw_oa),
        dw_o.reshape(N_CHIPS, dw_o.shape[0] // N_CHIPS, dw_o.shape[1]),
        dw_up,
        dw_down.reshape(N_CHIPS, dw_down.shape[0] // N_CHIPS, dw_down.shape[1]),
    ]
    names = ("w_in", "w_out_conv", "w_out_attn", "w_o", "w_up", "w_down")
    got = _exchange_sibling(grads, axes)
    sums = [_pair_sum("pair_sum_" + nm, place, g, r, ax) for nm, g, r, ax in zip(names, grads, got, axes)]
    arrivals = _exchange_chips(sums)
    halves = [_chip_sum("chip_sum_" + nm, place, g, r, arr, ax)
              for nm, g, r, arr, ax in zip(names, grads, got, arrivals, axes)]
    g_in, g_oc, g_oa, g_o, g_up, g_down = _share_sibling(halves, axes)

    packed, offs = _pack_small(smalls)
    total = _device_sum("device_sum", _gather_small(packed)).reshape(-1)
    shapes = [s.shape for s in smalls]
    d_g1, d_g2, d_gf, d_bg, d_bf, d_cmw, d_cfw = [
        total[o:o + math.prod(sh)].reshape(sh) for o, sh in zip(offs, shapes)]
    d_bf = d_bf[:, :HEADS]
    cw_s, cf_s = conv_mix_w.shape[2], conv_ffn_w.shape[2]
    d_cmw = lax.dynamic_slice(d_cmw, (0, chip * cw_s), (3, cw_s))
    d_cfw = lax.dynamic_slice(d_cfw, (0, chip * cf_s), (3, cf_s))

    loss = lax.psum(loss_local, ("x", "y", "c"))
    order = [
        ("norm_mix_g", norm_mix_g[0:1], d_g1, m_norm_mix_g, v_norm_mix_g),
        ("w_in", t_in, g_in, t_m_in, t_v_in),
        ("b_f", b_f, d_bf, m_b_f, v_b_f),
        ("b_gate", b_gate, d_bg, m_b_gate, v_b_gate),
        ("conv_mix_w", conv_mix_w[0], d_cmw, m_conv_mix_w[0], v_conv_mix_w[0]),
        ("w_out_conv", w_out_conv[0], g_oc, m_w_out_conv[0], v_w_out_conv[0]),
        ("w_out_attn", w_out_attn[0], g_oa, m_w_out_attn[0], v_w_out_attn[0]),
        ("w_o", w_o[0], g_o, m_w_o[0], v_w_o[0]),
        ("norm_ffn_g", norm_ffn_g, d_g2, m_norm_ffn_g, v_norm_ffn_g),
        ("w_up", w_up[0], g_up, m_w_up[0], v_w_up[0]),
        ("conv_ffn_w", conv_ffn_w[0], d_cfw, m_conv_ffn_w[0], v_conv_ffn_w[0]),
        ("w_down", w_down[0], g_down, m_w_down[0], v_w_down[0]),
        ("norm_f_g", norm_f_g.reshape(1, d), d_gf, m_norm_f_g.reshape(1, d), v_norm_f_g.reshape(1, d)),
    ]
    out_shapes = [norm_mix_g.shape, w_in.shape, b_f.shape, b_gate.shape, conv_mix_w.shape, w_out_conv.shape,
                  w_out_attn.shape, w_o.shape, norm_ffn_g.shape, w_up.shape, conv_ffn_w.shape, w_down.shape,
                  norm_f_g.shape]
    g_out, d_out, m_out, v_out = [], [], [], []
    for (nm, w, g, m, v), sh in zip(order, out_shapes):
        g = g.reshape(w.shape)
        delta, new_m, new_v = _adamw("adamw_" + nm, w, g, m.reshape(w.shape), v.reshape(w.shape))
        for dst, val in ((g_out, g), (d_out, delta), (m_out, new_m), (v_out, new_v)):
            dst.append((jnp.transpose(val) if nm == "w_in" else val).reshape(sh))
    return (loss, grad_x, *g_out, *d_out, *m_out, *v_out)
```

```python
import functools
import math

import jax
import jax.numpy as jnp
from jax import lax
from jax.experimental import pallas as pl
from jax.experimental.pallas import tpu as pltpu

F32 = jnp.float32
BF16 = jnp.bfloat16
MESH = pl.DeviceIdType.MESH

EPS = 1e-6
HEADS = 8
HEAD_DIM = 64
ATTN_WIDTH = HEADS * HEAD_DIM
HEAD_PAIRS = HEADS // 2
LANES = 128
F_PAD = 2 * LANES
NEG_BIG = -1e30
N_CHIPS = 4
N_DEV = 8

ADAM_LR = 0.001
ADAM_B1 = 0.9
ADAM_B2 = 0.999
ADAM_EPS = 1e-08
ADAM_WD = 0.01
ADAM_STEP = 10

_DIMS = {
    "nn": (((1,), (0,)), ((), ())),
    "nt": (((1,), (1,)), ((), ())),
    "tn": (((0,), (0,)), ((), ())),
}


def _tile(n, target, mult, also=()):
    best = None
    for t in range(mult, n + 1, mult):
        if n % t == 0 and t <= target and all(o % t == 0 for o in also):
            best = t
    if best is None:
        assert all(o == 0 for o in also), (n, target, mult, also)
        return n
    return best


def _sds(shape, dtype):
    return jax.ShapeDtypeStruct(shape, dtype)


def _mm(name, a, b, mode, out_dtype, *, m, n, k, a_off=0, b_off=0, b_roff=0, b3=False, out=None, o_off=0,
        o_width=None, o3=None, add=None, tm=1024, tn=2048, tk=2048):
    wb = b.shape[2] if b3 else None
    if mode == "nn":
        tm = _tile(m, tm, 16)
        tk = _tile(k, tk, LANES, (a_off,))
        tn = wb if b3 else _tile(n, tn, LANES, (b_off, o_off))
        a_spec = pl.BlockSpec((tm, tk), lambda i, j, kk: (i, a_off // tk + kk))
        if b3:
            b_spec = pl.BlockSpec((None, tk, tn), lambda i, j, kk: (b_off // tn + j, kk, 0))
        else:
            b_spec = pl.BlockSpec((tk, tn), lambda i, j, kk: (kk, b_off // tn + j))
    elif mode == "nt":
        tm = _tile(m, tm, 16)
        tk = wb if b3 else _tile(k, tk, LANES, (a_off, b_off))
        tn = _tile(n, tn, LANES, (o_off, b_roff))
        a_spec = pl.BlockSpec((tm, tk), lambda i, j, kk: (i, a_off // tk + kk))
        if b3:
            b_spec = pl.BlockSpec((None, tn, tk), lambda i, j, kk: (b_off // tk + kk, b_roff // tn + j, 0))
        else:
            b_spec = pl.BlockSpec((tn, tk), lambda i, j, kk: (b_roff // tn + j, b_off // tk + kk))
    else:
        tm = _tile(m, tm, LANES, (a_off,))
        tk = _tile(k, tk, 16)
        tn = _tile(n, tn, LANES, (b_off, o_off))
        a_spec = pl.BlockSpec((tk, tm), lambda i, j, kk: (kk, a_off // tm + i))
        b_spec = pl.BlockSpec((tk, tn), lambda i, j, kk: (kk, b_off // tn + j))
    assert m % tm == 0 and n % tn == 0 and k % tk == 0, (name, tm, tn, tk)
    nk = k // tk
    if o3 is not None:
        o_spec = pl.BlockSpec((None, tm, tn), lambda i, j, kk: (o_off // tn + j, i, 0))
        out_sds = _sds((o3, m, tn), out_dtype)
    else:
        o_spec = pl.BlockSpec((tm, tn), lambda i, j, kk: (i, o_off // tn + j))
        width = o_width if o_width is not None else (out.shape[1] if out is not None else n)
        out_sds = _sds((m, width), out_dtype)
    use_acc = nk > 1 and out_dtype != F32
    dims = _DIMS[mode]
    has_add, has_out = add is not None, out is not None

    def body(*refs):
        a_ref, b_ref = refs[0], refs[1]
        pos = 2
        add_ref = None
        if has_add:
            add_ref = refs[pos]
            pos += 1
        if has_out:
            pos += 1
        o_ref = refs[pos]
        acc_ref = refs[pos + 1] if use_acc else None
        part = lax.dot_general(a_ref[...].astype(BF16), b_ref[...].astype(BF16), dims,
                               preferred_element_type=F32)
        if nk == 1:
            if has_add:
                part = part + add_ref[...]
            o_ref[...] = part.astype(o_ref.dtype)
            return
        kk = pl.program_id(2)
        tgt = acc_ref if use_acc else o_ref

        @pl.when(kk == 0)
        def _():
            tgt[...] = part + add_ref[...] if has_add else part

        @pl.when(kk > 0)
        def _():
            tgt[...] += part

        if use_acc:
            @pl.when(kk == nk - 1)
            def _():
                o_ref[...] = acc_ref[...].astype(o_ref.dtype)

    operands, in_specs = [a, b], [a_spec, b_spec]
    if has_add:
        operands.append(add)
        in_specs.append(pl.BlockSpec((tm, tn), lambda i, j, kk: (i, j)))
    aliases = {}
    if has_out:
        aliases = {len(operands): 0}
        operands.append(out)
        in_specs.append(pl.BlockSpec(memory_space=pl.ANY))
    return pl.pallas_call(
        body,
        out_shape=out_sds,
        grid=(m // tm, n // tn, nk),
        in_specs=in_specs,
        out_specs=o_spec,
        scratch_shapes=[pltpu.VMEM((tm, tn), F32)] if use_acc else [],
        input_output_aliases=aliases,
        compiler_params=pltpu.CompilerParams(dimension_semantics=("parallel", "parallel", "arbitrary")),
        name=name,
    )(*operands)


def _rms_fwd(name, x, g):
    t, d = x.shape
    tm = _tile(t, 512, 16)

    def body(x_ref, g_ref, o_ref):
        xv = x_ref[...]
        r = lax.rsqrt(jnp.mean(xv * xv, axis=-1, keepdims=True) + EPS)
        o_ref[...] = ((xv * r) * g_ref[...]).astype(o_ref.dtype)

    return pl.pallas_call(
        body,
        out_shape=_sds((t, d), BF16),
        grid=(t // tm,),
        in_specs=[pl.BlockSpec((tm, d), lambda i: (i, 0)), pl.BlockSpec((1, d), lambda i: (0, 0))],
        out_specs=pl.BlockSpec((tm, d), lambda i: (i, 0)),
        compiler_params=pltpu.CompilerParams(dimension_semantics=("parallel",)),
        name=name,
    )(x, g)


def _rms_bwd(name, x, dh, g, res):
    t, d = x.shape
    tm = _tile(t, 512, 16)

    def body(x_ref, dh_ref, g_ref, res_ref, dx_ref, dg_ref):
        xv = x_ref[...]
        r = lax.rsqrt(jnp.mean(xv * xv, axis=-1, keepdims=True) + EPS)
        xh = xv * r
        dhv = dh_ref[...]
        dxh = dhv * g_ref[...]
        dx_ref[...] = res_ref[...] + r * (dxh - xh * jnp.mean(dxh * xh, axis=-1, keepdims=True))

        @pl.when(pl.program_id(0) == 0)
        def _():
            dg_ref[...] = jnp.zeros_like(dg_ref)

        dg_ref[...] += jnp.sum(dhv * xh, axis=0, keepdims=True)

    row = pl.BlockSpec((tm, d), lambda i: (i, 0))
    vec = pl.BlockSpec((1, d), lambda i: (0, 0))
    return pl.pallas_call(
        body,
        out_shape=(_sds((t, d), F32), _sds((1, d), F32)),
        grid=(t // tm,),
        in_specs=[row, row, vec, row],
        out_specs=(row, vec),
        compiler_params=pltpu.CompilerParams(dimension_semantics=("arbitrary",)),
        name=name,
    )(x, dh, g, res)


def _final_loss(name, x, g, target):
    t, d = x.shape
    tm = _tile(t, 512, 16)

    def body(x_ref, g_ref, t_ref, dx_ref, dxb_ref, loss_ref, dg_ref):
        xv = x_ref[...]
        gv = g_ref[...]
        r = lax.rsqrt(jnp.mean(xv * xv, axis=-1, keepdims=True) + EPS)
        xh = xv * r
        err = xh * gv - t_ref[...]
        dy = err * (1.0 / d)
        dxh = dy * gv
        dx = r * (dxh - xh * jnp.mean(dxh * xh, axis=-1, keepdims=True))
        dx_ref[...] = dx
        dxb_ref[...] = dx.astype(dxb_ref.dtype)
        per_row = jnp.sum(err * err, axis=-1, keepdims=True) * (0.5 / d)

        @pl.when(pl.program_id(0) == 0)
        def _():
            dg_ref[...] = jnp.zeros_like(dg_ref)
            loss_ref[...] = jnp.zeros_like(loss_ref)

        dg_ref[...] += jnp.sum(dy * xh, axis=0, keepdims=True)
        loss_ref[...] += jnp.sum(per_row, axis=0, keepdims=True)

    row = pl.BlockSpec((tm, d), lambda i: (i, 0))
    vec = pl.BlockSpec((1, d), lambda i: (0, 0))
    return pl.pallas_call(
        body,
        out_shape=(_sds((t, d), F32), _sds((t, d), BF16), _sds((1, LANES), F32), _sds((1, d), F32)),
        grid=(t // tm,),
        in_specs=[row, vec, row],
        out_specs=(row, row, pl.BlockSpec((1, LANES), lambda i: (0, 0)), vec),
        compiler_params=pltpu.CompilerParams(dimension_semantics=("arbitrary",)),
        name=name,
    )(x, g, target)


def _shift_down(z, k):
    row = lax.broadcasted_iota(jnp.int32, z.shape, 0)
    return jnp.where(row >= k, pltpu.roll(z, k, axis=0), 0.0)


def _shift_up(z, k):
    s = z.shape[0]
    row = lax.broadcasted_iota(jnp.int32, z.shape, 0)
    return jnp.where(row < s - k, pltpu.roll(z, s - k, axis=0), 0.0)


def _conv3(z, w):
    return (w[2:3] * z + w[0:1] * _shift_down(z, 2)) + w[1:2] * _shift_down(z, 1)


def _conv3_t(dz, w):
    return (w[2:3] * dz + w[0:1] * _shift_up(dz, 2)) + w[1:2] * _shift_up(dz, 1)


def _conv_fwd(name, pc, w, batch, seq, tc):
    cw = w.shape[1]
    nct = cw // tc

    def body(pc_ref, w_ref, o_ref):
        cb = pc_ref[:, 0:tc]
        z = pc_ref[:, tc:2 * tc] * pc_ref[:, 2 * tc:3 * tc]
        o_ref[...] = (cb * _conv3(z, w_ref[...])).astype(o_ref.dtype)

    return pl.pallas_call(
        body,
        out_shape=_sds((batch * seq, cw), BF16),
        grid=(batch, nct),
        in_specs=[pl.BlockSpec((seq, 3 * tc), lambda b, j: (b, j)), pl.BlockSpec((3, tc), lambda b, j: (0, j))],
        out_specs=pl.BlockSpec((seq, tc), lambda b, j: (b, j)),
        compiler_params=pltpu.CompilerParams(dimension_semantics=("parallel", "parallel")),
        name=name,
    )(pc, w)


def _conv_bwd(name, da, pc, w, dproj, batch, seq, tc):
    cw = w.shape[1]
    nct = cw // tc

    def body(da_ref, pc_ref, w_ref, _, dpc_ref, dw_ref):
        wv = w_ref[...]
        cb = pc_ref[:, 0:tc]
        cc = pc_ref[:, tc:2 * tc]
        cin = pc_ref[:, 2 * tc:3 * tc]
        z = cc * cin
        dav = da_ref[...]
        du = dav * cb
        dz = _conv3_t(du, wv)
        dpc_ref[:, 0:tc] = (dav * _conv3(z, wv)).astype(dpc_ref.dtype)
        dpc_ref[:, tc:2 * tc] = (dz * cin).astype(dpc_ref.dtype)
        dpc_ref[:, 2 * tc:3 * tc] = (dz * cc).astype(dpc_ref.dtype)

        @pl.when(pl.program_id(1) == 0)
        def _():
            dw_ref[...] = jnp.zeros_like(dw_ref)

        dw_ref[0:1, :] += jnp.sum(du * _shift_down(z, 2), axis=0, keepdims=True)
        dw_ref[1:2, :] += jnp.sum(du * _shift_down(z, 1), axis=0, keepdims=True)
        dw_ref[2:3, :] += jnp.sum(du * z, axis=0, keepdims=True)

    return pl.pallas_call(
        body,
        out_shape=(_sds(dproj.shape, dproj.dtype), _sds((3, cw), F32)),
        grid=(nct, batch),
        in_specs=[
            pl.BlockSpec((seq, tc), lambda j, b: (b, j)),
            pl.BlockSpec((seq, 3 * tc), lambda j, b: (b, j)),
            pl.BlockSpec((3, tc), lambda j, b: (0, j)),
            pl.BlockSpec(memory_space=pl.ANY),
        ],
        out_specs=(pl.BlockSpec((seq, 3 * tc), lambda j, b: (b, j)), pl.BlockSpec((3, tc), lambda j, b: (0, j))),
        input_output_aliases={3: 0},
        compiler_params=pltpu.CompilerParams(dimension_semantics=("parallel", "arbitrary")),
        name=name,
    )(da, pc, w, dproj)


def _ffn_up_act(name, h2, w_up, w, batch, seq, tc):
    d = h2.shape[1]
    fh = w.shape[1] // 2
    nf = fh // tc

    def body(h_ref, ma_ref, mb_ref, wa_ref, wb_ref, o_ref, ua_ref, ub_ref):
        hv = h_ref[...]
        ua = _dot(hv, ma_ref[...], "nn")
        ub = _dot(hv, mb_ref[...], "nn")
        ua_ref[...] = ua.astype(ua_ref.dtype)
        ub_ref[...] = ub.astype(ub_ref.dtype)
        a = _conv3(ua, wa_ref[...])
        b = _conv3(ub, wb_ref[...])
        o_ref[...] = (a * jax.nn.sigmoid(a) * b).astype(o_ref.dtype)

    act = pl.BlockSpec((seq, tc), lambda b, j: (b, j))
    shape = _sds((batch * seq, fh), BF16)
    return pl.pallas_call(
        body,
        out_shape=(shape, shape, shape),
        grid=(batch, nf),
        in_specs=[
            pl.BlockSpec((seq, d), lambda b, j: (b, 0)),
            pl.BlockSpec((d, tc), lambda b, j: (0, j)),
            pl.BlockSpec((d, tc), lambda b, j: (0, nf + j)),
            pl.BlockSpec((3, tc), lambda b, j: (0, j)),
            pl.BlockSpec((3, tc), lambda b, j: (0, nf + j)),
        ],
        out_specs=(act, act, act),
        compiler_params=pltpu.CompilerParams(dimension_semantics=("parallel", "parallel")),
        name=name,
    )(h2, w_up, w_up, w, w)


def _ffn_bwd(name, dx, w_down, ua, ub, w, batch, seq, tc):
    d = dx.shape[1]
    fh = w.shape[1] // 2
    nf = fh // tc

    def body(dx_ref, md_ref, ua_ref, ub_ref, wa_ref, wb_ref, dua_ref, dub_ref, dw_ref):
        j = pl.program_id(1)
        uav, ubv, wa, wb = ua_ref[...].astype(F32), ub_ref[...].astype(F32), wa_ref[...], wb_ref[...]
        dhv = _dot(dx_ref[...].astype(BF16), md_ref[...], "nt")
        a = _conv3(uav, wa)
        b = _conv3(ubv, wb)
        sg = jax.nn.sigmoid(a)
        da = dhv * b * (sg * (1.0 + a * (1.0 - sg)))
        db = dhv * (a * sg)
        dua_ref[...] = _conv3_t(da, wa).astype(dua_ref.dtype)
        dub_ref[...] = _conv3_t(db, wb).astype(dub_ref.dtype)

        @pl.when((pl.program_id(0) == 0) & (j == 0))
        def _():
            dw_ref[...] = jnp.zeros_like(dw_ref)

        for off, dv, uv in ((0, da, uav), (fh, db, ubv)):
            cols = pl.ds(pl.multiple_of(off + j * tc, LANES), tc)
            dw_ref[0:1, cols] += jnp.sum(dv * _shift_down(uv, 2), axis=0, keepdims=True)
            dw_ref[1:2, cols] += jnp.sum(dv * _shift_down(uv, 1), axis=0, keepdims=True)
            dw_ref[2:3, cols] += jnp.sum(dv * uv, axis=0, keepdims=True)

    act = pl.BlockSpec((seq, tc), lambda b, j: (b, j))
    shape = _sds((batch * seq, fh), BF16)
    return pl.pallas_call(
        body,
        out_shape=(shape, shape, _sds((3, 2 * fh), F32)),
        grid=(batch, nf),
        in_specs=[
            pl.BlockSpec((seq, d), lambda b, j: (b, 0)),
            pl.BlockSpec((tc, d), lambda b, j: (j, 0)),
            act,
            act,
            pl.BlockSpec((3, tc), lambda b, j: (0, j)),
            pl.BlockSpec((3, tc), lambda b, j: (0, nf + j)),
        ],
        out_specs=(act, act, pl.BlockSpec((3, 2 * fh), lambda b, j: (0, 0))),
        compiler_params=pltpu.CompilerParams(dimension_semantics=("arbitrary", "arbitrary")),
        name=name,
    )(dx, w_down, ua, ub, w, w)


def _merge_fwd(name, ycat, gl, bg):
    t, d2 = ycat.shape
    d = d2 // 2
    tm = _tile(t, 256, 16)

    def body(y_ref, gl_ref, bg_ref, o_ref):
        g = jax.nn.sigmoid(gl_ref[...] + bg_ref[...])
        prod = g * y_ref[...]
        o_ref[...] = (prod[:, 0:d] + prod[:, d:d2]).astype(o_ref.dtype)

    row = pl.BlockSpec((tm, d2), lambda i: (i, 0))
    return pl.pallas_call(
        body,
        out_shape=_sds((t, d), BF16),
        grid=(t // tm,),
        in_specs=[row, row, pl.BlockSpec((1, d2), lambda i: (0, 0))],
        out_specs=pl.BlockSpec((tm, d), lambda i: (i, 0)),
        compiler_params=pltpu.CompilerParams(dimension_semantics=("parallel",)),
        name=name,
    )(ycat, gl, bg)


def _merge_bwd(name, dm, ycat, gl, bg, width, gl_off):
    t, d2 = ycat.shape
    d = d2 // 2
    tm = _tile(t, 512, 16)
    wb = math.gcd(gl_off, d)
    nw = d // wb

    def body(dm_ref, y_ref, gl_ref, bg_ref, dgl_ref, dy_ref, dbg_ref):
        g = jax.nn.sigmoid(gl_ref[...] + bg_ref[...])
        dmv = dm_ref[...]
        dgl = dmv * y_ref[...] * (g * (1.0 - g))
        dgl_ref[...] = dgl.astype(dgl_ref.dtype)
        dy_ref[...] = (dmv * g).astype(dy_ref.dtype)

        @pl.when(pl.program_id(2) == 0)
        def _():
            dbg_ref[...] = jnp.zeros_like(dbg_ref)

        dbg_ref[...] += jnp.sum(dgl, axis=0, keepdims=True)

    half = pl.BlockSpec((tm, wb), lambda h, j, i: (i, h * nw + j))
    vec = pl.BlockSpec((1, wb), lambda h, j, i: (0, h * nw + j))
    return pl.pallas_call(
        body,
        out_shape=(_sds((t, width), BF16), _sds((t, d2), BF16), _sds((1, d2), F32)),
        grid=(2, nw, t // tm),
        in_specs=[pl.BlockSpec((tm, wb), lambda h, j, i: (i, j)), half, half, vec],
        out_specs=(pl.BlockSpec((tm, wb), lambda h, j, i: (i, gl_off // wb + h * nw + j)), half, vec),
        compiler_params=pltpu.CompilerParams(dimension_semantics=("parallel", "parallel", "arbitrary")),
        name=name,
    )(dm, ycat, gl, bg)


def _log_sigmoid(z):
    return jnp.minimum(z, 0.0) - jnp.log1p(jnp.exp(-jnp.abs(z)))


def _forget_fwd(name, fl, bf, batch, seq):
    def body(fl_ref, bf_ref, o_ref):
        lf = _log_sigmoid(fl_ref[:, 0:LANES] + bf_ref[:, 0:LANES])
        acc = lf.T[0:HEADS, :]
        lane = lax.broadcasted_iota(jnp.int32, acc.shape, 1)
        k = 1
        while k < seq:
            acc = acc + jnp.where(lane >= k, pltpu.roll(acc, k, axis=1), 0.0)
            k *= 2
        o_ref[...] = acc

    return pl.pallas_call(
        body,
        out_shape=_sds((batch, HEADS, seq), F32),
        grid=(batch,),
        in_specs=[pl.BlockSpec((seq, F_PAD), lambda b: (b, 0)), pl.BlockSpec((1, F_PAD), lambda b: (0, 0))],
        out_specs=pl.BlockSpec((None, HEADS, seq), lambda b: (b, 0, 0)),
        compiler_params=pltpu.CompilerParams(dimension_semantics=("parallel",)),
        name=name,
    )(fl, bf)


def _forget_bwd(name, d_key, d_query, fl, bf, dproj, f_off, batch, seq):
    nfb = F_PAD // LANES

    def body(dk_ref, dq_ref, fl_ref, bf_ref, _, df_ref, dbf_ref):
        jj = pl.program_id(1)
        key_t = jnp.concatenate([dk_ref[...], jnp.zeros((LANES - HEADS, seq), F32)], axis=0).T
        acc = dq_ref[...] - key_t
        row = lax.broadcasted_iota(jnp.int32, acc.shape, 0)
        k = 1
        while k < seq:
            acc = acc + jnp.where(row < seq - k, pltpu.roll(acc, seq - k, axis=0), 0.0)
            k *= 2
        z = fl_ref[:, 0:LANES] + bf_ref[:, 0:LANES]
        col = lax.broadcasted_iota(jnp.int32, acc.shape, 1)
        df = jnp.where(col < HEADS, acc * jax.nn.sigmoid(-z), 0.0)
        df = jnp.where(jj == 0, df, 0.0)
        df_ref[...] = df.astype(df_ref.dtype)

        @pl.when((pl.program_id(0) == 0) & (jj == 0))
        def _():
            dbf_ref[...] = jnp.zeros_like(dbf_ref)

        dbf_ref[...] += jnp.sum(df, axis=0, keepdims=True)

    return pl.pallas_call(
        body,
        out_shape=(_sds(dproj.shape, dproj.dtype), _sds((1, LANES), F32)),
        grid=(batch, nfb),
        in_specs=[
            pl.BlockSpec((None, HEADS, seq), lambda b, j: (b, 0, 0)),
            pl.BlockSpec((seq, LANES), lambda b, j: (b, 0)),
            pl.BlockSpec((seq, F_PAD), lambda b, j: (b, 0)),
            pl.BlockSpec((1, F_PAD), lambda b, j: (0, 0)),
            pl.BlockSpec(memory_space=pl.ANY),
        ],
        out_specs=(pl.BlockSpec((seq, LANES), lambda b, j: (b, f_off // LANES + j)),
                   pl.BlockSpec((1, LANES), lambda b, j: (0, 0))),
        input_output_aliases={4: 0},
        compiler_params=pltpu.CompilerParams(dimension_semantics=("arbitrary", "arbitrary")),
        name=name,
    )(d_key, d_query, fl, bf, dproj)


def _dot(a, b, mode):
    return lax.dot_general(a, b, _DIMS[mode], preferred_element_type=F32)


def _attn_fwd(name, qkv, frow, batch, seq, tq):
    nq = seq // tq
    scale = 1.0 / math.sqrt(HEAD_DIM)

    def body(q_ref, k_ref, v_ref, f_ref, o_ref, lse_ref):
        i = pl.program_id(2)
        lane = lax.broadcasted_iota(jnp.int32, (1, LANES), 1)
        lo = lane < HEAD_DIM
        qs = q_ref[...] * scale
        qh = (jnp.where(lo, qs, 0.0).astype(BF16), jnp.where(lo, 0.0, qs).astype(BF16))
        row = lax.broadcasted_iota(jnp.int32, (tq, tq), 0)
        col = lax.broadcasted_iota(jnp.int32, (tq, tq), 1)

        def step(j, carry, diag):
            m0, l0, m1, l1, acc = carry
            start = pl.multiple_of(j * tq, tq)
            kj = k_ref[pl.ds(start, tq), :]
            vj = v_ref[pl.ds(start, tq), :]
            ms, ls, pvs, alphas = [], [], [], []
            for h, (m_old, l_old) in enumerate(((m0, l0), (m1, l1))):
                s = _dot(qh[h], kj, "nt") - f_ref[h:h + 1, pl.ds(start, tq)]
                if diag:
                    s = jnp.where(col <= row, s, NEG_BIG)
                m_new = jnp.maximum(m_old, jnp.max(s, axis=1, keepdims=True))
                p = jnp.exp(s - m_new)
                alpha = jnp.exp(m_old - m_new)
                ls.append(alpha * l_old + jnp.sum(p, axis=1, keepdims=True))
                ms.append(m_new)
                alphas.append(alpha)
                vh = jnp.where(lo, vj, 0.0) if h == 0 else jnp.where(lo, 0.0, vj)
                pvs.append(_dot(p.astype(BF16), vh.astype(BF16), "nn"))
            acc = acc * jnp.where(lo, alphas[0], alphas[1]) + (pvs[0] + pvs[1])
            return ms[0], ls[0], ms[1], ls[1], acc

        neg = jnp.full((tq, 1), NEG_BIG, F32)
        zero = jnp.zeros((tq, 1), F32)
        init = (neg, zero, neg, zero, jnp.zeros((tq, LANES), F32))
        carry = lax.fori_loop(0, i, lambda j, c: step(j, c, False), init)
        m0, l0, m1, l1, acc = step(i, carry, True)
        o_ref[...] = (acc / jnp.where(lo, l0, l1)).astype(o_ref.dtype)
        lse_ref[:, 0:1] = m0 + jnp.log(l0)
        lse_ref[:, 1:2] = m1 + jnp.log(l1)

    return pl.pallas_call(
        body,
        out_shape=(_sds((batch * seq, ATTN_WIDTH), BF16), _sds((HEAD_PAIRS, batch * seq, 2), F32)),
        grid=(batch, HEAD_PAIRS, nq),
        in_specs=[
            pl.BlockSpec((tq, LANES), lambda b, hp, i: (b * nq + i, 3 * hp)),
            pl.BlockSpec((seq, LANES), lambda b, hp, i: (b, 3 * hp + 1)),
            pl.BlockSpec((seq, LANES), lambda b, hp, i: (b, 3 * hp + 2)),
            pl.BlockSpec((None, None, 2, seq), lambda b, hp, i: (b, hp, 0, 0)),
        ],
        out_specs=(
            pl.BlockSpec((tq, LANES), lambda b, hp, i: (b * nq + i, hp)),
            pl.BlockSpec((None, tq, 2), lambda b, hp, i: (hp, b * nq + i, 0)),
        ),
        compiler_params=pltpu.CompilerParams(dimension_semantics=("parallel", "parallel", "parallel")),
        name=name,
    )(qkv, qkv, qkv, frow)


def _attn_bwd(name, qkv, do, o, lse, frow, dproj, qkv_off, batch, seq, tq):
    nq = seq // tq
    scale = 1.0 / math.sqrt(HEAD_DIM)

    def body(q_ref, k_ref, v_ref, do_ref, o_ref, lse_ref, f_ref, _, dqkv_ref, df_ref, drow_ref,
             dq_acc, dk_acc, dv_acc, df_acc):
        j = pl.program_id(2)
        lane = lax.broadcasted_iota(jnp.int32, (1, LANES), 1)
        lo = lane < HEAD_DIM
        masks = (lo, jnp.logical_not(lo))
        row = lax.broadcasted_iota(jnp.int32, (tq, tq), 0)
        col = lax.broadcasted_iota(jnp.int32, (tq, tq), 1)

        @pl.when(j == 0)
        def _():
            dq_acc[...] = jnp.zeros_like(dq_acc)
            drow_ref[...] = jnp.zeros_like(drow_ref)

        dk_acc[...] = jnp.zeros_like(dk_acc)
        dv_acc[...] = jnp.zeros_like(dv_acc)
        df_acc[...] = jnp.zeros_like(df_acc)
        kj = k_ref[...]
        vj = v_ref[...]
        kstart = pl.multiple_of(j * tq, tq)
        kh = tuple(jnp.where(mk, kj, 0.0).astype(BF16) for mk in masks)

        def step(i, diag):
            start = pl.multiple_of(i * tq, tq)
            rows = pl.ds(start, tq)
            qi = q_ref[rows, :] * scale
            doi = do_ref[rows, :]
            prod = doi.astype(F32) * o_ref[rows, :].astype(F32)
            lse_i = lse_ref[rows, :]
            dq_i = jnp.zeros((tq, LANES), F32)
            for h, mk in enumerate(masks):
                q_h = jnp.where(mk, qi, 0.0).astype(BF16)
                do_h = jnp.where(mk, doi, 0.0).astype(BF16)
                delta = jnp.sum(jnp.where(mk, prod, 0.0), axis=1, keepdims=True)
                s = _dot(q_h, kj, "nt") - f_ref[h:h + 1, pl.ds(kstart, tq)]
                p = jnp.exp(s - lse_i[:, h:h + 1])
                if diag:
                    p = jnp.where(col <= row, p, 0.0)
                ds = p * (_dot(do_h, vj, "nt") - delta)
                df_acc[h:h + 1, :] += jnp.sum(ds, axis=0, keepdims=True)
                drow_ref[rows, h:h + 1] += jnp.sum(ds, axis=1, keepdims=True)
                dsb = ds.astype(BF16)
                dv_acc[...] += _dot(p.astype(BF16), do_h, "tn")
                dk_acc[...] += _dot(dsb, q_h, "tn")
                dq_i = dq_i + _dot(dsb, kh[h], "nn")
            dq_acc[rows, :] += dq_i

        step(j, True)
        lax.fori_loop(j + 1, nq, lambda i, c: (step(i, False), c)[1], 0)
        dqkv_ref[:, 0:LANES] = (dq_acc[pl.ds(kstart, tq), :] * scale).astype(dqkv_ref.dtype)
        dqkv_ref[:, LANES:2 * LANES] = dk_acc[...].astype(dqkv_ref.dtype)
        dqkv_ref[:, 2 * LANES:3 * LANES] = dv_acc[...].astype(dqkv_ref.dtype)
        df_ref[...] = df_acc[...]

    full = lambda c: pl.BlockSpec((seq, LANES), lambda b, hp, j: (b, c(hp)))
    blk = lambda c: pl.BlockSpec((tq, LANES), lambda b, hp, j: (b * nq + j, c(hp)))
    return pl.pallas_call(
        body,
        out_shape=(_sds(dproj.shape, dproj.dtype), _sds((batch, HEAD_PAIRS, 2, seq), F32),
                   _sds((HEAD_PAIRS, batch * seq, 2), F32)),
        grid=(batch, HEAD_PAIRS, nq),
        in_specs=[
            full(lambda hp: 3 * hp),
            blk(lambda hp: 3 * hp + 1),
            blk(lambda hp: 3 * hp + 2),
            full(lambda hp: hp),
            full(lambda hp: hp),
            pl.BlockSpec((None, seq, 2), lambda b, hp, j: (hp, b, 0)),
            pl.BlockSpec((None, None, 2, seq), lambda b, hp, j: (b, hp, 0, 0)),
            pl.BlockSpec(memory_space=pl.ANY),
        ],
        out_specs=(
            pl.BlockSpec((tq, 3 * LANES), lambda b, hp, j: (b * nq + j, qkv_off // (3 * LANES) + hp)),
            pl.BlockSpec((None, None, 2, tq), lambda b, hp, j: (b, hp, 0, j)),
            pl.BlockSpec((None, seq, 2), lambda b, hp, j: (hp, b, 0)),
        ),
        scratch_shapes=[
            pltpu.VMEM((seq, LANES), F32),
            pltpu.VMEM((tq, LANES), F32),
            pltpu.VMEM((tq, LANES), F32),
            pltpu.VMEM((2, tq), F32),
        ],
        input_output_aliases={7: 0},
        compiler_params=pltpu.CompilerParams(dimension_semantics=("parallel", "parallel", "arbitrary")),
        name=name,
    )(qkv, qkv, qkv, do, o, lse, frow, dproj)


def _mesh_place():
    x, y, c = lax.axis_index("x"), lax.axis_index("y"), lax.axis_index("c")
    chips = [(1 - x, y), (x, 1 - y), (1 - x, 1 - y)]
    return x, y, c, chips


def _hbm_specs(n):
    return [pl.BlockSpec(memory_space=pl.ANY)] * n


def _half(shape2d, axis, which):
    size = shape2d[axis] // 2
    sl = pl.ds(pl.multiple_of(which * size, 16 if axis == 0 else LANES), size)
    return (sl, slice(None)) if axis == 0 else (slice(None), sl)


def _gather_weights(bigs, axes, smalls):
    nb, ns = len(bigs), len(smalls)
    arrays = list(bigs) + list(smalls)
    n = nb + ns

    def body(*refs):
        ins, outs = refs[:n], refs[n:2 * n]
        send_sems, recv_sems = refs[2 * n:]
        x, y, c, chips = _mesh_place()
        me = 2 * x + y
        sibling = (x, y, 1 - c)

        def half(a, which):
            return _half(arrays[a].shape, axes[a], which)

        def copy(a, k, src, dst, to):
            return pltpu.make_async_remote_copy(src_ref=src, dst_ref=dst, send_sem=send_sems.at[a, k],
                                                recv_sem=recv_sems.at[a, k], device_id=to, device_id_type=MESH)

        sends = []
        for a in range(n):
            for j, chip in enumerate(chips):
                if a < nb:
                    cp = copy(a, j, ins[a].at[half(a, c)], outs[a].at[(me,) + half(a, c)], (*chip, c))
                else:
                    cp = copy(a, j, ins[a], outs[a].at[me], (*chip, c))
                cp.start()
                sends.append(cp)
        for a in range(nb):
            for j, (px, py) in enumerate(chips):
                blk = outs[a].at[(2 * px + py,) + half(a, c)]
                copy(a, j, blk, blk, (px, py, c)).wait_recv()
                fwd = copy(a, 3 + j, blk, blk, sibling)
                fwd.start()
                sends.append(fwd)
        for a in range(nb, n):
            for j, (px, py) in enumerate(chips):
                blk = outs[a].at[2 * px + py]
                copy(a, j, blk, blk, (px, py, c)).wait_recv()
        for a in range(nb):
            for j, (px, py) in enumerate(chips):
                blk = outs[a].at[(2 * px + py,) + half(a, 1 - c)]
                copy(a, 3 + j, blk, blk, sibling).wait_recv()
        for cp in sends:
            cp.wait_send()

    outs = pl.pallas_call(
        body,
        out_shape=tuple(_sds((N_CHIPS,) + a.shape, a.dtype) for a in arrays),
        in_specs=_hbm_specs(n),
        out_specs=tuple(_hbm_specs(n)),
        scratch_shapes=[pltpu.SemaphoreType.DMA((n, 6)), pltpu.SemaphoreType.DMA((n, 6))],
        name="gather_weights",
    )(*arrays)
    me = 2 * lax.axis_index("x") + lax.axis_index("y")
    return tuple(lax.dynamic_update_index_in_dim(o, a, me, 0) for o, a in zip(outs, arrays))


def _gather_small(v):
    m_per, ncol = v.shape

    def body(x_ref, out_ref, send_sems, recv_sems, local_sem):
        x, y, c, chips = _mesh_place()
        me, sibling = (x, y, c), (x, y, 1 - c)

        def rows(px, py, pc):
            return out_ref.at[pl.ds((4 * px + 2 * py + pc) * m_per, m_per), :]

        def copy(k, block, to, src=None):
            return pltpu.make_async_remote_copy(src_ref=rows(*block) if src is None else src, dst_ref=rows(*block),
                                                send_sem=send_sems.at[k], recv_sem=recv_sems.at[k],
                                                device_id=to, device_id_type=MESH)

        mine = pltpu.make_async_copy(x_ref, rows(*me), local_sem)
        mine.start()
        first = [copy(0, me, sibling, src=x_ref)]
        first += [copy(1 + j, me, (*chip, c), src=x_ref) for j, chip in enumerate(chips)]
        for cp in first:
            cp.start()
        passed = [copy(4 + j, (*chip, c), sibling) for j, chip in enumerate(chips)]
        for j, chip in enumerate(chips):
            copy(1 + j, (*chip, c), me).wait_recv()
            passed[j].start()
        copy(0, sibling, me).wait_recv()
        for j, chip in enumerate(chips):
            copy(4 + j, (*chip, 1 - c), me).wait_recv()
        for cp in first + passed:
            cp.wait_send()
        mine.wait()

    return pl.pallas_call(
        body,
        out_shape=_sds((N_DEV * m_per, ncol), v.dtype),
        in_specs=[pl.BlockSpec(memory_space=pltpu.VMEM)],
        out_specs=pl.BlockSpec(memory_space=pltpu.VMEM),
        scratch_shapes=[pltpu.SemaphoreType.DMA((7,)), pltpu.SemaphoreType.DMA((7,)), pltpu.SemaphoreType.DMA],
        name="gather_small",
    )(v)


def _half_shape(shape2d, axis):
    return (shape2d[0] // 2, shape2d[1]) if axis == 0 else (shape2d[0], shape2d[1] // 2)


def _exchange_sibling(grads, axes):
    n = len(grads)

    def body(*refs):
        ins, outs = refs[:n], refs[n:2 * n]
        send_sems, recv_sems = refs[2 * n:]
        x, y, c, _ = _mesh_place()
        copies = []
        for a in range(n):
            src = ins[a].at[(slice(None),) + _half(grads[a].shape[1:], axes[a], 1 - c)]
            cp = pltpu.make_async_remote_copy(src_ref=src, dst_ref=outs[a], send_sem=send_sems.at[a],
                                              recv_sem=recv_sems.at[a], device_id=(x, y, 1 - c), device_id_type=MESH)
            cp.start()
            copies.append(cp)
        for cp in copies:
            cp.wait()

    return pl.pallas_call(
        body,
        out_shape=tuple(_sds((N_CHIPS,) + _half_shape(g.shape[1:], ax), g.dtype) for g, ax in zip(grads, axes)),
        in_specs=_hbm_specs(n),
        out_specs=tuple(_hbm_specs(n)),
        scratch_shapes=[pltpu.SemaphoreType.DMA((n,)), pltpu.SemaphoreType.DMA((n,))],
        name="exchange_sibling",
    )(*grads)


def _exchange_chips(sums):
    n = len(sums)

    def body(*refs):
        ins, outs = refs[:n], refs[n:2 * n]
        send_sems, recv_sems = refs[2 * n:]
        _, _, c, chips = _mesh_place()
        copies = []
        for a in range(n):
            for j, (px, py) in enumerate(chips):
                cp = pltpu.make_async_remote_copy(src_ref=ins[a].at[2 * px + py], dst_ref=outs[a].at[j],
                                                  send_sem=send_sems.at[a, j], recv_sem=recv_sems.at[a, j],
                                                  device_id=(px, py, c), device_id_type=MESH)
                cp.start()
                copies.append(cp)
        for cp in copies:
            cp.wait()

    return pl.pallas_call(
        body,
        out_shape=tuple(_sds((3,) + s.shape[1:], s.dtype) for s in sums),
        in_specs=_hbm_specs(n),
        out_specs=tuple(_hbm_specs(n)),
        scratch_shapes=[pltpu.SemaphoreType.DMA((n, 3)), pltpu.SemaphoreType.DMA((n, 3))],
        name="exchange_chips",
    )(*sums)


def _share_sibling(shards, axes):
    n = len(shards)

    def body(*refs):
        ins, outs = refs[:n], refs[n:2 * n]
        send_sems, recv_sems = refs[2 * n:]
        x, y, c, _ = _mesh_place()
        started = []
        for a in range(n):
            mine = _half(shards[a].shape, axes[a], c)
            theirs = _half(shards[a].shape, axes[a], 1 - c)
            cp = pltpu.make_async_remote_copy(src_ref=ins[a].at[mine], dst_ref=outs[a].at[mine],
                                              send_sem=send_sems.at[a], recv_sem=recv_sems.at[a],
                                              device_id=(x, y, 1 - c), device_id_type=MESH)
            cp.start()
            arrival = pltpu.make_async_remote_copy(src_ref=ins[a].at[theirs], dst_ref=outs[a].at[theirs],
                                                   send_sem=send_sems.at[a], recv_sem=recv_sems.at[a],
                                                   device_id=(x, y, 1 - c), device_id_type=MESH)
            started.append((cp, arrival))
        for cp, arrival in started:
            arrival.wait_recv()
            cp.wait_send()

    return pl.pallas_call(
        body,
        out_shape=tuple(_sds(s.shape, s.dtype) for s in shards),
        in_specs=_hbm_specs(n),
        out_specs=tuple(_hbm_specs(n)),
        scratch_shapes=[pltpu.SemaphoreType.DMA((n,)), pltpu.SemaphoreType.DMA((n,))],
        input_output_aliases={a: a for a in range(n)},
        name="share_sibling",
    )(*shards)


def _pair_sum(name, place, g, got, axis):
    hr, hc = got.shape[1:]

    def body(place_ref, g_ref, got_ref, o_ref):
        o_ref[...] = (g_ref[...] + got_ref[...]).astype(o_ref.dtype)

    blk = (None, hr, hc)
    mine = (lambda k, pr: (k, pr[1], 0)) if axis == 0 else (lambda k, pr: (k, 0, pr[1]))
    return pl.pallas_call(
        body,
        out_shape=_sds((N_CHIPS, hr, hc), BF16),
        grid_spec=pltpu.PrefetchScalarGridSpec(
            num_scalar_prefetch=1,
            grid=(N_CHIPS,),
            in_specs=[pl.BlockSpec(blk, mine), pl.BlockSpec(blk, lambda k, pr: (k, 0, 0))],
            out_specs=pl.BlockSpec(blk, lambda k, pr: (k, 0, 0)),
        ),
        compiler_params=pltpu.CompilerParams(dimension_semantics=("parallel",)),
        name=name,
    )(place, g, got)


def _chip_sum(name, place, g, got, arrivals, axis):
    _, r, cdim = g.shape
    hr, hc = got.shape[1:]

    def body(place_ref, g_ref, got_ref, arr_ref, o_ref):
        acc = g_ref[...] + got_ref[...]
        for j in range(3):
            acc = acc + arr_ref[j].astype(F32)
        o_ref[...] = acc

    blk = (None, hr, hc)
    mine = (lambda i, pr: (pr[0], pr[1], 0)) if axis == 0 else (lambda i, pr: (pr[0], 0, pr[1]))
    dest = (lambda i, pr: (pr[1], 0)) if axis == 0 else (lambda i, pr: (0, pr[1]))
    return pl.pallas_call(
        body,
        out_shape=_sds((r, cdim), F32),
        grid_spec=pltpu.PrefetchScalarGridSpec(
            num_scalar_prefetch=1,
            grid=(1,),
            in_specs=[
                pl.BlockSpec(blk, mine),
                pl.BlockSpec(blk, lambda i, pr: (pr[0], 0, 0)),
                pl.BlockSpec((3, hr, hc), lambda i, pr: (0, 0, 0)),
            ],
            out_specs=pl.BlockSpec((hr, hc), dest),
        ),
        compiler_params=pltpu.CompilerParams(dimension_semantics=("arbitrary",)),
        name=name,
    )(place, g, got, arrivals)


def _device_sum(name, gathered):
    m_per = gathered.shape[0] // N_DEV

    def body(g_ref, o_ref):
        acc = g_ref[0:m_per, :]
        for dev in range(1, N_DEV):
            acc = acc + g_ref[dev * m_per:(dev + 1) * m_per, :]
        o_ref[...] = acc

    return pl.pallas_call(body, out_shape=_sds((m_per, gathered.shape[1]), F32), name=name)(gathered)


def _adamw(name, w, g, m, v):
    r, cdim = w.shape
    if r % 8 == 0:
        tr, tcol = _tile(r, 256, 8), cdim
    else:
        tr, tcol = r, (_tile(cdim, 256, LANES) if cdim % LANES == 0 else cdim)
    bc1 = 1.0 - ADAM_B1 ** ADAM_STEP
    bc2 = 1.0 - ADAM_B2 ** ADAM_STEP

    def body(w_ref, g_ref, m_ref, v_ref, d_ref, nm_ref, nv_ref):
        gv = g_ref[...]
        nm = ADAM_B1 * m_ref[...] + (1.0 - ADAM_B1) * gv
        nv = ADAM_B2 * v_ref[...] + (1.0 - ADAM_B2) * (gv * gv)
        d_ref[...] = -ADAM_LR * ((nm / bc1) / (jnp.sqrt(nv / bc2) + ADAM_EPS) + ADAM_WD * w_ref[...])
        nm_ref[...] = nm
        nv_ref[...] = nv

    blk = pl.BlockSpec((tr, tcol), lambda i, j: (i, j))
    shape = _sds((r, cdim), F32)
    return pl.pallas_call(
        body,
        out_shape=(shape, shape, shape),
        grid=(r // tr, cdim // tcol),
        in_specs=[blk] * 4,
        out_specs=(blk, blk, blk),
        compiler_params=pltpu.CompilerParams(dimension_semantics=("parallel", "parallel")),
        name=name,
    )(w, g, m, v)


def _cat_cols(g):
    return jnp.transpose(g, (1, 0, 2)).reshape(g.shape[1], N_CHIPS * g.shape[2])


def _split_cols(a):
    r, c4 = a.shape
    return jnp.transpose(a.reshape(r, N_CHIPS, c4 // N_CHIPS), (1, 0, 2))


def _local_step(x, target, w_int, w_oc, w_oa, w_o, w_up, w_down, cmw, cfw, g1, b_f, b_gate, g2, gf):
    batch, seq, d = x.shape
    t = batch * seq
    cw = d // 2
    fh = w_down.shape[0]
    tc = LANES
    nct = cw // tc
    tq = min(512, seq)
    pc_w, qkv_w, gl_w = 3 * cw, 3 * ATTN_WIDTH, 2 * d
    qkv_off, gl_off, f_off = pc_w, pc_w + qkv_w, pc_w + qkv_w + gl_w
    width = f_off + F_PAD
    f_col = pc_w + qkv_w

    w_pc = w_int[:pc_w].reshape(3, nct, tc, d).transpose(1, 0, 2, 3).reshape(pc_w, d)
    w_qkv = w_int[pc_w:f_col].reshape(3, HEAD_PAIRS, LANES, d).transpose(1, 0, 2, 3).reshape(qkv_w, d)
    w_f = jnp.pad(w_int[f_col:f_col + HEADS], ((0, F_PAD - HEADS), (0, 0)))
    w_inp = jnp.concatenate([w_pc, w_qkv, w_int[f_col + HEADS:], w_f], axis=0)
    bf_pad = jnp.pad(b_f, ((0, 0), (0, F_PAD - HEADS)))

    x2d = x.reshape(t, d)
    tgt2d = target.reshape(t, d)

    h1 = _rms_fwd("norm_mix", x2d, g1)
    pc = _mm("proj_conv", h1, w_inp, "nt", F32, m=t, n=pc_w, k=d, b_roff=0)
    qkv = _mm("proj_qkv", h1, w_inp, "nt", BF16, m=t, n=qkv_w, k=d, b_roff=qkv_off)
    gl = _mm("proj_gate", h1, w_inp, "nt", F32, m=t, n=gl_w, k=d, b_roff=gl_off)
    fl = _mm("proj_forget", h1, w_inp, "nt", F32, m=t, n=F_PAD, k=d, b_roff=f_off)
    a_c = _conv_fwd("conv_mix", pc, cmw, batch, seq, tc)
    f_cum = _forget_fwd("forget_cumsum", fl, bf_pad, batch, seq)
    frow = f_cum.reshape(batch, HEAD_PAIRS, 2, seq)
    o, lse = _attn_fwd("attn_fwd", qkv, frow, batch, seq, tq)
    ycat = _mm("out_conv", a_c, w_oc, "nn", F32, m=t, n=d, k=cw, o_off=0, o_width=2 * d)
    ycat = _mm("out_attn", o, w_oa, "nn", F32, m=t, n=d, k=ATTN_WIDTH, out=ycat, o_off=d)
    mg = _merge_fwd("gate_merge", ycat, gl, b_gate)
    x2 = _mm("mix_out", mg, w_o, "nn", F32, m=t, n=d, k=d, add=x2d)
    h2 = _rms_fwd("norm_ffn", x2, g2)
    tcf = min(2 * LANES, fh)
    hmid, ua, ub = _ffn_up_act("ffn_up_act", h2, _cat_cols(w_up), cfw, batch, seq, tcf)
    x3 = _mm("ffn_down", hmid, w_down, "nn", F32, m=t, n=d, k=fh, add=x2, tk=4096)

    dx3, dx3b, loss_row, d_gf = _final_loss("final_loss", x3, gf.reshape(1, d), tgt2d)
    dw_down = _mm("dw_down", hmid, dx3b, "tn", F32, m=fh, n=d, k=t, tm=1408, tk=512)
    du_a, du_b, d_cfw = _ffn_bwd("d_ffn", dx3b, w_down, ua, ub, cfw, batch, seq, tcf)
    ws = w_up.shape[2]
    dw_up = _mm("dw_up_a", h2, du_a, "tn", F32, m=d, n=fh, k=t, tn=ws, tk=512, o3=N_CHIPS)
    dw_up = _mm("dw_up_b", h2, du_b, "tn", F32, m=d, n=fh, k=t, tn=ws, tk=512, o3=N_CHIPS, out=dw_up, o_off=fh)
    dh2 = _mm("d_norm_ffn_a", du_a, w_up, "nt", F32, m=t, n=d, k=fh, b_off=0, b3=True)
    dh2 = _mm("d_norm_ffn_b", du_b, w_up, "nt", F32, m=t, n=d, k=fh, b_off=fh, b3=True, add=dh2)
    dx2, d_g2 = _rms_bwd("d_norm_ffn", x2, dh2, g2, dx3)
    dm = _mm("d_merge", dx2, w_o, "nt", F32, m=t, n=d, k=d)
    dw_o = _mm("dw_o", mg, dx2, "tn", F32, m=d, n=d, k=t, tk=512)
    dproj, dycat, d_bg = _merge_bwd("d_gate_merge", dm, ycat, gl, b_gate, width, gl_off)
    da_c = _mm("d_conv_out", dycat, w_oc, "nt", F32, m=t, n=cw, k=d, a_off=0)
    do = _mm("d_attn_out", dycat, w_oa, "nt", BF16, m=t, n=ATTN_WIDTH, k=d, a_off=d)
    dw_oc = _mm("dw_out_conv", a_c, dycat, "tn", F32, m=cw, n=d, k=t, b_off=0, tk=512)
    dw_oa = _mm("dw_out_attn", o, dycat, "tn", F32, m=ATTN_WIDTH, n=d, k=t, b_off=d, tk=512)
    dproj, d_cmw = _conv_bwd("d_conv_mix", da_c, pc, cmw, dproj, batch, seq, tc)
    dproj, d_fkey, d_fquery = _attn_bwd("attn_bwd", qkv, do, o, lse, frow, dproj, qkv_off, batch, seq, tq)
    d_fquery = jnp.pad(jnp.transpose(d_fquery, (1, 0, 2)).reshape(t, HEADS), ((0, 0), (0, LANES - HEADS)))
    dproj, d_bf = _forget_bwd("d_forget", d_fkey.reshape(batch, HEADS, seq), d_fquery, fl, bf_pad, dproj, f_off,
                              batch, seq)
    dw_inp = _mm("dw_in", dproj, h1, "tn", F32, m=width, n=d, k=t, tm=1792, tk=512)
    dh1 = _mm("d_norm_mix", dproj, w_inp, "nn", F32, m=t, n=d, k=width, tk=1792)
    grad_x, d_g1 = _rms_bwd("d_norm_mix_x", x2d, dh1, g1, dx2)

    d_pc = dw_inp[:pc_w].reshape(nct, 3, tc, d).transpose(1, 0, 2, 3).reshape(pc_w, d)
    d_qkv = dw_inp[qkv_off:gl_off].reshape(HEAD_PAIRS, 3, LANES, d).transpose(1, 0, 2, 3).reshape(qkv_w, d)
    dw_int = jnp.concatenate([d_pc, d_qkv, dw_inp[f_off:f_off + HEADS], dw_inp[gl_off:f_off]], axis=0)
    mats = (dw_int, dw_oc, dw_oa, dw_o, dw_up, dw_down)
    smalls = (d_g1, d_g2, d_gf, d_bg, d_bf, d_cmw, d_cfw)
    return loss_row[0, 0], grad_x.reshape(batch, seq, d), mats, smalls


def _pack_small(parts):
    flat = [p.reshape(-1) for p in parts]
    sizes = [f.shape[0] for f in flat]
    total = sum(sizes)
    padded = -(-total // (8 * LANES)) * (8 * LANES)
    vec = jnp.concatenate(flat + [jnp.zeros((padded - total,), F32)])
    offsets = [sum(sizes[:i]) for i in range(len(sizes))]
    return vec.reshape(padded // LANES, LANES), offsets


def kernel(x, norm_mix_g, w_in, b_f, b_gate, conv_mix_w, w_out_conv, w_out_attn, w_o, norm_ffn_g, w_up, conv_ffn_w, w_down, norm_f_g, loss_target, m_norm_mix_g, m_w_in, m_b_f, m_b_gate, m_conv_mix_w, m_w_out_conv, m_w_out_attn, m_w_o, m_norm_ffn_g, m_w_up, m_conv_ffn_w, m_w_down, m_norm_f_g, v_norm_mix_g, v_w_in, v_b_f, v_b_gate, v_conv_mix_w, v_w_out_conv, v_w_out_attn, v_w_o, v_norm_ffn_g, v_w_up, v_conv_ffn_w, v_w_down, v_norm_f_g):
    d = x.shape[-1]
    chip = 2 * lax.axis_index("x") + lax.axis_index("y")
    place = jnp.stack([chip, lax.axis_index("c")]).astype(jnp.int32)

    t_in, t_m_in, t_v_in = (jnp.transpose(w[0]) for w in (w_in, m_w_in, v_w_in))
    axes = (1, 0, 0, 0, 0, 0)

    bigs = [t_in.astype(BF16)] + [w[0].astype(BF16) for w in (w_out_conv, w_out_attn, w_o, w_up, w_down)]
    gathered = _gather_weights(bigs, axes, [conv_mix_w[0], conv_ffn_w[0]])
    a_in, a_oc, a_oa, a_o, a_up, a_down, a_cmw, a_cfw = gathered
    full_in = a_in.reshape(N_CHIPS * a_in.shape[1], a_in.shape[2])
    full_o = a_o.reshape(N_CHIPS * a_o.shape[1], a_o.shape[2])
    full_down = a_down.reshape(N_CHIPS * a_down.shape[1], a_down.shape[2])

    loss_local, grad_x, mats, smalls = _local_step(
        x, loss_target, full_in, _cat_cols(a_oc), _cat_cols(a_oa), full_o, a_up, full_down,
        _cat_cols(a_cmw), _cat_cols(a_cfw), norm_mix_g, b_f, b_gate, norm_ffn_g, norm_f_g)
    dw_int, dw_oc, dw_oa, dw_o, dw_up, dw_down = mats

    grads = [
        dw_int.reshape(N_CHIPS, dw_int.shape[0] // N_CHIPS, dw_int.shape[1]),
        _split_cols(dw_oc), _split_cols(dw_oa),
        dw_o.reshape(N_CHIPS, dw_o.shape[0] // N_CHIPS, dw_o.shape[1]),
        dw_up,
        dw_down.reshape(N_CHIPS, dw_down.shape[0] // N_CHIPS, dw_down.shape[1]),
    ]
    names = ("w_in", "w_out_conv", "w_out_attn", "w_o", "w_up", "w_down")
    got = _exchange_sibling(grads, axes)
    sums = [_pair_sum("pair_sum_" + nm, place, g, r, ax) for nm, g, r, ax in zip(names, grads, got, axes)]
    arrivals = _exchange_chips(sums)
    halves = [_chip_sum("chip_sum_" + nm, place, g, r, arr, ax)
              for nm, g, r, arr, ax in zip(names, grads, got, arrivals, axes)]
    g_in, g_oc, g_oa, g_o, g_up, g_down = _share_sibling(halves, axes)

    packed, offs = _pack_small(smalls)
    total = _device_sum("device_sum", _gather_small(packed)).reshape(-1)
    shapes = [s.shape for s in smalls]
    d_g1, d_g2, d_gf, d_bg, d_bf, d_cmw, d_cfw = [
        total[o:o + math.prod(sh)].reshape(sh) for o, sh in zip(offs, shapes)]
    d_bf = d_bf[:, :HEADS]
    cw_s, cf_s = conv_mix_w.shape[2], conv_ffn_w.shape[2]
    d_cmw = lax.dynamic_slice(d_cmw, (0, chip * cw_s), (3, cw_s))
    d_cfw = lax.dynamic_slice(d_cfw, (0, chip * cf_s), (3, cf_s))

    loss = lax.psum(loss_local, ("x", "y", "c"))
    order = [
        ("norm_mix_g", norm_mix_g[0:1], d_g1, m_norm_mix_g, v_norm_mix_g),
        ("w_in", t_in, g_in, t_m_in, t_v_in),
        ("b_f", b_f, d_bf, m_b_f, v_b_f),
        ("b_gate", b_gate, d_bg, m_b_gate, v_b_gate),
        ("conv_mix_w", conv_mix_w[0], d_cmw, m_conv_mix_w[0], v_conv_mix_w[0]),
        ("w_out_conv", w_out_conv[0], g_oc, m_w_out_conv[0], v_w_out_conv[0]),
        ("w_out_attn", w_out_attn[0], g_oa, m_w_out_attn[0], v_w_out_attn[0]),
        ("w_o", w_o[0], g_o, m_w_o[0], v_w_o[0]),
        ("norm_ffn_g", norm_ffn_g, d_g2, m_norm_ffn_g, v_norm_ffn_g),
        ("w_up", w_up[0], g_up, m_w_up[0], v_w_up[0]),
        ("conv_ffn_w", conv_ffn_w[0], d_cfw, m_conv_ffn_w[0], v_conv_ffn_w[0]),
        ("w_down", w_down[0], g_down, m_w_down[0], v_w_down[0]),
        ("norm_f_g", norm_f_g.reshape(1, d), d_gf, m_norm_f_g.reshape(1, d), v_norm_f_g.reshape(1, d)),
    ]
    out_shapes = [norm_mix_g.shape, w_in.shape, b_f.shape, b_gate.shape, conv_mix_w.shape, w_out_conv.shape,
                  w_out_attn.shape, w_o.shape, norm_ffn_g.shape, w_up.shape, conv_ffn_w.shape, w_down.shape,
                  norm_f_g.shape]
    g_out, d_out, m_out, v_out = [], [], [], []
    for (nm, w, g, m, v), sh in zip(order, out_shapes):
        g = g.reshape(w.shape)
        delta, new_m, new_v = _adamw("adamw_" + nm, w, g, m.reshape(w.shape), v.reshape(w.shape))
        for dst, val in ((g_out, g), (d_out, delta), (m_out, new_m), (v_out, new_v)):
            dst.append((jnp.transpose(val) if nm == "w_in" else val).reshape(sh))
    return (loss, grad_x, *g_out, *d_out, *m_out, *v_out)
```

```python
import functools
import math

import jax
import jax.numpy as jnp
from jax import lax
from jax.experimental import pallas as pl
from jax.experimental.pallas import tpu as pltpu

F32 = jnp.float32
BF16 = jnp.bfloat16
MESH = pl.DeviceIdType.MESH

EPS = 1e-6
HEADS = 8
HEAD_DIM = 64
ATTN_WIDTH = HEADS * HEAD_DIM
HEAD_PAIRS = HEADS // 2
LANES = 128
F_PAD = 2 * LANES
NEG_BIG = -1e30
N_CHIPS = 4
N_DEV = 8

ADAM_LR = 0.001
ADAM_B1 = 0.9
ADAM_B2 = 0.999
ADAM_EPS = 1e-08
ADAM_WD = 0.01
ADAM_STEP = 10

_DIMS = {
    "nn": (((1,), (0,)), ((), ())),
    "nt": (((1,), (1,)), ((), ())),
    "tn": (((0,), (0,)), ((), ())),
}


def _tile(n, target, mult, also=()):
    best = None
    for t in range(mult, n + 1, mult):
        if n % t == 0 and t <= target and all(o % t == 0 for o in also):
            best = t
    if best is None:
        assert all(o == 0 for o in also), (n, target, mult, also)
        return n
    return best


def _sds(shape, dtype):
    return jax.ShapeDtypeStruct(shape, dtype)


def _mm(name, a, b, mode, out_dtype, *, m, n, k, a_off=0, b_off=0, b_roff=0, b3=False, out=None, o_off=0,
        o_width=None, o3=None, add=None, tm=1024, tn=2048, tk=2048):
    wb = b.shape[2] if b3 else None
    if mode == "nn":
        tm = _tile(m, tm, 16)
        tk = _tile(k, tk, LANES, (a_off,))
        tn = wb if b3 else _tile(n, tn, LANES, (b_off, o_off))
        a_spec = pl.BlockSpec((tm, tk), lambda i, j, kk: (i, a_off // tk + kk))
        if b3:
            b_spec = pl.BlockSpec((None, tk, tn), lambda i, j, kk: (b_off // tn + j, kk, 0))
        else:
            b_spec = pl.BlockSpec((tk, tn), lambda i, j, kk: (kk, b_off // tn + j))
    elif mode == "nt":
        tm = _tile(m, tm, 16)
        tk = wb if b3 else _tile(k, tk, LANES, (a_off, b_off))
        tn = _tile(n, tn, LANES, (o_off, b_roff))
        a_spec = pl.BlockSpec((tm, tk), lambda i, j, kk: (i, a_off // tk + kk))
        if b3:
            b_spec = pl.BlockSpec((None, tn, tk), lambda i, j, kk: (b_off // tk + kk, b_roff // tn + j, 0))
        else:
            b_spec = pl.BlockSpec((tn, tk), lambda i, j, kk: (b_roff // tn + j, b_off // tk + kk))
    else:
        tm = _tile(m, tm, LANES, (a_off,))
        tk = _tile(k, tk, 16)
        tn = _tile(n, tn, LANES, (b_off, o_off))
        a_spec = pl.BlockSpec((tk, tm), lambda i, j, kk: (kk, a_off // tm + i))
        b_spec = pl.BlockSpec((tk, tn), lambda i, j, kk: (kk, b_off // tn + j))
    assert m % tm == 0 and n % tn == 0 and k % tk == 0, (name, tm, tn, tk)
    nk = k // tk
    if o3 is not None:
        o_spec = pl.BlockSpec((None, tm, tn), lambda i, j, kk: (o_off // tn + j, i, 0))
        out_sds = _sds((o3, m, tn), out_dtype)
    else:
        o_spec = pl.BlockSpec((tm, tn), lambda i, j, kk: (i, o_off // tn + j))
        width = o_width if o_width is not None else (out.shape[1] if out is not None else n)
        out_sds = _sds((m, width), out_dtype)
    use_acc = nk > 1 and out_dtype != F32
    dims = _DIMS[mode]
    has_add, has_out = add is not None, out is not None

    def body(*refs):
        a_ref, b_ref = refs[0], refs[1]
        pos = 2
        add_ref = None
        if has_add:
            add_ref = refs[pos]
            pos += 1
        if has_out:
            pos += 1
        o_ref = refs[pos]
        acc_ref = refs[pos + 1] if use_acc else None
        part = lax.dot_general(a_ref[...].astype(BF16), b_ref[...].astype(BF16), dims,
                               preferred_element_type=F32)
        if nk == 1:
            if has_add:
                part = part + add_ref[...]
            o_ref[...] = part.astype(o_ref.dtype)
            return
        kk = pl.program_id(2)
        tgt = acc_ref if use_acc else o_ref

        @pl.when(kk == 0)
        def _():
            tgt[...] = part + add_ref[...] if has_add else part

        @pl.when(kk > 0)
        def _():
            tgt[...] += part

        if use_acc:
            @pl.when(kk == nk - 1)
            def _():
                o_ref[...] = acc_ref[...].astype(o_ref.dtype)

    operands, in_specs = [a, b], [a_spec, b_spec]
    if has_add:
        operands.append(add)
        in_specs.append(pl.BlockSpec((tm, tn), lambda i, j, kk: (i, j)))
    aliases = {}
    if has_out:
        aliases = {len(operands): 0}
        operands.append(out)
        in_specs.append(pl.BlockSpec(memory_space=pl.ANY))
    return pl.pallas_call(
        body,
        out_shape=out_sds,
        grid=(m // tm, n // tn, nk),
        in_specs=in_specs,
        out_specs=o_spec,
        scratch_shapes=[pltpu.VMEM((tm, tn), F32)] if use_acc else [],
        input_output_aliases=aliases,
        compiler_params=pltpu.CompilerParams(dimension_semantics=("parallel", "parallel", "arbitrary")),
        name=name,
    )(*operands)


def _rms_fwd(name, x, g):
    t, d = x.shape
    tm = _tile(t, 512, 16)

    def body(x_ref, g_ref, o_ref):
        xv = x_ref[...]
        r = lax.rsqrt(jnp.mean(xv * xv, axis=-1, keepdims=True) + EPS)
        o_ref[...] = ((xv * r) * g_ref[...]).astype(o_ref.dtype)

    return pl.pallas_call(
        body,
        out_shape=_sds((t, d), BF16),
        grid=(t // tm,),
        in_specs=[pl.BlockSpec((tm, d), lambda i: (i, 0)), pl.BlockSpec((1, d), lambda i: (0, 0))],
        out_specs=pl.BlockSpec((tm, d), lambda i: (i, 0)),
        compiler_params=pltpu.CompilerParams(dimension_semantics=("parallel",)),
        name=name,
    )(x, g)


def _rms_bwd(name, x, dh, g, res):
    t, d = x.shape
    tm = _tile(t, 512, 16)

    def body(x_ref, dh_ref, g_ref, res_ref, dx_ref, dg_ref):
        xv = x_ref[...]
        r = lax.rsqrt(jnp.mean(xv * xv, axis=-1, keepdims=True) + EPS)
        xh = xv * r
        dhv = dh_ref[...]
        dxh = dhv * g_ref[...]
        dx_ref[...] = res_ref[...] + r * (dxh - xh * jnp.mean(dxh * xh, axis=-1, keepdims=True))

        @pl.when(pl.program_id(0) == 0)
        def _():
            dg_ref[...] = jnp.zeros_like(dg_ref)

        dg_ref[...] += jnp.sum(dhv * xh, axis=0, keepdims=True)

    row = pl.BlockSpec((tm, d), lambda i: (i, 0))
    vec = pl.BlockSpec((1, d), lambda i: (0, 0))
    return pl.pallas_call(
        body,
        out_shape=(_sds((t, d), F32), _sds((1, d), F32)),
        grid=(t // tm,),
        in_specs=[row, row, vec, row],
        out_specs=(row, vec),
        compiler_params=pltpu.CompilerParams(dimension_semantics=("arbitrary",)),
        name=name,
    )(x, dh, g, res)


def _final_loss(name, x, g, target):
    t, d = x.shape
    tm = _tile(t, 512, 16)

    def body(x_ref, g_ref, t_ref, dx_ref, dxb_ref, loss_ref, dg_ref):
        xv = x_ref[...]
        gv = g_ref[...]
        r = lax.rsqrt(jnp.mean(xv * xv, axis=-1, keepdims=True) + EPS)
        xh = xv * r
        err = xh * gv - t_ref[...]
        dy = err * (1.0 / d)
        dxh = dy * gv
        dx = r * (dxh - xh * jnp.mean(dxh * xh, axis=-1, keepdims=True))
        dx_ref[...] = dx
        dxb_ref[...] = dx.astype(dxb_ref.dtype)
        per_row = jnp.sum(err * err, axis=-1, keepdims=True) * (0.5 / d)

        @pl.when(pl.program_id(0) == 0)
        def _():
            dg_ref[...] = jnp.zeros_like(dg_ref)
            loss_ref[...] = jnp.zeros_like(loss_ref)

        dg_ref[...] += jnp.sum(dy * xh, axis=0, keepdims=True)
        loss_ref[...] += jnp.sum(per_row, axis=0, keepdims=True)

    row = pl.BlockSpec((tm, d), lambda i: (i, 0))
    vec = pl.BlockSpec((1, d), lambda i: (0, 0))
    return pl.pallas_call(
        body,
        out_shape=(_sds((t, d), F32), _sds((t, d), BF16), _sds((1, LANES), F32), _sds((1, d), F32)),
        grid=(t // tm,),
        in_specs=[row, vec, row],
        out_specs=(row, row, pl.BlockSpec((1, LANES), lambda i: (0, 0)), vec),
        compiler_params=pltpu.CompilerParams(dimension_semantics=("arbitrary",)),
        name=name,
    )(x, g, target)


def _shift_down(z, k):
    row = lax.broadcasted_iota(jnp.int32, z.shape, 0)
    return jnp.where(row >= k, pltpu.roll(z, k, axis=0), 0.0)


def _shift_up(z, k):
    s = z.shape[0]
    row = lax.broadcasted_iota(jnp.int32, z.shape, 0)
    return jnp.where(row < s - k, pltpu.roll(z, s - k, axis=0), 0.0)


def _conv3(z, w):
    return (w[2:3] * z + w[0:1] * _shift_down(z, 2)) + w[1:2] * _shift_down(z, 1)


def _conv3_t(dz, w):
    return (w[2:3] * dz + w[0:1] * _shift_up(dz, 2)) + w[1:2] * _shift_up(dz, 1)


def _conv_fwd(name, pc, w, batch, seq, tc):
    cw = w.shape[1]
    nct = cw // tc

    def body(pc_ref, w_ref, o_ref):
        cb = pc_ref[:, 0:tc]
        z = pc_ref[:, tc:2 * tc] * pc_ref[:, 2 * tc:3 * tc]
        o_ref[...] = (cb * _conv3(z, w_ref[...])).astype(o_ref.dtype)

    return pl.pallas_call(
        body,
        out_shape=_sds((batch * seq, cw), BF16),
        grid=(batch, nct),
        in_specs=[pl.BlockSpec((seq, 3 * tc), lambda b, j: (b, j)), pl.BlockSpec((3, tc), lambda b, j: (0, j))],
        out_specs=pl.BlockSpec((seq, tc), lambda b, j: (b, j)),
        compiler_params=pltpu.CompilerParams(dimension_semantics=("parallel", "parallel")),
        name=name,
    )(pc, w)


def _conv_bwd(name, da, pc, w, dproj, batch, seq, tc):
    cw = w.shape[1]
    nct = cw // tc

    def body(da_ref, pc_ref, w_ref, _, dpc_ref, dw_ref):
        wv = w_ref[...]
        cb = pc_ref[:, 0:tc]
        cc = pc_ref[:, tc:2 * tc]
        cin = pc_ref[:, 2 * tc:3 * tc]
        z = cc * cin
        dav = da_ref[...]
        du = dav * cb
        dz = _conv3_t(du, wv)
        dpc_ref[:, 0:tc] = (dav * _conv3(z, wv)).astype(dpc_ref.dtype)
        dpc_ref[:, tc:2 * tc] = (dz * cin).astype(dpc_ref.dtype)
        dpc_ref[:, 2 * tc:3 * tc] = (dz * cc).astype(dpc_ref.dtype)

        @pl.when(pl.program_id(1) == 0)
        def _():
            dw_ref[...] = jnp.zeros_like(dw_ref)

        dw_ref[0:1, :] += jnp.sum(du * _shift_down(z, 2), axis=0, keepdims=True)
        dw_ref[1:2, :] += jnp.sum(du * _shift_down(z, 1), axis=0, keepdims=True)
        dw_ref[2:3, :] += jnp.sum(du * z, axis=0, keepdims=True)

    return pl.pallas_call(
        body,
        out_shape=(_sds(dproj.shape, dproj.dtype), _sds((3, cw), F32)),
        grid=(nct, batch),
        in_specs=[
            pl.BlockSpec((seq, tc), lambda j, b: (b, j)),
            pl.BlockSpec((seq, 3 * tc), lambda j, b: (b, j)),
            pl.BlockSpec((3, tc), lambda j, b: (0, j)),
            pl.BlockSpec(memory_space=pl.ANY),
        ],
        out_specs=(pl.BlockSpec((seq, 3 * tc), lambda j, b: (b, j)), pl.BlockSpec((3, tc), lambda j, b: (0, j))),
        input_output_aliases={3: 0},
        compiler_params=pltpu.CompilerParams(dimension_semantics=("parallel", "arbitrary")),
        name=name,
    )(da, pc, w, dproj)


def _ffn_up_act(name, h2, w_up, w, batch, seq, tc):
    d = h2.shape[1]
    fh = w.shape[1] // 2
    nf = fh // tc

    def body(h_ref, ma_ref, mb_ref, wa_ref, wb_ref, o_ref, ua_ref, ub_ref):
        hv = h_ref[...]
        ua = _dot(hv, ma_ref[...], "nn")
        ub = _dot(hv, mb_ref[...], "nn")
        ua_ref[...] = ua.astype(ua_ref.dtype)
        ub_ref[...] = ub.astype(ub_ref.dtype)
        a = _conv3(ua, wa_ref[...])
        b = _conv3(ub, wb_ref[...])
        o_ref[...] = (a * jax.nn.sigmoid(a) * b).astype(o_ref.dtype)

    act = pl.BlockSpec((seq, tc), lambda b, j: (b, j))
    shape = _sds((batch * seq, fh), BF16)
    return pl.pallas_call(
        body,
        out_shape=(shape, shape, shape),
        grid=(batch, nf),
        in_specs=[
            pl.BlockSpec((seq, d), lambda b, j: (b, 0)),
            pl.BlockSpec((d, tc), lambda b, j: (0, j)),
            pl.BlockSpec((d, tc), lambda b, j: (0, nf + j)),
            pl.BlockSpec((3, tc), lambda b, j: (0, j)),
            pl.BlockSpec((3, tc), lambda b, j: (0, nf + j)),
        ],
        out_specs=(act, act, act),
        compiler_params=pltpu.CompilerParams(dimension_semantics=("parallel", "parallel")),
        name=name,
    )(h2, w_up, w_up, w, w)


def _ffn_bwd(name, dx, w_down, ua, ub, w, batch, seq, tc):
    d = dx.shape[1]
    fh = w.shape[1] // 2
    nf = fh // tc

    def body(dx_ref, md_ref, ua_ref, ub_ref, wa_ref, wb_ref, dua_ref, dub_ref, dw_ref):
        j = pl.program_id(1)
        uav, ubv, wa, wb = ua_ref[...].astype(F32), ub_ref[...].astype(F32), wa_ref[...], wb_ref[...]
        dhv = _dot(dx_ref[...].astype(BF16), md_ref[...], "nt")
        a = _conv3(uav, wa)
        b = _conv3(ubv, wb)
        sg = jax.nn.sigmoid(a)
        da = dhv * b * (sg * (1.0 + a * (1.0 - sg)))
        db = dhv * (a * sg)
        dua_ref[...] = _conv3_t(da, wa).astype(dua_ref.dtype)
        dub_ref[...] = _conv3_t(db, wb).astype(dub_ref.dtype)

        @pl.when((pl.program_id(0) == 0) & (j == 0))
        def _():
            dw_ref[...] = jnp.zeros_like(dw_ref)

        for off, dv, uv in ((0, da, uav), (fh, db, ubv)):
            cols = pl.ds(pl.multiple_of(off + j * tc, LANES), tc)
            dw_ref[0:1, cols] += jnp.sum(dv * _shift_down(uv, 2), axis=0, keepdims=True)
            dw_ref[1:2, cols] += jnp.sum(dv * _shift_down(uv, 1), axis=0, keepdims=True)
            dw_ref[2:3, cols] += jnp.sum(dv * uv, axis=0, keepdims=True)

    act = pl.BlockSpec((seq, tc), lambda b, j: (b, j))
    shape = _sds((batch * seq, fh), BF16)
    return pl.pallas_call(
        body,
        out_shape=(shape, shape, _sds((3, 2 * fh), F32)),
        grid=(batch, nf),
        in_specs=[
            pl.BlockSpec((seq, d), lambda b, j: (b, 0)),
            pl.BlockSpec((tc, d), lambda b, j: (j, 0)),
            act,
            act,
            pl.BlockSpec((3, tc), lambda b, j: (0, j)),
            pl.BlockSpec((3, tc), lambda b, j: (0, nf + j)),
        ],
        out_specs=(act, act, pl.BlockSpec((3, 2 * fh), lambda b, j: (0, 0))),
        compiler_params=pltpu.CompilerParams(dimension_semantics=("arbitrary", "arbitrary")),
        name=name,
    )(dx, w_down, ua, ub, w, w)


def _merge_fwd(name, ycat, gl, bg):
    t, d2 = ycat.shape
    d = d2 // 2
    tm = _tile(t, 256, 16)

    def body(y_ref, gl_ref, bg_ref, o_ref):
        g = jax.nn.sigmoid(gl_ref[...] + bg_ref[...])
        prod = g * y_ref[...]
        o_ref[...] = (prod[:, 0:d] + prod[:, d:d2]).astype(o_ref.dtype)

    row = pl.BlockSpec((tm, d2), lambda i: (i, 0))
    return pl.pallas_call(
        body,
        out_shape=_sds((t, d), BF16),
        grid=(t // tm,),
        in_specs=[row, row, pl.BlockSpec((1, d2), lambda i: (0, 0))],
        out_specs=pl.BlockSpec((tm, d), lambda i: (i, 0)),
        compiler_params=pltpu.CompilerParams(dimension_semantics=("parallel",)),
        name=name,
    )(ycat, gl, bg)


def _merge_bwd(name, dm, ycat, gl, bg, width, gl_off):
    t, d2 = ycat.shape
    d = d2 // 2
    tm = _tile(t, 512, 16)
    wb = math.gcd(gl_off, d)
    nw = d // wb

    def body(dm_ref, y_ref, gl_ref, bg_ref, dgl_ref, dy_ref, dbg_ref):
        g = jax.nn.sigmoid(gl_ref[...] + bg_ref[...])
        dmv = dm_ref[...]
        dgl = dmv * y_ref[...] * (g * (1.0 - g))
        dgl_ref[...] = dgl.astype(dgl_ref.dtype)
        dy_ref[...] = (dmv * g).astype(dy_ref.dtype)

        @pl.when(pl.program_id(2) == 0)
        def _():
            dbg_ref[...] = jnp.zeros_like(dbg_ref)

        dbg_ref[...] += jnp.sum(dgl, axis=0, keepdims=True)

    half = pl.BlockSpec((tm, wb), lambda h, j, i: (i, h * nw + j))
    vec = pl.BlockSpec((1, wb), lambda h, j, i: (0, h * nw + j))
    return pl.pallas_call(
        body,
        out_shape=(_sds((t, width), BF16), _sds((t, d2), BF16), _sds((1, d2), F32)),
        grid=(2, nw, t // tm),
        in_specs=[pl.BlockSpec((tm, wb), lambda h, j, i: (i, j)), half, half, vec],
        out_specs=(pl.BlockSpec((tm, wb), lambda h, j, i: (i, gl_off // wb + h * nw + j)), half, vec),
        compiler_params=pltpu.CompilerParams(dimension_semantics=("parallel", "parallel", "arbitrary")),
        name=name,
    )(dm, ycat, gl, bg)


def _log_sigmoid(z):
    return jnp.minimum(z, 0.0) - jnp.log1p(jnp.exp(-jnp.abs(z)))


def _forget_fwd(name, fl, bf, batch, seq):
    def body(fl_ref, bf_ref, o_ref):
        lf = _log_sigmoid(fl_ref[:, 0:LANES] + bf_ref[:, 0:LANES])
        acc = lf.T[0:HEADS, :]
        lane = lax.broadcasted_iota(jnp.int32, acc.shape, 1)
        k = 1
        while k < seq:
            acc = acc + jnp.where(lane >= k, pltpu.roll(acc, k, axis=1), 0.0)
            k *= 2
        o_ref[...] = acc

    return pl.pallas_call(
        body,
        out_shape=_sds((batch, HEADS, seq), F32),
        grid=(batch,),
        in_specs=[pl.BlockSpec((seq, F_PAD), lambda b: (b, 0)), pl.BlockSpec((1, F_PAD), lambda b: (0, 0))],
        out_specs=pl.BlockSpec((None, HEADS, seq), lambda b: (b, 0, 0)),
        compiler_params=pltpu.CompilerParams(dimension_semantics=("parallel",)),
        name=name,
    )(fl, bf)


def _forget_bwd(name, d_key, d_query, fl, bf, dproj, f_off, batch, seq):
    nfb = F_PAD // LANES

    def body(dk_ref, dq_ref, fl_ref, bf_ref, _, df_ref, dbf_ref):
        jj = pl.program_id(1)
        key_t = jnp.concatenate([dk_ref[...], jnp.zeros((LANES - HEADS, seq), F32)], axis=0).T
        acc = dq_ref[...] - key_t
        row = lax.broadcasted_iota(jnp.int32, acc.shape, 0)
        k = 1
        while k < seq:
            acc = acc + jnp.where(row < seq - k, pltpu.roll(acc, seq - k, axis=0), 0.0)
            k *= 2
        z = fl_ref[:, 0:LANES] + bf_ref[:, 0:LANES]
        col = lax.broadcasted_iota(jnp.int32, acc.shape, 1)
        df = jnp.where(col < HEADS, acc * jax.nn.sigmoid(-z), 0.0)
        df = jnp.where(jj == 0, df, 0.0)
        df_ref[...] = df.astype(df_ref.dtype)

        @pl.when((pl.program_id(0) == 0) & (jj == 0))
        def _():
            dbf_ref[...] = jnp.zeros_like(dbf_ref)

        dbf_ref[...] += jnp.sum(df, axis=0, keepdims=True)

    return pl.pallas_call(
        body,
        out_shape=(_sds(dproj.shape, dproj.dtype), _sds((1, LANES), F32)),
        grid=(batch, nfb),
        in_specs=[
            pl.BlockSpec((None, HEADS, seq), lambda b, j: (b, 0, 0)),
            pl.BlockSpec((seq, LANES), lambda b, j: (b, 0)),
            pl.BlockSpec((seq, F_PAD), lambda b, j: (b, 0)),
            pl.BlockSpec((1, F_PAD), lambda b, j: (0, 0)),
            pl.BlockSpec(memory_space=pl.ANY),
        ],
        out_specs=(pl.BlockSpec((seq, LANES), lambda b, j: (b, f_off // LANES + j)),
                   pl.BlockSpec((1, LANES), lambda b, j: (0, 0))),
        input_output_aliases={4: 0},
        compiler_params=pltpu.CompilerParams(dimension_semantics=("arbitrary", "arbitrary")),
        name=name,
    )(d_key, d_query, fl, bf, dproj)


def _dot(a, b, mode):
    return lax.dot_general(a, b, _DIMS[mode], preferred_element_type=F32)


def _attn_fwd(name, qkv, frow, batch, seq, tq):
    nq = seq // tq
    scale = 1.0 / math.sqrt(HEAD_DIM)

    def body(q_ref, k_ref, v_ref, f_ref, o_ref, lse_ref):
        i = pl.program_id(2)
        lane = lax.broadcasted_iota(jnp.int32, (1, LANES), 1)
        lo = lane < HEAD_DIM
        qs = q_ref[...] * scale
        qh = (jnp.where(lo, qs, 0.0).astype(BF16), jnp.where(lo, 0.0, qs).astype(BF16))
        row = lax.broadcasted_iota(jnp.int32, (tq, tq), 0)
        col = lax.broadcasted_iota(jnp.int32, (tq, tq), 1)

        def step(j, carry, diag):
            m0, l0, m1, l1, acc = carry
            start = pl.multiple_of(j * tq, tq)
            kj = k_ref[pl.ds(start, tq), :]
            vj = v_ref[pl.ds(start, tq), :]
            ms, ls, pvs, alphas = [], [], [], []
            for h, (m_old, l_old) in enumerate(((m0, l0), (m1, l1))):
                s = _dot(qh[h], kj, "nt") - f_ref[h:h + 1, pl.ds(start, tq)]
                if diag:
                    s = jnp.where(col <= row, s, NEG_BIG)
                m_new = jnp.maximum(m_old, jnp.max(s, axis=1, keepdims=True))
                p = jnp.exp(s - m_new)
                alpha = jnp.exp(m_old - m_new)
                ls.append(alpha * l_old + jnp.sum(p, axis=1, keepdims=True))
                ms.append(m_new)
                alphas.append(alpha)
                vh = jnp.where(lo, vj, 0.0) if h == 0 else jnp.where(lo, 0.0, vj)
                pvs.append(_dot(p.astype(BF16), vh.astype(BF16), "nn"))
            acc = acc * jnp.where(lo, alphas[0], alphas[1]) + (pvs[0] + pvs[1])
            return ms[0], ls[0], ms[1], ls[1], acc

        neg = jnp.full((tq, 1), NEG_BIG, F32)
        zero = jnp.zeros((tq, 1), F32)
        init = (neg, zero, neg, zero, jnp.zeros((tq, LANES), F32))
        carry = lax.fori_loop(0, i, lambda j, c: step(j, c, False), init)
        m0, l0, m1, l1, acc = step(i, carry, True)
        o_ref[...] = (acc / jnp.where(lo, l0, l1)).astype(o_ref.dtype)
        lse_ref[:, 0:1] = m0 + jnp.log(l0)
        lse_ref[:, 1:2] = m1 + jnp.log(l1)

    return pl.pallas_call(
        body,
        out_shape=(_sds((batch * seq, ATTN_WIDTH), BF16), _sds((HEAD_PAIRS, batch * seq, 2), F32)),
        grid=(batch, HEAD_PAIRS, nq),
        in_specs=[
            pl.BlockSpec((tq, LANES), lambda b, hp, i: (b * nq + i, 3 * hp)),
            pl.BlockSpec((seq, LANES), lambda b, hp, i: (b, 3 * hp + 1)),
            pl.BlockSpec((seq, LANES), lambda b, hp, i: (b, 3 * hp + 2)),
            pl.BlockSpec((None, None, 2, seq), lambda b, hp, i: (b, hp, 0, 0)),
        ],
        out_specs=(
            pl.BlockSpec((tq, LANES), lambda b, hp, i: (b * nq + i, hp)),
            pl.BlockSpec((None, tq, 2), lambda b, hp, i: (hp, b * nq + i, 0)),
        ),
        compiler_params=pltpu.CompilerParams(dimension_semantics=("parallel", "parallel", "parallel")),
        name=name,
    )(qkv, qkv, qkv, frow)


def _attn_bwd(name, qkv, do, o, lse, frow, dproj, qkv_off, batch, seq, tq):
    nq = seq // tq
    scale = 1.0 / math.sqrt(HEAD_DIM)

    def body(q_ref, k_ref, v_ref, do_ref, o_ref, lse_ref, f_ref, _, dqkv_ref, df_ref, drow_ref,
             dq_acc, dk_acc, dv_acc, df_acc):
        j = pl.program_id(2)
        lane = lax.broadcasted_iota(jnp.int32, (1, LANES), 1)
        lo = lane < HEAD_DIM
        masks = (lo, jnp.logical_not(lo))
        row = lax.broadcasted_iota(jnp.int32, (tq, tq), 0)
        col = lax.broadcasted_iota(jnp.int32, (tq, tq), 1)

        @pl.when(j == 0)
        def _():
            dq_acc[...] = jnp.zeros_like(dq_acc)
            drow_ref[...] = jnp.zeros_like(drow_ref)

        dk_acc[...] = jnp.zeros_like(dk_acc)
        dv_acc[...] = jnp.zeros_like(dv_acc)
        df_acc[...] = jnp.zeros_like(df_acc)
        kj = k_ref[...]
        vj = v_ref[...]
        kstart = pl.multiple_of(j * tq, tq)
        kh = tuple(jnp.where(mk, kj, 0.0).astype(BF16) for mk in masks)

        def step(i, diag):
            start = pl.multiple_of(i * tq, tq)
            rows = pl.ds(start, tq)
            qi = q_ref[rows, :] * scale
            doi = do_ref[rows, :]
            prod = doi.astype(F32) * o_ref[rows, :].astype(F32)
            lse_i = lse_ref[rows, :]
            dq_i = jnp.zeros((tq, LANES), F32)
            for h, mk in enumerate(masks):
                q_h = jnp.where(mk, qi, 0.0).astype(BF16)
                do_h = jnp.where(mk, doi, 0.0).astype(BF16)
                delta = jnp.sum(jnp.where(mk, prod, 0.0), axis=1, keepdims=True)
                s = _dot(q_h, kj, "nt") - f_ref[h:h + 1, pl.ds(kstart, tq)]
                p = jnp.exp(s - lse_i[:, h:h + 1])
                if diag:
                    p = jnp.where(col <= row, p, 0.0)
                ds = p * (_dot(do_h, vj, "nt") - delta)
                df_acc[h:h + 1, :] += jnp.sum(ds, axis=0, keepdims=True)
                drow_ref[rows, h:h + 1] += jnp.sum(ds, axis=1, keepdims=True)
                dsb = ds.astype(BF16)
                dv_acc[...] += _dot(p.astype(BF16), do_h, "tn")
                dk_acc[...] += _dot(dsb, q_h, "tn")
                dq_i = dq_i + _dot(dsb, kh[h], "nn")
            dq_acc[rows, :] += dq_i

        step(j, True)
        lax.fori_loop(j + 1, nq, lambda i, c: (step(i, False), c)[1], 0)
        dqkv_ref[:, 0:LANES] = (dq_acc[pl.ds(kstart, tq), :] * scale).astype(dqkv_ref.dtype)
        dqkv_ref[:, LANES:2 * LANES] = dk_acc[...].astype(dqkv_ref.dtype)
        dqkv_ref[:, 2 * LANES:3 * LANES] = dv_acc[...].astype(dqkv_ref.dtype)
        df_ref[...] = df_acc[...]

    full = lambda c: pl.BlockSpec((seq, LANES), lambda b, hp, j: (b, c(hp)))
    blk = lambda c: pl.BlockSpec((tq, LANES), lambda b, hp, j: (b * nq + j, c(hp)))
    return pl.pallas_call(
        body,
        out_shape=(_sds(dproj.shape, dproj.dtype), _sds((batch, HEAD_PAIRS, 2, seq), F32),
                   _sds((HEAD_PAIRS, batch * seq, 2), F32)),
        grid=(batch, HEAD_PAIRS, nq),
        in_specs=[
            full(lambda hp: 3 * hp),
            blk(lambda hp: 3 * hp + 1),
            blk(lambda hp: 3 * hp + 2),
            full(lambda hp: hp),
            full(lambda hp: hp),
            pl.BlockSpec((None, seq, 2), lambda b, hp, j: (hp, b, 0)),
            pl.BlockSpec((None, None, 2, seq), lambda b, hp, j: (b, hp, 0, 0)),
            pl.BlockSpec(memory_space=pl.ANY),
        ],
        out_specs=(
            pl.BlockSpec((tq, 3 * LANES), lambda b, hp, j: (b * nq + j, qkv_off // (3 * LANES) + hp)),
            pl.BlockSpec((None, None, 2, tq), lambda b, hp, j: (b, hp, 0, j)),
            pl.BlockSpec((None, seq, 2), lambda b, hp, j: (hp, b, 0)),
        ),
        scratch_shapes=[
            pltpu.VMEM((seq, LANES), F32),
            pltpu.VMEM((tq, LANES), F32),
            pltpu.VMEM((tq, LANES), F32),
            pltpu.VMEM((2, tq), F32),
        ],
        input_output_aliases={7: 0},
        compiler_params=pltpu.CompilerParams(dimension_semantics=("parallel", "parallel", "arbitrary")),
        name=name,
    )(qkv, qkv, qkv, do, o, lse, frow, dproj)


def _mesh_place():
    x, y, c = lax.axis_index("x"), lax.axis_index("y"), lax.axis_index("c")
    chips = [(1 - x, y), (x, 1 - y), (1 - x, 1 - y)]
    return x, y, c, chips


def _hbm_specs(n):
    return [pl.BlockSpec(memory_space=pl.ANY)] * n


def _half(shape2d, axis, which):
    size = shape2d[axis] // 2
    sl = pl.ds(pl.multiple_of(which * size, 16 if axis == 0 else LANES), size)
    return (sl, slice(None)) if axis == 0 else (slice(None), sl)


def _gather_weights(bigs, axes, smalls):
    nb, ns = len(bigs), len(smalls)
    arrays = list(bigs) + list(smalls)
    n = nb + ns

    def body(*refs):
        ins, outs = refs[:n], refs[n:2 * n]
        send_sems, recv_sems = refs[2 * n:]
        x, y, c, chips = _mesh_place()
        me = 2 * x + y
        sibling = (x, y, 1 - c)

        def half(a, which):
            return _half(arrays[a].shape, axes[a], which)

        def copy(a, k, src, dst, to):
            return pltpu.make_async_remote_copy(src_ref=src, dst_ref=dst, send_sem=send_sems.at[a, k],
                                                recv_sem=recv_sems.at[a, k], device_id=to, device_id_type=MESH)

        sends = []
        for a in range(n):
            for j, chip in enumerate(chips):
                if a < nb:
                    cp = copy(a, j, ins[a].at[half(a, c)], outs[a].at[(me,) + half(a, c)], (*chip, c))
                else:
                    cp = copy(a, j, ins[a], outs[a].at[me], (*chip, c))
                cp.start()
                sends.append(cp)
        for a in range(nb):
            for j, (px, py) in enumerate(chips):
                blk = outs[a].at[(2 * px + py,) + half(a, c)]
                copy(a, j, blk, blk, (px, py, c)).wait_recv()
                fwd = copy(a, 3 + j, blk, blk, sibling)
                fwd.start()
                sends.append(fwd)
        for a in range(nb, n):
            for j, (px, py) in enumerate(chips):
                blk = outs[a].at[2 * px + py]
                copy(a, j, blk, blk, (px, py, c)).wait_recv()
        for a in range(nb):
            for j, (px, py) in enumerate(chips):
                blk = outs[a].at[(2 * px + py,) + half(a, 1 - c)]
                copy(a, 3 + j, blk, blk, sibling).wait_recv()
        for cp in sends:
            cp.wait_send()

    outs = pl.pallas_call(
        body,
        out_shape=tuple(_sds((N_CHIPS,) + a.shape, a.dtype) for a in arrays),
        in_specs=_hbm_specs(n),
        out_specs=tuple(_hbm_specs(n)),
        scratch_shapes=[pltpu.SemaphoreType.DMA((n, 6)), pltpu.SemaphoreType.DMA((n, 6))],
        name="gather_weights",
    )(*arrays)
    me = 2 * lax.axis_index("x") + lax.axis_index("y")
    return tuple(lax.dynamic_update_index_in_dim(o, a, me, 0) for o, a in zip(outs, arrays))


def _gather_small(v):
    m_per, ncol = v.shape

    def body(x_ref, out_ref, send_sems, recv_sems, local_sem):
        x, y, c, chips = _mesh_place()
        me, sibling = (x, y, c), (x, y, 1 - c)

        def rows(px, py, pc):
            return out_ref.at[pl.ds((4 * px + 2 * py + pc) * m_per, m_per), :]

        def copy(k, block, to, src=None):
            return pltpu.make_async_remote_copy(src_ref=rows(*block) if src is None else src, dst_ref=rows(*block),
                                                send_sem=send_sems.at[k], recv_sem=recv_sems.at[k],
                                                device_id=to, device_id_type=MESH)

        mine = pltpu.make_async_copy(x_ref, rows(*me), local_sem)
        mine.start()
        first = [copy(0, me, sibling, src=x_ref)]
        first += [copy(1 + j, me, (*chip, c), src=x_ref) for j, chip in enumerate(chips)]
        for cp in first:
            cp.start()
        passed = [copy(4 + j, (*chip, c), sibling) for j, chip in enumerate(chips)]
        for j, chip in enumerate(chips):
            copy(1 + j, (*chip, c), me).wait_recv()
            passed[j].start()
        copy(0, sibling, me).wait_recv()
        for j, chip in enumerate(chips):
            copy(4 + j, (*chip, 1 - c), me).wait_recv()
        for cp in first + passed:
            cp.wait_send()
        mine.wait()

    return pl.pallas_call(
        body,
        out_shape=_sds((N_DEV * m_per, ncol), v.dtype),
        in_specs=[pl.BlockSpec(memory_space=pltpu.VMEM)],
        out_specs=pl.BlockSpec(memory_space=pltpu.VMEM),
        scratch_shapes=[pltpu.SemaphoreType.DMA((7,)), pltpu.SemaphoreType.DMA((7,)), pltpu.SemaphoreType.DMA],
        name="gather_small",
    )(v)


def _half_shape(shape2d, axis):
    return (shape2d[0] // 2, shape2d[1]) if axis == 0 else (shape2d[0], shape2d[1] // 2)


def _exchange_sibling(grads, axes):
    n = len(grads)

    def body(*refs):
        ins, outs = refs[:n], refs[n:2 * n]
        send_sems, recv_sems = refs[2 * n:]
        x, y, c, _ = _mesh_place()
        copies = []
        for a in range(n):
            src = ins[a].at[(slice(None),) + _half(grads[a].shape[1:], axes[a], 1 - c)]
            cp = pltpu.make_async_remote_copy(src_ref=src, dst_ref=outs[a], send_sem=send_sems.at[a],
                                              recv_sem=recv_sems.at[a], device_id=(x, y, 1 - c), device_id_type=MESH)
            cp.start()
            copies.append(cp)
        for cp in copies:
            cp.wait()

    return pl.pallas_call(
        body,
        out_shape=tuple(_sds((N_CHIPS,) + _half_shape(g.shape[1:], ax), g.dtype) for g, ax in zip(grads, axes)),
        in_specs=_hbm_specs(n),
        out_specs=tuple(_hbm_specs(n)),
        scratch_shapes=[pltpu.SemaphoreType.DMA((n,)), pltpu.SemaphoreType.DMA((n,))],
        name="exchange_sibling",
    )(*grads)


def _exchange_chips(sums):
    n = len(sums)

    def body(*refs):
        ins, outs = refs[:n], refs[n:2 * n]
        send_sems, recv_sems = refs[2 * n:]
        _, _, c, chips = _mesh_place()
        copies = []
        for a in range(n):
            for j, (px, py) in enumerate(chips):
                cp = pltpu.make_async_remote_copy(src_ref=ins[a].at[2 * px + py], dst_ref=outs[a].at[j],
                                                  send_sem=send_sems.at[a, j], recv_sem=recv_sems.at[a, j],
                                                  device_id=(px, py, c), device_id_type=MESH)
                cp.start()
                copies.append(cp)
        for cp in copies:
            cp.wait()

    return pl.pallas_call(
        body,
        out_shape=tuple(_sds((3,) + s.shape[1:], s.dtype) for s in sums),
        in_specs=_hbm_specs(n),
        out_specs=tuple(_hbm_specs(n)),
        scratch_shapes=[pltpu.SemaphoreType.DMA((n, 3)), pltpu.SemaphoreType.DMA((n, 3))],
        name="exchange_chips",
    )(*sums)


def _share_sibling(shards, axes):
    n = len(shards)

    def body(*refs):
        ins, outs = refs[:n], refs[n:2 * n]
        send_sems, recv_sems = refs[2 * n:]
        x, y, c, _ = _mesh_place()
        started = []
        for a in range(n):
            mine = _half(shards[a].shape, axes[a], c)
            theirs = _half(shards[a].shape, axes[a], 1 - c)
            cp = pltpu.make_async_remote_copy(src_ref=ins[a].at[mine], dst_ref=outs[a].at[mine],
                                              send_sem=send_sems.at[a], recv_sem=recv_sems.at[a],
                                              device_id=(x, y, 1 - c), device_id_type=MESH)
            cp.start()
            arrival = pltpu.make_async_remote_copy(src_ref=ins[a].at[theirs], dst_ref=outs[a].at[theirs],
                                                   send_sem=send_sems.at[a], recv_sem=recv_sems.at[a],
                                                   device_id=(x, y, 1 - c), device_id_type=MESH)
            started.append((cp, arrival))
        for cp, arrival in started:
            arrival.wait_recv()
            cp.wait_send()

    return pl.pallas_call(
        body,
        out_shape=tuple(_sds(s.shape, s.dtype) for s in shards),
        in_specs=_hbm_specs(n),
        out_specs=tuple(_hbm_specs(n)),
        scratch_shapes=[pltpu.SemaphoreType.DMA((n,)), pltpu.SemaphoreType.DMA((n,))],
        input_output_aliases={a: a for a in range(n)},
        name="share_sibling",
    )(*shards)


def _pair_sum(name, place, g, got, axis):
    hr, hc = got.shape[1:]

    def body(place_ref, g_ref, got_ref, o_ref):
        o_ref[...] = (g_ref[...] + got_ref[...]).astype(o_ref.dtype)

    blk = (None, hr, hc)
    mine = (lambda k, pr: (k, pr[1], 0)) if axis == 0 else (lambda k, pr: (k, 0, pr[1]))
    return pl.pallas_call(
        body,
        out_shape=_sds((N_CHIPS, hr, hc), BF16),
        grid_spec=pltpu.PrefetchScalarGridSpec(
            num_scalar_prefetch=1,
            grid=(N_CHIPS,),
            in_specs=[pl.BlockSpec(blk, mine), pl.BlockSpec(blk, lambda k, pr: (k, 0, 0))],
            out_specs=pl.BlockSpec(blk, lambda k, pr: (k, 0, 0)),
        ),
        compiler_params=pltpu.CompilerParams(dimension_semantics=("parallel",)),
        name=name,
    )(place, g, got)


def _chip_sum(name, place, g, got, arrivals, axis):
    _, r, cdim = g.shape
    hr, hc = got.shape[1:]

    def body(place_ref, g_ref, got_ref, arr_ref, o_ref):
        acc = g_ref[...] + got_ref[...]
        for j in range(3):
            acc = acc + arr_ref[j].astype(F32)
        o_ref[...] = acc

    blk = (None, hr, hc)
    mine = (lambda i, pr: (pr[0], pr[1], 0)) if axis == 0 else (lambda i, pr: (pr[0], 0, pr[1]))
    dest = (lambda i, pr: (pr[1], 0)) if axis == 0 else (lambda i, pr: (0, pr[1]))
    return pl.pallas_call(
        body,
        out_shape=_sds((r, cdim), F32),
        grid_spec=pltpu.PrefetchScalarGridSpec(
            num_scalar_prefetch=1,
            grid=(1,),
            in_specs=[
                pl.BlockSpec(blk, mine),
                pl.BlockSpec(blk, lambda i, pr: (pr[0], 0, 0)),
                pl.BlockSpec((3, hr, hc), lambda i, pr: (0, 0, 0)),
            ],
            out_specs=pl.BlockSpec((hr, hc), dest),
        ),
        compiler_params=pltpu.CompilerParams(dimension_semantics=("arbitrary",)),
        name=name,
    )(place, g, got, arrivals)


def _device_sum(name, gathered):
    m_per = gathered.shape[0] // N_DEV

    def body(g_ref, o_ref):
        acc = g_ref[0:m_per, :]
        for dev in range(1, N_DEV):
            acc = acc + g_ref[dev * m_per:(dev + 1) * m_per, :]
        o_ref[...] = acc

    return pl.pallas_call(body, out_shape=_sds((m_per, gathered.shape[1]), F32), name=name)(gathered)


def _adamw(name, w, g, m, v):
    r, cdim = w.shape
    if r % 8 == 0:
        tr, tcol = _tile(r, 256, 8), cdim
    else:
        tr, tcol = r, (_tile(cdim, 256, LANES) if cdim % LANES == 0 else cdim)
    blk = pl.BlockSpec((tr, tcol), lambda i, j: (i, j))
    grid = (r // tr, cdim // tcol)
    bc1 = 1.0 - ADAM_B1 ** ADAM_STEP
    bc2 = 1.0 - ADAM_B2 ** ADAM_STEP

    def body(w_ref, g_ref, m_ref, v_ref, d_ref, nm_ref, nv_ref):
        gv = g_ref[...]
        nm = ADAM_B1 * m_ref[...] + (1.0 - ADAM_B1) * gv
        nv = ADAM_B2 * v_ref[...] + (1.0 - ADAM_B2) * (gv * gv)
        d_ref[...] = -ADAM_LR * ((nm / bc1) / (jnp.sqrt(nv / bc2) + ADAM_EPS) + ADAM_WD * w_ref[...])
        nm_ref[...] = nm
        nv_ref[...] = nv

    shape = _sds(w.shape, F32)
    return pl.pallas_call(
        body,
        out_shape=(shape, shape, shape),
        grid=grid,
        in_specs=[blk] * 4,
        out_specs=(blk, blk, blk),
        compiler_params=pltpu.CompilerParams(dimension_semantics=("parallel", "parallel")),
        name=name,
    )(w, g, m, v)


def _cat_cols(g):
    return jnp.transpose(g, (1, 0, 2)).reshape(g.shape[1], N_CHIPS * g.shape[2])


def _split_cols(a):
    r, c4 = a.shape
    return jnp.transpose(a.reshape(r, N_CHIPS, c4 // N_CHIPS), (1, 0, 2))


def _local_step(x, target, w_int, w_oc, w_oa, w_o, w_up, w_down, cmw, cfw, g1, b_f, b_gate, g2, gf):
    batch, seq, d = x.shape
    t = batch * seq
    cw = d // 2
    fh = w_down.shape[0]
    tc = LANES
    nct = cw // tc
    tq = min(512, seq)
    pc_w, qkv_w, gl_w = 3 * cw, 3 * ATTN_WIDTH, 2 * d
    qkv_off, gl_off, f_off = pc_w, pc_w + qkv_w, pc_w + qkv_w + gl_w
    width = f_off + F_PAD
    f_col = pc_w + qkv_w

    w_pc = w_int[:pc_w].reshape(3, nct, tc, d).transpose(1, 0, 2, 3).reshape(pc_w, d)
    w_qkv = w_int[pc_w:f_col].reshape(3, HEAD_PAIRS, LANES, d).transpose(1, 0, 2, 3).reshape(qkv_w, d)
    w_f = jnp.pad(w_int[f_col:f_col + HEADS], ((0, F_PAD - HEADS), (0, 0)))
    w_inp = jnp.concatenate([w_pc, w_qkv, w_int[f_col + HEADS:], w_f], axis=0)
    bf_pad = jnp.pad(b_f, ((0, 0), (0, F_PAD - HEADS)))

    x2d = x.reshape(t, d)
    tgt2d = target.reshape(t, d)

    h1 = _rms_fwd("norm_mix", x2d, g1)
    pc = _mm("proj_conv", h1, w_inp, "nt", F32, m=t, n=pc_w, k=d, b_roff=0)
    qkv = _mm("proj_qkv", h1, w_inp, "nt", BF16, m=t, n=qkv_w, k=d, b_roff=qkv_off)
    gl = _mm("proj_gate", h1, w_inp, "nt", F32, m=t, n=gl_w, k=d, b_roff=gl_off)
    fl = _mm("proj_forget", h1, w_inp, "nt", F32, m=t, n=F_PAD, k=d, b_roff=f_off)
    a_c = _conv_fwd("conv_mix", pc, cmw, batch, seq, tc)
    f_cum = _forget_fwd("forget_cumsum", fl, bf_pad, batch, seq)
    frow = f_cum.reshape(batch, HEAD_PAIRS, 2, seq)
    o, lse = _attn_fwd("attn_fwd", qkv, frow, batch, seq, tq)
    ycat = _mm("out_conv", a_c, w_oc, "nn", F32, m=t, n=d, k=cw, o_off=0, o_width=2 * d)
    ycat = _mm("out_attn", o, w_oa, "nn", F32, m=t, n=d, k=ATTN_WIDTH, out=ycat, o_off=d)
    mg = _merge_fwd("gate_merge", ycat, gl, b_gate)
    x2 = _mm("mix_out", mg, w_o, "nn", F32, m=t, n=d, k=d, add=x2d)
    h2 = _rms_fwd("norm_ffn", x2, g2)
    tcf = min(2 * LANES, fh)
    hmid, ua, ub = _ffn_up_act("ffn_up_act", h2, _cat_cols(w_up), cfw, batch, seq, tcf)
    x3 = _mm("ffn_down", hmid, w_down, "nn", F32, m=t, n=d, k=fh, add=x2, tk=4096)

    dx3, dx3b, loss_row, d_gf = _final_loss("final_loss", x3, gf.reshape(1, d), tgt2d)
    dw_down = _mm("dw_down", hmid, dx3b, "tn", F32, m=fh, n=d, k=t, tm=1408, tk=2048)
    du_a, du_b, d_cfw = _ffn_bwd("d_ffn", dx3b, w_down, ua, ub, cfw, batch, seq, tcf)
    ws = w_up.shape[2]
    dw_up = _mm("dw_up_a", h2, du_a, "tn", F32, m=d, n=fh, k=t, tn=ws, tk=2048, o3=N_CHIPS)
    dw_up = _mm("dw_up_b", h2, du_b, "tn", F32, m=d, n=fh, k=t, tn=ws, tk=2048, o3=N_CHIPS, out=dw_up, o_off=fh)
    dh2 = _mm("d_norm_ffn_a", du_a, w_up, "nt", F32, m=t, n=d, k=fh, b_off=0, b3=True)
    dh2 = _mm("d_norm_ffn_b", du_b, w_up, "nt", F32, m=t, n=d, k=fh, b_off=fh, b3=True, add=dh2)
    dx2, d_g2 = _rms_bwd("d_norm_ffn", x2, dh2, g2, dx3)
    dm = _mm("d_merge", dx2, w_o, "nt", F32, m=t, n=d, k=d)
    dw_o = _mm("dw_o", mg, dx2, "tn", F32, m=d, n=d, k=t, tk=2048)
    dproj, dycat, d_bg = _merge_bwd("d_gate_merge", dm, ycat, gl, b_gate, width, gl_off)
    da_c = _mm("d_conv_out", dycat, w_oc, "nt", F32, m=t, n=cw, k=d, a_off=0)
    do = _mm("d_attn_out", dycat, w_oa, "nt", BF16, m=t, n=ATTN_WIDTH, k=d, a_off=d)
    dw_oc = _mm("dw_out_conv", a_c, dycat, "tn", F32, m=cw, n=d, k=t, b_off=0, tk=2048)
    dw_oa = _mm("dw_out_attn", o, dycat, "tn", F32, m=ATTN_WIDTH, n=d, k=t, b_off=d, tk=2048)
    dproj, d_cmw = _conv_bwd("d_conv_mix", da_c, pc, cmw, dproj, batch, seq, tc)
    dproj, d_fkey, d_fquery = _attn_bwd("attn_bwd", qkv, do, o, lse, frow, dproj, qkv_off, batch, seq, tq)
    d_fquery = jnp.pad(jnp.transpose(d_fquery, (1, 0, 2)).reshape(t, HEADS), ((0, 0), (0, LANES - HEADS)))
    dproj, d_bf = _forget_bwd("d_forget", d_fkey.reshape(batch, HEADS, seq), d_fquery, fl, bf_pad, dproj, f_off,
                              batch, seq)
    dw_inp = _mm("dw_in", dproj, h1, "tn", F32, m=width, n=d, k=t, tm=1792, tk=2048)
    dh1 = _mm("d_norm_mix", dproj, w_inp, "nn", F32, m=t, n=d, k=width, tk=1792)
    grad_x, d_g1 = _rms_bwd("d_norm_mix_x", x2d, dh1, g1, dx2)

    d_pc = dw_inp[:pc_w].reshape(nct, 3, tc, d).transpose(1, 0, 2, 3).reshape(pc_w, d)
    d_qkv = dw_inp[qkv_off:gl_off].reshape(HEAD_PAIRS, 3, LANES, d).transpose(1, 0, 2, 3).reshape(qkv_w, d)
    dw_int = jnp.concatenate([d_pc, d_qkv, dw_inp[f_off:f_off + HEADS], dw_inp[gl_off:f_off]], axis=0)
    mats = (dw_int, dw_oc, dw_oa, dw_o, dw_up, dw_down)
    smalls = (d_g1, d_g2, d_gf, d_bg, d_bf, d_cmw, d_cfw)
    return loss_row[0, 0], grad_x.reshape(batch, seq, d), mats, smalls


def _pack_small(parts):
    flat = [p.reshape(-1) for p in parts]
    sizes = [f.shape[0] for f in flat]
    total = sum(sizes)
    padded = -(-total // (8 * LANES)) * (8 * LANES)
    vec = jnp.concatenate(flat + [jnp.zeros((padded - total,), F32)])
    offsets = [sum(sizes[:i]) for i in range(len(sizes))]
    return vec.reshape(padded // LANES, LANES), offsets


def kernel(x, norm_mix_g, w_in, b_f, b_gate, conv_mix_w, w_out_conv, w_out_attn, w_o, norm_ffn_g, w_up, conv_ffn_w, w_down, norm_f_g, loss_target, m_norm_mix_g, m_w_in, m_b_f, m_b_gate, m_conv_mix_w, m_w_out_conv, m_w_out_attn, m_w_o, m_norm_ffn_g, m_w_up, m_conv_ffn_w, m_w_down, m_norm_f_g, v_norm_mix_g, v_w_in, v_b_f, v_b_gate, v_conv_mix_w, v_w_out_conv, v_w_out_attn, v_w_o, v_norm_ffn_g, v_w_up, v_conv_ffn_w, v_w_down, v_norm_f_g):
    d = x.shape[-1]
    chip = 2 * lax.axis_index("x") + lax.axis_index("y")
    place = jnp.stack([chip, lax.axis_index("c")]).astype(jnp.int32)

    t_in, t_m_in, t_v_in = (jnp.transpose(w[0]) for w in (w_in, m_w_in, v_w_in))
    axes = (1, 0, 0, 0, 0, 0)

    bigs = [t_in.astype(BF16)] + [w[0].astype(BF16) for w in (w_out_conv, w_out_attn, w_o, w_up, w_down)]
    gathered = _gather_weights(bigs, axes, [conv_mix_w[0], conv_ffn_w[0]])
    a_in, a_oc, a_oa, a_o, a_up, a_down, a_cmw, a_cfw = gathered
    full_in = a_in.reshape(N_CHIPS * a_in.shape[1], a_in.shape[2])
    full_o = a_o.reshape(N_CHIPS * a_o.shape[1], a_o.shape[2])
    full_down = a_down.reshape(N_CHIPS * a_down.shape[1], a_down.shape[2])

    loss_local, grad_x, mats, smalls = _local_step(
        x, loss_target, full_in, _cat_cols(a_oc), _cat_cols(a_oa), full_o, a_up, full_down,
        _cat_cols(a_cmw), _cat_cols(a_cfw), norm_mix_g, b_f, b_gate, norm_ffn_g, norm_f_g)
    dw_int, dw_oc, dw_oa, dw_o, dw_up, dw_down = mats

    grads = [
        dw_int.reshape(N_CHIPS, dw_int.shape[0] // N_CHIPS, dw_int.shape[1]),
        _split_cols(dw_oc), _split_cols(dw_oa),
        dw_o.reshape(N_CHIPS, dw_o.shape[0] // N_CHIPS, dw_o.shape[1]),
        dw_up,
        dw_down.reshape(N_CHIPS, dw_down.shape[0] // N_CHIPS, dw_down.shape[1]),
    ]
    names = ("w_in", "w_out_conv", "w_out_attn", "w_o", "w_up", "w_down")
    got = _exchange_sibling(grads, axes)
    sums = [_pair_sum("pair_sum_" + nm, place, g, r, ax) for nm, g, r, ax in zip(names, grads, got, axes)]
    arrivals = _exchange_chips(sums)
    halves = [_chip_sum("chip_sum_" + nm, place, g, r, arr, ax)
              for nm, g, r, arr, ax in zip(names, grads, got, arrivals, axes)]
    g_in, g_oc, g_oa, g_o, g_up, g_down = _share_sibling(halves, axes)

    packed, offs = _pack_small(smalls)
    total = _device_sum("device_sum", _gather_small(packed)).reshape(-1)
    shapes = [s.shape for s in smalls]
    d_g1, d_g2, d_gf, d_bg, d_bf, d_cmw, d_cfw = [
        total[o:o + math.prod(sh)].reshape(sh) for o, sh in zip(offs, shapes)]
    d_bf = d_bf[:, :HEADS]
    cw_s, cf_s = conv_mix_w.shape[2], conv_ffn_w.shape[2]
    d_cmw = lax.dynamic_slice(d_cmw, (0, chip * cw_s), (3, cw_s))
    d_cfw = lax.dynamic_slice(d_cfw, (0, chip * cf_s), (3, cf_s))

    loss = lax.psum(loss_local, ("x", "y", "c"))
    order = [
        ("norm_mix_g", norm_mix_g[0:1], d_g1, m_norm_mix_g, v_norm_mix_g),
        ("w_in", t_in, g_in, t_m_in, t_v_in),
        ("b_f", b_f, d_bf, m_b_f, v_b_f),
        ("b_gate", b_gate, d_bg, m_b_gate, v_b_gate),
        ("conv_mix_w", conv_mix_w[0], d_cmw, m_conv_mix_w[0], v_conv_mix_w[0]),
        ("w_out_conv", w_out_conv[0], g_oc, m_w_out_conv[0], v_w_out_conv[0]),
        ("w_out_attn", w_out_attn[0], g_oa, m_w_out_attn[0], v_w_out_attn[0]),
        ("w_o", w_o[0], g_o, m_w_o[0], v_w_o[0]),
        ("norm_ffn_g", norm_ffn_g, d_g2, m_norm_ffn_g, v_norm_ffn_g),
        ("w_up", w_up[0], g_up, m_w_up[0], v_w_up[0]),
        ("conv_ffn_w", conv_ffn_w[0], d_cfw, m_conv_ffn_w[0], v_conv_ffn_w[0]),
        ("w_down", w_down[0], g_down, m_w_down[0], v_w_down[0]),
        ("norm_f_g", norm_f_g.reshape(1, d), d_gf, m_norm_f_g.reshape(1, d), v_norm_f_g.reshape(1, d)),
    ]
    out_shapes = [norm_mix_g.shape, w_in.shape, b_f.shape, b_gate.shape, conv_mix_w.shape, w_out_conv.shape,
                  w_out_attn.shape, w_o.shape, norm_ffn_g.shape, w_up.shape, conv_ffn_w.shape, w_down.shape,
                  norm_f_g.shape]
    g_out, d_out, m_out, v_out = [], [], [], []
    for (nm, w, g, m, v), sh in zip(order, out_shapes):
        g = g.reshape(w.shape)
        delta, new_m, new_v = _adamw("adamw_" + nm, w, g, m.reshape(w.shape), v.reshape(w.shape))
        for dst, val in ((g_out, g), (d_out, delta), (m_out, new_m), (v_out, new_v)):
            dst.append((jnp.transpose(val) if nm == "w_in" else val).reshape(sh))
    return (loss, grad_x, *g_out, *d_out, *m_out, *v_out)
```

```python
import functools
import math

import jax
import jax.numpy as jnp
from jax import lax
from jax.experimental import pallas as pl
from jax.experimental.pallas import tpu as pltpu

F32 = jnp.float32
BF16 = jnp.bfloat16
MESH = pl.DeviceIdType.MESH

EPS = 1e-6
HEADS = 8
HEAD_DIM = 64
ATTN_WIDTH = HEADS * HEAD_DIM
HEAD_PAIRS = HEADS // 2
LANES = 128
F_PAD = 2 * LANES
NEG_BIG = -1e30
N_CHIPS = 4
N_DEV = 8

ADAM_LR = 0.001
ADAM_B1 = 0.9
ADAM_B2 = 0.999
ADAM_EPS = 1e-08
ADAM_WD = 0.01
ADAM_STEP = 10

_DIMS = {
    "nn": (((1,), (0,)), ((), ())),
    "nt": (((1,), (1,)), ((), ())),
    "tn": (((0,), (0,)), ((), ())),
}


def _tile(n, target, mult, also=()):
    best = None
    for t in range(mult, n + 1, mult):
        if n % t == 0 and t <= target and all(o % t == 0 for o in also):
            best = t
    if best is None:
        assert all(o == 0 for o in also), (n, target, mult, also)
        return n
    return best


def _sds(shape, dtype):
    return jax.ShapeDtypeStruct(shape, dtype)


def _mm(name, a, b, mode, out_dtype, *, m, n, k, a_off=0, b_off=0, b_roff=0, b3=False, out=None, o_off=0,
        o_width=None, o3=None, add=None, dep=None, tm=1024, tn=2048, tk=2048):
    wb = b.shape[2] if b3 else None
    if mode == "nn":
        tm = _tile(m, tm, 16)
        tk = _tile(k, tk, LANES, (a_off,))
        tn = wb if b3 else _tile(n, tn, LANES, (b_off, o_off))
        a_spec = pl.BlockSpec((tm, tk), lambda i, j, kk: (i, a_off // tk + kk))
        if b3:
            b_spec = pl.BlockSpec((None, tk, tn), lambda i, j, kk: (b_off // tn + j, kk, 0))
        else:
            b_spec = pl.BlockSpec((tk, tn), lambda i, j, kk: (kk, b_off // tn + j))
    elif mode == "nt":
        tm = _tile(m, tm, 16)
        tk = wb if b3 else _tile(k, tk, LANES, (a_off, b_off))
        tn = _tile(n, tn, LANES, (o_off, b_roff))
        a_spec = pl.BlockSpec((tm, tk), lambda i, j, kk: (i, a_off // tk + kk))
        if b3:
            b_spec = pl.BlockSpec((None, tn, tk), lambda i, j, kk: (b_off // tk + kk, b_roff // tn + j, 0))
        else:
            b_spec = pl.BlockSpec((tn, tk), lambda i, j, kk: (b_roff // tn + j, b_off // tk + kk))
    else:
        tm = _tile(m, tm, LANES, (a_off,))
        tk = _tile(k, tk, 16)
        tn = _tile(n, tn, LANES, (b_off, o_off))
        a_spec = pl.BlockSpec((tk, tm), lambda i, j, kk: (kk, a_off // tm + i))
        b_spec = pl.BlockSpec((tk, tn), lambda i, j, kk: (kk, b_off // tn + j))
    assert m % tm == 0 and n % tn == 0 and k % tk == 0, (name, tm, tn, tk)
    nk = k // tk
    if o3 is not None:
        o_spec = pl.BlockSpec((None, tm, tn), lambda i, j, kk: (o_off // tn + j, i, 0))
        out_sds = _sds((o3, m, tn), out_dtype)
    else:
        o_spec = pl.BlockSpec((tm, tn), lambda i, j, kk: (i, o_off // tn + j))
        width = o_width if o_width is not None else (out.shape[1] if out is not None else n)
        out_sds = _sds((m, width), out_dtype)
    use_acc = nk > 1 and out_dtype != F32
    dims = _DIMS[mode]
    has_add, has_out = add is not None, out is not None

    def body(*refs):
        a_ref, b_ref = refs[0], refs[1]
        pos = 2
        add_ref = None
        if has_add:
            add_ref = refs[pos]
            pos += 1
        if has_out:
            pos += 1
        if dep is not None:
            pos += 1
        o_ref = refs[pos]
        acc_ref = refs[pos + 1] if use_acc else None
        part = lax.dot_general(a_ref[...].astype(BF16), b_ref[...].astype(BF16), dims,
                               preferred_element_type=F32)
        if nk == 1:
            if has_add:
                part = part + add_ref[...]
            o_ref[...] = part.astype(o_ref.dtype)
            return
        kk = pl.program_id(2)
        tgt = acc_ref if use_acc else o_ref

        @pl.when(kk == 0)
        def _():
            tgt[...] = part + add_ref[...] if has_add else part

        @pl.when(kk > 0)
        def _():
            tgt[...] += part

        if use_acc:
            @pl.when(kk == nk - 1)
            def _():
                o_ref[...] = acc_ref[...].astype(o_ref.dtype)

    operands, in_specs = [a, b], [a_spec, b_spec]
    if has_add:
        operands.append(add)
        in_specs.append(pl.BlockSpec((tm, tn), lambda i, j, kk: (i, j)))
    aliases = {}
    if has_out:
        aliases = {len(operands): 0}
        operands.append(out)
        in_specs.append(pl.BlockSpec(memory_space=pl.ANY))
    if dep is not None:
        operands.append(dep)
        in_specs.append(pl.BlockSpec(memory_space=pl.ANY))
    return pl.pallas_call(
        body,
        out_shape=out_sds,
        grid=(m // tm, n // tn, nk),
        in_specs=in_specs,
        out_specs=o_spec,
        scratch_shapes=[pltpu.VMEM((tm, tn), F32)] if use_acc else [],
        input_output_aliases=aliases,
        compiler_params=pltpu.CompilerParams(dimension_semantics=("parallel", "parallel", "arbitrary")),
        name=name,
    )(*operands)


def _rms_fwd(name, x, g):
    t, d = x.shape
    tm = _tile(t, 512, 16)

    def body(x_ref, g_ref, o_ref):
        xv = x_ref[...]
        r = lax.rsqrt(jnp.mean(xv * xv, axis=-1, keepdims=True) + EPS)
        o_ref[...] = ((xv * r) * g_ref[...]).astype(o_ref.dtype)

    return pl.pallas_call(
        body,
        out_shape=_sds((t, d), BF16),
        grid=(t // tm,),
        in_specs=[pl.BlockSpec((tm, d), lambda i: (i, 0)), pl.BlockSpec((1, d), lambda i: (0, 0))],
        out_specs=pl.BlockSpec((tm, d), lambda i: (i, 0)),
        compiler_params=pltpu.CompilerParams(dimension_semantics=("parallel",)),
        name=name,
    )(x, g)


def _rms_bwd(name, x, dh, g, res):
    t, d = x.shape
    tm = _tile(t, 512, 16)

    def body(x_ref, dh_ref, g_ref, res_ref, dx_ref, dg_ref):
        xv = x_ref[...]
        r = lax.rsqrt(jnp.mean(xv * xv, axis=-1, keepdims=True) + EPS)
        xh = xv * r
        dhv = dh_ref[...]
        dxh = dhv * g_ref[...]
        dx_ref[...] = res_ref[...] + r * (dxh - xh * jnp.mean(dxh * xh, axis=-1, keepdims=True))

        @pl.when(pl.program_id(0) == 0)
        def _():
            dg_ref[...] = jnp.zeros_like(dg_ref)

        dg_ref[...] += jnp.sum(dhv * xh, axis=0, keepdims=True)

    row = pl.BlockSpec((tm, d), lambda i: (i, 0))
    vec = pl.BlockSpec((1, d), lambda i: (0, 0))
    return pl.pallas_call(
        body,
        out_shape=(_sds((t, d), F32), _sds((1, d), F32)),
        grid=(t // tm,),
        in_specs=[row, row, vec, row],
        out_specs=(row, vec),
        compiler_params=pltpu.CompilerParams(dimension_semantics=("arbitrary",)),
        name=name,
    )(x, dh, g, res)


def _final_loss(name, x, g, target):
    t, d = x.shape
    tm = _tile(t, 512, 16)

    def body(x_ref, g_ref, t_ref, dx_ref, dxb_ref, loss_ref, dg_ref):
        xv = x_ref[...]
        gv = g_ref[...]
        r = lax.rsqrt(jnp.mean(xv * xv, axis=-1, keepdims=True) + EPS)
        xh = xv * r
        err = xh * gv - t_ref[...]
        dy = err * (1.0 / d)
        dxh = dy * gv
        dx = r * (dxh - xh * jnp.mean(dxh * xh, axis=-1, keepdims=True))
        dx_ref[...] = dx
        dxb_ref[...] = dx.astype(dxb_ref.dtype)
        per_row = jnp.sum(err * err, axis=-1, keepdims=True) * (0.5 / d)

        @pl.when(pl.program_id(0) == 0)
        def _():
            dg_ref[...] = jnp.zeros_like(dg_ref)
            loss_ref[...] = jnp.zeros_like(loss_ref)

        dg_ref[...] += jnp.sum(dy * xh, axis=0, keepdims=True)
        loss_ref[...] += jnp.sum(per_row, axis=0, keepdims=True)

    row = pl.BlockSpec((tm, d), lambda i: (i, 0))
    vec = pl.BlockSpec((1, d), lambda i: (0, 0))
    return pl.pallas_call(
        body,
        out_shape=(_sds((t, d), F32), _sds((t, d), BF16), _sds((1, LANES), F32), _sds((1, d), F32)),
        grid=(t // tm,),
        in_specs=[row, vec, row],
        out_specs=(row, row, pl.BlockSpec((1, LANES), lambda i: (0, 0)), vec),
        compiler_params=pltpu.CompilerParams(dimension_semantics=("arbitrary",)),
        name=name,
    )(x, g, target)


def _shift_down(z, k):
    row = lax.broadcasted_iota(jnp.int32, z.shape, 0)
    return jnp.where(row >= k, pltpu.roll(z, k, axis=0), 0.0)


def _shift_up(z, k):
    s = z.shape[0]
    row = lax.broadcasted_iota(jnp.int32, z.shape, 0)
    return jnp.where(row < s - k, pltpu.roll(z, s - k, axis=0), 0.0)


def _conv3(z, w):
    return (w[2:3] * z + w[0:1] * _shift_down(z, 2)) + w[1:2] * _shift_down(z, 1)


def _conv3_t(dz, w):
    return (w[2:3] * dz + w[0:1] * _shift_up(dz, 2)) + w[1:2] * _shift_up(dz, 1)


def _conv_fwd(name, pc, w, batch, seq, tc):
    cw = w.shape[1]
    nct = cw // tc

    def body(pc_ref, w_ref, o_ref):
        cb = pc_ref[:, 0:tc]
        z = pc_ref[:, tc:2 * tc] * pc_ref[:, 2 * tc:3 * tc]
        o_ref[...] = (cb * _conv3(z, w_ref[...])).astype(o_ref.dtype)

    return pl.pallas_call(
        body,
        out_shape=_sds((batch * seq, cw), BF16),
        grid=(batch, nct),
        in_specs=[pl.BlockSpec((seq, 3 * tc), lambda b, j: (b, j)), pl.BlockSpec((3, tc), lambda b, j: (0, j))],
        out_specs=pl.BlockSpec((seq, tc), lambda b, j: (b, j)),
        compiler_params=pltpu.CompilerParams(dimension_semantics=("parallel", "parallel")),
        name=name,
    )(pc, w)


def _conv_bwd(name, da, pc, w, dproj, batch, seq, tc):
    cw = w.shape[1]
    nct = cw // tc

    def body(da_ref, pc_ref, w_ref, _, dpc_ref, dw_ref):
        wv = w_ref[...]
        cb = pc_ref[:, 0:tc]
        cc = pc_ref[:, tc:2 * tc]
        cin = pc_ref[:, 2 * tc:3 * tc]
        z = cc * cin
        dav = da_ref[...]
        du = dav * cb
        dz = _conv3_t(du, wv)
        dpc_ref[:, 0:tc] = (dav * _conv3(z, wv)).astype(dpc_ref.dtype)
        dpc_ref[:, tc:2 * tc] = (dz * cin).astype(dpc_ref.dtype)
        dpc_ref[:, 2 * tc:3 * tc] = (dz * cc).astype(dpc_ref.dtype)

        @pl.when(pl.program_id(1) == 0)
        def _():
            dw_ref[...] = jnp.zeros_like(dw_ref)

        dw_ref[0:1, :] += jnp.sum(du * _shift_down(z, 2), axis=0, keepdims=True)
        dw_ref[1:2, :] += jnp.sum(du * _shift_down(z, 1), axis=0, keepdims=True)
        dw_ref[2:3, :] += jnp.sum(du * z, axis=0, keepdims=True)

    return pl.pallas_call(
        body,
        out_shape=(_sds(dproj.shape, dproj.dtype), _sds((3, cw), F32)),
        grid=(nct, batch),
        in_specs=[
            pl.BlockSpec((seq, tc), lambda j, b: (b, j)),
            pl.BlockSpec((seq, 3 * tc), lambda j, b: (b, j)),
            pl.BlockSpec((3, tc), lambda j, b: (0, j)),
            pl.BlockSpec(memory_space=pl.ANY),
        ],
        out_specs=(pl.BlockSpec((seq, 3 * tc), lambda j, b: (b, j)), pl.BlockSpec((3, tc), lambda j, b: (0, j))),
        input_output_aliases={3: 0},
        compiler_params=pltpu.CompilerParams(dimension_semantics=("parallel", "arbitrary")),
        name=name,
    )(da, pc, w, dproj)


def _ffn_up_act(name, h2, w_up, w, batch, seq, tc):
    d = h2.shape[1]
    fh = w.shape[1] // 2
    nf = fh // tc

    def body(h_ref, ma_ref, mb_ref, wa_ref, wb_ref, o_ref, ua_ref, ub_ref):
        hv = h_ref[...]
        ua = _dot(hv, ma_ref[...], "nn")
        ub = _dot(hv, mb_ref[...], "nn")
        ua_ref[...] = ua.astype(ua_ref.dtype)
        ub_ref[...] = ub.astype(ub_ref.dtype)
        a = _conv3(ua, wa_ref[...])
        b = _conv3(ub, wb_ref[...])
        o_ref[...] = (a * jax.nn.sigmoid(a) * b).astype(o_ref.dtype)

    act = pl.BlockSpec((seq, tc), lambda b, j: (b, j))
    shape = _sds((batch * seq, fh), BF16)
    return pl.pallas_call(
        body,
        out_shape=(shape, shape, shape),
        grid=(batch, nf),
        in_specs=[
            pl.BlockSpec((seq, d), lambda b, j: (b, 0)),
            pl.BlockSpec((d, tc), lambda b, j: (0, j)),
            pl.BlockSpec((d, tc), lambda b, j: (0, nf + j)),
            pl.BlockSpec((3, tc), lambda b, j: (0, j)),
            pl.BlockSpec((3, tc), lambda b, j: (0, nf + j)),
        ],
        out_specs=(act, act, act),
        compiler_params=pltpu.CompilerParams(dimension_semantics=("parallel", "parallel")),
        name=name,
    )(h2, w_up, w_up, w, w)


def _ffn_bwd(name, dx, w_down, ua, ub, w, batch, seq, tc):
    d = dx.shape[1]
    fh = w.shape[1] // 2
    nf = fh // tc

    def body(dx_ref, md_ref, ua_ref, ub_ref, wa_ref, wb_ref, dua_ref, dub_ref, dw_ref):
        j = pl.program_id(1)
        uav, ubv, wa, wb = ua_ref[...].astype(F32), ub_ref[...].astype(F32), wa_ref[...], wb_ref[...]
        dhv = _dot(dx_ref[...].astype(BF16), md_ref[...], "nt")
        a = _conv3(uav, wa)
        b = _conv3(ubv, wb)
        sg = jax.nn.sigmoid(a)
        da = dhv * b * (sg * (1.0 + a * (1.0 - sg)))
        db = dhv * (a * sg)
        dua_ref[...] = _conv3_t(da, wa).astype(dua_ref.dtype)
        dub_ref[...] = _conv3_t(db, wb).astype(dub_ref.dtype)

        @pl.when((pl.program_id(0) == 0) & (j == 0))
        def _():
            dw_ref[...] = jnp.zeros_like(dw_ref)

        for off, dv, uv in ((0, da, uav), (fh, db, ubv)):
            cols = pl.ds(pl.multiple_of(off + j * tc, LANES), tc)
            dw_ref[0:1, cols] += jnp.sum(dv * _shift_down(uv, 2), axis=0, keepdims=True)
            dw_ref[1:2, cols] += jnp.sum(dv * _shift_down(uv, 1), axis=0, keepdims=True)
            dw_ref[2:3, cols] += jnp.sum(dv * uv, axis=0, keepdims=True)

    act = pl.BlockSpec((seq, tc), lambda b, j: (b, j))
    shape = _sds((batch * seq, fh), BF16)
    return pl.pallas_call(
        body,
        out_shape=(shape, shape, _sds((3, 2 * fh), F32)),
        grid=(batch, nf),
        in_specs=[
            pl.BlockSpec((seq, d), lambda b, j: (b, 0)),
            pl.BlockSpec((tc, d), lambda b, j: (j, 0)),
            act,
            act,
            pl.BlockSpec((3, tc), lambda b, j: (0, j)),
            pl.BlockSpec((3, tc), lambda b, j: (0, nf + j)),
        ],
        out_specs=(act, act, pl.BlockSpec((3, 2 * fh), lambda b, j: (0, 0))),
        compiler_params=pltpu.CompilerParams(dimension_semantics=("arbitrary", "arbitrary")),
        name=name,
    )(dx, w_down, ua, ub, w, w)


def _merge_fwd(name, ycat, gl, bg):
    t, d2 = ycat.shape
    d = d2 // 2
    tm = _tile(t, 256, 16)

    def body(y_ref, gl_ref, bg_ref, o_ref):
        g = jax.nn.sigmoid(gl_ref[...] + bg_ref[...])
        prod = g * y_ref[...]
        o_ref[...] = (prod[:, 0:d] + prod[:, d:d2]).astype(o_ref.dtype)

    row = pl.BlockSpec((tm, d2), lambda i: (i, 0))
    return pl.pallas_call(
        body,
        out_shape=_sds((t, d), BF16),
        grid=(t // tm,),
        in_specs=[row, row, pl.BlockSpec((1, d2), lambda i: (0, 0))],
        out_specs=pl.BlockSpec((tm, d), lambda i: (i, 0)),
        compiler_params=pltpu.CompilerParams(dimension_semantics=("parallel",)),
        name=name,
    )(ycat, gl, bg)


def _merge_bwd(name, dm, ycat, gl, bg, width, gl_off):
    t, d2 = ycat.shape
    d = d2 // 2
    tm = _tile(t, 512, 16)
    wb = math.gcd(gl_off, d)
    nw = d // wb

    def body(dm_ref, y_ref, gl_ref, bg_ref, dgl_ref, dy_ref, dbg_ref):
        g = jax.nn.sigmoid(gl_ref[...] + bg_ref[...])
        dmv = dm_ref[...]
        dgl = dmv * y_ref[...] * (g * (1.0 - g))
        dgl_ref[...] = dgl.astype(dgl_ref.dtype)
        dy_ref[...] = (dmv * g).astype(dy_ref.dtype)

        @pl.when(pl.program_id(2) == 0)
        def _():
            dbg_ref[...] = jnp.zeros_like(dbg_ref)

        dbg_ref[...] += jnp.sum(dgl, axis=0, keepdims=True)

    half = pl.BlockSpec((tm, wb), lambda h, j, i: (i, h * nw + j))
    vec = pl.BlockSpec((1, wb), lambda h, j, i: (0, h * nw + j))
    return pl.pallas_call(
        body,
        out_shape=(_sds((t, width), BF16), _sds((t, d2), BF16), _sds((1, d2), F32)),
        grid=(2, nw, t // tm),
        in_specs=[pl.BlockSpec((tm, wb), lambda h, j, i: (i, j)), half, half, vec],
        out_specs=(pl.BlockSpec((tm, wb), lambda h, j, i: (i, gl_off // wb + h * nw + j)), half, vec),
        compiler_params=pltpu.CompilerParams(dimension_semantics=("parallel", "parallel", "arbitrary")),
        name=name,
    )(dm, ycat, gl, bg)


def _log_sigmoid(z):
    return jnp.minimum(z, 0.0) - jnp.log1p(jnp.exp(-jnp.abs(z)))


def _forget_fwd(name, fl, bf, batch, seq):
    def body(fl_ref, bf_ref, o_ref):
        lf = _log_sigmoid(fl_ref[:, 0:LANES] + bf_ref[:, 0:LANES])
        acc = lf.T[0:HEADS, :]
        lane = lax.broadcasted_iota(jnp.int32, acc.shape, 1)
        k = 1
        while k < seq:
            acc = acc + jnp.where(lane >= k, pltpu.roll(acc, k, axis=1), 0.0)
            k *= 2
        o_ref[...] = acc

    return pl.pallas_call(
        body,
        out_shape=_sds((batch, HEADS, seq), F32),
        grid=(batch,),
        in_specs=[pl.BlockSpec((seq, F_PAD), lambda b: (b, 0)), pl.BlockSpec((1, F_PAD), lambda b: (0, 0))],
        out_specs=pl.BlockSpec((None, HEADS, seq), lambda b: (b, 0, 0)),
        compiler_params=pltpu.CompilerParams(dimension_semantics=("parallel",)),
        name=name,
    )(fl, bf)


def _forget_bwd(name, d_key, d_query, fl, bf, dproj, f_off, batch, seq):
    nfb = F_PAD // LANES

    def body(dk_ref, dq_ref, fl_ref, bf_ref, _, df_ref, dbf_ref):
        jj = pl.program_id(1)
        key_t = jnp.concatenate([dk_ref[...], jnp.zeros((LANES - HEADS, seq), F32)], axis=0).T
        acc = dq_ref[...] - key_t
        row = lax.broadcasted_iota(jnp.int32, acc.shape, 0)
        k = 1
        while k < seq:
            acc = acc + jnp.where(row < seq - k, pltpu.roll(acc, seq - k, axis=0), 0.0)
            k *= 2
        z = fl_ref[:, 0:LANES] + bf_ref[:, 0:LANES]
        col = lax.broadcasted_iota(jnp.int32, acc.shape, 1)
        df = jnp.where(col < HEADS, acc * jax.nn.sigmoid(-z), 0.0)
        df = jnp.where(jj == 0, df, 0.0)
        df_ref[...] = df.astype(df_ref.dtype)

        @pl.when((pl.program_id(0) == 0) & (jj == 0))
        def _():
            dbf_ref[...] = jnp.zeros_like(dbf_ref)

        dbf_ref[...] += jnp.sum(df, axis=0, keepdims=True)

    return pl.pallas_call(
        body,
        out_shape=(_sds(dproj.shape, dproj.dtype), _sds((1, LANES), F32)),
        grid=(batch, nfb),
        in_specs=[
            pl.BlockSpec((None, HEADS, seq), lambda b, j: (b, 0, 0)),
            pl.BlockSpec((seq, LANES), lambda b, j: (b, 0)),
            pl.BlockSpec((seq, F_PAD), lambda b, j: (b, 0)),
            pl.BlockSpec((1, F_PAD), lambda b, j: (0, 0)),
            pl.BlockSpec(memory_space=pl.ANY),
        ],
        out_specs=(pl.BlockSpec((seq, LANES), lambda b, j: (b, f_off // LANES + j)),
                   pl.BlockSpec((1, LANES), lambda b, j: (0, 0))),
        input_output_aliases={4: 0},
        compiler_params=pltpu.CompilerParams(dimension_semantics=("arbitrary", "arbitrary")),
        name=name,
    )(d_key, d_query, fl, bf, dproj)


def _dot(a, b, mode):
    return lax.dot_general(a, b, _DIMS[mode], preferred_element_type=F32)


def _attn_fwd(name, qkv, frow, batch, seq, tq):
    nq = seq // tq
    scale = 1.0 / math.sqrt(HEAD_DIM)

    def body(q_ref, k_ref, v_ref, f_ref, o_ref, lse_ref):
        i = pl.program_id(2)
        lane = lax.broadcasted_iota(jnp.int32, (1, LANES), 1)
        lo = lane < HEAD_DIM
        qs = q_ref[...] * scale
        qh = (jnp.where(lo, qs, 0.0).astype(BF16), jnp.where(lo, 0.0, qs).astype(BF16))
        row = lax.broadcasted_iota(jnp.int32, (tq, tq), 0)
        col = lax.broadcasted_iota(jnp.int32, (tq, tq), 1)

        def step(j, carry, diag):
            m0, l0, m1, l1, acc = carry
            start = pl.multiple_of(j * tq, tq)
            kj = k_ref[pl.ds(start, tq), :]
            vj = v_ref[pl.ds(start, tq), :]
            ms, ls, pvs, alphas = [], [], [], []
            for h, (m_old, l_old) in enumerate(((m0, l0), (m1, l1))):
                s = _dot(qh[h], kj, "nt") - f_ref[h:h + 1, pl.ds(start, tq)]
                if diag:
                    s = jnp.where(col <= row, s, NEG_BIG)
                m_new = jnp.maximum(m_old, jnp.max(s, axis=1, keepdims=True))
                p = jnp.exp(s - m_new)
                alpha = jnp.exp(m_old - m_new)
                ls.append(alpha * l_old + jnp.sum(p, axis=1, keepdims=True))
                ms.append(m_new)
                alphas.append(alpha)
                vh = jnp.where(lo, vj, 0.0) if h == 0 else jnp.where(lo, 0.0, vj)
                pvs.append(_dot(p.astype(BF16), vh.astype(BF16), "nn"))
            acc = acc * jnp.where(lo, alphas[0], alphas[1]) + (pvs[0] + pvs[1])
            return ms[0], ls[0], ms[1], ls[1], acc

        neg = jnp.full((tq, 1), NEG_BIG, F32)
        zero = jnp.zeros((tq, 1), F32)
        init = (neg, zero, neg, zero, jnp.zeros((tq, LANES), F32))
        carry = lax.fori_loop(0, i, lambda j, c: step(j, c, False), init)
        m0, l0, m1, l1, acc = step(i, carry, True)
        o_ref[...] = (acc / jnp.where(lo, l0, l1)).astype(o_ref.dtype)
        lse_ref[:, 0:1] = m0 + jnp.log(l0)
        lse_ref[:, 1:2] = m1 + jnp.log(l1)

    return pl.pallas_call(
        body,
        out_shape=(_sds((batch * seq, ATTN_WIDTH), BF16), _sds((HEAD_PAIRS, batch * seq, 2), F32)),
        grid=(batch, HEAD_PAIRS, nq),
        in_specs=[
            pl.BlockSpec((tq, LANES), lambda b, hp, i: (b * nq + i, 3 * hp)),
            pl.BlockSpec((seq, LANES), lambda b, hp, i: (b, 3 * hp + 1)),
            pl.BlockSpec((seq, LANES), lambda b, hp, i: (b, 3 * hp + 2)),
            pl.BlockSpec((None, None, 2, seq), lambda b, hp, i: (b, hp, 0, 0)),
        ],
        out_specs=(
            pl.BlockSpec((tq, LANES), lambda b, hp, i: (b * nq + i, hp)),
            pl.BlockSpec((None, tq, 2), lambda b, hp, i: (hp, b * nq + i, 0)),
        ),
        compiler_params=pltpu.CompilerParams(dimension_semantics=("parallel", "parallel", "parallel")),
        name=name,
    )(qkv, qkv, qkv, frow)


def _attn_bwd(name, qkv, do, o, lse, frow, dproj, qkv_off, batch, seq, tq):
    nq = seq // tq
    scale = 1.0 / math.sqrt(HEAD_DIM)

    def body(q_ref, k_ref, v_ref, do_ref, o_ref, lse_ref, f_ref, _, dqkv_ref, df_ref, drow_ref,
             dq_acc, dk_acc, dv_acc, df_acc):
        j = pl.program_id(2)
        lane = lax.broadcasted_iota(jnp.int32, (1, LANES), 1)
        lo = lane < HEAD_DIM
        masks = (lo, jnp.logical_not(lo))
        row = lax.broadcasted_iota(jnp.int32, (tq, tq), 0)
        col = lax.broadcasted_iota(jnp.int32, (tq, tq), 1)

        @pl.when(j == 0)
        def _():
            dq_acc[...] = jnp.zeros_like(dq_acc)
            drow_ref[...] = jnp.zeros_like(drow_ref)

        dk_acc[...] = jnp.zeros_like(dk_acc)
        dv_acc[...] = jnp.zeros_like(dv_acc)
        df_acc[...] = jnp.zeros_like(df_acc)
        kj = k_ref[...]
        vj = v_ref[...]
        kstart = pl.multiple_of(j * tq, tq)
        kh = tuple(jnp.where(mk, kj, 0.0).astype(BF16) for mk in masks)

        def step(i, diag):
            start = pl.multiple_of(i * tq, tq)
            rows = pl.ds(start, tq)
            qi = q_ref[rows, :] * scale
            doi = do_ref[rows, :]
            prod = doi.astype(F32) * o_ref[rows, :].astype(F32)
            lse_i = lse_ref[rows, :]
            dq_i = jnp.zeros((tq, LANES), F32)
            for h, mk in enumerate(masks):
                q_h = jnp.where(mk, qi, 0.0).astype(BF16)
                do_h = jnp.where(mk, doi, 0.0).astype(BF16)
                delta = jnp.sum(jnp.where(mk, prod, 0.0), axis=1, keepdims=True)
                s = _dot(q_h, kj, "nt") - f_ref[h:h + 1, pl.ds(kstart, tq)]
                p = jnp.exp(s - lse_i[:, h:h + 1])
                if diag:
                    p = jnp.where(col <= row, p, 0.0)
                ds = p * (_dot(do_h, vj, "nt") - delta)
                df_acc[h:h + 1, :] += jnp.sum(ds, axis=0, keepdims=True)
                drow_ref[rows, h:h + 1] += jnp.sum(ds, axis=1, keepdims=True)
                dsb = ds.astype(BF16)
                dv_acc[...] += _dot(p.astype(BF16), do_h, "tn")
                dk_acc[...] += _dot(dsb, q_h, "tn")
                dq_i = dq_i + _dot(dsb, kh[h], "nn")
            dq_acc[rows, :] += dq_i

        step(j, True)
        lax.fori_loop(j + 1, nq, lambda i, c: (step(i, False), c)[1], 0)
        dqkv_ref[:, 0:LANES] = (dq_acc[pl.ds(kstart, tq), :] * scale).astype(dqkv_ref.dtype)
        dqkv_ref[:, LANES:2 * LANES] = dk_acc[...].astype(dqkv_ref.dtype)
        dqkv_ref[:, 2 * LANES:3 * LANES] = dv_acc[...].astype(dqkv_ref.dtype)
        df_ref[...] = df_acc[...]

    full = lambda c: pl.BlockSpec((seq, LANES), lambda b, hp, j: (b, c(hp)))
    blk = lambda c: pl.BlockSpec((tq, LANES), lambda b, hp, j: (b * nq + j, c(hp)))
    return pl.pallas_call(
        body,
        out_shape=(_sds(dproj.shape, dproj.dtype), _sds((batch, HEAD_PAIRS, 2, seq), F32),
                   _sds((HEAD_PAIRS, batch * seq, 2), F32)),
        grid=(batch, HEAD_PAIRS, nq),
        in_specs=[
            full(lambda hp: 3 * hp),
            blk(lambda hp: 3 * hp + 1),
            blk(lambda hp: 3 * hp + 2),
            full(lambda hp: hp),
            full(lambda hp: hp),
            pl.BlockSpec((None, seq, 2), lambda b, hp, j: (hp, b, 0)),
            pl.BlockSpec((None, None, 2, seq), lambda b, hp, j: (b, hp, 0, 0)),
            pl.BlockSpec(memory_space=pl.ANY),
        ],
        out_specs=(
            pl.BlockSpec((tq, 3 * LANES), lambda b, hp, j: (b * nq + j, qkv_off // (3 * LANES) + hp)),
            pl.BlockSpec((None, None, 2, tq), lambda b, hp, j: (b, hp, 0, j)),
            pl.BlockSpec((None, seq, 2), lambda b, hp, j: (hp, b, 0)),
        ),
        scratch_shapes=[
            pltpu.VMEM((seq, LANES), F32),
            pltpu.VMEM((tq, LANES), F32),
            pltpu.VMEM((tq, LANES), F32),
            pltpu.VMEM((2, tq), F32),
        ],
        input_output_aliases={7: 0},
        compiler_params=pltpu.CompilerParams(dimension_semantics=("parallel", "parallel", "arbitrary")),
        name=name,
    )(qkv, qkv, qkv, do, o, lse, frow, dproj)


def _mesh_place():
    x, y, c = lax.axis_index("x"), lax.axis_index("y"), lax.axis_index("c")
    chips = [(1 - x, y), (x, 1 - y), (1 - x, 1 - y)]
    return x, y, c, chips


def _hbm_specs(n):
    return [pl.BlockSpec(memory_space=pl.ANY)] * n


def _half(shape2d, axis, which):
    size = shape2d[axis] // 2
    sl = pl.ds(pl.multiple_of(which * size, 16 if axis == 0 else LANES), size)
    return (sl, slice(None)) if axis == 0 else (slice(None), sl)


def _gather_weights(bigs, axes, smalls):
    nb, ns = len(bigs), len(smalls)
    arrays = list(bigs) + list(smalls)
    n = nb + ns

    def body(*refs):
        ins, outs = refs[:n], refs[n:2 * n]
        send_sems, recv_sems = refs[2 * n:]
        x, y, c, chips = _mesh_place()
        me = 2 * x + y
        sibling = (x, y, 1 - c)

        def half(a, which):
            return _half(arrays[a].shape, axes[a], which)

        def copy(a, k, src, dst, to):
            return pltpu.make_async_remote_copy(src_ref=src, dst_ref=dst, send_sem=send_sems.at[a, k],
                                                recv_sem=recv_sems.at[a, k], device_id=to, device_id_type=MESH)

        sends = []
        for a in range(n):
            for j, chip in enumerate(chips):
                if a < nb:
                    cp = copy(a, j, ins[a].at[half(a, c)], outs[a].at[(me,) + half(a, c)], (*chip, c))
                else:
                    cp = copy(a, j, ins[a], outs[a].at[me], (*chip, c))
                cp.start()
                sends.append(cp)
        for a in range(nb):
            for j, (px, py) in enumerate(chips):
                blk = outs[a].at[(2 * px + py,) + half(a, c)]
                copy(a, j, blk, blk, (px, py, c)).wait_recv()
                fwd = copy(a, 3 + j, blk, blk, sibling)
                fwd.start()
                sends.append(fwd)
        for a in range(nb, n):
            for j, (px, py) in enumerate(chips):
                blk = outs[a].at[2 * px + py]
                copy(a, j, blk, blk, (px, py, c)).wait_recv()
        for a in range(nb):
            for j, (px, py) in enumerate(chips):
                blk = outs[a].at[(2 * px + py,) + half(a, 1 - c)]
                copy(a, 3 + j, blk, blk, sibling).wait_recv()
        for cp in sends:
            cp.wait_send()

    outs = pl.pallas_call(
        body,
        out_shape=tuple(_sds((N_CHIPS,) + a.shape, a.dtype) for a in arrays),
        in_specs=_hbm_specs(n),
        out_specs=tuple(_hbm_specs(n)),
        scratch_shapes=[pltpu.SemaphoreType.DMA((n, 6)), pltpu.SemaphoreType.DMA((n, 6))],
        name="gather_weights",
    )(*arrays)
    me = 2 * lax.axis_index("x") + lax.axis_index("y")
    return tuple(lax.dynamic_update_index_in_dim(o, a, me, 0) for o, a in zip(outs, arrays))


def _gather_small(v):
    m_per, ncol = v.shape

    def body(x_ref, out_ref, send_sems, recv_sems, local_sem):
        x, y, c, chips = _mesh_place()
        me, sibling = (x, y, c), (x, y, 1 - c)

        def rows(px, py, pc):
            return out_ref.at[pl.ds((4 * px + 2 * py + pc) * m_per, m_per), :]

        def copy(k, block, to, src=None):
            return pltpu.make_async_remote_copy(src_ref=rows(*block) if src is None else src, dst_ref=rows(*block),
                                                send_sem=send_sems.at[k], recv_sem=recv_sems.at[k],
                                                device_id=to, device_id_type=MESH)

        mine = pltpu.make_async_copy(x_ref, rows(*me), local_sem)
        mine.start()
        first = [copy(0, me, sibling, src=x_ref)]
        first += [copy(1 + j, me, (*chip, c), src=x_ref) for j, chip in enumerate(chips)]
        for cp in first:
            cp.start()
        passed = [copy(4 + j, (*chip, c), sibling) for j, chip in enumerate(chips)]
        for j, chip in enumerate(chips):
            copy(1 + j, (*chip, c), me).wait_recv()
            passed[j].start()
        copy(0, sibling, me).wait_recv()
        for j, chip in enumerate(chips):
            copy(4 + j, (*chip, 1 - c), me).wait_recv()
        for cp in first + passed:
            cp.wait_send()
        mine.wait()

    return pl.pallas_call(
        body,
        out_shape=_sds((N_DEV * m_per, ncol), v.dtype),
        in_specs=[pl.BlockSpec(memory_space=pltpu.VMEM)],
        out_specs=pl.BlockSpec(memory_space=pltpu.VMEM),
        scratch_shapes=[pltpu.SemaphoreType.DMA((7,)), pltpu.SemaphoreType.DMA((7,)), pltpu.SemaphoreType.DMA],
        name="gather_small",
    )(v)


def _half_shape(shape2d, axis):
    return (shape2d[0] // 2, shape2d[1]) if axis == 0 else (shape2d[0], shape2d[1] // 2)


def _exchange_sibling(name, grads, axes):
    n = len(grads)

    def body(*refs):
        ins, outs = refs[:n], refs[n:2 * n]
        send_sems, recv_sems = refs[2 * n:]
        x, y, c, _ = _mesh_place()
        copies = []
        for a in range(n):
            src = ins[a].at[(slice(None),) + _half(grads[a].shape[1:], axes[a], 1 - c)]
            cp = pltpu.make_async_remote_copy(src_ref=src, dst_ref=outs[a], send_sem=send_sems.at[a],
                                              recv_sem=recv_sems.at[a], device_id=(x, y, 1 - c), device_id_type=MESH)
            cp.start()
            copies.append(cp)
        for cp in copies:
            cp.wait()

    return pl.pallas_call(
        body,
        out_shape=tuple(_sds((N_CHIPS,) + _half_shape(g.shape[1:], ax), g.dtype) for g, ax in zip(grads, axes)),
        in_specs=_hbm_specs(n),
        out_specs=tuple(_hbm_specs(n)),
        scratch_shapes=[pltpu.SemaphoreType.DMA((n,)), pltpu.SemaphoreType.DMA((n,))],
        name=name,
    )(*grads)


_HBM = pl.BlockSpec(memory_space=pltpu.HBM)
_SEM = pl.BlockSpec(memory_space=pltpu.SEMAPHORE)
_EFFECT = pltpu.SideEffectType.DATAFLOW_SIDE_EFFECTING


def _chip_copies(kind, srcs, lands, send_sems, recv_sems):
    x, y, c, chips = _mesh_place()
    copies = []
    for a in range(len(srcs)):
        for j, (px, py) in enumerate(chips):
            if kind == "gather":
                src, dst = srcs[a], lands[a].at[2 * x + y]
            else:
                src, dst = srcs[a].at[2 * px + py], lands[a].at[j]
            copies.append(pltpu.make_async_remote_copy(src_ref=src, dst_ref=dst, send_sem=send_sems.at[3 * a + j],
                                                       recv_sem=recv_sems.at[3 * a + j], device_id=(px, py, c),
                                                       device_id_type=MESH))
    return copies


def _chips_start(name, kind, srcs):
    n = len(srcs)
    slots = N_CHIPS if kind == "gather" else 3
    lands = [lax.empty((slots,) + (s.shape if kind == "gather" else s.shape[1:]), s.dtype) for s in srcs]

    def body(*refs):
        for cp in _chip_copies(kind, refs[:n], refs[n:2 * n], refs[2 * n], refs[2 * n + 1]):
            cp.start()
        refs[-1][...] = jnp.zeros_like(refs[-1])

    outs = pl.pallas_call(
        body,
        out_shape=(pltpu.SemaphoreType.DMA((3 * n,)), pltpu.SemaphoreType.DMA((3 * n,)),
                   *[pltpu.HBM(v.shape, v.dtype) for v in (*srcs, *lands)], _sds((8, LANES), F32)),
        in_specs=[_HBM] * (2 * n),
        out_specs=(_SEM, _SEM, *[_HBM] * (2 * n), pl.BlockSpec(memory_space=pltpu.VMEM)),
        input_output_aliases={i: 2 + i for i in range(2 * n)},
        compiler_params=pltpu.CompilerParams(has_side_effects=_EFFECT),
        name=name,
    )(*[pltpu.with_memory_space_constraint(v, pltpu.HBM) for v in (*srcs, *lands)])
    return outs[:-1], outs[-1]


def _chips_wait(name, kind, handles, after):
    send_sems, recv_sems, *thru = handles
    n = len(thru) // 2

    def body(*refs):
        for cp in _chip_copies(kind, refs[:n], refs[n:2 * n], refs[2 * n], refs[2 * n + 1]):
            cp.wait_send()
            cp.wait_recv()

    outs = pl.pallas_call(
        body,
        out_shape=tuple(pltpu.HBM(v.shape, v.dtype) for v in thru),
        in_specs=[_HBM] * (2 * n) + [_SEM, _SEM, pl.BlockSpec(memory_space=pl.ANY)],
        out_specs=tuple([_HBM] * (2 * n)),
        input_output_aliases={i: i for i in range(2 * n)},
        compiler_params=pltpu.CompilerParams(has_side_effects=_EFFECT),
        name=name,
    )(*thru, send_sems, recv_sems, after)
    return outs[n:]


def _share_sibling(name, shards, axes):
    n = len(shards)

    def body(*refs):
        ins, outs = refs[:n], refs[n:2 * n]
        send_sems, recv_sems = refs[2 * n:]
        x, y, c, _ = _mesh_place()
        started = []
        for a in range(n):
            mine = _half(shards[a].shape, axes[a], c)
            theirs = _half(shards[a].shape, axes[a], 1 - c)
            cp = pltpu.make_async_remote_copy(src_ref=ins[a].at[mine], dst_ref=outs[a].at[mine],
                                              send_sem=send_sems.at[a], recv_sem=recv_sems.at[a],
                                              device_id=(x, y, 1 - c), device_id_type=MESH)
            cp.start()
            arrival = pltpu.make_async_remote_copy(src_ref=ins[a].at[theirs], dst_ref=outs[a].at[theirs],
                                                   send_sem=send_sems.at[a], recv_sem=recv_sems.at[a],
                                                   device_id=(x, y, 1 - c), device_id_type=MESH)
            started.append((cp, arrival))
        for cp, arrival in started:
            arrival.wait_recv()
            cp.wait_send()

    return pl.pallas_call(
        body,
        out_shape=tuple(_sds(s.shape, s.dtype) for s in shards),
        in_specs=_hbm_specs(n),
        out_specs=tuple(_hbm_specs(n)),
        scratch_shapes=[pltpu.SemaphoreType.DMA((n,)), pltpu.SemaphoreType.DMA((n,))],
        input_output_aliases={a: a for a in range(n)},
        name=name,
    )(*shards)


def _pair_sum(name, place, g, got, axis):
    hr, hc = got.shape[1:]

    def body(place_ref, g_ref, got_ref, o_ref):
        o_ref[...] = (g_ref[...] + got_ref[...]).astype(o_ref.dtype)

    blk = (None, hr, hc)
    mine = (lambda k, pr: (k, pr[1], 0)) if axis == 0 else (lambda k, pr: (k, 0, pr[1]))
    return pl.pallas_call(
        body,
        out_shape=_sds((N_CHIPS, hr, hc), BF16),
        grid_spec=pltpu.PrefetchScalarGridSpec(
            num_scalar_prefetch=1,
            grid=(N_CHIPS,),
            in_specs=[pl.BlockSpec(blk, mine), pl.BlockSpec(blk, lambda k, pr: (k, 0, 0))],
            out_specs=pl.BlockSpec(blk, lambda k, pr: (k, 0, 0)),
        ),
        compiler_params=pltpu.CompilerParams(dimension_semantics=("parallel",)),
        name=name,
    )(place, g, got)


def _chip_sum(name, place, g, got, arrivals, axis):
    _, r, cdim = g.shape
    hr, hc = got.shape[1:]

    def body(place_ref, g_ref, got_ref, arr_ref, o_ref):
        acc = g_ref[...] + got_ref[...]
        for j in range(3):
            acc = acc + arr_ref[j].astype(F32)
        o_ref[...] = acc

    blk = (None, hr, hc)
    mine = (lambda i, pr: (pr[0], pr[1], 0)) if axis == 0 else (lambda i, pr: (pr[0], 0, pr[1]))
    dest = (lambda i, pr: (pr[1], 0)) if axis == 0 else (lambda i, pr: (0, pr[1]))
    return pl.pallas_call(
        body,
        out_shape=_sds((r, cdim), F32),
        grid_spec=pltpu.PrefetchScalarGridSpec(
            num_scalar_prefetch=1,
            grid=(1,),
            in_specs=[
                pl.BlockSpec(blk, mine),
                pl.BlockSpec(blk, lambda i, pr: (pr[0], 0, 0)),
                pl.BlockSpec((3, hr, hc), lambda i, pr: (0, 0, 0)),
            ],
            out_specs=pl.BlockSpec((hr, hc), dest),
        ),
        compiler_params=pltpu.CompilerParams(dimension_semantics=("arbitrary",)),
        name=name,
    )(place, g, got, arrivals)


def _device_sum(name, gathered):
    m_per = gathered.shape[0] // N_DEV

    def body(g_ref, o_ref):
        acc = g_ref[0:m_per, :]
        for dev in range(1, N_DEV):
            acc = acc + g_ref[dev * m_per:(dev + 1) * m_per, :]
        o_ref[...] = acc

    return pl.pallas_call(body, out_shape=_sds((m_per, gathered.shape[1]), F32), name=name)(gathered)


def _adamw(name, w, g, m, v):
    r, cdim = w.shape
    if r % 8 == 0:
        tr, tcol = _tile(r, 256, 8), cdim
    else:
        tr, tcol = r, (_tile(cdim, 256, LANES) if cdim % LANES == 0 else cdim)
    blk = pl.BlockSpec((tr, tcol), lambda i, j: (i, j))
    grid = (r // tr, cdim // tcol)
    bc1 = 1.0 - ADAM_B1 ** ADAM_STEP
    bc2 = 1.0 - ADAM_B2 ** ADAM_STEP

    def body(w_ref, g_ref, m_ref, v_ref, d_ref, nm_ref, nv_ref):
        gv = g_ref[...]
        nm = ADAM_B1 * m_ref[...] + (1.0 - ADAM_B1) * gv
        nv = ADAM_B2 * v_ref[...] + (1.0 - ADAM_B2) * (gv * gv)
        d_ref[...] = -ADAM_LR * ((nm / bc1) / (jnp.sqrt(nv / bc2) + ADAM_EPS) + ADAM_WD * w_ref[...])
        nm_ref[...] = nm
        nv_ref[...] = nv

    shape = _sds(w.shape, F32)
    return pl.pallas_call(
        body,
        out_shape=(shape, shape, shape),
        grid=grid,
        in_specs=[blk] * 4,
        out_specs=(blk, blk, blk),
        compiler_params=pltpu.CompilerParams(dimension_semantics=("parallel", "parallel")),
        name=name,
    )(w, g, m, v)


def _cat_cols(g):
    return jnp.transpose(g, (1, 0, 2)).reshape(g.shape[1], N_CHIPS * g.shape[2])


def _split_cols(a):
    r, c4 = a.shape
    return jnp.transpose(a.reshape(r, N_CHIPS, c4 // N_CHIPS), (1, 0, 2))


def _local_step(x, target, w_int, w_oc, w_oa, late_weights, cmw, cfw, g1, b_f, b_gate, g2, gf,
                ffn_grads_ready, mix_grads_ready):
    batch, seq, d = x.shape
    t = batch * seq
    cw = d // 2
    fh = cfw.shape[1] // 2
    tc = LANES
    nct = cw // tc
    tq = min(512, seq)
    pc_w, qkv_w, gl_w = 3 * cw, 3 * ATTN_WIDTH, 2 * d
    qkv_off, gl_off, f_off = pc_w, pc_w + qkv_w, pc_w + qkv_w + gl_w
    width = f_off + F_PAD
    f_col = pc_w + qkv_w

    w_pc = w_int[:pc_w].reshape(3, nct, tc, d).transpose(1, 0, 2, 3).reshape(pc_w, d)
    w_qkv = w_int[pc_w:f_col].reshape(3, HEAD_PAIRS, LANES, d).transpose(1, 0, 2, 3).reshape(qkv_w, d)
    w_f = jnp.pad(w_int[f_col:f_col + HEADS], ((0, F_PAD - HEADS), (0, 0)))
    w_inp = jnp.concatenate([w_pc, w_qkv, w_int[f_col + HEADS:], w_f], axis=0)
    bf_pad = jnp.pad(b_f, ((0, 0), (0, F_PAD - HEADS)))

    x2d = x.reshape(t, d)
    tgt2d = target.reshape(t, d)

    h1 = _rms_fwd("norm_mix", x2d, g1)
    pc = _mm("proj_conv", h1, w_inp, "nt", F32, m=t, n=pc_w, k=d, b_roff=0)
    qkv = _mm("proj_qkv", h1, w_inp, "nt", BF16, m=t, n=qkv_w, k=d, b_roff=qkv_off)
    gl = _mm("proj_gate", h1, w_inp, "nt", F32, m=t, n=gl_w, k=d, b_roff=gl_off)
    fl = _mm("proj_forget", h1, w_inp, "nt", F32, m=t, n=F_PAD, k=d, b_roff=f_off)
    a_c = _conv_fwd("conv_mix", pc, cmw, batch, seq, tc)
    f_cum = _forget_fwd("forget_cumsum", fl, bf_pad, batch, seq)
    frow = f_cum.reshape(batch, HEAD_PAIRS, 2, seq)
    o, lse = _attn_fwd("attn_fwd", qkv, frow, batch, seq, tq)
    ycat = _mm("out_conv", a_c, w_oc, "nn", F32, m=t, n=d, k=cw, o_off=0, o_width=2 * d)
    ycat = _mm("out_attn", o, w_oa, "nn", F32, m=t, n=d, k=ATTN_WIDTH, out=ycat, o_off=d)
    mg = _merge_fwd("gate_merge", ycat, gl, b_gate)
    w_o, w_up, w_down = late_weights(mg)
    x2 = _mm("mix_out", mg, w_o, "nn", F32, m=t, n=d, k=d, add=x2d)
    h2 = _rms_fwd("norm_ffn", x2, g2)
    tcf = min(2 * LANES, fh)
    hmid, ua, ub = _ffn_up_act("ffn_up_act", h2, _cat_cols(w_up), cfw, batch, seq, tcf)
    x3 = _mm("ffn_down", hmid, w_down, "nn", F32, m=t, n=d, k=fh, add=x2, tk=4096)

    dx3, dx3b, loss_row, d_gf = _final_loss("final_loss", x3, gf.reshape(1, d), tgt2d)
    dw_down = _mm("dw_down", hmid, dx3b, "tn", F32, m=fh, n=d, k=t, tm=1408, tk=2048)
    du_a, du_b, d_cfw = _ffn_bwd("d_ffn", dx3b, w_down, ua, ub, cfw, batch, seq, tcf)
    ws = w_up.shape[2]
    dh2 = _mm("d_norm_ffn_a", du_a, w_up, "nt", F32, m=t, n=d, k=fh, b_off=0, b3=True)
    dh2 = _mm("d_norm_ffn_b", du_b, w_up, "nt", F32, m=t, n=d, k=fh, b_off=fh, b3=True, add=dh2)
    dw_up = _mm("dw_up_a", h2, du_a, "tn", F32, m=d, n=fh, k=t, tn=ws, tk=2048, o3=N_CHIPS)
    dw_up = _mm("dw_up_b", h2, du_b, "tn", F32, m=d, n=fh, k=t, tn=ws, tk=2048, o3=N_CHIPS, out=dw_up, o_off=fh)
    token = ffn_grads_ready(dw_up, dw_down)
    if token is not None:
        g2 = g2 + token[0:1, 0:1]
    dx2, d_g2 = _rms_bwd("d_norm_ffn", x2, dh2, g2, dx3)
    dm = _mm("d_merge", dx2, w_o, "nt", F32, m=t, n=d, k=d)
    dw_o = _mm("dw_o", mg, dx2, "tn", F32, m=d, n=d, k=t, tk=2048)
    dproj, dycat, d_bg = _merge_bwd("d_gate_merge", dm, ycat, gl, b_gate, width, gl_off)
    da_c = _mm("d_conv_out", dycat, w_oc, "nt", F32, m=t, n=cw, k=d, a_off=0)
    do = _mm("d_attn_out", dycat, w_oa, "nt", BF16, m=t, n=ATTN_WIDTH, k=d, a_off=d)
    dw_oc = _mm("dw_out_conv", a_c, dycat, "tn", F32, m=cw, n=d, k=t, b_off=0, tk=2048)
    dw_oa = _mm("dw_out_attn", o, dycat, "tn", F32, m=ATTN_WIDTH, n=d, k=t, b_off=d, tk=2048)
    dproj, d_cmw = _conv_bwd("d_conv_mix", da_c, pc, cmw, dproj, batch, seq, tc)
    dproj, d_fkey, d_fquery = _attn_bwd("attn_bwd", qkv, do, o, lse, frow, dproj, qkv_off, batch, seq, tq)
    d_fquery = jnp.pad(jnp.transpose(d_fquery, (1, 0, 2)).reshape(t, HEADS), ((0, 0), (0, LANES - HEADS)))
    dproj, d_bf = _forget_bwd("d_forget", d_fkey.reshape(batch, HEADS, seq), d_fquery, fl, bf_pad, dproj, f_off,
                              batch, seq)
    dw_inp = _mm("dw_in", dproj, h1, "tn", F32, m=width, n=d, k=t, tm=1792, tk=2048)
    d_pc = dw_inp[:pc_w].reshape(nct, 3, tc, d).transpose(1, 0, 2, 3).reshape(pc_w, d)
    d_qkv = dw_inp[qkv_off:gl_off].reshape(HEAD_PAIRS, 3, LANES, d).transpose(1, 0, 2, 3).reshape(qkv_w, d)
    dw_int = jnp.concatenate([d_pc, d_qkv, dw_inp[f_off:f_off + HEADS], dw_inp[gl_off:f_off]], axis=0)
    token = mix_grads_ready(dw_int, dw_oc, dw_oa, dw_o)
    dh1 = _mm("d_norm_mix", dproj, w_inp, "nn", F32, m=t, n=d, k=width, tk=1792, dep=token)
    grad_x, d_g1 = _rms_bwd("d_norm_mix_x", x2d, dh1, g1, dx2)
    smalls = (d_g1, d_g2, d_gf, d_bg, d_bf, d_cmw, d_cfw)
    return loss_row[0, 0], grad_x.reshape(batch, seq, d), smalls


def _pack_small(parts):
    flat = [p.reshape(-1) for p in parts]
    sizes = [f.shape[0] for f in flat]
    total = sum(sizes)
    padded = -(-total // (8 * LANES)) * (8 * LANES)
    vec = jnp.concatenate(flat + [jnp.zeros((padded - total,), F32)])
    offsets = [sum(sizes[:i]) for i in range(len(sizes))]
    return vec.reshape(padded // LANES, LANES), offsets


def kernel(x, norm_mix_g, w_in, b_f, b_gate, conv_mix_w, w_out_conv, w_out_attn, w_o, norm_ffn_g, w_up, conv_ffn_w, w_down, norm_f_g, loss_target, m_norm_mix_g, m_w_in, m_b_f, m_b_gate, m_conv_mix_w, m_w_out_conv, m_w_out_attn, m_w_o, m_norm_ffn_g, m_w_up, m_conv_ffn_w, m_w_down, m_norm_f_g, v_norm_mix_g, v_w_in, v_b_f, v_b_gate, v_conv_mix_w, v_w_out_conv, v_w_out_attn, v_w_o, v_norm_ffn_g, v_w_up, v_conv_ffn_w, v_w_down, v_norm_f_g):
    d = x.shape[-1]
    chip = 2 * lax.axis_index("x") + lax.axis_index("y")
    place = jnp.stack([chip, lax.axis_index("c")]).astype(jnp.int32)

    t_in, t_m_in, t_v_in = (jnp.transpose(w[0]) for w in (w_in, m_w_in, v_w_in))

    def row_shards(a):
        return a.reshape(N_CHIPS, a.shape[0] // N_CHIPS, a.shape[1])

    def stacked(a):
        return a.reshape(N_CHIPS * a.shape[1], a.shape[2])

    first = [t_in.astype(BF16), w_out_conv[0].astype(BF16), w_out_attn[0].astype(BF16)]
    a_in, a_oc, a_oa, a_cmw, a_cfw = _gather_weights(first, (1, 0, 0), [conv_mix_w[0], conv_ffn_w[0]])
    late = [w[0].astype(BF16) for w in (w_o, w_up, w_down)]
    late_handles, late_token = _chips_start("gather_late_start", "gather", late)

    def late_weights(after):
        lands = _chips_wait("gather_late_wait", "gather", late_handles, after)
        a_o, a_up, a_down = (lax.dynamic_update_index_in_dim(buf, own, chip, 0) for buf, own in zip(lands, late))
        return stacked(a_o), a_up, stacked(a_down)

    pending = []

    def reduce_start(tag, names, grads, axes):
        got = _exchange_sibling("exchange_sibling_" + tag, grads, axes)
        sums = [_pair_sum("pair_sum_" + nm, place, g, r, ax) for nm, g, r, ax in zip(names, grads, got, axes)]
        handles, token = _chips_start("exchange_chips_start_" + tag, "reduce", sums)
        pending.append((tag, names, grads, axes, got, handles))
        return token

    def ffn_grads_ready(dw_up, dw_down):
        return reduce_start("ffn", ("w_up", "w_down"), [dw_up, row_shards(dw_down)], (0, 0))

    def mix_grads_ready(dw_int, dw_oc, dw_oa, dw_o):
        return reduce_start("mix", ("w_in", "w_out_conv", "w_out_attn", "w_o"),
                            [row_shards(dw_int), _split_cols(dw_oc), _split_cols(dw_oa), row_shards(dw_o)],
                            (1, 0, 0, 0))

    loss_local, grad_x, smalls = _local_step(
        x, loss_target, stacked(a_in), _cat_cols(a_oc), _cat_cols(a_oa), late_weights, _cat_cols(a_cmw),
        _cat_cols(a_cfw), norm_mix_g + late_token[0:1, 0:1], b_f, b_gate, norm_ffn_g, norm_f_g,
        ffn_grads_ready, mix_grads_ready)

    reduced = {}
    for tag, names, grads, axes, got, handles in pending:
        arrivals = _chips_wait("exchange_chips_wait_" + tag, "reduce", handles, grad_x)
        halves = [_chip_sum("chip_sum_" + nm, place, g, r, arr, ax)
                  for nm, g, r, arr, ax in zip(names, grads, got, arrivals, axes)]
        reduced.update(zip(names, _share_sibling("share_sibling_" + tag, halves, axes)))
    g_in, g_oc, g_oa, g_o, g_up, g_down = (
        reduced[nm] for nm in ("w_in", "w_out_conv", "w_out_attn", "w_o", "w_up", "w_down"))

    packed, offs = _pack_small(smalls)
    total = _device_sum("device_sum", _gather_small(packed)).reshape(-1)
    shapes = [s.shape for s in smalls]
    d_g1, d_g2, d_gf, d_bg, d_bf, d_cmw, d_cfw = [
        total[o:o + math.prod(sh)].reshape(sh) for o, sh in zip(offs, shapes)]
    d_bf = d_bf[:, :HEADS]
    cw_s, cf_s = conv_mix_w.shape[2], conv_ffn_w.shape[2]
    d_cmw = lax.dynamic_slice(d_cmw, (0, chip * cw_s), (3, cw_s))
    d_cfw = lax.dynamic_slice(d_cfw, (0, chip * cf_s), (3, cf_s))

    loss = lax.psum(loss_local, ("x", "y", "c"))
    order = [
        ("norm_mix_g", norm_mix_g[0:1], d_g1, m_norm_mix_g, v_norm_mix_g),
        ("w_in", t_in, g_in, t_m_in, t_v_in),
        ("b_f", b_f, d_bf, m_b_f, v_b_f),
        ("b_gate", b_gate, d_bg, m_b_gate, v_b_gate),
        ("conv_mix_w", conv_mix_w[0], d_cmw, m_conv_mix_w[0], v_conv_mix_w[0]),
        ("w_out_conv", w_out_conv[0], g_oc, m_w_out_conv[0], v_w_out_conv[0]),
        ("w_out_attn", w_out_attn[0], g_oa, m_w_out_attn[0], v_w_out_attn[0]),
        ("w_o", w_o[0], g_o, m_w_o[0], v_w_o[0]),
        ("norm_ffn_g", norm_ffn_g, d_g2, m_norm_ffn_g, v_norm_ffn_g),
        ("w_up", w_up[0], g_up, m_w_up[0], v_w_up[0]),
        ("conv_ffn_w", conv_ffn_w[0], d_cfw, m_conv_ffn_w[0], v_conv_ffn_w[0]),
        ("w_down", w_down[0], g_down, m_w_down[0], v_w_down[0]),
        ("norm_f_g", norm_f_g.reshape(1, d), d_gf, m_norm_f_g.reshape(1, d), v_norm_f_g.reshape(1, d)),
    ]
    out_shapes = [norm_mix_g.shape, w_in.shape, b_f.shape, b_gate.shape, conv_mix_w.shape, w_out_conv.shape,
                  w_out_attn.shape, w_o.shape, norm_ffn_g.shape, w_up.shape, conv_ffn_w.shape, w_down.shape,
                  norm_f_g.shape]
    g_out, d_out, m_out, v_out = [], [], [], []
    for (nm, w, g, m, v), sh in zip(order, out_shapes):
        g = g.reshape(w.shape)
        delta, new_m, new_v = _adamw("adamw_" + nm, w, g, m.reshape(w.shape), v.reshape(w.shape))
        for dst, val in ((g_out, g), (d_out, delta), (m_out, new_m), (v_out, new_v)):
            dst.append((jnp.transpose(val) if nm == "w_in" else val).reshape(sh))
    return (loss, grad_x, *g_out, *d_out, *m_out, *v_out)
```

```python
import functools
import math

import jax
import jax.numpy as jnp
from jax import lax
from jax.experimental import pallas as pl
from jax.experimental.pallas import tpu as pltpu

F32 = jnp.float32
BF16 = jnp.bfloat16
MESH = pl.DeviceIdType.MESH

EPS = 1e-6
HEADS = 8
HEAD_DIM = 64
ATTN_WIDTH = HEADS * HEAD_DIM
HEAD_PAIRS = HEADS // 2
LANES = 128
F_PAD = 2 * LANES
NEG_BIG = -1e30
N_CHIPS = 4
N_DEV = 8

ADAM_LR = 0.001
ADAM_B1 = 0.9
ADAM_B2 = 0.999
ADAM_EPS = 1e-08
ADAM_WD = 0.01
ADAM_STEP = 10

_DIMS = {
    "nn": (((1,), (0,)), ((), ())),
    "nt": (((1,), (1,)), ((), ())),
    "tn": (((0,), (0,)), ((), ())),
}


def _tile(n, target, mult, also=()):
    best = None
    for t in range(mult, n + 1, mult):
        if n % t == 0 and t <= target and all(o % t == 0 for o in also):
            best = t
    if best is None:
        assert all(o == 0 for o in also), (n, target, mult, also)
        return n
    return best


def _sds(shape, dtype):
    return jax.ShapeDtypeStruct(shape, dtype)


def _mm(name, a, b, mode, out_dtype, *, m, n, k, a_off=0, b_off=0, b_roff=0, b3=False, out=None, o_off=0,
        o_width=None, o3=None, add=None, dep=None, tm=1024, tn=2048, tk=2048):
    wb = b.shape[2] if b3 else None
    if mode == "nn":
        tm = _tile(m, tm, 16)
        tk = _tile(k, tk, LANES, (a_off,))
        tn = wb if b3 else _tile(n, tn, LANES, (b_off, o_off))
        a_spec = pl.BlockSpec((tm, tk), lambda i, j, kk: (i, a_off // tk + kk))
        if b3:
            b_spec = pl.BlockSpec((None, tk, tn), lambda i, j, kk: (b_off // tn + j, kk, 0))
        else:
            b_spec = pl.BlockSpec((tk, tn), lambda i, j, kk: (kk, b_off // tn + j))
    elif mode == "nt":
        tm = _tile(m, tm, 16)
        tk = wb if b3 else _tile(k, tk, LANES, (a_off, b_off))
        tn = _tile(n, tn, LANES, (o_off, b_roff))
        a_spec = pl.BlockSpec((tm, tk), lambda i, j, kk: (i, a_off // tk + kk))
        if b3:
            b_spec = pl.BlockSpec((None, tn, tk), lambda i, j, kk: (b_off // tk + kk, b_roff // tn + j, 0))
        else:
            b_spec = pl.BlockSpec((tn, tk), lambda i, j, kk: (b_roff // tn + j, b_off // tk + kk))
    else:
        tm = _tile(m, tm, LANES, (a_off,))
        tk = _tile(k, tk, 16)
        tn = _tile(n, tn, LANES, (b_off, o_off))
        a_spec = pl.BlockSpec((tk, tm), lambda i, j, kk: (kk, a_off // tm + i))
        b_spec = pl.BlockSpec((tk, tn), lambda i, j, kk: (kk, b_off // tn + j))
    assert m % tm == 0 and n % tn == 0 and k % tk == 0, (name, tm, tn, tk)
    nk = k // tk
    if o3 is not None:
        o_spec = pl.BlockSpec((None, tm, tn), lambda i, j, kk: (o_off // tn + j, i, 0))
        out_sds = _sds((o3, m, tn), out_dtype)
    else:
        o_spec = pl.BlockSpec((tm, tn), lambda i, j, kk: (i, o_off // tn + j))
        width = o_width if o_width is not None else (out.shape[1] if out is not None else n)
        out_sds = _sds((m, width), out_dtype)
    use_acc = nk > 1 and out_dtype != F32
    dims = _DIMS[mode]
    has_add, has_out = add is not None, out is not None

    def body(*refs):
        a_ref, b_ref = refs[0], refs[1]
        pos = 2
        add_ref = None
        if has_add:
            add_ref = refs[pos]
            pos += 1
        if has_out:
            pos += 1
        if dep is not None:
            pos += 1
        o_ref = refs[pos]
        acc_ref = refs[pos + 1] if use_acc else None
        part = lax.dot_general(a_ref[...].astype(BF16), b_ref[...].astype(BF16), dims,
                               preferred_element_type=F32)
        if nk == 1:
            if has_add:
                part = part + add_ref[...]
            o_ref[...] = part.astype(o_ref.dtype)
            return
        kk = pl.program_id(2)
        tgt = acc_ref if use_acc else o_ref

        @pl.when(kk == 0)
        def _():
            tgt[...] = part + add_ref[...] if has_add else part

        @pl.when(kk > 0)
        def _():
            tgt[...] += part

        if use_acc:
            @pl.when(kk == nk - 1)
            def _():
                o_ref[...] = acc_ref[...].astype(o_ref.dtype)

    operands, in_specs = [a, b], [a_spec, b_spec]
    if has_add:
        operands.append(add)
        in_specs.append(pl.BlockSpec((tm, tn), lambda i, j, kk: (i, j)))
    aliases = {}
    if has_out:
        aliases = {len(operands): 0}
        operands.append(out)
        in_specs.append(pl.BlockSpec(memory_space=pl.ANY))
    if dep is not None:
        operands.append(dep)
        in_specs.append(pl.BlockSpec(memory_space=pl.ANY))
    return pl.pallas_call(
        body,
        out_shape=out_sds,
        grid=(m // tm, n // tn, nk),
        in_specs=in_specs,
        out_specs=o_spec,
        scratch_shapes=[pltpu.VMEM((tm, tn), F32)] if use_acc else [],
        input_output_aliases=aliases,
        compiler_params=pltpu.CompilerParams(dimension_semantics=("parallel", "parallel", "arbitrary")),
        name=name,
    )(*operands)


def _rms_fwd(name, x, g):
    t, d = x.shape
    tm = _tile(t, 512, 16)

    def body(x_ref, g_ref, o_ref):
        xv = x_ref[...]
        r = lax.rsqrt(jnp.mean(xv * xv, axis=-1, keepdims=True) + EPS)
        o_ref[...] = ((xv * r) * g_ref[...]).astype(o_ref.dtype)

    return pl.pallas_call(
        body,
        out_shape=_sds((t, d), BF16),
        grid=(t // tm,),
        in_specs=[pl.BlockSpec((tm, d), lambda i: (i, 0)), pl.BlockSpec((1, d), lambda i: (0, 0))],
        out_specs=pl.BlockSpec((tm, d), lambda i: (i, 0)),
        compiler_params=pltpu.CompilerParams(dimension_semantics=("parallel",)),
        name=name,
    )(x, g)


def _rms_bwd(name, x, dh, g, res):
    t, d = x.shape
    tm = _tile(t, 512, 16)

    def body(x_ref, dh_ref, g_ref, res_ref, dx_ref, dg_ref):
        xv = x_ref[...]
        r = lax.rsqrt(jnp.mean(xv * xv, axis=-1, keepdims=True) + EPS)
        xh = xv * r
        dhv = dh_ref[...].astype(F32)
        dxh = dhv * g_ref[...]
        dx_ref[...] = res_ref[...] + r * (dxh - xh * jnp.mean(dxh * xh, axis=-1, keepdims=True))

        @pl.when(pl.program_id(0) == 0)
        def _():
            dg_ref[...] = jnp.zeros_like(dg_ref)

        dg_ref[...] += jnp.sum(dhv * xh, axis=0, keepdims=True)

    row = pl.BlockSpec((tm, d), lambda i: (i, 0))
    vec = pl.BlockSpec((1, d), lambda i: (0, 0))
    return pl.pallas_call(
        body,
        out_shape=(_sds((t, d), F32), _sds((1, d), F32)),
        grid=(t // tm,),
        in_specs=[row, row, vec, row],
        out_specs=(row, vec),
        compiler_params=pltpu.CompilerParams(dimension_semantics=("arbitrary",)),
        name=name,
    )(x, dh, g, res)


def _final_loss(name, x, g, target):
    t, d = x.shape
    tm = _tile(t, 512, 16)

    def body(x_ref, g_ref, t_ref, dx_ref, dxb_ref, loss_ref, dg_ref):
        xv = x_ref[...]
        gv = g_ref[...]
        r = lax.rsqrt(jnp.mean(xv * xv, axis=-1, keepdims=True) + EPS)
        xh = xv * r
        err = xh * gv - t_ref[...]
        dy = err * (1.0 / d)
        dxh = dy * gv
        dx = r * (dxh - xh * jnp.mean(dxh * xh, axis=-1, keepdims=True))
        dx_ref[...] = dx
        dxb_ref[...] = dx.astype(dxb_ref.dtype)
        per_row = jnp.sum(err * err, axis=-1, keepdims=True) * (0.5 / d)

        @pl.when(pl.program_id(0) == 0)
        def _():
            dg_ref[...] = jnp.zeros_like(dg_ref)
            loss_ref[...] = jnp.zeros_like(loss_ref)

        dg_ref[...] += jnp.sum(dy * xh, axis=0, keepdims=True)
        loss_ref[...] += jnp.sum(per_row, axis=0, keepdims=True)

    row = pl.BlockSpec((tm, d), lambda i: (i, 0))
    vec = pl.BlockSpec((1, d), lambda i: (0, 0))
    return pl.pallas_call(
        body,
        out_shape=(_sds((t, d), F32), _sds((t, d), BF16), _sds((1, LANES), F32), _sds((1, d), F32)),
        grid=(t // tm,),
        in_specs=[row, vec, row],
        out_specs=(row, row, pl.BlockSpec((1, LANES), lambda i: (0, 0)), vec),
        compiler_params=pltpu.CompilerParams(dimension_semantics=("arbitrary",)),
        name=name,
    )(x, g, target)


def _shift_down(z, k):
    row = lax.broadcasted_iota(jnp.int32, z.shape, 0)
    return jnp.where(row >= k, pltpu.roll(z, k, axis=0), 0.0)


def _shift_up(z, k):
    s = z.shape[0]
    row = lax.broadcasted_iota(jnp.int32, z.shape, 0)
    return jnp.where(row < s - k, pltpu.roll(z, s - k, axis=0), 0.0)


def _conv3(z, w):
    return (w[2:3] * z + w[0:1] * _shift_down(z, 2)) + w[1:2] * _shift_down(z, 1)


def _conv3_t(dz, w):
    return (w[2:3] * dz + w[0:1] * _shift_up(dz, 2)) + w[1:2] * _shift_up(dz, 1)


def _conv_fwd(name, pc, w, batch, seq, tc):
    cw = w.shape[1]
    nct = cw // tc

    def body(pc_ref, w_ref, o_ref):
        cb = pc_ref[:, 0:tc].astype(F32)
        z = pc_ref[:, tc:2 * tc].astype(F32) * pc_ref[:, 2 * tc:3 * tc].astype(F32)
        o_ref[...] = (cb * _conv3(z, w_ref[...])).astype(o_ref.dtype)

    return pl.pallas_call(
        body,
        out_shape=_sds((batch * seq, cw), BF16),
        grid=(batch, nct),
        in_specs=[pl.BlockSpec((seq, 3 * tc), lambda b, j: (b, j)), pl.BlockSpec((3, tc), lambda b, j: (0, j))],
        out_specs=pl.BlockSpec((seq, tc), lambda b, j: (b, j)),
        compiler_params=pltpu.CompilerParams(dimension_semantics=("parallel", "parallel")),
        name=name,
    )(pc, w)


def _conv_bwd(name, da, pc, w, dproj, batch, seq, tc):
    cw = w.shape[1]
    nct = cw // tc

    def body(da_ref, pc_ref, w_ref, _, dpc_ref, dw_ref):
        wv = w_ref[...]
        cb = pc_ref[:, 0:tc].astype(F32)
        cc = pc_ref[:, tc:2 * tc].astype(F32)
        cin = pc_ref[:, 2 * tc:3 * tc].astype(F32)
        z = cc * cin
        dav = da_ref[...].astype(F32)
        du = dav * cb
        dz = _conv3_t(du, wv)
        dpc_ref[:, 0:tc] = (dav * _conv3(z, wv)).astype(dpc_ref.dtype)
        dpc_ref[:, tc:2 * tc] = (dz * cin).astype(dpc_ref.dtype)
        dpc_ref[:, 2 * tc:3 * tc] = (dz * cc).astype(dpc_ref.dtype)

        @pl.when(pl.program_id(1) == 0)
        def _():
            dw_ref[...] = jnp.zeros_like(dw_ref)

        dw_ref[0:1, :] += jnp.sum(du * _shift_down(z, 2), axis=0, keepdims=True)
        dw_ref[1:2, :] += jnp.sum(du * _shift_down(z, 1), axis=0, keepdims=True)
        dw_ref[2:3, :] += jnp.sum(du * z, axis=0, keepdims=True)

    return pl.pallas_call(
        body,
        out_shape=(_sds(dproj.shape, dproj.dtype), _sds((3, cw), F32)),
        grid=(nct, batch),
        in_specs=[
            pl.BlockSpec((seq, tc), lambda j, b: (b, j)),
            pl.BlockSpec((seq, 3 * tc), lambda j, b: (b, j)),
            pl.BlockSpec((3, tc), lambda j, b: (0, j)),
            pl.BlockSpec(memory_space=pl.ANY),
        ],
        out_specs=(pl.BlockSpec((seq, 3 * tc), lambda j, b: (b, j)), pl.BlockSpec((3, tc), lambda j, b: (0, j))),
        input_output_aliases={3: 0},
        compiler_params=pltpu.CompilerParams(dimension_semantics=("parallel", "arbitrary")),
        name=name,
    )(da, pc, w, dproj)


def _ffn_up_act(name, h2, w_up, w, batch, seq, tc):
    d = h2.shape[1]
    fh = w.shape[1] // 2
    nf = fh // tc

    def body(h_ref, ma_ref, mb_ref, wa_ref, wb_ref, o_ref, ua_ref, ub_ref):
        hv = h_ref[...]
        ua = _dot(hv, ma_ref[...], "nn")
        ub = _dot(hv, mb_ref[...], "nn")
        ua_ref[...] = ua.astype(ua_ref.dtype)
        ub_ref[...] = ub.astype(ub_ref.dtype)
        a = _conv3(ua, wa_ref[...])
        b = _conv3(ub, wb_ref[...])
        o_ref[...] = (a * jax.nn.sigmoid(a) * b).astype(o_ref.dtype)

    act = pl.BlockSpec((seq, tc), lambda b, j: (b, j))
    shape = _sds((batch * seq, fh), BF16)
    return pl.pallas_call(
        body,
        out_shape=(shape, shape, shape),
        grid=(batch, nf),
        in_specs=[
            pl.BlockSpec((seq, d), lambda b, j: (b, 0)),
            pl.BlockSpec((d, tc), lambda b, j: (0, j)),
            pl.BlockSpec((d, tc), lambda b, j: (0, nf + j)),
            pl.BlockSpec((3, tc), lambda b, j: (0, j)),
            pl.BlockSpec((3, tc), lambda b, j: (0, nf + j)),
        ],
        out_specs=(act, act, act),
        compiler_params=pltpu.CompilerParams(dimension_semantics=("parallel", "parallel")),
        name=name,
    )(h2, w_up, w_up, w, w)


def _ffn_bwd(name, dx, w_down, ua, ub, w, batch, seq, tc):
    d = dx.shape[1]
    fh = w.shape[1] // 2
    nf = fh // tc

    def body(dx_ref, md_ref, ua_ref, ub_ref, wa_ref, wb_ref, dua_ref, dub_ref, dw_ref):
        j = pl.program_id(1)
        uav, ubv, wa, wb = ua_ref[...].astype(F32), ub_ref[...].astype(F32), wa_ref[...], wb_ref[...]
        dhv = _dot(dx_ref[...].astype(BF16), md_ref[...], "nt")
        a = _conv3(uav, wa)
        b = _conv3(ubv, wb)
        sg = jax.nn.sigmoid(a)
        da = dhv * b * (sg * (1.0 + a * (1.0 - sg)))
        db = dhv * (a * sg)
        dua_ref[...] = _conv3_t(da, wa).astype(dua_ref.dtype)
        dub_ref[...] = _conv3_t(db, wb).astype(dub_ref.dtype)

        @pl.when((pl.program_id(0) == 0) & (j == 0))
        def _():
            dw_ref[...] = jnp.zeros_like(dw_ref)

        for off, dv, uv in ((0, da, uav), (fh, db, ubv)):
            cols = pl.ds(pl.multiple_of(off + j * tc, LANES), tc)
            dw_ref[0:1, cols] += jnp.sum(dv * _shift_down(uv, 2), axis=0, keepdims=True)
            dw_ref[1:2, cols] += jnp.sum(dv * _shift_down(uv, 1), axis=0, keepdims=True)
            dw_ref[2:3, cols] += jnp.sum(dv * uv, axis=0, keepdims=True)

    act = pl.BlockSpec((seq, tc), lambda b, j: (b, j))
    shape = _sds((batch * seq, fh), BF16)
    return pl.pallas_call(
        body,
        out_shape=(shape, shape, _sds((3, 2 * fh), F32)),
        grid=(batch, nf),
        in_specs=[
            pl.BlockSpec((seq, d), lambda b, j: (b, 0)),
            pl.BlockSpec((tc, d), lambda b, j: (j, 0)),
            act,
            act,
            pl.BlockSpec((3, tc), lambda b, j: (0, j)),
            pl.BlockSpec((3, tc), lambda b, j: (0, nf + j)),
        ],
        out_specs=(act, act, pl.BlockSpec((3, 2 * fh), lambda b, j: (0, 0))),
        compiler_params=pltpu.CompilerParams(dimension_semantics=("arbitrary", "arbitrary")),
        name=name,
    )(dx, w_down, ua, ub, w, w)


def _merge_fwd(name, ycat, gl, bg):
    t, d2 = ycat.shape
    d = d2 // 2
    tm = _tile(t, 256, 16)

    def body(y_ref, gl_ref, bg_ref, o_ref):
        g = jax.nn.sigmoid(gl_ref[...].astype(F32) + bg_ref[...])
        prod = g * y_ref[...].astype(F32)
        o_ref[...] = (prod[:, 0:d] + prod[:, d:d2]).astype(o_ref.dtype)

    row = pl.BlockSpec((tm, d2), lambda i: (i, 0))
    return pl.pallas_call(
        body,
        out_shape=_sds((t, d), BF16),
        grid=(t // tm,),
        in_specs=[row, row, pl.BlockSpec((1, d2), lambda i: (0, 0))],
        out_specs=pl.BlockSpec((tm, d), lambda i: (i, 0)),
        compiler_params=pltpu.CompilerParams(dimension_semantics=("parallel",)),
        name=name,
    )(ycat, gl, bg)


def _merge_bwd(name, dm, ycat, gl, bg, width, gl_off):
    t, d2 = ycat.shape
    d = d2 // 2
    tm = _tile(t, 512, 16)
    wb = math.gcd(gl_off, d)
    nw = d // wb

    def body(dm_ref, y_ref, gl_ref, bg_ref, dgl_ref, dy_ref, dbg_ref):
        g = jax.nn.sigmoid(gl_ref[...].astype(F32) + bg_ref[...])
        dmv = dm_ref[...].astype(F32)
        dgl = dmv * y_ref[...].astype(F32) * (g * (1.0 - g))
        dgl_ref[...] = dgl.astype(dgl_ref.dtype)
        dy_ref[...] = (dmv * g).astype(dy_ref.dtype)

        @pl.when(pl.program_id(2) == 0)
        def _():
            dbg_ref[...] = jnp.zeros_like(dbg_ref)

        dbg_ref[...] += jnp.sum(dgl, axis=0, keepdims=True)

    half = pl.BlockSpec((tm, wb), lambda h, j, i: (i, h * nw + j))
    vec = pl.BlockSpec((1, wb), lambda h, j, i: (0, h * nw + j))
    return pl.pallas_call(
        body,
        out_shape=(_sds((t, width), BF16), _sds((t, d2), BF16), _sds((1, d2), F32)),
        grid=(2, nw, t // tm),
        in_specs=[pl.BlockSpec((tm, wb), lambda h, j, i: (i, j)), half, half, vec],
        out_specs=(pl.BlockSpec((tm, wb), lambda h, j, i: (i, gl_off // wb + h * nw + j)), half, vec),
        compiler_params=pltpu.CompilerParams(dimension_semantics=("parallel", "parallel", "arbitrary")),
        name=name,
    )(dm, ycat, gl, bg)


def _log_sigmoid(z):
    return jnp.minimum(z, 0.0) - jnp.log1p(jnp.exp(-jnp.abs(z)))


def _forget_fwd(name, fl, bf, batch, seq):
    def body(fl_ref, bf_ref, o_ref):
        lf = _log_sigmoid(fl_ref[:, 0:LANES] + bf_ref[:, 0:LANES])
        acc = lf.T[0:HEADS, :]
        lane = lax.broadcasted_iota(jnp.int32, acc.shape, 1)
        k = 1
        while k < seq:
            acc = acc + jnp.where(lane >= k, pltpu.roll(acc, k, axis=1), 0.0)
            k *= 2
        o_ref[...] = acc

    return pl.pallas_call(
        body,
        out_shape=_sds((batch, HEADS, seq), F32),
        grid=(batch,),
        in_specs=[pl.BlockSpec((seq, F_PAD), lambda b: (b, 0)), pl.BlockSpec((1, F_PAD), lambda b: (0, 0))],
        out_specs=pl.BlockSpec((None, HEADS, seq), lambda b: (b, 0, 0)),
        compiler_params=pltpu.CompilerParams(dimension_semantics=("parallel",)),
        name=name,
    )(fl, bf)


def _forget_bwd(name, d_key, d_query, fl, bf, dproj, f_off, batch, seq):
    nfb = F_PAD // LANES

    def body(dk_ref, dq_ref, fl_ref, bf_ref, _, df_ref, dbf_ref):
        jj = pl.program_id(1)
        key_t = jnp.concatenate([dk_ref[...], jnp.zeros((LANES - HEADS, seq), F32)], axis=0).T
        acc = dq_ref[...] - key_t
        row = lax.broadcasted_iota(jnp.int32, acc.shape, 0)
        k = 1
        while k < seq:
            acc = acc + jnp.where(row < seq - k, pltpu.roll(acc, seq - k, axis=0), 0.0)
            k *= 2
        z = fl_ref[:, 0:LANES] + bf_ref[:, 0:LANES]
        col = lax.broadcasted_iota(jnp.int32, acc.shape, 1)
        df = jnp.where(col < HEADS, acc * jax.nn.sigmoid(-z), 0.0)
        df = jnp.where(jj == 0, df, 0.0)
        df_ref[...] = df.astype(df_ref.dtype)

        @pl.when((pl.program_id(0) == 0) & (jj == 0))
        def _():
            dbf_ref[...] = jnp.zeros_like(dbf_ref)

        dbf_ref[...] += jnp.sum(df, axis=0, keepdims=True)

    return pl.pallas_call(
        body,
        out_shape=(_sds(dproj.shape, dproj.dtype), _sds((1, LANES), F32)),
        grid=(batch, nfb),
        in_specs=[
            pl.BlockSpec((None, HEADS, seq), lambda b, j: (b, 0, 0)),
            pl.BlockSpec((seq, LANES), lambda b, j: (b, 0)),
            pl.BlockSpec((seq, F_PAD), lambda b, j: (b, 0)),
            pl.BlockSpec((1, F_PAD), lambda b, j: (0, 0)),
            pl.BlockSpec(memory_space=pl.ANY),
        ],
        out_specs=(pl.BlockSpec((seq, LANES), lambda b, j: (b, f_off // LANES + j)),
                   pl.BlockSpec((1, LANES), lambda b, j: (0, 0))),
        input_output_aliases={4: 0},
        compiler_params=pltpu.CompilerParams(dimension_semantics=("arbitrary", "arbitrary")),
        name=name,
    )(d_key, d_query, fl, bf, dproj)


def _dot(a, b, mode):
    return lax.dot_general(a, b, _DIMS[mode], preferred_element_type=F32)


def _attn_fwd(name, qkv, frow, batch, seq, tq):
    nq = seq // tq
    scale = 1.0 / math.sqrt(HEAD_DIM)

    def body(q_ref, k_ref, v_ref, f_ref, o_ref, lse_ref):
        i = pl.program_id(2)
        lane = lax.broadcasted_iota(jnp.int32, (1, LANES), 1)
        lo = lane < HEAD_DIM
        qs = q_ref[...] * scale
        qh = (jnp.where(lo, qs, 0.0).astype(BF16), jnp.where(lo, 0.0, qs).astype(BF16))
        row = lax.broadcasted_iota(jnp.int32, (tq, tq), 0)
        col = lax.broadcasted_iota(jnp.int32, (tq, tq), 1)

        def step(j, carry, diag):
            m0, l0, m1, l1, acc = carry
            start = pl.multiple_of(j * tq, tq)
            kj = k_ref[pl.ds(start, tq), :]
            vj = v_ref[pl.ds(start, tq), :]
            ms, ls, pvs, alphas = [], [], [], []
            for h, (m_old, l_old) in enumerate(((m0, l0), (m1, l1))):
                s = _dot(qh[h], kj, "nt") - f_ref[h:h + 1, pl.ds(start, tq)]
                if diag:
                    s = jnp.where(col <= row, s, NEG_BIG)
                m_new = jnp.maximum(m_old, jnp.max(s, axis=1, keepdims=True))
                p = jnp.exp(s - m_new)
                alpha = jnp.exp(m_old - m_new)
                ls.append(alpha * l_old + jnp.sum(p, axis=1, keepdims=True))
                ms.append(m_new)
                alphas.append(alpha)
                vh = jnp.where(lo, vj, 0.0) if h == 0 else jnp.where(lo, 0.0, vj)
                pvs.append(_dot(p.astype(BF16), vh.astype(BF16), "nn"))
            acc = acc * jnp.where(lo, alphas[0], alphas[1]) + (pvs[0] + pvs[1])
            return ms[0], ls[0], ms[1], ls[1], acc

        neg = jnp.full((tq, 1), NEG_BIG, F32)
        zero = jnp.zeros((tq, 1), F32)
        init = (neg, zero, neg, zero, jnp.zeros((tq, LANES), F32))
        carry = lax.fori_loop(0, i, lambda j, c: step(j, c, False), init)
        m0, l0, m1, l1, acc = step(i, carry, True)
        o_ref[...] = (acc / jnp.where(lo, l0, l1)).astype(o_ref.dtype)
        lse_ref[:, 0:1] = m0 + jnp.log(l0)
        lse_ref[:, 1:2] = m1 + jnp.log(l1)

    return pl.pallas_call(
        body,
        out_shape=(_sds((batch * seq, ATTN_WIDTH), BF16), _sds((HEAD_PAIRS, batch * seq, 2), F32)),
        grid=(batch, HEAD_PAIRS, nq),
        in_specs=[
            pl.BlockSpec((tq, LANES), lambda b, hp, i: (b * nq + i, 3 * hp)),
            pl.BlockSpec((seq, LANES), lambda b, hp, i: (b, 3 * hp + 1)),
            pl.BlockSpec((seq, LANES), lambda b, hp, i: (b, 3 * hp + 2)),
            pl.BlockSpec((None, None, 2, seq), lambda b, hp, i: (b, hp, 0, 0)),
        ],
        out_specs=(
            pl.BlockSpec((tq, LANES), lambda b, hp, i: (b * nq + i, hp)),
            pl.BlockSpec((None, tq, 2), lambda b, hp, i: (hp, b * nq + i, 0)),
        ),
        compiler_params=pltpu.CompilerParams(dimension_semantics=("parallel", "parallel", "parallel")),
        name=name,
    )(qkv, qkv, qkv, frow)


def _attn_bwd(name, qkv, do, o, lse, frow, dproj, qkv_off, batch, seq, tq):
    nq = seq // tq
    scale = 1.0 / math.sqrt(HEAD_DIM)

    def body(q_ref, k_ref, v_ref, do_ref, o_ref, lse_ref, f_ref, _, dqkv_ref, df_ref, drow_ref,
             dq_acc, dk_acc, dv_acc, df_acc):
        j = pl.program_id(2)
        lane = lax.broadcasted_iota(jnp.int32, (1, LANES), 1)
        lo = lane < HEAD_DIM
        masks = (lo, jnp.logical_not(lo))
        row = lax.broadcasted_iota(jnp.int32, (tq, tq), 0)
        col = lax.broadcasted_iota(jnp.int32, (tq, tq), 1)

        @pl.when(j == 0)
        def _():
            dq_acc[...] = jnp.zeros_like(dq_acc)
            drow_ref[...] = jnp.zeros_like(drow_ref)

        dk_acc[...] = jnp.zeros_like(dk_acc)
        dv_acc[...] = jnp.zeros_like(dv_acc)
        df_acc[...] = jnp.zeros_like(df_acc)
        kj = k_ref[...]
        vj = v_ref[...]
        kstart = pl.multiple_of(j * tq, tq)
        kh = tuple(jnp.where(mk, kj, 0.0).astype(BF16) for mk in masks)

        def step(i, diag):
            start = pl.multiple_of(i * tq, tq)
            rows = pl.ds(start, tq)
            qi = q_ref[rows, :] * scale
            doi = do_ref[rows, :]
            prod = doi.astype(F32) * o_ref[rows, :].astype(F32)
            lse_i = lse_ref[rows, :]
            dq_i = jnp.zeros((tq, LANES), F32)
            for h, mk in enumerate(masks):
                q_h = jnp.where(mk, qi, 0.0).astype(BF16)
                do_h = jnp.where(mk, doi, 0.0).astype(BF16)
                delta = jnp.sum(jnp.where(mk, prod, 0.0), axis=1, keepdims=True)
                s = _dot(q_h, kj, "nt") - f_ref[h:h + 1, pl.ds(kstart, tq)]
                p = jnp.exp(s - lse_i[:, h:h + 1])
                if diag:
                    p = jnp.where(col <= row, p, 0.0)
                ds = p * (_dot(do_h, vj, "nt") - delta)
                df_acc[h:h + 1, :] += jnp.sum(ds, axis=0, keepdims=True)
                drow_ref[rows, h:h + 1] += jnp.sum(ds, axis=1, keepdims=True)
                dsb = ds.astype(BF16)
                dv_acc[...] += _dot(p.astype(BF16), do_h, "tn")
                dk_acc[...] += _dot(dsb, q_h, "tn")
                dq_i = dq_i + _dot(dsb, kh[h], "nn")
            dq_acc[rows, :] += dq_i

        step(j, True)
        lax.fori_loop(j + 1, nq, lambda i, c: (step(i, False), c)[1], 0)
        dqkv_ref[:, 0:LANES] = (dq_acc[pl.ds(kstart, tq), :] * scale).astype(dqkv_ref.dtype)
        dqkv_ref[:, LANES:2 * LANES] = dk_acc[...].astype(dqkv_ref.dtype)
        dqkv_ref[:, 2 * LANES:3 * LANES] = dv_acc[...].astype(dqkv_ref.dtype)
        df_ref[...] = df_acc[...]

    full = lambda c: pl.BlockSpec((seq, LANES), lambda b, hp, j: (b, c(hp)))
    blk = lambda c: pl.BlockSpec((tq, LANES), lambda b, hp, j: (b * nq + j, c(hp)))
    return pl.pallas_call(
        body,
        out_shape=(_sds(dproj.shape, dproj.dtype), _sds((batch, HEAD_PAIRS, 2, seq), F32),
                   _sds((HEAD_PAIRS, batch * seq, 2), F32)),
        grid=(batch, HEAD_PAIRS, nq),
        in_specs=[
            full(lambda hp: 3 * hp),
            blk(lambda hp: 3 * hp + 1),
            blk(lambda hp: 3 * hp + 2),
            full(lambda hp: hp),
            full(lambda hp: hp),
            pl.BlockSpec((None, seq, 2), lambda b, hp, j: (hp, b, 0)),
            pl.BlockSpec((None, None, 2, seq), lambda b, hp, j: (b, hp, 0, 0)),
            pl.BlockSpec(memory_space=pl.ANY),
        ],
        out_specs=(
            pl.BlockSpec((tq, 3 * LANES), lambda b, hp, j: (b * nq + j, qkv_off // (3 * LANES) + hp)),
            pl.BlockSpec((None, None, 2, tq), lambda b, hp, j: (b, hp, 0, j)),
            pl.BlockSpec((None, seq, 2), lambda b, hp, j: (hp, b, 0)),
        ),
        scratch_shapes=[
            pltpu.VMEM((seq, LANES), F32),
            pltpu.VMEM((tq, LANES), F32),
            pltpu.VMEM((tq, LANES), F32),
            pltpu.VMEM((2, tq), F32),
        ],
        input_output_aliases={7: 0},
        compiler_params=pltpu.CompilerParams(dimension_semantics=("parallel", "parallel", "arbitrary")),
        name=name,
    )(qkv, qkv, qkv, do, o, lse, frow, dproj)


def _mesh_place():
    x, y, c = lax.axis_index("x"), lax.axis_index("y"), lax.axis_index("c")
    chips = [(1 - x, y), (x, 1 - y), (1 - x, 1 - y)]
    return x, y, c, chips


def _hbm_specs(n):
    return [pl.BlockSpec(memory_space=pl.ANY)] * n


def _half(shape2d, axis, which):
    size = shape2d[axis] // 2
    sl = pl.ds(pl.multiple_of(which * size, 16 if axis == 0 else LANES), size)
    return (sl, slice(None)) if axis == 0 else (slice(None), sl)


def _gather_weights(bigs, axes, smalls):
    nb, ns = len(bigs), len(smalls)
    arrays = list(bigs) + list(smalls)
    n = nb + ns

    def body(*refs):
        ins, outs = refs[:n], refs[n:2 * n]
        send_sems, recv_sems = refs[2 * n:]
        x, y, c, chips = _mesh_place()
        me = 2 * x + y
        sibling = (x, y, 1 - c)

        def half(a, which):
            return _half(arrays[a].shape, axes[a], which)

        def copy(a, k, src, dst, to):
            return pltpu.make_async_remote_copy(src_ref=src, dst_ref=dst, send_sem=send_sems.at[a, k],
                                                recv_sem=recv_sems.at[a, k], device_id=to, device_id_type=MESH)

        sends = []
        for a in range(n):
            for j, chip in enumerate(chips):
                if a < nb:
                    cp = copy(a, j, ins[a].at[half(a, c)], outs[a].at[(me,) + half(a, c)], (*chip, c))
                else:
                    cp = copy(a, j, ins[a], outs[a].at[me], (*chip, c))
                cp.start()
                sends.append(cp)
        for a in range(nb):
            for j, (px, py) in enumerate(chips):
                blk = outs[a].at[(2 * px + py,) + half(a, c)]
                copy(a, j, blk, blk, (px, py, c)).wait_recv()
                fwd = copy(a, 3 + j, blk, blk, sibling)
                fwd.start()
                sends.append(fwd)
        for a in range(nb, n):
            for j, (px, py) in enumerate(chips):
                blk = outs[a].at[2 * px + py]
                copy(a, j, blk, blk, (px, py, c)).wait_recv()
        for a in range(nb):
            for j, (px, py) in enumerate(chips):
                blk = outs[a].at[(2 * px + py,) + half(a, 1 - c)]
                copy(a, 3 + j, blk, blk, sibling).wait_recv()
        for cp in sends:
            cp.wait_send()

    outs = pl.pallas_call(
        body,
        out_shape=tuple(_sds((N_CHIPS,) + a.shape, a.dtype) for a in arrays),
        in_specs=_hbm_specs(n),
        out_specs=tuple(_hbm_specs(n)),
        scratch_shapes=[pltpu.SemaphoreType.DMA((n, 6)), pltpu.SemaphoreType.DMA((n, 6))],
        name="gather_weights",
    )(*arrays)
    me = 2 * lax.axis_index("x") + lax.axis_index("y")
    return tuple(lax.dynamic_update_index_in_dim(o, a, me, 0) for o, a in zip(outs, arrays))


def _gather_small(v):
    m_per, ncol = v.shape

    def body(x_ref, out_ref, send_sems, recv_sems, local_sem):
        x, y, c, chips = _mesh_place()
        me, sibling = (x, y, c), (x, y, 1 - c)

        def rows(px, py, pc):
            return out_ref.at[pl.ds((4 * px + 2 * py + pc) * m_per, m_per), :]

        def copy(k, block, to, src=None):
            return pltpu.make_async_remote_copy(src_ref=rows(*block) if src is None else src, dst_ref=rows(*block),
                                                send_sem=send_sems.at[k], recv_sem=recv_sems.at[k],
                                                device_id=to, device_id_type=MESH)

        mine = pltpu.make_async_copy(x_ref, rows(*me), local_sem)
        mine.start()
        first = [copy(0, me, sibling, src=x_ref)]
        first += [copy(1 + j, me, (*chip, c), src=x_ref) for j, chip in enumerate(chips)]
        for cp in first:
            cp.start()
        passed = [copy(4 + j, (*chip, c), sibling) for j, chip in enumerate(chips)]
        for j, chip in enumerate(chips):
            copy(1 + j, (*chip, c), me).wait_recv()
            passed[j].start()
        copy(0, sibling, me).wait_recv()
        for j, chip in enumerate(chips):
            copy(4 + j, (*chip, 1 - c), me).wait_recv()
        for cp in first + passed:
            cp.wait_send()
        mine.wait()

    return pl.pallas_call(
        body,
        out_shape=_sds((N_DEV * m_per, ncol), v.dtype),
        in_specs=[pl.BlockSpec(memory_space=pltpu.VMEM)],
        out_specs=pl.BlockSpec(memory_space=pltpu.VMEM),
        scratch_shapes=[pltpu.SemaphoreType.DMA((7,)), pltpu.SemaphoreType.DMA((7,)), pltpu.SemaphoreType.DMA],
        name="gather_small",
    )(v)


def _half_shape(shape2d, axis):
    return (shape2d[0] // 2, shape2d[1]) if axis == 0 else (shape2d[0], shape2d[1] // 2)


def _exchange_sibling(name, grads, axes):
    n = len(grads)

    def body(*refs):
        ins, outs = refs[:n], refs[n:2 * n]
        send_sems, recv_sems = refs[2 * n:]
        x, y, c, _ = _mesh_place()
        copies = []
        for a in range(n):
            src = ins[a].at[(slice(None),) + _half(grads[a].shape[1:], axes[a], 1 - c)]
            cp = pltpu.make_async_remote_copy(src_ref=src, dst_ref=outs[a], send_sem=send_sems.at[a],
                                              recv_sem=recv_sems.at[a], device_id=(x, y, 1 - c), device_id_type=MESH)
            cp.start()
            copies.append(cp)
        for cp in copies:
            cp.wait()

    return pl.pallas_call(
        body,
        out_shape=tuple(_sds((N_CHIPS,) + _half_shape(g.shape[1:], ax), g.dtype) for g, ax in zip(grads, axes)),
        in_specs=_hbm_specs(n),
        out_specs=tuple(_hbm_specs(n)),
        scratch_shapes=[pltpu.SemaphoreType.DMA((n,)), pltpu.SemaphoreType.DMA((n,))],
        name=name,
    )(*grads)


_HBM = pl.BlockSpec(memory_space=pltpu.HBM)
_SEM = pl.BlockSpec(memory_space=pltpu.SEMAPHORE)
_EFFECT = pltpu.SideEffectType.DATAFLOW_SIDE_EFFECTING


def _chip_copies(kind, srcs, lands, send_sems, recv_sems):
    x, y, c, chips = _mesh_place()
    copies = []
    for a in range(len(srcs)):
        for j, (px, py) in enumerate(chips):
            if kind == "gather":
                src, dst = srcs[a], lands[a].at[2 * x + y]
            else:
                src, dst = srcs[a].at[2 * px + py], lands[a].at[j]
            copies.append(pltpu.make_async_remote_copy(src_ref=src, dst_ref=dst, send_sem=send_sems.at[3 * a + j],
                                                       recv_sem=recv_sems.at[3 * a + j], device_id=(px, py, c),
                                                       device_id_type=MESH))
    return copies


def _chips_start(name, kind, srcs):
    n = len(srcs)
    slots = N_CHIPS if kind == "gather" else 3
    lands = [lax.empty((slots,) + (s.shape if kind == "gather" else s.shape[1:]), s.dtype) for s in srcs]

    def body(*refs):
        for cp in _chip_copies(kind, refs[:n], refs[n:2 * n], refs[2 * n], refs[2 * n + 1]):
            cp.start()
        refs[-1][...] = jnp.zeros_like(refs[-1])

    outs = pl.pallas_call(
        body,
        out_shape=(pltpu.SemaphoreType.DMA((3 * n,)), pltpu.SemaphoreType.DMA((3 * n,)),
                   *[pltpu.HBM(v.shape, v.dtype) for v in (*srcs, *lands)], _sds((8, LANES), F32)),
        in_specs=[_HBM] * (2 * n),
        out_specs=(_SEM, _SEM, *[_HBM] * (2 * n), pl.BlockSpec(memory_space=pltpu.VMEM)),
        input_output_aliases={i: 2 + i for i in range(2 * n)},
        compiler_params=pltpu.CompilerParams(has_side_effects=_EFFECT),
        name=name,
    )(*[pltpu.with_memory_space_constraint(v, pltpu.HBM) for v in (*srcs, *lands)])
    return outs[:-1], outs[-1]


def _chips_wait(name, kind, handles, after):
    send_sems, recv_sems, *thru = handles
    n = len(thru) // 2

    def body(*refs):
        for cp in _chip_copies(kind, refs[:n], refs[n:2 * n], refs[2 * n], refs[2 * n + 1]):
            cp.wait_send()
            cp.wait_recv()

    outs = pl.pallas_call(
        body,
        out_shape=tuple(pltpu.HBM(v.shape, v.dtype) for v in thru),
        in_specs=[_HBM] * (2 * n) + [_SEM, _SEM, pl.BlockSpec(memory_space=pl.ANY)],
        out_specs=tuple([_HBM] * (2 * n)),
        input_output_aliases={i: i for i in range(2 * n)},
        compiler_params=pltpu.CompilerParams(has_side_effects=_EFFECT),
        name=name,
    )(*thru, send_sems, recv_sems, after)
    return outs[n:]


def _share_sibling(name, shards, axes):
    n = len(shards)

    def body(*refs):
        ins, outs = refs[:n], refs[n:2 * n]
        send_sems, recv_sems = refs[2 * n:]
        x, y, c, _ = _mesh_place()
        started = []
        for a in range(n):
            mine = _half(shards[a].shape, axes[a], c)
            theirs = _half(shards[a].shape, axes[a], 1 - c)
            cp = pltpu.make_async_remote_copy(src_ref=ins[a].at[mine], dst_ref=outs[a].at[mine],
                                              send_sem=send_sems.at[a], recv_sem=recv_sems.at[a],
                                              device_id=(x, y, 1 - c), device_id_type=MESH)
            cp.start()
            arrival = pltpu.make_async_remote_copy(src_ref=ins[a].at[theirs], dst_ref=outs[a].at[theirs],
                                                   send_sem=send_sems.at[a], recv_sem=recv_sems.at[a],
                                                   device_id=(x, y, 1 - c), device_id_type=MESH)
            started.append((cp, arrival))
        for cp, arrival in started:
            arrival.wait_recv()
            cp.wait_send()

    return pl.pallas_call(
        body,
        out_shape=tuple(_sds(s.shape, s.dtype) for s in shards),
        in_specs=_hbm_specs(n),
        out_specs=tuple(_hbm_specs(n)),
        scratch_shapes=[pltpu.SemaphoreType.DMA((n,)), pltpu.SemaphoreType.DMA((n,))],
        input_output_aliases={a: a for a in range(n)},
        name=name,
    )(*shards)


def _pair_sum(name, place, g, got, axis):
    hr, hc = got.shape[1:]

    def body(place_ref, g_ref, got_ref, o_ref):
        o_ref[...] = (g_ref[...] + got_ref[...]).astype(o_ref.dtype)

    blk = (None, hr, hc)
    mine = (lambda k, pr: (k, pr[1], 0)) if axis == 0 else (lambda k, pr: (k, 0, pr[1]))
    return pl.pallas_call(
        body,
        out_shape=_sds((N_CHIPS, hr, hc), BF16),
        grid_spec=pltpu.PrefetchScalarGridSpec(
            num_scalar_prefetch=1,
            grid=(N_CHIPS,),
            in_specs=[pl.BlockSpec(blk, mine), pl.BlockSpec(blk, lambda k, pr: (k, 0, 0))],
            out_specs=pl.BlockSpec(blk, lambda k, pr: (k, 0, 0)),
        ),
        compiler_params=pltpu.CompilerParams(dimension_semantics=("parallel",)),
        name=name,
    )(place, g, got)


def _chip_sum(name, place, g, got, arrivals, axis):
    _, r, cdim = g.shape
    hr, hc = got.shape[1:]

    def body(place_ref, g_ref, got_ref, arr_ref, o_ref):
        acc = g_ref[...] + got_ref[...]
        for j in range(3):
            acc = acc + arr_ref[j].astype(F32)
        o_ref[...] = acc

    blk = (None, hr, hc)
    mine = (lambda i, pr: (pr[0], pr[1], 0)) if axis == 0 else (lambda i, pr: (pr[0], 0, pr[1]))
    dest = (lambda i, pr: (pr[1], 0)) if axis == 0 else (lambda i, pr: (0, pr[1]))
    return pl.pallas_call(
        body,
        out_shape=_sds((r, cdim), F32),
        grid_spec=pltpu.PrefetchScalarGridSpec(
            num_scalar_prefetch=1,
            grid=(1,),
            in_specs=[
                pl.BlockSpec(blk, mine),
                pl.BlockSpec(blk, lambda i, pr: (pr[0], 0, 0)),
                pl.BlockSpec((3, hr, hc), lambda i, pr: (0, 0, 0)),
            ],
            out_specs=pl.BlockSpec((hr, hc), dest),
        ),
        compiler_params=pltpu.CompilerParams(dimension_semantics=("arbitrary",)),
        name=name,
    )(place, g, got, arrivals)


def _device_sum(name, gathered):
    m_per = gathered.shape[0] // N_DEV

    def body(g_ref, o_ref):
        acc = g_ref[0:m_per, :]
        for dev in range(1, N_DEV):
            acc = acc + g_ref[dev * m_per:(dev + 1) * m_per, :]
        o_ref[...] = acc

    return pl.pallas_call(body, out_shape=_sds((m_per, gathered.shape[1]), F32), name=name)(gathered)


def _adamw(name, w, g, m, v):
    r, cdim = w.shape
    if r % 8 == 0:
        tr, tcol = _tile(r, 256, 8), cdim
    else:
        tr, tcol = r, (_tile(cdim, 256, LANES) if cdim % LANES == 0 else cdim)
    blk = pl.BlockSpec((tr, tcol), lambda i, j: (i, j))
    grid = (r // tr, cdim // tcol)
    bc1 = 1.0 - ADAM_B1 ** ADAM_STEP
    bc2 = 1.0 - ADAM_B2 ** ADAM_STEP

    def body(w_ref, g_ref, m_ref, v_ref, d_ref, nm_ref, nv_ref):
        gv = g_ref[...]
        nm = ADAM_B1 * m_ref[...] + (1.0 - ADAM_B1) * gv
        nv = ADAM_B2 * v_ref[...] + (1.0 - ADAM_B2) * (gv * gv)
        d_ref[...] = -ADAM_LR * ((nm / bc1) / (jnp.sqrt(nv / bc2) + ADAM_EPS) + ADAM_WD * w_ref[...])
        nm_ref[...] = nm
        nv_ref[...] = nv

    shape = _sds(w.shape, F32)
    return pl.pallas_call(
        body,
        out_shape=(shape, shape, shape),
        grid=grid,
        in_specs=[blk] * 4,
        out_specs=(blk, blk, blk),
        compiler_params=pltpu.CompilerParams(dimension_semantics=("parallel", "parallel")),
        name=name,
    )(w, g, m, v)


def _cat_cols(g):
    return jnp.transpose(g, (1, 0, 2)).reshape(g.shape[1], N_CHIPS * g.shape[2])


def _split_cols(a):
    r, c4 = a.shape
    return jnp.transpose(a.reshape(r, N_CHIPS, c4 // N_CHIPS), (1, 0, 2))


def _local_step(x, target, w_int, w_oc, w_oa, late_weights, cmw, cfw, g1, b_f, b_gate, g2, gf,
                ffn_grads_ready, mix_grads_ready):
    batch, seq, d = x.shape
    t = batch * seq
    cw = d // 2
    fh = cfw.shape[1] // 2
    tc = LANES
    nct = cw // tc
    tq = min(512, seq)
    pc_w, qkv_w, gl_w = 3 * cw, 3 * ATTN_WIDTH, 2 * d
    qkv_off, gl_off, f_off = pc_w, pc_w + qkv_w, pc_w + qkv_w + gl_w
    width = f_off + F_PAD
    f_col = pc_w + qkv_w

    w_pc = w_int[:pc_w].reshape(3, nct, tc, d).transpose(1, 0, 2, 3).reshape(pc_w, d)
    w_qkv = w_int[pc_w:f_col].reshape(3, HEAD_PAIRS, LANES, d).transpose(1, 0, 2, 3).reshape(qkv_w, d)
    w_f = jnp.pad(w_int[f_col:f_col + HEADS], ((0, F_PAD - HEADS), (0, 0)))
    w_inp = jnp.concatenate([w_pc, w_qkv, w_int[f_col + HEADS:], w_f], axis=0)
    bf_pad = jnp.pad(b_f, ((0, 0), (0, F_PAD - HEADS)))

    x2d = x.reshape(t, d)
    tgt2d = target.reshape(t, d)

    h1 = _rms_fwd("norm_mix", x2d, g1)
    pc = _mm("proj_conv", h1, w_inp, "nt", BF16, m=t, n=pc_w, k=d, b_roff=0)
    qkv = _mm("proj_qkv", h1, w_inp, "nt", BF16, m=t, n=qkv_w, k=d, b_roff=qkv_off)
    gl = _mm("proj_gate", h1, w_inp, "nt", BF16, m=t, n=gl_w, k=d, b_roff=gl_off)
    fl = _mm("proj_forget", h1, w_inp, "nt", F32, m=t, n=F_PAD, k=d, b_roff=f_off)
    a_c = _conv_fwd("conv_mix", pc, cmw, batch, seq, tc)
    f_cum = _forget_fwd("forget_cumsum", fl, bf_pad, batch, seq)
    frow = f_cum.reshape(batch, HEAD_PAIRS, 2, seq)
    o, lse = _attn_fwd("attn_fwd", qkv, frow, batch, seq, tq)
    ycat = _mm("out_conv", a_c, w_oc, "nn", BF16, m=t, n=d, k=cw, o_off=0, o_width=2 * d)
    ycat = _mm("out_attn", o, w_oa, "nn", BF16, m=t, n=d, k=ATTN_WIDTH, out=ycat, o_off=d)
    mg = _merge_fwd("gate_merge", ycat, gl, b_gate)
    w_o, w_up, w_down = late_weights(mg)
    x2 = _mm("mix_out", mg, w_o, "nn", F32, m=t, n=d, k=d, add=x2d)
    h2 = _rms_fwd("norm_ffn", x2, g2)
    tcf = min(2 * LANES, fh)
    hmid, ua, ub = _ffn_up_act("ffn_up_act", h2, _cat_cols(w_up), cfw, batch, seq, tcf)
    x3 = _mm("ffn_down", hmid, w_down, "nn", F32, m=t, n=d, k=fh, add=x2, tk=4096)

    dx3, dx3b, loss_row, d_gf = _final_loss("final_loss", x3, gf.reshape(1, d), tgt2d)
    dw_down = _mm("dw_down", hmid, dx3b, "tn", F32, m=fh, n=d, k=t, tm=1408, tk=2048)
    du_a, du_b, d_cfw = _ffn_bwd("d_ffn", dx3b, w_down, ua, ub, cfw, batch, seq, tcf)
    ws = w_up.shape[2]
    dh2 = _mm("d_norm_ffn_a", du_a, w_up, "nt", BF16, m=t, n=d, k=fh, b_off=0, b3=True)
    dh2 = _mm("d_norm_ffn_b", du_b, w_up, "nt", BF16, m=t, n=d, k=fh, b_off=fh, b3=True, add=dh2)
    dw_up = _mm("dw_up_a", h2, du_a, "tn", F32, m=d, n=fh, k=t, tn=ws, tk=2048, o3=N_CHIPS)
    dw_up = _mm("dw_up_b", h2, du_b, "tn", F32, m=d, n=fh, k=t, tn=ws, tk=2048, o3=N_CHIPS, out=dw_up, o_off=fh)
    token = ffn_grads_ready(dw_up, dw_down)
    if token is not None:
        g2 = g2 + token[0:1, 0:1]
    dx2, d_g2 = _rms_bwd("d_norm_ffn", x2, dh2, g2, dx3)
    dm = _mm("d_merge", dx2, w_o, "nt", BF16, m=t, n=d, k=d)
    dw_o = _mm("dw_o", mg, dx2, "tn", F32, m=d, n=d, k=t, tk=2048)
    dproj, dycat, d_bg = _merge_bwd("d_gate_merge", dm, ycat, gl, b_gate, width, gl_off)
    da_c = _mm("d_conv_out", dycat, w_oc, "nt", BF16, m=t, n=cw, k=d, a_off=0)
    do = _mm("d_attn_out", dycat, w_oa, "nt", BF16, m=t, n=ATTN_WIDTH, k=d, a_off=d)
    dw_oc = _mm("dw_out_conv", a_c, dycat, "tn", F32, m=cw, n=d, k=t, b_off=0, tk=2048)
    dw_oa = _mm("dw_out_attn", o, dycat, "tn", F32, m=ATTN_WIDTH, n=d, k=t, b_off=d, tk=2048)
    dproj, d_cmw = _conv_bwd("d_conv_mix", da_c, pc, cmw, dproj, batch, seq, tc)
    dproj, d_fkey, d_fquery = _attn_bwd("attn_bwd", qkv, do, o, lse, frow, dproj, qkv_off, batch, seq, tq)
    d_fquery = jnp.pad(jnp.transpose(d_fquery, (1, 0, 2)).reshape(t, HEADS), ((0, 0), (0, LANES - HEADS)))
    dproj, d_bf = _forget_bwd("d_forget", d_fkey.reshape(batch, HEADS, seq), d_fquery, fl, bf_pad, dproj, f_off,
                              batch, seq)
    dw_inp = _mm("dw_in", dproj, h1, "tn", F32, m=width, n=d, k=t, tm=1792, tk=2048)
    d_pc = dw_inp[:pc_w].reshape(nct, 3, tc, d).transpose(1, 0, 2, 3).reshape(pc_w, d)
    d_qkv = dw_inp[qkv_off:gl_off].reshape(HEAD_PAIRS, 3, LANES, d).transpose(1, 0, 2, 3).reshape(qkv_w, d)
    dw_int = jnp.concatenate([d_pc, d_qkv, dw_inp[f_off:f_off + HEADS], dw_inp[gl_off:f_off]], axis=0)
    token = mix_grads_ready(dw_int, dw_oc, dw_oa, dw_o)
    dh1 = _mm("d_norm_mix", dproj, w_inp, "nn", BF16, m=t, n=d, k=width, tk=1792, dep=token)
    grad_x, d_g1 = _rms_bwd("d_norm_mix_x", x2d, dh1, g1, dx2)
    smalls = (d_g1, d_g2, d_gf, d_bg, d_bf, d_cmw, d_cfw)
    return loss_row[0, 0], grad_x.reshape(batch, seq, d), smalls


def _pack_small(parts):
    flat = [p.reshape(-1) for p in parts]
    sizes = [f.shape[0] for f in flat]
    total = sum(sizes)
    padded = -(-total // (8 * LANES)) * (8 * LANES)
    vec = jnp.concatenate(flat + [jnp.zeros((padded - total,), F32)])
    offsets = [sum(sizes[:i]) for i in range(len(sizes))]
    return vec.reshape(padded // LANES, LANES), offsets


def kernel(x, norm_mix_g, w_in, b_f, b_gate, conv_mix_w, w_out_conv, w_out_attn, w_o, norm_ffn_g, w_up, conv_ffn_w, w_down, norm_f_g, loss_target, m_norm_mix_g, m_w_in, m_b_f, m_b_gate, m_conv_mix_w, m_w_out_conv, m_w_out_attn, m_w_o, m_norm_ffn_g, m_w_up, m_conv_ffn_w, m_w_down, m_norm_f_g, v_norm_mix_g, v_w_in, v_b_f, v_b_gate, v_conv_mix_w, v_w_out_conv, v_w_out_attn, v_w_o, v_norm_ffn_g, v_w_up, v_conv_ffn_w, v_w_down, v_norm_f_g):
    d = x.shape[-1]
    chip = 2 * lax.axis_index("x") + lax.axis_index("y")
    place = jnp.stack([chip, lax.axis_index("c")]).astype(jnp.int32)

    t_in, t_m_in, t_v_in = (jnp.transpose(w[0]) for w in (w_in, m_w_in, v_w_in))

    def row_shards(a):
        return a.reshape(N_CHIPS, a.shape[0] // N_CHIPS, a.shape[1])

    def stacked(a):
        return a.reshape(N_CHIPS * a.shape[1], a.shape[2])

    first = [t_in.astype(BF16), w_out_conv[0].astype(BF16), w_out_attn[0].astype(BF16)]
    a_in, a_oc, a_oa, a_cmw, a_cfw = _gather_weights(first, (1, 0, 0), [conv_mix_w[0], conv_ffn_w[0]])
    late = [w[0].astype(BF16) for w in (w_o, w_up, w_down)]
    late_handles, late_token = _chips_start("gather_late_start", "gather", late)

    def late_weights(after):
        lands = _chips_wait("gather_late_wait", "gather", late_handles, after)
        a_o, a_up, a_down = (lax.dynamic_update_index_in_dim(buf, own, chip, 0) for buf, own in zip(lands, late))
        return stacked(a_o), a_up, stacked(a_down)

    pending = []

    def reduce_start(tag, names, grads, axes):
        got = _exchange_sibling("exchange_sibling_" + tag, grads, axes)
        sums = [_pair_sum("pair_sum_" + nm, place, g, r, ax) for nm, g, r, ax in zip(names, grads, got, axes)]
        handles, token = _chips_start("exchange_chips_start_" + tag, "reduce", sums)
        pending.append((tag, names, grads, axes, got, handles))
        return token

    def ffn_grads_ready(dw_up, dw_down):
        return reduce_start("ffn", ("w_up", "w_down"), [dw_up, row_shards(dw_down)], (0, 0))

    def mix_grads_ready(dw_int, dw_oc, dw_oa, dw_o):
        return reduce_start("mix", ("w_in", "w_out_conv", "w_out_attn", "w_o"),
                            [row_shards(dw_int), _split_cols(dw_oc), _split_cols(dw_oa), row_shards(dw_o)],
                            (1, 0, 0, 0))

    loss_local, grad_x, smalls = _local_step(
        x, loss_target, stacked(a_in), _cat_cols(a_oc), _cat_cols(a_oa), late_weights, _cat_cols(a_cmw),
        _cat_cols(a_cfw), norm_mix_g + late_token[0:1, 0:1], b_f, b_gate, norm_ffn_g, norm_f_g,
        ffn_grads_ready, mix_grads_ready)

    reduced = {}
    for tag, names, grads, axes, got, handles in pending:
        arrivals = _chips_wait("exchange_chips_wait_" + tag, "reduce", handles, grad_x)
        halves = [_chip_sum("chip_sum_" + nm, place, g, r, arr, ax)
                  for nm, g, r, arr, ax in zip(names, grads, got, arrivals, axes)]
        reduced.update(zip(names, _share_sibling("share_sibling_" + tag, halves, axes)))
    g_in, g_oc, g_oa, g_o, g_up, g_down = (
        reduced[nm] for nm in ("w_in", "w_out_conv", "w_out_attn", "w_o", "w_up", "w_down"))

    packed, offs = _pack_small(smalls)
    total = _device_sum("device_sum", _gather_small(packed)).reshape(-1)
    shapes = [s.shape for s in smalls]
    d_g1, d_g2, d_gf, d_bg, d_bf, d_cmw, d_cfw = [
        total[o:o + math.prod(sh)].reshape(sh) for o, sh in zip(offs, shapes)]
    d_bf = d_bf[:, :HEADS]
    cw_s, cf_s = conv_mix_w.shape[2], conv_ffn_w.shape[2]
    d_cmw = lax.dynamic_slice(d_cmw, (0, chip * cw_s), (3, cw_s))
    d_cfw = lax.dynamic_slice(d_cfw, (0, chip * cf_s), (3, cf_s))

    loss = lax.psum(loss_local, ("x", "y", "c"))
    order = [
        ("norm_mix_g", norm_mix_g[0:1], d_g1, m_norm_mix_g, v_norm_mix_g),
        ("w_in", t_in, g_in, t_m_in, t_v_in),
        ("b_f", b_f, d_bf, m_b_f, v_b_f),
        ("b_gate", b_gate, d_bg, m_b_gate, v_b_gate),
        ("conv_mix_w", conv_mix_w[0], d_cmw, m_conv_mix_w[0], v_conv_mix_w[0]),
        ("w_out_conv", w_out_conv[0], g_oc, m_w_out_conv[0], v_w_out_conv[0]),
        ("w_out_attn", w_out_attn[0], g_oa, m_w_out_attn[0], v_w_out_attn[0]),
        ("w_o", w_o[0], g_o, m_w_o[0], v_w_o[0]),
        ("norm_ffn_g", norm_ffn_g, d_g2, m_norm_ffn_g, v_norm_ffn_g),
        ("w_up", w_up[0], g_up, m_w_up[0], v_w_up[0]),
        ("conv_ffn_w", conv_ffn_w[0], d_cfw, m_conv_ffn_w[0], v_conv_ffn_w[0]),
        ("w_down", w_down[0], g_down, m_w_down[0], v_w_down[0]),
        ("norm_f_g", norm_f_g.reshape(1, d), d_gf, m_norm_f_g.reshape(1, d), v_norm_f_g.reshape(1, d)),
    ]
    out_shapes = [norm_mix_g.shape, w_in.shape, b_f.shape, b_gate.shape, conv_mix_w.shape, w_out_conv.shape,
                  w_out_attn.shape, w_o.shape, norm_ffn_g.shape, w_up.shape, conv_ffn_w.shape, w_down.shape,
                  norm_f_g.shape]
    g_out, d_out, m_out, v_out = [], [], [], []
    for (nm, w, g, m, v), sh in zip(order, out_shapes):
        g = g.reshape(w.shape)
        delta, new_m, new_v = _adamw("adamw_" + nm, w, g, m.reshape(w.shape), v.reshape(w.shape))
        for dst, val in ((g_out, g), (d_out, delta), (m_out, new_m), (v_out, new_v)):
            dst.append((jnp.transpose(val) if nm == "w_in" else val).reshape(sh))
    return (loss, grad_x, *g_out, *d_out, *m_out, *v_out)
```

```python
import functools
import math

import jax
import jax.numpy as jnp
from jax import lax
from jax.experimental import pallas as pl
from jax.experimental.pallas import tpu as pltpu

F32 = jnp.float32
BF16 = jnp.bfloat16
MESH = pl.DeviceIdType.MESH

EPS = 1e-6
HEADS = 8
HEAD_DIM = 64
ATTN_WIDTH = HEADS * HEAD_DIM
HEAD_PAIRS = HEADS // 2
LANES = 128
F_PAD = 2 * LANES
NEG_BIG = -1e30
N_CHIPS = 4
N_DEV = 8

ADAM_LR = 0.001
ADAM_B1 = 0.9
ADAM_B2 = 0.999
ADAM_EPS = 1e-08
ADAM_WD = 0.01
ADAM_STEP = 10

_DIMS = {
    "nn": (((1,), (0,)), ((), ())),
    "nt": (((1,), (1,)), ((), ())),
    "tn": (((0,), (0,)), ((), ())),
}


def _tile(n, target, mult, also=()):
    best = None
    for t in range(mult, n + 1, mult):
        if n % t == 0 and t <= target and all(o % t == 0 for o in also):
            best = t
    if best is None:
        assert all(o == 0 for o in also), (n, target, mult, also)
        return n
    return best


def _sds(shape, dtype):
    return jax.ShapeDtypeStruct(shape, dtype)


def _mm(name, a, b, mode, out_dtype, *, m, n, k, a_off=0, b_off=0, b_roff=0, b3=False, out=None, o_off=0,
        o_width=None, o3=None, add=None, dep=None, tm=1024, tn=2048, tk=2048):
    wb = b.shape[2] if b3 else None
    if mode == "nn":
        tm = _tile(m, tm, 16)
        tk = _tile(k, tk, LANES, (a_off,))
        tn = wb if b3 else _tile(n, tn, LANES, (b_off, o_off))
        a_spec = pl.BlockSpec((tm, tk), lambda i, j, kk: (i, a_off // tk + kk))
        if b3:
            b_spec = pl.BlockSpec((None, tk, tn), lambda i, j, kk: (b_off // tn + j, kk, 0))
        else:
            b_spec = pl.BlockSpec((tk, tn), lambda i, j, kk: (kk, b_off // tn + j))
    elif mode == "nt":
        tm = _tile(m, tm, 16)
        tk = wb if b3 else _tile(k, tk, LANES, (a_off, b_off))
        tn = _tile(n, tn, LANES, (o_off, b_roff))
        a_spec = pl.BlockSpec((tm, tk), lambda i, j, kk: (i, a_off // tk + kk))
        if b3:
            b_spec = pl.BlockSpec((None, tn, tk), lambda i, j, kk: (b_off // tk + kk, b_roff // tn + j, 0))
        else:
            b_spec = pl.BlockSpec((tn, tk), lambda i, j, kk: (b_roff // tn + j, b_off // tk + kk))
    else:
        tm = _tile(m, tm, LANES, (a_off,))
        tk = _tile(k, tk, 16)
        tn = _tile(n, tn, LANES, (b_off, o_off))
        a_spec = pl.BlockSpec((tk, tm), lambda i, j, kk: (kk, a_off // tm + i))
        b_spec = pl.BlockSpec((tk, tn), lambda i, j, kk: (kk, b_off // tn + j))
    assert m % tm == 0 and n % tn == 0 and k % tk == 0, (name, tm, tn, tk)
    nk = k // tk
    if o3 is not None:
        o_spec = pl.BlockSpec((None, tm, tn), lambda i, j, kk: (o_off // tn + j, i, 0))
        out_sds = _sds((o3, m, tn), out_dtype)
    else:
        o_spec = pl.BlockSpec((tm, tn), lambda i, j, kk: (i, o_off // tn + j))
        width = o_width if o_width is not None else (out.shape[1] if out is not None else n)
        out_sds = _sds((m, width), out_dtype)
    use_acc = nk > 1 and out_dtype != F32
    dims = _DIMS[mode]
    has_add, has_out = add is not None, out is not None

    def body(*refs):
        a_ref, b_ref = refs[0], refs[1]
        pos = 2
        add_ref = None
        if has_add:
            add_ref = refs[pos]
            pos += 1
        if has_out:
            pos += 1
        if dep is not None:
            pos += 1
        o_ref = refs[pos]
        acc_ref = refs[pos + 1] if use_acc else None
        part = lax.dot_general(a_ref[...].astype(BF16), b_ref[...].astype(BF16), dims,
                               preferred_element_type=F32)
        if nk == 1:
            if has_add:
                part = part + add_ref[...]
            o_ref[...] = part.astype(o_ref.dtype)
            return
        kk = pl.program_id(2)
        tgt = acc_ref if use_acc else o_ref

        @pl.when(kk == 0)
        def _():
            tgt[...] = part + add_ref[...] if has_add else part

        @pl.when(kk > 0)
        def _():
            tgt[...] += part

        if use_acc:
            @pl.when(kk == nk - 1)
            def _():
                o_ref[...] = acc_ref[...].astype(o_ref.dtype)

    operands, in_specs = [a, b], [a_spec, b_spec]
    if has_add:
        operands.append(add)
        in_specs.append(pl.BlockSpec((tm, tn), lambda i, j, kk: (i, j)))
    aliases = {}
    if has_out:
        aliases = {len(operands): 0}
        operands.append(out)
        in_specs.append(pl.BlockSpec(memory_space=pl.ANY))
    if dep is not None:
        operands.append(dep)
        in_specs.append(pl.BlockSpec(memory_space=pl.ANY))
    return pl.pallas_call(
        body,
        out_shape=out_sds,
        grid=(m // tm, n // tn, nk),
        in_specs=in_specs,
        out_specs=o_spec,
        scratch_shapes=[pltpu.VMEM((tm, tn), F32)] if use_acc else [],
        input_output_aliases=aliases,
        compiler_params=pltpu.CompilerParams(dimension_semantics=("parallel", "parallel", "arbitrary")),
        name=name,
    )(*operands)


def _rms_fwd(name, x, g):
    t, d = x.shape
    tm = _tile(t, 512, 16)

    def body(x_ref, g_ref, o_ref):
        xv = x_ref[...]
        r = lax.rsqrt(jnp.mean(xv * xv, axis=-1, keepdims=True) + EPS)
        o_ref[...] = ((xv * r) * g_ref[...]).astype(o_ref.dtype)

    return pl.pallas_call(
        body,
        out_shape=_sds((t, d), BF16),
        grid=(t // tm,),
        in_specs=[pl.BlockSpec((tm, d), lambda i: (i, 0)), pl.BlockSpec((1, d), lambda i: (0, 0))],
        out_specs=pl.BlockSpec((tm, d), lambda i: (i, 0)),
        compiler_params=pltpu.CompilerParams(dimension_semantics=("parallel",)),
        name=name,
    )(x, g)


def _rms_bwd(name, x, dh, g, res):
    t, d = x.shape
    tm = _tile(t, 512, 16)

    def body(x_ref, dh_ref, g_ref, res_ref, dx_ref, dg_ref):
        xv = x_ref[...]
        r = lax.rsqrt(jnp.mean(xv * xv, axis=-1, keepdims=True) + EPS)
        xh = xv * r
        dhv = dh_ref[...].astype(F32)
        dxh = dhv * g_ref[...]
        dx_ref[...] = res_ref[...] + r * (dxh - xh * jnp.mean(dxh * xh, axis=-1, keepdims=True))

        @pl.when(pl.program_id(0) == 0)
        def _():
            dg_ref[...] = jnp.zeros_like(dg_ref)

        dg_ref[...] += jnp.sum(dhv * xh, axis=0, keepdims=True)

    row = pl.BlockSpec((tm, d), lambda i: (i, 0))
    vec = pl.BlockSpec((1, d), lambda i: (0, 0))
    return pl.pallas_call(
        body,
        out_shape=(_sds((t, d), F32), _sds((1, d), F32)),
        grid=(t // tm,),
        in_specs=[row, row, vec, row],
        out_specs=(row, vec),
        compiler_params=pltpu.CompilerParams(dimension_semantics=("arbitrary",)),
        name=name,
    )(x, dh, g, res)


def _final_loss(name, x, g, target):
    t, d = x.shape
    tm = _tile(t, 512, 16)

    def body(x_ref, g_ref, t_ref, dx_ref, dxb_ref, loss_ref, dg_ref):
        xv = x_ref[...]
        gv = g_ref[...]
        r = lax.rsqrt(jnp.mean(xv * xv, axis=-1, keepdims=True) + EPS)
        xh = xv * r
        err = xh * gv - t_ref[...]
        dy = err * (1.0 / d)
        dxh = dy * gv
        dx = r * (dxh - xh * jnp.mean(dxh * xh, axis=-1, keepdims=True))
        dx_ref[...] = dx
        dxb_ref[...] = dx.astype(dxb_ref.dtype)
        per_row = jnp.sum(err * err, axis=-1, keepdims=True) * (0.5 / d)

        @pl.when(pl.program_id(0) == 0)
        def _():
            dg_ref[...] = jnp.zeros_like(dg_ref)
            loss_ref[...] = jnp.zeros_like(loss_ref)

        dg_ref[...] += jnp.sum(dy * xh, axis=0, keepdims=True)
        loss_ref[...] += jnp.sum(per_row, axis=0, keepdims=True)

    row = pl.BlockSpec((tm, d), lambda i: (i, 0))
    vec = pl.BlockSpec((1, d), lambda i: (0, 0))
    return pl.pallas_call(
        body,
        out_shape=(_sds((t, d), F32), _sds((t, d), BF16), _sds((1, LANES), F32), _sds((1, d), F32)),
        grid=(t // tm,),
        in_specs=[row, vec, row],
        out_specs=(row, row, pl.BlockSpec((1, LANES), lambda i: (0, 0)), vec),
        compiler_params=pltpu.CompilerParams(dimension_semantics=("arbitrary",)),
        name=name,
    )(x, g, target)


def _shift_down(z, k):
    row = lax.broadcasted_iota(jnp.int32, z.shape, 0)
    return jnp.where(row >= k, pltpu.roll(z, k, axis=0), 0.0)


def _shift_up(z, k):
    s = z.shape[0]
    row = lax.broadcasted_iota(jnp.int32, z.shape, 0)
    return jnp.where(row < s - k, pltpu.roll(z, s - k, axis=0), 0.0)


def _conv3(z, w):
    return (w[2:3] * z + w[0:1] * _shift_down(z, 2)) + w[1:2] * _shift_down(z, 1)


def _conv3_t(dz, w):
    return (w[2:3] * dz + w[0:1] * _shift_up(dz, 2)) + w[1:2] * _shift_up(dz, 1)


def _conv_fwd(name, pc, w, batch, seq, tc):
    cw = w.shape[1]
    nct = cw // tc

    def body(pc_ref, w_ref, o_ref):
        cb = pc_ref[:, 0:tc].astype(F32)
        z = pc_ref[:, tc:2 * tc].astype(F32) * pc_ref[:, 2 * tc:3 * tc].astype(F32)
        o_ref[...] = (cb * _conv3(z, w_ref[...])).astype(o_ref.dtype)

    return pl.pallas_call(
        body,
        out_shape=_sds((batch * seq, cw), BF16),
        grid=(batch, nct),
        in_specs=[pl.BlockSpec((seq, 3 * tc), lambda b, j: (b, j)), pl.BlockSpec((3, tc), lambda b, j: (0, j))],
        out_specs=pl.BlockSpec((seq, tc), lambda b, j: (b, j)),
        compiler_params=pltpu.CompilerParams(dimension_semantics=("parallel", "parallel")),
        name=name,
    )(pc, w)


def _conv_bwd(name, da, pc, w, dproj, batch, seq, tc):
    cw = w.shape[1]
    nct = cw // tc

    def body(da_ref, pc_ref, w_ref, _, dpc_ref, dw_ref):
        wv = w_ref[...]
        cb = pc_ref[:, 0:tc].astype(F32)
        cc = pc_ref[:, tc:2 * tc].astype(F32)
        cin = pc_ref[:, 2 * tc:3 * tc].astype(F32)
        z = cc * cin
        dav = da_ref[...].astype(F32)
        du = dav * cb
        dz = _conv3_t(du, wv)
        dpc_ref[:, 0:tc] = (dav * _conv3(z, wv)).astype(dpc_ref.dtype)
        dpc_ref[:, tc:2 * tc] = (dz * cin).astype(dpc_ref.dtype)
        dpc_ref[:, 2 * tc:3 * tc] = (dz * cc).astype(dpc_ref.dtype)

        @pl.when(pl.program_id(1) == 0)
        def _():
            dw_ref[...] = jnp.zeros_like(dw_ref)

        dw_ref[0:1, :] += jnp.sum(du * _shift_down(z, 2), axis=0, keepdims=True)
        dw_ref[1:2, :] += jnp.sum(du * _shift_down(z, 1), axis=0, keepdims=True)
        dw_ref[2:3, :] += jnp.sum(du * z, axis=0, keepdims=True)

    return pl.pallas_call(
        body,
        out_shape=(_sds(dproj.shape, dproj.dtype), _sds((3, cw), F32)),
        grid=(nct, batch),
        in_specs=[
            pl.BlockSpec((seq, tc), lambda j, b: (b, j)),
            pl.BlockSpec((seq, 3 * tc), lambda j, b: (b, j)),
            pl.BlockSpec((3, tc), lambda j, b: (0, j)),
            pl.BlockSpec(memory_space=pl.ANY),
        ],
        out_specs=(pl.BlockSpec((seq, 3 * tc), lambda j, b: (b, j)), pl.BlockSpec((3, tc), lambda j, b: (0, j))),
        input_output_aliases={3: 0},
        compiler_params=pltpu.CompilerParams(dimension_semantics=("parallel", "arbitrary")),
        name=name,
    )(da, pc, w, dproj)


def _ffn_up_act(name, h2, w_up, w, batch, seq, tc):
    d = h2.shape[1]
    fh = w.shape[1] // 2
    nf = fh // tc

    def body(h_ref, ma_ref, mb_ref, wa_ref, wb_ref, o_ref, ua_ref, ub_ref):
        hv = h_ref[...]
        ua = _dot(hv, ma_ref[...], "nn")
        ub = _dot(hv, mb_ref[...], "nn")
        ua_ref[...] = ua.astype(ua_ref.dtype)
        ub_ref[...] = ub.astype(ub_ref.dtype)
        a = _conv3(ua, wa_ref[...])
        b = _conv3(ub, wb_ref[...])
        o_ref[...] = (a * jax.nn.sigmoid(a) * b).astype(o_ref.dtype)

    act = pl.BlockSpec((seq, tc), lambda b, j: (b, j))
    shape = _sds((batch * seq, fh), BF16)
    return pl.pallas_call(
        body,
        out_shape=(shape, shape, shape),
        grid=(batch, nf),
        in_specs=[
            pl.BlockSpec((seq, d), lambda b, j: (b, 0)),
            pl.BlockSpec((d, tc), lambda b, j: (0, j)),
            pl.BlockSpec((d, tc), lambda b, j: (0, nf + j)),
            pl.BlockSpec((3, tc), lambda b, j: (0, j)),
            pl.BlockSpec((3, tc), lambda b, j: (0, nf + j)),
        ],
        out_specs=(act, act, act),
        compiler_params=pltpu.CompilerParams(dimension_semantics=("parallel", "parallel")),
        name=name,
    )(h2, w_up, w_up, w, w)


def _ffn_bwd(name, dx, w_down, ua, ub, w, batch, seq, tc):
    d = dx.shape[1]
    fh = w.shape[1] // 2
    nf = fh // tc

    def body(dx_ref, md_ref, ua_ref, ub_ref, wa_ref, wb_ref, dua_ref, dub_ref, dw_ref):
        j = pl.program_id(1)
        uav, ubv, wa, wb = ua_ref[...].astype(F32), ub_ref[...].astype(F32), wa_ref[...], wb_ref[...]
        dhv = _dot(dx_ref[...].astype(BF16), md_ref[...], "nt")
        a = _conv3(uav, wa)
        b = _conv3(ubv, wb)
        sg = jax.nn.sigmoid(a)
        da = dhv * b * (sg * (1.0 + a * (1.0 - sg)))
        db = dhv * (a * sg)
        dua_ref[...] = _conv3_t(da, wa).astype(dua_ref.dtype)
        dub_ref[...] = _conv3_t(db, wb).astype(dub_ref.dtype)

        @pl.when((pl.program_id(0) == 0) & (j == 0))
        def _():
            dw_ref[...] = jnp.zeros_like(dw_ref)

        for off, dv, uv in ((0, da, uav), (fh, db, ubv)):
            cols = pl.ds(pl.multiple_of(off + j * tc, LANES), tc)
            dw_ref[0:1, cols] += jnp.sum(dv * _shift_down(uv, 2), axis=0, keepdims=True)
            dw_ref[1:2, cols] += jnp.sum(dv * _shift_down(uv, 1), axis=0, keepdims=True)
            dw_ref[2:3, cols] += jnp.sum(dv * uv, axis=0, keepdims=True)

    act = pl.BlockSpec((seq, tc), lambda b, j: (b, j))
    shape = _sds((batch * seq, fh), BF16)
    return pl.pallas_call(
        body,
        out_shape=(shape, shape, _sds((3, 2 * fh), F32)),
        grid=(batch, nf),
        in_specs=[
            pl.BlockSpec((seq, d), lambda b, j: (b, 0)),
            pl.BlockSpec((tc, d), lambda b, j: (j, 0)),
            act,
            act,
            pl.BlockSpec((3, tc), lambda b, j: (0, j)),
            pl.BlockSpec((3, tc), lambda b, j: (0, nf + j)),
        ],
        out_specs=(act, act, pl.BlockSpec((3, 2 * fh), lambda b, j: (0, 0))),
        compiler_params=pltpu.CompilerParams(dimension_semantics=("arbitrary", "arbitrary")),
        name=name,
    )(dx, w_down, ua, ub, w, w)


def _merge_fwd(name, ycat, gl, bg):
    t, d2 = ycat.shape
    d = d2 // 2
    tm = _tile(t, 256, 16)

    def body(y_ref, gl_ref, bg_ref, o_ref):
        g = jax.nn.sigmoid(gl_ref[...].astype(F32) + bg_ref[...])
        prod = g * y_ref[...].astype(F32)
        o_ref[...] = (prod[:, 0:d] + prod[:, d:d2]).astype(o_ref.dtype)

    row = pl.BlockSpec((tm, d2), lambda i: (i, 0))
    return pl.pallas_call(
        body,
        out_shape=_sds((t, d), BF16),
        grid=(t // tm,),
        in_specs=[row, row, pl.BlockSpec((1, d2), lambda i: (0, 0))],
        out_specs=pl.BlockSpec((tm, d), lambda i: (i, 0)),
        compiler_params=pltpu.CompilerParams(dimension_semantics=("parallel",)),
        name=name,
    )(ycat, gl, bg)


def _merge_bwd(name, dm, ycat, gl, bg, width, gl_off):
    t, d2 = ycat.shape
    d = d2 // 2
    tm = _tile(t, 512, 16)
    wb = math.gcd(gl_off, d)
    nw = d // wb

    def body(dm_ref, y_ref, gl_ref, bg_ref, dgl_ref, dy_ref, dbg_ref):
        g = jax.nn.sigmoid(gl_ref[...].astype(F32) + bg_ref[...])
        dmv = dm_ref[...].astype(F32)
        dgl = dmv * y_ref[...].astype(F32) * (g * (1.0 - g))
        dgl_ref[...] = dgl.astype(dgl_ref.dtype)
        dy_ref[...] = (dmv * g).astype(dy_ref.dtype)

        @pl.when(pl.program_id(2) == 0)
        def _():
            dbg_ref[...] = jnp.zeros_like(dbg_ref)

        dbg_ref[...] += jnp.sum(dgl, axis=0, keepdims=True)

    half = pl.BlockSpec((tm, wb), lambda h, j, i: (i, h * nw + j))
    vec = pl.BlockSpec((1, wb), lambda h, j, i: (0, h * nw + j))
    return pl.pallas_call(
        body,
        out_shape=(_sds((t, width), BF16), _sds((t, d2), BF16), _sds((1, d2), F32)),
        grid=(2, nw, t // tm),
        in_specs=[pl.BlockSpec((tm, wb), lambda h, j, i: (i, j)), half, half, vec],
        out_specs=(pl.BlockSpec((tm, wb), lambda h, j, i: (i, gl_off // wb + h * nw + j)), half, vec),
        compiler_params=pltpu.CompilerParams(dimension_semantics=("parallel", "parallel", "arbitrary")),
        name=name,
    )(dm, ycat, gl, bg)


def _log_sigmoid(z):
    return jnp.minimum(z, 0.0) - jnp.log1p(jnp.exp(-jnp.abs(z)))


def _forget_fwd(name, fl, bf, batch, seq):
    def body(fl_ref, bf_ref, o_ref):
        lf = _log_sigmoid(fl_ref[:, 0:LANES] + bf_ref[:, 0:LANES])
        acc = lf.T[0:HEADS, :]
        lane = lax.broadcasted_iota(jnp.int32, acc.shape, 1)
        k = 1
        while k < seq:
            acc = acc + jnp.where(lane >= k, pltpu.roll(acc, k, axis=1), 0.0)
            k *= 2
        o_ref[...] = acc

    return pl.pallas_call(
        body,
        out_shape=_sds((batch, HEADS, seq), F32),
        grid=(batch,),
        in_specs=[pl.BlockSpec((seq, F_PAD), lambda b: (b, 0)), pl.BlockSpec((1, F_PAD), lambda b: (0, 0))],
        out_specs=pl.BlockSpec((None, HEADS, seq), lambda b: (b, 0, 0)),
        compiler_params=pltpu.CompilerParams(dimension_semantics=("parallel",)),
        name=name,
    )(fl, bf)


def _forget_bwd(name, d_key, d_query, fl, bf, dproj, f_off, batch, seq):
    nfb = F_PAD // LANES

    def body(dk_ref, dq_ref, fl_ref, bf_ref, _, df_ref, dbf_ref):
        jj = pl.program_id(1)
        key_t = jnp.concatenate([dk_ref[...], jnp.zeros((LANES - HEADS, seq), F32)], axis=0).T
        acc = dq_ref[...] - key_t
        row = lax.broadcasted_iota(jnp.int32, acc.shape, 0)
        k = 1
        while k < seq:
            acc = acc + jnp.where(row < seq - k, pltpu.roll(acc, seq - k, axis=0), 0.0)
            k *= 2
        z = fl_ref[:, 0:LANES] + bf_ref[:, 0:LANES]
        col = lax.broadcasted_iota(jnp.int32, acc.shape, 1)
        df = jnp.where(col < HEADS, acc * jax.nn.sigmoid(-z), 0.0)
        df = jnp.where(jj == 0, df, 0.0)
        df_ref[...] = df.astype(df_ref.dtype)

        @pl.when((pl.program_id(0) == 0) & (jj == 0))
        def _():
            dbf_ref[...] = jnp.zeros_like(dbf_ref)

        dbf_ref[...] += jnp.sum(df, axis=0, keepdims=True)

    return pl.pallas_call(
        body,
        out_shape=(_sds(dproj.shape, dproj.dtype), _sds((1, LANES), F32)),
        grid=(batch, nfb),
        in_specs=[
            pl.BlockSpec((None, HEADS, seq), lambda b, j: (b, 0, 0)),
            pl.BlockSpec((seq, LANES), lambda b, j: (b, 0)),
            pl.BlockSpec((seq, F_PAD), lambda b, j: (b, 0)),
            pl.BlockSpec((1, F_PAD), lambda b, j: (0, 0)),
            pl.BlockSpec(memory_space=pl.ANY),
        ],
        out_specs=(pl.BlockSpec((seq, LANES), lambda b, j: (b, f_off // LANES + j)),
                   pl.BlockSpec((1, LANES), lambda b, j: (0, 0))),
        input_output_aliases={4: 0},
        compiler_params=pltpu.CompilerParams(dimension_semantics=("arbitrary", "arbitrary")),
        name=name,
    )(d_key, d_query, fl, bf, dproj)


def _dot(a, b, mode):
    return lax.dot_general(a, b, _DIMS[mode], preferred_element_type=F32)


def _attn_fwd(name, qkv, frow, batch, seq, tq):
    nq = seq // tq
    scale = 1.0 / math.sqrt(HEAD_DIM)

    def body(q_ref, k_ref, v_ref, f_ref, o_ref, lse_ref):
        i = pl.program_id(2)
        lane = lax.broadcasted_iota(jnp.int32, (1, LANES), 1)
        lo = lane < HEAD_DIM
        qs = q_ref[...] * scale
        qh = (jnp.where(lo, qs, 0.0).astype(BF16), jnp.where(lo, 0.0, qs).astype(BF16))
        row = lax.broadcasted_iota(jnp.int32, (tq, tq), 0)
        col = lax.broadcasted_iota(jnp.int32, (tq, tq), 1)

        def step(j, carry, diag):
            m0, l0, m1, l1, acc = carry
            start = pl.multiple_of(j * tq, tq)
            kj = k_ref[pl.ds(start, tq), :]
            vj = v_ref[pl.ds(start, tq), :]
            ms, ls, pvs, alphas = [], [], [], []
            for h, (m_old, l_old) in enumerate(((m0, l0), (m1, l1))):
                s = _dot(qh[h], kj, "nt") - f_ref[h:h + 1, pl.ds(start, tq)]
                if diag:
                    s = jnp.where(col <= row, s, NEG_BIG)
                m_new = jnp.maximum(m_old, jnp.max(s, axis=1, keepdims=True))
                p = jnp.exp(s - m_new)
                alpha = jnp.exp(m_old - m_new)
                ls.append(alpha * l_old + jnp.sum(p, axis=1, keepdims=True))
                ms.append(m_new)
                alphas.append(alpha)
                vh = jnp.where(lo, vj, 0.0) if h == 0 else jnp.where(lo, 0.0, vj)
                pvs.append(_dot(p.astype(BF16), vh.astype(BF16), "nn"))
            acc = acc * jnp.where(lo, alphas[0], alphas[1]) + (pvs[0] + pvs[1])
            return ms[0], ls[0], ms[1], ls[1], acc

        neg = jnp.full((tq, 1), NEG_BIG, F32)
        zero = jnp.zeros((tq, 1), F32)
        init = (neg, zero, neg, zero, jnp.zeros((tq, LANES), F32))
        carry = lax.fori_loop(0, i, lambda j, c: step(j, c, False), init)
        m0, l0, m1, l1, acc = step(i, carry, True)
        o_ref[...] = (acc / jnp.where(lo, l0, l1)).astype(o_ref.dtype)
        lse_ref[:, 0:1] = m0 + jnp.log(l0)
        lse_ref[:, 1:2] = m1 + jnp.log(l1)

    return pl.pallas_call(
        body,
        out_shape=(_sds((batch * seq, ATTN_WIDTH), BF16), _sds((HEAD_PAIRS, batch * seq, 2), F32)),
        grid=(batch, HEAD_PAIRS, nq),
        in_specs=[
            pl.BlockSpec((tq, LANES), lambda b, hp, i: (b * nq + i, 3 * hp)),
            pl.BlockSpec((seq, LANES), lambda b, hp, i: (b, 3 * hp + 1)),
            pl.BlockSpec((seq, LANES), lambda b, hp, i: (b, 3 * hp + 2)),
            pl.BlockSpec((None, None, 2, seq), lambda b, hp, i: (b, hp, 0, 0)),
        ],
        out_specs=(
            pl.BlockSpec((tq, LANES), lambda b, hp, i: (b * nq + i, hp)),
            pl.BlockSpec((None, tq, 2), lambda b, hp, i: (hp, b * nq + i, 0)),
        ),
        compiler_params=pltpu.CompilerParams(dimension_semantics=("parallel", "parallel", "parallel")),
        name=name,
    )(qkv, qkv, qkv, frow)


def _attn_bwd(name, qkv, do, o, lse, frow, dproj, qkv_off, batch, seq, tq):
    nq = seq // tq
    scale = 1.0 / math.sqrt(HEAD_DIM)

    def body(q_ref, k_ref, v_ref, do_ref, o_ref, lse_ref, f_ref, _, dqkv_ref, df_ref, drow_ref,
             dq_acc, dk_acc, dv_acc, df_acc):
        j = pl.program_id(2)
        lane = lax.broadcasted_iota(jnp.int32, (1, LANES), 1)
        lo = lane < HEAD_DIM
        masks = (lo, jnp.logical_not(lo))
        row = lax.broadcasted_iota(jnp.int32, (tq, tq), 0)
        col = lax.broadcasted_iota(jnp.int32, (tq, tq), 1)

        @pl.when(j == 0)
        def _():
            dq_acc[...] = jnp.zeros_like(dq_acc)
            drow_ref[...] = jnp.zeros_like(drow_ref)

        dk_acc[...] = jnp.zeros_like(dk_acc)
        dv_acc[...] = jnp.zeros_like(dv_acc)
        df_acc[...] = jnp.zeros_like(df_acc)
        kj = k_ref[...]
        vj = v_ref[...]
        kstart = pl.multiple_of(j * tq, tq)
        kh = tuple(jnp.where(mk, kj, 0.0).astype(BF16) for mk in masks)

        def step(i, diag):
            start = pl.multiple_of(i * tq, tq)
            rows = pl.ds(start, tq)
            qi = q_ref[rows, :] * scale
            doi = do_ref[rows, :]
            prod = doi.astype(F32) * o_ref[rows, :].astype(F32)
            lse_i = lse_ref[rows, :]
            dq_i = jnp.zeros((tq, LANES), F32)
            for h, mk in enumerate(masks):
                q_h = jnp.where(mk, qi, 0.0).astype(BF16)
                do_h = jnp.where(mk, doi, 0.0).astype(BF16)
                delta = jnp.sum(jnp.where(mk, prod, 0.0), axis=1, keepdims=True)
                s = _dot(q_h, kj, "nt") - f_ref[h:h + 1, pl.ds(kstart, tq)]
                p = jnp.exp(s - lse_i[:, h:h + 1])
                if diag:
                    p = jnp.where(col <= row, p, 0.0)
                ds = p * (_dot(do_h, vj, "nt") - delta)
                df_acc[h:h + 1, :] += jnp.sum(ds, axis=0, keepdims=True)
                drow_ref[rows, h:h + 1] += jnp.sum(ds, axis=1, keepdims=True)
                dsb = ds.astype(BF16)
                dv_acc[...] += _dot(p.astype(BF16), do_h, "tn")
                dk_acc[...] += _dot(dsb, q_h, "tn")
                dq_i = dq_i + _dot(dsb, kh[h], "nn")
            dq_acc[rows, :] += dq_i

        step(j, True)
        lax.fori_loop(j + 1, nq, lambda i, c: (step(i, False), c)[1], 0)
        dqkv_ref[:, 0:LANES] = (dq_acc[pl.ds(kstart, tq), :] * scale).astype(dqkv_ref.dtype)
        dqkv_ref[:, LANES:2 * LANES] = dk_acc[...].astype(dqkv_ref.dtype)
        dqkv_ref[:, 2 * LANES:3 * LANES] = dv_acc[...].astype(dqkv_ref.dtype)
        df_ref[...] = df_acc[...]

    full = lambda c: pl.BlockSpec((seq, LANES), lambda b, hp, j: (b, c(hp)))
    blk = lambda c: pl.BlockSpec((tq, LANES), lambda b, hp, j: (b * nq + j, c(hp)))
    return pl.pallas_call(
        body,
        out_shape=(_sds(dproj.shape, dproj.dtype), _sds((batch, HEAD_PAIRS, 2, seq), F32),
                   _sds((HEAD_PAIRS, batch * seq, 2), F32)),
        grid=(batch, HEAD_PAIRS, nq),
        in_specs=[
            full(lambda hp: 3 * hp),
            blk(lambda hp: 3 * hp + 1),
            blk(lambda hp: 3 * hp + 2),
            full(lambda hp: hp),
            full(lambda hp: hp),
            pl.BlockSpec((None, seq, 2), lambda b, hp, j: (hp, b, 0)),
            pl.BlockSpec((None, None, 2, seq), lambda b, hp, j: (b, hp, 0, 0)),
            pl.BlockSpec(memory_space=pl.ANY),
        ],
        out_specs=(
            pl.BlockSpec((tq, 3 * LANES), lambda b, hp, j: (b * nq + j, qkv_off // (3 * LANES) + hp)),
            pl.BlockSpec((None, None, 2, tq), lambda b, hp, j: (b, hp, 0, j)),
            pl.BlockSpec((None, seq, 2), lambda b, hp, j: (hp, b, 0)),
        ),
        scratch_shapes=[
            pltpu.VMEM((seq, LANES), F32),
            pltpu.VMEM((tq, LANES), F32),
            pltpu.VMEM((tq, LANES), F32),
            pltpu.VMEM((2, tq), F32),
        ],
        input_output_aliases={7: 0},
        compiler_params=pltpu.CompilerParams(dimension_semantics=("parallel", "parallel", "arbitrary")),
        name=name,
    )(qkv, qkv, qkv, do, o, lse, frow, dproj)


def _mesh_place():
    x, y, c = lax.axis_index("x"), lax.axis_index("y"), lax.axis_index("c")
    chips = [(1 - x, y), (x, 1 - y), (1 - x, 1 - y)]
    return x, y, c, chips


def _hbm_specs(n):
    return [pl.BlockSpec(memory_space=pl.ANY)] * n


def _half(shape2d, axis, which):
    size = shape2d[axis] // 2
    sl = pl.ds(pl.multiple_of(which * size, 16 if axis == 0 else LANES), size)
    return (sl, slice(None)) if axis == 0 else (slice(None), sl)


def _gather_weights(bigs, axes, smalls):
    nb, ns = len(bigs), len(smalls)
    arrays = list(bigs) + list(smalls)
    n = nb + ns

    def body(*refs):
        ins, outs = refs[:n], refs[n:2 * n]
        send_sems, recv_sems = refs[2 * n:]
        x, y, c, chips = _mesh_place()
        me = 2 * x + y
        sibling = (x, y, 1 - c)

        def half(a, which):
            return _half(arrays[a].shape, axes[a], which)

        def copy(a, k, src, dst, to):
            return pltpu.make_async_remote_copy(src_ref=src, dst_ref=dst, send_sem=send_sems.at[a, k],
                                                recv_sem=recv_sems.at[a, k], device_id=to, device_id_type=MESH)

        sends = []
        for a in range(n):
            for j, chip in enumerate(chips):
                if a < nb:
                    cp = copy(a, j, ins[a].at[half(a, c)], outs[a].at[(me,) + half(a, c)], (*chip, c))
                else:
                    cp = copy(a, j, ins[a], outs[a].at[me], (*chip, c))
                cp.start()
                sends.append(cp)
        for a in range(nb):
            for j, (px, py) in enumerate(chips):
                blk = outs[a].at[(2 * px + py,) + half(a, c)]
                copy(a, j, blk, blk, (px, py, c)).wait_recv()
                fwd = copy(a, 3 + j, blk, blk, sibling)
                fwd.start()
                sends.append(fwd)
        for a in range(nb, n):
            for j, (px, py) in enumerate(chips):
                blk = outs[a].at[2 * px + py]
                copy(a, j, blk, blk, (px, py, c)).wait_recv()
        for a in range(nb):
            for j, (px, py) in enumerate(chips):
                blk = outs[a].at[(2 * px + py,) + half(a, 1 - c)]
                copy(a, 3 + j, blk, blk, sibling).wait_recv()
        for cp in sends:
            cp.wait_send()

    outs = pl.pallas_call(
        body,
        out_shape=tuple(_sds((N_CHIPS,) + a.shape, a.dtype) for a in arrays),
        in_specs=_hbm_specs(n),
        out_specs=tuple(_hbm_specs(n)),
        scratch_shapes=[pltpu.SemaphoreType.DMA((n, 6)), pltpu.SemaphoreType.DMA((n, 6))],
        name="gather_weights",
    )(*arrays)
    me = 2 * lax.axis_index("x") + lax.axis_index("y")
    return tuple(lax.dynamic_update_index_in_dim(o, a, me, 0) for o, a in zip(outs, arrays))


def _gather_small(v):
    m_per, ncol = v.shape

    def body(x_ref, out_ref, send_sems, recv_sems, local_sem):
        x, y, c, chips = _mesh_place()
        me, sibling = (x, y, c), (x, y, 1 - c)

        def rows(px, py, pc):
            return out_ref.at[pl.ds((4 * px + 2 * py + pc) * m_per, m_per), :]

        def copy(k, block, to, src=None):
            return pltpu.make_async_remote_copy(src_ref=rows(*block) if src is None else src, dst_ref=rows(*block),
                                                send_sem=send_sems.at[k], recv_sem=recv_sems.at[k],
                                                device_id=to, device_id_type=MESH)

        mine = pltpu.make_async_copy(x_ref, rows(*me), local_sem)
        mine.start()
        first = [copy(0, me, sibling, src=x_ref)]
        first += [copy(1 + j, me, (*chip, c), src=x_ref) for j, chip in enumerate(chips)]
        for cp in first:
            cp.start()
        passed = [copy(4 + j, (*chip, c), sibling) for j, chip in enumerate(chips)]
        for j, chip in enumerate(chips):
            copy(1 + j, (*chip, c), me).wait_recv()
            passed[j].start()
        copy(0, sibling, me).wait_recv()
        for j, chip in enumerate(chips):
            copy(4 + j, (*chip, 1 - c), me).wait_recv()
        for cp in first + passed:
            cp.wait_send()
        mine.wait()

    return pl.pallas_call(
        body,
        out_shape=_sds((N_DEV * m_per, ncol), v.dtype),
        in_specs=[pl.BlockSpec(memory_space=pltpu.VMEM)],
        out_specs=pl.BlockSpec(memory_space=pltpu.VMEM),
        scratch_shapes=[pltpu.SemaphoreType.DMA((7,)), pltpu.SemaphoreType.DMA((7,)), pltpu.SemaphoreType.DMA],
        name="gather_small",
    )(v)


def _half_shape(shape2d, axis):
    return (shape2d[0] // 2, shape2d[1]) if axis == 0 else (shape2d[0], shape2d[1] // 2)


def _exchange_sibling(name, grads, axes):
    n = len(grads)

    def body(*refs):
        ins, outs = refs[:n], refs[n:2 * n]
        send_sems, recv_sems = refs[2 * n:]
        x, y, c, _ = _mesh_place()
        copies = []
        for a in range(n):
            src = ins[a].at[(slice(None),) + _half(grads[a].shape[1:], axes[a], 1 - c)]
            cp = pltpu.make_async_remote_copy(src_ref=src, dst_ref=outs[a], send_sem=send_sems.at[a],
                                              recv_sem=recv_sems.at[a], device_id=(x, y, 1 - c), device_id_type=MESH)
            cp.start()
            copies.append(cp)
        for cp in copies:
            cp.wait()

    return pl.pallas_call(
        body,
        out_shape=tuple(_sds((N_CHIPS,) + _half_shape(g.shape[1:], ax), g.dtype) for g, ax in zip(grads, axes)),
        in_specs=_hbm_specs(n),
        out_specs=tuple(_hbm_specs(n)),
        scratch_shapes=[pltpu.SemaphoreType.DMA((n,)), pltpu.SemaphoreType.DMA((n,))],
        name=name,
    )(*grads)


_HBM = pl.BlockSpec(memory_space=pltpu.HBM)
_SEM = pl.BlockSpec(memory_space=pltpu.SEMAPHORE)
_EFFECT = pltpu.SideEffectType.DATAFLOW_SIDE_EFFECTING


def _chip_copies(kind, srcs, lands, send_sems, recv_sems):
    x, y, c, chips = _mesh_place()
    copies = []
    for a in range(len(srcs)):
        for j, (px, py) in enumerate(chips):
            if kind == "gather":
                src, dst = srcs[a], lands[a].at[2 * x + y]
            else:
                src, dst = srcs[a].at[j], lands[a].at[j]
            copies.append(pltpu.make_async_remote_copy(src_ref=src, dst_ref=dst, send_sem=send_sems.at[3 * a + j],
                                                       recv_sem=recv_sems.at[3 * a + j], device_id=(px, py, c),
                                                       device_id_type=MESH))
    return copies


def _chips_start(name, kind, srcs):
    n = len(srcs)
    slots = N_CHIPS if kind == "gather" else 3
    lands = [lax.empty((slots,) + (s.shape if kind == "gather" else s.shape[1:]), s.dtype) for s in srcs]

    def body(*refs):
        for cp in _chip_copies(kind, refs[:n], refs[n:2 * n], refs[2 * n], refs[2 * n + 1]):
            cp.start()
        refs[-1][...] = jnp.zeros_like(refs[-1])

    outs = pl.pallas_call(
        body,
        out_shape=(pltpu.SemaphoreType.DMA((3 * n,)), pltpu.SemaphoreType.DMA((3 * n,)),
                   *[pltpu.HBM(v.shape, v.dtype) for v in (*srcs, *lands)], _sds((8, LANES), F32)),
        in_specs=[_HBM] * (2 * n),
        out_specs=(_SEM, _SEM, *[_HBM] * (2 * n), pl.BlockSpec(memory_space=pltpu.VMEM)),
        input_output_aliases={i: 2 + i for i in range(2 * n)},
        compiler_params=pltpu.CompilerParams(has_side_effects=_EFFECT),
        name=name,
    )(*[pltpu.with_memory_space_constraint(v, pltpu.HBM) for v in (*srcs, *lands)])
    return outs[:-1], outs[-1]


def _chips_wait(name, kind, handles, after):
    send_sems, recv_sems, *thru = handles
    n = len(thru) // 2

    def body(*refs):
        for cp in _chip_copies(kind, refs[:n], refs[n:2 * n], refs[2 * n], refs[2 * n + 1]):
            cp.wait_send()
            cp.wait_recv()

    outs = pl.pallas_call(
        body,
        out_shape=tuple(pltpu.HBM(v.shape, v.dtype) for v in thru),
        in_specs=[_HBM] * (2 * n) + [_SEM, _SEM, pl.BlockSpec(memory_space=pl.ANY)],
        out_specs=tuple([_HBM] * (2 * n)),
        input_output_aliases={i: i for i in range(2 * n)},
        compiler_params=pltpu.CompilerParams(has_side_effects=_EFFECT),
        name=name,
    )(*thru, send_sems, recv_sems, after)
    return outs[n:]


def _share_sibling(name, shards, axes):
    n = len(shards)

    def body(*refs):
        ins, outs = refs[:n], refs[n:2 * n]
        send_sems, recv_sems = refs[2 * n:]
        x, y, c, _ = _mesh_place()
        started = []
        for a in range(n):
            mine = _half(shards[a].shape, axes[a], c)
            theirs = _half(shards[a].shape, axes[a], 1 - c)
            cp = pltpu.make_async_remote_copy(src_ref=ins[a].at[mine], dst_ref=outs[a].at[mine],
                                              send_sem=send_sems.at[a], recv_sem=recv_sems.at[a],
                                              device_id=(x, y, 1 - c), device_id_type=MESH)
            cp.start()
            arrival = pltpu.make_async_remote_copy(src_ref=ins[a].at[theirs], dst_ref=outs[a].at[theirs],
                                                   send_sem=send_sems.at[a], recv_sem=recv_sems.at[a],
                                                   device_id=(x, y, 1 - c), device_id_type=MESH)
            started.append((cp, arrival))
        for cp, arrival in started:
            arrival.wait_recv()
            cp.wait_send()

    return pl.pallas_call(
        body,
        out_shape=tuple(_sds(s.shape, s.dtype) for s in shards),
        in_specs=_hbm_specs(n),
        out_specs=tuple(_hbm_specs(n)),
        scratch_shapes=[pltpu.SemaphoreType.DMA((n,)), pltpu.SemaphoreType.DMA((n,))],
        input_output_aliases={a: a for a in range(n)},
        name=name,
    )(*shards)


def _pair_sum(name, place, g, got, axis):
    hr, hc = got.shape[1:]

    def body(place_ref, g_ref, got_ref, o_ref):
        o_ref[...] = (g_ref[...] + got_ref[...]).astype(o_ref.dtype)

    blk = (None, hr, hc)
    mine = (lambda j, pr: (pr[2 + j], pr[1], 0)) if axis == 0 else (lambda j, pr: (pr[2 + j], 0, pr[1]))
    return pl.pallas_call(
        body,
        out_shape=_sds((N_CHIPS - 1, hr, hc), BF16),
        grid_spec=pltpu.PrefetchScalarGridSpec(
            num_scalar_prefetch=1,
            grid=(N_CHIPS - 1,),
            in_specs=[pl.BlockSpec(blk, mine), pl.BlockSpec(blk, lambda j, pr: (pr[2 + j], 0, 0))],
            out_specs=pl.BlockSpec(blk, lambda j, pr: (j, 0, 0)),
        ),
        compiler_params=pltpu.CompilerParams(dimension_semantics=("parallel",)),
        name=name,
    )(place, g, got)


def _chip_sum(name, place, g, got, arrivals, axis):
    _, r, cdim = g.shape
    hr, hc = got.shape[1:]

    def body(place_ref, g_ref, got_ref, arr_ref, o_ref):
        acc = g_ref[...] + got_ref[...]
        for j in range(3):
            acc = acc + arr_ref[j].astype(F32)
        o_ref[...] = acc

    blk = (None, hr, hc)
    mine = (lambda i, pr: (pr[0], pr[1], 0)) if axis == 0 else (lambda i, pr: (pr[0], 0, pr[1]))
    dest = (lambda i, pr: (pr[1], 0)) if axis == 0 else (lambda i, pr: (0, pr[1]))
    return pl.pallas_call(
        body,
        out_shape=_sds((r, cdim), F32),
        grid_spec=pltpu.PrefetchScalarGridSpec(
            num_scalar_prefetch=1,
            grid=(1,),
            in_specs=[
                pl.BlockSpec(blk, mine),
                pl.BlockSpec(blk, lambda i, pr: (pr[0], 0, 0)),
                pl.BlockSpec((3, hr, hc), lambda i, pr: (0, 0, 0)),
            ],
            out_specs=pl.BlockSpec((hr, hc), dest),
        ),
        compiler_params=pltpu.CompilerParams(dimension_semantics=("arbitrary",)),
        name=name,
    )(place, g, got, arrivals)


def _device_sum(name, gathered):
    m_per = gathered.shape[0] // N_DEV

    def body(g_ref, o_ref):
        acc = g_ref[0:m_per, :]
        for dev in range(1, N_DEV):
            acc = acc + g_ref[dev * m_per:(dev + 1) * m_per, :]
        o_ref[...] = acc

    return pl.pallas_call(body, out_shape=_sds((m_per, gathered.shape[1]), F32), name=name)(gathered)


def _adamw(name, w, g, m, v):
    r, cdim = w.shape
    if r % 8 == 0:
        tr, tcol = _tile(r, 256, 8), cdim
    else:
        tr, tcol = r, (_tile(cdim, 256, LANES) if cdim % LANES == 0 else cdim)
    blk = pl.BlockSpec((tr, tcol), lambda i, j: (i, j))
    grid = (r // tr, cdim // tcol)
    bc1 = 1.0 - ADAM_B1 ** ADAM_STEP
    bc2 = 1.0 - ADAM_B2 ** ADAM_STEP

    def body(w_ref, g_ref, m_ref, v_ref, d_ref, nm_ref, nv_ref):
        gv = g_ref[...]
        nm = ADAM_B1 * m_ref[...] + (1.0 - ADAM_B1) * gv
        nv = ADAM_B2 * v_ref[...] + (1.0 - ADAM_B2) * (gv * gv)
        d_ref[...] = -ADAM_LR * ((nm / bc1) / (jnp.sqrt(nv / bc2) + ADAM_EPS) + ADAM_WD * w_ref[...])
        nm_ref[...] = nm
        nv_ref[...] = nv

    shape = _sds(w.shape, F32)
    return pl.pallas_call(
        body,
        out_shape=(shape, shape, shape),
        grid=grid,
        in_specs=[blk] * 4,
        out_specs=(blk, blk, blk),
        compiler_params=pltpu.CompilerParams(dimension_semantics=("parallel", "parallel")),
        name=name,
    )(w, g, m, v)


def _cat_cols(g):
    return jnp.transpose(g, (1, 0, 2)).reshape(g.shape[1], N_CHIPS * g.shape[2])


def _split_cols(a):
    r, c4 = a.shape
    return jnp.transpose(a.reshape(r, N_CHIPS, c4 // N_CHIPS), (1, 0, 2))


def _local_step(x, target, w_int, late_weights, cmw, cfw, g1, b_f, b_gate, g2, gf,
                ffn_grads_ready, mix_grads_ready):
    batch, seq, d = x.shape
    t = batch * seq
    cw = d // 2
    fh = cfw.shape[1] // 2
    tc = LANES
    nct = cw // tc
    tq = min(512, seq)
    pc_w, qkv_w, gl_w = 3 * cw, 3 * ATTN_WIDTH, 2 * d
    qkv_off, gl_off, f_off = pc_w, pc_w + qkv_w, pc_w + qkv_w + gl_w
    width = f_off + F_PAD
    f_col = pc_w + qkv_w

    w_pc = w_int[:pc_w].reshape(3, nct, tc, d).transpose(1, 0, 2, 3).reshape(pc_w, d)
    w_qkv = w_int[pc_w:f_col].reshape(3, HEAD_PAIRS, LANES, d).transpose(1, 0, 2, 3).reshape(qkv_w, d)
    w_f = jnp.pad(w_int[f_col:f_col + HEADS], ((0, F_PAD - HEADS), (0, 0)))
    w_inp = jnp.concatenate([w_pc, w_qkv, w_int[f_col + HEADS:], w_f], axis=0)
    bf_pad = jnp.pad(b_f, ((0, 0), (0, F_PAD - HEADS)))

    x2d = x.reshape(t, d)
    tgt2d = target.reshape(t, d)

    h1 = _rms_fwd("norm_mix", x2d, g1)
    pc = _mm("proj_conv", h1, w_inp, "nt", BF16, m=t, n=pc_w, k=d, b_roff=0)
    qkv = _mm("proj_qkv", h1, w_inp, "nt", BF16, m=t, n=qkv_w, k=d, b_roff=qkv_off)
    gl = _mm("proj_gate", h1, w_inp, "nt", BF16, m=t, n=gl_w, k=d, b_roff=gl_off)
    fl = _mm("proj_forget", h1, w_inp, "nt", F32, m=t, n=F_PAD, k=d, b_roff=f_off)
    a_c = _conv_fwd("conv_mix", pc, cmw, batch, seq, tc)
    f_cum = _forget_fwd("forget_cumsum", fl, bf_pad, batch, seq)
    frow = f_cum.reshape(batch, HEAD_PAIRS, 2, seq)
    o, lse = _attn_fwd("attn_fwd", qkv, frow, batch, seq, tq)
    w_oc, w_oa, w_o, w_up, w_down = late_weights(o)
    ycat = _mm("out_conv", a_c, w_oc, "nn", BF16, m=t, n=d, k=cw, o_off=0, o_width=2 * d)
    ycat = _mm("out_attn", o, w_oa, "nn", BF16, m=t, n=d, k=ATTN_WIDTH, out=ycat, o_off=d)
    mg = _merge_fwd("gate_merge", ycat, gl, b_gate)
    x2 = _mm("mix_out", mg, w_o, "nn", F32, m=t, n=d, k=d, add=x2d)
    h2 = _rms_fwd("norm_ffn", x2, g2)
    tcf = min(2 * LANES, fh)
    hmid, ua, ub = _ffn_up_act("ffn_up_act", h2, _cat_cols(w_up), cfw, batch, seq, tcf)
    x3 = _mm("ffn_down", hmid, w_down, "nn", F32, m=t, n=d, k=fh, add=x2, tk=4096)

    dx3, dx3b, loss_row, d_gf = _final_loss("final_loss", x3, gf.reshape(1, d), tgt2d)
    dw_down = _mm("dw_down", hmid, dx3b, "tn", F32, m=fh, n=d, k=t, tm=1408, tk=2048)
    du_a, du_b, d_cfw = _ffn_bwd("d_ffn", dx3b, w_down, ua, ub, cfw, batch, seq, tcf)
    ws = w_up.shape[2]
    dh2 = _mm("d_norm_ffn_a", du_a, w_up, "nt", BF16, m=t, n=d, k=fh, b_off=0, b3=True)
    dh2 = _mm("d_norm_ffn_b", du_b, w_up, "nt", BF16, m=t, n=d, k=fh, b_off=fh, b3=True, add=dh2)
    dw_up = _mm("dw_up_a", h2, du_a, "tn", F32, m=d, n=fh, k=t, tn=ws, tk=2048, o3=N_CHIPS)
    dw_up = _mm("dw_up_b", h2, du_b, "tn", F32, m=d, n=fh, k=t, tn=ws, tk=2048, o3=N_CHIPS, out=dw_up, o_off=fh)
    token = ffn_grads_ready(dw_up, dw_down)
    if token is not None:
        g2 = g2 + token[0:1, 0:1]
    dx2, d_g2 = _rms_bwd("d_norm_ffn", x2, dh2, g2, dx3)
    dm = _mm("d_merge", dx2, w_o, "nt", BF16, m=t, n=d, k=d)
    dw_o = _mm("dw_o", mg, dx2, "tn", F32, m=d, n=d, k=t, tk=2048)
    dproj, dycat, d_bg = _merge_bwd("d_gate_merge", dm, ycat, gl, b_gate, width, gl_off)
    da_c = _mm("d_conv_out", dycat, w_oc, "nt", BF16, m=t, n=cw, k=d, a_off=0)
    do = _mm("d_attn_out", dycat, w_oa, "nt", BF16, m=t, n=ATTN_WIDTH, k=d, a_off=d)
    dw_oc = _mm("dw_out_conv", a_c, dycat, "tn", F32, m=cw, n=d, k=t, b_off=0, tk=2048)
    dw_oa = _mm("dw_out_attn", o, dycat, "tn", F32, m=ATTN_WIDTH, n=d, k=t, b_off=d, tk=2048)
    dproj, d_cmw = _conv_bwd("d_conv_mix", da_c, pc, cmw, dproj, batch, seq, tc)
    dproj, d_fkey, d_fquery = _attn_bwd("attn_bwd", qkv, do, o, lse, frow, dproj, qkv_off, batch, seq, tq)
    d_fquery = jnp.pad(jnp.transpose(d_fquery, (1, 0, 2)).reshape(t, HEADS), ((0, 0), (0, LANES - HEADS)))
    dproj, d_bf = _forget_bwd("d_forget", d_fkey.reshape(batch, HEADS, seq), d_fquery, fl, bf_pad, dproj, f_off,
                              batch, seq)
    dw_inp = _mm("dw_in", dproj, h1, "tn", F32, m=width, n=d, k=t, tm=1792, tk=2048)
    d_pc = dw_inp[:pc_w].reshape(nct, 3, tc, d).transpose(1, 0, 2, 3).reshape(pc_w, d)
    d_qkv = dw_inp[qkv_off:gl_off].reshape(HEAD_PAIRS, 3, LANES, d).transpose(1, 0, 2, 3).reshape(qkv_w, d)
    dw_int = jnp.concatenate([d_pc, d_qkv, dw_inp[f_off:f_off + HEADS], dw_inp[gl_off:f_off]], axis=0)
    token = mix_grads_ready(dw_int, dw_oc, dw_oa, dw_o)
    dh1 = _mm("d_norm_mix", dproj, w_inp, "nn", BF16, m=t, n=d, k=width, tk=1792, dep=token)
    grad_x, d_g1 = _rms_bwd("d_norm_mix_x", x2d, dh1, g1, dx2)
    smalls = (d_g1, d_g2, d_gf, d_bg, d_bf, d_cmw, d_cfw)
    return loss_row[0, 0], grad_x.reshape(batch, seq, d), smalls


def _pack_small(parts):
    flat = [p.reshape(-1) for p in parts]
    sizes = [f.shape[0] for f in flat]
    total = sum(sizes)
    padded = -(-total // (8 * LANES)) * (8 * LANES)
    vec = jnp.concatenate(flat + [jnp.zeros((padded - total,), F32)])
    offsets = [sum(sizes[:i]) for i in range(len(sizes))]
    return vec.reshape(padded // LANES, LANES), offsets


def kernel(x, norm_mix_g, w_in, b_f, b_gate, conv_mix_w, w_out_conv, w_out_attn, w_o, norm_ffn_g, w_up, conv_ffn_w, w_down, norm_f_g, loss_target, m_norm_mix_g, m_w_in, m_b_f, m_b_gate, m_conv_mix_w, m_w_out_conv, m_w_out_attn, m_w_o, m_norm_ffn_g, m_w_up, m_conv_ffn_w, m_w_down, m_norm_f_g, v_norm_mix_g, v_w_in, v_b_f, v_b_gate, v_conv_mix_w, v_w_out_conv, v_w_out_attn, v_w_o, v_norm_ffn_g, v_w_up, v_conv_ffn_w, v_w_down, v_norm_f_g):
    d = x.shape[-1]
    chip = 2 * lax.axis_index("x") + lax.axis_index("y")
    xi, yi = lax.axis_index("x"), lax.axis_index("y")
    peers = [2 * px + py for px, py in ((1 - xi, yi), (xi, 1 - yi), (1 - xi, 1 - yi))]
    place = jnp.stack([chip, lax.axis_index("c"), *peers]).astype(jnp.int32)

    t_in, t_m_in, t_v_in = (jnp.transpose(w[0]) for w in (w_in, m_w_in, v_w_in))

    def row_shards(a):
        return a.reshape(N_CHIPS, a.shape[0] // N_CHIPS, a.shape[1])

    def stacked(a):
        return a.reshape(N_CHIPS * a.shape[1], a.shape[2])

    a_in, a_cmw, a_cfw = _gather_weights([t_in.astype(BF16)], (1,), [conv_mix_w[0], conv_ffn_w[0]])
    late = [w[0].astype(BF16) for w in (w_out_conv, w_out_attn, w_o, w_up, w_down)]
    late_handles, late_token = _chips_start("gather_late_start", "gather", late)

    def late_weights(after):
        lands = _chips_wait("gather_late_wait", "gather", late_handles, after)
        a_oc, a_oa, a_o, a_up, a_down = (
            lax.dynamic_update_index_in_dim(buf, own, chip, 0) for buf, own in zip(lands, late))
        return _cat_cols(a_oc), _cat_cols(a_oa), stacked(a_o), a_up, stacked(a_down)

    pending = []

    def reduce_start(tag, names, grads, axes):
        got = _exchange_sibling("exchange_sibling_" + tag, grads, axes)
        sums = [_pair_sum("pair_sum_" + nm, place, g, r, ax) for nm, g, r, ax in zip(names, grads, got, axes)]
        handles, token = _chips_start("exchange_chips_start_" + tag, "reduce", sums)
        pending.append((tag, names, grads, axes, got, handles))
        return token

    def ffn_grads_ready(dw_up, dw_down):
        return reduce_start("ffn", ("w_up", "w_down"), [dw_up, row_shards(dw_down)], (0, 0))

    def mix_grads_ready(dw_int, dw_oc, dw_oa, dw_o):
        return reduce_start("mix", ("w_in", "w_out_conv", "w_out_attn", "w_o"),
                            [row_shards(dw_int), _split_cols(dw_oc), _split_cols(dw_oa), row_shards(dw_o)],
                            (1, 0, 0, 0))

    loss_local, grad_x, smalls = _local_step(
        x, loss_target, stacked(a_in), late_weights, _cat_cols(a_cmw),
        _cat_cols(a_cfw), norm_mix_g + late_token[0:1, 0:1], b_f, b_gate, norm_ffn_g, norm_f_g,
        ffn_grads_ready, mix_grads_ready)

    reduced = {}
    for tag, names, grads, axes, got, handles in pending:
        arrivals = _chips_wait("exchange_chips_wait_" + tag, "reduce", handles, grad_x)
        halves = [_chip_sum("chip_sum_" + nm, place, g, r, arr, ax)
                  for nm, g, r, arr, ax in zip(names, grads, got, arrivals, axes)]
        reduced.update(zip(names, _share_sibling("share_sibling_" + tag, halves, axes)))
    g_in, g_oc, g_oa, g_o, g_up, g_down = (
        reduced[nm] for nm in ("w_in", "w_out_conv", "w_out_attn", "w_o", "w_up", "w_down"))

    smalls = (*smalls, loss_local.reshape(1, 1))
    packed, offs = _pack_small(smalls)
    total = _device_sum("device_sum", _gather_small(packed)).reshape(-1)
    shapes = [s.shape for s in smalls]
    d_g1, d_g2, d_gf, d_bg, d_bf, d_cmw, d_cfw, loss = [
        total[o:o + math.prod(sh)].reshape(sh) for o, sh in zip(offs, shapes)]
    loss = loss[0, 0]
    d_bf = d_bf[:, :HEADS]
    cw_s, cf_s = conv_mix_w.shape[2], conv_ffn_w.shape[2]
    d_cmw = lax.dynamic_slice(d_cmw, (0, chip * cw_s), (3, cw_s))
    d_cfw = lax.dynamic_slice(d_cfw, (0, chip * cf_s), (3, cf_s))

    order = [
        ("norm_mix_g", norm_mix_g[0:1], d_g1, m_norm_mix_g, v_norm_mix_g),
        ("w_in", t_in, g_in, t_m_in, t_v_in),
        ("b_f", b_f, d_bf, m_b_f, v_b_f),
        ("b_gate", b_gate, d_bg, m_b_gate, v_b_gate),
        ("conv_mix_w", conv_mix_w[0], d_cmw, m_conv_mix_w[0], v_conv_mix_w[0]),
        ("w_out_conv", w_out_conv[0], g_oc, m_w_out_conv[0], v_w_out_conv[0]),
        ("w_out_attn", w_out_attn[0], g_oa, m_w_out_attn[0], v_w_out_attn[0]),
        ("w_o", w_o[0], g_o, m_w_o[0], v_w_o[0]),
        ("norm_ffn_g", norm_ffn_g, d_g2, m_norm_ffn_g, v_norm_ffn_g),
        ("w_up", w_up[0], g_up, m_w_up[0], v_w_up[0]),
        ("conv_ffn_w", conv_ffn_w[0], d_cfw, m_conv_ffn_w[0], v_conv_ffn_w[0]),
        ("w_down", w_down[0], g_down, m_w_down[0], v_w_down[0]),
        ("norm_f_g", norm_f_g.reshape(1, d), d_gf, m_norm_f_g.reshape(1, d), v_norm_f_g.reshape(1, d)),
    ]
    out_shapes = [norm_mix_g.shape, w_in.shape, b_f.shape, b_gate.shape, conv_mix_w.shape, w_out_conv.shape,
                  w_out_attn.shape, w_o.shape, norm_ffn_g.shape, w_up.shape, conv_ffn_w.shape, w_down.shape,
                  norm_f_g.shape]
    g_out, d_out, m_out, v_out = [], [], [], []
    for (nm, w, g, m, v), sh in zip(order, out_shapes):
        g = g.reshape(w.shape)
        delta, new_m, new_v = _adamw("adamw_" + nm, w, g, m.reshape(w.shape), v.reshape(w.shape))
        for dst, val in ((g_out, g), (d_out, delta), (m_out, new_m), (v_out, new_v)):
            dst.append((jnp.transpose(val) if nm == "w_in" else val).reshape(sh))
    return (loss, grad_x, *g_out, *d_out, *m_out, *v_out)
```

```python
import functools
import math

import jax
import jax.numpy as jnp
from jax import lax
from jax.experimental import pallas as pl
from jax.experimental.pallas import tpu as pltpu

F32 = jnp.float32
BF16 = jnp.bfloat16
MESH = pl.DeviceIdType.MESH

EPS = 1e-6
HEADS = 8
HEAD_DIM = 64
ATTN_WIDTH = HEADS * HEAD_DIM
HEAD_PAIRS = HEADS // 2
LANES = 128
F_PAD = 2 * LANES
NEG_BIG = -1e30
N_CHIPS = 4
N_DEV = 8

ADAM_LR = 0.001
ADAM_B1 = 0.9
ADAM_B2 = 0.999
ADAM_EPS = 1e-08
ADAM_WD = 0.01
ADAM_STEP = 10

_DIMS = {
    "nn": (((1,), (0,)), ((), ())),
    "nt": (((1,), (1,)), ((), ())),
    "tn": (((0,), (0,)), ((), ())),
}


def _tile(n, target, mult, also=()):
    best = None
    for t in range(mult, n + 1, mult):
        if n % t == 0 and t <= target and all(o % t == 0 for o in also):
            best = t
    if best is None:
        assert all(o == 0 for o in also), (n, target, mult, also)
        return n
    return best


def _sds(shape, dtype):
    return jax.ShapeDtypeStruct(shape, dtype)


def _mm(name, a, b, mode, out_dtype, *, m, n, k, a_off=0, b_off=0, b_roff=0, b3=False, out=None, o_off=0,
        o_width=None, o3=None, add=None, dep=None, tm=1024, tn=2048, tk=2048):
    wb = b.shape[2] if b3 else None
    if mode == "nn":
        tm = _tile(m, tm, 16)
        tk = _tile(k, tk, LANES, (a_off,))
        tn = wb if b3 else _tile(n, tn, LANES, (b_off, o_off))
        a_spec = pl.BlockSpec((tm, tk), lambda i, j, kk: (i, a_off // tk + kk))
        if b3:
            b_spec = pl.BlockSpec((None, tk, tn), lambda i, j, kk: (b_off // tn + j, kk, 0))
        else:
            b_spec = pl.BlockSpec((tk, tn), lambda i, j, kk: (kk, b_off // tn + j))
    elif mode == "nt":
        tm = _tile(m, tm, 16)
        tk = wb if b3 else _tile(k, tk, LANES, (a_off, b_off))
        tn = _tile(n, tn, LANES, (o_off, b_roff))
        a_spec = pl.BlockSpec((tm, tk), lambda i, j, kk: (i, a_off // tk + kk))
        if b3:
            b_spec = pl.BlockSpec((None, tn, tk), lambda i, j, kk: (b_off // tk + kk, b_roff // tn + j, 0))
        else:
            b_spec = pl.BlockSpec((tn, tk), lambda i, j, kk: (b_roff // tn + j, b_off // tk + kk))
    else:
        tm = _tile(m, tm, LANES, (a_off,))
        tk = _tile(k, tk, 16)
        tn = _tile(n, tn, LANES, (b_off, o_off))
        a_spec = pl.BlockSpec((tk, tm), lambda i, j, kk: (kk, a_off // tm + i))
        b_spec = pl.BlockSpec((tk, tn), lambda i, j, kk: (kk, b_off // tn + j))
    assert m % tm == 0 and n % tn == 0 and k % tk == 0, (name, tm, tn, tk)
    nk = k // tk
    if o3 is not None:
        o_spec = pl.BlockSpec((None, tm, tn), lambda i, j, kk: (o_off // tn + j, i, 0))
        out_sds = _sds((o3, m, tn), out_dtype)
    else:
        o_spec = pl.BlockSpec((tm, tn), lambda i, j, kk: (i, o_off // tn + j))
        width = o_width if o_width is not None else (out.shape[1] if out is not None else n)
        out_sds = _sds((m, width), out_dtype)
    use_acc = nk > 1 and out_dtype != F32
    dims = _DIMS[mode]
    has_add, has_out = add is not None, out is not None

    def body(*refs):
        a_ref, b_ref = refs[0], refs[1]
        pos = 2
        add_ref = None
        if has_add:
            add_ref = refs[pos]
            pos += 1
        if has_out:
            pos += 1
        if dep is not None:
            pos += 1
        o_ref = refs[pos]
        acc_ref = refs[pos + 1] if use_acc else None
        part = lax.dot_general(a_ref[...].astype(BF16), b_ref[...].astype(BF16), dims,
                               preferred_element_type=F32)
        if nk == 1:
            if has_add:
                part = part + add_ref[...]
            o_ref[...] = part.astype(o_ref.dtype)
            return
        kk = pl.program_id(2)
        tgt = acc_ref if use_acc else o_ref

        @pl.when(kk == 0)
        def _():
            tgt[...] = part + add_ref[...] if has_add else part

        @pl.when(kk > 0)
        def _():
            tgt[...] += part

        if use_acc:
            @pl.when(kk == nk - 1)
            def _():
                o_ref[...] = acc_ref[...].astype(o_ref.dtype)

    operands, in_specs = [a, b], [a_spec, b_spec]
    if has_add:
        operands.append(add)
        in_specs.append(pl.BlockSpec((tm, tn), lambda i, j, kk: (i, j)))
    aliases = {}
    if has_out:
        aliases = {len(operands): 0}
        operands.append(out)
        in_specs.append(pl.BlockSpec(memory_space=pl.ANY))
    if dep is not None:
        operands.append(dep)
        in_specs.append(pl.BlockSpec(memory_space=pl.ANY))
    return pl.pallas_call(
        body,
        out_shape=out_sds,
        grid=(m // tm, n // tn, nk),
        in_specs=in_specs,
        out_specs=o_spec,
        scratch_shapes=[pltpu.VMEM((tm, tn), F32)] if use_acc else [],
        input_output_aliases=aliases,
        compiler_params=pltpu.CompilerParams(dimension_semantics=("parallel", "parallel", "arbitrary")),
        name=name,
    )(*operands)


def _project(name, h, w_t, groups):
    t, d = h.shape
    tm = _tile(t, 512, 16)
    offs = [sum(n for n, _ in groups[:i]) for i in range(len(groups))]

    def body(h_ref, w_ref, *o_refs):
        hv = h_ref[...]
        for (n, _), off, o_ref in zip(groups, offs, o_refs):
            o_ref[...] = _dot(hv, w_ref[off:off + n, :], "nt").astype(o_ref.dtype)

    return pl.pallas_call(
        body,
        out_shape=tuple(_sds((t, n), dt) for n, dt in groups),
        grid=(t // tm,),
        in_specs=[pl.BlockSpec((tm, d), lambda i: (i, 0)), pl.BlockSpec(w_t.shape, lambda i: (0, 0))],
        out_specs=tuple(pl.BlockSpec((tm, n), lambda i: (i, 0)) for n, _ in groups),
        compiler_params=pltpu.CompilerParams(dimension_semantics=("parallel",)),
        name=name,
    )(h, w_t)


def _rms_fwd(name, x, g):
    t, d = x.shape
    tm = _tile(t, 512, 16)

    def body(x_ref, g_ref, o_ref):
        xv = x_ref[...]
        r = lax.rsqrt(jnp.mean(xv * xv, axis=-1, keepdims=True) + EPS)
        o_ref[...] = ((xv * r) * g_ref[...]).astype(o_ref.dtype)

    return pl.pallas_call(
        body,
        out_shape=_sds((t, d), BF16),
        grid=(t // tm,),
        in_specs=[pl.BlockSpec((tm, d), lambda i: (i, 0)), pl.BlockSpec((1, d), lambda i: (0, 0))],
        out_specs=pl.BlockSpec((tm, d), lambda i: (i, 0)),
        compiler_params=pltpu.CompilerParams(dimension_semantics=("parallel",)),
        name=name,
    )(x, g)


def _rms_bwd(name, x, dh, g, res):
    t, d = x.shape
    tm = _tile(t, 512, 16)

    def body(x_ref, dh_ref, g_ref, res_ref, dx_ref, dg_ref):
        xv = x_ref[...]
        r = lax.rsqrt(jnp.mean(xv * xv, axis=-1, keepdims=True) + EPS)
        xh = xv * r
        dhv = dh_ref[...].astype(F32)
        dxh = dhv * g_ref[...]
        dx_ref[...] = res_ref[...] + r * (dxh - xh * jnp.mean(dxh * xh, axis=-1, keepdims=True))

        @pl.when(pl.program_id(0) == 0)
        def _():
            dg_ref[...] = jnp.zeros_like(dg_ref)

        dg_ref[...] += jnp.sum(dhv * xh, axis=0, keepdims=True)

    row = pl.BlockSpec((tm, d), lambda i: (i, 0))
    vec = pl.BlockSpec((1, d), lambda i: (0, 0))
    return pl.pallas_call(
        body,
        out_shape=(_sds((t, d), F32), _sds((1, d), F32)),
        grid=(t // tm,),
        in_specs=[row, row, vec, row],
        out_specs=(row, vec),
        compiler_params=pltpu.CompilerParams(dimension_semantics=("arbitrary",)),
        name=name,
    )(x, dh, g, res)


def _final_loss(name, x, g, target):
    t, d = x.shape
    tm = _tile(t, 512, 16)

    def body(x_ref, g_ref, t_ref, dx_ref, dxb_ref, loss_ref, dg_ref):
        xv = x_ref[...]
        gv = g_ref[...]
        r = lax.rsqrt(jnp.mean(xv * xv, axis=-1, keepdims=True) + EPS)
        xh = xv * r
        err = xh * gv - t_ref[...]
        dy = err * (1.0 / d)
        dxh = dy * gv
        dx = r * (dxh - xh * jnp.mean(dxh * xh, axis=-1, keepdims=True))
        dx_ref[...] = dx
        dxb_ref[...] = dx.astype(dxb_ref.dtype)
        per_row = jnp.sum(err * err, axis=-1, keepdims=True) * (0.5 / d)

        @pl.when(pl.program_id(0) == 0)
        def _():
            dg_ref[...] = jnp.zeros_like(dg_ref)
            loss_ref[...] = jnp.zeros_like(loss_ref)

        dg_ref[...] += jnp.sum(dy * xh, axis=0, keepdims=True)
        loss_ref[...] += jnp.sum(per_row, axis=0, keepdims=True)

    row = pl.BlockSpec((tm, d), lambda i: (i, 0))
    vec = pl.BlockSpec((1, d), lambda i: (0, 0))
    return pl.pallas_call(
        body,
        out_shape=(_sds((t, d), F32), _sds((t, d), BF16), _sds((1, LANES), F32), _sds((1, d), F32)),
        grid=(t // tm,),
        in_specs=[row, vec, row],
        out_specs=(row, row, pl.BlockSpec((1, LANES), lambda i: (0, 0)), vec),
        compiler_params=pltpu.CompilerParams(dimension_semantics=("arbitrary",)),
        name=name,
    )(x, g, target)


def _shift_down(z, k):
    row = lax.broadcasted_iota(jnp.int32, z.shape, 0)
    return jnp.where(row >= k, pltpu.roll(z, k, axis=0), 0.0)


def _shift_up(z, k):
    s = z.shape[0]
    row = lax.broadcasted_iota(jnp.int32, z.shape, 0)
    return jnp.where(row < s - k, pltpu.roll(z, s - k, axis=0), 0.0)


def _conv3(z, w):
    return (w[2:3] * z + w[0:1] * _shift_down(z, 2)) + w[1:2] * _shift_down(z, 1)


def _conv3_t(dz, w):
    return (w[2:3] * dz + w[0:1] * _shift_up(dz, 2)) + w[1:2] * _shift_up(dz, 1)


def _conv_fwd(name, pc, w, batch, seq, tc):
    cw = w.shape[1]
    nct = cw // tc

    def body(pc_ref, w_ref, o_ref):
        cb = pc_ref[:, 0:tc].astype(F32)
        z = pc_ref[:, tc:2 * tc].astype(F32) * pc_ref[:, 2 * tc:3 * tc].astype(F32)
        o_ref[...] = (cb * _conv3(z, w_ref[...])).astype(o_ref.dtype)

    return pl.pallas_call(
        body,
        out_shape=_sds((batch * seq, cw), BF16),
        grid=(batch, nct),
        in_specs=[pl.BlockSpec((seq, 3 * tc), lambda b, j: (b, j)), pl.BlockSpec((3, tc), lambda b, j: (0, j))],
        out_specs=pl.BlockSpec((seq, tc), lambda b, j: (b, j)),
        compiler_params=pltpu.CompilerParams(dimension_semantics=("parallel", "parallel")),
        name=name,
    )(pc, w)


def _conv_bwd(name, da, pc, w, dproj, batch, seq, tc):
    cw = w.shape[1]
    nct = cw // tc

    def body(da_ref, pc_ref, w_ref, _, dpc_ref, dw_ref):
        wv = w_ref[...]
        cb = pc_ref[:, 0:tc].astype(F32)
        cc = pc_ref[:, tc:2 * tc].astype(F32)
        cin = pc_ref[:, 2 * tc:3 * tc].astype(F32)
        z = cc * cin
        dav = da_ref[...].astype(F32)
        du = dav * cb
        dz = _conv3_t(du, wv)
        dpc_ref[:, 0:tc] = (dav * _conv3(z, wv)).astype(dpc_ref.dtype)
        dpc_ref[:, tc:2 * tc] = (dz * cin).astype(dpc_ref.dtype)
        dpc_ref[:, 2 * tc:3 * tc] = (dz * cc).astype(dpc_ref.dtype)

        @pl.when(pl.program_id(1) == 0)
        def _():
            dw_ref[...] = jnp.zeros_like(dw_ref)

        dw_ref[0:1, :] += jnp.sum(du * _shift_down(z, 2), axis=0, keepdims=True)
        dw_ref[1:2, :] += jnp.sum(du * _shift_down(z, 1), axis=0, keepdims=True)
        dw_ref[2:3, :] += jnp.sum(du * z, axis=0, keepdims=True)

    return pl.pallas_call(
        body,
        out_shape=(_sds(dproj.shape, dproj.dtype), _sds((3, cw), F32)),
        grid=(nct, batch),
        in_specs=[
            pl.BlockSpec((seq, tc), lambda j, b: (b, j)),
            pl.BlockSpec((seq, 3 * tc), lambda j, b: (b, j)),
            pl.BlockSpec((3, tc), lambda j, b: (0, j)),
            pl.BlockSpec(memory_space=pl.ANY),
        ],
        out_specs=(pl.BlockSpec((seq, 3 * tc), lambda j, b: (b, j)), pl.BlockSpec((3, tc), lambda j, b: (0, j))),
        input_output_aliases={3: 0},
        compiler_params=pltpu.CompilerParams(dimension_semantics=("parallel", "arbitrary")),
        name=name,
    )(da, pc, w, dproj)


def _ffn_up_act(name, h2, w_up, w, batch, seq, tc):
    d = h2.shape[1]
    fh = w.shape[1] // 2
    nf = fh // tc

    def body(h_ref, ma_ref, mb_ref, wa_ref, wb_ref, o_ref, ua_ref, ub_ref):
        hv = h_ref[...]
        ua = _dot(hv, ma_ref[...], "nn")
        ub = _dot(hv, mb_ref[...], "nn")
        ua_ref[...] = ua.astype(ua_ref.dtype)
        ub_ref[...] = ub.astype(ub_ref.dtype)
        a = _conv3(ua, wa_ref[...])
        b = _conv3(ub, wb_ref[...])
        o_ref[...] = (a * jax.nn.sigmoid(a) * b).astype(o_ref.dtype)

    act = pl.BlockSpec((seq, tc), lambda b, j: (b, j))
    shape = _sds((batch * seq, fh), BF16)
    return pl.pallas_call(
        body,
        out_shape=(shape, shape, shape),
        grid=(batch, nf),
        in_specs=[
            pl.BlockSpec((seq, d), lambda b, j: (b, 0)),
            pl.BlockSpec((d, tc), lambda b, j: (0, j)),
            pl.BlockSpec((d, tc), lambda b, j: (0, nf + j)),
            pl.BlockSpec((3, tc), lambda b, j: (0, j)),
            pl.BlockSpec((3, tc), lambda b, j: (0, nf + j)),
        ],
        out_specs=(act, act, act),
        compiler_params=pltpu.CompilerParams(dimension_semantics=("parallel", "parallel")),
        name=name,
    )(h2, w_up, w_up, w, w)


def _ffn_bwd(name, dx, w_down, ua, ub, w, batch, seq, tc):
    d = dx.shape[1]
    fh = w.shape[1] // 2
    nf = fh // tc

    def body(dx_ref, md_ref, ua_ref, ub_ref, wa_ref, wb_ref, dua_ref, dub_ref, dw_ref):
        j = pl.program_id(1)
        uav, ubv, wa, wb = ua_ref[...].astype(F32), ub_ref[...].astype(F32), wa_ref[...], wb_ref[...]
        dhv = _dot(dx_ref[...].astype(BF16), md_ref[...], "nt")
        a = _conv3(uav, wa)
        b = _conv3(ubv, wb)
        sg = jax.nn.sigmoid(a)
        da = dhv * b * (sg * (1.0 + a * (1.0 - sg)))
        db = dhv * (a * sg)
        dua_ref[...] = _conv3_t(da, wa).astype(dua_ref.dtype)
        dub_ref[...] = _conv3_t(db, wb).astype(dub_ref.dtype)

        @pl.when((pl.program_id(0) == 0) & (j == 0))
        def _():
            dw_ref[...] = jnp.zeros_like(dw_ref)

        for off, dv, uv in ((0, da, uav), (fh, db, ubv)):
            cols = pl.ds(pl.multiple_of(off + j * tc, LANES), tc)
            dw_ref[0:1, cols] += jnp.sum(dv * _shift_down(uv, 2), axis=0, keepdims=True)
            dw_ref[1:2, cols] += jnp.sum(dv * _shift_down(uv, 1), axis=0, keepdims=True)
            dw_ref[2:3, cols] += jnp.sum(dv * uv, axis=0, keepdims=True)

    act = pl.BlockSpec((seq, tc), lambda b, j: (b, j))
    shape = _sds((batch * seq, fh), BF16)
    return pl.pallas_call(
        body,
        out_shape=(shape, shape, _sds((3, 2 * fh), F32)),
        grid=(batch, nf),
        in_specs=[
            pl.BlockSpec((seq, d), lambda b, j: (b, 0)),
            pl.BlockSpec((tc, d), lambda b, j: (j, 0)),
            act,
            act,
            pl.BlockSpec((3, tc), lambda b, j: (0, j)),
            pl.BlockSpec((3, tc), lambda b, j: (0, nf + j)),
        ],
        out_specs=(act, act, pl.BlockSpec((3, 2 * fh), lambda b, j: (0, 0))),
        compiler_params=pltpu.CompilerParams(dimension_semantics=("arbitrary", "arbitrary")),
        name=name,
    )(dx, w_down, ua, ub, w, w)


def _merge_fwd(name, ycat, gl, bg):
    t, d2 = ycat.shape
    d = d2 // 2
    tm = _tile(t, 1024, 16)

    def body(y_ref, gl_ref, bg_ref, o_ref):
        g = jax.nn.sigmoid(gl_ref[...].astype(F32) + bg_ref[...])
        prod = g * y_ref[...].astype(F32)
        o_ref[...] = (prod[:, 0:d] + prod[:, d:d2]).astype(o_ref.dtype)

    row = pl.BlockSpec((tm, d2), lambda i: (i, 0))
    return pl.pallas_call(
        body,
        out_shape=_sds((t, d), BF16),
        grid=(t // tm,),
        in_specs=[row, row, pl.BlockSpec((1, d2), lambda i: (0, 0))],
        out_specs=pl.BlockSpec((tm, d), lambda i: (i, 0)),
        compiler_params=pltpu.CompilerParams(dimension_semantics=("parallel",)),
        name=name,
    )(ycat, gl, bg)


def _merge_bwd(name, dm, ycat, gl, bg, width, gl_off):
    t, d2 = ycat.shape
    d = d2 // 2
    tm = _tile(t, 1024, 16)
    wb = math.gcd(gl_off, d)
    nw = d // wb

    def body(dm_ref, y_ref, gl_ref, bg_ref, dgl_ref, dy_ref, dbg_ref):
        g = jax.nn.sigmoid(gl_ref[...].astype(F32) + bg_ref[...])
        dmv = dm_ref[...].astype(F32)
        dgl = dmv * y_ref[...].astype(F32) * (g * (1.0 - g))
        dgl_ref[...] = dgl.astype(dgl_ref.dtype)
        dy_ref[...] = (dmv * g).astype(dy_ref.dtype)

        @pl.when(pl.program_id(2) == 0)
        def _():
            dbg_ref[...] = jnp.zeros_like(dbg_ref)

        dbg_ref[...] += jnp.sum(dgl, axis=0, keepdims=True)

    half = pl.BlockSpec((tm, wb), lambda h, j, i: (i, h * nw + j))
    vec = pl.BlockSpec((1, wb), lambda h, j, i: (0, h * nw + j))
    return pl.pallas_call(
        body,
        out_shape=(_sds((t, width), BF16), _sds((t, d2), BF16), _sds((1, d2), F32)),
        grid=(2, nw, t // tm),
        in_specs=[pl.BlockSpec((tm, wb), lambda h, j, i: (i, j)), half, half, vec],
        out_specs=(pl.BlockSpec((tm, wb), lambda h, j, i: (i, gl_off // wb + h * nw + j)), half, vec),
        compiler_params=pltpu.CompilerParams(dimension_semantics=("parallel", "parallel", "arbitrary")),
        name=name,
    )(dm, ycat, gl, bg)


def _log_sigmoid(z):
    return jnp.minimum(z, 0.0) - jnp.log1p(jnp.exp(-jnp.abs(z)))


def _forget_fwd(name, fl, bf, batch, seq):
    def body(fl_ref, bf_ref, o_ref):
        lf = _log_sigmoid(fl_ref[:, 0:LANES] + bf_ref[:, 0:LANES])
        acc = lf.T[0:HEADS, :]
        lane = lax.broadcasted_iota(jnp.int32, acc.shape, 1)
        k = 1
        while k < seq:
            acc = acc + jnp.where(lane >= k, pltpu.roll(acc, k, axis=1), 0.0)
            k *= 2
        o_ref[...] = acc

    return pl.pallas_call(
        body,
        out_shape=_sds((batch, HEADS, seq), F32),
        grid=(batch,),
        in_specs=[pl.BlockSpec((seq, F_PAD), lambda b: (b, 0)), pl.BlockSpec((1, F_PAD), lambda b: (0, 0))],
        out_specs=pl.BlockSpec((None, HEADS, seq), lambda b: (b, 0, 0)),
        compiler_params=pltpu.CompilerParams(dimension_semantics=("parallel",)),
        name=name,
    )(fl, bf)


def _forget_bwd(name, d_key, d_query, fl, bf, dproj, f_off, batch, seq):
    nfb = F_PAD // LANES

    def body(dk_ref, dq_ref, fl_ref, bf_ref, _, df_ref, dbf_ref):
        jj = pl.program_id(1)
        key_t = jnp.concatenate([dk_ref[...], jnp.zeros((LANES - HEADS, seq), F32)], axis=0).T
        acc = dq_ref[...] - key_t
        row = lax.broadcasted_iota(jnp.int32, acc.shape, 0)
        k = 1
        while k < seq:
            acc = acc + jnp.where(row < seq - k, pltpu.roll(acc, seq - k, axis=0), 0.0)
            k *= 2
        z = fl_ref[:, 0:LANES] + bf_ref[:, 0:LANES]
        col = lax.broadcasted_iota(jnp.int32, acc.shape, 1)
        df = jnp.where(col < HEADS, acc * jax.nn.sigmoid(-z), 0.0)
        df = jnp.where(jj == 0, df, 0.0)
        df_ref[...] = df.astype(df_ref.dtype)

        @pl.when((pl.program_id(0) == 0) & (jj == 0))
        def _():
            dbf_ref[...] = jnp.zeros_like(dbf_ref)

        dbf_ref[...] += jnp.sum(df, axis=0, keepdims=True)

    return pl.pallas_call(
        body,
        out_shape=(_sds(dproj.shape, dproj.dtype), _sds((1, LANES), F32)),
        grid=(batch, nfb),
        in_specs=[
            pl.BlockSpec((None, HEADS, seq), lambda b, j: (b, 0, 0)),
            pl.BlockSpec((seq, LANES), lambda b, j: (b, 0)),
            pl.BlockSpec((seq, F_PAD), lambda b, j: (b, 0)),
            pl.BlockSpec((1, F_PAD), lambda b, j: (0, 0)),
            pl.BlockSpec(memory_space=pl.ANY),
        ],
        out_specs=(pl.BlockSpec((seq, LANES), lambda b, j: (b, f_off // LANES + j)),
                   pl.BlockSpec((1, LANES), lambda b, j: (0, 0))),
        input_output_aliases={4: 0},
        compiler_params=pltpu.CompilerParams(dimension_semantics=("arbitrary", "arbitrary")),
        name=name,
    )(d_key, d_query, fl, bf, dproj)


def _dot(a, b, mode):
    return lax.dot_general(a, b, _DIMS[mode], preferred_element_type=F32)


def _attn_fwd(name, qkv, frow, batch, seq, tq):
    nq = seq // tq
    scale = 1.0 / math.sqrt(HEAD_DIM)

    def body(q_ref, k_ref, v_ref, f_ref, o_ref, lse_ref):
        i = pl.program_id(2)
        lane = lax.broadcasted_iota(jnp.int32, (1, LANES), 1)
        lo = lane < HEAD_DIM
        qs = q_ref[...] * scale
        qh = (jnp.where(lo, qs, 0.0).astype(BF16), jnp.where(lo, 0.0, qs).astype(BF16))
        row = lax.broadcasted_iota(jnp.int32, (tq, tq), 0)
        col = lax.broadcasted_iota(jnp.int32, (tq, tq), 1)

        def step(j, carry, diag):
            m0, l0, m1, l1, acc = carry
            start = pl.multiple_of(j * tq, tq)
            kj = k_ref[pl.ds(start, tq), :]
            vj = v_ref[pl.ds(start, tq), :]
            ms, ls, pvs, alphas = [], [], [], []
            for h, (m_old, l_old) in enumerate(((m0, l0), (m1, l1))):
                s = _dot(qh[h], kj, "nt") - f_ref[h:h + 1, pl.ds(start, tq)]
                if diag:
                    s = jnp.where(col <= row, s, NEG_BIG)
                m_new = jnp.maximum(m_old, jnp.max(s, axis=1, keepdims=True))
                p = jnp.exp(s - m_new)
                alpha = jnp.exp(m_old - m_new)
                ls.append(alpha * l_old + jnp.sum(p, axis=1, keepdims=True))
                ms.append(m_new)
                alphas.append(alpha)
                vh = jnp.where(lo, vj, 0.0) if h == 0 else jnp.where(lo, 0.0, vj)
                pvs.append(_dot(p.astype(BF16), vh.astype(BF16), "nn"))
            acc = acc * jnp.where(lo, alphas[0], alphas[1]) + (pvs[0] + pvs[1])
            return ms[0], ls[0], ms[1], ls[1], acc

        neg = jnp.full((tq, 1), NEG_BIG, F32)
        zero = jnp.zeros((tq, 1), F32)
        init = (neg, zero, neg, zero, jnp.zeros((tq, LANES), F32))
        carry = lax.fori_loop(0, i, lambda j, c: step(j, c, False), init)
        m0, l0, m1, l1, acc = step(i, carry, True)
        o_ref[...] = (acc / jnp.where(lo, l0, l1)).astype(o_ref.dtype)
        lse_ref[:, 0:1] = m0 + jnp.log(l0)
        lse_ref[:, 1:2] = m1 + jnp.log(l1)

    return pl.pallas_call(
        body,
        out_shape=(_sds((batch * seq, ATTN_WIDTH), BF16), _sds((HEAD_PAIRS, batch * seq, 2), F32)),
        grid=(batch, HEAD_PAIRS, nq),
        in_specs=[
            pl.BlockSpec((tq, LANES), lambda b, hp, i: (b * nq + i, 3 * hp)),
            pl.BlockSpec((seq, LANES), lambda b, hp, i: (b, 3 * hp + 1)),
            pl.BlockSpec((seq, LANES), lambda b, hp, i: (b, 3 * hp + 2)),
            pl.BlockSpec((None, None, 2, seq), lambda b, hp, i: (b, hp, 0, 0)),
        ],
        out_specs=(
            pl.BlockSpec((tq, LANES), lambda b, hp, i: (b * nq + i, hp)),
            pl.BlockSpec((None, tq, 2), lambda b, hp, i: (hp, b * nq + i, 0)),
        ),
        compiler_params=pltpu.CompilerParams(dimension_semantics=("parallel", "parallel", "parallel")),
        name=name,
    )(qkv, qkv, qkv, frow)


def _attn_bwd(name, qkv, do, o, lse, frow, dproj, qkv_off, batch, seq, tq):
    nq = seq // tq
    scale = 1.0 / math.sqrt(HEAD_DIM)

    def body(q_ref, k_ref, v_ref, do_ref, o_ref, lse_ref, f_ref, _, dqkv_ref, df_ref, drow_ref,
             dq_acc, dk_acc, dv_acc, df_acc):
        j = pl.program_id(2)
        lane = lax.broadcasted_iota(jnp.int32, (1, LANES), 1)
        lo = lane < HEAD_DIM
        masks = (lo, jnp.logical_not(lo))
        row = lax.broadcasted_iota(jnp.int32, (tq, tq), 0)
        col = lax.broadcasted_iota(jnp.int32, (tq, tq), 1)

        @pl.when(j == 0)
        def _():
            dq_acc[...] = jnp.zeros_like(dq_acc)
            drow_ref[...] = jnp.zeros_like(drow_ref)

        dk_acc[...] = jnp.zeros_like(dk_acc)
        dv_acc[...] = jnp.zeros_like(dv_acc)
        df_acc[...] = jnp.zeros_like(df_acc)
        kj = k_ref[...]
        vj = v_ref[...]
        kstart = pl.multiple_of(j * tq, tq)
        kh = tuple(jnp.where(mk, kj, 0.0).astype(BF16) for mk in masks)

        def step(i, diag):
            start = pl.multiple_of(i * tq, tq)
            rows = pl.ds(start, tq)
            qi = q_ref[rows, :] * scale
            doi = do_ref[rows, :]
            prod = doi.astype(F32) * o_ref[rows, :].astype(F32)
            lse_i = lse_ref[rows, :]
            dq_i = jnp.zeros((tq, LANES), F32)
            for h, mk in enumerate(masks):
                q_h = jnp.where(mk, qi, 0.0).astype(BF16)
                do_h = jnp.where(mk, doi, 0.0).astype(BF16)
                delta = jnp.sum(jnp.where(mk, prod, 0.0), axis=1, keepdims=True)
                s = _dot(q_h, kj, "nt") - f_ref[h:h + 1, pl.ds(kstart, tq)]
                p = jnp.exp(s - lse_i[:, h:h + 1])
                if diag:
                    p = jnp.where(col <= row, p, 0.0)
                ds = p * (_dot(do_h, vj, "nt") - delta)
                df_acc[h:h + 1, :] += jnp.sum(ds, axis=0, keepdims=True)
                drow_ref[rows, h:h + 1] += jnp.sum(ds, axis=1, keepdims=True)
                dsb = ds.astype(BF16)
                dv_acc[...] += _dot(p.astype(BF16), do_h, "tn")
                dk_acc[...] += _dot(dsb, q_h, "tn")
                dq_i = dq_i + _dot(dsb, kh[h], "nn")
            dq_acc[rows, :] += dq_i

        step(j, True)
        lax.fori_loop(j + 1, nq, lambda i, c: (step(i, False), c)[1], 0)
        dqkv_ref[:, 0:LANES] = (dq_acc[pl.ds(kstart, tq), :] * scale).astype(dqkv_ref.dtype)
        dqkv_ref[:, LANES:2 * LANES] = dk_acc[...].astype(dqkv_ref.dtype)
        dqkv_ref[:, 2 * LANES:3 * LANES] = dv_acc[...].astype(dqkv_ref.dtype)
        df_ref[...] = df_acc[...]

    full = lambda c: pl.BlockSpec((seq, LANES), lambda b, hp, j: (b, c(hp)))
    blk = lambda c: pl.BlockSpec((tq, LANES), lambda b, hp, j: (b * nq + j, c(hp)))
    return pl.pallas_call(
        body,
        out_shape=(_sds(dproj.shape, dproj.dtype), _sds((batch, HEAD_PAIRS, 2, seq), F32),
                   _sds((HEAD_PAIRS, batch * seq, 2), F32)),
        grid=(batch, HEAD_PAIRS, nq),
        in_specs=[
            full(lambda hp: 3 * hp),
            blk(lambda hp: 3 * hp + 1),
            blk(lambda hp: 3 * hp + 2),
            full(lambda hp: hp),
            full(lambda hp: hp),
            pl.BlockSpec((None, seq, 2), lambda b, hp, j: (hp, b, 0)),
            pl.BlockSpec((None, None, 2, seq), lambda b, hp, j: (b, hp, 0, 0)),
            pl.BlockSpec(memory_space=pl.ANY),
        ],
        out_specs=(
            pl.BlockSpec((tq, 3 * LANES), lambda b, hp, j: (b * nq + j, qkv_off // (3 * LANES) + hp)),
            pl.BlockSpec((None, None, 2, tq), lambda b, hp, j: (b, hp, 0, j)),
            pl.BlockSpec((None, seq, 2), lambda b, hp, j: (hp, b, 0)),
        ),
        scratch_shapes=[
            pltpu.VMEM((seq, LANES), F32),
            pltpu.VMEM((tq, LANES), F32),
            pltpu.VMEM((tq, LANES), F32),
            pltpu.VMEM((2, tq), F32),
        ],
        input_output_aliases={7: 0},
        compiler_params=pltpu.CompilerParams(dimension_semantics=("parallel", "parallel", "arbitrary")),
        name=name,
    )(qkv, qkv, qkv, do, o, lse, frow, dproj)


def _mesh_place():
    x, y, c = lax.axis_index("x"), lax.axis_index("y"), lax.axis_index("c")
    chips = [(1 - x, y), (x, 1 - y), (1 - x, 1 - y)]
    return x, y, c, chips


def _hbm_specs(n):
    return [pl.BlockSpec(memory_space=pl.ANY)] * n


def _half(shape2d, axis, which):
    size = shape2d[axis] // 2
    sl = pl.ds(pl.multiple_of(which * size, 16 if axis == 0 else LANES), size)
    return (sl, slice(None)) if axis == 0 else (slice(None), sl)


def _gather_weights(bigs, axes, smalls):
    nb, ns = len(bigs), len(smalls)
    arrays = list(bigs) + list(smalls)
    n = nb + ns

    def body(*refs):
        ins, outs = refs[:n], refs[n:2 * n]
        send_sems, recv_sems = refs[2 * n:]
        x, y, c, chips = _mesh_place()
        me = 2 * x + y
        sibling = (x, y, 1 - c)

        def half(a, which):
            return _half(arrays[a].shape, axes[a], which)

        def copy(a, k, src, dst, to):
            return pltpu.make_async_remote_copy(src_ref=src, dst_ref=dst, send_sem=send_sems.at[a, k],
                                                recv_sem=recv_sems.at[a, k], device_id=to, device_id_type=MESH)

        sends = []
        for a in range(n):
            for j, chip in enumerate(chips):
                if a < nb:
                    cp = copy(a, j, ins[a].at[half(a, c)], outs[a].at[(me,) + half(a, c)], (*chip, c))
                else:
                    cp = copy(a, j, ins[a], outs[a].at[me], (*chip, c))
                cp.start()
                sends.append(cp)
        for a in range(nb):
            for j, (px, py) in enumerate(chips):
                blk = outs[a].at[(2 * px + py,) + half(a, c)]
                copy(a, j, blk, blk, (px, py, c)).wait_recv()
                fwd = copy(a, 3 + j, blk, blk, sibling)
                fwd.start()
                sends.append(fwd)
        for a in range(nb, n):
            for j, (px, py) in enumerate(chips):
                blk = outs[a].at[2 * px + py]
                copy(a, j, blk, blk, (px, py, c)).wait_recv()
        for a in range(nb):
            for j, (px, py) in enumerate(chips):
                blk = outs[a].at[(2 * px + py,) + half(a, 1 - c)]
                copy(a, 3 + j, blk, blk, sibling).wait_recv()
        for cp in sends:
            cp.wait_send()

    outs = pl.pallas_call(
        body,
        out_shape=tuple(_sds((N_CHIPS,) + a.shape, a.dtype) for a in arrays),
        in_specs=_hbm_specs(n),
        out_specs=tuple(_hbm_specs(n)),
        scratch_shapes=[pltpu.SemaphoreType.DMA((n, 6)), pltpu.SemaphoreType.DMA((n, 6))],
        name="gather_weights",
    )(*arrays)
    me = 2 * lax.axis_index("x") + lax.axis_index("y")
    return tuple(lax.dynamic_update_index_in_dim(o, a, me, 0) for o, a in zip(outs, arrays))


def _gather_small(v):
    m_per, ncol = v.shape

    def body(x_ref, out_ref, send_sems, recv_sems, local_sem):
        x, y, c, chips = _mesh_place()
        me, sibling = (x, y, c), (x, y, 1 - c)

        def rows(px, py, pc):
            return out_ref.at[pl.ds((4 * px + 2 * py + pc) * m_per, m_per), :]

        def copy(k, block, to, src=None):
            return pltpu.make_async_remote_copy(src_ref=rows(*block) if src is None else src, dst_ref=rows(*block),
                                                send_sem=send_sems.at[k], recv_sem=recv_sems.at[k],
                                                device_id=to, device_id_type=MESH)

        mine = pltpu.make_async_copy(x_ref, rows(*me), local_sem)
        mine.start()
        first = [copy(0, me, sibling, src=x_ref)]
        first += [copy(1 + j, me, (*chip, c), src=x_ref) for j, chip in enumerate(chips)]
        for cp in first:
            cp.start()
        passed = [copy(4 + j, (*chip, c), sibling) for j, chip in enumerate(chips)]
        for j, chip in enumerate(chips):
            copy(1 + j, (*chip, c), me).wait_recv()
            passed[j].start()
        copy(0, sibling, me).wait_recv()
        for j, chip in enumerate(chips):
            copy(4 + j, (*chip, 1 - c), me).wait_recv()
        for cp in first + passed:
            cp.wait_send()
        mine.wait()

    return pl.pallas_call(
        body,
        out_shape=_sds((N_DEV * m_per, ncol), v.dtype),
        in_specs=[pl.BlockSpec(memory_space=pltpu.VMEM)],
        out_specs=pl.BlockSpec(memory_space=pltpu.VMEM),
        scratch_shapes=[pltpu.SemaphoreType.DMA((7,)), pltpu.SemaphoreType.DMA((7,)), pltpu.SemaphoreType.DMA],
        name="gather_small",
    )(v)


def _half_shape(shape2d, axis):
    return (shape2d[0] // 2, shape2d[1]) if axis == 0 else (shape2d[0], shape2d[1] // 2)


def _exchange_sibling(name, grads, axes):
    n = len(grads)

    def body(*refs):
        ins, outs = refs[:n], refs[n:2 * n]
        send_sems, recv_sems = refs[2 * n:]
        x, y, c, _ = _mesh_place()
        copies = []
        for a in range(n):
            src = ins[a].at[(slice(None),) + _half(grads[a].shape[1:], axes[a], 1 - c)]
            cp = pltpu.make_async_remote_copy(src_ref=src, dst_ref=outs[a], send_sem=send_sems.at[a],
                                              recv_sem=recv_sems.at[a], device_id=(x, y, 1 - c), device_id_type=MESH)
            cp.start()
            copies.append(cp)
        for cp in copies:
            cp.wait()

    return pl.pallas_call(
        body,
        out_shape=tuple(_sds((N_CHIPS,) + _half_shape(g.shape[1:], ax), g.dtype) for g, ax in zip(grads, axes)),
        in_specs=_hbm_specs(n),
        out_specs=tuple(_hbm_specs(n)),
        scratch_shapes=[pltpu.SemaphoreType.DMA((n,)), pltpu.SemaphoreType.DMA((n,))],
        name=name,
    )(*grads)


_HBM = pl.BlockSpec(memory_space=pltpu.HBM)
_SEM = pl.BlockSpec(memory_space=pltpu.SEMAPHORE)
_EFFECT = pltpu.SideEffectType.DATAFLOW_SIDE_EFFECTING


def _chip_copies(kind, srcs, lands, send_sems, recv_sems):
    x, y, c, chips = _mesh_place()
    copies = []
    for a in range(len(srcs)):
        for j, (px, py) in enumerate(chips):
            if kind == "gather":
                src, dst = srcs[a], lands[a].at[2 * x + y]
            else:
                src, dst = srcs[a].at[j], lands[a].at[j]
            copies.append(pltpu.make_async_remote_copy(src_ref=src, dst_ref=dst, send_sem=send_sems.at[3 * a + j],
                                                       recv_sem=recv_sems.at[3 * a + j], device_id=(px, py, c),
                                                       device_id_type=MESH))
    return copies


def _chips_start(name, kind, srcs):
    n = len(srcs)
    slots = N_CHIPS if kind == "gather" else 3
    lands = [lax.empty((slots,) + (s.shape if kind == "gather" else s.shape[1:]), s.dtype) for s in srcs]

    def body(*refs):
        for cp in _chip_copies(kind, refs[:n], refs[n:2 * n], refs[2 * n], refs[2 * n + 1]):
            cp.start()
        refs[-1][...] = jnp.zeros_like(refs[-1])

    outs = pl.pallas_call(
        body,
        out_shape=(pltpu.SemaphoreType.DMA((3 * n,)), pltpu.SemaphoreType.DMA((3 * n,)),
                   *[pltpu.HBM(v.shape, v.dtype) for v in (*srcs, *lands)], _sds((8, LANES), F32)),
        in_specs=[_HBM] * (2 * n),
        out_specs=(_SEM, _SEM, *[_HBM] * (2 * n), pl.BlockSpec(memory_space=pltpu.VMEM)),
        input_output_aliases={i: 2 + i for i in range(2 * n)},
        compiler_params=pltpu.CompilerParams(has_side_effects=_EFFECT),
        name=name,
    )(*[pltpu.with_memory_space_constraint(v, pltpu.HBM) for v in (*srcs, *lands)])
    return outs[:-1], outs[-1]


def _chips_wait(name, kind, handles, after):
    send_sems, recv_sems, *thru = handles
    n = len(thru) // 2

    def body(*refs):
        for cp in _chip_copies(kind, refs[:n], refs[n:2 * n], refs[2 * n], refs[2 * n + 1]):
            cp.wait_send()
            cp.wait_recv()

    outs = pl.pallas_call(
        body,
        out_shape=tuple(pltpu.HBM(v.shape, v.dtype) for v in thru),
        in_specs=[_HBM] * (2 * n) + [_SEM, _SEM, pl.BlockSpec(memory_space=pl.ANY)],
        out_specs=tuple([_HBM] * (2 * n)),
        input_output_aliases={i: i for i in range(2 * n)},
        compiler_params=pltpu.CompilerParams(has_side_effects=_EFFECT),
        name=name,
    )(*thru, send_sems, recv_sems, after)
    return outs[n:]


def _share_sibling(name, shards, axes):
    n = len(shards)

    def body(*refs):
        ins, outs = refs[:n], refs[n:2 * n]
        send_sems, recv_sems = refs[2 * n:]
        x, y, c, _ = _mesh_place()
        started = []
        for a in range(n):
            mine = _half(shards[a].shape, axes[a], c)
            theirs = _half(shards[a].shape, axes[a], 1 - c)
            cp = pltpu.make_async_remote_copy(src_ref=ins[a].at[mine], dst_ref=outs[a].at[mine],
                                              send_sem=send_sems.at[a], recv_sem=recv_sems.at[a],
                                              device_id=(x, y, 1 - c), device_id_type=MESH)
            cp.start()
            arrival = pltpu.make_async_remote_copy(src_ref=ins[a].at[theirs], dst_ref=outs[a].at[theirs],
                                                   send_sem=send_sems.at[a], recv_sem=recv_sems.at[a],
                                                   device_id=(x, y, 1 - c), device_id_type=MESH)
            started.append((cp, arrival))
        for cp, arrival in started:
            arrival.wait_recv()
            cp.wait_send()

    return pl.pallas_call(
        body,
        out_shape=tuple(_sds(s.shape, s.dtype) for s in shards),
        in_specs=_hbm_specs(n),
        out_specs=tuple(_hbm_specs(n)),
        scratch_shapes=[pltpu.SemaphoreType.DMA((n,)), pltpu.SemaphoreType.DMA((n,))],
        input_output_aliases={a: a for a in range(n)},
        name=name,
    )(*shards)


def _pair_sum(name, place, g, got, axis):
    hr, hc = got.shape[1:]

    def body(place_ref, g_ref, got_ref, o_ref):
        o_ref[...] = (g_ref[...] + got_ref[...]).astype(o_ref.dtype)

    blk = (None, hr, hc)
    mine = (lambda j, pr: (pr[2 + j], pr[1], 0)) if axis == 0 else (lambda j, pr: (pr[2 + j], 0, pr[1]))
    return pl.pallas_call(
        body,
        out_shape=_sds((N_CHIPS - 1, hr, hc), BF16),
        grid_spec=pltpu.PrefetchScalarGridSpec(
            num_scalar_prefetch=1,
            grid=(N_CHIPS - 1,),
            in_specs=[pl.BlockSpec(blk, mine), pl.BlockSpec(blk, lambda j, pr: (pr[2 + j], 0, 0))],
            out_specs=pl.BlockSpec(blk, lambda j, pr: (j, 0, 0)),
        ),
        compiler_params=pltpu.CompilerParams(dimension_semantics=("parallel",)),
        name=name,
    )(place, g, got)


def _chip_sum(name, place, g, got, arrivals, axis):
    _, r, cdim = g.shape
    hr, hc = got.shape[1:]

    def body(place_ref, g_ref, got_ref, arr_ref, o_ref):
        acc = g_ref[...] + got_ref[...]
        for j in range(3):
            acc = acc + arr_ref[j].astype(F32)
        o_ref[...] = acc

    blk = (None, hr, hc)
    mine = (lambda i, pr: (pr[0], pr[1], 0)) if axis == 0 else (lambda i, pr: (pr[0], 0, pr[1]))
    dest = (lambda i, pr: (pr[1], 0)) if axis == 0 else (lambda i, pr: (0, pr[1]))
    return pl.pallas_call(
        body,
        out_shape=_sds((r, cdim), F32),
        grid_spec=pltpu.PrefetchScalarGridSpec(
            num_scalar_prefetch=1,
            grid=(1,),
            in_specs=[
                pl.BlockSpec(blk, mine),
                pl.BlockSpec(blk, lambda i, pr: (pr[0], 0, 0)),
                pl.BlockSpec((3, hr, hc), lambda i, pr: (0, 0, 0)),
            ],
            out_specs=pl.BlockSpec((hr, hc), dest),
        ),
        compiler_params=pltpu.CompilerParams(dimension_semantics=("arbitrary",)),
        name=name,
    )(place, g, got, arrivals)


def _device_sum(name, gathered):
    m_per = gathered.shape[0] // N_DEV

    def body(g_ref, o_ref):
        acc = g_ref[0:m_per, :]
        for dev in range(1, N_DEV):
            acc = acc + g_ref[dev * m_per:(dev + 1) * m_per, :]
        o_ref[...] = acc

    return pl.pallas_call(body, out_shape=_sds((m_per, gathered.shape[1]), F32), name=name)(gathered)


def _adamw(name, w, g, m, v):
    r, cdim = w.shape
    if r % 8 == 0:
        tr, tcol = _tile(r, 256, 8), cdim
    else:
        tr, tcol = r, (_tile(cdim, 256, LANES) if cdim % LANES == 0 else cdim)
    blk = pl.BlockSpec((tr, tcol), lambda i, j: (i, j))
    grid = (r // tr, cdim // tcol)
    bc1 = 1.0 - ADAM_B1 ** ADAM_STEP
    bc2 = 1.0 - ADAM_B2 ** ADAM_STEP

    def body(w_ref, g_ref, m_ref, v_ref, d_ref, nm_ref, nv_ref):
        gv = g_ref[...]
        nm = ADAM_B1 * m_ref[...] + (1.0 - ADAM_B1) * gv
        nv = ADAM_B2 * v_ref[...] + (1.0 - ADAM_B2) * (gv * gv)
        d_ref[...] = -ADAM_LR * ((nm / bc1) / (jnp.sqrt(nv / bc2) + ADAM_EPS) + ADAM_WD * w_ref[...])
        nm_ref[...] = nm
        nv_ref[...] = nv

    shape = _sds(w.shape, F32)
    return pl.pallas_call(
        body,
        out_shape=(shape, shape, shape),
        grid=grid,
        in_specs=[blk] * 4,
        out_specs=(blk, blk, blk),
        compiler_params=pltpu.CompilerParams(dimension_semantics=("parallel", "parallel")),
        name=name,
    )(w, g, m, v)


def _cat_cols(g):
    return jnp.transpose(g, (1, 0, 2)).reshape(g.shape[1], N_CHIPS * g.shape[2])


def _split_cols(a):
    r, c4 = a.shape
    return jnp.transpose(a.reshape(r, N_CHIPS, c4 // N_CHIPS), (1, 0, 2))


def _local_step(x, target, w_int, late_weights, cmw, cfw, g1, b_f, b_gate, g2, gf,
                ffn_grads_ready, mix_grads_ready):
    batch, seq, d = x.shape
    t = batch * seq
    cw = d // 2
    fh = cfw.shape[1] // 2
    tc = LANES
    nct = cw // tc
    tq = min(512, seq)
    pc_w, qkv_w, gl_w = 3 * cw, 3 * ATTN_WIDTH, 2 * d
    qkv_off, gl_off, f_off = pc_w, pc_w + qkv_w, pc_w + qkv_w + gl_w
    width = f_off + F_PAD
    f_col = pc_w + qkv_w

    w_pc = w_int[:pc_w].reshape(3, nct, tc, d).transpose(1, 0, 2, 3).reshape(pc_w, d)
    w_qkv = w_int[pc_w:f_col].reshape(3, HEAD_PAIRS, LANES, d).transpose(1, 0, 2, 3).reshape(qkv_w, d)
    w_f = jnp.pad(w_int[f_col:f_col + HEADS], ((0, F_PAD - HEADS), (0, 0)))
    w_inp = jnp.concatenate([w_pc, w_qkv, w_int[f_col + HEADS:], w_f], axis=0)
    bf_pad = jnp.pad(b_f, ((0, 0), (0, F_PAD - HEADS)))

    x2d = x.reshape(t, d)
    tgt2d = target.reshape(t, d)

    h1 = _rms_fwd("norm_mix", x2d, g1)
    pc, qkv, gl, fl = _project("proj_in", h1, w_inp, [(pc_w, BF16), (qkv_w, BF16), (gl_w, BF16), (F_PAD, F32)])
    a_c = _conv_fwd("conv_mix", pc, cmw, batch, seq, tc)
    f_cum = _forget_fwd("forget_cumsum", fl, bf_pad, batch, seq)
    frow = f_cum.reshape(batch, HEAD_PAIRS, 2, seq)
    o, lse = _attn_fwd("attn_fwd", qkv, frow, batch, seq, tq)
    w_oc, w_oa, w_o, w_up, w_down = late_weights(o)
    ycat = _mm("out_conv", a_c, w_oc, "nn", BF16, m=t, n=d, k=cw, o_off=0, o_width=2 * d)
    ycat = _mm("out_attn", o, w_oa, "nn", BF16, m=t, n=d, k=ATTN_WIDTH, out=ycat, o_off=d)
    mg = _merge_fwd("gate_merge", ycat, gl, b_gate)
    x2 = _mm("mix_out", mg, w_o, "nn", F32, m=t, n=d, k=d, add=x2d)
    h2 = _rms_fwd("norm_ffn", x2, g2)
    tcf = min(2 * LANES, fh)
    w_up2 = _cat_cols(w_up)
    hmid, ua, ub = _ffn_up_act("ffn_up_act", h2, w_up2, cfw, batch, seq, tcf)
    x3 = _mm("ffn_down", hmid, w_down, "nn", F32, m=t, n=d, k=fh, add=x2, tk=4096)

    dx3, dx3b, loss_row, d_gf = _final_loss("final_loss", x3, gf.reshape(1, d), tgt2d)
    dw_down = _mm("dw_down", hmid, dx3b, "tn", F32, m=fh, n=d, k=t, tm=1408, tk=2048)
    du_a, du_b, d_cfw = _ffn_bwd("d_ffn", dx3b, w_down, ua, ub, cfw, batch, seq, tcf)
    ws = w_up.shape[2]
    dh2 = _mm("d_norm_ffn_a", du_a, w_up2, "nt", BF16, m=t, n=d, k=fh, b_off=0, tk=4096)
    dh2 = _mm("d_norm_ffn_b", du_b, w_up2, "nt", BF16, m=t, n=d, k=fh, b_off=fh, add=dh2, tk=4096)
    dw_up = _mm("dw_up_a", h2, du_a, "tn", F32, m=d, n=fh, k=t, tn=ws, tk=2048, o3=N_CHIPS)
    dw_up = _mm("dw_up_b", h2, du_b, "tn", F32, m=d, n=fh, k=t, tn=ws, tk=2048, o3=N_CHIPS, out=dw_up, o_off=fh)
    token = ffn_grads_ready(dw_up, dw_down)
    if token is not None:
        g2 = g2 + token[0:1, 0:1]
    dx2, d_g2 = _rms_bwd("d_norm_ffn", x2, dh2, g2, dx3)
    dm = _mm("d_merge", dx2, w_o, "nt", BF16, m=t, n=d, k=d)
    dw_o = _mm("dw_o", mg, dx2, "tn", F32, m=d, n=d, k=t, tk=2048)
    dproj, dycat, d_bg = _merge_bwd("d_gate_merge", dm, ycat, gl, b_gate, width, gl_off)
    da_c = _mm("d_conv_out", dycat, w_oc, "nt", BF16, m=t, n=cw, k=d, a_off=0)
    do = _mm("d_attn_out", dycat, w_oa, "nt", BF16, m=t, n=ATTN_WIDTH, k=d, a_off=d)
    dw_oc = _mm("dw_out_conv", a_c, dycat, "tn", F32, m=cw, n=d, k=t, b_off=0, tk=2048)
    dw_oa = _mm("dw_out_attn", o, dycat, "tn", F32, m=ATTN_WIDTH, n=d, k=t, b_off=d, tk=2048)
    dproj, d_cmw = _conv_bwd("d_conv_mix", da_c, pc, cmw, dproj, batch, seq, tc)
    dproj, d_fkey, d_fquery = _attn_bwd("attn_bwd", qkv, do, o, lse, frow, dproj, qkv_off, batch, seq, tq)
    d_fquery = jnp.pad(jnp.transpose(d_fquery, (1, 0, 2)).reshape(t, HEADS), ((0, 0), (0, LANES - HEADS)))
    dproj, d_bf = _forget_bwd("d_forget", d_fkey.reshape(batch, HEADS, seq), d_fquery, fl, bf_pad, dproj, f_off,
                              batch, seq)
    dw_inp = _mm("dw_in", dproj, h1, "tn", F32, m=width, n=d, k=t, tm=1792, tk=2048)
    d_pc = dw_inp[:pc_w].reshape(nct, 3, tc, d).transpose(1, 0, 2, 3).reshape(pc_w, d)
    d_qkv = dw_inp[qkv_off:gl_off].reshape(HEAD_PAIRS, 3, LANES, d).transpose(1, 0, 2, 3).reshape(qkv_w, d)
    dw_int = jnp.concatenate([d_pc, d_qkv, dw_inp[f_off:f_off + HEADS], dw_inp[gl_off:f_off]], axis=0)
    token = mix_grads_ready(dw_int, dw_oc, dw_oa, dw_o)
    dh1 = _mm("d_norm_mix", dproj, w_inp, "nn", BF16, m=t, n=d, k=width, tm=512, tk=8192, dep=token)
    grad_x, d_g1 = _rms_bwd("d_norm_mix_x", x2d, dh1, g1, dx2)
    smalls = (d_g1, d_g2, d_gf, d_bg, d_bf, d_cmw, d_cfw)
    return loss_row[0, 0], grad_x.reshape(batch, seq, d), smalls


def _pack_small(parts):
    flat = [p.reshape(-1) for p in parts]
    sizes = [f.shape[0] for f in flat]
    total = sum(sizes)
    padded = -(-total // (8 * LANES)) * (8 * LANES)
    vec = jnp.concatenate(flat + [jnp.zeros((padded - total,), F32)])
    offsets = [sum(sizes[:i]) for i in range(len(sizes))]
    return vec.reshape(padded // LANES, LANES), offsets


def kernel(x, norm_mix_g, w_in, b_f, b_gate, conv_mix_w, w_out_conv, w_out_attn, w_o, norm_ffn_g, w_up, conv_ffn_w, w_down, norm_f_g, loss_target, m_norm_mix_g, m_w_in, m_b_f, m_b_gate, m_conv_mix_w, m_w_out_conv, m_w_out_attn, m_w_o, m_norm_ffn_g, m_w_up, m_conv_ffn_w, m_w_down, m_norm_f_g, v_norm_mix_g, v_w_in, v_b_f, v_b_gate, v_conv_mix_w, v_w_out_conv, v_w_out_attn, v_w_o, v_norm_ffn_g, v_w_up, v_conv_ffn_w, v_w_down, v_norm_f_g):
    d = x.shape[-1]
    chip = 2 * lax.axis_index("x") + lax.axis_index("y")
    xi, yi = lax.axis_index("x"), lax.axis_index("y")
    peers = [2 * px + py for px, py in ((1 - xi, yi), (xi, 1 - yi), (1 - xi, 1 - yi))]
    place = jnp.stack([chip, lax.axis_index("c"), *peers]).astype(jnp.int32)

    t_in, t_m_in, t_v_in = (jnp.transpose(w[0]) for w in (w_in, m_w_in, v_w_in))

    def row_shards(a):
        return a.reshape(N_CHIPS, a.shape[0] // N_CHIPS, a.shape[1])

    def stacked(a):
        return a.reshape(N_CHIPS * a.shape[1], a.shape[2])

    a_in, a_cmw, a_cfw = _gather_weights([t_in.astype(BF16)], (1,), [conv_mix_w[0], conv_ffn_w[0]])
    late = [w[0].astype(BF16) for w in (w_out_conv, w_out_attn, w_o, w_up, w_down)]
    late_handles, late_token = _chips_start("gather_late_start", "gather", late)

    def late_weights(after):
        lands = _chips_wait("gather_late_wait", "gather", late_handles, after)
        a_oc, a_oa, a_o, a_up, a_down = (
            lax.dynamic_update_index_in_dim(buf, own, chip, 0) for buf, own in zip(lands, late))
        return _cat_cols(a_oc), _cat_cols(a_oa), stacked(a_o), a_up, stacked(a_down)

    pending = []

    def reduce_start(tag, names, grads, axes):
        got = _exchange_sibling("exchange_sibling_" + tag, grads, axes)
        sums = [_pair_sum("pair_sum_" + nm, place, g, r, ax) for nm, g, r, ax in zip(names, grads, got, axes)]
        handles, token = _chips_start("exchange_chips_start_" + tag, "reduce", sums)
        pending.append((tag, names, grads, axes, got, handles))
        return token

    def ffn_grads_ready(dw_up, dw_down):
        return reduce_start("ffn", ("w_up", "w_down"), [dw_up, row_shards(dw_down)], (0, 0))

    def mix_grads_ready(dw_int, dw_oc, dw_oa, dw_o):
        return reduce_start("mix", ("w_in", "w_out_conv", "w_out_attn", "w_o"),
                            [row_shards(dw_int), _split_cols(dw_oc), _split_cols(dw_oa), row_shards(dw_o)],
                            (1, 0, 0, 0))

    loss_local, grad_x, smalls = _local_step(
        x, loss_target, stacked(a_in), late_weights, _cat_cols(a_cmw),
        _cat_cols(a_cfw), norm_mix_g + late_token[0:1, 0:1], b_f, b_gate, norm_ffn_g, norm_f_g,
        ffn_grads_ready, mix_grads_ready)

    reduced = {}
    for tag, names, grads, axes, got, handles in pending:
        arrivals = _chips_wait("exchange_chips_wait_" + tag, "reduce", handles, grad_x)
        halves = [_chip_sum("chip_sum_" + nm, place, g, r, arr, ax)
                  for nm, g, r, arr, ax in zip(names, grads, got, arrivals, axes)]
        reduced.update(zip(names, _share_sibling("share_sibling_" + tag, halves, axes)))
    g_in, g_oc, g_oa, g_o, g_up, g_down = (
        reduced[nm] for nm in ("w_in", "w_out_conv", "w_out_attn", "w_o", "w_up", "w_down"))

    smalls = (*smalls, loss_local.reshape(1, 1))
    packed, offs = _pack_small(smalls)
    total = _device_sum("device_sum", _gather_small(packed)).reshape(-1)
    shapes = [s.shape for s in smalls]
    d_g1, d_g2, d_gf, d_bg, d_bf, d_cmw, d_cfw, loss = [
        total[o:o + math.prod(sh)].reshape(sh) for o, sh in zip(offs, shapes)]
    loss = loss[0, 0]
    d_bf = d_bf[:, :HEADS]
    cw_s, cf_s = conv_mix_w.shape[2], conv_ffn_w.shape[2]
    d_cmw = lax.dynamic_slice(d_cmw, (0, chip * cw_s), (3, cw_s))
    d_cfw = lax.dynamic_slice(d_cfw, (0, chip * cf_s), (3, cf_s))

    order = [
        ("norm_mix_g", norm_mix_g[0:1], d_g1, m_norm_mix_g, v_norm_mix_g),
        ("w_in", t_in, g_in, t_m_in, t_v_in),
        ("b_f", b_f, d_bf, m_b_f, v_b_f),
        ("b_gate", b_gate, d_bg, m_b_gate, v_b_gate),
        ("conv_mix_w", conv_mix_w[0], d_cmw, m_conv_mix_w[0], v_conv_mix_w[0]),
        ("w_out_conv", w_out_conv[0], g_oc, m_w_out_conv[0], v_w_out_conv[0]),
        ("w_out_attn", w_out_attn[0], g_oa, m_w_out_attn[0], v_w_out_attn[0]),
        ("w_o", w_o[0], g_o, m_w_o[0], v_w_o[0]),
        ("norm_ffn_g", norm_ffn_g, d_g2, m_norm_ffn_g, v_norm_ffn_g),
        ("w_up", w_up[0], g_up, m_w_up[0], v_w_up[0]),
        ("conv_ffn_w", conv_ffn_w[0], d_cfw, m_conv_ffn_w[0], v_conv_ffn_w[0]),
        ("w_down", w_down[0], g_down, m_w_down[0], v_w_down[0]),
        ("norm_f_g", norm_f_g.reshape(1, d), d_gf, m_norm_f_g.reshape(1, d), v_norm_f_g.reshape(1, d)),
    ]
    out_shapes = [norm_mix_g.shape, w_in.shape, b_f.shape, b_gate.shape, conv_mix_w.shape, w_out_conv.shape,
                  w_out_attn.shape, w_o.shape, norm_ffn_g.shape, w_up.shape, conv_ffn_w.shape, w_down.shape,
                  norm_f_g.shape]
    g_out, d_out, m_out, v_out = [], [], [], []
    for (nm, w, g, m, v), sh in zip(order, out_shapes):
        g = g.reshape(w.shape)
        delta, new_m, new_v = _adamw("adamw_" + nm, w, g, m.reshape(w.shape), v.reshape(w.shape))
        for dst, val in ((g_out, g), (d_out, delta), (m_out, new_m), (v_out, new_v)):
            dst.append((jnp.transpose(val) if nm == "w_in" else val).reshape(sh))
    return (loss, grad_x, *g_out, *d_out, *m_out, *v_out)
```

```python
import functools
import math

import jax
import jax.numpy as jnp
from jax import lax
from jax.experimental import pallas as pl
from jax.experimental.pallas import tpu as pltpu

F32 = jnp.float32
BF16 = jnp.bfloat16
MESH = pl.DeviceIdType.MESH

EPS = 1e-6
HEADS = 8
HEAD_DIM = 64
ATTN_WIDTH = HEADS * HEAD_DIM
HEAD_PAIRS = HEADS // 2
LANES = 128
F_PAD = 2 * LANES
NEG_BIG = -1e30
N_CHIPS = 4
N_DEV = 8

ADAM_LR = 0.001
ADAM_B1 = 0.9
ADAM_B2 = 0.999
ADAM_EPS = 1e-08
ADAM_WD = 0.01
ADAM_STEP = 10

_DIMS = {
    "nn": (((1,), (0,)), ((), ())),
    "nt": (((1,), (1,)), ((), ())),
    "tn": (((0,), (0,)), ((), ())),
}


def _tile(n, target, mult, also=()):
    best = None
    for t in range(mult, n + 1, mult):
        if n % t == 0 and t <= target and all(o % t == 0 for o in also):
            best = t
    if best is None:
        assert all(o == 0 for o in also), (n, target, mult, also)
        return n
    return best


def _sds(shape, dtype):
    return jax.ShapeDtypeStruct(shape, dtype)


def _mm(name, a, b, mode, out_dtype, *, m, n, k, a_off=0, b_off=0, b_roff=0, b3=False, out=None, o_off=0,
        o_width=None, o3=None, add=None, dep=None, tm=1024, tn=2048, tk=2048):
    wb = b.shape[2] if b3 else None
    if mode == "nn":
        tm = _tile(m, tm, 16)
        tk = _tile(k, tk, LANES, (a_off,))
        tn = wb if b3 else _tile(n, tn, LANES, (b_off, o_off))
        a_spec = pl.BlockSpec((tm, tk), lambda i, j, kk: (i, a_off // tk + kk))
        if b3:
            b_spec = pl.BlockSpec((None, tk, tn), lambda i, j, kk: (b_off // tn + j, kk, 0))
        else:
            b_spec = pl.BlockSpec((tk, tn), lambda i, j, kk: (kk, b_off // tn + j))
    elif mode == "nt":
        tm = _tile(m, tm, 16)
        tk = wb if b3 else _tile(k, tk, LANES, (a_off, b_off))
        tn = _tile(n, tn, LANES, (o_off, b_roff))
        a_spec = pl.BlockSpec((tm, tk), lambda i, j, kk: (i, a_off // tk + kk))
        if b3:
            b_spec = pl.BlockSpec((None, tn, tk), lambda i, j, kk: (b_off // tk + kk, b_roff // tn + j, 0))
        else:
            b_spec = pl.BlockSpec((tn, tk), lambda i, j, kk: (b_roff // tn + j, b_off // tk + kk))
    else:
        tm = _tile(m, tm, LANES, (a_off,))
        tk = _tile(k, tk, 16)
        tn = _tile(n, tn, LANES, (b_off, o_off))
        a_spec = pl.BlockSpec((tk, tm), lambda i, j, kk: (kk, a_off // tm + i))
        b_spec = pl.BlockSpec((tk, tn), lambda i, j, kk: (kk, b_off // tn + j))
    assert m % tm == 0 and n % tn == 0 and k % tk == 0, (name, tm, tn, tk)
    nk = k // tk
    if o3 is not None:
        o_spec = pl.BlockSpec((None, tm, tn), lambda i, j, kk: (o_off // tn + j, i, 0))
        out_sds = _sds((o3, m, tn), out_dtype)
    else:
        o_spec = pl.BlockSpec((tm, tn), lambda i, j, kk: (i, o_off // tn + j))
        width = o_width if o_width is not None else (out.shape[1] if out is not None else n)
        out_sds = _sds((m, width), out_dtype)
    use_acc = nk > 1 and out_dtype != F32
    dims = _DIMS[mode]
    has_add, has_out = add is not None, out is not None

    def body(*refs):
        a_ref, b_ref = refs[0], refs[1]
        pos = 2
        add_ref = None
        if has_add:
            add_ref = refs[pos]
            pos += 1
        if has_out:
            pos += 1
        if dep is not None:
            pos += 1
        o_ref = refs[pos]
        acc_ref = refs[pos + 1] if use_acc else None
        part = lax.dot_general(a_ref[...].astype(BF16), b_ref[...].astype(BF16), dims,
                               preferred_element_type=F32)
        if nk == 1:
            if has_add:
                part = part + add_ref[...]
            o_ref[...] = part.astype(o_ref.dtype)
            return
        kk = pl.program_id(2)
        tgt = acc_ref if use_acc else o_ref

        @pl.when(kk == 0)
        def _():
            tgt[...] = part + add_ref[...] if has_add else part

        @pl.when(kk > 0)
        def _():
            tgt[...] += part

        if use_acc:
            @pl.when(kk == nk - 1)
            def _():
                o_ref[...] = acc_ref[...].astype(o_ref.dtype)

    operands, in_specs = [a, b], [a_spec, b_spec]
    if has_add:
        operands.append(add)
        in_specs.append(pl.BlockSpec((tm, tn), lambda i, j, kk: (i, j)))
    aliases = {}
    if has_out:
        aliases = {len(operands): 0}
        operands.append(out)
        in_specs.append(pl.BlockSpec(memory_space=pl.ANY))
    if dep is not None:
        operands.append(dep)
        in_specs.append(pl.BlockSpec(memory_space=pl.ANY))
    return pl.pallas_call(
        body,
        out_shape=out_sds,
        grid=(m // tm, n // tn, nk),
        in_specs=in_specs,
        out_specs=o_spec,
        scratch_shapes=[pltpu.VMEM((tm, tn), F32)] if use_acc else [],
        input_output_aliases=aliases,
        compiler_params=pltpu.CompilerParams(dimension_semantics=("parallel", "parallel", "arbitrary")),
        name=name,
    )(*operands)


def _project(name, h, w_t, groups):
    t, d = h.shape
    tm = _tile(t, 512, 16)
    offs = [sum(n for n, _ in groups[:i]) for i in range(len(groups))]

    def body(h_ref, w_ref, *o_refs):
        hv = h_ref[...]
        for (n, _), off, o_ref in zip(groups, offs, o_refs):
            o_ref[...] = _dot(hv, w_ref[off:off + n, :], "nt").astype(o_ref.dtype)

    return pl.pallas_call(
        body,
        out_shape=tuple(_sds((t, n), dt) for n, dt in groups),
        grid=(t // tm,),
        in_specs=[pl.BlockSpec((tm, d), lambda i: (i, 0)), pl.BlockSpec(w_t.shape, lambda i: (0, 0))],
        out_specs=tuple(pl.BlockSpec((tm, n), lambda i: (i, 0)) for n, _ in groups),
        compiler_params=pltpu.CompilerParams(dimension_semantics=("parallel",)),
        name=name,
    )(h, w_t)


def _rms_fwd(name, x, g):
    t, d = x.shape
    tm = _tile(t, 512, 16)

    def body(x_ref, g_ref, o_ref):
        xv = x_ref[...]
        r = lax.rsqrt(jnp.mean(xv * xv, axis=-1, keepdims=True) + EPS)
        o_ref[...] = ((xv * r) * g_ref[...]).astype(o_ref.dtype)

    return pl.pallas_call(
        body,
        out_shape=_sds((t, d), BF16),
        grid=(t // tm,),
        in_specs=[pl.BlockSpec((tm, d), lambda i: (i, 0)), pl.BlockSpec((1, d), lambda i: (0, 0))],
        out_specs=pl.BlockSpec((tm, d), lambda i: (i, 0)),
        compiler_params=pltpu.CompilerParams(dimension_semantics=("parallel",)),
        name=name,
    )(x, g)


def _rms_bwd(name, x, dh, g, res):
    t, d = x.shape
    tm = _tile(t, 512, 16)

    def body(x_ref, dh_ref, g_ref, res_ref, dx_ref, dg_ref):
        xv = x_ref[...]
        r = lax.rsqrt(jnp.mean(xv * xv, axis=-1, keepdims=True) + EPS)
        xh = xv * r
        dhv = dh_ref[...].astype(F32)
        dxh = dhv * g_ref[...]
        dx_ref[...] = res_ref[...] + r * (dxh - xh * jnp.mean(dxh * xh, axis=-1, keepdims=True))

        @pl.when(pl.program_id(0) == 0)
        def _():
            dg_ref[...] = jnp.zeros_like(dg_ref)

        dg_ref[...] += jnp.sum(dhv * xh, axis=0, keepdims=True)

    row = pl.BlockSpec((tm, d), lambda i: (i, 0))
    vec = pl.BlockSpec((1, d), lambda i: (0, 0))
    return pl.pallas_call(
        body,
        out_shape=(_sds((t, d), F32), _sds((1, d), F32)),
        grid=(t // tm,),
        in_specs=[row, row, vec, row],
        out_specs=(row, vec),
        compiler_params=pltpu.CompilerParams(dimension_semantics=("arbitrary",)),
        name=name,
    )(x, dh, g, res)


def _final_loss(name, x, g, target):
    t, d = x.shape
    tm = _tile(t, 512, 16)

    def body(x_ref, g_ref, t_ref, dx_ref, dxb_ref, loss_ref, dg_ref):
        xv = x_ref[...]
        gv = g_ref[...]
        r = lax.rsqrt(jnp.mean(xv * xv, axis=-1, keepdims=True) + EPS)
        xh = xv * r
        err = xh * gv - t_ref[...]
        dy = err * (1.0 / d)
        dxh = dy * gv
        dx = r * (dxh - xh * jnp.mean(dxh * xh, axis=-1, keepdims=True))
        dx_ref[...] = dx
        dxb_ref[...] = dx.astype(dxb_ref.dtype)
        per_row = jnp.sum(err * err, axis=-1, keepdims=True) * (0.5 / d)

        @pl.when(pl.program_id(0) == 0)
        def _():
            dg_ref[...] = jnp.zeros_like(dg_ref)
            loss_ref[...] = jnp.zeros_like(loss_ref)

        dg_ref[...] += jnp.sum(dy * xh, axis=0, keepdims=True)
        loss_ref[...] += jnp.sum(per_row, axis=0, keepdims=True)

    row = pl.BlockSpec((tm, d), lambda i: (i, 0))
    vec = pl.BlockSpec((1, d), lambda i: (0, 0))
    return pl.pallas_call(
        body,
        out_shape=(_sds((t, d), F32), _sds((t, d), BF16), _sds((1, LANES), F32), _sds((1, d), F32)),
        grid=(t // tm,),
        in_specs=[row, vec, row],
        out_specs=(row, row, pl.BlockSpec((1, LANES), lambda i: (0, 0)), vec),
        compiler_params=pltpu.CompilerParams(dimension_semantics=("arbitrary",)),
        name=name,
    )(x, g, target)


def _shift_down(z, k):
    row = lax.broadcasted_iota(jnp.int32, z.shape, 0)
    return jnp.where(row >= k, pltpu.roll(z, k, axis=0), 0.0)


def _shift_up(z, k):
    s = z.shape[0]
    row = lax.broadcasted_iota(jnp.int32, z.shape, 0)
    return jnp.where(row < s - k, pltpu.roll(z, s - k, axis=0), 0.0)


def _conv3(z, w):
    return (w[2:3] * z + w[0:1] * _shift_down(z, 2)) + w[1:2] * _shift_down(z, 1)


def _conv3_t(dz, w):
    return (w[2:3] * dz + w[0:1] * _shift_up(dz, 2)) + w[1:2] * _shift_up(dz, 1)


def _conv_fwd(name, pc, w, batch, seq, tc):
    cw = w.shape[1]
    nct = cw // tc

    def body(pc_ref, w_ref, o_ref):
        cb = pc_ref[:, 0:tc].astype(F32)
        z = pc_ref[:, tc:2 * tc].astype(F32) * pc_ref[:, 2 * tc:3 * tc].astype(F32)
        o_ref[...] = (cb * _conv3(z, w_ref[...])).astype(o_ref.dtype)

    return pl.pallas_call(
        body,
        out_shape=_sds((batch * seq, cw), BF16),
        grid=(batch, nct),
        in_specs=[pl.BlockSpec((seq, 3 * tc), lambda b, j: (b, j)), pl.BlockSpec((3, tc), lambda b, j: (0, j))],
        out_specs=pl.BlockSpec((seq, tc), lambda b, j: (b, j)),
        compiler_params=pltpu.CompilerParams(dimension_semantics=("parallel", "parallel")),
        name=name,
    )(pc, w)


def _conv_bwd(name, da, pc, w, dproj, batch, seq, tc):
    cw = w.shape[1]
    nct = cw // tc

    def body(da_ref, pc_ref, w_ref, _, dpc_ref, dw_ref):
        wv = w_ref[...]
        cb = pc_ref[:, 0:tc].astype(F32)
        cc = pc_ref[:, tc:2 * tc].astype(F32)
        cin = pc_ref[:, 2 * tc:3 * tc].astype(F32)
        z = cc * cin
        dav = da_ref[...].astype(F32)
        du = dav * cb
        dz = _conv3_t(du, wv)
        dpc_ref[:, 0:tc] = (dav * _conv3(z, wv)).astype(dpc_ref.dtype)
        dpc_ref[:, tc:2 * tc] = (dz * cin).astype(dpc_ref.dtype)
        dpc_ref[:, 2 * tc:3 * tc] = (dz * cc).astype(dpc_ref.dtype)

        @pl.when(pl.program_id(1) == 0)
        def _():
            dw_ref[...] = jnp.zeros_like(dw_ref)

        dw_ref[0:1, :] += jnp.sum(du * _shift_down(z, 2), axis=0, keepdims=True)
        dw_ref[1:2, :] += jnp.sum(du * _shift_down(z, 1), axis=0, keepdims=True)
        dw_ref[2:3, :] += jnp.sum(du * z, axis=0, keepdims=True)

    return pl.pallas_call(
        body,
        out_shape=(_sds(dproj.shape, dproj.dtype), _sds((3, cw), F32)),
        grid=(nct, batch),
        in_specs=[
            pl.BlockSpec((seq, tc), lambda j, b: (b, j)),
            pl.BlockSpec((seq, 3 * tc), lambda j, b: (b, j)),
            pl.BlockSpec((3, tc), lambda j, b: (0, j)),
            pl.BlockSpec(memory_space=pl.ANY),
        ],
        out_specs=(pl.BlockSpec((seq, 3 * tc), lambda j, b: (b, j)), pl.BlockSpec((3, tc), lambda j, b: (0, j))),
        input_output_aliases={3: 0},
        compiler_params=pltpu.CompilerParams(dimension_semantics=("parallel", "arbitrary")),
        name=name,
    )(da, pc, w, dproj)


def _ffn_up_act(name, h2, w_up, w, batch, seq, tc):
    d = h2.shape[1]
    fh = w.shape[1] // 2
    nf = fh // tc

    def body(h_ref, ma_ref, mb_ref, wa_ref, wb_ref, o_ref, ua_ref, ub_ref, a_ref, b_ref):
        hv = h_ref[...]
        ua = _dot(hv, ma_ref[...], "nn")
        ub = _dot(hv, mb_ref[...], "nn")
        ua_ref[...] = ua.astype(ua_ref.dtype)
        ub_ref[...] = ub.astype(ub_ref.dtype)
        a = _conv3(ua, wa_ref[...])
        b = _conv3(ub, wb_ref[...])
        a_ref[...] = a.astype(a_ref.dtype)
        b_ref[...] = b.astype(b_ref.dtype)
        o_ref[...] = (a * jax.nn.sigmoid(a) * b).astype(o_ref.dtype)

    act = pl.BlockSpec((seq, tc), lambda b, j: (b, j))
    shape = _sds((batch * seq, fh), BF16)
    return pl.pallas_call(
        body,
        out_shape=(shape,) * 5,
        grid=(batch, nf),
        in_specs=[
            pl.BlockSpec((seq, d), lambda b, j: (b, 0)),
            pl.BlockSpec((d, tc), lambda b, j: (0, j)),
            pl.BlockSpec((d, tc), lambda b, j: (0, nf + j)),
            pl.BlockSpec((3, tc), lambda b, j: (0, j)),
            pl.BlockSpec((3, tc), lambda b, j: (0, nf + j)),
        ],
        out_specs=(act,) * 5,
        compiler_params=pltpu.CompilerParams(dimension_semantics=("parallel", "parallel")),
        name=name,
    )(h2, w_up, w_up, w, w)


def _ffn_bwd(name, dx, w_down, ua, ub, av, bv, w, batch, seq, tc):
    d = dx.shape[1]
    fh = w.shape[1] // 2
    nf = fh // tc

    rb = _tile(seq, 128, 8)
    halo = 8

    def body(dx_ref, md_ref, ua_ref, ub_ref, a_ref, b_ref, wa_ref, wb_ref, dua_ref, dub_ref, dw_ref,
             dh_scr, da_scr, db_scr):
        j = pl.program_id(1)
        dh_scr[...] = _dot(dx_ref[...].astype(BF16), md_ref[...], "nt")
        da_scr[seq:seq + halo, :] = jnp.zeros((halo, tc), F32)
        db_scr[seq:seq + halo, :] = jnp.zeros((halo, tc), F32)

        def silu_bwd(r, carry):
            rows = pl.ds(pl.multiple_of(r * rb, rb), rb)
            a, b, dhv = a_ref[rows, :].astype(F32), b_ref[rows, :].astype(F32), dh_scr[rows, :]
            sg = jax.nn.sigmoid(a)
            da_scr[rows, :] = dhv * b * (sg * (1.0 + a * (1.0 - sg)))
            db_scr[rows, :] = dhv * (a * sg)
            return carry

        lax.fori_loop(0, seq // rb, silu_bwd, 0)
        wa, wb = wa_ref[...], wb_ref[...]

        def conv_bwd(r, sums):
            r0 = pl.multiple_of(r * rb, rb)
            rows = pl.ds(r0, rb)
            out = []
            for d_scr, u_ref, wv, du_ref, acc in ((da_scr, ua_ref, wa, dua_ref, sums[0:3]),
                                                  (db_scr, ub_ref, wb, dub_ref, sums[3:6])):
                x = d_scr[pl.ds(r0, rb + halo), :]
                dv = x[0:rb]
                up1 = pltpu.roll(x, rb + halo - 1, axis=0)[0:rb]
                up2 = pltpu.roll(x, rb + halo - 2, axis=0)[0:rb]
                du_ref[rows, :] = ((wv[2:3] * dv + wv[0:1] * up2) + wv[1:2] * up1).astype(du_ref.dtype)
                uv = u_ref[rows, :].astype(F32)
                out += [acc[0] + jnp.sum(up2 * uv, axis=0, keepdims=True),
                        acc[1] + jnp.sum(up1 * uv, axis=0, keepdims=True),
                        acc[2] + jnp.sum(dv * uv, axis=0, keepdims=True)]
            return tuple(out)

        sums = lax.fori_loop(0, seq // rb, conv_bwd, (jnp.zeros((1, tc), F32),) * 6)

        @pl.when((pl.program_id(0) == 0) & (j == 0))
        def _():
            dw_ref[...] = jnp.zeros_like(dw_ref)

        for half, off in enumerate((0, fh)):
            cols = pl.ds(pl.multiple_of(off + j * tc, LANES), tc)
            for k in range(3):
                dw_ref[k:k + 1, cols] += sums[3 * half + k]

    act = pl.BlockSpec((seq, tc), lambda b, j: (b, j))
    shape = _sds((batch * seq, fh), BF16)
    return pl.pallas_call(
        body,
        out_shape=(shape, shape, _sds((3, 2 * fh), F32)),
        grid=(batch, nf),
        in_specs=[
            pl.BlockSpec((seq, d), lambda b, j: (b, 0)),
            pl.BlockSpec((tc, d), lambda b, j: (j, 0)),
            act,
            act,
            act,
            act,
            pl.BlockSpec((3, tc), lambda b, j: (0, j)),
            pl.BlockSpec((3, tc), lambda b, j: (0, nf + j)),
        ],
        out_specs=(act, act, pl.BlockSpec((3, 2 * fh), lambda b, j: (0, 0))),
        scratch_shapes=[pltpu.VMEM((seq, tc), F32), pltpu.VMEM((seq + halo, tc), F32),
                        pltpu.VMEM((seq + halo, tc), F32)],
        compiler_params=pltpu.CompilerParams(dimension_semantics=("arbitrary", "arbitrary")),
        name=name,
    )(dx, w_down, ua, ub, av, bv, w, w)


def _merge_fwd(name, ycat, gl, bg):
    t, d2 = ycat.shape
    d = d2 // 2
    tm = _tile(t, 1024, 16)

    def body(y_ref, gl_ref, bg_ref, o_ref):
        g = jax.nn.sigmoid(gl_ref[...].astype(F32) + bg_ref[...])
        prod = g * y_ref[...].astype(F32)
        o_ref[...] = (prod[:, 0:d] + prod[:, d:d2]).astype(o_ref.dtype)

    row = pl.BlockSpec((tm, d2), lambda i: (i, 0))
    return pl.pallas_call(
        body,
        out_shape=_sds((t, d), BF16),
        grid=(t // tm,),
        in_specs=[row, row, pl.BlockSpec((1, d2), lambda i: (0, 0))],
        out_specs=pl.BlockSpec((tm, d), lambda i: (i, 0)),
        compiler_params=pltpu.CompilerParams(dimension_semantics=("parallel",)),
        name=name,
    )(ycat, gl, bg)


def _merge_bwd(name, dm, ycat, gl, bg, width, gl_off):
    t, d2 = ycat.shape
    d = d2 // 2
    tm = _tile(t, 1024, 16)
    wb = math.gcd(gl_off, d)
    nw = d // wb

    def body(dm_ref, y_ref, gl_ref, bg_ref, dgl_ref, dy_ref, dbg_ref):
        g = jax.nn.sigmoid(gl_ref[...].astype(F32) + bg_ref[...])
        dmv = dm_ref[...].astype(F32)
        dgl = dmv * y_ref[...].astype(F32) * (g * (1.0 - g))
        dgl_ref[...] = dgl.astype(dgl_ref.dtype)
        dy_ref[...] = (dmv * g).astype(dy_ref.dtype)

        @pl.when(pl.program_id(2) == 0)
        def _():
            dbg_ref[...] = jnp.zeros_like(dbg_ref)

        dbg_ref[...] += jnp.sum(dgl, axis=0, keepdims=True)

    half = pl.BlockSpec((tm, wb), lambda h, j, i: (i, h * nw + j))
    vec = pl.BlockSpec((1, wb), lambda h, j, i: (0, h * nw + j))
    return pl.pallas_call(
        body,
        out_shape=(_sds((t, width), BF16), _sds((t, d2), BF16), _sds((1, d2), F32)),
        grid=(2, nw, t // tm),
        in_specs=[pl.BlockSpec((tm, wb), lambda h, j, i: (i, j)), half, half, vec],
        out_specs=(pl.BlockSpec((tm, wb), lambda h, j, i: (i, gl_off // wb + h * nw + j)), half, vec),
        compiler_params=pltpu.CompilerParams(dimension_semantics=("parallel", "parallel", "arbitrary")),
        name=name,
    )(dm, ycat, gl, bg)


def _log_sigmoid(z):
    return jnp.minimum(z, 0.0) - jnp.log1p(jnp.exp(-jnp.abs(z)))


def _forget_fwd(name, fl, bf, batch, seq):
    def body(fl_ref, bf_ref, o_ref):
        lf = _log_sigmoid(fl_ref[:, 0:LANES] + bf_ref[:, 0:LANES])
        acc = lf.T[0:HEADS, :]
        lane = lax.broadcasted_iota(jnp.int32, acc.shape, 1)
        k = 1
        while k < seq:
            acc = acc + jnp.where(lane >= k, pltpu.roll(acc, k, axis=1), 0.0)
            k *= 2
        o_ref[...] = acc

    return pl.pallas_call(
        body,
        out_shape=_sds((batch, HEADS, seq), F32),
        grid=(batch,),
        in_specs=[pl.BlockSpec((seq, F_PAD), lambda b: (b, 0)), pl.BlockSpec((1, F_PAD), lambda b: (0, 0))],
        out_specs=pl.BlockSpec((None, HEADS, seq), lambda b: (b, 0, 0)),
        compiler_params=pltpu.CompilerParams(dimension_semantics=("parallel",)),
        name=name,
    )(fl, bf)


def _forget_bwd(name, d_key, d_query, fl, bf, dproj, f_off, batch, seq):
    nfb = F_PAD // LANES

    def body(dk_ref, dq_ref, fl_ref, bf_ref, _, df_ref, dbf_ref):
        jj = pl.program_id(1)
        key_t = jnp.concatenate([dk_ref[...], jnp.zeros((LANES - HEADS, seq), F32)], axis=0).T
        acc = dq_ref[...] - key_t
        row = lax.broadcasted_iota(jnp.int32, acc.shape, 0)
        k = 1
        while k < seq:
            acc = acc + jnp.where(row < seq - k, pltpu.roll(acc, seq - k, axis=0), 0.0)
            k *= 2
        z = fl_ref[:, 0:LANES] + bf_ref[:, 0:LANES]
        col = lax.broadcasted_iota(jnp.int32, acc.shape, 1)
        df = jnp.where(col < HEADS, acc * jax.nn.sigmoid(-z), 0.0)
        df = jnp.where(jj == 0, df, 0.0)
        df_ref[...] = df.astype(df_ref.dtype)

        @pl.when((pl.program_id(0) == 0) & (jj == 0))
        def _():
            dbf_ref[...] = jnp.zeros_like(dbf_ref)

        dbf_ref[...] += jnp.sum(df, axis=0, keepdims=True)

    return pl.pallas_call(
        body,
        out_shape=(_sds(dproj.shape, dproj.dtype), _sds((1, LANES), F32)),
        grid=(batch, nfb),
        in_specs=[
            pl.BlockSpec((None, HEADS, seq), lambda b, j: (b, 0, 0)),
            pl.BlockSpec((seq, LANES), lambda b, j: (b, 0)),
            pl.BlockSpec((seq, F_PAD), lambda b, j: (b, 0)),
            pl.BlockSpec((1, F_PAD), lambda b, j: (0, 0)),
            pl.BlockSpec(memory_space=pl.ANY),
        ],
        out_specs=(pl.BlockSpec((seq, LANES), lambda b, j: (b, f_off // LANES + j)),
                   pl.BlockSpec((1, LANES), lambda b, j: (0, 0))),
        input_output_aliases={4: 0},
        compiler_params=pltpu.CompilerParams(dimension_semantics=("arbitrary", "arbitrary")),
        name=name,
    )(d_key, d_query, fl, bf, dproj)


def _dot(a, b, mode):
    return lax.dot_general(a, b, _DIMS[mode], preferred_element_type=F32)


def _attn_fwd(name, qkv, frow, batch, seq, tq):
    nq = seq // tq
    scale = 1.0 / math.sqrt(HEAD_DIM)

    def body(q_ref, k_ref, v_ref, f_ref, o_ref, lse_ref):
        i = pl.program_id(2)
        lane = lax.broadcasted_iota(jnp.int32, (1, LANES), 1)
        lo = lane < HEAD_DIM
        qs = q_ref[...] * scale
        qh = (jnp.where(lo, qs, 0.0).astype(BF16), jnp.where(lo, 0.0, qs).astype(BF16))
        row = lax.broadcasted_iota(jnp.int32, (tq, tq), 0)
        col = lax.broadcasted_iota(jnp.int32, (tq, tq), 1)

        def step(j, carry, diag):
            m0, l0, m1, l1, acc = carry
            start = pl.multiple_of(j * tq, tq)
            kj = k_ref[pl.ds(start, tq), :]
            vj = v_ref[pl.ds(start, tq), :]
            ms, ls, pvs, alphas = [], [], [], []
            for h, (m_old, l_old) in enumerate(((m0, l0), (m1, l1))):
                s = _dot(qh[h], kj, "nt") - f_ref[h:h + 1, pl.ds(start, tq)]
                if diag:
                    s = jnp.where(col <= row, s, NEG_BIG)
                m_new = jnp.maximum(m_old, jnp.max(s, axis=1, keepdims=True))
                p = jnp.exp(s - m_new)
                alpha = jnp.exp(m_old - m_new)
                ls.append(alpha * l_old + jnp.sum(p, axis=1, keepdims=True))
                ms.append(m_new)
                alphas.append(alpha)
                vh = jnp.where(lo, vj, 0.0) if h == 0 else jnp.where(lo, 0.0, vj)
                pvs.append(_dot(p.astype(BF16), vh.astype(BF16), "nn"))
            acc = acc * jnp.where(lo, alphas[0], alphas[1]) + (pvs[0] + pvs[1])
            return ms[0], ls[0], ms[1], ls[1], acc

        neg = jnp.full((tq, 1), NEG_BIG, F32)
        zero = jnp.zeros((tq, 1), F32)
        init = (neg, zero, neg, zero, jnp.zeros((tq, LANES), F32))
        carry = lax.fori_loop(0, i, lambda j, c: step(j, c, False), init)
        m0, l0, m1, l1, acc = step(i, carry, True)
        o_ref[...] = (acc / jnp.where(lo, l0, l1)).astype(o_ref.dtype)
        lse_ref[:, 0:1] = m0 + jnp.log(l0)
        lse_ref[:, 1:2] = m1 + jnp.log(l1)

    return pl.pallas_call(
        body,
        out_shape=(_sds((batch * seq, ATTN_WIDTH), BF16), _sds((HEAD_PAIRS, batch * seq, 2), F32)),
        grid=(batch, HEAD_PAIRS, nq),
        in_specs=[
            pl.BlockSpec((tq, LANES), lambda b, hp, i: (b * nq + i, 3 * hp)),
            pl.BlockSpec((seq, LANES), lambda b, hp, i: (b, 3 * hp + 1)),
            pl.BlockSpec((seq, LANES), lambda b, hp, i: (b, 3 * hp + 2)),
            pl.BlockSpec((None, None, 2, seq), lambda b, hp, i: (b, hp, 0, 0)),
        ],
        out_specs=(
            pl.BlockSpec((tq, LANES), lambda b, hp, i: (b * nq + i, hp)),
            pl.BlockSpec((None, tq, 2), lambda b, hp, i: (hp, b * nq + i, 0)),
        ),
        compiler_params=pltpu.CompilerParams(dimension_semantics=("parallel", "parallel", "parallel")),
        name=name,
    )(qkv, qkv, qkv, frow)


def _attn_bwd(name, qkv, do, o, lse, frow, dproj, qkv_off, batch, seq, tq):
    nq = seq // tq
    scale = 1.0 / math.sqrt(HEAD_DIM)

    def body(q_ref, k_ref, v_ref, do_ref, o_ref, lse_ref, f_ref, _, dqkv_ref, df_ref, drow_ref,
             dq_acc, dk_acc, dv_acc, df_acc):
        j = pl.program_id(2)
        lane = lax.broadcasted_iota(jnp.int32, (1, LANES), 1)
        lo = lane < HEAD_DIM
        masks = (lo, jnp.logical_not(lo))
        row = lax.broadcasted_iota(jnp.int32, (tq, tq), 0)
        col = lax.broadcasted_iota(jnp.int32, (tq, tq), 1)

        @pl.when(j == 0)
        def _():
            dq_acc[...] = jnp.zeros_like(dq_acc)
            drow_ref[...] = jnp.zeros_like(drow_ref)

        dk_acc[...] = jnp.zeros_like(dk_acc)
        dv_acc[...] = jnp.zeros_like(dv_acc)
        df_acc[...] = jnp.zeros_like(df_acc)
        kj = k_ref[...]
        vj = v_ref[...]
        kstart = pl.multiple_of(j * tq, tq)
        kh = tuple(jnp.where(mk, kj, 0.0).astype(BF16) for mk in masks)

        def step(i, diag):
            start = pl.multiple_of(i * tq, tq)
            rows = pl.ds(start, tq)
            qi = q_ref[rows, :] * scale
            doi = do_ref[rows, :]
            prod = doi.astype(F32) * o_ref[rows, :].astype(F32)
            lse_i = lse_ref[rows, :]
            dq_i = jnp.zeros((tq, LANES), F32)
            for h, mk in enumerate(masks):
                q_h = jnp.where(mk, qi, 0.0).astype(BF16)
                do_h = jnp.where(mk, doi, 0.0).astype(BF16)
                delta = jnp.sum(jnp.where(mk, prod, 0.0), axis=1, keepdims=True)
                s = _dot(q_h, kj, "nt") - f_ref[h:h + 1, pl.ds(kstart, tq)]
                p = jnp.exp(s - lse_i[:, h:h + 1])
                if diag:
                    p = jnp.where(col <= row, p, 0.0)
                ds = p * (_dot(do_h, vj, "nt") - delta)
                df_acc[h:h + 1, :] += jnp.sum(ds, axis=0, keepdims=True)
                drow_ref[rows, h:h + 1] += jnp.sum(ds, axis=1, keepdims=True)
                dsb = ds.astype(BF16)
                dv_acc[...] += _dot(p.astype(BF16), do_h, "tn")
                dk_acc[...] += _dot(dsb, q_h, "tn")
                dq_i = dq_i + _dot(dsb, kh[h], "nn")
            dq_acc[rows, :] += dq_i

        step(j, True)
        lax.fori_loop(j + 1, nq, lambda i, c: (step(i, False), c)[1], 0)
        dqkv_ref[:, 0:LANES] = (dq_acc[pl.ds(kstart, tq), :] * scale).astype(dqkv_ref.dtype)
        dqkv_ref[:, LANES:2 * LANES] = dk_acc[...].astype(dqkv_ref.dtype)
        dqkv_ref[:, 2 * LANES:3 * LANES] = dv_acc[...].astype(dqkv_ref.dtype)
        df_ref[...] = df_acc[...]

    full = lambda c: pl.BlockSpec((seq, LANES), lambda b, hp, j: (b, c(hp)))
    blk = lambda c: pl.BlockSpec((tq, LANES), lambda b, hp, j: (b * nq + j, c(hp)))
    return pl.pallas_call(
        body,
        out_shape=(_sds(dproj.shape, dproj.dtype), _sds((batch, HEAD_PAIRS, 2, seq), F32),
                   _sds((HEAD_PAIRS, batch * seq, 2), F32)),
        grid=(batch, HEAD_PAIRS, nq),
        in_specs=[
            full(lambda hp: 3 * hp),
            blk(lambda hp: 3 * hp + 1),
            blk(lambda hp: 3 * hp + 2),
            full(lambda hp: hp),
            full(lambda hp: hp),
            pl.BlockSpec((None, seq, 2), lambda b, hp, j: (hp, b, 0)),
            pl.BlockSpec((None, None, 2, seq), lambda b, hp, j: (b, hp, 0, 0)),
            pl.BlockSpec(memory_space=pl.ANY),
        ],
        out_specs=(
            pl.BlockSpec((tq, 3 * LANES), lambda b, hp, j: (b * nq + j, qkv_off // (3 * LANES) + hp)),
            pl.BlockSpec((None, None, 2, tq), lambda b, hp, j: (b, hp, 0, j)),
            pl.BlockSpec((None, seq, 2), lambda b, hp, j: (hp, b, 0)),
        ),
        scratch_shapes=[
            pltpu.VMEM((seq, LANES), F32),
            pltpu.VMEM((tq, LANES), F32),
            pltpu.VMEM((tq, LANES), F32),
            pltpu.VMEM((2, tq), F32),
        ],
        input_output_aliases={7: 0},
        compiler_params=pltpu.CompilerParams(dimension_semantics=("parallel", "parallel", "arbitrary")),
        name=name,
    )(qkv, qkv, qkv, do, o, lse, frow, dproj)


def _mesh_place():
    x, y, c = lax.axis_index("x"), lax.axis_index("y"), lax.axis_index("c")
    chips = [(1 - x, y), (x, 1 - y), (1 - x, 1 - y)]
    return x, y, c, chips


def _hbm_specs(n):
    return [pl.BlockSpec(memory_space=pl.ANY)] * n


def _half(shape2d, axis, which):
    size = shape2d[axis] // 2
    sl = pl.ds(pl.multiple_of(which * size, 16 if axis == 0 else LANES), size)
    return (sl, slice(None)) if axis == 0 else (slice(None), sl)


def _gather_weights(bigs, axes, smalls):
    nb, ns = len(bigs), len(smalls)
    arrays = list(bigs) + list(smalls)
    n = nb + ns

    def body(*refs):
        ins, outs = refs[:n], refs[n:2 * n]
        send_sems, recv_sems = refs[2 * n:]
        x, y, c, chips = _mesh_place()
        me = 2 * x + y
        sibling = (x, y, 1 - c)

        def half(a, which):
            return _half(arrays[a].shape, axes[a], which)

        def copy(a, k, src, dst, to):
            return pltpu.make_async_remote_copy(src_ref=src, dst_ref=dst, send_sem=send_sems.at[a, k],
                                                recv_sem=recv_sems.at[a, k], device_id=to, device_id_type=MESH)

        sends = []
        for a in range(n):
            for j, chip in enumerate(chips):
                if a < nb:
                    cp = copy(a, j, ins[a].at[half(a, c)], outs[a].at[(me,) + half(a, c)], (*chip, c))
                else:
                    cp = copy(a, j, ins[a], outs[a].at[me], (*chip, c))
                cp.start()
                sends.append(cp)
        for a in range(nb):
            for j, (px, py) in enumerate(chips):
                blk = outs[a].at[(2 * px + py,) + half(a, c)]
                copy(a, j, blk, blk, (px, py, c)).wait_recv()
                fwd = copy(a, 3 + j, blk, blk, sibling)
                fwd.start()
                sends.append(fwd)
        for a in range(nb, n):
            for j, (px, py) in enumerate(chips):
                blk = outs[a].at[2 * px + py]
                copy(a, j, blk, blk, (px, py, c)).wait_recv()
        for a in range(nb):
            for j, (px, py) in enumerate(chips):
                blk = outs[a].at[(2 * px + py,) + half(a, 1 - c)]
                copy(a, 3 + j, blk, blk, sibling).wait_recv()
        for cp in sends:
            cp.wait_send()

    outs = pl.pallas_call(
        body,
        out_shape=tuple(_sds((N_CHIPS,) + a.shape, a.dtype) for a in arrays),
        in_specs=_hbm_specs(n),
        out_specs=tuple(_hbm_specs(n)),
        scratch_shapes=[pltpu.SemaphoreType.DMA((n, 6)), pltpu.SemaphoreType.DMA((n, 6))],
        name="gather_weights",
    )(*arrays)
    me = 2 * lax.axis_index("x") + lax.axis_index("y")
    return tuple(lax.dynamic_update_index_in_dim(o, a, me, 0) for o, a in zip(outs, arrays))


def _gather_small(v):
    m_per, ncol = v.shape

    def body(x_ref, out_ref, send_sems, recv_sems, local_sem):
        x, y, c, chips = _mesh_place()
        me, sibling = (x, y, c), (x, y, 1 - c)

        def rows(px, py, pc):
            return out_ref.at[pl.ds((4 * px + 2 * py + pc) * m_per, m_per), :]

        def copy(k, block, to, src=None):
            return pltpu.make_async_remote_copy(src_ref=rows(*block) if src is None else src, dst_ref=rows(*block),
                                                send_sem=send_sems.at[k], recv_sem=recv_sems.at[k],
                                                device_id=to, device_id_type=MESH)

        mine = pltpu.make_async_copy(x_ref, rows(*me), local_sem)
        mine.start()
        first = [copy(0, me, sibling, src=x_ref)]
        first += [copy(1 + j, me, (*chip, c), src=x_ref) for j, chip in enumerate(chips)]
        for cp in first:
            cp.start()
        passed = [copy(4 + j, (*chip, c), sibling) for j, chip in enumerate(chips)]
        for j, chip in enumerate(chips):
            copy(1 + j, (*chip, c), me).wait_recv()
            passed[j].start()
        copy(0, sibling, me).wait_recv()
        for j, chip in enumerate(chips):
            copy(4 + j, (*chip, 1 - c), me).wait_recv()
        for cp in first + passed:
            cp.wait_send()
        mine.wait()

    return pl.pallas_call(
        body,
        out_shape=_sds((N_DEV * m_per, ncol), v.dtype),
        in_specs=[pl.BlockSpec(memory_space=pltpu.VMEM)],
        out_specs=pl.BlockSpec(memory_space=pltpu.VMEM),
        scratch_shapes=[pltpu.SemaphoreType.DMA((7,)), pltpu.SemaphoreType.DMA((7,)), pltpu.SemaphoreType.DMA],
        name="gather_small",
    )(v)


def _half_shape(shape2d, axis):
    return (shape2d[0] // 2, shape2d[1]) if axis == 0 else (shape2d[0], shape2d[1] // 2)


def _exchange_sibling(name, grads, axes):
    n = len(grads)

    def body(*refs):
        ins, outs = refs[:n], refs[n:2 * n]
        send_sems, recv_sems = refs[2 * n:]
        x, y, c, _ = _mesh_place()
        copies = []
        for a in range(n):
            src = ins[a].at[(slice(None),) + _half(grads[a].shape[1:], axes[a], 1 - c)]
            cp = pltpu.make_async_remote_copy(src_ref=src, dst_ref=outs[a], send_sem=send_sems.at[a],
                                              recv_sem=recv_sems.at[a], device_id=(x, y, 1 - c), device_id_type=MESH)
            cp.start()
            copies.append(cp)
        for cp in copies:
            cp.wait()

    return pl.pallas_call(
        body,
        out_shape=tuple(_sds((N_CHIPS,) + _half_shape(g.shape[1:], ax), g.dtype) for g, ax in zip(grads, axes)),
        in_specs=_hbm_specs(n),
        out_specs=tuple(_hbm_specs(n)),
        scratch_shapes=[pltpu.SemaphoreType.DMA((n,)), pltpu.SemaphoreType.DMA((n,))],
        name=name,
    )(*grads)


_HBM = pl.BlockSpec(memory_space=pltpu.HBM)
_SEM = pl.BlockSpec(memory_space=pltpu.SEMAPHORE)
_EFFECT = pltpu.SideEffectType.DATAFLOW_SIDE_EFFECTING


def _chip_copies(kind, srcs, lands, send_sems, recv_sems):
    x, y, c, chips = _mesh_place()
    copies = []
    for a in range(len(srcs)):
        for j, (px, py) in enumerate(chips):
            if kind == "gather":
                src, dst = srcs[a], lands[a].at[2 * x + y]
            else:
                src, dst = srcs[a].at[j], lands[a].at[j]
            copies.append(pltpu.make_async_remote_copy(src_ref=src, dst_ref=dst, send_sem=send_sems.at[3 * a + j],
                                                       recv_sem=recv_sems.at[3 * a + j], device_id=(px, py, c),
                                                       device_id_type=MESH))
    return copies


def _chips_start(name, kind, srcs):
    n = len(srcs)
    slots = N_CHIPS if kind == "gather" else 3
    lands = [lax.empty((slots,) + (s.shape if kind == "gather" else s.shape[1:]), s.dtype) for s in srcs]

    def body(*refs):
        for cp in _chip_copies(kind, refs[:n], refs[n:2 * n], refs[2 * n], refs[2 * n + 1]):
            cp.start()
        refs[-1][...] = jnp.zeros_like(refs[-1])

    outs = pl.pallas_call(
        body,
        out_shape=(pltpu.SemaphoreType.DMA((3 * n,)), pltpu.SemaphoreType.DMA((3 * n,)),
                   *[pltpu.HBM(v.shape, v.dtype) for v in (*srcs, *lands)], _sds((8, LANES), F32)),
        in_specs=[_HBM] * (2 * n),
        out_specs=(_SEM, _SEM, *[_HBM] * (2 * n), pl.BlockSpec(memory_space=pltpu.VMEM)),
        input_output_aliases={i: 2 + i for i in range(2 * n)},
        compiler_params=pltpu.CompilerParams(has_side_effects=_EFFECT),
        name=name,
    )(*[pltpu.with_memory_space_constraint(v, pltpu.HBM) for v in (*srcs, *lands)])
    return outs[:-1], outs[-1]


def _chips_wait(name, kind, handles, after):
    send_sems, recv_sems, *thru = handles
    n = len(thru) // 2

    def body(*refs):
        for cp in _chip_copies(kind, refs[:n], refs[n:2 * n], refs[2 * n], refs[2 * n + 1]):
            cp.wait_send()
            cp.wait_recv()

    outs = pl.pallas_call(
        body,
        out_shape=tuple(pltpu.HBM(v.shape, v.dtype) for v in thru),
        in_specs=[_HBM] * (2 * n) + [_SEM, _SEM, pl.BlockSpec(memory_space=pl.ANY)],
        out_specs=tuple([_HBM] * (2 * n)),
        input_output_aliases={i: i for i in range(2 * n)},
        compiler_params=pltpu.CompilerParams(has_side_effects=_EFFECT),
        name=name,
    )(*thru, send_sems, recv_sems, after)
    return outs[n:]


def _share_sibling(name, shards, axes):
    n = len(shards)

    def body(*refs):
        ins, outs = refs[:n], refs[n:2 * n]
        send_sems, recv_sems = refs[2 * n:]
        x, y, c, _ = _mesh_place()
        started = []
        for a in range(n):
            mine = _half(shards[a].shape, axes[a], c)
            theirs = _half(shards[a].shape, axes[a], 1 - c)
            cp = pltpu.make_async_remote_copy(src_ref=ins[a].at[mine], dst_ref=outs[a].at[mine],
                                              send_sem=send_sems.at[a], recv_sem=recv_sems.at[a],
                                              device_id=(x, y, 1 - c), device_id_type=MESH)
            cp.start()
            arrival = pltpu.make_async_remote_copy(src_ref=ins[a].at[theirs], dst_ref=outs[a].at[theirs],
                                                   send_sem=send_sems.at[a], recv_sem=recv_sems.at[a],
                                                   device_id=(x, y, 1 - c), device_id_type=MESH)
            started.append((cp, arrival))
        for cp, arrival in started:
            arrival.wait_recv()
            cp.wait_send()

    return pl.pallas_call(
        body,
        out_shape=tuple(_sds(s.shape, s.dtype) for s in shards),
        in_specs=_hbm_specs(n),
        out_specs=tuple(_hbm_specs(n)),
        scratch_shapes=[pltpu.SemaphoreType.DMA((n,)), pltpu.SemaphoreType.DMA((n,))],
        input_output_aliases={a: a for a in range(n)},
        name=name,
    )(*shards)


def _pair_sum(name, place, g, got, axis):
    hr, hc = got.shape[1:]

    def body(place_ref, g_ref, got_ref, o_ref):
        o_ref[...] = (g_ref[...] + got_ref[...]).astype(o_ref.dtype)

    blk = (None, hr, hc)
    mine = (lambda j, pr: (pr[2 + j], pr[1], 0)) if axis == 0 else (lambda j, pr: (pr[2 + j], 0, pr[1]))
    return pl.pallas_call(
        body,
        out_shape=_sds((N_CHIPS - 1, hr, hc), BF16),
        grid_spec=pltpu.PrefetchScalarGridSpec(
            num_scalar_prefetch=1,
            grid=(N_CHIPS - 1,),
            in_specs=[pl.BlockSpec(blk, mine), pl.BlockSpec(blk, lambda j, pr: (pr[2 + j], 0, 0))],
            out_specs=pl.BlockSpec(blk, lambda j, pr: (j, 0, 0)),
        ),
        compiler_params=pltpu.CompilerParams(dimension_semantics=("parallel",)),
        name=name,
    )(place, g, got)


def _chip_sum(name, place, g, got, arrivals, axis):
    _, r, cdim = g.shape
    hr, hc = got.shape[1:]

    def body(place_ref, g_ref, got_ref, arr_ref, o_ref):
        acc = g_ref[...] + got_ref[...]
        for j in range(3):
            acc = acc + arr_ref[j].astype(F32)
        o_ref[...] = acc

    blk = (None, hr, hc)
    mine = (lambda i, pr: (pr[0], pr[1], 0)) if axis == 0 else (lambda i, pr: (pr[0], 0, pr[1]))
    dest = (lambda i, pr: (pr[1], 0)) if axis == 0 else (lambda i, pr: (0, pr[1]))
    return pl.pallas_call(
        body,
        out_shape=_sds((r, cdim), F32),
        grid_spec=pltpu.PrefetchScalarGridSpec(
            num_scalar_prefetch=1,
            grid=(1,),
            in_specs=[
                pl.BlockSpec(blk, mine),
                pl.BlockSpec(blk, lambda i, pr: (pr[0], 0, 0)),
                pl.BlockSpec((3, hr, hc), lambda i, pr: (0, 0, 0)),
            ],
            out_specs=pl.BlockSpec((hr, hc), dest),
        ),
        compiler_params=pltpu.CompilerParams(dimension_semantics=("arbitrary",)),
        name=name,
    )(place, g, got, arrivals)


def _device_sum(name, gathered):
    m_per = gathered.shape[0] // N_DEV

    def body(g_ref, o_ref):
        acc = g_ref[0:m_per, :]
        for dev in range(1, N_DEV):
            acc = acc + g_ref[dev * m_per:(dev + 1) * m_per, :]
        o_ref[...] = acc

    return pl.pallas_call(body, out_shape=_sds((m_per, gathered.shape[1]), F32), name=name)(gathered)


def _adamw(name, w, g, m, v):
    r, cdim = w.shape
    if r % 8 == 0:
        tr, tcol = _tile(r, 256, 8), cdim
    else:
        tr, tcol = r, (_tile(cdim, 256, LANES) if cdim % LANES == 0 else cdim)
    blk = pl.BlockSpec((tr, tcol), lambda i, j: (i, j))
    grid = (r // tr, cdim // tcol)
    bc1 = 1.0 - ADAM_B1 ** ADAM_STEP
    bc2 = 1.0 - ADAM_B2 ** ADAM_STEP

    def body(w_ref, g_ref, m_ref, v_ref, d_ref, nm_ref, nv_ref):
        gv = g_ref[...]
        nm = ADAM_B1 * m_ref[...] + (1.0 - ADAM_B1) * gv
        nv = ADAM_B2 * v_ref[...] + (1.0 - ADAM_B2) * (gv * gv)
        d_ref[...] = -ADAM_LR * ((nm / bc1) / (jnp.sqrt(nv / bc2) + ADAM_EPS) + ADAM_WD * w_ref[...])
        nm_ref[...] = nm
        nv_ref[...] = nv

    shape = _sds(w.shape, F32)
    return pl.pallas_call(
        body,
        out_shape=(shape, shape, shape),
        grid=grid,
        in_specs=[blk] * 4,
        out_specs=(blk, blk, blk),
        compiler_params=pltpu.CompilerParams(dimension_semantics=("parallel", "parallel")),
        name=name,
    )(w, g, m, v)


def _cat_cols(g):
    return jnp.transpose(g, (1, 0, 2)).reshape(g.shape[1], N_CHIPS * g.shape[2])


def _split_cols(a):
    r, c4 = a.shape
    return jnp.transpose(a.reshape(r, N_CHIPS, c4 // N_CHIPS), (1, 0, 2))


def _local_step(x, target, w_int, late_weights, cmw, cfw, g1, b_f, b_gate, g2, gf,
                ffn_grads_ready, mix_grads_ready):
    batch, seq, d = x.shape
    t = batch * seq
    cw = d // 2
    fh = cfw.shape[1] // 2
    tc = LANES
    nct = cw // tc
    tq = min(512, seq)
    pc_w, qkv_w, gl_w = 3 * cw, 3 * ATTN_WIDTH, 2 * d
    qkv_off, gl_off, f_off = pc_w, pc_w + qkv_w, pc_w + qkv_w + gl_w
    width = f_off + F_PAD
    f_col = pc_w + qkv_w

    w_pc = w_int[:pc_w].reshape(3, nct, tc, d).transpose(1, 0, 2, 3).reshape(pc_w, d)
    w_qkv = w_int[pc_w:f_col].reshape(3, HEAD_PAIRS, LANES, d).transpose(1, 0, 2, 3).reshape(qkv_w, d)
    w_f = jnp.pad(w_int[f_col:f_col + HEADS], ((0, F_PAD - HEADS), (0, 0)))
    w_inp = jnp.concatenate([w_pc, w_qkv, w_int[f_col + HEADS:], w_f], axis=0)
    bf_pad = jnp.pad(b_f, ((0, 0), (0, F_PAD - HEADS)))

    x2d = x.reshape(t, d)
    tgt2d = target.reshape(t, d)

    h1 = _rms_fwd("norm_mix", x2d, g1)
    pc, qkv, gl, fl = _project("proj_in", h1, w_inp, [(pc_w, BF16), (qkv_w, BF16), (gl_w, BF16), (F_PAD, F32)])
    a_c = _conv_fwd("conv_mix", pc, cmw, batch, seq, tc)
    f_cum = _forget_fwd("forget_cumsum", fl, bf_pad, batch, seq)
    frow = f_cum.reshape(batch, HEAD_PAIRS, 2, seq)
    o, lse = _attn_fwd("attn_fwd", qkv, frow, batch, seq, tq)
    w_oc, w_oa, w_o, w_up, w_down = late_weights(o)
    ycat = _mm("out_conv", a_c, w_oc, "nn", BF16, m=t, n=d, k=cw, o_off=0, o_width=2 * d)
    ycat = _mm("out_attn", o, w_oa, "nn", BF16, m=t, n=d, k=ATTN_WIDTH, out=ycat, o_off=d)
    mg = _merge_fwd("gate_merge", ycat, gl, b_gate)
    x2 = _mm("mix_out", mg, w_o, "nn", F32, m=t, n=d, k=d, add=x2d)
    h2 = _rms_fwd("norm_ffn", x2, g2)
    tcf = min(2 * LANES, fh)
    w_up2 = _cat_cols(w_up)
    hmid, ua, ub, ffn_a, ffn_b = _ffn_up_act("ffn_up_act", h2, w_up2, cfw, batch, seq, tcf)
    x3 = _mm("ffn_down", hmid, w_down, "nn", F32, m=t, n=d, k=fh, add=x2, tk=4096)

    dx3, dx3b, loss_row, d_gf = _final_loss("final_loss", x3, gf.reshape(1, d), tgt2d)
    dw_down = _mm("dw_down", hmid, dx3b, "tn", F32, m=fh, n=d, k=t, tm=256, tk=8192)
    du_a, du_b, d_cfw = _ffn_bwd("d_ffn", dx3b, w_down, ua, ub, ffn_a, ffn_b, cfw, batch, seq, tcf)
    ws = w_up.shape[2]
    dh2 = _mm("d_norm_ffn_a", du_a, w_up2, "nt", BF16, m=t, n=d, k=fh, b_off=0, tk=4096)
    dh2 = _mm("d_norm_ffn_b", du_b, w_up2, "nt", BF16, m=t, n=d, k=fh, b_off=fh, add=dh2, tk=4096)
    dw_up = _mm("dw_up_a", h2, du_a, "tn", F32, m=d, n=fh, k=t, tn=ws, tk=2048, o3=N_CHIPS)
    dw_up = _mm("dw_up_b", h2, du_b, "tn", F32, m=d, n=fh, k=t, tn=ws, tk=2048, o3=N_CHIPS, out=dw_up, o_off=fh)
    token = ffn_grads_ready(dw_up, dw_down)
    if token is not None:
        g2 = g2 + token[0:1, 0:1]
    dx2, d_g2 = _rms_bwd("d_norm_ffn", x2, dh2, g2, dx3)
    dm = _mm("d_merge", dx2, w_o, "nt", BF16, m=t, n=d, k=d)
    dw_o = _mm("dw_o", mg, dx2, "tn", F32, m=d, n=d, k=t, tk=2048)
    dproj, dycat, d_bg = _merge_bwd("d_gate_merge", dm, ycat, gl, b_gate, width, gl_off)
    da_c = _mm("d_conv_out", dycat, w_oc, "nt", BF16, m=t, n=cw, k=d, a_off=0)
    do = _mm("d_attn_out", dycat, w_oa, "nt", BF16, m=t, n=ATTN_WIDTH, k=d, a_off=d)
    dw_oc = _mm("dw_out_conv", a_c, dycat, "tn", F32, m=cw, n=d, k=t, b_off=0, tk=2048)
    dw_oa = _mm("dw_out_attn", o, dycat, "tn", F32, m=ATTN_WIDTH, n=d, k=t, b_off=d, tk=2048)
    dproj, d_cmw = _conv_bwd("d_conv_mix", da_c, pc, cmw, dproj, batch, seq, tc)
    dproj, d_fkey, d_fquery = _attn_bwd("attn_bwd", qkv, do, o, lse, frow, dproj, qkv_off, batch, seq, tq)
    d_fquery = jnp.pad(jnp.transpose(d_fquery, (1, 0, 2)).reshape(t, HEADS), ((0, 0), (0, LANES - HEADS)))
    dproj, d_bf = _forget_bwd("d_forget", d_fkey.reshape(batch, HEADS, seq), d_fquery, fl, bf_pad, dproj, f_off,
                              batch, seq)
    dw_inp = _mm("dw_in", dproj, h1, "tn", F32, m=width, n=d, k=t, tm=256, tk=8192)
    d_pc = dw_inp[:pc_w].reshape(nct, 3, tc, d).transpose(1, 0, 2, 3).reshape(pc_w, d)
    d_qkv = dw_inp[qkv_off:gl_off].reshape(HEAD_PAIRS, 3, LANES, d).transpose(1, 0, 2, 3).reshape(qkv_w, d)
    dw_int = jnp.concatenate([d_pc, d_qkv, dw_inp[f_off:f_off + HEADS], dw_inp[gl_off:f_off]], axis=0)
    token = mix_grads_ready(dw_int, dw_oc, dw_oa, dw_o)
    dh1 = _mm("d_norm_mix", dproj, w_inp, "nn", BF16, m=t, n=d, k=width, tm=512, tk=8192, dep=token)
    grad_x, d_g1 = _rms_bwd("d_norm_mix_x", x2d, dh1, g1, dx2)
    smalls = (d_g1, d_g2, d_gf, d_bg, d_bf, d_cmw, d_cfw)
    return loss_row[0, 0], grad_x.reshape(batch, seq, d), smalls


def _pack_small(parts):
    flat = [p.reshape(-1) for p in parts]
    sizes = [f.shape[0] for f in flat]
    total = sum(sizes)
    padded = -(-total // (8 * LANES)) * (8 * LANES)
    vec = jnp.concatenate(flat + [jnp.zeros((padded - total,), F32)])
    offsets = [sum(sizes[:i]) for i in range(len(sizes))]
    return vec.reshape(padded // LANES, LANES), offsets


def kernel(x, norm_mix_g, w_in, b_f, b_gate, conv_mix_w, w_out_conv, w_out_attn, w_o, norm_ffn_g, w_up, conv_ffn_w, w_down, norm_f_g, loss_target, m_norm_mix_g, m_w_in, m_b_f, m_b_gate, m_conv_mix_w, m_w_out_conv, m_w_out_attn, m_w_o, m_norm_ffn_g, m_w_up, m_conv_ffn_w, m_w_down, m_norm_f_g, v_norm_mix_g, v_w_in, v_b_f, v_b_gate, v_conv_mix_w, v_w_out_conv, v_w_out_attn, v_w_o, v_norm_ffn_g, v_w_up, v_conv_ffn_w, v_w_down, v_norm_f_g):
    d = x.shape[-1]
    chip = 2 * lax.axis_index("x") + lax.axis_index("y")
    xi, yi = lax.axis_index("x"), lax.axis_index("y")
    peers = [2 * px + py for px, py in ((1 - xi, yi), (xi, 1 - yi), (1 - xi, 1 - yi))]
    place = jnp.stack([chip, lax.axis_index("c"), *peers]).astype(jnp.int32)

    t_in, t_m_in, t_v_in = (jnp.transpose(w[0]) for w in (w_in, m_w_in, v_w_in))

    def row_shards(a):
        return a.reshape(N_CHIPS, a.shape[0] // N_CHIPS, a.shape[1])

    def stacked(a):
        return a.reshape(N_CHIPS * a.shape[1], a.shape[2])

    a_in, a_cmw, a_cfw = _gather_weights([t_in.astype(BF16)], (1,), [conv_mix_w[0], conv_ffn_w[0]])
    late = [w[0].astype(BF16) for w in (w_out_conv, w_out_attn, w_o, w_up, w_down)]
    late_handles, late_token = _chips_start("gather_late_start", "gather", late)

    def late_weights(after):
        lands = _chips_wait("gather_late_wait", "gather", late_handles, after)
        a_oc, a_oa, a_o, a_up, a_down = (
            lax.dynamic_update_index_in_dim(buf, own, chip, 0) for buf, own in zip(lands, late))
        return _cat_cols(a_oc), _cat_cols(a_oa), stacked(a_o), a_up, stacked(a_down)

    pending = []

    def reduce_start(tag, names, grads, axes):
        got = _exchange_sibling("exchange_sibling_" + tag, grads, axes)
        sums = [_pair_sum("pair_sum_" + nm, place, g, r, ax) for nm, g, r, ax in zip(names, grads, got, axes)]
        handles, token = _chips_start("exchange_chips_start_" + tag, "reduce", sums)
        pending.append((tag, names, grads, axes, got, handles))
        return token

    def ffn_grads_ready(dw_up, dw_down):
        return reduce_start("ffn", ("w_up", "w_down"), [dw_up, row_shards(dw_down)], (0, 0))

    def mix_grads_ready(dw_int, dw_oc, dw_oa, dw_o):
        return reduce_start("mix", ("w_in", "w_out_conv", "w_out_attn", "w_o"),
                            [row_shards(dw_int), _split_cols(dw_oc), _split_cols(dw_oa), row_shards(dw_o)],
                            (1, 0, 0, 0))

    loss_local, grad_x, smalls = _local_step(
        x, loss_target, stacked(a_in), late_weights, _cat_cols(a_cmw),
        _cat_cols(a_cfw), norm_mix_g + late_token[0:1, 0:1], b_f, b_gate, norm_ffn_g, norm_f_g,
        ffn_grads_ready, mix_grads_ready)

    reduced = {}
    for tag, names, grads, axes, got, handles in pending:
        arrivals = _chips_wait("exchange_chips_wait_" + tag, "reduce", handles, grad_x)
        halves = [_chip_sum("chip_sum_" + nm, place, g, r, arr, ax)
                  for nm, g, r, arr, ax in zip(names, grads, got, arrivals, axes)]
        reduced.update(zip(names, _share_sibling("share_sibling_" + tag, halves, axes)))
    g_in, g_oc, g_oa, g_o, g_up, g_down = (
        reduced[nm] for nm in ("w_in", "w_out_conv", "w_out_attn", "w_o", "w_up", "w_down"))

    smalls = (*smalls, loss_local.reshape(1, 1))
    packed, offs = _pack_small(smalls)
    total = _device_sum("device_sum", _gather_small(packed)).reshape(-1)
    shapes = [s.shape for s in smalls]
    d_g1, d_g2, d_gf, d_bg, d_bf, d_cmw, d_cfw, loss = [
        total[o:o + math.prod(sh)].reshape(sh) for o, sh in zip(offs, shapes)]
    loss = loss[0, 0]
    d_bf = d_bf[:, :HEADS]
    cw_s, cf_s = conv_mix_w.shape[2], conv_ffn_w.shape[2]
    d_cmw = lax.dynamic_slice(d_cmw, (0, chip * cw_s), (3, cw_s))
    d_cfw = lax.dynamic_slice(d_cfw, (0, chip * cf_s), (3, cf_s))

    order = [
        ("norm_mix_g", norm_mix_g[0:1], d_g1, m_norm_mix_g, v_norm_mix_g),
        ("w_in", t_in, g_in, t_m_in, t_v_in),
        ("b_f", b_f, d_bf, m_b_f, v_b_f),
        ("b_gate", b_gate, d_bg, m_b_gate, v_b_gate),
        ("conv_mix_w", conv_mix_w[0], d_cmw, m_conv_mix_w[0], v_conv_mix_w[0]),
        ("w_out_conv", w_out_conv[0], g_oc, m_w_out_conv[0], v_w_out_conv[0]),
        ("w_out_attn", w_out_attn[0], g_oa, m_w_out_attn[0], v_w_out_attn[0]),
        ("w_o", w_o[0], g_o, m_w_o[0], v_w_o[0]),
        ("norm_ffn_g", norm_ffn_g, d_g2, m_norm_ffn_g, v_norm_ffn_g),
        ("w_up", w_up[0], g_up, m_w_up[0], v_w_up[0]),
        ("conv_ffn_w", conv_ffn_w[0], d_cfw, m_conv_ffn_w[0], v_conv_ffn_w[0]),
        ("w_down", w_down[0], g_down, m_w_down[0], v_w_down[0]),
        ("norm_f_g", norm_f_g.reshape(1, d), d_gf, m_norm_f_g.reshape(1, d), v_norm_f_g.reshape(1, d)),
    ]
    out_shapes = [norm_mix_g.shape, w_in.shape, b_f.shape, b_gate.shape, conv_mix_w.shape, w_out_conv.shape,
                  w_out_attn.shape, w_o.shape, norm_ffn_g.shape, w_up.shape, conv_ffn_w.shape, w_down.shape,
                  norm_f_g.shape]
    g_out, d_out, m_out, v_out = [], [], [], []
    for (nm, w, g, m, v), sh in zip(order, out_shapes):
        g = g.reshape(w.shape)
        delta, new_m, new_v = _adamw("adamw_" + nm, w, g, m.reshape(w.shape), v.reshape(w.shape))
        for dst, val in ((g_out, g), (d_out, delta), (m_out, new_m), (v_out, new_v)):
            dst.append((jnp.transpose(val) if nm == "w_in" else val).reshape(sh))
    return (loss, grad_x, *g_out, *d_out, *m_out, *v_out)
```

```python
import math

import jax
import jax.numpy as jnp
from jax import lax
from jax.experimental import pallas as pl
from jax.experimental.pallas import tpu as pltpu

F32 = jnp.float32
BF16 = jnp.bfloat16
MESH = pl.DeviceIdType.MESH

EPS = 1e-6
HEADS = 8
HEAD_DIM = 64
ATTN_WIDTH = HEADS * HEAD_DIM
HEAD_PAIRS = HEADS // 2
LANES = 128
F_PAD = 2 * LANES
NEG_BIG = -1e30
N_CHIPS = 4
N_DEV = 8

ADAM_LR = 0.001
ADAM_B1 = 0.9
ADAM_B2 = 0.999
ADAM_EPS = 1e-08
ADAM_WD = 0.01
ADAM_STEP = 10

_DIMS = {
    "nn": (((1,), (0,)), ((), ())),
    "nt": (((1,), (1,)), ((), ())),
    "tn": (((0,), (0,)), ((), ())),
}


def _tile(n, target, mult, also=()):
    best = None
    for t in range(mult, n + 1, mult):
        if n % t == 0 and t <= target and all(o % t == 0 for o in also):
            best = t
    if best is None:
        assert all(o == 0 for o in also), (n, target, mult, also)
        return n
    return best


def _sds(shape, dtype):
    return jax.ShapeDtypeStruct(shape, dtype)


def _mm(name, a, b, mode, out_dtype, *, m, n, k, a_off=0, b_off=0, out=None, o_off=0,
        o_width=None, o3=None, add=None, dep=None, tm=1024, tn=2048, tk=2048):
    if mode == "nn":
        tm = _tile(m, tm, 16)
        tk = _tile(k, tk, LANES, (a_off,))
        tn = _tile(n, tn, LANES, (b_off, o_off))
        a_spec = pl.BlockSpec((tm, tk), lambda i, j, kk: (i, a_off // tk + kk))
        b_spec = pl.BlockSpec((tk, tn), lambda i, j, kk: (kk, b_off // tn + j))
    elif mode == "nt":
        tm = _tile(m, tm, 16)
        tk = _tile(k, tk, LANES, (a_off, b_off))
        tn = _tile(n, tn, LANES, (o_off,))
        a_spec = pl.BlockSpec((tm, tk), lambda i, j, kk: (i, a_off // tk + kk))
        b_spec = pl.BlockSpec((tn, tk), lambda i, j, kk: (j, b_off // tk + kk))
    else:
        tm = _tile(m, tm, LANES, (a_off,))
        tk = _tile(k, tk, 16)
        tn = _tile(n, tn, LANES, (b_off, o_off))
        a_spec = pl.BlockSpec((tk, tm), lambda i, j, kk: (kk, a_off // tm + i))
        b_spec = pl.BlockSpec((tk, tn), lambda i, j, kk: (kk, b_off // tn + j))
    assert m % tm == 0 and n % tn == 0 and k % tk == 0, (name, tm, tn, tk)
    nk = k // tk
    if o3 is not None:
        o_spec = pl.BlockSpec((None, tm, tn), lambda i, j, kk: (o_off // tn + j, i, 0))
        out_sds = _sds((o3, m, tn), out_dtype)
    else:
        o_spec = pl.BlockSpec((tm, tn), lambda i, j, kk: (i, o_off // tn + j))
        width = o_width if o_width is not None else (out.shape[1] if out is not None else n)
        out_sds = _sds((m, width), out_dtype)
    use_acc = nk > 1 and out_dtype != F32
    dims = _DIMS[mode]
    has_add, has_out = add is not None, out is not None

    def body(*refs):
        a_ref, b_ref = refs[0], refs[1]
        pos = 2
        add_ref = None
        if has_add:
            add_ref = refs[pos]
            pos += 1
        if has_out:
            pos += 1
        if dep is not None:
            pos += 1
        o_ref = refs[pos]
        acc_ref = refs[pos + 1] if use_acc else None
        part = lax.dot_general(a_ref[...].astype(BF16), b_ref[...].astype(BF16), dims,
                               preferred_element_type=F32)
        if nk == 1:
            if has_add:
                part = part + add_ref[...]
            o_ref[...] = part.astype(o_ref.dtype)
            return
        kk = pl.program_id(2)
        tgt = acc_ref if use_acc else o_ref

        @pl.when(kk == 0)
        def _():
            tgt[...] = part + add_ref[...] if has_add else part

        @pl.when(kk > 0)
        def _():
            tgt[...] += part

        if use_acc:
            @pl.when(kk == nk - 1)
            def _():
                o_ref[...] = acc_ref[...].astype(o_ref.dtype)

    operands, in_specs = [a, b], [a_spec, b_spec]
    if has_add:
        operands.append(add)
        in_specs.append(pl.BlockSpec((tm, tn), lambda i, j, kk: (i, j)))
    aliases = {}
    if has_out:
        aliases = {len(operands): 0}
        operands.append(out)
        in_specs.append(pl.BlockSpec(memory_space=pl.ANY))
    if dep is not None:
        operands.append(dep)
        in_specs.append(pl.BlockSpec(memory_space=pl.ANY))
    return pl.pallas_call(
        body,
        out_shape=out_sds,
        grid=(m // tm, n // tn, nk),
        in_specs=in_specs,
        out_specs=o_spec,
        scratch_shapes=[pltpu.VMEM((tm, tn), F32)] if use_acc else [],
        input_output_aliases=aliases,
        compiler_params=pltpu.CompilerParams(dimension_semantics=("parallel", "parallel", "arbitrary")),
        name=name,
    )(*operands)


def _project(name, h, w_t, groups):
    t, d = h.shape
    tm = _tile(t, 512, 16)
    offs = [sum(n for n, _ in groups[:i]) for i in range(len(groups))]

    def body(h_ref, w_ref, *o_refs):
        hv = h_ref[...]
        for (n, _), off, o_ref in zip(groups, offs, o_refs):
            o_ref[...] = _dot(hv, w_ref[off:off + n, :], "nt").astype(o_ref.dtype)

    return pl.pallas_call(
        body,
        out_shape=tuple(_sds((t, n), dt) for n, dt in groups),
        grid=(t // tm,),
        in_specs=[pl.BlockSpec((tm, d), lambda i: (i, 0)), pl.BlockSpec(w_t.shape, lambda i: (0, 0))],
        out_specs=tuple(pl.BlockSpec((tm, n), lambda i: (i, 0)) for n, _ in groups),
        compiler_params=pltpu.CompilerParams(dimension_semantics=("parallel",)),
        name=name,
    )(h, w_t)


def _rms_fwd(name, x, g):
    t, d = x.shape
    tm = _tile(t, 512, 16)

    def body(x_ref, g_ref, o_ref):
        xv = x_ref[...]
        r = lax.rsqrt(jnp.mean(xv * xv, axis=-1, keepdims=True) + EPS)
        o_ref[...] = ((xv * r) * g_ref[...]).astype(o_ref.dtype)

    return pl.pallas_call(
        body,
        out_shape=_sds((t, d), BF16),
        grid=(t // tm,),
        in_specs=[pl.BlockSpec((tm, d), lambda i: (i, 0)), pl.BlockSpec((1, d), lambda i: (0, 0))],
        out_specs=pl.BlockSpec((tm, d), lambda i: (i, 0)),
        compiler_params=pltpu.CompilerParams(dimension_semantics=("parallel",)),
        name=name,
    )(x, g)


def _rms_bwd(name, x, dh, g, res):
    t, d = x.shape
    tm = _tile(t, 512, 16)

    def body(x_ref, dh_ref, g_ref, res_ref, dx_ref, dg_ref):
        xv = x_ref[...]
        r = lax.rsqrt(jnp.mean(xv * xv, axis=-1, keepdims=True) + EPS)
        xh = xv * r
        dhv = dh_ref[...].astype(F32)
        dxh = dhv * g_ref[...]
        dx_ref[...] = res_ref[...] + r * (dxh - xh * jnp.mean(dxh * xh, axis=-1, keepdims=True))

        @pl.when(pl.program_id(0) == 0)
        def _():
            dg_ref[...] = jnp.zeros_like(dg_ref)

        dg_ref[...] += jnp.sum(dhv * xh, axis=0, keepdims=True)

    row = pl.BlockSpec((tm, d), lambda i: (i, 0))
    vec = pl.BlockSpec((1, d), lambda i: (0, 0))
    return pl.pallas_call(
        body,
        out_shape=(_sds((t, d), F32), _sds((1, d), F32)),
        grid=(t // tm,),
        in_specs=[row, row, vec, row],
        out_specs=(row, vec),
        compiler_params=pltpu.CompilerParams(dimension_semantics=("arbitrary",)),
        name=name,
    )(x, dh, g, res)


def _final_loss(name, x, g, target):
    t, d = x.shape
    tm = _tile(t, 512, 16)

    def body(x_ref, g_ref, t_ref, dx_ref, dxb_ref, loss_ref, dg_ref):
        xv = x_ref[...]
        gv = g_ref[...]
        r = lax.rsqrt(jnp.mean(xv * xv, axis=-1, keepdims=True) + EPS)
        xh = xv * r
        err = xh * gv - t_ref[...]
        dy = err * (1.0 / d)
        dxh = dy * gv
        dx = r * (dxh - xh * jnp.mean(dxh * xh, axis=-1, keepdims=True))
        dx_ref[...] = dx
        dxb_ref[...] = dx.astype(dxb_ref.dtype)
        per_row = jnp.sum(err * err, axis=-1, keepdims=True) * (0.5 / d)

        @pl.when(pl.program_id(0) == 0)
        def _():
            dg_ref[...] = jnp.zeros_like(dg_ref)
            loss_ref[...] = jnp.zeros_like(loss_ref)

        dg_ref[...] += jnp.sum(dy * xh, axis=0, keepdims=True)
        loss_ref[...] += jnp.sum(per_row, axis=0, keepdims=True)

    row = pl.BlockSpec((tm, d), lambda i: (i, 0))
    vec = pl.BlockSpec((1, d), lambda i: (0, 0))
    return pl.pallas_call(
        body,
        out_shape=(_sds((t, d), F32), _sds((t, d), BF16), _sds((1, LANES), F32), _sds((1, d), F32)),
        grid=(t // tm,),
        in_specs=[row, vec, row],
        out_specs=(row, row, pl.BlockSpec((1, LANES), lambda i: (0, 0)), vec),
        compiler_params=pltpu.CompilerParams(dimension_semantics=("arbitrary",)),
        name=name,
    )(x, g, target)


def _shift_down(z, k):
    row = lax.broadcasted_iota(jnp.int32, z.shape, 0)
    return jnp.where(row >= k, pltpu.roll(z, k, axis=0), 0.0)


def _shift_up(z, k):
    s = z.shape[0]
    row = lax.broadcasted_iota(jnp.int32, z.shape, 0)
    return jnp.where(row < s - k, pltpu.roll(z, s - k, axis=0), 0.0)


def _conv3(z, w):
    return (w[2:3] * z + w[0:1] * _shift_down(z, 2)) + w[1:2] * _shift_down(z, 1)


def _conv3_t(dz, w):
    return (w[2:3] * dz + w[0:1] * _shift_up(dz, 2)) + w[1:2] * _shift_up(dz, 1)


def _conv_fwd(name, pc, w, batch, seq, tc):
    cw = w.shape[1]
    nct = cw // tc

    def body(pc_ref, w_ref, o_ref):
        cb = pc_ref[:, 0:tc].astype(F32)
        z = pc_ref[:, tc:2 * tc].astype(F32) * pc_ref[:, 2 * tc:3 * tc].astype(F32)
        o_ref[...] = (cb * _conv3(z, w_ref[...])).astype(o_ref.dtype)

    return pl.pallas_call(
        body,
        out_shape=_sds((batch * seq, cw), BF16),
        grid=(batch, nct),
        in_specs=[pl.BlockSpec((seq, 3 * tc), lambda b, j: (b, j)), pl.BlockSpec((3, tc), lambda b, j: (0, j))],
        out_specs=pl.BlockSpec((seq, tc), lambda b, j: (b, j)),
        compiler_params=pltpu.CompilerParams(dimension_semantics=("parallel", "parallel")),
        name=name,
    )(pc, w)


def _conv_bwd(name, da, pc, w, dproj, batch, seq, tc):
    cw = w.shape[1]
    nct = cw // tc

    def body(da_ref, pc_ref, w_ref, _, dpc_ref, dw_ref):
        wv = w_ref[...]
        cb = pc_ref[:, 0:tc].astype(F32)
        cc = pc_ref[:, tc:2 * tc].astype(F32)
        cin = pc_ref[:, 2 * tc:3 * tc].astype(F32)
        z = cc * cin
        dav = da_ref[...].astype(F32)
        du = dav * cb
        dz = _conv3_t(du, wv)
        dpc_ref[:, 0:tc] = (dav * _conv3(z, wv)).astype(dpc_ref.dtype)
        dpc_ref[:, tc:2 * tc] = (dz * cin).astype(dpc_ref.dtype)
        dpc_ref[:, 2 * tc:3 * tc] = (dz * cc).astype(dpc_ref.dtype)

        @pl.when(pl.program_id(1) == 0)
        def _():
            dw_ref[...] = jnp.zeros_like(dw_ref)

        dw_ref[0:1, :] += jnp.sum(du * _shift_down(z, 2), axis=0, keepdims=True)
        dw_ref[1:2, :] += jnp.sum(du * _shift_down(z, 1), axis=0, keepdims=True)
        dw_ref[2:3, :] += jnp.sum(du * z, axis=0, keepdims=True)

    return pl.pallas_call(
        body,
        out_shape=(_sds(dproj.shape, dproj.dtype), _sds((3, cw), F32)),
        grid=(nct, batch),
        in_specs=[
            pl.BlockSpec((seq, tc), lambda j, b: (b, j)),
            pl.BlockSpec((seq, 3 * tc), lambda j, b: (b, j)),
            pl.BlockSpec((3, tc), lambda j, b: (0, j)),
            pl.BlockSpec(memory_space=pl.ANY),
        ],
        out_specs=(pl.BlockSpec((seq, 3 * tc), lambda j, b: (b, j)), pl.BlockSpec((3, tc), lambda j, b: (0, j))),
        input_output_aliases={3: 0},
        compiler_params=pltpu.CompilerParams(dimension_semantics=("parallel", "arbitrary")),
        name=name,
    )(da, pc, w, dproj)


def _ffn_up_act(name, h2, w_up, w, batch, seq, tc):
    d = h2.shape[1]
    fh = w.shape[1] // 2
    nf = fh // tc

    def body(h_ref, ma_ref, mb_ref, wa_ref, wb_ref, o_ref, ua_ref, ub_ref, a_ref, b_ref):
        hv = h_ref[...]
        ua = _dot(hv, ma_ref[...], "nn")
        ub = _dot(hv, mb_ref[...], "nn")
        ua_ref[...] = ua.astype(ua_ref.dtype)
        ub_ref[...] = ub.astype(ub_ref.dtype)
        a = _conv3(ua, wa_ref[...])
        b = _conv3(ub, wb_ref[...])
        a_ref[...] = a.astype(a_ref.dtype)
        b_ref[...] = b.astype(b_ref.dtype)
        o_ref[...] = (a * jax.nn.sigmoid(a) * b).astype(o_ref.dtype)

    act = pl.BlockSpec((seq, tc), lambda b, j: (b, j))
    shape = _sds((batch * seq, fh), BF16)
    return pl.pallas_call(
        body,
        out_shape=(shape,) * 5,
        grid=(batch, nf),
        in_specs=[
            pl.BlockSpec((seq, d), lambda b, j: (b, 0)),
            pl.BlockSpec((d, tc), lambda b, j: (0, j)),
            pl.BlockSpec((d, tc), lambda b, j: (0, nf + j)),
            pl.BlockSpec((3, tc), lambda b, j: (0, j)),
            pl.BlockSpec((3, tc), lambda b, j: (0, nf + j)),
        ],
        out_specs=(act,) * 5,
        compiler_params=pltpu.CompilerParams(dimension_semantics=("parallel", "parallel")),
        name=name,
    )(h2, w_up, w_up, w, w)


def _ffn_bwd(name, dx, w_down, ua, ub, av, bv, w, batch, seq, tc):
    d = dx.shape[1]
    fh = w.shape[1] // 2
    nf = fh // tc

    rb = _tile(seq, 128, 8)
    halo = 8

    def body(dx_ref, md_ref, ua_ref, ub_ref, a_ref, b_ref, wa_ref, wb_ref, dua_ref, dub_ref, dw_ref,
             dh_scr, da_scr, db_scr):
        j = pl.program_id(1)
        dh_scr[...] = _dot(dx_ref[...].astype(BF16), md_ref[...], "nt")
        da_scr[seq:seq + halo, :] = jnp.zeros((halo, tc), F32)
        db_scr[seq:seq + halo, :] = jnp.zeros((halo, tc), F32)

        def silu_bwd(r, carry):
            rows = pl.ds(pl.multiple_of(r * rb, rb), rb)
            a, b, dhv = a_ref[rows, :].astype(F32), b_ref[rows, :].astype(F32), dh_scr[rows, :]
            sg = jax.nn.sigmoid(a)
            da_scr[rows, :] = dhv * b * (sg * (1.0 + a * (1.0 - sg)))
            db_scr[rows, :] = dhv * (a * sg)
            return carry

        lax.fori_loop(0, seq // rb, silu_bwd, 0)
        wa, wb = wa_ref[...], wb_ref[...]

        def conv_bwd(r, sums):
            r0 = pl.multiple_of(r * rb, rb)
            rows = pl.ds(r0, rb)
            out = []
            for d_scr, u_ref, wv, du_ref, acc in ((da_scr, ua_ref, wa, dua_ref, sums[0:3]),
                                                  (db_scr, ub_ref, wb, dub_ref, sums[3:6])):
                x = d_scr[pl.ds(r0, rb + halo), :]
                dv = x[0:rb]
                up1 = pltpu.roll(x, rb + halo - 1, axis=0)[0:rb]
                up2 = pltpu.roll(x, rb + halo - 2, axis=0)[0:rb]
                du_ref[rows, :] = ((wv[2:3] * dv + wv[0:1] * up2) + wv[1:2] * up1).astype(du_ref.dtype)
                uv = u_ref[rows, :].astype(F32)
                out += [acc[0] + jnp.sum(up2 * uv, axis=0, keepdims=True),
                        acc[1] + jnp.sum(up1 * uv, axis=0, keepdims=True),
                        acc[2] + jnp.sum(dv * uv, axis=0, keepdims=True)]
            return tuple(out)

        sums = lax.fori_loop(0, seq // rb, conv_bwd, (jnp.zeros((1, tc), F32),) * 6)

        @pl.when((pl.program_id(0) == 0) & (j == 0))
        def _():
            dw_ref[...] = jnp.zeros_like(dw_ref)

        for half, off in enumerate((0, fh)):
            cols = pl.ds(pl.multiple_of(off + j * tc, LANES), tc)
            for k in range(3):
                dw_ref[k:k + 1, cols] += sums[3 * half + k]

    act = pl.BlockSpec((seq, tc), lambda b, j: (b, j))
    shape = _sds((batch * seq, fh), BF16)
    return pl.pallas_call(
        body,
        out_shape=(shape, shape, _sds((3, 2 * fh), F32)),
        grid=(batch, nf),
        in_specs=[
            pl.BlockSpec((seq, d), lambda b, j: (b, 0)),
            pl.BlockSpec((tc, d), lambda b, j: (j, 0)),
            act,
            act,
            act,
            act,
            pl.BlockSpec((3, tc), lambda b, j: (0, j)),
            pl.BlockSpec((3, tc), lambda b, j: (0, nf + j)),
        ],
        out_specs=(act, act, pl.BlockSpec((3, 2 * fh), lambda b, j: (0, 0))),
        scratch_shapes=[pltpu.VMEM((seq, tc), F32), pltpu.VMEM((seq + halo, tc), F32),
                        pltpu.VMEM((seq + halo, tc), F32)],
        compiler_params=pltpu.CompilerParams(dimension_semantics=("arbitrary", "arbitrary")),
        name=name,
    )(dx, w_down, ua, ub, av, bv, w, w)


def _merge_fwd(name, ycat, gl, bg):
    t, d2 = ycat.shape
    d = d2 // 2
    tm = _tile(t, 1024, 16)

    def body(y_ref, gl_ref, bg_ref, o_ref):
        g = jax.nn.sigmoid(gl_ref[...].astype(F32) + bg_ref[...])
        prod = g * y_ref[...].astype(F32)
        o_ref[...] = (prod[:, 0:d] + prod[:, d:d2]).astype(o_ref.dtype)

    row = pl.BlockSpec((tm, d2), lambda i: (i, 0))
    return pl.pallas_call(
        body,
        out_shape=_sds((t, d), BF16),
        grid=(t // tm,),
        in_specs=[row, row, pl.BlockSpec((1, d2), lambda i: (0, 0))],
        out_specs=pl.BlockSpec((tm, d), lambda i: (i, 0)),
        compiler_params=pltpu.CompilerParams(dimension_semantics=("parallel",)),
        name=name,
    )(ycat, gl, bg)


def _merge_bwd(name, dm, ycat, gl, bg, width, gl_off):
    t, d2 = ycat.shape
    d = d2 // 2
    tm = _tile(t, 1024, 16)
    wb = math.gcd(gl_off, d)
    nw = d // wb

    def body(dm_ref, y_ref, gl_ref, bg_ref, dgl_ref, dy_ref, dbg_ref):
        g = jax.nn.sigmoid(gl_ref[...].astype(F32) + bg_ref[...])
        dmv = dm_ref[...].astype(F32)
        dgl = dmv * y_ref[...].astype(F32) * (g * (1.0 - g))
        dgl_ref[...] = dgl.astype(dgl_ref.dtype)
        dy_ref[...] = (dmv * g).astype(dy_ref.dtype)

        @pl.when(pl.program_id(2) == 0)
        def _():
            dbg_ref[...] = jnp.zeros_like(dbg_ref)

        dbg_ref[...] += jnp.sum(dgl, axis=0, keepdims=True)

    half = pl.BlockSpec((tm, wb), lambda h, j, i: (i, h * nw + j))
    vec = pl.BlockSpec((1, wb), lambda h, j, i: (0, h * nw + j))
    return pl.pallas_call(
        body,
        out_shape=(_sds((t, width), BF16), _sds((t, d2), BF16), _sds((1, d2), F32)),
        grid=(2, nw, t // tm),
        in_specs=[pl.BlockSpec((tm, wb), lambda h, j, i: (i, j)), half, half, vec],
        out_specs=(pl.BlockSpec((tm, wb), lambda h, j, i: (i, gl_off // wb + h * nw + j)), half, vec),
        compiler_params=pltpu.CompilerParams(dimension_semantics=("parallel", "parallel", "arbitrary")),
        name=name,
    )(dm, ycat, gl, bg)


def _log_sigmoid(z):
    return jnp.minimum(z, 0.0) - jnp.log1p(jnp.exp(-jnp.abs(z)))


def _forget_fwd(name, fl, bf, batch, seq):
    def body(fl_ref, bf_ref, o_ref):
        lf = _log_sigmoid(fl_ref[:, 0:LANES] + bf_ref[:, 0:LANES])
        acc = lf.T[0:HEADS, :]
        lane = lax.broadcasted_iota(jnp.int32, acc.shape, 1)
        k = 1
        while k < seq:
            acc = acc + jnp.where(lane >= k, pltpu.roll(acc, k, axis=1), 0.0)
            k *= 2
        o_ref[...] = acc

    return pl.pallas_call(
        body,
        out_shape=_sds((batch, HEADS, seq), F32),
        grid=(batch,),
        in_specs=[pl.BlockSpec((seq, F_PAD), lambda b: (b, 0)), pl.BlockSpec((1, F_PAD), lambda b: (0, 0))],
        out_specs=pl.BlockSpec((None, HEADS, seq), lambda b: (b, 0, 0)),
        compiler_params=pltpu.CompilerParams(dimension_semantics=("parallel",)),
        name=name,
    )(fl, bf)


def _forget_bwd(name, d_key, d_query, fl, bf, dproj, f_off, batch, seq):
    nfb = F_PAD // LANES

    def body(dk_ref, dq_ref, fl_ref, bf_ref, _, df_ref, dbf_ref):
        jj = pl.program_id(1)
        key_t = jnp.concatenate([dk_ref[...], jnp.zeros((LANES - HEADS, seq), F32)], axis=0).T
        acc = dq_ref[...] - key_t
        row = lax.broadcasted_iota(jnp.int32, acc.shape, 0)
        k = 1
        while k < seq:
            acc = acc + jnp.where(row < seq - k, pltpu.roll(acc, seq - k, axis=0), 0.0)
            k *= 2
        z = fl_ref[:, 0:LANES] + bf_ref[:, 0:LANES]
        col = lax.broadcasted_iota(jnp.int32, acc.shape, 1)
        df = jnp.where(col < HEADS, acc * jax.nn.sigmoid(-z), 0.0)
        df = jnp.where(jj == 0, df, 0.0)
        df_ref[...] = df.astype(df_ref.dtype)

        @pl.when((pl.program_id(0) == 0) & (jj == 0))
        def _():
            dbf_ref[...] = jnp.zeros_like(dbf_ref)

        dbf_ref[...] += jnp.sum(df, axis=0, keepdims=True)

    return pl.pallas_call(
        body,
        out_shape=(_sds(dproj.shape, dproj.dtype), _sds((1, LANES), F32)),
        grid=(batch, nfb),
        in_specs=[
            pl.BlockSpec((None, HEADS, seq), lambda b, j: (b, 0, 0)),
            pl.BlockSpec((seq, LANES), lambda b, j: (b, 0)),
            pl.BlockSpec((seq, F_PAD), lambda b, j: (b, 0)),
            pl.BlockSpec((1, F_PAD), lambda b, j: (0, 0)),
            pl.BlockSpec(memory_space=pl.ANY),
        ],
        out_specs=(pl.BlockSpec((seq, LANES), lambda b, j: (b, f_off // LANES + j)),
                   pl.BlockSpec((1, LANES), lambda b, j: (0, 0))),
        input_output_aliases={4: 0},
        compiler_params=pltpu.CompilerParams(dimension_semantics=("arbitrary", "arbitrary")),
        name=name,
    )(d_key, d_query, fl, bf, dproj)


def _dot(a, b, mode):
    return lax.dot_general(a, b, _DIMS[mode], preferred_element_type=F32)


def _attn_fwd(name, qkv, frow, batch, seq, tq):
    nq = seq // tq
    scale = 1.0 / math.sqrt(HEAD_DIM)

    def body(q_ref, k_ref, v_ref, f_ref, o_ref, lse_ref):
        i = pl.program_id(2)
        lane = lax.broadcasted_iota(jnp.int32, (1, LANES), 1)
        lo = lane < HEAD_DIM
        qs = q_ref[...] * scale
        qh = (jnp.where(lo, qs, 0.0).astype(BF16), jnp.where(lo, 0.0, qs).astype(BF16))
        row = lax.broadcasted_iota(jnp.int32, (tq, tq), 0)
        col = lax.broadcasted_iota(jnp.int32, (tq, tq), 1)

        def step(j, carry, diag):
            m0, l0, m1, l1, acc = carry
            start = pl.multiple_of(j * tq, tq)
            kj = k_ref[pl.ds(start, tq), :]
            vj = v_ref[pl.ds(start, tq), :]
            ms, ls, pvs, alphas = [], [], [], []
            for h, (m_old, l_old) in enumerate(((m0, l0), (m1, l1))):
                s = _dot(qh[h], kj, "nt") - f_ref[h:h + 1, pl.ds(start, tq)]
                if diag:
                    s = jnp.where(col <= row, s, NEG_BIG)
                m_new = jnp.maximum(m_old, jnp.max(s, axis=1, keepdims=True))
                p = jnp.exp(s - m_new)
                alpha = jnp.exp(m_old - m_new)
                ls.append(alpha * l_old + jnp.sum(p, axis=1, keepdims=True))
                ms.append(m_new)
                alphas.append(alpha)
                vh = jnp.where(lo, vj, 0.0) if h == 0 else jnp.where(lo, 0.0, vj)
                pvs.append(_dot(p.astype(BF16), vh.astype(BF16), "nn"))
            acc = acc * jnp.where(lo, alphas[0], alphas[1]) + (pvs[0] + pvs[1])
            return ms[0], ls[0], ms[1], ls[1], acc

        neg = jnp.full((tq, 1), NEG_BIG, F32)
        zero = jnp.zeros((tq, 1), F32)
        init = (neg, zero, neg, zero, jnp.zeros((tq, LANES), F32))
        carry = lax.fori_loop(0, i, lambda j, c: step(j, c, False), init)
        m0, l0, m1, l1, acc = step(i, carry, True)
        o_ref[...] = (acc / jnp.where(lo, l0, l1)).astype(o_ref.dtype)
        lse_ref[:, 0:1] = m0 + jnp.log(l0)
        lse_ref[:, 1:2] = m1 + jnp.log(l1)

    return pl.pallas_call(
        body,
        out_shape=(_sds((batch * seq, ATTN_WIDTH), BF16), _sds((HEAD_PAIRS, batch * seq, 2), F32)),
        grid=(batch, HEAD_PAIRS, nq),
        in_specs=[
            pl.BlockSpec((tq, LANES), lambda b, hp, i: (b * nq + i, 3 * hp)),
            pl.BlockSpec((seq, LANES), lambda b, hp, i: (b, 3 * hp + 1)),
            pl.BlockSpec((seq, LANES), lambda b, hp, i: (b, 3 * hp + 2)),
            pl.BlockSpec((None, None, 2, seq), lambda b, hp, i: (b, hp, 0, 0)),
        ],
        out_specs=(
            pl.BlockSpec((tq, LANES), lambda b, hp, i: (b * nq + i, hp)),
            pl.BlockSpec((None, tq, 2), lambda b, hp, i: (hp, b * nq + i, 0)),
        ),
        compiler_params=pltpu.CompilerParams(dimension_semantics=("parallel", "parallel", "parallel")),
        name=name,
    )(qkv, qkv, qkv, frow)


def _attn_bwd(name, qkv, do, o, lse, frow, dproj, qkv_off, batch, seq, tq):
    nq = seq // tq
    scale = 1.0 / math.sqrt(HEAD_DIM)

    def body(q_ref, k_ref, v_ref, do_ref, o_ref, lse_ref, f_ref, _, dqkv_ref, df_ref, drow_ref,
             dq_acc, dk_acc, dv_acc, df_acc):
        j = pl.program_id(2)
        lane = lax.broadcasted_iota(jnp.int32, (1, LANES), 1)
        lo = lane < HEAD_DIM
        masks = (lo, jnp.logical_not(lo))
        row = lax.broadcasted_iota(jnp.int32, (tq, tq), 0)
        col = lax.broadcasted_iota(jnp.int32, (tq, tq), 1)

        @pl.when(j == 0)
        def _():
            dq_acc[...] = jnp.zeros_like(dq_acc)
            drow_ref[...] = jnp.zeros_like(drow_ref)

        dk_acc[...] = jnp.zeros_like(dk_acc)
        dv_acc[...] = jnp.zeros_like(dv_acc)
        df_acc[...] = jnp.zeros_like(df_acc)
        kj = k_ref[...]
        vj = v_ref[...]
        kstart = pl.multiple_of(j * tq, tq)
        kh = tuple(jnp.where(mk, kj, 0.0).astype(BF16) for mk in masks)

        def step(i, diag):
            start = pl.multiple_of(i * tq, tq)
            rows = pl.ds(start, tq)
            qi = q_ref[rows, :] * scale
            doi = do_ref[rows, :]
            prod = doi.astype(F32) * o_ref[rows, :].astype(F32)
            lse_i = lse_ref[rows, :]
            dq_i = jnp.zeros((tq, LANES), F32)
            for h, mk in enumerate(masks):
                q_h = jnp.where(mk, qi, 0.0).astype(BF16)
                do_h = jnp.where(mk, doi, 0.0).astype(BF16)
                delta = jnp.sum(jnp.where(mk, prod, 0.0), axis=1, keepdims=True)
                s = _dot(q_h, kj, "nt") - f_ref[h:h + 1, pl.ds(kstart, tq)]
                p = jnp.exp(s - lse_i[:, h:h + 1])
                if diag:
                    p = jnp.where(col <= row, p, 0.0)
                ds = p * (_dot(do_h, vj, "nt") - delta)
                df_acc[h:h + 1, :] += jnp.sum(ds, axis=0, keepdims=True)
                drow_ref[rows, h:h + 1] += jnp.sum(ds, axis=1, keepdims=True)
                dsb = ds.astype(BF16)
                dv_acc[...] += _dot(p.astype(BF16), do_h, "tn")
                dk_acc[...] += _dot(dsb, q_h, "tn")
                dq_i = dq_i + _dot(dsb, kh[h], "nn")
            dq_acc[rows, :] += dq_i

        step(j, True)
        lax.fori_loop(j + 1, nq, lambda i, c: (step(i, False), c)[1], 0)
        dqkv_ref[:, 0:LANES] = (dq_acc[pl.ds(kstart, tq), :] * scale).astype(dqkv_ref.dtype)
        dqkv_ref[:, LANES:2 * LANES] = dk_acc[...].astype(dqkv_ref.dtype)
        dqkv_ref[:, 2 * LANES:3 * LANES] = dv_acc[...].astype(dqkv_ref.dtype)
        df_ref[...] = df_acc[...]

    full = lambda c: pl.BlockSpec((seq, LANES), lambda b, hp, j: (b, c(hp)))
    blk = lambda c: pl.BlockSpec((tq, LANES), lambda b, hp, j: (b * nq + j, c(hp)))
    return pl.pallas_call(
        body,
        out_shape=(_sds(dproj.shape, dproj.dtype), _sds((batch, HEAD_PAIRS, 2, seq), F32),
                   _sds((HEAD_PAIRS, batch * seq, 2), F32)),
        grid=(batch, HEAD_PAIRS, nq),
        in_specs=[
            full(lambda hp: 3 * hp),
            blk(lambda hp: 3 * hp + 1),
            blk(lambda hp: 3 * hp + 2),
            full(lambda hp: hp),
            full(lambda hp: hp),
            pl.BlockSpec((None, seq, 2), lambda b, hp, j: (hp, b, 0)),
            pl.BlockSpec((None, None, 2, seq), lambda b, hp, j: (b, hp, 0, 0)),
            pl.BlockSpec(memory_space=pl.ANY),
        ],
        out_specs=(
            pl.BlockSpec((tq, 3 * LANES), lambda b, hp, j: (b * nq + j, qkv_off // (3 * LANES) + hp)),
            pl.BlockSpec((None, None, 2, tq), lambda b, hp, j: (b, hp, 0, j)),
            pl.BlockSpec((None, seq, 2), lambda b, hp, j: (hp, b, 0)),
        ),
        scratch_shapes=[
            pltpu.VMEM((seq, LANES), F32),
            pltpu.VMEM((tq, LANES), F32),
            pltpu.VMEM((tq, LANES), F32),
            pltpu.VMEM((2, tq), F32),
        ],
        input_output_aliases={7: 0},
        compiler_params=pltpu.CompilerParams(dimension_semantics=("parallel", "parallel", "arbitrary")),
        name=name,
    )(qkv, qkv, qkv, do, o, lse, frow, dproj)


def _mesh_place():
    x, y, c = lax.axis_index("x"), lax.axis_index("y"), lax.axis_index("c")
    chips = [(1 - x, y), (x, 1 - y), (1 - x, 1 - y)]
    return x, y, c, chips


def _hbm_specs(n):
    return [pl.BlockSpec(memory_space=pl.ANY)] * n


def _half(shape2d, axis, which):
    size = shape2d[axis] // 2
    sl = pl.ds(pl.multiple_of(which * size, 16 if axis == 0 else LANES), size)
    return (sl, slice(None)) if axis == 0 else (slice(None), sl)


def _gather_weights(bigs, axes, smalls):
    nb, ns = len(bigs), len(smalls)
    arrays = list(bigs) + list(smalls)
    n = nb + ns

    def body(*refs):
        ins, outs = refs[:n], refs[n:2 * n]
        send_sems, recv_sems = refs[2 * n:]
        x, y, c, chips = _mesh_place()
        me = 2 * x + y
        sibling = (x, y, 1 - c)

        def half(a, which):
            return _half(arrays[a].shape, axes[a], which)

        def copy(a, k, src, dst, to):
            return pltpu.make_async_remote_copy(src_ref=src, dst_ref=dst, send_sem=send_sems.at[a, k],
                                                recv_sem=recv_sems.at[a, k], device_id=to, device_id_type=MESH)

        sends = []
        for a in range(n):
            for j, chip in enumerate(chips):
                if a < nb:
                    cp = copy(a, j, ins[a].at[half(a, c)], outs[a].at[(me,) + half(a, c)], (*chip, c))
                else:
                    cp = copy(a, j, ins[a], outs[a].at[me], (*chip, c))
                cp.start()
                sends.append(cp)
        for a in range(nb):
            for j, (px, py) in enumerate(chips):
                blk = outs[a].at[(2 * px + py,) + half(a, c)]
                copy(a, j, blk, blk, (px, py, c)).wait_recv()
                fwd = copy(a, 3 + j, blk, blk, sibling)
                fwd.start()
                sends.append(fwd)
        for a in range(nb, n):
            for j, (px, py) in enumerate(chips):
                blk = outs[a].at[2 * px + py]
                copy(a, j, blk, blk, (px, py, c)).wait_recv()
        for a in range(nb):
            for j, (px, py) in enumerate(chips):
                blk = outs[a].at[(2 * px + py,) + half(a, 1 - c)]
                copy(a, 3 + j, blk, blk, sibling).wait_recv()
        for cp in sends:
            cp.wait_send()

    outs = pl.pallas_call(
        body,
        out_shape=tuple(_sds((N_CHIPS,) + a.shape, a.dtype) for a in arrays),
        in_specs=_hbm_specs(n),
        out_specs=tuple(_hbm_specs(n)),
        scratch_shapes=[pltpu.SemaphoreType.DMA((n, 6)), pltpu.SemaphoreType.DMA((n, 6))],
        name="gather_weights",
    )(*arrays)
    me = 2 * lax.axis_index("x") + lax.axis_index("y")
    return tuple(lax.dynamic_update_index_in_dim(o, a, me, 0) for o, a in zip(outs, arrays))


def _gather_small(v):
    m_per, ncol = v.shape

    def body(x_ref, out_ref, send_sems, recv_sems, local_sem):
        x, y, c, chips = _mesh_place()
        me, sibling = (x, y, c), (x, y, 1 - c)

        def rows(px, py, pc):
            return out_ref.at[pl.ds((4 * px + 2 * py + pc) * m_per, m_per), :]

        def copy(k, block, to, src=None):
            return pltpu.make_async_remote_copy(src_ref=rows(*block) if src is None else src, dst_ref=rows(*block),
                                                send_sem=send_sems.at[k], recv_sem=recv_sems.at[k],
                                                device_id=to, device_id_type=MESH)

        mine = pltpu.make_async_copy(x_ref, rows(*me), local_sem)
        mine.start()
        first = [copy(0, me, sibling, src=x_ref)]
        first += [copy(1 + j, me, (*chip, c), src=x_ref) for j, chip in enumerate(chips)]
        for cp in first:
            cp.start()
        passed = [copy(4 + j, (*chip, c), sibling) for j, chip in enumerate(chips)]
        for j, chip in enumerate(chips):
            copy(1 + j, (*chip, c), me).wait_recv()
            passed[j].start()
        copy(0, sibling, me).wait_recv()
        for j, chip in enumerate(chips):
            copy(4 + j, (*chip, 1 - c), me).wait_recv()
        for cp in first + passed:
            cp.wait_send()
        mine.wait()

    return pl.pallas_call(
        body,
        out_shape=_sds((N_DEV * m_per, ncol), v.dtype),
        in_specs=[pl.BlockSpec(memory_space=pltpu.VMEM)],
        out_specs=pl.BlockSpec(memory_space=pltpu.VMEM),
        scratch_shapes=[pltpu.SemaphoreType.DMA((7,)), pltpu.SemaphoreType.DMA((7,)), pltpu.SemaphoreType.DMA],
        name="gather_small",
    )(v)


def _half_shape(shape2d, axis):
    return (shape2d[0] // 2, shape2d[1]) if axis == 0 else (shape2d[0], shape2d[1] // 2)


def _exchange_sibling(name, grads, axes):
    n = len(grads)

    def body(*refs):
        ins, outs = refs[:n], refs[n:2 * n]
        send_sems, recv_sems = refs[2 * n:]
        x, y, c, _ = _mesh_place()
        copies = []
        for a in range(n):
            src = ins[a].at[(slice(None),) + _half(grads[a].shape[1:], axes[a], 1 - c)]
            cp = pltpu.make_async_remote_copy(src_ref=src, dst_ref=outs[a], send_sem=send_sems.at[a],
                                              recv_sem=recv_sems.at[a], device_id=(x, y, 1 - c), device_id_type=MESH)
            cp.start()
            copies.append(cp)
        for cp in copies:
            cp.wait()

    return pl.pallas_call(
        body,
        out_shape=tuple(_sds((N_CHIPS,) + _half_shape(g.shape[1:], ax), g.dtype) for g, ax in zip(grads, axes)),
        in_specs=_hbm_specs(n),
        out_specs=tuple(_hbm_specs(n)),
        scratch_shapes=[pltpu.SemaphoreType.DMA((n,)), pltpu.SemaphoreType.DMA((n,))],
        name=name,
    )(*grads)


_HBM = pl.BlockSpec(memory_space=pltpu.HBM)
_SEM = pl.BlockSpec(memory_space=pltpu.SEMAPHORE)
_EFFECT = pltpu.SideEffectType.DATAFLOW_SIDE_EFFECTING


def _chip_copies(kind, srcs, lands, send_sems, recv_sems):
    x, y, c, chips = _mesh_place()
    copies = []
    for a in range(len(srcs)):
        for j, (px, py) in enumerate(chips):
            if kind == "gather":
                src, dst = srcs[a], lands[a].at[2 * x + y]
            else:
                src, dst = srcs[a].at[j], lands[a].at[j]
            copies.append(pltpu.make_async_remote_copy(src_ref=src, dst_ref=dst, send_sem=send_sems.at[3 * a + j],
                                                       recv_sem=recv_sems.at[3 * a + j], device_id=(px, py, c),
                                                       device_id_type=MESH))
    return copies


def _chips_start(name, kind, srcs):
    n = len(srcs)
    slots = N_CHIPS if kind == "gather" else 3
    lands = [lax.empty((slots,) + (s.shape if kind == "gather" else s.shape[1:]), s.dtype) for s in srcs]

    def body(*refs):
        for cp in _chip_copies(kind, refs[:n], refs[n:2 * n], refs[2 * n], refs[2 * n + 1]):
            cp.start()
        refs[-1][...] = jnp.zeros_like(refs[-1])

    outs = pl.pallas_call(
        body,
        out_shape=(pltpu.SemaphoreType.DMA((3 * n,)), pltpu.SemaphoreType.DMA((3 * n,)),
                   *[pltpu.HBM(v.shape, v.dtype) for v in (*srcs, *lands)], _sds((8, LANES), F32)),
        in_specs=[_HBM] * (2 * n),
        out_specs=(_SEM, _SEM, *[_HBM] * (2 * n), pl.BlockSpec(memory_space=pltpu.VMEM)),
        input_output_aliases={i: 2 + i for i in range(2 * n)},
        compiler_params=pltpu.CompilerParams(has_side_effects=_EFFECT),
        name=name,
    )(*[pltpu.with_memory_space_constraint(v, pltpu.HBM) for v in (*srcs, *lands)])
    return outs[:-1], outs[-1]


def _chips_wait(name, kind, handles, after):
    send_sems, recv_sems, *thru = handles
    n = len(thru) // 2

    def body(*refs):
        for cp in _chip_copies(kind, refs[:n], refs[n:2 * n], refs[2 * n], refs[2 * n + 1]):
            cp.wait_send()
            cp.wait_recv()

    outs = pl.pallas_call(
        body,
        out_shape=tuple(pltpu.HBM(v.shape, v.dtype) for v in thru),
        in_specs=[_HBM] * (2 * n) + [_SEM, _SEM, pl.BlockSpec(memory_space=pl.ANY)],
        out_specs=tuple([_HBM] * (2 * n)),
        input_output_aliases={i: i for i in range(2 * n)},
        compiler_params=pltpu.CompilerParams(has_side_effects=_EFFECT),
        name=name,
    )(*thru, send_sems, recv_sems, after)
    return outs[n:]


def _share_sibling(name, shards, axes):
    n = len(shards)

    def body(*refs):
        ins, outs = refs[:n], refs[n:2 * n]
        send_sems, recv_sems = refs[2 * n:]
        x, y, c, _ = _mesh_place()
        started = []
        for a in range(n):
            mine = _half(shards[a].shape, axes[a], c)
            theirs = _half(shards[a].shape, axes[a], 1 - c)
            cp = pltpu.make_async_remote_copy(src_ref=ins[a].at[mine], dst_ref=outs[a].at[mine],
                                              send_sem=send_sems.at[a], recv_sem=recv_sems.at[a],
                                              device_id=(x, y, 1 - c), device_id_type=MESH)
            cp.start()
            arrival = pltpu.make_async_remote_copy(src_ref=ins[a].at[theirs], dst_ref=outs[a].at[theirs],
                                                   send_sem=send_sems.at[a], recv_sem=recv_sems.at[a],
                                                   device_id=(x, y, 1 - c), device_id_type=MESH)
            started.append((cp, arrival))
        for cp, arrival in started:
            arrival.wait_recv()
            cp.wait_send()

    return pl.pallas_call(
        body,
        out_shape=tuple(_sds(s.shape, s.dtype) for s in shards),
        in_specs=_hbm_specs(n),
        out_specs=tuple(_hbm_specs(n)),
        scratch_shapes=[pltpu.SemaphoreType.DMA((n,)), pltpu.SemaphoreType.DMA((n,))],
        input_output_aliases={a: a for a in range(n)},
        name=name,
    )(*shards)


def _pair_sum(name, place, g, got, axis):
    hr, hc = got.shape[1:]

    def body(place_ref, g_ref, got_ref, o_ref):
        o_ref[...] = (g_ref[...] + got_ref[...]).astype(o_ref.dtype)

    blk = (None, hr, hc)
    mine = (lambda j, pr: (pr[2 + j], pr[1], 0)) if axis == 0 else (lambda j, pr: (pr[2 + j], 0, pr[1]))
    return pl.pallas_call(
        body,
        out_shape=_sds((N_CHIPS - 1, hr, hc), BF16),
        grid_spec=pltpu.PrefetchScalarGridSpec(
            num_scalar_prefetch=1,
            grid=(N_CHIPS - 1,),
            in_specs=[pl.BlockSpec(blk, mine), pl.BlockSpec(blk, lambda j, pr: (pr[2 + j], 0, 0))],
            out_specs=pl.BlockSpec(blk, lambda j, pr: (j, 0, 0)),
        ),
        compiler_params=pltpu.CompilerParams(dimension_semantics=("parallel",)),
        name=name,
    )(place, g, got)


def _chip_sum(name, place, g, got, arrivals, axis):
    _, r, cdim = g.shape
    hr, hc = got.shape[1:]

    def body(place_ref, g_ref, got_ref, arr_ref, o_ref):
        acc = g_ref[...] + got_ref[...]
        for j in range(3):
            acc = acc + arr_ref[j].astype(F32)
        o_ref[...] = acc

    blk = (None, hr, hc)
    mine = (lambda i, pr: (pr[0], pr[1], 0)) if axis == 0 else (lambda i, pr: (pr[0], 0, pr[1]))
    dest = (lambda i, pr: (pr[1], 0)) if axis == 0 else (lambda i, pr: (0, pr[1]))
    return pl.pallas_call(
        body,
        out_shape=_sds((r, cdim), F32),
        grid_spec=pltpu.PrefetchScalarGridSpec(
            num_scalar_prefetch=1,
            grid=(1,),
            in_specs=[
                pl.BlockSpec(blk, mine),
                pl.BlockSpec(blk, lambda i, pr: (pr[0], 0, 0)),
                pl.BlockSpec((3, hr, hc), lambda i, pr: (0, 0, 0)),
            ],
            out_specs=pl.BlockSpec((hr, hc), dest),
        ),
        compiler_params=pltpu.CompilerParams(dimension_semantics=("arbitrary",)),
        name=name,
    )(place, g, got, arrivals)


def _device_sum(name, gathered):
    m_per = gathered.shape[0] // N_DEV

    def body(g_ref, o_ref):
        acc = g_ref[0:m_per, :]
        for dev in range(1, N_DEV):
            acc = acc + g_ref[dev * m_per:(dev + 1) * m_per, :]
        o_ref[...] = acc

    return pl.pallas_call(body, out_shape=_sds((m_per, gathered.shape[1]), F32), name=name)(gathered)


def _adamw(name, w, g, m, v):
    r, cdim = w.shape
    if r % 8 == 0:
        tr, tcol = _tile(r, 256, 8), cdim
    else:
        tr, tcol = r, (_tile(cdim, 256, LANES) if cdim % LANES == 0 else cdim)
    blk = pl.BlockSpec((tr, tcol), lambda i, j: (i, j))
    grid = (r // tr, cdim // tcol)
    bc1 = 1.0 - ADAM_B1 ** ADAM_STEP
    bc2 = 1.0 - ADAM_B2 ** ADAM_STEP

    def body(w_ref, g_ref, m_ref, v_ref, d_ref, nm_ref, nv_ref):
        gv = g_ref[...]
        nm = ADAM_B1 * m_ref[...] + (1.0 - ADAM_B1) * gv
        nv = ADAM_B2 * v_ref[...] + (1.0 - ADAM_B2) * (gv * gv)
        d_ref[...] = -ADAM_LR * ((nm / bc1) / (jnp.sqrt(nv / bc2) + ADAM_EPS) + ADAM_WD * w_ref[...])
        nm_ref[...] = nm
        nv_ref[...] = nv

    shape = _sds(w.shape, F32)
    return pl.pallas_call(
        body,
        out_shape=(shape, shape, shape),
        grid=grid,
        in_specs=[blk] * 4,
        out_specs=(blk, blk, blk),
        compiler_params=pltpu.CompilerParams(dimension_semantics=("parallel", "parallel")),
        name=name,
    )(w, g, m, v)


def _cat_cols(g):
    return jnp.transpose(g, (1, 0, 2)).reshape(g.shape[1], N_CHIPS * g.shape[2])


def _split_cols(a):
    r, c4 = a.shape
    return jnp.transpose(a.reshape(r, N_CHIPS, c4 // N_CHIPS), (1, 0, 2))


def _local_step(x, target, w_int, late_weights, cmw, cfw, g1, b_f, b_gate, g2, gf,
                ffn_grads_ready, mix_grads_ready):
    batch, seq, d = x.shape
    t = batch * seq
    cw = d // 2
    fh = cfw.shape[1] // 2
    tc = LANES
    nct = cw // tc
    tq = min(512, seq)
    pc_w, qkv_w, gl_w = 3 * cw, 3 * ATTN_WIDTH, 2 * d
    qkv_off, gl_off, f_off = pc_w, pc_w + qkv_w, pc_w + qkv_w + gl_w
    width = f_off + F_PAD
    f_col = pc_w + qkv_w

    w_pc = w_int[:pc_w].reshape(3, nct, tc, d).transpose(1, 0, 2, 3).reshape(pc_w, d)
    w_qkv = w_int[pc_w:f_col].reshape(3, HEAD_PAIRS, LANES, d).transpose(1, 0, 2, 3).reshape(qkv_w, d)
    w_f = jnp.pad(w_int[f_col:f_col + HEADS], ((0, F_PAD - HEADS), (0, 0)))
    w_inp = jnp.concatenate([w_pc, w_qkv, w_int[f_col + HEADS:], w_f], axis=0)
    bf_pad = jnp.pad(b_f, ((0, 0), (0, F_PAD - HEADS)))

    x2d = x.reshape(t, d)
    tgt2d = target.reshape(t, d)

    h1 = _rms_fwd("norm_mix", x2d, g1)
    pc, qkv, gl, fl = _project("proj_in", h1, w_inp, [(pc_w, BF16), (qkv_w, BF16), (gl_w, BF16), (F_PAD, F32)])
    a_c = _conv_fwd("conv_mix", pc, cmw, batch, seq, tc)
    f_cum = _forget_fwd("forget_cumsum", fl, bf_pad, batch, seq)
    frow = f_cum.reshape(batch, HEAD_PAIRS, 2, seq)
    o, lse = _attn_fwd("attn_fwd", qkv, frow, batch, seq, tq)
    w_oc, w_oa, w_o, w_up, w_down = late_weights(o)
    ycat = _mm("out_conv", a_c, w_oc, "nn", BF16, m=t, n=d, k=cw, o_off=0, o_width=2 * d)
    ycat = _mm("out_attn", o, w_oa, "nn", BF16, m=t, n=d, k=ATTN_WIDTH, out=ycat, o_off=d)
    mg = _merge_fwd("gate_merge", ycat, gl, b_gate)
    x2 = _mm("mix_out", mg, w_o, "nn", F32, m=t, n=d, k=d, add=x2d)
    h2 = _rms_fwd("norm_ffn", x2, g2)
    tcf = min(2 * LANES, fh)
    w_up2 = _cat_cols(w_up)
    hmid, ua, ub, ffn_a, ffn_b = _ffn_up_act("ffn_up_act", h2, w_up2, cfw, batch, seq, tcf)
    x3 = _mm("ffn_down", hmid, w_down, "nn", F32, m=t, n=d, k=fh, add=x2, tk=4096)

    dx3, dx3b, loss_row, d_gf = _final_loss("final_loss", x3, gf.reshape(1, d), tgt2d)
    dw_down = _mm("dw_down", hmid, dx3b, "tn", F32, m=fh, n=d, k=t, tm=256, tk=8192)
    du_a, du_b, d_cfw = _ffn_bwd("d_ffn", dx3b, w_down, ua, ub, ffn_a, ffn_b, cfw, batch, seq, tcf)
    ws = w_up.shape[2]
    dh2 = _mm("d_norm_ffn_a", du_a, w_up2, "nt", BF16, m=t, n=d, k=fh, b_off=0, tk=4096)
    dh2 = _mm("d_norm_ffn_b", du_b, w_up2, "nt", BF16, m=t, n=d, k=fh, b_off=fh, add=dh2, tk=4096)
    dw_up = _mm("dw_up_a", h2, du_a, "tn", F32, m=d, n=fh, k=t, tm=512, tn=ws, tk=4096, o3=N_CHIPS)
    dw_up = _mm("dw_up_b", h2, du_b, "tn", F32, m=d, n=fh, k=t, tm=512, tn=ws, tk=4096, o3=N_CHIPS, out=dw_up,
                o_off=fh)
    token = ffn_grads_ready(dw_up, dw_down)
    if token is not None:
        g2 = g2 + token[0:1, 0:1]
    dx2, d_g2 = _rms_bwd("d_norm_ffn", x2, dh2, g2, dx3)
    dm = _mm("d_merge", dx2, w_o, "nt", BF16, m=t, n=d, k=d)
    dw_o = _mm("dw_o", mg, dx2, "tn", F32, m=d, n=d, k=t, tk=2048)
    dproj, dycat, d_bg = _merge_bwd("d_gate_merge", dm, ycat, gl, b_gate, width, gl_off)
    da_c = _mm("d_conv_out", dycat, w_oc, "nt", BF16, m=t, n=cw, k=d, a_off=0)
    do = _mm("d_attn_out", dycat, w_oa, "nt", BF16, m=t, n=ATTN_WIDTH, k=d, a_off=d)
    dw_oc = _mm("dw_out_conv", a_c, dycat, "tn", F32, m=cw, n=d, k=t, b_off=0, tk=2048)
    dw_oa = _mm("dw_out_attn", o, dycat, "tn", F32, m=ATTN_WIDTH, n=d, k=t, b_off=d, tk=2048)
    dproj, d_cmw = _conv_bwd("d_conv_mix", da_c, pc, cmw, dproj, batch, seq, tc)
    dproj, d_fkey, d_fquery = _attn_bwd("attn_bwd", qkv, do, o, lse, frow, dproj, qkv_off, batch, seq, tq)
    d_fquery = jnp.pad(jnp.transpose(d_fquery, (1, 0, 2)).reshape(t, HEADS), ((0, 0), (0, LANES - HEADS)))
    dproj, d_bf = _forget_bwd("d_forget", d_fkey.reshape(batch, HEADS, seq), d_fquery, fl, bf_pad, dproj, f_off,
                              batch, seq)
    dw_inp = _mm("dw_in", dproj, h1, "tn", F32, m=width, n=d, k=t, tm=256, tk=8192)
    d_pc = dw_inp[:pc_w].reshape(nct, 3, tc, d).transpose(1, 0, 2, 3).reshape(pc_w, d)
    d_qkv = dw_inp[qkv_off:gl_off].reshape(HEAD_PAIRS, 3, LANES, d).transpose(1, 0, 2, 3).reshape(qkv_w, d)
    dw_int = jnp.concatenate([d_pc, d_qkv, dw_inp[f_off:f_off + HEADS], dw_inp[gl_off:f_off]], axis=0)
    token = mix_grads_ready(dw_int, dw_oc, dw_oa, dw_o)
    dh1 = _mm("d_norm_mix", dproj, w_inp, "nn", BF16, m=t, n=d, k=width, tm=512, tk=8192, dep=token)
    grad_x, d_g1 = _rms_bwd("d_norm_mix_x", x2d, dh1, g1, dx2)
    smalls = (d_g1, d_g2, d_gf, d_bg, d_bf, d_cmw, d_cfw)
    return loss_row[0, 0], grad_x.reshape(batch, seq, d), smalls


def _pack_small(parts):
    flat = [p.reshape(-1) for p in parts]
    sizes = [f.shape[0] for f in flat]
    total = sum(sizes)
    padded = -(-total // (8 * LANES)) * (8 * LANES)
    vec = jnp.concatenate(flat + [jnp.zeros((padded - total,), F32)])
    offsets = [sum(sizes[:i]) for i in range(len(sizes))]
    return vec.reshape(padded // LANES, LANES), offsets


def kernel(x, norm_mix_g, w_in, b_f, b_gate, conv_mix_w, w_out_conv, w_out_attn, w_o, norm_ffn_g, w_up, conv_ffn_w, w_down, norm_f_g, loss_target, m_norm_mix_g, m_w_in, m_b_f, m_b_gate, m_conv_mix_w, m_w_out_conv, m_w_out_attn, m_w_o, m_norm_ffn_g, m_w_up, m_conv_ffn_w, m_w_down, m_norm_f_g, v_norm_mix_g, v_w_in, v_b_f, v_b_gate, v_conv_mix_w, v_w_out_conv, v_w_out_attn, v_w_o, v_norm_ffn_g, v_w_up, v_conv_ffn_w, v_w_down, v_norm_f_g):
    d = x.shape[-1]
    chip = 2 * lax.axis_index("x") + lax.axis_index("y")
    xi, yi = lax.axis_index("x"), lax.axis_index("y")
    peers = [2 * px + py for px, py in ((1 - xi, yi), (xi, 1 - yi), (1 - xi, 1 - yi))]
    place = jnp.stack([chip, lax.axis_index("c"), *peers]).astype(jnp.int32)

    t_in, t_m_in, t_v_in = (jnp.transpose(w[0]) for w in (w_in, m_w_in, v_w_in))

    def row_shards(a):
        return a.reshape(N_CHIPS, a.shape[0] // N_CHIPS, a.shape[1])

    def stacked(a):
        return a.reshape(N_CHIPS * a.shape[1], a.shape[2])

    a_in, a_cmw, a_cfw = _gather_weights([t_in.astype(BF16)], (1,), [conv_mix_w[0], conv_ffn_w[0]])
    late = [w[0].astype(BF16) for w in (w_out_conv, w_out_attn, w_o, w_up, w_down)]
    late_handles, late_token = _chips_start("gather_late_start", "gather", late)

    def late_weights(after):
        lands = _chips_wait("gather_late_wait", "gather", late_handles, after)
        a_oc, a_oa, a_o, a_up, a_down = (
            lax.dynamic_update_index_in_dim(buf, own, chip, 0) for buf, own in zip(lands, late))
        return _cat_cols(a_oc), _cat_cols(a_oa), stacked(a_o), a_up, stacked(a_down)

    pending = []

    def reduce_start(tag, names, grads, axes):
        got = _exchange_sibling("exchange_sibling_" + tag, grads, axes)
        sums = [_pair_sum("pair_sum_" + nm, place, g, r, ax) for nm, g, r, ax in zip(names, grads, got, axes)]
        handles, token = _chips_start("exchange_chips_start_" + tag, "reduce", sums)
        pending.append((tag, names, grads, axes, got, handles))
        return token

    def ffn_grads_ready(dw_up, dw_down):
        return reduce_start("ffn", ("w_up", "w_down"), [dw_up, row_shards(dw_down)], (0, 0))

    def mix_grads_ready(dw_int, dw_oc, dw_oa, dw_o):
        return reduce_start("mix", ("w_in", "w_out_conv", "w_out_attn", "w_o"),
                            [row_shards(dw_int), _split_cols(dw_oc), _split_cols(dw_oa), row_shards(dw_o)],
                            (1, 0, 0, 0))

    loss_local, grad_x, smalls = _local_step(
        x, loss_target, stacked(a_in), late_weights, _cat_cols(a_cmw),
        _cat_cols(a_cfw), norm_mix_g + late_token[0:1, 0:1], b_f, b_gate, norm_ffn_g, norm_f_g,
        ffn_grads_ready, mix_grads_ready)

    reduced = {}
    for tag, names, grads, axes, got, handles in pending:
        arrivals = _chips_wait("exchange_chips_wait_" + tag, "reduce", handles, grad_x)
        halves = [_chip_sum("chip_sum_" + nm, place, g, r, arr, ax)
                  for nm, g, r, arr, ax in zip(names, grads, got, arrivals, axes)]
        reduced.update(zip(names, _share_sibling("share_sibling_" + tag, halves, axes)))
    g_in, g_oc, g_oa, g_o, g_up, g_down = (
        reduced[nm] for nm in ("w_in", "w_out_conv", "w_out_attn", "w_o", "w_up", "w_down"))

    smalls = (*smalls, loss_local.reshape(1, 1))
    packed, offs = _pack_small(smalls)
    total = _device_sum("device_sum", _gather_small(packed)).reshape(-1)
    shapes = [s.shape for s in smalls]
    d_g1, d_g2, d_gf, d_bg, d_bf, d_cmw, d_cfw, loss = [
        total[o:o + math.prod(sh)].reshape(sh) for o, sh in zip(offs, shapes)]
    loss = loss[0, 0]
    d_bf = d_bf[:, :HEADS]
    cw_s, cf_s = conv_mix_w.shape[2], conv_ffn_w.shape[2]
    d_cmw = lax.dynamic_slice(d_cmw, (0, chip * cw_s), (3, cw_s))
    d_cfw = lax.dynamic_slice(d_cfw, (0, chip * cf_s), (3, cf_s))

    order = [
        ("norm_mix_g", norm_mix_g[0:1], d_g1, m_norm_mix_g, v_norm_mix_g),
        ("w_in", t_in, g_in, t_m_in, t_v_in),
        ("b_f", b_f, d_bf, m_b_f, v_b_f),
        ("b_gate", b_gate, d_bg, m_b_gate, v_b_gate),
        ("conv_mix_w", conv_mix_w[0], d_cmw, m_conv_mix_w[0], v_conv_mix_w[0]),
        ("w_out_conv", w_out_conv[0], g_oc, m_w_out_conv[0], v_w_out_conv[0]),
        ("w_out_attn", w_out_attn[0], g_oa, m_w_out_attn[0], v_w_out_attn[0]),
        ("w_o", w_o[0], g_o, m_w_o[0], v_w_o[0]),
        ("norm_ffn_g", norm_ffn_g, d_g2, m_norm_ffn_g, v_norm_ffn_g),
        ("w_up", w_up[0], g_up, m_w_up[0], v_w_up[0]),
        ("conv_ffn_w", conv_ffn_w[0], d_cfw, m_conv_ffn_w[0], v_conv_ffn_w[0]),
        ("w_down", w_down[0], g_down, m_w_down[0], v_w_down[0]),
        ("norm_f_g", norm_f_g.reshape(1, d), d_gf, m_norm_f_g.reshape(1, d), v_norm_f_g.reshape(1, d)),
    ]
    out_shapes = [norm_mix_g.shape, w_in.shape, b_f.shape, b_gate.shape, conv_mix_w.shape, w_out_conv.shape,
                  w_out_attn.shape, w_o.shape, norm_ffn_g.shape, w_up.shape, conv_ffn_w.shape, w_down.shape,
                  norm_f_g.shape]
    g_out, d_out, m_out, v_out = [], [], [], []
    for (nm, w, g, m, v), sh in zip(order, out_shapes):
        g = g.reshape(w.shape)
        delta, new_m, new_v = _adamw("adamw_" + nm, w, g, m.reshape(w.shape), v.reshape(w.shape))
        for dst, val in ((g_out, g), (d_out, delta), (m_out, new_m), (v_out, new_v)):
            dst.append((jnp.transpose(val) if nm == "w_in" else val).reshape(sh))
    return (loss, grad_x, *g_out, *d_out, *m_out, *v_out)
```

```python
import math

import jax
import jax.numpy as jnp
from jax import lax
from jax.experimental import pallas as pl
from jax.experimental.pallas import tpu as pltpu

F32 = jnp.float32
BF16 = jnp.bfloat16
MESH = pl.DeviceIdType.MESH

EPS = 1e-6
HEADS = 8
HEAD_DIM = 64
ATTN_WIDTH = HEADS * HEAD_DIM
HEAD_PAIRS = HEADS // 2
LANES = 128
F_PAD = 2 * LANES
NEG_BIG = -1e30
N_CHIPS = 4
N_DEV = 8

ADAM_LR = 0.001
ADAM_B1 = 0.9
ADAM_B2 = 0.999
ADAM_EPS = 1e-08
ADAM_WD = 0.01
ADAM_STEP = 10

_DIMS = {
    "nn": (((1,), (0,)), ((), ())),
    "nt": (((1,), (1,)), ((), ())),
    "tn": (((0,), (0,)), ((), ())),
}


def _tile(n, target, mult, also=()):
    best = None
    for t in range(mult, n + 1, mult):
        if n % t == 0 and t <= target and all(o % t == 0 for o in also):
            best = t
    if best is None:
        assert all(o == 0 for o in also), (n, target, mult, also)
        return n
    return best


def _sds(shape, dtype):
    return jax.ShapeDtypeStruct(shape, dtype)


def _mm(name, a, b, mode, out_dtype, *, m, n, k, a_off=0, b_off=0, out=None, o_off=0,
        o_width=None, o3=None, add=None, dep=None, tm=1024, tn=2048, tk=2048):
    if mode == "nn":
        tm = _tile(m, tm, 16)
        tk = _tile(k, tk, LANES, (a_off,))
        tn = _tile(n, tn, LANES, (b_off, o_off))
        a_spec = pl.BlockSpec((tm, tk), lambda i, j, kk: (i, a_off // tk + kk))
        b_spec = pl.BlockSpec((tk, tn), lambda i, j, kk: (kk, b_off // tn + j))
    elif mode == "nt":
        tm = _tile(m, tm, 16)
        tk = _tile(k, tk, LANES, (a_off, b_off))
        tn = _tile(n, tn, LANES, (o_off,))
        a_spec = pl.BlockSpec((tm, tk), lambda i, j, kk: (i, a_off // tk + kk))
        b_spec = pl.BlockSpec((tn, tk), lambda i, j, kk: (j, b_off // tk + kk))
    else:
        tm = _tile(m, tm, LANES, (a_off,))
        tk = _tile(k, tk, 16)
        tn = _tile(n, tn, LANES, (b_off, o_off))
        a_spec = pl.BlockSpec((tk, tm), lambda i, j, kk: (kk, a_off // tm + i))
        b_spec = pl.BlockSpec((tk, tn), lambda i, j, kk: (kk, b_off // tn + j))
    assert m % tm == 0 and n % tn == 0 and k % tk == 0, (name, tm, tn, tk)
    nk = k // tk
    if o3 is not None:
        o_spec = pl.BlockSpec((None, tm, tn), lambda i, j, kk: (o_off // tn + j, i, 0))
        out_sds = _sds((o3, m, tn), out_dtype)
    else:
        o_spec = pl.BlockSpec((tm, tn), lambda i, j, kk: (i, o_off // tn + j))
        width = o_width if o_width is not None else (out.shape[1] if out is not None else n)
        out_sds = _sds((m, width), out_dtype)
    use_acc = nk > 1 and out_dtype != F32
    dims = _DIMS[mode]
    has_add, has_out = add is not None, out is not None

    def body(*refs):
        a_ref, b_ref = refs[0], refs[1]
        pos = 2
        add_ref = None
        if has_add:
            add_ref = refs[pos]
            pos += 1
        if has_out:
            pos += 1
        if dep is not None:
            pos += 1
        o_ref = refs[pos]
        acc_ref = refs[pos + 1] if use_acc else None
        part = lax.dot_general(a_ref[...].astype(BF16), b_ref[...].astype(BF16), dims,
                               preferred_element_type=F32)
        if nk == 1:
            if has_add:
                part = part + add_ref[...]
            o_ref[...] = part.astype(o_ref.dtype)
            return
        kk = pl.program_id(2)
        tgt = acc_ref if use_acc else o_ref

        @pl.when(kk == 0)
        def _():
            tgt[...] = part + add_ref[...] if has_add else part

        @pl.when(kk > 0)
        def _():
            tgt[...] += part

        if use_acc:
            @pl.when(kk == nk - 1)
            def _():
                o_ref[...] = acc_ref[...].astype(o_ref.dtype)

    operands, in_specs = [a, b], [a_spec, b_spec]
    if has_add:
        operands.append(add)
        in_specs.append(pl.BlockSpec((tm, tn), lambda i, j, kk: (i, j)))
    aliases = {}
    if has_out:
        aliases = {len(operands): 0}
        operands.append(out)
        in_specs.append(pl.BlockSpec(memory_space=pl.ANY))
    if dep is not None:
        operands.append(dep)
        in_specs.append(pl.BlockSpec(memory_space=pl.ANY))
    return pl.pallas_call(
        body,
        out_shape=out_sds,
        grid=(m // tm, n // tn, nk),
        in_specs=in_specs,
        out_specs=o_spec,
        scratch_shapes=[pltpu.VMEM((tm, tn), F32)] if use_acc else [],
        input_output_aliases=aliases,
        compiler_params=pltpu.CompilerParams(dimension_semantics=("parallel", "parallel", "arbitrary")),
        name=name,
    )(*operands)


def _project(name, h, w_t, groups):
    t, d = h.shape
    tm = _tile(t, 512, 16)
    offs = [sum(n for n, _ in groups[:i]) for i in range(len(groups))]

    def body(h_ref, w_ref, *o_refs):
        hv = h_ref[...]
        for (n, _), off, o_ref in zip(groups, offs, o_refs):
            o_ref[...] = _dot(hv, w_ref[off:off + n, :], "nt").astype(o_ref.dtype)

    return pl.pallas_call(
        body,
        out_shape=tuple(_sds((t, n), dt) for n, dt in groups),
        grid=(t // tm,),
        in_specs=[pl.BlockSpec((tm, d), lambda i: (i, 0)), pl.BlockSpec(w_t.shape, lambda i: (0, 0))],
        out_specs=tuple(pl.BlockSpec((tm, n), lambda i: (i, 0)) for n, _ in groups),
        compiler_params=pltpu.CompilerParams(dimension_semantics=("parallel",)),
        name=name,
    )(h, w_t)


def _rms_fwd(name, x, g):
    t, d = x.shape
    tm = _tile(t, 512, 16)

    def body(x_ref, g_ref, o_ref):
        xv = x_ref[...]
        r = lax.rsqrt(jnp.mean(xv * xv, axis=-1, keepdims=True) + EPS)
        o_ref[...] = ((xv * r) * g_ref[...]).astype(o_ref.dtype)

    return pl.pallas_call(
        body,
        out_shape=_sds((t, d), BF16),
        grid=(t // tm,),
        in_specs=[pl.BlockSpec((tm, d), lambda i: (i, 0)), pl.BlockSpec((1, d), lambda i: (0, 0))],
        out_specs=pl.BlockSpec((tm, d), lambda i: (i, 0)),
        compiler_params=pltpu.CompilerParams(dimension_semantics=("parallel",)),
        name=name,
    )(x, g)


def _rms_bwd(name, x, dh, g, res):
    t, d = x.shape
    tm = _tile(t, 512, 16)

    def body(x_ref, dh_ref, g_ref, res_ref, dx_ref, dg_ref):
        xv = x_ref[...]
        r = lax.rsqrt(jnp.mean(xv * xv, axis=-1, keepdims=True) + EPS)
        xh = xv * r
        dhv = dh_ref[...].astype(F32)
        dxh = dhv * g_ref[...]
        dx_ref[...] = res_ref[...] + r * (dxh - xh * jnp.mean(dxh * xh, axis=-1, keepdims=True))

        @pl.when(pl.program_id(0) == 0)
        def _():
            dg_ref[...] = jnp.zeros_like(dg_ref)

        dg_ref[...] += jnp.sum(dhv * xh, axis=0, keepdims=True)

    row = pl.BlockSpec((tm, d), lambda i: (i, 0))
    vec = pl.BlockSpec((1, d), lambda i: (0, 0))
    return pl.pallas_call(
        body,
        out_shape=(_sds((t, d), F32), _sds((1, d), F32)),
        grid=(t // tm,),
        in_specs=[row, row, vec, row],
        out_specs=(row, vec),
        compiler_params=pltpu.CompilerParams(dimension_semantics=("arbitrary",)),
        name=name,
    )(x, dh, g, res)


def _final_loss(name, x, g, target):
    t, d = x.shape
    tm = _tile(t, 512, 16)

    def body(x_ref, g_ref, t_ref, dx_ref, dxb_ref, loss_ref, dg_ref):
        xv = x_ref[...]
        gv = g_ref[...]
        r = lax.rsqrt(jnp.mean(xv * xv, axis=-1, keepdims=True) + EPS)
        xh = xv * r
        err = xh * gv - t_ref[...]
        dy = err * (1.0 / d)
        dxh = dy * gv
        dx = r * (dxh - xh * jnp.mean(dxh * xh, axis=-1, keepdims=True))
        dx_ref[...] = dx
        dxb_ref[...] = dx.astype(dxb_ref.dtype)
        per_row = jnp.sum(err * err, axis=-1, keepdims=True) * (0.5 / d)

        @pl.when(pl.program_id(0) == 0)
        def _():
            dg_ref[...] = jnp.zeros_like(dg_ref)
            loss_ref[...] = jnp.zeros_like(loss_ref)

        dg_ref[...] += jnp.sum(dy * xh, axis=0, keepdims=True)
        loss_ref[...] += jnp.sum(per_row, axis=0, keepdims=True)

    row = pl.BlockSpec((tm, d), lambda i: (i, 0))
    vec = pl.BlockSpec((1, d), lambda i: (0, 0))
    return pl.pallas_call(
        body,
        out_shape=(_sds((t, d), F32), _sds((t, d), BF16), _sds((1, LANES), F32), _sds((1, d), F32)),
        grid=(t // tm,),
        in_specs=[row, vec, row],
        out_specs=(row, row, pl.BlockSpec((1, LANES), lambda i: (0, 0)), vec),
        compiler_params=pltpu.CompilerParams(dimension_semantics=("arbitrary",)),
        name=name,
    )(x, g, target)


def _shift_down(z, k):
    row = lax.broadcasted_iota(jnp.int32, z.shape, 0)
    return jnp.where(row >= k, pltpu.roll(z, k, axis=0), 0.0)


def _shift_up(z, k):
    s = z.shape[0]
    row = lax.broadcasted_iota(jnp.int32, z.shape, 0)
    return jnp.where(row < s - k, pltpu.roll(z, s - k, axis=0), 0.0)


def _conv3(z, w):
    return (w[2:3] * z + w[0:1] * _shift_down(z, 2)) + w[1:2] * _shift_down(z, 1)


def _conv3_t(dz, w):
    return (w[2:3] * dz + w[0:1] * _shift_up(dz, 2)) + w[1:2] * _shift_up(dz, 1)


def _conv_fwd(name, pc, w, batch, seq, tc):
    cw = w.shape[1]
    nct = cw // tc

    def body(pc_ref, w_ref, o_ref):
        cb = pc_ref[:, 0:tc].astype(F32)
        z = pc_ref[:, tc:2 * tc].astype(F32) * pc_ref[:, 2 * tc:3 * tc].astype(F32)
        o_ref[...] = (cb * _conv3(z, w_ref[...])).astype(o_ref.dtype)

    return pl.pallas_call(
        body,
        out_shape=_sds((batch * seq, cw), BF16),
        grid=(batch, nct),
        in_specs=[pl.BlockSpec((seq, 3 * tc), lambda b, j: (b, j)), pl.BlockSpec((3, tc), lambda b, j: (0, j))],
        out_specs=pl.BlockSpec((seq, tc), lambda b, j: (b, j)),
        compiler_params=pltpu.CompilerParams(dimension_semantics=("parallel", "parallel")),
        name=name,
    )(pc, w)


def _conv_bwd(name, da, pc, w, dproj, batch, seq, tc):
    cw = w.shape[1]
    nct = cw // tc

    def body(da_ref, pc_ref, w_ref, _, dpc_ref, dw_ref):
        wv = w_ref[...]
        cb = pc_ref[:, 0:tc].astype(F32)
        cc = pc_ref[:, tc:2 * tc].astype(F32)
        cin = pc_ref[:, 2 * tc:3 * tc].astype(F32)
        z = cc * cin
        dav = da_ref[...].astype(F32)
        du = dav * cb
        dz = _conv3_t(du, wv)
        dpc_ref[:, 0:tc] = (dav * _conv3(z, wv)).astype(dpc_ref.dtype)
        dpc_ref[:, tc:2 * tc] = (dz * cin).astype(dpc_ref.dtype)
        dpc_ref[:, 2 * tc:3 * tc] = (dz * cc).astype(dpc_ref.dtype)

        @pl.when(pl.program_id(1) == 0)
        def _():
            dw_ref[...] = jnp.zeros_like(dw_ref)

        dw_ref[0:1, :] += jnp.sum(du * _shift_down(z, 2), axis=0, keepdims=True)
        dw_ref[1:2, :] += jnp.sum(du * _shift_down(z, 1), axis=0, keepdims=True)
        dw_ref[2:3, :] += jnp.sum(du * z, axis=0, keepdims=True)

    return pl.pallas_call(
        body,
        out_shape=(_sds(dproj.shape, dproj.dtype), _sds((3, cw), F32)),
        grid=(nct, batch),
        in_specs=[
            pl.BlockSpec((seq, tc), lambda j, b: (b, j)),
            pl.BlockSpec((seq, 3 * tc), lambda j, b: (b, j)),
            pl.BlockSpec((3, tc), lambda j, b: (0, j)),
            pl.BlockSpec(memory_space=pl.ANY),
        ],
        out_specs=(pl.BlockSpec((seq, 3 * tc), lambda j, b: (b, j)), pl.BlockSpec((3, tc), lambda j, b: (0, j))),
        input_output_aliases={3: 0},
        compiler_params=pltpu.CompilerParams(dimension_semantics=("parallel", "arbitrary")),
        name=name,
    )(da, pc, w, dproj)


def _ffn_up_act(name, h2, w_up, w, batch, seq, tc):
    d = h2.shape[1]
    fh = w.shape[1] // 2
    nf = fh // tc

    def body(h_ref, ma_ref, mb_ref, wa_ref, wb_ref, o_ref, ua_ref, ub_ref, a_ref, b_ref):
        hv = h_ref[...]
        ua = _dot(hv, ma_ref[...], "nn")
        ub = _dot(hv, mb_ref[...], "nn")
        ua_ref[...] = ua.astype(ua_ref.dtype)
        ub_ref[...] = ub.astype(ub_ref.dtype)
        a = _conv3(ua, wa_ref[...])
        b = _conv3(ub, wb_ref[...])
        a_ref[...] = a.astype(a_ref.dtype)
        b_ref[...] = b.astype(b_ref.dtype)
        o_ref[...] = (a * jax.nn.sigmoid(a) * b).astype(o_ref.dtype)

    act = pl.BlockSpec((seq, tc), lambda b, j: (b, j))
    shape = _sds((batch * seq, fh), BF16)
    return pl.pallas_call(
        body,
        out_shape=(shape,) * 5,
        grid=(batch, nf),
        in_specs=[
            pl.BlockSpec((seq, d), lambda b, j: (b, 0)),
            pl.BlockSpec((d, tc), lambda b, j: (0, j)),
            pl.BlockSpec((d, tc), lambda b, j: (0, nf + j)),
            pl.BlockSpec((3, tc), lambda b, j: (0, j)),
            pl.BlockSpec((3, tc), lambda b, j: (0, nf + j)),
        ],
        out_specs=(act,) * 5,
        compiler_params=pltpu.CompilerParams(dimension_semantics=("parallel", "parallel")),
        name=name,
    )(h2, w_up, w_up, w, w)


def _ffn_bwd(name, dx, w_down, ua, ub, av, bv, w, batch, seq, tc):
    d = dx.shape[1]
    fh = w.shape[1] // 2
    nf = fh // tc

    rb = _tile(seq, 128, 8)
    halo = 8

    def body(dx_ref, md_ref, ua_ref, ub_ref, a_ref, b_ref, wa_ref, wb_ref, dua_ref, dub_ref, dw_ref,
             dh_scr, da_scr, db_scr):
        j = pl.program_id(1)
        dh_scr[...] = _dot(dx_ref[...].astype(BF16), md_ref[...], "nt")
        da_scr[seq:seq + halo, :] = jnp.zeros((halo, tc), F32)
        db_scr[seq:seq + halo, :] = jnp.zeros((halo, tc), F32)

        def silu_bwd(r, carry):
            rows = pl.ds(pl.multiple_of(r * rb, rb), rb)
            a, b, dhv = a_ref[rows, :].astype(F32), b_ref[rows, :].astype(F32), dh_scr[rows, :]
            sg = jax.nn.sigmoid(a)
            da_scr[rows, :] = dhv * b * (sg * (1.0 + a * (1.0 - sg)))
            db_scr[rows, :] = dhv * (a * sg)
            return carry

        lax.fori_loop(0, seq // rb, silu_bwd, 0)
        wa, wb = wa_ref[...], wb_ref[...]

        def conv_bwd(r, sums):
            r0 = pl.multiple_of(r * rb, rb)
            rows = pl.ds(r0, rb)
            out = []
            for d_scr, u_ref, wv, du_ref, acc in ((da_scr, ua_ref, wa, dua_ref, sums[0:3]),
                                                  (db_scr, ub_ref, wb, dub_ref, sums[3:6])):
                x = d_scr[pl.ds(r0, rb + halo), :]
                dv = x[0:rb]
                up1 = pltpu.roll(x, rb + halo - 1, axis=0)[0:rb]
                up2 = pltpu.roll(x, rb + halo - 2, axis=0)[0:rb]
                du_ref[rows, :] = ((wv[2:3] * dv + wv[0:1] * up2) + wv[1:2] * up1).astype(du_ref.dtype)
                uv = u_ref[rows, :].astype(F32)
                out += [acc[0] + jnp.sum(up2 * uv, axis=0, keepdims=True),
                        acc[1] + jnp.sum(up1 * uv, axis=0, keepdims=True),
                        acc[2] + jnp.sum(dv * uv, axis=0, keepdims=True)]
            return tuple(out)

        sums = lax.fori_loop(0, seq // rb, conv_bwd, (jnp.zeros((1, tc), F32),) * 6)

        @pl.when((pl.program_id(0) == 0) & (j == 0))
        def _():
            dw_ref[...] = jnp.zeros_like(dw_ref)

        for half, off in enumerate((0, fh)):
            cols = pl.ds(pl.multiple_of(off + j * tc, LANES), tc)
            for k in range(3):
                dw_ref[k:k + 1, cols] += sums[3 * half + k]

    act = pl.BlockSpec((seq, tc), lambda b, j: (b, j))
    shape = _sds((batch * seq, fh), BF16)
    return pl.pallas_call(
        body,
        out_shape=(shape, shape, _sds((3, 2 * fh), F32)),
        grid=(batch, nf),
        in_specs=[
            pl.BlockSpec((seq, d), lambda b, j: (b, 0)),
            pl.BlockSpec((tc, d), lambda b, j: (j, 0)),
            act,
            act,
            act,
            act,
            pl.BlockSpec((3, tc), lambda b, j: (0, j)),
            pl.BlockSpec((3, tc), lambda b, j: (0, nf + j)),
        ],
        out_specs=(act, act, pl.BlockSpec((3, 2 * fh), lambda b, j: (0, 0))),
        scratch_shapes=[pltpu.VMEM((seq, tc), F32), pltpu.VMEM((seq + halo, tc), F32),
                        pltpu.VMEM((seq + halo, tc), F32)],
        compiler_params=pltpu.CompilerParams(dimension_semantics=("arbitrary", "arbitrary")),
        name=name,
    )(dx, w_down, ua, ub, av, bv, w, w)


def _merge_fwd(name, ycat, gl, bg):
    t, d2 = ycat.shape
    d = d2 // 2
    tm = _tile(t, 1024, 16)

    def body(y_ref, gl_ref, bg_ref, o_ref):
        g = jax.nn.sigmoid(gl_ref[...].astype(F32) + bg_ref[...])
        prod = g * y_ref[...].astype(F32)
        o_ref[...] = (prod[:, 0:d] + prod[:, d:d2]).astype(o_ref.dtype)

    row = pl.BlockSpec((tm, d2), lambda i: (i, 0))
    return pl.pallas_call(
        body,
        out_shape=_sds((t, d), BF16),
        grid=(t // tm,),
        in_specs=[row, row, pl.BlockSpec((1, d2), lambda i: (0, 0))],
        out_specs=pl.BlockSpec((tm, d), lambda i: (i, 0)),
        compiler_params=pltpu.CompilerParams(dimension_semantics=("parallel",)),
        name=name,
    )(ycat, gl, bg)


def _merge_bwd(name, dm, ycat, gl, bg, width, gl_off):
    t, d2 = ycat.shape
    d = d2 // 2
    tm = _tile(t, 1024, 16)
    wb = math.gcd(gl_off, d)
    nw = d // wb

    def body(dm_ref, y_ref, gl_ref, bg_ref, dgl_ref, dy_ref, dbg_ref):
        g = jax.nn.sigmoid(gl_ref[...].astype(F32) + bg_ref[...])
        dmv = dm_ref[...].astype(F32)
        dgl = dmv * y_ref[...].astype(F32) * (g * (1.0 - g))
        dgl_ref[...] = dgl.astype(dgl_ref.dtype)
        dy_ref[...] = (dmv * g).astype(dy_ref.dtype)

        @pl.when(pl.program_id(2) == 0)
        def _():
            dbg_ref[...] = jnp.zeros_like(dbg_ref)

        dbg_ref[...] += jnp.sum(dgl, axis=0, keepdims=True)

    half = pl.BlockSpec((tm, wb), lambda h, j, i: (i, h * nw + j))
    vec = pl.BlockSpec((1, wb), lambda h, j, i: (0, h * nw + j))
    return pl.pallas_call(
        body,
        out_shape=(_sds((t, width), BF16), _sds((t, d2), BF16), _sds((1, d2), F32)),
        grid=(2, nw, t // tm),
        in_specs=[pl.BlockSpec((tm, wb), lambda h, j, i: (i, j)), half, half, vec],
        out_specs=(pl.BlockSpec((tm, wb), lambda h, j, i: (i, gl_off // wb + h * nw + j)), half, vec),
        compiler_params=pltpu.CompilerParams(dimension_semantics=("parallel", "parallel", "arbitrary")),
        name=name,
    )(dm, ycat, gl, bg)


def _log_sigmoid(z):
    return jnp.minimum(z, 0.0) - jnp.log1p(jnp.exp(-jnp.abs(z)))


def _forget_fwd(name, fl, bf, batch, seq):
    def body(fl_ref, bf_ref, o_ref):
        lf = _log_sigmoid(fl_ref[:, 0:LANES] + bf_ref[:, 0:LANES])
        acc = lf.T[0:HEADS, :]
        lane = lax.broadcasted_iota(jnp.int32, acc.shape, 1)
        k = 1
        while k < seq:
            acc = acc + jnp.where(lane >= k, pltpu.roll(acc, k, axis=1), 0.0)
            k *= 2
        o_ref[...] = acc

    return pl.pallas_call(
        body,
        out_shape=_sds((batch, HEADS, seq), F32),
        grid=(batch,),
        in_specs=[pl.BlockSpec((seq, F_PAD), lambda b: (b, 0)), pl.BlockSpec((1, F_PAD), lambda b: (0, 0))],
        out_specs=pl.BlockSpec((None, HEADS, seq), lambda b: (b, 0, 0)),
        compiler_params=pltpu.CompilerParams(dimension_semantics=("parallel",)),
        name=name,
    )(fl, bf)


def _forget_bwd(name, d_key, d_query, fl, bf, dproj, f_off, batch, seq):
    nfb = F_PAD // LANES

    def body(dk_ref, dq_ref, fl_ref, bf_ref, _, df_ref, dbf_ref):
        jj = pl.program_id(1)
        key_t = jnp.concatenate([dk_ref[...], jnp.zeros((LANES - HEADS, seq), F32)], axis=0).T
        acc = dq_ref[...] - key_t
        row = lax.broadcasted_iota(jnp.int32, acc.shape, 0)
        k = 1
        while k < seq:
            acc = acc + jnp.where(row < seq - k, pltpu.roll(acc, seq - k, axis=0), 0.0)
            k *= 2
        z = fl_ref[:, 0:LANES] + bf_ref[:, 0:LANES]
        col = lax.broadcasted_iota(jnp.int32, acc.shape, 1)
        df = jnp.where(col < HEADS, acc * jax.nn.sigmoid(-z), 0.0)
        df = jnp.where(jj == 0, df, 0.0)
        df_ref[...] = df.astype(df_ref.dtype)

        @pl.when((pl.program_id(0) == 0) & (jj == 0))
        def _():
            dbf_ref[...] = jnp.zeros_like(dbf_ref)

        dbf_ref[...] += jnp.sum(df, axis=0, keepdims=True)

    return pl.pallas_call(
        body,
        out_shape=(_sds(dproj.shape, dproj.dtype), _sds((1, LANES), F32)),
        grid=(batch, nfb),
        in_specs=[
            pl.BlockSpec((None, HEADS, seq), lambda b, j: (b, 0, 0)),
            pl.BlockSpec((seq, LANES), lambda b, j: (b, 0)),
            pl.BlockSpec((seq, F_PAD), lambda b, j: (b, 0)),
            pl.BlockSpec((1, F_PAD), lambda b, j: (0, 0)),
            pl.BlockSpec(memory_space=pl.ANY),
        ],
        out_specs=(pl.BlockSpec((seq, LANES), lambda b, j: (b, f_off // LANES + j)),
                   pl.BlockSpec((1, LANES), lambda b, j: (0, 0))),
        input_output_aliases={4: 0},
        compiler_params=pltpu.CompilerParams(dimension_semantics=("arbitrary", "arbitrary")),
        name=name,
    )(d_key, d_query, fl, bf, dproj)


def _dot(a, b, mode):
    return lax.dot_general(a, b, _DIMS[mode], preferred_element_type=F32)


def _attn_fwd(name, qkv, frow, batch, seq, tq):
    nq = seq // tq
    scale = 1.0 / math.sqrt(HEAD_DIM)

    def body(q_ref, k_ref, v_ref, f_ref, o_ref, lse_ref):
        i = pl.program_id(2)
        lane = lax.broadcasted_iota(jnp.int32, (1, LANES), 1)
        lo = lane < HEAD_DIM
        qs = q_ref[...] * scale
        qh = (jnp.where(lo, qs, 0.0).astype(BF16), jnp.where(lo, 0.0, qs).astype(BF16))
        row = lax.broadcasted_iota(jnp.int32, (tq, tq), 0)
        col = lax.broadcasted_iota(jnp.int32, (tq, tq), 1)

        def step(j, carry, diag):
            m0, l0, m1, l1, acc = carry
            start = pl.multiple_of(j * tq, tq)
            kj = k_ref[pl.ds(start, tq), :]
            vj = v_ref[pl.ds(start, tq), :]
            ms, ls, pvs, alphas = [], [], [], []
            for h, (m_old, l_old) in enumerate(((m0, l0), (m1, l1))):
                s = _dot(qh[h], kj, "nt") - f_ref[h:h + 1, pl.ds(start, tq)]
                if diag:
                    s = jnp.where(col <= row, s, NEG_BIG)
                m_new = jnp.maximum(m_old, jnp.max(s, axis=1, keepdims=True))
                p = jnp.exp(s - m_new)
                alpha = jnp.exp(m_old - m_new)
                ls.append(alpha * l_old + jnp.sum(p, axis=1, keepdims=True))
                ms.append(m_new)
                alphas.append(alpha)
                vh = jnp.where(lo, vj, 0.0) if h == 0 else jnp.where(lo, 0.0, vj)
                pvs.append(_dot(p.astype(BF16), vh.astype(BF16), "nn"))
            acc = acc * jnp.where(lo, alphas[0], alphas[1]) + (pvs[0] + pvs[1])
            return ms[0], ls[0], ms[1], ls[1], acc

        neg = jnp.full((tq, 1), NEG_BIG, F32)
        zero = jnp.zeros((tq, 1), F32)
        init = (neg, zero, neg, zero, jnp.zeros((tq, LANES), F32))
        carry = lax.fori_loop(0, i, lambda j, c: step(j, c, False), init)
        m0, l0, m1, l1, acc = step(i, carry, True)
        o_ref[...] = (acc / jnp.where(lo, l0, l1)).astype(o_ref.dtype)
        lse_ref[:, 0:1] = m0 + jnp.log(l0)
        lse_ref[:, 1:2] = m1 + jnp.log(l1)

    return pl.pallas_call(
        body,
        out_shape=(_sds((batch * seq, ATTN_WIDTH), BF16), _sds((HEAD_PAIRS, batch * seq, 2), F32)),
        grid=(batch, HEAD_PAIRS, nq),
        in_specs=[
            pl.BlockSpec((tq, LANES), lambda b, hp, i: (b * nq + i, 3 * hp)),
            pl.BlockSpec((seq, LANES), lambda b, hp, i: (b, 3 * hp + 1)),
            pl.BlockSpec((seq, LANES), lambda b, hp, i: (b, 3 * hp + 2)),
            pl.BlockSpec((None, None, 2, seq), lambda b, hp, i: (b, hp, 0, 0)),
        ],
        out_specs=(
            pl.BlockSpec((tq, LANES), lambda b, hp, i: (b * nq + i, hp)),
            pl.BlockSpec((None, tq, 2), lambda b, hp, i: (hp, b * nq + i, 0)),
        ),
        compiler_params=pltpu.CompilerParams(dimension_semantics=("parallel", "parallel", "parallel")),
        name=name,
    )(qkv, qkv, qkv, frow)


def _attn_bwd(name, qkv, do, o, lse, frow, dproj, qkv_off, batch, seq, tq):
    nq = seq // tq
    scale = 1.0 / math.sqrt(HEAD_DIM)

    def body(q_ref, k_ref, v_ref, do_ref, o_ref, lse_ref, f_ref, _, dqkv_ref, df_ref, drow_ref,
             dq_acc, dk_acc, dv_acc, df_acc):
        j = pl.program_id(2)
        lane = lax.broadcasted_iota(jnp.int32, (1, LANES), 1)
        lo = lane < HEAD_DIM
        masks = (lo, jnp.logical_not(lo))
        row = lax.broadcasted_iota(jnp.int32, (tq, tq), 0)
        col = lax.broadcasted_iota(jnp.int32, (tq, tq), 1)

        @pl.when(j == 0)
        def _():
            dq_acc[...] = jnp.zeros_like(dq_acc)
            drow_ref[...] = jnp.zeros_like(drow_ref)

        dk_acc[...] = jnp.zeros_like(dk_acc)
        dv_acc[...] = jnp.zeros_like(dv_acc)
        df_acc[...] = jnp.zeros_like(df_acc)
        kj = k_ref[...]
        vj = v_ref[...]
        kstart = pl.multiple_of(j * tq, tq)
        kh = tuple(jnp.where(mk, kj, 0.0).astype(BF16) for mk in masks)

        def step(i, diag):
            start = pl.multiple_of(i * tq, tq)
            rows = pl.ds(start, tq)
            qi = q_ref[rows, :] * scale
            doi = do_ref[rows, :]
            prod = doi.astype(F32) * o_ref[rows, :].astype(F32)
            lse_i = lse_ref[rows, :]
            dq_i = jnp.zeros((tq, LANES), F32)
            for h, mk in enumerate(masks):
                q_h = jnp.where(mk, qi, 0.0).astype(BF16)
                do_h = jnp.where(mk, doi, 0.0).astype(BF16)
                delta = jnp.sum(jnp.where(mk, prod, 0.0), axis=1, keepdims=True)
                s = _dot(q_h, kj, "nt") - f_ref[h:h + 1, pl.ds(kstart, tq)]
                p = jnp.exp(s - lse_i[:, h:h + 1])
                if diag:
                    p = jnp.where(col <= row, p, 0.0)
                ds = p * (_dot(do_h, vj, "nt") - delta)
                df_acc[h:h + 1, :] += jnp.sum(ds, axis=0, keepdims=True)
                drow_ref[rows, h:h + 1] += jnp.sum(ds, axis=1, keepdims=True)
                dsb = ds.astype(BF16)
                dv_acc[...] += _dot(p.astype(BF16), do_h, "tn")
                dk_acc[...] += _dot(dsb, q_h, "tn")
                dq_i = dq_i + _dot(dsb, kh[h], "nn")
            dq_acc[rows, :] += dq_i

        step(j, True)
        lax.fori_loop(j + 1, nq, lambda i, c: (step(i, False), c)[1], 0)
        dqkv_ref[:, 0:LANES] = (dq_acc[pl.ds(kstart, tq), :] * scale).astype(dqkv_ref.dtype)
        dqkv_ref[:, LANES:2 * LANES] = dk_acc[...].astype(dqkv_ref.dtype)
        dqkv_ref[:, 2 * LANES:3 * LANES] = dv_acc[...].astype(dqkv_ref.dtype)
        df_ref[...] = df_acc[...]

    full = lambda c: pl.BlockSpec((seq, LANES), lambda b, hp, j: (b, c(hp)))
    blk = lambda c: pl.BlockSpec((tq, LANES), lambda b, hp, j: (b * nq + j, c(hp)))
    return pl.pallas_call(
        body,
        out_shape=(_sds(dproj.shape, dproj.dtype), _sds((batch, HEAD_PAIRS, 2, seq), F32),
                   _sds((HEAD_PAIRS, batch * seq, 2), F32)),
        grid=(batch, HEAD_PAIRS, nq),
        in_specs=[
            full(lambda hp: 3 * hp),
            blk(lambda hp: 3 * hp + 1),
            blk(lambda hp: 3 * hp + 2),
            full(lambda hp: hp),
            full(lambda hp: hp),
            pl.BlockSpec((None, seq, 2), lambda b, hp, j: (hp, b, 0)),
            pl.BlockSpec((None, None, 2, seq), lambda b, hp, j: (b, hp, 0, 0)),
            pl.BlockSpec(memory_space=pl.ANY),
        ],
        out_specs=(
            pl.BlockSpec((tq, 3 * LANES), lambda b, hp, j: (b * nq + j, qkv_off // (3 * LANES) + hp)),
            pl.BlockSpec((None, None, 2, tq), lambda b, hp, j: (b, hp, 0, j)),
            pl.BlockSpec((None, seq, 2), lambda b, hp, j: (hp, b, 0)),
        ),
        scratch_shapes=[
            pltpu.VMEM((seq, LANES), F32),
            pltpu.VMEM((tq, LANES), F32),
            pltpu.VMEM((tq, LANES), F32),
            pltpu.VMEM((2, tq), F32),
        ],
        input_output_aliases={7: 0},
        compiler_params=pltpu.CompilerParams(dimension_semantics=("parallel", "parallel", "arbitrary")),
        name=name,
    )(qkv, qkv, qkv, do, o, lse, frow, dproj)


def _mesh_place():
    x, y, c = lax.axis_index("x"), lax.axis_index("y"), lax.axis_index("c")
    chips = [(1 - x, y), (x, 1 - y), (1 - x, 1 - y)]
    return x, y, c, chips


def _hbm_specs(n):
    return [pl.BlockSpec(memory_space=pl.ANY)] * n


def _half(shape2d, axis, which):
    size = shape2d[axis] // 2
    sl = pl.ds(pl.multiple_of(which * size, 16 if axis == 0 else LANES), size)
    return (sl, slice(None)) if axis == 0 else (slice(None), sl)


def _gather_weights(bigs, axes, smalls):
    nb, ns = len(bigs), len(smalls)
    arrays = list(bigs) + list(smalls)
    n = nb + ns

    def body(*refs):
        ins, outs = refs[:n], refs[n:2 * n]
        send_sems, recv_sems = refs[2 * n:]
        x, y, c, chips = _mesh_place()
        me = 2 * x + y
        sibling = (x, y, 1 - c)

        def half(a, which):
            return _half(arrays[a].shape, axes[a], which)

        def copy(a, k, src, dst, to):
            return pltpu.make_async_remote_copy(src_ref=src, dst_ref=dst, send_sem=send_sems.at[a, k],
                                                recv_sem=recv_sems.at[a, k], device_id=to, device_id_type=MESH)

        sends = []
        for a in range(n):
            for j, chip in enumerate(chips):
                if a < nb:
                    cp = copy(a, j, ins[a].at[half(a, c)], outs[a].at[(me,) + half(a, c)], (*chip, c))
                else:
                    cp = copy(a, j, ins[a], outs[a].at[me], (*chip, c))
                cp.start()
                sends.append(cp)
        for a in range(nb):
            for j, (px, py) in enumerate(chips):
                blk = outs[a].at[(2 * px + py,) + half(a, c)]
                copy(a, j, blk, blk, (px, py, c)).wait_recv()
                fwd = copy(a, 3 + j, blk, blk, sibling)
                fwd.start()
                sends.append(fwd)
        for a in range(nb, n):
            for j, (px, py) in enumerate(chips):
                blk = outs[a].at[2 * px + py]
                copy(a, j, blk, blk, (px, py, c)).wait_recv()
        for a in range(nb):
            for j, (px, py) in enumerate(chips):
                blk = outs[a].at[(2 * px + py,) + half(a, 1 - c)]
                copy(a, 3 + j, blk, blk, sibling).wait_recv()
        for cp in sends:
            cp.wait_send()

    outs = pl.pallas_call(
        body,
        out_shape=tuple(_sds((N_CHIPS,) + a.shape, a.dtype) for a in arrays),
        in_specs=_hbm_specs(n),
        out_specs=tuple(_hbm_specs(n)),
        scratch_shapes=[pltpu.SemaphoreType.DMA((n, 6)), pltpu.SemaphoreType.DMA((n, 6))],
        name="gather_weights",
    )(*arrays)
    me = 2 * lax.axis_index("x") + lax.axis_index("y")
    return tuple(lax.dynamic_update_index_in_dim(o, a, me, 0) for o, a in zip(outs, arrays))


def _gather_small(v):
    m_per, ncol = v.shape

    def body(x_ref, out_ref, send_sems, recv_sems, local_sem):
        x, y, c, chips = _mesh_place()
        me, sibling = (x, y, c), (x, y, 1 - c)

        def rows(px, py, pc):
            return out_ref.at[pl.ds((4 * px + 2 * py + pc) * m_per, m_per), :]

        def copy(k, block, to, src=None):
            return pltpu.make_async_remote_copy(src_ref=rows(*block) if src is None else src, dst_ref=rows(*block),
                                                send_sem=send_sems.at[k], recv_sem=recv_sems.at[k],
                                                device_id=to, device_id_type=MESH)

        mine = pltpu.make_async_copy(x_ref, rows(*me), local_sem)
        mine.start()
        first = [copy(0, me, sibling, src=x_ref)]
        first += [copy(1 + j, me, (*chip, c), src=x_ref) for j, chip in enumerate(chips)]
        for cp in first:
            cp.start()
        passed = [copy(4 + j, (*chip, c), sibling) for j, chip in enumerate(chips)]
        for j, chip in enumerate(chips):
            copy(1 + j, (*chip, c), me).wait_recv()
            passed[j].start()
        copy(0, sibling, me).wait_recv()
        for j, chip in enumerate(chips):
            copy(4 + j, (*chip, 1 - c), me).wait_recv()
        for cp in first + passed:
            cp.wait_send()
        mine.wait()

    return pl.pallas_call(
        body,
        out_shape=_sds((N_DEV * m_per, ncol), v.dtype),
        in_specs=[pl.BlockSpec(memory_space=pltpu.VMEM)],
        out_specs=pl.BlockSpec(memory_space=pltpu.VMEM),
        scratch_shapes=[pltpu.SemaphoreType.DMA((7,)), pltpu.SemaphoreType.DMA((7,)), pltpu.SemaphoreType.DMA],
        name="gather_small",
    )(v)


def _half_shape(shape2d, axis):
    return (shape2d[0] // 2, shape2d[1]) if axis == 0 else (shape2d[0], shape2d[1] // 2)


def _exchange_sibling(name, grads, axes):
    n = len(grads)

    def body(*refs):
        ins, outs = refs[:n], refs[n:2 * n]
        send_sems, recv_sems = refs[2 * n:]
        x, y, c, _ = _mesh_place()
        copies = []
        for a in range(n):
            src = ins[a].at[(slice(None),) + _half(grads[a].shape[1:], axes[a], 1 - c)]
            cp = pltpu.make_async_remote_copy(src_ref=src, dst_ref=outs[a], send_sem=send_sems.at[a],
                                              recv_sem=recv_sems.at[a], device_id=(x, y, 1 - c), device_id_type=MESH)
            cp.start()
            copies.append(cp)
        for cp in copies:
            cp.wait()

    return pl.pallas_call(
        body,
        out_shape=tuple(_sds((N_CHIPS,) + _half_shape(g.shape[1:], ax), g.dtype) for g, ax in zip(grads, axes)),
        in_specs=_hbm_specs(n),
        out_specs=tuple(_hbm_specs(n)),
        scratch_shapes=[pltpu.SemaphoreType.DMA((n,)), pltpu.SemaphoreType.DMA((n,))],
        name=name,
    )(*grads)


_HBM = pl.BlockSpec(memory_space=pltpu.HBM)
_SEM = pl.BlockSpec(memory_space=pltpu.SEMAPHORE)
_EFFECT = pltpu.SideEffectType.DATAFLOW_SIDE_EFFECTING


def _chip_copies(kind, srcs, lands, send_sems, recv_sems):
    x, y, c, chips = _mesh_place()
    copies = []
    for a in range(len(srcs)):
        for j, (px, py) in enumerate(chips):
            if kind == "gather" and len(lands[a].shape) == 2:
                cw = srcs[a].shape[1]
                src, dst = srcs[a], lands[a].at[:, pl.ds(pl.multiple_of((2 * x + y) * cw, LANES), cw)]
            elif kind == "gather":
                src, dst = srcs[a], lands[a].at[2 * x + y]
            else:
                src, dst = srcs[a].at[j], lands[a].at[j]
            copies.append(pltpu.make_async_remote_copy(src_ref=src, dst_ref=dst, send_sem=send_sems.at[3 * a + j],
                                                       recv_sem=recv_sems.at[3 * a + j], device_id=(px, py, c),
                                                       device_id_type=MESH))
    return copies


def _chips_start(name, kind, srcs, as_columns=()):
    n = len(srcs)
    lands = []
    for a, s in enumerate(srcs):
        if kind == "reduce":
            shape = (3,) + s.shape[1:]
        elif a in as_columns:
            shape = (s.shape[0], N_CHIPS * s.shape[1])
        else:
            shape = (N_CHIPS,) + s.shape
        lands.append(lax.empty(shape, s.dtype))

    def body(*refs):
        for cp in _chip_copies(kind, refs[:n], refs[n:2 * n], refs[2 * n], refs[2 * n + 1]):
            cp.start()
        refs[-1][...] = jnp.zeros_like(refs[-1])

    outs = pl.pallas_call(
        body,
        out_shape=(pltpu.SemaphoreType.DMA((3 * n,)), pltpu.SemaphoreType.DMA((3 * n,)),
                   *[pltpu.HBM(v.shape, v.dtype) for v in (*srcs, *lands)], _sds((8, LANES), F32)),
        in_specs=[_HBM] * (2 * n),
        out_specs=(_SEM, _SEM, *[_HBM] * (2 * n), pl.BlockSpec(memory_space=pltpu.VMEM)),
        input_output_aliases={i: 2 + i for i in range(2 * n)},
        compiler_params=pltpu.CompilerParams(has_side_effects=_EFFECT),
        name=name,
    )(*[pltpu.with_memory_space_constraint(v, pltpu.HBM) for v in (*srcs, *lands)])
    return outs[:-1], outs[-1]


def _chips_wait(name, kind, handles, after):
    send_sems, recv_sems, *thru = handles
    n = len(thru) // 2

    def body(*refs):
        for cp in _chip_copies(kind, refs[:n], refs[n:2 * n], refs[2 * n], refs[2 * n + 1]):
            cp.wait_send()
            cp.wait_recv()

    outs = pl.pallas_call(
        body,
        out_shape=tuple(pltpu.HBM(v.shape, v.dtype) for v in thru),
        in_specs=[_HBM] * (2 * n) + [_SEM, _SEM, pl.BlockSpec(memory_space=pl.ANY)],
        out_specs=tuple([_HBM] * (2 * n)),
        input_output_aliases={i: i for i in range(2 * n)},
        compiler_params=pltpu.CompilerParams(has_side_effects=_EFFECT),
        name=name,
    )(*thru, send_sems, recv_sems, after)
    return outs[n:]


def _share_sibling(name, shards, axes):
    n = len(shards)

    def body(*refs):
        ins, outs = refs[:n], refs[n:2 * n]
        send_sems, recv_sems = refs[2 * n:]
        x, y, c, _ = _mesh_place()
        started = []
        for a in range(n):
            mine = _half(shards[a].shape, axes[a], c)
            theirs = _half(shards[a].shape, axes[a], 1 - c)
            cp = pltpu.make_async_remote_copy(src_ref=ins[a].at[mine], dst_ref=outs[a].at[mine],
                                              send_sem=send_sems.at[a], recv_sem=recv_sems.at[a],
                                              device_id=(x, y, 1 - c), device_id_type=MESH)
            cp.start()
            arrival = pltpu.make_async_remote_copy(src_ref=ins[a].at[theirs], dst_ref=outs[a].at[theirs],
                                                   send_sem=send_sems.at[a], recv_sem=recv_sems.at[a],
                                                   device_id=(x, y, 1 - c), device_id_type=MESH)
            started.append((cp, arrival))
        for cp, arrival in started:
            arrival.wait_recv()
            cp.wait_send()

    return pl.pallas_call(
        body,
        out_shape=tuple(_sds(s.shape, s.dtype) for s in shards),
        in_specs=_hbm_specs(n),
        out_specs=tuple(_hbm_specs(n)),
        scratch_shapes=[pltpu.SemaphoreType.DMA((n,)), pltpu.SemaphoreType.DMA((n,))],
        input_output_aliases={a: a for a in range(n)},
        name=name,
    )(*shards)


def _pair_sum(name, place, g, got, axis):
    hr, hc = got.shape[1:]

    def body(place_ref, g_ref, got_ref, o_ref):
        o_ref[...] = (g_ref[...] + got_ref[...]).astype(o_ref.dtype)

    blk = (None, hr, hc)
    mine = (lambda j, pr: (pr[2 + j], pr[1], 0)) if axis == 0 else (lambda j, pr: (pr[2 + j], 0, pr[1]))
    return pl.pallas_call(
        body,
        out_shape=_sds((N_CHIPS - 1, hr, hc), BF16),
        grid_spec=pltpu.PrefetchScalarGridSpec(
            num_scalar_prefetch=1,
            grid=(N_CHIPS - 1,),
            in_specs=[pl.BlockSpec(blk, mine), pl.BlockSpec(blk, lambda j, pr: (pr[2 + j], 0, 0))],
            out_specs=pl.BlockSpec(blk, lambda j, pr: (j, 0, 0)),
        ),
        compiler_params=pltpu.CompilerParams(dimension_semantics=("parallel",)),
        name=name,
    )(place, g, got)


def _chip_sum(name, place, g, got, arrivals, axis):
    _, r, cdim = g.shape
    hr, hc = got.shape[1:]

    def body(place_ref, g_ref, got_ref, arr_ref, o_ref):
        acc = g_ref[...] + got_ref[...]
        for j in range(3):
            acc = acc + arr_ref[j].astype(F32)
        o_ref[...] = acc

    blk = (None, hr, hc)
    mine = (lambda i, pr: (pr[0], pr[1], 0)) if axis == 0 else (lambda i, pr: (pr[0], 0, pr[1]))
    dest = (lambda i, pr: (pr[1], 0)) if axis == 0 else (lambda i, pr: (0, pr[1]))
    return pl.pallas_call(
        body,
        out_shape=_sds((r, cdim), F32),
        grid_spec=pltpu.PrefetchScalarGridSpec(
            num_scalar_prefetch=1,
            grid=(1,),
            in_specs=[
                pl.BlockSpec(blk, mine),
                pl.BlockSpec(blk, lambda i, pr: (pr[0], 0, 0)),
                pl.BlockSpec((3, hr, hc), lambda i, pr: (0, 0, 0)),
            ],
            out_specs=pl.BlockSpec((hr, hc), dest),
        ),
        compiler_params=pltpu.CompilerParams(dimension_semantics=("arbitrary",)),
        name=name,
    )(place, g, got, arrivals)


def _device_sum(name, gathered):
    m_per = gathered.shape[0] // N_DEV

    def body(g_ref, o_ref):
        acc = g_ref[0:m_per, :]
        for dev in range(1, N_DEV):
            acc = acc + g_ref[dev * m_per:(dev + 1) * m_per, :]
        o_ref[...] = acc

    return pl.pallas_call(body, out_shape=_sds((m_per, gathered.shape[1]), F32), name=name)(gathered)


def _adamw(name, w, g, m, v):
    r, cdim = w.shape
    if r % 8 == 0:
        tr, tcol = _tile(r, 256, 8), cdim
    else:
        tr, tcol = r, (_tile(cdim, 256, LANES) if cdim % LANES == 0 else cdim)
    blk = pl.BlockSpec((tr, tcol), lambda i, j: (i, j))
    grid = (r // tr, cdim // tcol)
    bc1 = 1.0 - ADAM_B1 ** ADAM_STEP
    bc2 = 1.0 - ADAM_B2 ** ADAM_STEP

    def body(w_ref, g_ref, m_ref, v_ref, d_ref, nm_ref, nv_ref):
        gv = g_ref[...]
        nm = ADAM_B1 * m_ref[...] + (1.0 - ADAM_B1) * gv
        nv = ADAM_B2 * v_ref[...] + (1.0 - ADAM_B2) * (gv * gv)
        d_ref[...] = -ADAM_LR * ((nm / bc1) / (jnp.sqrt(nv / bc2) + ADAM_EPS) + ADAM_WD * w_ref[...])
        nm_ref[...] = nm
        nv_ref[...] = nv

    shape = _sds(w.shape, F32)
    return pl.pallas_call(
        body,
        out_shape=(shape, shape, shape),
        grid=grid,
        in_specs=[blk] * 4,
        out_specs=(blk, blk, blk),
        compiler_params=pltpu.CompilerParams(dimension_semantics=("parallel", "parallel")),
        name=name,
    )(w, g, m, v)


def _cat_cols(g):
    return jnp.transpose(g, (1, 0, 2)).reshape(g.shape[1], N_CHIPS * g.shape[2])


def _split_cols(a):
    r, c4 = a.shape
    return jnp.transpose(a.reshape(r, N_CHIPS, c4 // N_CHIPS), (1, 0, 2))


def _local_step(x, target, w_int, late_weights, cmw, cfw, g1, b_f, b_gate, g2, gf,
                ffn_grads_ready, mix_grads_ready):
    batch, seq, d = x.shape
    t = batch * seq
    cw = d // 2
    fh = cfw.shape[1] // 2
    tc = LANES
    nct = cw // tc
    tq = min(512, seq)
    pc_w, qkv_w, gl_w = 3 * cw, 3 * ATTN_WIDTH, 2 * d
    qkv_off, gl_off, f_off = pc_w, pc_w + qkv_w, pc_w + qkv_w + gl_w
    width = f_off + F_PAD
    f_col = pc_w + qkv_w

    w_pc = w_int[:pc_w].reshape(3, nct, tc, d).transpose(1, 0, 2, 3).reshape(pc_w, d)
    w_qkv = w_int[pc_w:f_col].reshape(3, HEAD_PAIRS, LANES, d).transpose(1, 0, 2, 3).reshape(qkv_w, d)
    w_f = jnp.pad(w_int[f_col:f_col + HEADS], ((0, F_PAD - HEADS), (0, 0)))
    w_inp = jnp.concatenate([w_pc, w_qkv, w_int[f_col + HEADS:], w_f], axis=0)
    bf_pad = jnp.pad(b_f, ((0, 0), (0, F_PAD - HEADS)))

    x2d = x.reshape(t, d)
    tgt2d = target.reshape(t, d)

    h1 = _rms_fwd("norm_mix", x2d, g1)
    pc, qkv, gl, fl = _project("proj_in", h1, w_inp, [(pc_w, BF16), (qkv_w, BF16), (gl_w, BF16), (F_PAD, F32)])
    a_c = _conv_fwd("conv_mix", pc, cmw, batch, seq, tc)
    f_cum = _forget_fwd("forget_cumsum", fl, bf_pad, batch, seq)
    frow = f_cum.reshape(batch, HEAD_PAIRS, 2, seq)
    o, lse = _attn_fwd("attn_fwd", qkv, frow, batch, seq, tq)
    w_oc, w_oa, w_o, w_up, w_down = late_weights(o)
    ycat = _mm("out_conv", a_c, w_oc, "nn", BF16, m=t, n=d, k=cw, o_off=0, o_width=2 * d)
    ycat = _mm("out_attn", o, w_oa, "nn", BF16, m=t, n=d, k=ATTN_WIDTH, out=ycat, o_off=d)
    mg = _merge_fwd("gate_merge", ycat, gl, b_gate)
    x2 = _mm("mix_out", mg, w_o, "nn", F32, m=t, n=d, k=d, add=x2d)
    h2 = _rms_fwd("norm_ffn", x2, g2)
    tcf = min(2 * LANES, fh)
    hmid, ua, ub, ffn_a, ffn_b = _ffn_up_act("ffn_up_act", h2, w_up, cfw, batch, seq, tcf)
    x3 = _mm("ffn_down", hmid, w_down, "nn", F32, m=t, n=d, k=fh, add=x2, tk=4096)

    dx3, dx3b, loss_row, d_gf = _final_loss("final_loss", x3, gf.reshape(1, d), tgt2d)
    dw_down = _mm("dw_down", hmid, dx3b, "tn", F32, m=fh, n=d, k=t, tm=256, tk=8192)
    du_a, du_b, d_cfw = _ffn_bwd("d_ffn", dx3b, w_down, ua, ub, ffn_a, ffn_b, cfw, batch, seq, tcf)
    ws = w_up.shape[1] // N_CHIPS
    dh2 = _mm("d_norm_ffn_a", du_a, w_up, "nt", BF16, m=t, n=d, k=fh, b_off=0, tk=4096)
    dh2 = _mm("d_norm_ffn_b", du_b, w_up, "nt", BF16, m=t, n=d, k=fh, b_off=fh, add=dh2, tk=4096)
    dw_up = _mm("dw_up_a", h2, du_a, "tn", F32, m=d, n=fh, k=t, tm=512, tn=ws, tk=4096, o3=N_CHIPS)
    dw_up = _mm("dw_up_b", h2, du_b, "tn", F32, m=d, n=fh, k=t, tm=512, tn=ws, tk=4096, o3=N_CHIPS, out=dw_up,
                o_off=fh)
    token = ffn_grads_ready(dw_up, dw_down)
    if token is not None:
        g2 = g2 + token[0:1, 0:1]
    dx2, d_g2 = _rms_bwd("d_norm_ffn", x2, dh2, g2, dx3)
    dm = _mm("d_merge", dx2, w_o, "nt", BF16, m=t, n=d, k=d)
    dw_o = _mm("dw_o", mg, dx2, "tn", F32, m=d, n=d, k=t, tk=2048)
    dproj, dycat, d_bg = _merge_bwd("d_gate_merge", dm, ycat, gl, b_gate, width, gl_off)
    da_c = _mm("d_conv_out", dycat, w_oc, "nt", BF16, m=t, n=cw, k=d, a_off=0)
    do = _mm("d_attn_out", dycat, w_oa, "nt", BF16, m=t, n=ATTN_WIDTH, k=d, a_off=d)
    dw_oc = _mm("dw_out_conv", a_c, dycat, "tn", F32, m=cw, n=d, k=t, b_off=0, tk=2048)
    dw_oa = _mm("dw_out_attn", o, dycat, "tn", F32, m=ATTN_WIDTH, n=d, k=t, b_off=d, tk=2048)
    dproj, d_cmw = _conv_bwd("d_conv_mix", da_c, pc, cmw, dproj, batch, seq, tc)
    dproj, d_fkey, d_fquery = _attn_bwd("attn_bwd", qkv, do, o, lse, frow, dproj, qkv_off, batch, seq, tq)
    d_fquery = jnp.pad(jnp.transpose(d_fquery, (1, 0, 2)).reshape(t, HEADS), ((0, 0), (0, LANES - HEADS)))
    dproj, d_bf = _forget_bwd("d_forget", d_fkey.reshape(batch, HEADS, seq), d_fquery, fl, bf_pad, dproj, f_off,
                              batch, seq)
    dw_inp = _mm("dw_in", dproj, h1, "tn", F32, m=width, n=d, k=t, tm=256, tk=8192)
    d_pc = dw_inp[:pc_w].reshape(nct, 3, tc, d).transpose(1, 0, 2, 3).reshape(pc_w, d)
    d_qkv = dw_inp[qkv_off:gl_off].reshape(HEAD_PAIRS, 3, LANES, d).transpose(1, 0, 2, 3).reshape(qkv_w, d)
    dw_int = jnp.concatenate([d_pc, d_qkv, dw_inp[f_off:f_off + HEADS], dw_inp[gl_off:f_off]], axis=0)
    token = mix_grads_ready(dw_int, dw_oc, dw_oa, dw_o)
    dh1 = _mm("d_norm_mix", dproj, w_inp, "nn", BF16, m=t, n=d, k=width, tm=512, tk=8192, dep=token)
    grad_x, d_g1 = _rms_bwd("d_norm_mix_x", x2d, dh1, g1, dx2)
    smalls = (d_g1, d_g2, d_gf, d_bg, d_bf, d_cmw, d_cfw)
    return loss_row[0, 0], grad_x.reshape(batch, seq, d), smalls


def _pack_small(parts):
    flat = [p.reshape(-1) for p in parts]
    sizes = [f.shape[0] for f in flat]
    total = sum(sizes)
    padded = -(-total // (8 * LANES)) * (8 * LANES)
    vec = jnp.concatenate(flat + [jnp.zeros((padded - total,), F32)])
    offsets = [sum(sizes[:i]) for i in range(len(sizes))]
    return vec.reshape(padded // LANES, LANES), offsets


def kernel(x, norm_mix_g, w_in, b_f, b_gate, conv_mix_w, w_out_conv, w_out_attn, w_o, norm_ffn_g, w_up, conv_ffn_w, w_down, norm_f_g, loss_target, m_norm_mix_g, m_w_in, m_b_f, m_b_gate, m_conv_mix_w, m_w_out_conv, m_w_out_attn, m_w_o, m_norm_ffn_g, m_w_up, m_conv_ffn_w, m_w_down, m_norm_f_g, v_norm_mix_g, v_w_in, v_b_f, v_b_gate, v_conv_mix_w, v_w_out_conv, v_w_out_attn, v_w_o, v_norm_ffn_g, v_w_up, v_conv_ffn_w, v_w_down, v_norm_f_g):
    d = x.shape[-1]
    chip = 2 * lax.axis_index("x") + lax.axis_index("y")
    xi, yi = lax.axis_index("x"), lax.axis_index("y")
    peers = [2 * px + py for px, py in ((1 - xi, yi), (xi, 1 - yi), (1 - xi, 1 - yi))]
    place = jnp.stack([chip, lax.axis_index("c"), *peers]).astype(jnp.int32)

    t_in, t_m_in, t_v_in = (jnp.transpose(w[0]) for w in (w_in, m_w_in, v_w_in))

    def row_shards(a):
        return a.reshape(N_CHIPS, a.shape[0] // N_CHIPS, a.shape[1])

    def stacked(a):
        return a.reshape(N_CHIPS * a.shape[1], a.shape[2])

    a_in, a_cmw, a_cfw = _gather_weights([t_in.astype(BF16)], (1,), [conv_mix_w[0], conv_ffn_w[0]])
    late = [w[0].astype(BF16) for w in (w_out_conv, w_out_attn, w_o, w_up, w_down)]
    late_handles, late_token = _chips_start("gather_late_start", "gather", late, as_columns=(0, 1, 3))

    def late_weights(after):
        lands = _chips_wait("gather_late_wait", "gather", late_handles, after)
        f_oc, f_oa, f_up = (lax.dynamic_update_slice(lands[a], late[a], (0, chip * late[a].shape[1]))
                            for a in (0, 1, 3))
        a_o, a_down = (lax.dynamic_update_index_in_dim(lands[a], late[a], chip, 0) for a in (2, 4))
        return f_oc, f_oa, stacked(a_o), f_up, stacked(a_down)

    pending = []

    def reduce_start(tag, names, grads, axes):
        got = _exchange_sibling("exchange_sibling_" + tag, grads, axes)
        sums = [_pair_sum("pair_sum_" + nm, place, g, r, ax) for nm, g, r, ax in zip(names, grads, got, axes)]
        handles, token = _chips_start("exchange_chips_start_" + tag, "reduce", sums)
        pending.append((tag, names, grads, axes, got, handles))
        return token

    def ffn_grads_ready(dw_up, dw_down):
        return reduce_start("ffn", ("w_up", "w_down"), [dw_up, row_shards(dw_down)], (0, 0))

    def mix_grads_ready(dw_int, dw_oc, dw_oa, dw_o):
        return reduce_start("mix", ("w_in", "w_out_conv", "w_out_attn", "w_o"),
                            [row_shards(dw_int), _split_cols(dw_oc), _split_cols(dw_oa), row_shards(dw_o)],
                            (1, 0, 0, 0))

    loss_local, grad_x, smalls = _local_step(
        x, loss_target, stacked(a_in), late_weights, _cat_cols(a_cmw),
        _cat_cols(a_cfw), norm_mix_g + late_token[0:1, 0:1], b_f, b_gate, norm_ffn_g, norm_f_g,
        ffn_grads_ready, mix_grads_ready)

    reduced = {}
    for tag, names, grads, axes, got, handles in pending:
        arrivals = _chips_wait("exchange_chips_wait_" + tag, "reduce", handles, grad_x)
        halves = [_chip_sum("chip_sum_" + nm, place, g, r, arr, ax)
                  for nm, g, r, arr, ax in zip(names, grads, got, arrivals, axes)]
        reduced.update(zip(names, _share_sibling("share_sibling_" + tag, halves, axes)))
    g_in, g_oc, g_oa, g_o, g_up, g_down = (
        reduced[nm] for nm in ("w_in", "w_out_conv", "w_out_attn", "w_o", "w_up", "w_down"))

    smalls = (*smalls, loss_local.reshape(1, 1))
    packed, offs = _pack_small(smalls)
    total = _device_sum("device_sum", _gather_small(packed)).reshape(-1)
    shapes = [s.shape for s in smalls]
    d_g1, d_g2, d_gf, d_bg, d_bf, d_cmw, d_cfw, loss = [
        total[o:o + math.prod(sh)].reshape(sh) for o, sh in zip(offs, shapes)]
    loss = loss[0, 0]
    d_bf = d_bf[:, :HEADS]
    cw_s, cf_s = conv_mix_w.shape[2], conv_ffn_w.shape[2]
    d_cmw = lax.dynamic_slice(d_cmw, (0, chip * cw_s), (3, cw_s))
    d_cfw = lax.dynamic_slice(d_cfw, (0, chip * cf_s), (3, cf_s))

    order = [
        ("norm_mix_g", norm_mix_g[0:1], d_g1, m_norm_mix_g, v_norm_mix_g),
        ("w_in", t_in, g_in, t_m_in, t_v_in),
        ("b_f", b_f, d_bf, m_b_f, v_b_f),
        ("b_gate", b_gate, d_bg, m_b_gate, v_b_gate),
        ("conv_mix_w", conv_mix_w[0], d_cmw, m_conv_mix_w[0], v_conv_mix_w[0]),
        ("w_out_conv", w_out_conv[0], g_oc, m_w_out_conv[0], v_w_out_conv[0]),
        ("w_out_attn", w_out_attn[0], g_oa, m_w_out_attn[0], v_w_out_attn[0]),
        ("w_o", w_o[0], g_o, m_w_o[0], v_w_o[0]),
        ("norm_ffn_g", norm_ffn_g, d_g2, m_norm_ffn_g, v_norm_ffn_g),
        ("w_up", w_up[0], g_up, m_w_up[0], v_w_up[0]),
        ("conv_ffn_w", conv_ffn_w[0], d_cfw, m_conv_ffn_w[0], v_conv_ffn_w[0]),
        ("w_down", w_down[0], g_down, m_w_down[0], v_w_down[0]),
        ("norm_f_g", norm_f_g.reshape(1, d), d_gf, m_norm_f_g.reshape(1, d), v_norm_f_g.reshape(1, d)),
    ]
    out_shapes = [norm_mix_g.shape, w_in.shape, b_f.shape, b_gate.shape, conv_mix_w.shape, w_out_conv.shape,
                  w_out_attn.shape, w_o.shape, norm_ffn_g.shape, w_up.shape, conv_ffn_w.shape, w_down.shape,
                  norm_f_g.shape]
    g_out, d_out, m_out, v_out = [], [], [], []
    for (nm, w, g, m, v), sh in zip(order, out_shapes):
        g = g.reshape(w.shape)
        delta, new_m, new_v = _adamw("adamw_" + nm, w, g, m.reshape(w.shape), v.reshape(w.shape))
        for dst, val in ((g_out, g), (d_out, delta), (m_out, new_m), (v_out, new_v)):
            dst.append((jnp.transpose(val) if nm == "w_in" else val).reshape(sh))
    return (loss, grad_x, *g_out, *d_out, *m_out, *v_out)
```

```python
import math

import jax
import jax.numpy as jnp
from jax import lax
from jax.experimental import pallas as pl
from jax.experimental.pallas import tpu as pltpu

F32 = jnp.float32
BF16 = jnp.bfloat16
MESH = pl.DeviceIdType.MESH

EPS = 1e-6
HEADS = 8
HEAD_DIM = 64
ATTN_WIDTH = HEADS * HEAD_DIM
HEAD_PAIRS = HEADS // 2
LANES = 128
F_PAD = 2 * LANES
NEG_BIG = -1e30
N_CHIPS = 4
N_DEV = 8

ADAM_LR = 0.001
ADAM_B1 = 0.9
ADAM_B2 = 0.999
ADAM_EPS = 1e-08
ADAM_WD = 0.01
ADAM_STEP = 10

_DIMS = {
    "nn": (((1,), (0,)), ((), ())),
    "nt": (((1,), (1,)), ((), ())),
    "tn": (((0,), (0,)), ((), ())),
}


def _tile(n, target, mult, also=()):
    best = None
    for t in range(mult, n + 1, mult):
        if n % t == 0 and t <= target and all(o % t == 0 for o in also):
            best = t
    if best is None:
        assert all(o == 0 for o in also), (n, target, mult, also)
        return n
    return best


def _sds(shape, dtype):
    return jax.ShapeDtypeStruct(shape, dtype)


def _mm(name, a, b, mode, out_dtype, *, m, n, k, a_off=0, b_off=0, out=None, o_off=0,
        o_width=None, o3=None, add=None, dep=None, tm=1024, tn=2048, tk=2048):
    if mode == "nn":
        tm = _tile(m, tm, 16)
        tk = _tile(k, tk, LANES, (a_off,))
        tn = _tile(n, tn, LANES, (b_off, o_off))
        a_spec = pl.BlockSpec((tm, tk), lambda i, j, kk: (i, a_off // tk + kk))
        b_spec = pl.BlockSpec((tk, tn), lambda i, j, kk: (kk, b_off // tn + j))
    elif mode == "nt":
        tm = _tile(m, tm, 16)
        tk = _tile(k, tk, LANES, (a_off, b_off))
        tn = _tile(n, tn, LANES, (o_off,))
        a_spec = pl.BlockSpec((tm, tk), lambda i, j, kk: (i, a_off // tk + kk))
        b_spec = pl.BlockSpec((tn, tk), lambda i, j, kk: (j, b_off // tk + kk))
    else:
        tm = _tile(m, tm, LANES, (a_off,))
        tk = _tile(k, tk, 16)
        tn = _tile(n, tn, LANES, (b_off, o_off))
        a_spec = pl.BlockSpec((tk, tm), lambda i, j, kk: (kk, a_off // tm + i))
        b_spec = pl.BlockSpec((tk, tn), lambda i, j, kk: (kk, b_off // tn + j))
    assert m % tm == 0 and n % tn == 0 and k % tk == 0, (name, tm, tn, tk)
    nk = k // tk
    if o3 is not None:
        o_spec = pl.BlockSpec((None, tm, tn), lambda i, j, kk: (o_off // tn + j, i, 0))
        out_sds = _sds((o3, m, tn), out_dtype)
    else:
        o_spec = pl.BlockSpec((tm, tn), lambda i, j, kk: (i, o_off // tn + j))
        width = o_width if o_width is not None else (out.shape[1] if out is not None else n)
        out_sds = _sds((m, width), out_dtype)
    use_acc = nk > 1 and out_dtype != F32
    dims = _DIMS[mode]
    has_add, has_out = add is not None, out is not None

    def body(*refs):
        a_ref, b_ref = refs[0], refs[1]
        pos = 2
        add_ref = None
        if has_add:
            add_ref = refs[pos]
            pos += 1
        if has_out:
            pos += 1
        if dep is not None:
            pos += 1
        o_ref = refs[pos]
        acc_ref = refs[pos + 1] if use_acc else None
        part = lax.dot_general(a_ref[...].astype(BF16), b_ref[...].astype(BF16), dims,
                               preferred_element_type=F32)
        if nk == 1:
            if has_add:
                part = part + add_ref[...]
            o_ref[...] = part.astype(o_ref.dtype)
            return
        kk = pl.program_id(2)
        tgt = acc_ref if use_acc else o_ref

        @pl.when(kk == 0)
        def _():
            tgt[...] = part + add_ref[...] if has_add else part

        @pl.when(kk > 0)
        def _():
            tgt[...] += part

        if use_acc:
            @pl.when(kk == nk - 1)
            def _():
                o_ref[...] = acc_ref[...].astype(o_ref.dtype)

    operands, in_specs = [a, b], [a_spec, b_spec]
    if has_add:
        operands.append(add)
        in_specs.append(pl.BlockSpec((tm, tn), lambda i, j, kk: (i, j)))
    aliases = {}
    if has_out:
        aliases = {len(operands): 0}
        operands.append(out)
        in_specs.append(pl.BlockSpec(memory_space=pl.ANY))
    if dep is not None:
        operands.append(dep)
        in_specs.append(pl.BlockSpec(memory_space=pl.ANY))
    return pl.pallas_call(
        body,
        out_shape=out_sds,
        grid=(m // tm, n // tn, nk),
        in_specs=in_specs,
        out_specs=o_spec,
        scratch_shapes=[pltpu.VMEM((tm, tn), F32)] if use_acc else [],
        input_output_aliases=aliases,
        compiler_params=pltpu.CompilerParams(dimension_semantics=("parallel", "parallel", "arbitrary")),
        name=name,
    )(*operands)


def _project(name, h, w_t, groups):
    t, d = h.shape
    tm = _tile(t, 512, 16)
    offs = [sum(n for n, _ in groups[:i]) for i in range(len(groups))]

    def body(h_ref, w_ref, *o_refs):
        hv = h_ref[...]
        for (n, _), off, o_ref in zip(groups, offs, o_refs):
            o_ref[...] = _dot(hv, w_ref[off:off + n, :], "nt").astype(o_ref.dtype)

    return pl.pallas_call(
        body,
        out_shape=tuple(_sds((t, n), dt) for n, dt in groups),
        grid=(t // tm,),
        in_specs=[pl.BlockSpec((tm, d), lambda i: (i, 0)), pl.BlockSpec(w_t.shape, lambda i: (0, 0))],
        out_specs=tuple(pl.BlockSpec((tm, n), lambda i: (i, 0)) for n, _ in groups),
        compiler_params=pltpu.CompilerParams(dimension_semantics=("parallel",)),
        name=name,
    )(h, w_t)


def _branch_out(name, a1, w1, a2, w2):
    t = a1.shape[0]
    n1, n2 = w1.shape[1], w2.shape[1]
    tm = _tile(t, 1024, 16)

    def body(a1_ref, w1_ref, a2_ref, w2_ref, o_ref):
        o_ref[:, 0:n1] = _dot(a1_ref[...], w1_ref[...], "nn").astype(o_ref.dtype)
        o_ref[:, n1:n1 + n2] = _dot(a2_ref[...], w2_ref[...], "nn").astype(o_ref.dtype)

    whole = lambda w: pl.BlockSpec(w.shape, lambda i: (0, 0))
    rows = lambda a: pl.BlockSpec((tm, a.shape[1]), lambda i: (i, 0))
    return pl.pallas_call(
        body,
        out_shape=_sds((t, n1 + n2), BF16),
        grid=(t // tm,),
        in_specs=[rows(a1), whole(w1), rows(a2), whole(w2)],
        out_specs=pl.BlockSpec((tm, n1 + n2), lambda i: (i, 0)),
        compiler_params=pltpu.CompilerParams(dimension_semantics=("parallel",)),
        name=name,
    )(a1, w1, a2, w2)


def _branch_out_bwd(name, dy, w1, w2):
    t, d2 = dy.shape
    d = d2 // 2
    tm = _tile(t, 1024, 16)

    def body(dy_ref, w1_ref, w2_ref, o1_ref, o2_ref):
        o1_ref[...] = _dot(dy_ref[:, 0:d], w1_ref[...], "nt").astype(o1_ref.dtype)
        o2_ref[...] = _dot(dy_ref[:, d:d2], w2_ref[...], "nt").astype(o2_ref.dtype)

    whole = lambda w: pl.BlockSpec(w.shape, lambda i: (0, 0))
    return pl.pallas_call(
        body,
        out_shape=(_sds((t, w1.shape[0]), BF16), _sds((t, w2.shape[0]), BF16)),
        grid=(t // tm,),
        in_specs=[pl.BlockSpec((tm, d2), lambda i: (i, 0)), whole(w1), whole(w2)],
        out_specs=(pl.BlockSpec((tm, w1.shape[0]), lambda i: (i, 0)), pl.BlockSpec((tm, w2.shape[0]), lambda i: (i, 0))),
        compiler_params=pltpu.CompilerParams(dimension_semantics=("parallel",)),
        name=name,
    )(dy, w1, w2)


def _branch_out_dw(name, a1, a2, dy):
    t, d2 = dy.shape
    d = d2 // 2
    tk = _tile(t, 2048, 16)

    def body(a1_ref, a2_ref, dy_ref, o1_ref, o2_ref):
        @pl.when(pl.program_id(0) == 0)
        def _():
            o1_ref[...] = jnp.zeros_like(o1_ref)
            o2_ref[...] = jnp.zeros_like(o2_ref)

        o1_ref[...] += _dot(a1_ref[...], dy_ref[:, 0:d], "tn")
        o2_ref[...] += _dot(a2_ref[...], dy_ref[:, d:d2], "tn")

    rows = lambda a: pl.BlockSpec((tk, a.shape[1]), lambda k: (k, 0))
    acc = lambda a: pl.BlockSpec((a.shape[1], d), lambda k: (0, 0))
    return pl.pallas_call(
        body,
        out_shape=(_sds((a1.shape[1], d), F32), _sds((a2.shape[1], d), F32)),
        grid=(t // tk,),
        in_specs=[rows(a1), rows(a2), rows(dy)],
        out_specs=(acc(a1), acc(a2)),
        compiler_params=pltpu.CompilerParams(dimension_semantics=("arbitrary",)),
        name=name,
    )(a1, a2, dy)


def _rms_fwd(name, x, g):
    t, d = x.shape
    tm = _tile(t, 512, 16)

    def body(x_ref, g_ref, o_ref):
        xv = x_ref[...]
        r = lax.rsqrt(jnp.mean(xv * xv, axis=-1, keepdims=True) + EPS)
        o_ref[...] = ((xv * r) * g_ref[...]).astype(o_ref.dtype)

    return pl.pallas_call(
        body,
        out_shape=_sds((t, d), BF16),
        grid=(t // tm,),
        in_specs=[pl.BlockSpec((tm, d), lambda i: (i, 0)), pl.BlockSpec((1, d), lambda i: (0, 0))],
        out_specs=pl.BlockSpec((tm, d), lambda i: (i, 0)),
        compiler_params=pltpu.CompilerParams(dimension_semantics=("parallel",)),
        name=name,
    )(x, g)


def _rms_bwd(name, x, dh, g, res):
    t, d = x.shape
    tm = _tile(t, 512, 16)

    def body(x_ref, dh_ref, g_ref, res_ref, dx_ref, dg_ref):
        xv = x_ref[...]
        r = lax.rsqrt(jnp.mean(xv * xv, axis=-1, keepdims=True) + EPS)
        xh = xv * r
        dhv = dh_ref[...].astype(F32)
        dxh = dhv * g_ref[...]
        dx_ref[...] = res_ref[...] + r * (dxh - xh * jnp.mean(dxh * xh, axis=-1, keepdims=True))

        @pl.when(pl.program_id(0) == 0)
        def _():
            dg_ref[...] = jnp.zeros_like(dg_ref)

        dg_ref[...] += jnp.sum(dhv * xh, axis=0, keepdims=True)

    row = pl.BlockSpec((tm, d), lambda i: (i, 0))
    vec = pl.BlockSpec((1, d), lambda i: (0, 0))
    return pl.pallas_call(
        body,
        out_shape=(_sds((t, d), F32), _sds((1, d), F32)),
        grid=(t // tm,),
        in_specs=[row, row, vec, row],
        out_specs=(row, vec),
        compiler_params=pltpu.CompilerParams(dimension_semantics=("arbitrary",)),
        name=name,
    )(x, dh, g, res)


def _final_loss(name, x, g, target):
    t, d = x.shape
    tm = _tile(t, 512, 16)

    def body(x_ref, g_ref, t_ref, dx_ref, dxb_ref, loss_ref, dg_ref):
        xv = x_ref[...]
        gv = g_ref[...]
        r = lax.rsqrt(jnp.mean(xv * xv, axis=-1, keepdims=True) + EPS)
        xh = xv * r
        err = xh * gv - t_ref[...]
        dy = err * (1.0 / d)
        dxh = dy * gv
        dx = r * (dxh - xh * jnp.mean(dxh * xh, axis=-1, keepdims=True))
        dx_ref[...] = dx
        dxb_ref[...] = dx.astype(dxb_ref.dtype)
        per_row = jnp.sum(err * err, axis=-1, keepdims=True) * (0.5 / d)

        @pl.when(pl.program_id(0) == 0)
        def _():
            dg_ref[...] = jnp.zeros_like(dg_ref)
            loss_ref[...] = jnp.zeros_like(loss_ref)

        dg_ref[...] += jnp.sum(dy * xh, axis=0, keepdims=True)
        loss_ref[...] += jnp.sum(per_row, axis=0, keepdims=True)

    row = pl.BlockSpec((tm, d), lambda i: (i, 0))
    vec = pl.BlockSpec((1, d), lambda i: (0, 0))
    return pl.pallas_call(
        body,
        out_shape=(_sds((t, d), F32), _sds((t, d), BF16), _sds((1, LANES), F32), _sds((1, d), F32)),
        grid=(t // tm,),
        in_specs=[row, vec, row],
        out_specs=(row, row, pl.BlockSpec((1, LANES), lambda i: (0, 0)), vec),
        compiler_params=pltpu.CompilerParams(dimension_semantics=("arbitrary",)),
        name=name,
    )(x, g, target)


def _shift_down(z, k):
    row = lax.broadcasted_iota(jnp.int32, z.shape, 0)
    return jnp.where(row >= k, pltpu.roll(z, k, axis=0), 0.0)


def _shift_up(z, k):
    s = z.shape[0]
    row = lax.broadcasted_iota(jnp.int32, z.shape, 0)
    return jnp.where(row < s - k, pltpu.roll(z, s - k, axis=0), 0.0)


def _conv3(z, w):
    return (w[2:3] * z + w[0:1] * _shift_down(z, 2)) + w[1:2] * _shift_down(z, 1)


def _conv3_t(dz, w):
    return (w[2:3] * dz + w[0:1] * _shift_up(dz, 2)) + w[1:2] * _shift_up(dz, 1)


def _conv_fwd(name, pc, w, batch, seq, tc):
    cw = w.shape[1]
    nct = cw // tc

    def body(pc_ref, w_ref, o_ref):
        cb = pc_ref[:, 0:tc].astype(F32)
        z = pc_ref[:, tc:2 * tc].astype(F32) * pc_ref[:, 2 * tc:3 * tc].astype(F32)
        o_ref[...] = (cb * _conv3(z, w_ref[...])).astype(o_ref.dtype)

    return pl.pallas_call(
        body,
        out_shape=_sds((batch * seq, cw), BF16),
        grid=(batch, nct),
        in_specs=[pl.BlockSpec((seq, 3 * tc), lambda b, j: (b, j)), pl.BlockSpec((3, tc), lambda b, j: (0, j))],
        out_specs=pl.BlockSpec((seq, tc), lambda b, j: (b, j)),
        compiler_params=pltpu.CompilerParams(dimension_semantics=("parallel", "parallel")),
        name=name,
    )(pc, w)


def _conv_bwd(name, da, pc, w, dproj, batch, seq, tc):
    cw = w.shape[1]
    nct = cw // tc

    def body(da_ref, pc_ref, w_ref, _, dpc_ref, dw_ref):
        wv = w_ref[...]
        cb = pc_ref[:, 0:tc].astype(F32)
        cc = pc_ref[:, tc:2 * tc].astype(F32)
        cin = pc_ref[:, 2 * tc:3 * tc].astype(F32)
        z = cc * cin
        dav = da_ref[...].astype(F32)
        du = dav * cb
        dz = _conv3_t(du, wv)
        dpc_ref[:, 0:tc] = (dav * _conv3(z, wv)).astype(dpc_ref.dtype)
        dpc_ref[:, tc:2 * tc] = (dz * cin).astype(dpc_ref.dtype)
        dpc_ref[:, 2 * tc:3 * tc] = (dz * cc).astype(dpc_ref.dtype)

        @pl.when(pl.program_id(1) == 0)
        def _():
            dw_ref[...] = jnp.zeros_like(dw_ref)

        dw_ref[0:1, :] += jnp.sum(du * _shift_down(z, 2), axis=0, keepdims=True)
        dw_ref[1:2, :] += jnp.sum(du * _shift_down(z, 1), axis=0, keepdims=True)
        dw_ref[2:3, :] += jnp.sum(du * z, axis=0, keepdims=True)

    return pl.pallas_call(
        body,
        out_shape=(_sds(dproj.shape, dproj.dtype), _sds((3, cw), F32)),
        grid=(nct, batch),
        in_specs=[
            pl.BlockSpec((seq, tc), lambda j, b: (b, j)),
            pl.BlockSpec((seq, 3 * tc), lambda j, b: (b, j)),
            pl.BlockSpec((3, tc), lambda j, b: (0, j)),
            pl.BlockSpec(memory_space=pl.ANY),
        ],
        out_specs=(pl.BlockSpec((seq, 3 * tc), lambda j, b: (b, j)), pl.BlockSpec((3, tc), lambda j, b: (0, j))),
        input_output_aliases={3: 0},
        compiler_params=pltpu.CompilerParams(dimension_semantics=("parallel", "arbitrary")),
        name=name,
    )(da, pc, w, dproj)


def _ffn_up_act(name, h2, w_up, w, batch, seq, tc):
    d = h2.shape[1]
    fh = w.shape[1] // 2
    nf = fh // tc

    def body(h_ref, ma_ref, mb_ref, wa_ref, wb_ref, o_ref, ua_ref, ub_ref, a_ref, b_ref):
        hv = h_ref[...]
        ua = _dot(hv, ma_ref[...], "nn")
        ub = _dot(hv, mb_ref[...], "nn")
        ua_ref[...] = ua.astype(ua_ref.dtype)
        ub_ref[...] = ub.astype(ub_ref.dtype)
        a = _conv3(ua, wa_ref[...])
        b = _conv3(ub, wb_ref[...])
        a_ref[...] = a.astype(a_ref.dtype)
        b_ref[...] = b.astype(b_ref.dtype)
        o_ref[...] = (a * jax.nn.sigmoid(a) * b).astype(o_ref.dtype)

    act = pl.BlockSpec((seq, tc), lambda b, j: (b, j))
    shape = _sds((batch * seq, fh), BF16)
    return pl.pallas_call(
        body,
        out_shape=(shape,) * 5,
        grid=(batch, nf),
        in_specs=[
            pl.BlockSpec((seq, d), lambda b, j: (b, 0)),
            pl.BlockSpec((d, tc), lambda b, j: (0, j)),
            pl.BlockSpec((d, tc), lambda b, j: (0, nf + j)),
            pl.BlockSpec((3, tc), lambda b, j: (0, j)),
            pl.BlockSpec((3, tc), lambda b, j: (0, nf + j)),
        ],
        out_specs=(act,) * 5,
        compiler_params=pltpu.CompilerParams(dimension_semantics=("parallel", "parallel")),
        name=name,
    )(h2, w_up, w_up, w, w)


def _ffn_bwd(name, dx, w_down, ua, ub, av, bv, w, batch, seq, tc):
    d = dx.shape[1]
    fh = w.shape[1] // 2
    nf = fh // tc

    rb = _tile(seq, 128, 8)
    halo = 8

    def body(dx_ref, md_ref, ua_ref, ub_ref, a_ref, b_ref, wa_ref, wb_ref, dua_ref, dub_ref, dw_ref,
             dh_scr, da_scr, db_scr):
        j = pl.program_id(1)
        dh_scr[...] = _dot(dx_ref[...].astype(BF16), md_ref[...], "nt")
        da_scr[seq:seq + halo, :] = jnp.zeros((halo, tc), F32)
        db_scr[seq:seq + halo, :] = jnp.zeros((halo, tc), F32)

        def silu_bwd(r, carry):
            rows = pl.ds(pl.multiple_of(r * rb, rb), rb)
            a, b, dhv = a_ref[rows, :].astype(F32), b_ref[rows, :].astype(F32), dh_scr[rows, :]
            sg = jax.nn.sigmoid(a)
            da_scr[rows, :] = dhv * b * (sg * (1.0 + a * (1.0 - sg)))
            db_scr[rows, :] = dhv * (a * sg)
            return carry

        lax.fori_loop(0, seq // rb, silu_bwd, 0)
        wa, wb = wa_ref[...], wb_ref[...]

        def conv_bwd(r, sums):
            r0 = pl.multiple_of(r * rb, rb)
            rows = pl.ds(r0, rb)
            out = []
            for d_scr, u_ref, wv, du_ref, acc in ((da_scr, ua_ref, wa, dua_ref, sums[0:3]),
                                                  (db_scr, ub_ref, wb, dub_ref, sums[3:6])):
                x = d_scr[pl.ds(r0, rb + halo), :]
                dv = x[0:rb]
                up1 = pltpu.roll(x, rb + halo - 1, axis=0)[0:rb]
                up2 = pltpu.roll(x, rb + halo - 2, axis=0)[0:rb]
                du_ref[rows, :] = ((wv[2:3] * dv + wv[0:1] * up2) + wv[1:2] * up1).astype(du_ref.dtype)
                uv = u_ref[rows, :].astype(F32)
                out += [acc[0] + jnp.sum(up2 * uv, axis=0, keepdims=True),
                        acc[1] + jnp.sum(up1 * uv, axis=0, keepdims=True),
                        acc[2] + jnp.sum(dv * uv, axis=0, keepdims=True)]
            return tuple(out)

        sums = lax.fori_loop(0, seq // rb, conv_bwd, (jnp.zeros((1, tc), F32),) * 6)

        @pl.when((pl.program_id(0) == 0) & (j == 0))
        def _():
            dw_ref[...] = jnp.zeros_like(dw_ref)

        for half, off in enumerate((0, fh)):
            cols = pl.ds(pl.multiple_of(off + j * tc, LANES), tc)
            for k in range(3):
                dw_ref[k:k + 1, cols] += sums[3 * half + k]

    act = pl.BlockSpec((seq, tc), lambda b, j: (b, j))
    shape = _sds((batch * seq, fh), BF16)
    return pl.pallas_call(
        body,
        out_shape=(shape, shape, _sds((3, 2 * fh), F32)),
        grid=(batch, nf),
        in_specs=[
            pl.BlockSpec((seq, d), lambda b, j: (b, 0)),
            pl.BlockSpec((tc, d), lambda b, j: (j, 0)),
            act,
            act,
            act,
            act,
            pl.BlockSpec((3, tc), lambda b, j: (0, j)),
            pl.BlockSpec((3, tc), lambda b, j: (0, nf + j)),
        ],
        out_specs=(act, act, pl.BlockSpec((3, 2 * fh), lambda b, j: (0, 0))),
        scratch_shapes=[pltpu.VMEM((seq, tc), F32), pltpu.VMEM((seq + halo, tc), F32),
                        pltpu.VMEM((seq + halo, tc), F32)],
        compiler_params=pltpu.CompilerParams(dimension_semantics=("arbitrary", "arbitrary")),
        name=name,
    )(dx, w_down, ua, ub, av, bv, w, w)


def _merge_fwd(name, ycat, gl, bg):
    t, d2 = ycat.shape
    d = d2 // 2
    tm = _tile(t, 1024, 16)

    def body(y_ref, gl_ref, bg_ref, o_ref):
        g = jax.nn.sigmoid(gl_ref[...].astype(F32) + bg_ref[...])
        prod = g * y_ref[...].astype(F32)
        o_ref[...] = (prod[:, 0:d] + prod[:, d:d2]).astype(o_ref.dtype)

    row = pl.BlockSpec((tm, d2), lambda i: (i, 0))
    return pl.pallas_call(
        body,
        out_shape=_sds((t, d), BF16),
        grid=(t // tm,),
        in_specs=[row, row, pl.BlockSpec((1, d2), lambda i: (0, 0))],
        out_specs=pl.BlockSpec((tm, d), lambda i: (i, 0)),
        compiler_params=pltpu.CompilerParams(dimension_semantics=("parallel",)),
        name=name,
    )(ycat, gl, bg)


def _merge_bwd(name, dm, ycat, gl, bg, width, gl_off):
    t, d2 = ycat.shape
    d = d2 // 2
    tm = _tile(t, 1024, 16)
    wb = math.gcd(gl_off, d)
    nw = d // wb

    def body(dm_ref, y_ref, gl_ref, bg_ref, dgl_ref, dy_ref, dbg_ref):
        g = jax.nn.sigmoid(gl_ref[...].astype(F32) + bg_ref[...])
        dmv = dm_ref[...].astype(F32)
        dgl = dmv * y_ref[...].astype(F32) * (g * (1.0 - g))
        dgl_ref[...] = dgl.astype(dgl_ref.dtype)
        dy_ref[...] = (dmv * g).astype(dy_ref.dtype)

        @pl.when(pl.program_id(2) == 0)
        def _():
            dbg_ref[...] = jnp.zeros_like(dbg_ref)

        dbg_ref[...] += jnp.sum(dgl, axis=0, keepdims=True)

    half = pl.BlockSpec((tm, wb), lambda h, j, i: (i, h * nw + j))
    vec = pl.BlockSpec((1, wb), lambda h, j, i: (0, h * nw + j))
    return pl.pallas_call(
        body,
        out_shape=(_sds((t, width), BF16), _sds((t, d2), BF16), _sds((1, d2), F32)),
        grid=(2, nw, t // tm),
        in_specs=[pl.BlockSpec((tm, wb), lambda h, j, i: (i, j)), half, half, vec],
        out_specs=(pl.BlockSpec((tm, wb), lambda h, j, i: (i, gl_off // wb + h * nw + j)), half, vec),
        compiler_params=pltpu.CompilerParams(dimension_semantics=("parallel", "parallel", "arbitrary")),
        name=name,
    )(dm, ycat, gl, bg)


def _log_sigmoid(z):
    return jnp.minimum(z, 0.0) - jnp.log1p(jnp.exp(-jnp.abs(z)))


def _forget_fwd(name, fl, bf, batch, seq):
    def body(fl_ref, bf_ref, o_ref):
        lf = _log_sigmoid(fl_ref[:, 0:LANES] + bf_ref[:, 0:LANES])
        acc = lf.T[0:HEADS, :]
        lane = lax.broadcasted_iota(jnp.int32, acc.shape, 1)
        k = 1
        while k < seq:
            acc = acc + jnp.where(lane >= k, pltpu.roll(acc, k, axis=1), 0.0)
            k *= 2
        o_ref[...] = acc

    return pl.pallas_call(
        body,
        out_shape=_sds((batch, HEADS, seq), F32),
        grid=(batch,),
        in_specs=[pl.BlockSpec((seq, F_PAD), lambda b: (b, 0)), pl.BlockSpec((1, F_PAD), lambda b: (0, 0))],
        out_specs=pl.BlockSpec((None, HEADS, seq), lambda b: (b, 0, 0)),
        compiler_params=pltpu.CompilerParams(dimension_semantics=("parallel",)),
        name=name,
    )(fl, bf)


def _forget_bwd(name, d_key, d_query, fl, bf, dproj, f_off, batch, seq):
    nfb = F_PAD // LANES

    def body(dk_ref, dq_ref, fl_ref, bf_ref, _, df_ref, dbf_ref):
        jj = pl.program_id(1)
        key_t = jnp.concatenate([dk_ref[...], jnp.zeros((LANES - HEADS, seq), F32)], axis=0).T
        acc = dq_ref[...] - key_t
        row = lax.broadcasted_iota(jnp.int32, acc.shape, 0)
        k = 1
        while k < seq:
            acc = acc + jnp.where(row < seq - k, pltpu.roll(acc, seq - k, axis=0), 0.0)
            k *= 2
        z = fl_ref[:, 0:LANES] + bf_ref[:, 0:LANES]
        col = lax.broadcasted_iota(jnp.int32, acc.shape, 1)
        df = jnp.where(col < HEADS, acc * jax.nn.sigmoid(-z), 0.0)
        df = jnp.where(jj == 0, df, 0.0)
        df_ref[...] = df.astype(df_ref.dtype)

        @pl.when((pl.program_id(0) == 0) & (jj == 0))
        def _():
            dbf_ref[...] = jnp.zeros_like(dbf_ref)

        dbf_ref[...] += jnp.sum(df, axis=0, keepdims=True)

    return pl.pallas_call(
        body,
        out_shape=(_sds(dproj.shape, dproj.dtype), _sds((1, LANES), F32)),
        grid=(batch, nfb),
        in_specs=[
            pl.BlockSpec((None, HEADS, seq), lambda b, j: (b, 0, 0)),
            pl.BlockSpec((seq, LANES), lambda b, j: (b, 0)),
            pl.BlockSpec((seq, F_PAD), lambda b, j: (b, 0)),
            pl.BlockSpec((1, F_PAD), lambda b, j: (0, 0)),
            pl.BlockSpec(memory_space=pl.ANY),
        ],
        out_specs=(pl.BlockSpec((seq, LANES), lambda b, j: (b, f_off // LANES + j)),
                   pl.BlockSpec((1, LANES), lambda b, j: (0, 0))),
        input_output_aliases={4: 0},
        compiler_params=pltpu.CompilerParams(dimension_semantics=("arbitrary", "arbitrary")),
        name=name,
    )(d_key, d_query, fl, bf, dproj)


def _dot(a, b, mode):
    return lax.dot_general(a, b, _DIMS[mode], preferred_element_type=F32)


def _attn_fwd(name, qkv, frow, batch, seq, tq):
    nq = seq // tq
    scale = 1.0 / math.sqrt(HEAD_DIM)

    def body(q_ref, k_ref, v_ref, f_ref, o_ref, lse_ref):
        i = pl.program_id(2)
        lane = lax.broadcasted_iota(jnp.int32, (1, LANES), 1)
        lo = lane < HEAD_DIM
        qs = q_ref[...] * scale
        qh = (jnp.where(lo, qs, 0.0).astype(BF16), jnp.where(lo, 0.0, qs).astype(BF16))
        row = lax.broadcasted_iota(jnp.int32, (tq, tq), 0)
        col = lax.broadcasted_iota(jnp.int32, (tq, tq), 1)

        def step(j, carry, diag):
            m0, l0, m1, l1, acc = carry
            start = pl.multiple_of(j * tq, tq)
            kj = k_ref[pl.ds(start, tq), :]
            vj = v_ref[pl.ds(start, tq), :]
            ms, ls, pvs, alphas = [], [], [], []
            for h, (m_old, l_old) in enumerate(((m0, l0), (m1, l1))):
                s = _dot(qh[h], kj, "nt") - f_ref[h:h + 1, pl.ds(start, tq)]
                if diag:
                    s = jnp.where(col <= row, s, NEG_BIG)
                m_new = jnp.maximum(m_old, jnp.max(s, axis=1, keepdims=True))
                p = jnp.exp(s - m_new)
                alpha = jnp.exp(m_old - m_new)
                ls.append(alpha * l_old + jnp.sum(p, axis=1, keepdims=True))
                ms.append(m_new)
                alphas.append(alpha)
                vh = jnp.where(lo, vj, 0.0) if h == 0 else jnp.where(lo, 0.0, vj)
                pvs.append(_dot(p.astype(BF16), vh.astype(BF16), "nn"))
            acc = acc * jnp.where(lo, alphas[0], alphas[1]) + (pvs[0] + pvs[1])
            return ms[0], ls[0], ms[1], ls[1], acc

        neg = jnp.full((tq, 1), NEG_BIG, F32)
        zero = jnp.zeros((tq, 1), F32)
        init = (neg, zero, neg, zero, jnp.zeros((tq, LANES), F32))
        carry = lax.fori_loop(0, i, lambda j, c: step(j, c, False), init)
        m0, l0, m1, l1, acc = step(i, carry, True)
        o_ref[...] = (acc / jnp.where(lo, l0, l1)).astype(o_ref.dtype)
        lse_ref[:, 0:1] = m0 + jnp.log(l0)
        lse_ref[:, 1:2] = m1 + jnp.log(l1)

    return pl.pallas_call(
        body,
        out_shape=(_sds((batch * seq, ATTN_WIDTH), BF16), _sds((HEAD_PAIRS, batch * seq, 2), F32)),
        grid=(batch, HEAD_PAIRS, nq),
        in_specs=[
            pl.BlockSpec((tq, LANES), lambda b, hp, i: (b * nq + i, 3 * hp)),
            pl.BlockSpec((seq, LANES), lambda b, hp, i: (b, 3 * hp + 1)),
            pl.BlockSpec((seq, LANES), lambda b, hp, i: (b, 3 * hp + 2)),
            pl.BlockSpec((None, None, 2, seq), lambda b, hp, i: (b, hp, 0, 0)),
        ],
        out_specs=(
            pl.BlockSpec((tq, LANES), lambda b, hp, i: (b * nq + i, hp)),
            pl.BlockSpec((None, tq, 2), lambda b, hp, i: (hp, b * nq + i, 0)),
        ),
        compiler_params=pltpu.CompilerParams(dimension_semantics=("parallel", "parallel", "parallel")),
        name=name,
    )(qkv, qkv, qkv, frow)


def _attn_bwd(name, qkv, do, o, lse, frow, dproj, qkv_off, batch, seq, tq):
    nq = seq // tq
    scale = 1.0 / math.sqrt(HEAD_DIM)

    def body(q_ref, k_ref, v_ref, do_ref, o_ref, lse_ref, f_ref, _, dqkv_ref, df_ref, drow_ref,
             dq_acc, dk_acc, dv_acc, df_acc):
        j = pl.program_id(2)
        lane = lax.broadcasted_iota(jnp.int32, (1, LANES), 1)
        lo = lane < HEAD_DIM
        masks = (lo, jnp.logical_not(lo))
        row = lax.broadcasted_iota(jnp.int32, (tq, tq), 0)
        col = lax.broadcasted_iota(jnp.int32, (tq, tq), 1)

        @pl.when(j == 0)
        def _():
            dq_acc[...] = jnp.zeros_like(dq_acc)
            drow_ref[...] = jnp.zeros_like(drow_ref)

        dk_acc[...] = jnp.zeros_like(dk_acc)
        dv_acc[...] = jnp.zeros_like(dv_acc)
        df_acc[...] = jnp.zeros_like(df_acc)
        kj = k_ref[...]
        vj = v_ref[...]
        kstart = pl.multiple_of(j * tq, tq)
        kh = tuple(jnp.where(mk, kj, 0.0).astype(BF16) for mk in masks)

        def step(i, diag):
            start = pl.multiple_of(i * tq, tq)
            rows = pl.ds(start, tq)
            qi = q_ref[rows, :] * scale
            doi = do_ref[rows, :]
            prod = doi.astype(F32) * o_ref[rows, :].astype(F32)
            lse_i = lse_ref[rows, :]
            dq_i = jnp.zeros((tq, LANES), F32)
            for h, mk in enumerate(masks):
                q_h = jnp.where(mk, qi, 0.0).astype(BF16)
                do_h = jnp.where(mk, doi, 0.0).astype(BF16)
                delta = jnp.sum(jnp.where(mk, prod, 0.0), axis=1, keepdims=True)
                s = _dot(q_h, kj, "nt") - f_ref[h:h + 1, pl.ds(kstart, tq)]
                p = jnp.exp(s - lse_i[:, h:h + 1])
                if diag:
                    p = jnp.where(col <= row, p, 0.0)
                ds = p * (_dot(do_h, vj, "nt") - delta)
                df_acc[h:h + 1, :] += jnp.sum(ds, axis=0, keepdims=True)
                drow_ref[rows, h:h + 1] += jnp.sum(ds, axis=1, keepdims=True)
                dsb = ds.astype(BF16)
                dv_acc[...] += _dot(p.astype(BF16), do_h, "tn")
                dk_acc[...] += _dot(dsb, q_h, "tn")
                dq_i = dq_i + _dot(dsb, kh[h], "nn")
            dq_acc[rows, :] += dq_i

        step(j, True)
        lax.fori_loop(j + 1, nq, lambda i, c: (step(i, False), c)[1], 0)
        dqkv_ref[:, 0:LANES] = (dq_acc[pl.ds(kstart, tq), :] * scale).astype(dqkv_ref.dtype)
        dqkv_ref[:, LANES:2 * LANES] = dk_acc[...].astype(dqkv_ref.dtype)
        dqkv_ref[:, 2 * LANES:3 * LANES] = dv_acc[...].astype(dqkv_ref.dtype)
        df_ref[...] = df_acc[...]

    full = lambda c: pl.BlockSpec((seq, LANES), lambda b, hp, j: (b, c(hp)))
    blk = lambda c: pl.BlockSpec((tq, LANES), lambda b, hp, j: (b * nq + j, c(hp)))
    return pl.pallas_call(
        body,
        out_shape=(_sds(dproj.shape, dproj.dtype), _sds((batch, HEAD_PAIRS, 2, seq), F32),
                   _sds((HEAD_PAIRS, batch * seq, 2), F32)),
        grid=(batch, HEAD_PAIRS, nq),
        in_specs=[
            full(lambda hp: 3 * hp),
            blk(lambda hp: 3 * hp + 1),
            blk(lambda hp: 3 * hp + 2),
            full(lambda hp: hp),
            full(lambda hp: hp),
            pl.BlockSpec((None, seq, 2), lambda b, hp, j: (hp, b, 0)),
            pl.BlockSpec((None, None, 2, seq), lambda b, hp, j: (b, hp, 0, 0)),
            pl.BlockSpec(memory_space=pl.ANY),
        ],
        out_specs=(
            pl.BlockSpec((tq, 3 * LANES), lambda b, hp, j: (b * nq + j, qkv_off // (3 * LANES) + hp)),
            pl.BlockSpec((None, None, 2, tq), lambda b, hp, j: (b, hp, 0, j)),
            pl.BlockSpec((None, seq, 2), lambda b, hp, j: (hp, b, 0)),
        ),
        scratch_shapes=[
            pltpu.VMEM((seq, LANES), F32),
            pltpu.VMEM((tq, LANES), F32),
            pltpu.VMEM((tq, LANES), F32),
            pltpu.VMEM((2, tq), F32),
        ],
        input_output_aliases={7: 0},
        compiler_params=pltpu.CompilerParams(dimension_semantics=("parallel", "parallel", "arbitrary")),
        name=name,
    )(qkv, qkv, qkv, do, o, lse, frow, dproj)


def _mesh_place():
    x, y, c = lax.axis_index("x"), lax.axis_index("y"), lax.axis_index("c")
    chips = [(1 - x, y), (x, 1 - y), (1 - x, 1 - y)]
    return x, y, c, chips


def _hbm_specs(n):
    return [pl.BlockSpec(memory_space=pl.ANY)] * n


def _half(shape2d, axis, which):
    size = shape2d[axis] // 2
    sl = pl.ds(pl.multiple_of(which * size, 16 if axis == 0 else LANES), size)
    return (sl, slice(None)) if axis == 0 else (slice(None), sl)


def _gather_weights(bigs, axes, smalls):
    nb, ns = len(bigs), len(smalls)
    arrays = list(bigs) + list(smalls)
    n = nb + ns

    def body(*refs):
        ins, outs = refs[:n], refs[n:2 * n]
        send_sems, recv_sems = refs[2 * n:]
        x, y, c, chips = _mesh_place()
        me = 2 * x + y
        sibling = (x, y, 1 - c)

        def half(a, which):
            return _half(arrays[a].shape, axes[a], which)

        def copy(a, k, src, dst, to):
            return pltpu.make_async_remote_copy(src_ref=src, dst_ref=dst, send_sem=send_sems.at[a, k],
                                                recv_sem=recv_sems.at[a, k], device_id=to, device_id_type=MESH)

        sends = []
        for a in range(n):
            for j, chip in enumerate(chips):
                if a < nb:
                    cp = copy(a, j, ins[a].at[half(a, c)], outs[a].at[(me,) + half(a, c)], (*chip, c))
                else:
                    cp = copy(a, j, ins[a], outs[a].at[me], (*chip, c))
                cp.start()
                sends.append(cp)
        for a in range(nb):
            for j, (px, py) in enumerate(chips):
                blk = outs[a].at[(2 * px + py,) + half(a, c)]
                copy(a, j, blk, blk, (px, py, c)).wait_recv()
                fwd = copy(a, 3 + j, blk, blk, sibling)
                fwd.start()
                sends.append(fwd)
        for a in range(nb, n):
            for j, (px, py) in enumerate(chips):
                blk = outs[a].at[2 * px + py]
                copy(a, j, blk, blk, (px, py, c)).wait_recv()
        for a in range(nb):
            for j, (px, py) in enumerate(chips):
                blk = outs[a].at[(2 * px + py,) + half(a, 1 - c)]
                copy(a, 3 + j, blk, blk, sibling).wait_recv()
        for cp in sends:
            cp.wait_send()

    outs = pl.pallas_call(
        body,
        out_shape=tuple(_sds((N_CHIPS,) + a.shape, a.dtype) for a in arrays),
        in_specs=_hbm_specs(n),
        out_specs=tuple(_hbm_specs(n)),
        scratch_shapes=[pltpu.SemaphoreType.DMA((n, 6)), pltpu.SemaphoreType.DMA((n, 6))],
        name="gather_weights",
    )(*arrays)
    me = 2 * lax.axis_index("x") + lax.axis_index("y")
    return tuple(lax.dynamic_update_index_in_dim(o, a, me, 0) for o, a in zip(outs, arrays))


def _gather_small(v):
    m_per, ncol = v.shape

    def body(x_ref, out_ref, send_sems, recv_sems, local_sem):
        x, y, c, chips = _mesh_place()
        me, sibling = (x, y, c), (x, y, 1 - c)

        def rows(px, py, pc):
            return out_ref.at[pl.ds((4 * px + 2 * py + pc) * m_per, m_per), :]

        def copy(k, block, to, src=None):
            return pltpu.make_async_remote_copy(src_ref=rows(*block) if src is None else src, dst_ref=rows(*block),
                                                send_sem=send_sems.at[k], recv_sem=recv_sems.at[k],
                                                device_id=to, device_id_type=MESH)

        mine = pltpu.make_async_copy(x_ref, rows(*me), local_sem)
        mine.start()
        first = [copy(0, me, sibling, src=x_ref)]
        first += [copy(1 + j, me, (*chip, c), src=x_ref) for j, chip in enumerate(chips)]
        for cp in first:
            cp.start()
        passed = [copy(4 + j, (*chip, c), sibling) for j, chip in enumerate(chips)]
        for j, chip in enumerate(chips):
            copy(1 + j, (*chip, c), me).wait_recv()
            passed[j].start()
        copy(0, sibling, me).wait_recv()
        for j, chip in enumerate(chips):
            copy(4 + j, (*chip, 1 - c), me).wait_recv()
        for cp in first + passed:
            cp.wait_send()
        mine.wait()

    return pl.pallas_call(
        body,
        out_shape=_sds((N_DEV * m_per, ncol), v.dtype),
        in_specs=[pl.BlockSpec(memory_space=pltpu.VMEM)],
        out_specs=pl.BlockSpec(memory_space=pltpu.VMEM),
        scratch_shapes=[pltpu.SemaphoreType.DMA((7,)), pltpu.SemaphoreType.DMA((7,)), pltpu.SemaphoreType.DMA],
        name="gather_small",
    )(v)


def _half_shape(shape2d, axis):
    return (shape2d[0] // 2, shape2d[1]) if axis == 0 else (shape2d[0], shape2d[1] // 2)


def _exchange_sibling(name, grads, axes):
    n = len(grads)

    def body(*refs):
        ins, outs = refs[:n], refs[n:2 * n]
        send_sems, recv_sems = refs[2 * n:]
        x, y, c, _ = _mesh_place()
        copies = []
        for a in range(n):
            src = ins[a].at[(slice(None),) + _half(grads[a].shape[1:], axes[a], 1 - c)]
            cp = pltpu.make_async_remote_copy(src_ref=src, dst_ref=outs[a], send_sem=send_sems.at[a],
                                              recv_sem=recv_sems.at[a], device_id=(x, y, 1 - c), device_id_type=MESH)
            cp.start()
            copies.append(cp)
        for cp in copies:
            cp.wait()

    return pl.pallas_call(
        body,
        out_shape=tuple(_sds((N_CHIPS,) + _half_shape(g.shape[1:], ax), g.dtype) for g, ax in zip(grads, axes)),
        in_specs=_hbm_specs(n),
        out_specs=tuple(_hbm_specs(n)),
        scratch_shapes=[pltpu.SemaphoreType.DMA((n,)), pltpu.SemaphoreType.DMA((n,))],
        name=name,
    )(*grads)


_HBM = pl.BlockSpec(memory_space=pltpu.HBM)
_SEM = pl.BlockSpec(memory_space=pltpu.SEMAPHORE)
_EFFECT = pltpu.SideEffectType.DATAFLOW_SIDE_EFFECTING


def _chip_copies(kind, srcs, lands, send_sems, recv_sems):
    x, y, c, chips = _mesh_place()
    copies = []
    for a in range(len(srcs)):
        for j, (px, py) in enumerate(chips):
            if kind == "gather":
                src, dst = srcs[a], lands[a].at[2 * x + y]
            else:
                src, dst = srcs[a].at[j], lands[a].at[j]
            copies.append(pltpu.make_async_remote_copy(src_ref=src, dst_ref=dst, send_sem=send_sems.at[3 * a + j],
                                                       recv_sem=recv_sems.at[3 * a + j], device_id=(px, py, c),
                                                       device_id_type=MESH))
    return copies


def _chips_start(name, kind, srcs):
    n = len(srcs)
    slots = N_CHIPS if kind == "gather" else 3
    lands = [lax.empty((slots,) + (s.shape if kind == "gather" else s.shape[1:]), s.dtype) for s in srcs]

    def body(*refs):
        for cp in _chip_copies(kind, refs[:n], refs[n:2 * n], refs[2 * n], refs[2 * n + 1]):
            cp.start()
        refs[-1][...] = jnp.zeros_like(refs[-1])

    outs = pl.pallas_call(
        body,
        out_shape=(pltpu.SemaphoreType.DMA((3 * n,)), pltpu.SemaphoreType.DMA((3 * n,)),
                   *[pltpu.HBM(v.shape, v.dtype) for v in (*srcs, *lands)], _sds((8, LANES), F32)),
        in_specs=[_HBM] * (2 * n),
        out_specs=(_SEM, _SEM, *[_HBM] * (2 * n), pl.BlockSpec(memory_space=pltpu.VMEM)),
        input_output_aliases={i: 2 + i for i in range(2 * n)},
        compiler_params=pltpu.CompilerParams(has_side_effects=_EFFECT),
        name=name,
    )(*[pltpu.with_memory_space_constraint(v, pltpu.HBM) for v in (*srcs, *lands)])
    return outs[:-1], outs[-1]


def _chips_wait(name, kind, handles, after):
    send_sems, recv_sems, *thru = handles
    n = len(thru) // 2

    def body(*refs):
        for cp in _chip_copies(kind, refs[:n], refs[n:2 * n], refs[2 * n], refs[2 * n + 1]):
            cp.wait_send()
            cp.wait_recv()

    outs = pl.pallas_call(
        body,
        out_shape=tuple(pltpu.HBM(v.shape, v.dtype) for v in thru),
        in_specs=[_HBM] * (2 * n) + [_SEM, _SEM, pl.BlockSpec(memory_space=pl.ANY)],
        out_specs=tuple([_HBM] * (2 * n)),
        input_output_aliases={i: i for i in range(2 * n)},
        compiler_params=pltpu.CompilerParams(has_side_effects=_EFFECT),
        name=name,
    )(*thru, send_sems, recv_sems, after)
    return outs[n:]


def _share_sibling(name, shards, axes):
    n = len(shards)

    def body(*refs):
        ins, outs = refs[:n], refs[n:2 * n]
        send_sems, recv_sems = refs[2 * n:]
        x, y, c, _ = _mesh_place()
        started = []
        for a in range(n):
            mine = _half(shards[a].shape, axes[a], c)
            theirs = _half(shards[a].shape, axes[a], 1 - c)
            cp = pltpu.make_async_remote_copy(src_ref=ins[a].at[mine], dst_ref=outs[a].at[mine],
                                              send_sem=send_sems.at[a], recv_sem=recv_sems.at[a],
                                              device_id=(x, y, 1 - c), device_id_type=MESH)
            cp.start()
            arrival = pltpu.make_async_remote_copy(src_ref=ins[a].at[theirs], dst_ref=outs[a].at[theirs],
                                                   send_sem=send_sems.at[a], recv_sem=recv_sems.at[a],
                                                   device_id=(x, y, 1 - c), device_id_type=MESH)
            started.append((cp, arrival))
        for cp, arrival in started:
            arrival.wait_recv()
            cp.wait_send()

    return pl.pallas_call(
        body,
        out_shape=tuple(_sds(s.shape, s.dtype) for s in shards),
        in_specs=_hbm_specs(n),
        out_specs=tuple(_hbm_specs(n)),
        scratch_shapes=[pltpu.SemaphoreType.DMA((n,)), pltpu.SemaphoreType.DMA((n,))],
        input_output_aliases={a: a for a in range(n)},
        name=name,
    )(*shards)


def _pair_sum(name, place, g, got, axis):
    hr, hc = got.shape[1:]

    def body(place_ref, g_ref, got_ref, o_ref):
        o_ref[...] = (g_ref[...] + got_ref[...]).astype(o_ref.dtype)

    blk = (None, hr, hc)
    mine = (lambda j, pr: (pr[2 + j], pr[1], 0)) if axis == 0 else (lambda j, pr: (pr[2 + j], 0, pr[1]))
    return pl.pallas_call(
        body,
        out_shape=_sds((N_CHIPS - 1, hr, hc), BF16),
        grid_spec=pltpu.PrefetchScalarGridSpec(
            num_scalar_prefetch=1,
            grid=(N_CHIPS - 1,),
            in_specs=[pl.BlockSpec(blk, mine), pl.BlockSpec(blk, lambda j, pr: (pr[2 + j], 0, 0))],
            out_specs=pl.BlockSpec(blk, lambda j, pr: (j, 0, 0)),
        ),
        compiler_params=pltpu.CompilerParams(dimension_semantics=("parallel",)),
        name=name,
    )(place, g, got)


def _chip_sum(name, place, g, got, arrivals, axis):
    _, r, cdim = g.shape
    hr, hc = got.shape[1:]

    def body(place_ref, g_ref, got_ref, arr_ref, o_ref):
        acc = g_ref[...] + got_ref[...]
        for j in range(3):
            acc = acc + arr_ref[j].astype(F32)
        o_ref[...] = acc

    blk = (None, hr, hc)
    mine = (lambda i, pr: (pr[0], pr[1], 0)) if axis == 0 else (lambda i, pr: (pr[0], 0, pr[1]))
    dest = (lambda i, pr: (pr[1], 0)) if axis == 0 else (lambda i, pr: (0, pr[1]))
    return pl.pallas_call(
        body,
        out_shape=_sds((r, cdim), F32),
        grid_spec=pltpu.PrefetchScalarGridSpec(
            num_scalar_prefetch=1,
            grid=(1,),
            in_specs=[
                pl.BlockSpec(blk, mine),
                pl.BlockSpec(blk, lambda i, pr: (pr[0], 0, 0)),
                pl.BlockSpec((3, hr, hc), lambda i, pr: (0, 0, 0)),
            ],
            out_specs=pl.BlockSpec((hr, hc), dest),
        ),
        compiler_params=pltpu.CompilerParams(dimension_semantics=("arbitrary",)),
        name=name,
    )(place, g, got, arrivals)


def _device_sum(name, gathered):
    m_per = gathered.shape[0] // N_DEV

    def body(g_ref, o_ref):
        acc = g_ref[0:m_per, :]
        for dev in range(1, N_DEV):
            acc = acc + g_ref[dev * m_per:(dev + 1) * m_per, :]
        o_ref[...] = acc

    return pl.pallas_call(body, out_shape=_sds((m_per, gathered.shape[1]), F32), name=name)(gathered)


def _adamw(name, w, g, m, v):
    r, cdim = w.shape
    if r % 8 == 0:
        tr, tcol = _tile(r, 256, 8), cdim
    else:
        tr, tcol = r, (_tile(cdim, 256, LANES) if cdim % LANES == 0 else cdim)
    blk = pl.BlockSpec((tr, tcol), lambda i, j: (i, j))
    grid = (r // tr, cdim // tcol)
    bc1 = 1.0 - ADAM_B1 ** ADAM_STEP
    bc2 = 1.0 - ADAM_B2 ** ADAM_STEP

    def body(w_ref, g_ref, m_ref, v_ref, d_ref, nm_ref, nv_ref):
        gv = g_ref[...]
        nm = ADAM_B1 * m_ref[...] + (1.0 - ADAM_B1) * gv
        nv = ADAM_B2 * v_ref[...] + (1.0 - ADAM_B2) * (gv * gv)
        d_ref[...] = -ADAM_LR * ((nm / bc1) / (jnp.sqrt(nv / bc2) + ADAM_EPS) + ADAM_WD * w_ref[...])
        nm_ref[...] = nm
        nv_ref[...] = nv

    shape = _sds(w.shape, F32)
    return pl.pallas_call(
        body,
        out_shape=(shape, shape, shape),
        grid=grid,
        in_specs=[blk] * 4,
        out_specs=(blk, blk, blk),
        compiler_params=pltpu.CompilerParams(dimension_semantics=("parallel", "parallel")),
        name=name,
    )(w, g, m, v)


def _cat_cols(g):
    return jnp.transpose(g, (1, 0, 2)).reshape(g.shape[1], N_CHIPS * g.shape[2])


def _split_cols(a):
    r, c4 = a.shape
    return jnp.transpose(a.reshape(r, N_CHIPS, c4 // N_CHIPS), (1, 0, 2))


def _local_step(x, target, w_int, late_weights, cmw, cfw, g1, b_f, b_gate, g2, gf,
                ffn_grads_ready, mix_grads_ready):
    batch, seq, d = x.shape
    t = batch * seq
    cw = d // 2
    fh = cfw.shape[1] // 2
    tc = LANES
    nct = cw // tc
    tq = min(512, seq)
    pc_w, qkv_w, gl_w = 3 * cw, 3 * ATTN_WIDTH, 2 * d
    qkv_off, gl_off, f_off = pc_w, pc_w + qkv_w, pc_w + qkv_w + gl_w
    width = f_off + F_PAD
    f_col = pc_w + qkv_w

    w_pc = w_int[:pc_w].reshape(3, nct, tc, d).transpose(1, 0, 2, 3).reshape(pc_w, d)
    w_qkv = w_int[pc_w:f_col].reshape(3, HEAD_PAIRS, LANES, d).transpose(1, 0, 2, 3).reshape(qkv_w, d)
    w_f = jnp.pad(w_int[f_col:f_col + HEADS], ((0, F_PAD - HEADS), (0, 0)))
    w_inp = jnp.concatenate([w_pc, w_qkv, w_int[f_col + HEADS:], w_f], axis=0)
    bf_pad = jnp.pad(b_f, ((0, 0), (0, F_PAD - HEADS)))

    x2d = x.reshape(t, d)
    tgt2d = target.reshape(t, d)

    h1 = _rms_fwd("norm_mix", x2d, g1)
    pc, qkv, gl, fl = _project("proj_in", h1, w_inp, [(pc_w, BF16), (qkv_w, BF16), (gl_w, BF16), (F_PAD, F32)])
    a_c = _conv_fwd("conv_mix", pc, cmw, batch, seq, tc)
    f_cum = _forget_fwd("forget_cumsum", fl, bf_pad, batch, seq)
    frow = f_cum.reshape(batch, HEAD_PAIRS, 2, seq)
    o, lse = _attn_fwd("attn_fwd", qkv, frow, batch, seq, tq)
    w_oc, w_oa, w_o, w_up, w_down = late_weights(o)
    ycat = _branch_out("branch_out", a_c, w_oc, o, w_oa)
    mg = _merge_fwd("gate_merge", ycat, gl, b_gate)
    x2 = _mm("mix_out", mg, w_o, "nn", F32, m=t, n=d, k=d, add=x2d)
    h2 = _rms_fwd("norm_ffn", x2, g2)
    tcf = min(2 * LANES, fh)
    w_up2 = _cat_cols(w_up)
    hmid, ua, ub, ffn_a, ffn_b = _ffn_up_act("ffn_up_act", h2, w_up2, cfw, batch, seq, tcf)
    x3 = _mm("ffn_down", hmid, w_down, "nn", F32, m=t, n=d, k=fh, add=x2, tk=4096)

    dx3, dx3b, loss_row, d_gf = _final_loss("final_loss", x3, gf.reshape(1, d), tgt2d)
    dw_down = _mm("dw_down", hmid, dx3b, "tn", F32, m=fh, n=d, k=t, tm=256, tk=8192)
    du_a, du_b, d_cfw = _ffn_bwd("d_ffn", dx3b, w_down, ua, ub, ffn_a, ffn_b, cfw, batch, seq, tcf)
    ws = w_up.shape[2]
    dh2 = _mm("d_norm_ffn_a", du_a, w_up2, "nt", BF16, m=t, n=d, k=fh, b_off=0, tk=4096)
    dh2 = _mm("d_norm_ffn_b", du_b, w_up2, "nt", BF16, m=t, n=d, k=fh, b_off=fh, add=dh2, tk=4096)
    dw_up = _mm("dw_up_a", h2, du_a, "tn", F32, m=d, n=fh, k=t, tm=512, tn=ws, tk=4096, o3=N_CHIPS)
    dw_up = _mm("dw_up_b", h2, du_b, "tn", F32, m=d, n=fh, k=t, tm=512, tn=ws, tk=4096, o3=N_CHIPS, out=dw_up,
                o_off=fh)
    token = ffn_grads_ready(dw_up, dw_down)
    if token is not None:
        g2 = g2 + token[0:1, 0:1]
    dx2, d_g2 = _rms_bwd("d_norm_ffn", x2, dh2, g2, dx3)
    dm = _mm("d_merge", dx2, w_o, "nt", BF16, m=t, n=d, k=d)
    dw_o = _mm("dw_o", mg, dx2, "tn", F32, m=d, n=d, k=t, tk=2048)
    dproj, dycat, d_bg = _merge_bwd("d_gate_merge", dm, ycat, gl, b_gate, width, gl_off)
    da_c, do = _branch_out_bwd("d_branch_out", dycat, w_oc, w_oa)
    dw_oc, dw_oa = _branch_out_dw("dw_branch_out", a_c, o, dycat)
    dproj, d_cmw = _conv_bwd("d_conv_mix", da_c, pc, cmw, dproj, batch, seq, tc)
    dproj, d_fkey, d_fquery = _attn_bwd("attn_bwd", qkv, do, o, lse, frow, dproj, qkv_off, batch, seq, tq)
    d_fquery = jnp.pad(jnp.transpose(d_fquery, (1, 0, 2)).reshape(t, HEADS), ((0, 0), (0, LANES - HEADS)))
    dproj, d_bf = _forget_bwd("d_forget", d_fkey.reshape(batch, HEADS, seq), d_fquery, fl, bf_pad, dproj, f_off,
                              batch, seq)
    dw_inp = _mm("dw_in", dproj, h1, "tn", F32, m=width, n=d, k=t, tm=256, tk=8192)
    d_pc = dw_inp[:pc_w].reshape(nct, 3, tc, d).transpose(1, 0, 2, 3).reshape(pc_w, d)
    d_qkv = dw_inp[qkv_off:gl_off].reshape(HEAD_PAIRS, 3, LANES, d).transpose(1, 0, 2, 3).reshape(qkv_w, d)
    dw_int = jnp.concatenate([d_pc, d_qkv, dw_inp[f_off:f_off + HEADS], dw_inp[gl_off:f_off]], axis=0)
    token = mix_grads_ready(dw_int, dw_oc, dw_oa, dw_o)
    dh1 = _mm("d_norm_mix", dproj, w_inp, "nn", BF16, m=t, n=d, k=width, tm=512, tk=8192, dep=token)
    grad_x, d_g1 = _rms_bwd("d_norm_mix_x", x2d, dh1, g1, dx2)
    smalls = (d_g1, d_g2, d_gf, d_bg, d_bf, d_cmw, d_cfw)
    return loss_row[0, 0], grad_x.reshape(batch, seq, d), smalls


def _pack_small(parts):
    flat = [p.reshape(-1) for p in parts]
    sizes = [f.shape[0] for f in flat]
    total = sum(sizes)
    padded = -(-total // (8 * LANES)) * (8 * LANES)
    vec = jnp.concatenate(flat + [jnp.zeros((padded - total,), F32)])
    offsets = [sum(sizes[:i]) for i in range(len(sizes))]
    return vec.reshape(padded // LANES, LANES), offsets


def kernel(x, norm_mix_g, w_in, b_f, b_gate, conv_mix_w, w_out_conv, w_out_attn, w_o, norm_ffn_g, w_up, conv_ffn_w, w_down, norm_f_g, loss_target, m_norm_mix_g, m_w_in, m_b_f, m_b_gate, m_conv_mix_w, m_w_out_conv, m_w_out_attn, m_w_o, m_norm_ffn_g, m_w_up, m_conv_ffn_w, m_w_down, m_norm_f_g, v_norm_mix_g, v_w_in, v_b_f, v_b_gate, v_conv_mix_w, v_w_out_conv, v_w_out_attn, v_w_o, v_norm_ffn_g, v_w_up, v_conv_ffn_w, v_w_down, v_norm_f_g):
    d = x.shape[-1]
    chip = 2 * lax.axis_index("x") + lax.axis_index("y")
    xi, yi = lax.axis_index("x"), lax.axis_index("y")
    peers = [2 * px + py for px, py in ((1 - xi, yi), (xi, 1 - yi), (1 - xi, 1 - yi))]
    place = jnp.stack([chip, lax.axis_index("c"), *peers]).astype(jnp.int32)

    t_in, t_m_in, t_v_in = (jnp.transpose(w[0]) for w in (w_in, m_w_in, v_w_in))

    def row_shards(a):
        return a.reshape(N_CHIPS, a.shape[0] // N_CHIPS, a.shape[1])

    def stacked(a):
        return a.reshape(N_CHIPS * a.shape[1], a.shape[2])

    a_in, a_cmw, a_cfw = _gather_weights([t_in.astype(BF16)], (1,), [conv_mix_w[0], conv_ffn_w[0]])
    late = [w[0].astype(BF16) for w in (w_out_conv, w_out_attn, w_o, w_up, w_down)]
    late_handles, late_token = _chips_start("gather_late_start", "gather", late)

    def late_weights(after):
        lands = _chips_wait("gather_late_wait", "gather", late_handles, after)
        a_oc, a_oa, a_o, a_up, a_down = (
            lax.dynamic_update_index_in_dim(buf, own, chip, 0) for buf, own in zip(lands, late))
        return _cat_cols(a_oc), _cat_cols(a_oa), stacked(a_o), a_up, stacked(a_down)

    pending = []

    def reduce_start(tag, names, grads, axes):
        got = _exchange_sibling("exchange_sibling_" + tag, grads, axes)
        sums = [_pair_sum("pair_sum_" + nm, place, g, r, ax) for nm, g, r, ax in zip(names, grads, got, axes)]
        handles, token = _chips_start("exchange_chips_start_" + tag, "reduce", sums)
        pending.append((tag, names, grads, axes, got, handles))
        return token

    def ffn_grads_ready(dw_up, dw_down):
        return reduce_start("ffn", ("w_up", "w_down"), [dw_up, row_shards(dw_down)], (0, 0))

    def mix_grads_ready(dw_int, dw_oc, dw_oa, dw_o):
        return reduce_start("mix", ("w_in", "w_out_conv", "w_out_attn", "w_o"),
                            [row_shards(dw_int), _split_cols(dw_oc), _split_cols(dw_oa), row_shards(dw_o)],
                            (1, 0, 0, 0))

    loss_local, grad_x, smalls = _local_step(
        x, loss_target, stacked(a_in), late_weights, _cat_cols(a_cmw),
        _cat_cols(a_cfw), norm_mix_g + late_token[0:1, 0:1], b_f, b_gate, norm_ffn_g, norm_f_g,
        ffn_grads_ready, mix_grads_ready)

    reduced = {}
    for tag, names, grads, axes, got, handles in pending:
        arrivals = _chips_wait("exchange_chips_wait_" + tag, "reduce", handles, grad_x)
        halves = [_chip_sum("chip_sum_" + nm, place, g, r, arr, ax)
                  for nm, g, r, arr, ax in zip(names, grads, got, arrivals, axes)]
        reduced.update(zip(names, _share_sibling("share_sibling_" + tag, halves, axes)))
    g_in, g_oc, g_oa, g_o, g_up, g_down = (
        reduced[nm] for nm in ("w_in", "w_out_conv", "w_out_attn", "w_o", "w_up", "w_down"))

    smalls = (*smalls, loss_local.reshape(1, 1))
    packed, offs = _pack_small(smalls)
    total = _device_sum("device_sum", _gather_small(packed)).reshape(-1)
    shapes = [s.shape for s in smalls]
    d_g1, d_g2, d_gf, d_bg, d_bf, d_cmw, d_cfw, loss = [
        total[o:o + math.prod(sh)].reshape(sh) for o, sh in zip(offs, shapes)]
    loss = loss[0, 0]
    d_bf = d_bf[:, :HEADS]
    cw_s, cf_s = conv_mix_w.shape[2], conv_ffn_w.shape[2]
    d_cmw = lax.dynamic_slice(d_cmw, (0, chip * cw_s), (3, cw_s))
    d_cfw = lax.dynamic_slice(d_cfw, (0, chip * cf_s), (3, cf_s))

    order = [
        ("norm_mix_g", norm_mix_g[0:1], d_g1, m_norm_mix_g, v_norm_mix_g),
        ("w_in", t_in, g_in, t_m_in, t_v_in),
        ("b_f", b_f, d_bf, m_b_f, v_b_f),
        ("b_gate", b_gate, d_bg, m_b_gate, v_b_gate),
        ("conv_mix_w", conv_mix_w[0], d_cmw, m_conv_mix_w[0], v_conv_mix_w[0]),
        ("w_out_conv", w_out_conv[0], g_oc, m_w_out_conv[0], v_w_out_conv[0]),
        ("w_out_attn", w_out_attn[0], g_oa, m_w_out_attn[0], v_w_out_attn[0]),
        ("w_o", w_o[0], g_o, m_w_o[0], v_w_o[0]),
        ("norm_ffn_g", norm_ffn_g, d_g2, m_norm_ffn_g, v_norm_ffn_g),
        ("w_up", w_up[0], g_up, m_w_up[0], v_w_up[0]),
        ("conv_ffn_w", conv_ffn_w[0], d_cfw, m_conv_ffn_w[0], v_conv_ffn_w[0]),
        ("w_down", w_down[0], g_down, m_w_down[0], v_w_down[0]),
        ("norm_f_g", norm_f_g.reshape(1, d), d_gf, m_norm_f_g.reshape(1, d), v_norm_f_g.reshape(1, d)),
    ]
    out_shapes = [norm_mix_g.shape, w_in.shape, b_f.shape, b_gate.shape, conv_mix_w.shape, w_out_conv.shape,
                  w_out_attn.shape, w_o.shape, norm_ffn_g.shape, w_up.shape, conv_ffn_w.shape, w_down.shape,
                  norm_f_g.shape]
    g_out, d_out, m_out, v_out = [], [], [], []
    for (nm, w, g, m, v), sh in zip(order, out_shapes):
        g = g.reshape(w.shape)
        delta, new_m, new_v = _adamw("adamw_" + nm, w, g, m.reshape(w.shape), v.reshape(w.shape))
        for dst, val in ((g_out, g), (d_out, delta), (m_out, new_m), (v_out, new_v)):
            dst.append((jnp.transpose(val) if nm == "w_in" else val).reshape(sh))
    return (loss, grad_x, *g_out, *d_out, *m_out, *v_out)
```

```python
import math

import jax
import jax.numpy as jnp
from jax import lax
from jax.experimental import pallas as pl
from jax.experimental.pallas import tpu as pltpu

F32 = jnp.float32
BF16 = jnp.bfloat16
MESH = pl.DeviceIdType.MESH

EPS = 1e-6
HEADS = 8
HEAD_DIM = 64
ATTN_WIDTH = HEADS * HEAD_DIM
HEAD_PAIRS = HEADS // 2
LANES = 128
F_PAD = 2 * LANES
NEG_BIG = -1e30
N_CHIPS = 4
N_DEV = 8

ADAM_LR = 0.001
ADAM_B1 = 0.9
ADAM_B2 = 0.999
ADAM_EPS = 1e-08
ADAM_WD = 0.01
ADAM_STEP = 10

_DIMS = {
    "nn": (((1,), (0,)), ((), ())),
    "nt": (((1,), (1,)), ((), ())),
    "tn": (((0,), (0,)), ((), ())),
}


def _tile(n, target, mult, also=()):
    best = None
    for t in range(mult, n + 1, mult):
        if n % t == 0 and t <= target and all(o % t == 0 for o in also):
            best = t
    if best is None:
        assert all(o == 0 for o in also), (n, target, mult, also)
        return n
    return best


def _sds(shape, dtype):
    return jax.ShapeDtypeStruct(shape, dtype)


def _mm(name, a, b, mode, out_dtype, *, m, n, k, a_off=0, b_off=0, out=None, o_off=0,
        o_width=None, o3=None, add=None, dep=None, tm=1024, tn=2048, tk=2048):
    if mode == "nn":
        tm = _tile(m, tm, 16)
        tk = _tile(k, tk, LANES, (a_off,))
        tn = _tile(n, tn, LANES, (b_off, o_off))
        a_spec = pl.BlockSpec((tm, tk), lambda i, j, kk: (i, a_off // tk + kk))
        b_spec = pl.BlockSpec((tk, tn), lambda i, j, kk: (kk, b_off // tn + j))
    elif mode == "nt":
        tm = _tile(m, tm, 16)
        tk = _tile(k, tk, LANES, (a_off, b_off))
        tn = _tile(n, tn, LANES, (o_off,))
        a_spec = pl.BlockSpec((tm, tk), lambda i, j, kk: (i, a_off // tk + kk))
        b_spec = pl.BlockSpec((tn, tk), lambda i, j, kk: (j, b_off // tk + kk))
    else:
        tm = _tile(m, tm, LANES, (a_off,))
        tk = _tile(k, tk, 16)
        tn = _tile(n, tn, LANES, (b_off, o_off))
        a_spec = pl.BlockSpec((tk, tm), lambda i, j, kk: (kk, a_off // tm + i))
        b_spec = pl.BlockSpec((tk, tn), lambda i, j, kk: (kk, b_off // tn + j))
    assert m % tm == 0 and n % tn == 0 and k % tk == 0, (name, tm, tn, tk)
    nk = k // tk
    if o3 is not None:
        o_spec = pl.BlockSpec((None, tm, tn), lambda i, j, kk: (o_off // tn + j, i, 0))
        out_sds = _sds((o3, m, tn), out_dtype)
    else:
        o_spec = pl.BlockSpec((tm, tn), lambda i, j, kk: (i, o_off // tn + j))
        width = o_width if o_width is not None else (out.shape[1] if out is not None else n)
        out_sds = _sds((m, width), out_dtype)
    use_acc = nk > 1 and out_dtype != F32
    dims = _DIMS[mode]
    has_add, has_out = add is not None, out is not None

    def body(*refs):
        a_ref, b_ref = refs[0], refs[1]
        pos = 2
        add_ref = None
        if has_add:
            add_ref = refs[pos]
            pos += 1
        if has_out:
            pos += 1
        if dep is not None:
            pos += 1
        o_ref = refs[pos]
        acc_ref = refs[pos + 1] if use_acc else None
        part = lax.dot_general(a_ref[...].astype(BF16), b_ref[...].astype(BF16), dims,
                               preferred_element_type=F32)
        if nk == 1:
            if has_add:
                part = part + add_ref[...]
            o_ref[...] = part.astype(o_ref.dtype)
            return
        kk = pl.program_id(2)
        tgt = acc_ref if use_acc else o_ref

        @pl.when(kk == 0)
        def _():
            tgt[...] = part + add_ref[...] if has_add else part

        @pl.when(kk > 0)
        def _():
            tgt[...] += part

        if use_acc:
            @pl.when(kk == nk - 1)
            def _():
                o_ref[...] = acc_ref[...].astype(o_ref.dtype)

    operands, in_specs = [a, b], [a_spec, b_spec]
    if has_add:
        operands.append(add)
        in_specs.append(pl.BlockSpec((tm, tn), lambda i, j, kk: (i, j)))
    aliases = {}
    if has_out:
        aliases = {len(operands): 0}
        operands.append(out)
        in_specs.append(pl.BlockSpec(memory_space=pl.ANY))
    if dep is not None:
        operands.append(dep)
        in_specs.append(pl.BlockSpec(memory_space=pl.ANY))
    return pl.pallas_call(
        body,
        out_shape=out_sds,
        grid=(m // tm, n // tn, nk),
        in_specs=in_specs,
        out_specs=o_spec,
        scratch_shapes=[pltpu.VMEM((tm, tn), F32)] if use_acc else [],
        input_output_aliases=aliases,
        compiler_params=pltpu.CompilerParams(dimension_semantics=("parallel", "parallel", "arbitrary")),
        name=name,
    )(*operands)


def _project(name, h, w_t, groups):
    t, d = h.shape
    tm = _tile(t, 512, 16)
    offs = [sum(n for n, _ in groups[:i]) for i in range(len(groups))]

    def body(h_ref, w_ref, *o_refs):
        hv = h_ref[...]
        for (n, _), off, o_ref in zip(groups, offs, o_refs):
            o_ref[...] = _dot(hv, w_ref[off:off + n, :], "nt").astype(o_ref.dtype)

    return pl.pallas_call(
        body,
        out_shape=tuple(_sds((t, n), dt) for n, dt in groups),
        grid=(t // tm,),
        in_specs=[pl.BlockSpec((tm, d), lambda i: (i, 0)), pl.BlockSpec(w_t.shape, lambda i: (0, 0))],
        out_specs=tuple(pl.BlockSpec((tm, n), lambda i: (i, 0)) for n, _ in groups),
        compiler_params=pltpu.CompilerParams(dimension_semantics=("parallel",)),
        name=name,
    )(h, w_t)


def _branch_out(name, a1, w1, a2, w2):
    t = a1.shape[0]
    n1, n2 = w1.shape[1], w2.shape[1]
    tm = _tile(t, 1024, 16)

    def body(a1_ref, w1_ref, a2_ref, w2_ref, o_ref):
        o_ref[:, 0:n1] = _dot(a1_ref[...], w1_ref[...], "nn").astype(o_ref.dtype)
        o_ref[:, n1:n1 + n2] = _dot(a2_ref[...], w2_ref[...], "nn").astype(o_ref.dtype)

    whole = lambda w: pl.BlockSpec(w.shape, lambda i: (0, 0))
    rows = lambda a: pl.BlockSpec((tm, a.shape[1]), lambda i: (i, 0))
    return pl.pallas_call(
        body,
        out_shape=_sds((t, n1 + n2), BF16),
        grid=(t // tm,),
        in_specs=[rows(a1), whole(w1), rows(a2), whole(w2)],
        out_specs=pl.BlockSpec((tm, n1 + n2), lambda i: (i, 0)),
        compiler_params=pltpu.CompilerParams(dimension_semantics=("parallel",)),
        name=name,
    )(a1, w1, a2, w2)


def _branch_out_bwd(name, dy, w1, w2):
    t, d2 = dy.shape
    d = d2 // 2
    tm = _tile(t, 1024, 16)

    def body(dy_ref, w1_ref, w2_ref, o1_ref, o2_ref):
        o1_ref[...] = _dot(dy_ref[:, 0:d], w1_ref[...], "nt").astype(o1_ref.dtype)
        o2_ref[...] = _dot(dy_ref[:, d:d2], w2_ref[...], "nt").astype(o2_ref.dtype)

    whole = lambda w: pl.BlockSpec(w.shape, lambda i: (0, 0))
    return pl.pallas_call(
        body,
        out_shape=(_sds((t, w1.shape[0]), BF16), _sds((t, w2.shape[0]), BF16)),
        grid=(t // tm,),
        in_specs=[pl.BlockSpec((tm, d2), lambda i: (i, 0)), whole(w1), whole(w2)],
        out_specs=(pl.BlockSpec((tm, w1.shape[0]), lambda i: (i, 0)), pl.BlockSpec((tm, w2.shape[0]), lambda i: (i, 0))),
        compiler_params=pltpu.CompilerParams(dimension_semantics=("parallel",)),
        name=name,
    )(dy, w1, w2)


def _ffn_up_bwd(name, du_a, du_b, w_up):
    t, fh = du_a.shape
    d = w_up.shape[0]
    tm = _tile(t, 512, 16)

    def body(da_ref, db_ref, wa_ref, wb_ref, o_ref):
        acc = _dot(da_ref[...], wa_ref[...], "nt") + _dot(db_ref[...], wb_ref[...], "nt")
        o_ref[...] = acc.astype(o_ref.dtype)

    rows = pl.BlockSpec((tm, fh), lambda i: (i, 0))
    return pl.pallas_call(
        body,
        out_shape=_sds((t, d), BF16),
        grid=(t // tm,),
        in_specs=[rows, rows, pl.BlockSpec((d, fh), lambda i: (0, 0)), pl.BlockSpec((d, fh), lambda i: (0, 1))],
        out_specs=pl.BlockSpec((tm, d), lambda i: (i, 0)),
        compiler_params=pltpu.CompilerParams(dimension_semantics=("parallel",)),
        name=name,
    )(du_a, du_b, w_up, w_up)


def _branch_out_dw(name, a1, a2, dy):
    t, d2 = dy.shape
    d = d2 // 2
    tk = _tile(t, 2048, 16)

    def body(a1_ref, a2_ref, dy_ref, o1_ref, o2_ref):
        @pl.when(pl.program_id(0) == 0)
        def _():
            o1_ref[...] = jnp.zeros_like(o1_ref)
            o2_ref[...] = jnp.zeros_like(o2_ref)

        o1_ref[...] += _dot(a1_ref[...], dy_ref[:, 0:d], "tn")
        o2_ref[...] += _dot(a2_ref[...], dy_ref[:, d:d2], "tn")

    rows = lambda a: pl.BlockSpec((tk, a.shape[1]), lambda k: (k, 0))
    acc = lambda a: pl.BlockSpec((a.shape[1], d), lambda k: (0, 0))
    return pl.pallas_call(
        body,
        out_shape=(_sds((a1.shape[1], d), F32), _sds((a2.shape[1], d), F32)),
        grid=(t // tk,),
        in_specs=[rows(a1), rows(a2), rows(dy)],
        out_specs=(acc(a1), acc(a2)),
        compiler_params=pltpu.CompilerParams(dimension_semantics=("arbitrary",)),
        name=name,
    )(a1, a2, dy)


def _rms_fwd(name, x, g):
    t, d = x.shape
    tm = _tile(t, 512, 16)

    def body(x_ref, g_ref, o_ref):
        xv = x_ref[...]
        r = lax.rsqrt(jnp.mean(xv * xv, axis=-1, keepdims=True) + EPS)
        o_ref[...] = ((xv * r) * g_ref[...]).astype(o_ref.dtype)

    return pl.pallas_call(
        body,
        out_shape=_sds((t, d), BF16),
        grid=(t // tm,),
        in_specs=[pl.BlockSpec((tm, d), lambda i: (i, 0)), pl.BlockSpec((1, d), lambda i: (0, 0))],
        out_specs=pl.BlockSpec((tm, d), lambda i: (i, 0)),
        compiler_params=pltpu.CompilerParams(dimension_semantics=("parallel",)),
        name=name,
    )(x, g)


def _rms_bwd(name, x, dh, g, res):
    t, d = x.shape
    tm = _tile(t, 512, 16)

    def body(x_ref, dh_ref, g_ref, res_ref, dx_ref, dg_ref):
        xv = x_ref[...]
        r = lax.rsqrt(jnp.mean(xv * xv, axis=-1, keepdims=True) + EPS)
        xh = xv * r
        dhv = dh_ref[...].astype(F32)
        dxh = dhv * g_ref[...]
        dx_ref[...] = res_ref[...] + r * (dxh - xh * jnp.mean(dxh * xh, axis=-1, keepdims=True))

        @pl.when(pl.program_id(0) == 0)
        def _():
            dg_ref[...] = jnp.zeros_like(dg_ref)

        dg_ref[...] += jnp.sum(dhv * xh, axis=0, keepdims=True)

    row = pl.BlockSpec((tm, d), lambda i: (i, 0))
    vec = pl.BlockSpec((1, d), lambda i: (0, 0))
    return pl.pallas_call(
        body,
        out_shape=(_sds((t, d), F32), _sds((1, d), F32)),
        grid=(t // tm,),
        in_specs=[row, row, vec, row],
        out_specs=(row, vec),
        compiler_params=pltpu.CompilerParams(dimension_semantics=("arbitrary",)),
        name=name,
    )(x, dh, g, res)


def _final_loss(name, x, g, target):
    t, d = x.shape
    tm = _tile(t, 512, 16)

    def body(x_ref, g_ref, t_ref, dx_ref, dxb_ref, loss_ref, dg_ref):
        xv = x_ref[...]
        gv = g_ref[...]
        r = lax.rsqrt(jnp.mean(xv * xv, axis=-1, keepdims=True) + EPS)
        xh = xv * r
        err = xh * gv - t_ref[...]
        dy = err * (1.0 / d)
        dxh = dy * gv
        dx = r * (dxh - xh * jnp.mean(dxh * xh, axis=-1, keepdims=True))
        dx_ref[...] = dx
        dxb_ref[...] = dx.astype(dxb_ref.dtype)
        per_row = jnp.sum(err * err, axis=-1, keepdims=True) * (0.5 / d)

        @pl.when(pl.program_id(0) == 0)
        def _():
            dg_ref[...] = jnp.zeros_like(dg_ref)
            loss_ref[...] = jnp.zeros_like(loss_ref)

        dg_ref[...] += jnp.sum(dy * xh, axis=0, keepdims=True)
        loss_ref[...] += jnp.sum(per_row, axis=0, keepdims=True)

    row = pl.BlockSpec((tm, d), lambda i: (i, 0))
    vec = pl.BlockSpec((1, d), lambda i: (0, 0))
    return pl.pallas_call(
        body,
        out_shape=(_sds((t, d), F32), _sds((t, d), BF16), _sds((1, LANES), F32), _sds((1, d), F32)),
        grid=(t // tm,),
        in_specs=[row, vec, row],
        out_specs=(row, row, pl.BlockSpec((1, LANES), lambda i: (0, 0)), vec),
        compiler_params=pltpu.CompilerParams(dimension_semantics=("arbitrary",)),
        name=name,
    )(x, g, target)


def _shift_down(z, k):
    row = lax.broadcasted_iota(jnp.int32, z.shape, 0)
    return jnp.where(row >= k, pltpu.roll(z, k, axis=0), 0.0)


def _shift_up(z, k):
    s = z.shape[0]
    row = lax.broadcasted_iota(jnp.int32, z.shape, 0)
    return jnp.where(row < s - k, pltpu.roll(z, s - k, axis=0), 0.0)


def _conv3(z, w):
    return (w[2:3] * z + w[0:1] * _shift_down(z, 2)) + w[1:2] * _shift_down(z, 1)


def _conv3_t(dz, w):
    return (w[2:3] * dz + w[0:1] * _shift_up(dz, 2)) + w[1:2] * _shift_up(dz, 1)


def _conv_fwd(name, pc, w, batch, seq, tc):
    cw = w.shape[1]
    nct = cw // tc

    def body(pc_ref, w_ref, o_ref):
        cb = pc_ref[:, 0:tc].astype(F32)
        z = pc_ref[:, tc:2 * tc].astype(F32) * pc_ref[:, 2 * tc:3 * tc].astype(F32)
        o_ref[...] = (cb * _conv3(z, w_ref[...])).astype(o_ref.dtype)

    return pl.pallas_call(
        body,
        out_shape=_sds((batch * seq, cw), BF16),
        grid=(batch, nct),
        in_specs=[pl.BlockSpec((seq, 3 * tc), lambda b, j: (b, j)), pl.BlockSpec((3, tc), lambda b, j: (0, j))],
        out_specs=pl.BlockSpec((seq, tc), lambda b, j: (b, j)),
        compiler_params=pltpu.CompilerParams(dimension_semantics=("parallel", "parallel")),
        name=name,
    )(pc, w)


def _conv_bwd(name, da, pc, w, dproj, batch, seq, tc):
    cw = w.shape[1]
    nct = cw // tc

    def body(da_ref, pc_ref, w_ref, _, dpc_ref, dw_ref):
        wv = w_ref[...]
        cb = pc_ref[:, 0:tc].astype(F32)
        cc = pc_ref[:, tc:2 * tc].astype(F32)
        cin = pc_ref[:, 2 * tc:3 * tc].astype(F32)
        z = cc * cin
        dav = da_ref[...].astype(F32)
        du = dav * cb
        dz = _conv3_t(du, wv)
        dpc_ref[:, 0:tc] = (dav * _conv3(z, wv)).astype(dpc_ref.dtype)
        dpc_ref[:, tc:2 * tc] = (dz * cin).astype(dpc_ref.dtype)
        dpc_ref[:, 2 * tc:3 * tc] = (dz * cc).astype(dpc_ref.dtype)

        @pl.when(pl.program_id(1) == 0)
        def _():
            dw_ref[...] = jnp.zeros_like(dw_ref)

        dw_ref[0:1, :] += jnp.sum(du * _shift_down(z, 2), axis=0, keepdims=True)
        dw_ref[1:2, :] += jnp.sum(du * _shift_down(z, 1), axis=0, keepdims=True)
        dw_ref[2:3, :] += jnp.sum(du * z, axis=0, keepdims=True)

    return pl.pallas_call(
        body,
        out_shape=(_sds(dproj.shape, dproj.dtype), _sds((3, cw), F32)),
        grid=(nct, batch),
        in_specs=[
            pl.BlockSpec((seq, tc), lambda j, b: (b, j)),
            pl.BlockSpec((seq, 3 * tc), lambda j, b: (b, j)),
            pl.BlockSpec((3, tc), lambda j, b: (0, j)),
            pl.BlockSpec(memory_space=pl.ANY),
        ],
        out_specs=(pl.BlockSpec((seq, 3 * tc), lambda j, b: (b, j)), pl.BlockSpec((3, tc), lambda j, b: (0, j))),
        input_output_aliases={3: 0},
        compiler_params=pltpu.CompilerParams(dimension_semantics=("parallel", "arbitrary")),
        name=name,
    )(da, pc, w, dproj)


def _ffn_up_act(name, h2, w_up, w, batch, seq, tc):
    d = h2.shape[1]
    fh = w.shape[1] // 2
    nf = fh // tc

    def body(h_ref, ma_ref, mb_ref, wa_ref, wb_ref, o_ref, ua_ref, ub_ref, a_ref, b_ref):
        hv = h_ref[...]
        ua = _dot(hv, ma_ref[...], "nn")
        ub = _dot(hv, mb_ref[...], "nn")
        ua_ref[...] = ua.astype(ua_ref.dtype)
        ub_ref[...] = ub.astype(ub_ref.dtype)
        a = _conv3(ua, wa_ref[...])
        b = _conv3(ub, wb_ref[...])
        a_ref[...] = a.astype(a_ref.dtype)
        b_ref[...] = b.astype(b_ref.dtype)
        o_ref[...] = (a * jax.nn.sigmoid(a) * b).astype(o_ref.dtype)

    act = pl.BlockSpec((seq, tc), lambda b, j: (b, j))
    shape = _sds((batch * seq, fh), BF16)
    return pl.pallas_call(
        body,
        out_shape=(shape,) * 5,
        grid=(batch, nf),
        in_specs=[
            pl.BlockSpec((seq, d), lambda b, j: (b, 0)),
            pl.BlockSpec((d, tc), lambda b, j: (0, j)),
            pl.BlockSpec((d, tc), lambda b, j: (0, nf + j)),
            pl.BlockSpec((3, tc), lambda b, j: (0, j)),
            pl.BlockSpec((3, tc), lambda b, j: (0, nf + j)),
        ],
        out_specs=(act,) * 5,
        compiler_params=pltpu.CompilerParams(dimension_semantics=("parallel", "parallel")),
        name=name,
    )(h2, w_up, w_up, w, w)


def _ffn_bwd(name, dx, w_down, ua, ub, av, bv, w, batch, seq, tc):
    d = dx.shape[1]
    fh = w.shape[1] // 2
    nf = fh // tc

    rb = _tile(seq, 128, 8)
    halo = 8

    def body(dx_ref, md_ref, ua_ref, ub_ref, a_ref, b_ref, wa_ref, wb_ref, dua_ref, dub_ref, dw_ref,
             dh_scr, da_scr, db_scr):
        j = pl.program_id(1)
        dh_scr[...] = _dot(dx_ref[...].astype(BF16), md_ref[...], "nt")
        da_scr[seq:seq + halo, :] = jnp.zeros((halo, tc), F32)
        db_scr[seq:seq + halo, :] = jnp.zeros((halo, tc), F32)

        def silu_bwd(r, carry):
            rows = pl.ds(pl.multiple_of(r * rb, rb), rb)
            a, b, dhv = a_ref[rows, :].astype(F32), b_ref[rows, :].astype(F32), dh_scr[rows, :]
            sg = jax.nn.sigmoid(a)
            da_scr[rows, :] = dhv * b * (sg * (1.0 + a * (1.0 - sg)))
            db_scr[rows, :] = dhv * (a * sg)
            return carry

        lax.fori_loop(0, seq // rb, silu_bwd, 0)
        wa, wb = wa_ref[...], wb_ref[...]

        def conv_bwd(r, sums):
            r0 = pl.multiple_of(r * rb, rb)
            rows = pl.ds(r0, rb)
            out = []
            for d_scr, u_ref, wv, du_ref, acc in ((da_scr, ua_ref, wa, dua_ref, sums[0:3]),
                                                  (db_scr, ub_ref, wb, dub_ref, sums[3:6])):
                x = d_scr[pl.ds(r0, rb + halo), :]
                dv = x[0:rb]
                up1 = pltpu.roll(x, rb + halo - 1, axis=0)[0:rb]
                up2 = pltpu.roll(x, rb + halo - 2, axis=0)[0:rb]
                du_ref[rows, :] = ((wv[2:3] * dv + wv[0:1] * up2) + wv[1:2] * up1).astype(du_ref.dtype)
                uv = u_ref[rows, :].astype(F32)
                out += [acc[0] + jnp.sum(up2 * uv, axis=0, keepdims=True),
                        acc[1] + jnp.sum(up1 * uv, axis=0, keepdims=True),
                        acc[2] + jnp.sum(dv * uv, axis=0, keepdims=True)]
            return tuple(out)

        sums = lax.fori_loop(0, seq // rb, conv_bwd, (jnp.zeros((1, tc), F32),) * 6)

        @pl.when((pl.program_id(0) == 0) & (j == 0))
        def _():
            dw_ref[...] = jnp.zeros_like(dw_ref)

        for half, off in enumerate((0, fh)):
            cols = pl.ds(pl.multiple_of(off + j * tc, LANES), tc)
            for k in range(3):
                dw_ref[k:k + 1, cols] += sums[3 * half + k]

    act = pl.BlockSpec((seq, tc), lambda b, j: (b, j))
    shape = _sds((batch * seq, fh), BF16)
    return pl.pallas_call(
        body,
        out_shape=(shape, shape, _sds((3, 2 * fh), F32)),
        grid=(batch, nf),
        in_specs=[
            pl.BlockSpec((seq, d), lambda b, j: (b, 0)),
            pl.BlockSpec((tc, d), lambda b, j: (j, 0)),
            act,
            act,
            act,
            act,
            pl.BlockSpec((3, tc), lambda b, j: (0, j)),
            pl.BlockSpec((3, tc), lambda b, j: (0, nf + j)),
        ],
        out_specs=(act, act, pl.BlockSpec((3, 2 * fh), lambda b, j: (0, 0))),
        scratch_shapes=[pltpu.VMEM((seq, tc), F32), pltpu.VMEM((seq + halo, tc), F32),
                        pltpu.VMEM((seq + halo, tc), F32)],
        compiler_params=pltpu.CompilerParams(dimension_semantics=("arbitrary", "arbitrary")),
        name=name,
    )(dx, w_down, ua, ub, av, bv, w, w)


def _merge_fwd(name, ycat, gl, bg):
    t, d2 = ycat.shape
    d = d2 // 2
    tm = _tile(t, 1024, 16)

    def body(y_ref, gl_ref, bg_ref, o_ref):
        g = jax.nn.sigmoid(gl_ref[...].astype(F32) + bg_ref[...])
        prod = g * y_ref[...].astype(F32)
        o_ref[...] = (prod[:, 0:d] + prod[:, d:d2]).astype(o_ref.dtype)

    row = pl.BlockSpec((tm, d2), lambda i: (i, 0))
    return pl.pallas_call(
        body,
        out_shape=_sds((t, d), BF16),
        grid=(t // tm,),
        in_specs=[row, row, pl.BlockSpec((1, d2), lambda i: (0, 0))],
        out_specs=pl.BlockSpec((tm, d), lambda i: (i, 0)),
        compiler_params=pltpu.CompilerParams(dimension_semantics=("parallel",)),
        name=name,
    )(ycat, gl, bg)


def _merge_bwd(name, dm, ycat, gl, bg, width, gl_off):
    t, d2 = ycat.shape
    d = d2 // 2
    tm = _tile(t, 1024, 16)
    wb = math.gcd(gl_off, d)
    nw = d // wb

    def body(dm_ref, y_ref, gl_ref, bg_ref, dgl_ref, dy_ref, dbg_ref):
        g = jax.nn.sigmoid(gl_ref[...].astype(F32) + bg_ref[...])
        dmv = dm_ref[...].astype(F32)
        dgl = dmv * y_ref[...].astype(F32) * (g * (1.0 - g))
        dgl_ref[...] = dgl.astype(dgl_ref.dtype)
        dy_ref[...] = (dmv * g).astype(dy_ref.dtype)

        @pl.when(pl.program_id(2) == 0)
        def _():
            dbg_ref[...] = jnp.zeros_like(dbg_ref)

        dbg_ref[...] += jnp.sum(dgl, axis=0, keepdims=True)

    half = pl.BlockSpec((tm, wb), lambda h, j, i: (i, h * nw + j))
    vec = pl.BlockSpec((1, wb), lambda h, j, i: (0, h * nw + j))
    return pl.pallas_call(
        body,
        out_shape=(_sds((t, width), BF16), _sds((t, d2), BF16), _sds((1, d2), F32)),
        grid=(2, nw, t // tm),
        in_specs=[pl.BlockSpec((tm, wb), lambda h, j, i: (i, j)), half, half, vec],
        out_specs=(pl.BlockSpec((tm, wb), lambda h, j, i: (i, gl_off // wb + h * nw + j)), half, vec),
        compiler_params=pltpu.CompilerParams(dimension_semantics=("parallel", "parallel", "arbitrary")),
        name=name,
    )(dm, ycat, gl, bg)


def _log_sigmoid(z):
    return jnp.minimum(z, 0.0) - jnp.log1p(jnp.exp(-jnp.abs(z)))


def _forget_fwd(name, fl, bf, batch, seq):
    def body(fl_ref, bf_ref, o_ref):
        lf = _log_sigmoid(fl_ref[:, 0:LANES] + bf_ref[:, 0:LANES])
        acc = lf.T[0:HEADS, :]
        lane = lax.broadcasted_iota(jnp.int32, acc.shape, 1)
        k = 1
        while k < seq:
            acc = acc + jnp.where(lane >= k, pltpu.roll(acc, k, axis=1), 0.0)
            k *= 2
        o_ref[...] = acc

    return pl.pallas_call(
        body,
        out_shape=_sds((batch, HEADS, seq), F32),
        grid=(batch,),
        in_specs=[pl.BlockSpec((seq, F_PAD), lambda b: (b, 0)), pl.BlockSpec((1, F_PAD), lambda b: (0, 0))],
        out_specs=pl.BlockSpec((None, HEADS, seq), lambda b: (b, 0, 0)),
        compiler_params=pltpu.CompilerParams(dimension_semantics=("parallel",)),
        name=name,
    )(fl, bf)


def _forget_bwd(name, d_key, d_query, fl, bf, dproj, f_off, batch, seq):
    nfb = F_PAD // LANES

    def body(dk_ref, dq_ref, fl_ref, bf_ref, _, df_ref, dbf_ref):
        jj = pl.program_id(1)
        key_t = jnp.concatenate([dk_ref[...], jnp.zeros((LANES - HEADS, seq), F32)], axis=0).T
        acc = dq_ref[...] - key_t
        row = lax.broadcasted_iota(jnp.int32, acc.shape, 0)
        k = 1
        while k < seq:
            acc = acc + jnp.where(row < seq - k, pltpu.roll(acc, seq - k, axis=0), 0.0)
            k *= 2
        z = fl_ref[:, 0:LANES] + bf_ref[:, 0:LANES]
        col = lax.broadcasted_iota(jnp.int32, acc.shape, 1)
        df = jnp.where(col < HEADS, acc * jax.nn.sigmoid(-z), 0.0)
        df = jnp.where(jj == 0, df, 0.0)
        df_ref[...] = df.astype(df_ref.dtype)

        @pl.when((pl.program_id(0) == 0) & (jj == 0))
        def _():
            dbf_ref[...] = jnp.zeros_like(dbf_ref)

        dbf_ref[...] += jnp.sum(df, axis=0, keepdims=True)

    return pl.pallas_call(
        body,
        out_shape=(_sds(dproj.shape, dproj.dtype), _sds((1, LANES), F32)),
        grid=(batch, nfb),
        in_specs=[
            pl.BlockSpec((None, HEADS, seq), lambda b, j: (b, 0, 0)),
            pl.BlockSpec((seq, LANES), lambda b, j: (b, 0)),
            pl.BlockSpec((seq, F_PAD), lambda b, j: (b, 0)),
            pl.BlockSpec((1, F_PAD), lambda b, j: (0, 0)),
            pl.BlockSpec(memory_space=pl.ANY),
        ],
        out_specs=(pl.BlockSpec((seq, LANES), lambda b, j: (b, f_off // LANES + j)),
                   pl.BlockSpec((1, LANES), lambda b, j: (0, 0))),
        input_output_aliases={4: 0},
        compiler_params=pltpu.CompilerParams(dimension_semantics=("arbitrary", "arbitrary")),
        name=name,
    )(d_key, d_query, fl, bf, dproj)


def _dot(a, b, mode):
    return lax.dot_general(a, b, _DIMS[mode], preferred_element_type=F32)


def _attn_fwd(name, qkv, frow, batch, seq, tq):
    nq = seq // tq
    scale = 1.0 / math.sqrt(HEAD_DIM)

    def body(q_ref, k_ref, v_ref, f_ref, o_ref, lse_ref):
        i = pl.program_id(2)
        lane = lax.broadcasted_iota(jnp.int32, (1, LANES), 1)
        lo = lane < HEAD_DIM
        qs = q_ref[...] * scale
        qh = (jnp.where(lo, qs, 0.0).astype(BF16), jnp.where(lo, 0.0, qs).astype(BF16))
        row = lax.broadcasted_iota(jnp.int32, (tq, tq), 0)
        col = lax.broadcasted_iota(jnp.int32, (tq, tq), 1)

        def step(j, carry, diag):
            m0, l0, m1, l1, acc = carry
            start = pl.multiple_of(j * tq, tq)
            kj = k_ref[pl.ds(start, tq), :]
            vj = v_ref[pl.ds(start, tq), :]
            ms, ls, pvs, alphas = [], [], [], []
            for h, (m_old, l_old) in enumerate(((m0, l0), (m1, l1))):
                s = _dot(qh[h], kj, "nt") - f_ref[h:h + 1, pl.ds(start, tq)]
                if diag:
                    s = jnp.where(col <= row, s, NEG_BIG)
                m_new = jnp.maximum(m_old, jnp.max(s, axis=1, keepdims=True))
                p = jnp.exp(s - m_new)
                alpha = jnp.exp(m_old - m_new)
                ls.append(alpha * l_old + jnp.sum(p, axis=1, keepdims=True))
                ms.append(m_new)
                alphas.append(alpha)
                vh = jnp.where(lo, vj, 0.0) if h == 0 else jnp.where(lo, 0.0, vj)
                pvs.append(_dot(p.astype(BF16), vh.astype(BF16), "nn"))
            acc = acc * jnp.where(lo, alphas[0], alphas[1]) + (pvs[0] + pvs[1])
            return ms[0], ls[0], ms[1], ls[1], acc

        neg = jnp.full((tq, 1), NEG_BIG, F32)
        zero = jnp.zeros((tq, 1), F32)
        init = (neg, zero, neg, zero, jnp.zeros((tq, LANES), F32))
        carry = lax.fori_loop(0, i, lambda j, c: step(j, c, False), init)
        m0, l0, m1, l1, acc = step(i, carry, True)
        o_ref[...] = (acc / jnp.where(lo, l0, l1)).astype(o_ref.dtype)
        lse_ref[:, 0:1] = m0 + jnp.log(l0)
        lse_ref[:, 1:2] = m1 + jnp.log(l1)

    return pl.pallas_call(
        body,
        out_shape=(_sds((batch * seq, ATTN_WIDTH), BF16), _sds((HEAD_PAIRS, batch * seq, 2), F32)),
        grid=(batch, HEAD_PAIRS, nq),
        in_specs=[
            pl.BlockSpec((tq, LANES), lambda b, hp, i: (b * nq + i, 3 * hp)),
            pl.BlockSpec((seq, LANES), lambda b, hp, i: (b, 3 * hp + 1)),
            pl.BlockSpec((seq, LANES), lambda b, hp, i: (b, 3 * hp + 2)),
            pl.BlockSpec((None, None, 2, seq), lambda b, hp, i: (b, hp, 0, 0)),
        ],
        out_specs=(
            pl.BlockSpec((tq, LANES), lambda b, hp, i: (b * nq + i, hp)),
            pl.BlockSpec((None, tq, 2), lambda b, hp, i: (hp, b * nq + i, 0)),
        ),
        compiler_params=pltpu.CompilerParams(dimension_semantics=("parallel", "parallel", "parallel")),
        name=name,
    )(qkv, qkv, qkv, frow)


def _attn_bwd(name, qkv, do, o, lse, frow, dproj, qkv_off, batch, seq, tq):
    nq = seq // tq
    scale = 1.0 / math.sqrt(HEAD_DIM)

    def body(q_ref, k_ref, v_ref, do_ref, o_ref, lse_ref, f_ref, _, dqkv_ref, df_ref, drow_ref,
             dq_acc, dk_acc, dv_acc, df_acc):
        j = pl.program_id(2)
        lane = lax.broadcasted_iota(jnp.int32, (1, LANES), 1)
        lo = lane < HEAD_DIM
        masks = (lo, jnp.logical_not(lo))
        row = lax.broadcasted_iota(jnp.int32, (tq, tq), 0)
        col = lax.broadcasted_iota(jnp.int32, (tq, tq), 1)

        @pl.when(j == 0)
        def _():
            dq_acc[...] = jnp.zeros_like(dq_acc)
            drow_ref[...] = jnp.zeros_like(drow_ref)

        dk_acc[...] = jnp.zeros_like(dk_acc)
        dv_acc[...] = jnp.zeros_like(dv_acc)
        df_acc[...] = jnp.zeros_like(df_acc)
        kj = k_ref[...]
        vj = v_ref[...]
        kstart = pl.multiple_of(j * tq, tq)
        kh = tuple(jnp.where(mk, kj, 0.0).astype(BF16) for mk in masks)

        def step(i, diag):
            start = pl.multiple_of(i * tq, tq)
            rows = pl.ds(start, tq)
            qi = q_ref[rows, :] * scale
            doi = do_ref[rows, :]
            prod = doi.astype(F32) * o_ref[rows, :].astype(F32)
            lse_i = lse_ref[rows, :]
            dq_i = jnp.zeros((tq, LANES), F32)
            for h, mk in enumerate(masks):
                q_h = jnp.where(mk, qi, 0.0).astype(BF16)
                do_h = jnp.where(mk, doi, 0.0).astype(BF16)
                delta = jnp.sum(jnp.where(mk, prod, 0.0), axis=1, keepdims=True)
                s = _dot(q_h, kj, "nt") - f_ref[h:h + 1, pl.ds(kstart, tq)]
                p = jnp.exp(s - lse_i[:, h:h + 1])
                if diag:
                    p = jnp.where(col <= row, p, 0.0)
                ds = p * (_dot(do_h, vj, "nt") - delta)
                df_acc[h:h + 1, :] += jnp.sum(ds, axis=0, keepdims=True)
                drow_ref[rows, h:h + 1] += jnp.sum(ds, axis=1, keepdims=True)
                dsb = ds.astype(BF16)
                dv_acc[...] += _dot(p.astype(BF16), do_h, "tn")
                dk_acc[...] += _dot(dsb, q_h, "tn")
                dq_i = dq_i + _dot(dsb, kh[h], "nn")
            dq_acc[rows, :] += dq_i

        step(j, True)
        lax.fori_loop(j + 1, nq, lambda i, c: (step(i, False), c)[1], 0)
        dqkv_ref[:, 0:LANES] = (dq_acc[pl.ds(kstart, tq), :] * scale).astype(dqkv_ref.dtype)
        dqkv_ref[:, LANES:2 * LANES] = dk_acc[...].astype(dqkv_ref.dtype)
        dqkv_ref[:, 2 * LANES:3 * LANES] = dv_acc[...].astype(dqkv_ref.dtype)
        df_ref[...] = df_acc[...]

    full = lambda c: pl.BlockSpec((seq, LANES), lambda b, hp, j: (b, c(hp)))
    blk = lambda c: pl.BlockSpec((tq, LANES), lambda b, hp, j: (b * nq + j, c(hp)))
    return pl.pallas_call(
        body,
        out_shape=(_sds(dproj.shape, dproj.dtype), _sds((batch, HEAD_PAIRS, 2, seq), F32),
                   _sds((HEAD_PAIRS, batch * seq, 2), F32)),
        grid=(batch, HEAD_PAIRS, nq),
        in_specs=[
            full(lambda hp: 3 * hp),
            blk(lambda hp: 3 * hp + 1),
            blk(lambda hp: 3 * hp + 2),
            full(lambda hp: hp),
            full(lambda hp: hp),
            pl.BlockSpec((None, seq, 2), lambda b, hp, j: (hp, b, 0)),
            pl.BlockSpec((None, None, 2, seq), lambda b, hp, j: (b, hp, 0, 0)),
            pl.BlockSpec(memory_space=pl.ANY),
        ],
        out_specs=(
            pl.BlockSpec((tq, 3 * LANES), lambda b, hp, j: (b * nq + j, qkv_off // (3 * LANES) + hp)),
            pl.BlockSpec((None, None, 2, tq), lambda b, hp, j: (b, hp, 0, j)),
            pl.BlockSpec((None, seq, 2), lambda b, hp, j: (hp, b, 0)),
        ),
        scratch_shapes=[
            pltpu.VMEM((seq, LANES), F32),
            pltpu.VMEM((tq, LANES), F32),
            pltpu.VMEM((tq, LANES), F32),
            pltpu.VMEM((2, tq), F32),
        ],
        input_output_aliases={7: 0},
        compiler_params=pltpu.CompilerParams(dimension_semantics=("parallel", "parallel", "arbitrary")),
        name=name,
    )(qkv, qkv, qkv, do, o, lse, frow, dproj)


def _mesh_place():
    x, y, c = lax.axis_index("x"), lax.axis_index("y"), lax.axis_index("c")
    chips = [(1 - x, y), (x, 1 - y), (1 - x, 1 - y)]
    return x, y, c, chips


def _hbm_specs(n):
    return [pl.BlockSpec(memory_space=pl.ANY)] * n


def _half(shape2d, axis, which):
    size = shape2d[axis] // 2
    sl = pl.ds(pl.multiple_of(which * size, 16 if axis == 0 else LANES), size)
    return (sl, slice(None)) if axis == 0 else (slice(None), sl)


def _gather_weights(bigs, axes, smalls):
    nb, ns = len(bigs), len(smalls)
    arrays = list(bigs) + list(smalls)
    n = nb + ns

    def body(*refs):
        ins, outs = refs[:n], refs[n:2 * n]
        send_sems, recv_sems = refs[2 * n:]
        x, y, c, chips = _mesh_place()
        me = 2 * x + y
        sibling = (x, y, 1 - c)

        def half(a, which):
            return _half(arrays[a].shape, axes[a], which)

        def copy(a, k, src, dst, to):
            return pltpu.make_async_remote_copy(src_ref=src, dst_ref=dst, send_sem=send_sems.at[a, k],
                                                recv_sem=recv_sems.at[a, k], device_id=to, device_id_type=MESH)

        sends = []
        for a in range(n):
            for j, chip in enumerate(chips):
                if a < nb:
                    cp = copy(a, j, ins[a].at[half(a, c)], outs[a].at[(me,) + half(a, c)], (*chip, c))
                else:
                    cp = copy(a, j, ins[a], outs[a].at[me], (*chip, c))
                cp.start()
                sends.append(cp)
        for a in range(nb):
            for j, (px, py) in enumerate(chips):
                blk = outs[a].at[(2 * px + py,) + half(a, c)]
                copy(a, j, blk, blk, (px, py, c)).wait_recv()
                fwd = copy(a, 3 + j, blk, blk, sibling)
                fwd.start()
                sends.append(fwd)
        for a in range(nb, n):
            for j, (px, py) in enumerate(chips):
                blk = outs[a].at[2 * px + py]
                copy(a, j, blk, blk, (px, py, c)).wait_recv()
        for a in range(nb):
            for j, (px, py) in enumerate(chips):
                blk = outs[a].at[(2 * px + py,) + half(a, 1 - c)]
                copy(a, 3 + j, blk, blk, sibling).wait_recv()
        for cp in sends:
            cp.wait_send()

    outs = pl.pallas_call(
        body,
        out_shape=tuple(_sds((N_CHIPS,) + a.shape, a.dtype) for a in arrays),
        in_specs=_hbm_specs(n),
        out_specs=tuple(_hbm_specs(n)),
        scratch_shapes=[pltpu.SemaphoreType.DMA((n, 6)), pltpu.SemaphoreType.DMA((n, 6))],
        name="gather_weights",
    )(*arrays)
    me = 2 * lax.axis_index("x") + lax.axis_index("y")
    return tuple(lax.dynamic_update_index_in_dim(o, a, me, 0) for o, a in zip(outs, arrays))


def _gather_small(v):
    m_per, ncol = v.shape

    def body(x_ref, out_ref, send_sems, recv_sems, local_sem):
        x, y, c, chips = _mesh_place()
        me, sibling = (x, y, c), (x, y, 1 - c)

        def rows(px, py, pc):
            return out_ref.at[pl.ds((4 * px + 2 * py + pc) * m_per, m_per), :]

        def copy(k, block, to, src=None):
            return pltpu.make_async_remote_copy(src_ref=rows(*block) if src is None else src, dst_ref=rows(*block),
                                                send_sem=send_sems.at[k], recv_sem=recv_sems.at[k],
                                                device_id=to, device_id_type=MESH)

        mine = pltpu.make_async_copy(x_ref, rows(*me), local_sem)
        mine.start()
        first = [copy(0, me, sibling, src=x_ref)]
        first += [copy(1 + j, me, (*chip, c), src=x_ref) for j, chip in enumerate(chips)]
        for cp in first:
            cp.start()
        passed = [copy(4 + j, (*chip, c), sibling) for j, chip in enumerate(chips)]
        for j, chip in enumerate(chips):
            copy(1 + j, (*chip, c), me).wait_recv()
            passed[j].start()
        copy(0, sibling, me).wait_recv()
        for j, chip in enumerate(chips):
            copy(4 + j, (*chip, 1 - c), me).wait_recv()
        for cp in first + passed:
            cp.wait_send()
        mine.wait()

    return pl.pallas_call(
        body,
        out_shape=_sds((N_DEV * m_per, ncol), v.dtype),
        in_specs=[pl.BlockSpec(memory_space=pltpu.VMEM)],
        out_specs=pl.BlockSpec(memory_space=pltpu.VMEM),
        scratch_shapes=[pltpu.SemaphoreType.DMA((7,)), pltpu.SemaphoreType.DMA((7,)), pltpu.SemaphoreType.DMA],
        name="gather_small",
    )(v)


def _half_shape(shape2d, axis):
    return (shape2d[0] // 2, shape2d[1]) if axis == 0 else (shape2d[0], shape2d[1] // 2)


def _exchange_sibling(name, grads, axes):
    n = len(grads)

    def body(*refs):
        ins, outs = refs[:n], refs[n:2 * n]
        send_sems, recv_sems = refs[2 * n:]
        x, y, c, _ = _mesh_place()
        copies = []
        for a in range(n):
            src = ins[a].at[(slice(None),) + _half(grads[a].shape[1:], axes[a], 1 - c)]
            cp = pltpu.make_async_remote_copy(src_ref=src, dst_ref=outs[a], send_sem=send_sems.at[a],
                                              recv_sem=recv_sems.at[a], device_id=(x, y, 1 - c), device_id_type=MESH)
            cp.start()
            copies.append(cp)
        for cp in copies:
            cp.wait()

    return pl.pallas_call(
        body,
        out_shape=tuple(_sds((N_CHIPS,) + _half_shape(g.shape[1:], ax), g.dtype) for g, ax in zip(grads, axes)),
        in_specs=_hbm_specs(n),
        out_specs=tuple(_hbm_specs(n)),
        scratch_shapes=[pltpu.SemaphoreType.DMA((n,)), pltpu.SemaphoreType.DMA((n,))],
        name=name,
    )(*grads)


_HBM = pl.BlockSpec(memory_space=pltpu.HBM)
_SEM = pl.BlockSpec(memory_space=pltpu.SEMAPHORE)
_EFFECT = pltpu.SideEffectType.DATAFLOW_SIDE_EFFECTING


def _chip_copies(kind, srcs, lands, send_sems, recv_sems):
    x, y, c, chips = _mesh_place()
    copies = []
    for a in range(len(srcs)):
        for j, (px, py) in enumerate(chips):
            if kind == "gather":
                src, dst = srcs[a], lands[a].at[2 * x + y]
            else:
                src, dst = srcs[a].at[j], lands[a].at[j]
            copies.append(pltpu.make_async_remote_copy(src_ref=src, dst_ref=dst, send_sem=send_sems.at[3 * a + j],
                                                       recv_sem=recv_sems.at[3 * a + j], device_id=(px, py, c),
                                                       device_id_type=MESH))
    return copies


def _chips_start(name, kind, srcs):
    n = len(srcs)
    slots = N_CHIPS if kind == "gather" else 3
    lands = [lax.empty((slots,) + (s.shape if kind == "gather" else s.shape[1:]), s.dtype) for s in srcs]

    def body(*refs):
        for cp in _chip_copies(kind, refs[:n], refs[n:2 * n], refs[2 * n], refs[2 * n + 1]):
            cp.start()
        refs[-1][...] = jnp.zeros_like(refs[-1])

    outs = pl.pallas_call(
        body,
        out_shape=(pltpu.SemaphoreType.DMA((3 * n,)), pltpu.SemaphoreType.DMA((3 * n,)),
                   *[pltpu.HBM(v.shape, v.dtype) for v in (*srcs, *lands)], _sds((8, LANES), F32)),
        in_specs=[_HBM] * (2 * n),
        out_specs=(_SEM, _SEM, *[_HBM] * (2 * n), pl.BlockSpec(memory_space=pltpu.VMEM)),
        input_output_aliases={i: 2 + i for i in range(2 * n)},
        compiler_params=pltpu.CompilerParams(has_side_effects=_EFFECT),
        name=name,
    )(*[pltpu.with_memory_space_constraint(v, pltpu.HBM) for v in (*srcs, *lands)])
    return outs[:-1], outs[-1]


def _chips_wait(name, kind, handles, after):
    send_sems, recv_sems, *thru = handles
    n = len(thru) // 2

    def body(*refs):
        for cp in _chip_copies(kind, refs[:n], refs[n:2 * n], refs[2 * n], refs[2 * n + 1]):
            cp.wait_send()
            cp.wait_recv()

    outs = pl.pallas_call(
        body,
        out_shape=tuple(pltpu.HBM(v.shape, v.dtype) for v in thru),
        in_specs=[_HBM] * (2 * n) + [_SEM, _SEM, pl.BlockSpec(memory_space=pl.ANY)],
        out_specs=tuple([_HBM] * (2 * n)),
        input_output_aliases={i: i for i in range(2 * n)},
        compiler_params=pltpu.CompilerParams(has_side_effects=_EFFECT),
        name=name,
    )(*thru, send_sems, recv_sems, after)
    return outs[n:]


def _share_sibling(name, shards, axes):
    n = len(shards)

    def body(*refs):
        ins, outs = refs[:n], refs[n:2 * n]
        send_sems, recv_sems = refs[2 * n:]
        x, y, c, _ = _mesh_place()
        started = []
        for a in range(n):
            mine = _half(shards[a].shape, axes[a], c)
            theirs = _half(shards[a].shape, axes[a], 1 - c)
            cp = pltpu.make_async_remote_copy(src_ref=ins[a].at[mine], dst_ref=outs[a].at[mine],
                                              send_sem=send_sems.at[a], recv_sem=recv_sems.at[a],
                                              device_id=(x, y, 1 - c), device_id_type=MESH)
            cp.start()
            arrival = pltpu.make_async_remote_copy(src_ref=ins[a].at[theirs], dst_ref=outs[a].at[theirs],
                                                   send_sem=send_sems.at[a], recv_sem=recv_sems.at[a],
                                                   device_id=(x, y, 1 - c), device_id_type=MESH)
            started.append((cp, arrival))
        for cp, arrival in started:
            arrival.wait_recv()
            cp.wait_send()

    return pl.pallas_call(
        body,
        out_shape=tuple(_sds(s.shape, s.dtype) for s in shards),
        in_specs=_hbm_specs(n),
        out_specs=tuple(_hbm_specs(n)),
        scratch_shapes=[pltpu.SemaphoreType.DMA((n,)), pltpu.SemaphoreType.DMA((n,))],
        input_output_aliases={a: a for a in range(n)},
        name=name,
    )(*shards)


def _pair_sum(name, place, g, got, axis):
    hr, hc = got.shape[1:]

    def body(place_ref, g_ref, got_ref, o_ref):
        o_ref[...] = (g_ref[...] + got_ref[...]).astype(o_ref.dtype)

    blk = (None, hr, hc)
    mine = (lambda j, pr: (pr[2 + j], pr[1], 0)) if axis == 0 else (lambda j, pr: (pr[2 + j], 0, pr[1]))
    return pl.pallas_call(
        body,
        out_shape=_sds((N_CHIPS - 1, hr, hc), BF16),
        grid_spec=pltpu.PrefetchScalarGridSpec(
            num_scalar_prefetch=1,
            grid=(N_CHIPS - 1,),
            in_specs=[pl.BlockSpec(blk, mine), pl.BlockSpec(blk, lambda j, pr: (pr[2 + j], 0, 0))],
            out_specs=pl.BlockSpec(blk, lambda j, pr: (j, 0, 0)),
        ),
        compiler_params=pltpu.CompilerParams(dimension_semantics=("parallel",)),
        name=name,
    )(place, g, got)


def _chip_sum(name, place, g, got, arrivals, axis):
    _, r, cdim = g.shape
    hr, hc = got.shape[1:]

    def body(place_ref, g_ref, got_ref, arr_ref, o_ref):
        acc = g_ref[...] + got_ref[...]
        for j in range(3):
            acc = acc + arr_ref[j].astype(F32)
        o_ref[...] = acc

    blk = (None, hr, hc)
    mine = (lambda i, pr: (pr[0], pr[1], 0)) if axis == 0 else (lambda i, pr: (pr[0], 0, pr[1]))
    dest = (lambda i, pr: (pr[1], 0)) if axis == 0 else (lambda i, pr: (0, pr[1]))
    return pl.pallas_call(
        body,
        out_shape=_sds((r, cdim), F32),
        grid_spec=pltpu.PrefetchScalarGridSpec(
            num_scalar_prefetch=1,
            grid=(1,),
            in_specs=[
                pl.BlockSpec(blk, mine),
                pl.BlockSpec(blk, lambda i, pr: (pr[0], 0, 0)),
                pl.BlockSpec((3, hr, hc), lambda i, pr: (0, 0, 0)),
            ],
            out_specs=pl.BlockSpec((hr, hc), dest),
        ),
        compiler_params=pltpu.CompilerParams(dimension_semantics=("arbitrary",)),
        name=name,
    )(place, g, got, arrivals)


def _device_sum(name, gathered):
    m_per = gathered.shape[0] // N_DEV

    def body(g_ref, o_ref):
        acc = g_ref[0:m_per, :]
        for dev in range(1, N_DEV):
            acc = acc + g_ref[dev * m_per:(dev + 1) * m_per, :]
        o_ref[...] = acc

    return pl.pallas_call(body, out_shape=_sds((m_per, gathered.shape[1]), F32), name=name)(gathered)


def _adamw(name, w, g, m, v):
    r, cdim = w.shape
    if r % 8 == 0:
        tr, tcol = _tile(r, 256, 8), cdim
    else:
        tr, tcol = r, (_tile(cdim, 256, LANES) if cdim % LANES == 0 else cdim)
    blk = pl.BlockSpec((tr, tcol), lambda i, j: (i, j))
    grid = (r // tr, cdim // tcol)
    bc1 = 1.0 - ADAM_B1 ** ADAM_STEP
    bc2 = 1.0 - ADAM_B2 ** ADAM_STEP

    def body(w_ref, g_ref, m_ref, v_ref, d_ref, nm_ref, nv_ref):
        gv = g_ref[...]
        nm = ADAM_B1 * m_ref[...] + (1.0 - ADAM_B1) * gv
        nv = ADAM_B2 * v_ref[...] + (1.0 - ADAM_B2) * (gv * gv)
        d_ref[...] = -ADAM_LR * ((nm / bc1) / (jnp.sqrt(nv / bc2) + ADAM_EPS) + ADAM_WD * w_ref[...])
        nm_ref[...] = nm
        nv_ref[...] = nv

    shape = _sds(w.shape, F32)
    return pl.pallas_call(
        body,
        out_shape=(shape, shape, shape),
        grid=grid,
        in_specs=[blk] * 4,
        out_specs=(blk, blk, blk),
        compiler_params=pltpu.CompilerParams(dimension_semantics=("parallel", "parallel")),
        name=name,
    )(w, g, m, v)


def _cat_cols(g):
    return jnp.transpose(g, (1, 0, 2)).reshape(g.shape[1], N_CHIPS * g.shape[2])


def _split_cols(a):
    r, c4 = a.shape
    return jnp.transpose(a.reshape(r, N_CHIPS, c4 // N_CHIPS), (1, 0, 2))


def _local_step(x, target, w_int, late_weights, cmw, cfw, g1, b_f, b_gate, g2, gf,
                ffn_grads_ready, mix_grads_ready):
    batch, seq, d = x.shape
    t = batch * seq
    cw = d // 2
    fh = cfw.shape[1] // 2
    tc = LANES
    nct = cw // tc
    tq = min(512, seq)
    pc_w, qkv_w, gl_w = 3 * cw, 3 * ATTN_WIDTH, 2 * d
    qkv_off, gl_off, f_off = pc_w, pc_w + qkv_w, pc_w + qkv_w + gl_w
    width = f_off + F_PAD
    f_col = pc_w + qkv_w

    w_pc = w_int[:pc_w].reshape(3, nct, tc, d).transpose(1, 0, 2, 3).reshape(pc_w, d)
    w_qkv = w_int[pc_w:f_col].reshape(3, HEAD_PAIRS, LANES, d).transpose(1, 0, 2, 3).reshape(qkv_w, d)
    w_f = jnp.pad(w_int[f_col:f_col + HEADS], ((0, F_PAD - HEADS), (0, 0)))
    w_inp = jnp.concatenate([w_pc, w_qkv, w_int[f_col + HEADS:], w_f], axis=0)
    bf_pad = jnp.pad(b_f, ((0, 0), (0, F_PAD - HEADS)))

    x2d = x.reshape(t, d)
    tgt2d = target.reshape(t, d)

    h1 = _rms_fwd("norm_mix", x2d, g1)
    pc, qkv, gl, fl = _project("proj_in", h1, w_inp, [(pc_w, BF16), (qkv_w, BF16), (gl_w, BF16), (F_PAD, F32)])
    a_c = _conv_fwd("conv_mix", pc, cmw, batch, seq, tc)
    f_cum = _forget_fwd("forget_cumsum", fl, bf_pad, batch, seq)
    frow = f_cum.reshape(batch, HEAD_PAIRS, 2, seq)
    o, lse = _attn_fwd("attn_fwd", qkv, frow, batch, seq, tq)
    w_oc, w_oa, w_o, w_up, w_down = late_weights(o)
    ycat = _branch_out("branch_out", a_c, w_oc, o, w_oa)
    mg = _merge_fwd("gate_merge", ycat, gl, b_gate)
    x2 = _mm("mix_out", mg, w_o, "nn", F32, m=t, n=d, k=d, add=x2d)
    h2 = _rms_fwd("norm_ffn", x2, g2)
    tcf = min(2 * LANES, fh)
    w_up2 = _cat_cols(w_up)
    hmid, ua, ub, ffn_a, ffn_b = _ffn_up_act("ffn_up_act", h2, w_up2, cfw, batch, seq, tcf)
    x3 = _mm("ffn_down", hmid, w_down, "nn", F32, m=t, n=d, k=fh, add=x2, tk=4096)

    dx3, dx3b, loss_row, d_gf = _final_loss("final_loss", x3, gf.reshape(1, d), tgt2d)
    dw_down = _mm("dw_down", hmid, dx3b, "tn", F32, m=fh, n=d, k=t, tm=256, tk=8192)
    du_a, du_b, d_cfw = _ffn_bwd("d_ffn", dx3b, w_down, ua, ub, ffn_a, ffn_b, cfw, batch, seq, tcf)
    ws = w_up.shape[2]
    dh2 = _ffn_up_bwd("d_ffn_up", du_a, du_b, w_up2)
    dw_up = _mm("dw_up_a", h2, du_a, "tn", F32, m=d, n=fh, k=t, tm=512, tn=ws, tk=4096, o3=N_CHIPS)
    dw_up = _mm("dw_up_b", h2, du_b, "tn", F32, m=d, n=fh, k=t, tm=512, tn=ws, tk=4096, o3=N_CHIPS, out=dw_up,
                o_off=fh)
    token = ffn_grads_ready(dw_up, dw_down)
    if token is not None:
        g2 = g2 + token[0:1, 0:1]
    dx2, d_g2 = _rms_bwd("d_norm_ffn", x2, dh2, g2, dx3)
    dm = _mm("d_merge", dx2, w_o, "nt", BF16, m=t, n=d, k=d)
    dw_o = _mm("dw_o", mg, dx2, "tn", F32, m=d, n=d, k=t, tk=2048)
    dproj, dycat, d_bg = _merge_bwd("d_gate_merge", dm, ycat, gl, b_gate, width, gl_off)
    da_c, do = _branch_out_bwd("d_branch_out", dycat, w_oc, w_oa)
    dw_oc, dw_oa = _branch_out_dw("dw_branch_out", a_c, o, dycat)
    dproj, d_cmw = _conv_bwd("d_conv_mix", da_c, pc, cmw, dproj, batch, seq, tc)
    dproj, d_fkey, d_fquery = _attn_bwd("attn_bwd", qkv, do, o, lse, frow, dproj, qkv_off, batch, seq, tq)
    d_fquery = jnp.pad(jnp.transpose(d_fquery, (1, 0, 2)).reshape(t, HEADS), ((0, 0), (0, LANES - HEADS)))
    dproj, d_bf = _forget_bwd("d_forget", d_fkey.reshape(batch, HEADS, seq), d_fquery, fl, bf_pad, dproj, f_off,
                              batch, seq)
    dw_inp = _mm("dw_in", dproj, h1, "tn", F32, m=width, n=d, k=t, tm=256, tk=8192)
    d_pc = dw_inp[:pc_w].reshape(nct, 3, tc, d).transpose(1, 0, 2, 3).reshape(pc_w, d)
    d_qkv = dw_inp[qkv_off:gl_off].reshape(HEAD_PAIRS, 3, LANES, d).transpose(1, 0, 2, 3).reshape(qkv_w, d)
    dw_int = jnp.concatenate([d_pc, d_qkv, dw_inp[f_off:f_off + HEADS], dw_inp[gl_off:f_off]], axis=0)
    token = mix_grads_ready(dw_int, dw_oc, dw_oa, dw_o)
    dh1 = _mm("d_norm_mix", dproj, w_inp, "nn", BF16, m=t, n=d, k=width, tm=512, tk=8192, dep=token)
    grad_x, d_g1 = _rms_bwd("d_norm_mix_x", x2d, dh1, g1, dx2)
    smalls = (d_g1, d_g2, d_gf, d_bg, d_bf, d_cmw, d_cfw)
    return loss_row[0, 0], grad_x.reshape(batch, seq, d), smalls


def _pack_small(parts):
    flat = [p.reshape(-1) for p in parts]
    sizes = [f.shape[0] for f in flat]
    total = sum(sizes)
    padded = -(-total // (8 * LANES)) * (8 * LANES)
    vec = jnp.concatenate(flat + [jnp.zeros((padded - total,), F32)])
    offsets = [sum(sizes[:i]) for i in range(len(sizes))]
    return vec.reshape(padded // LANES, LANES), offsets


def kernel(x, norm_mix_g, w_in, b_f, b_gate, conv_mix_w, w_out_conv, w_out_attn, w_o, norm_ffn_g, w_up, conv_ffn_w, w_down, norm_f_g, loss_target, m_norm_mix_g, m_w_in, m_b_f, m_b_gate, m_conv_mix_w, m_w_out_conv, m_w_out_attn, m_w_o, m_norm_ffn_g, m_w_up, m_conv_ffn_w, m_w_down, m_norm_f_g, v_norm_mix_g, v_w_in, v_b_f, v_b_gate, v_conv_mix_w, v_w_out_conv, v_w_out_attn, v_w_o, v_norm_ffn_g, v_w_up, v_conv_ffn_w, v_w_down, v_norm_f_g):
    d = x.shape[-1]
    chip = 2 * lax.axis_index("x") + lax.axis_index("y")
    xi, yi = lax.axis_index("x"), lax.axis_index("y")
    peers = [2 * px + py for px, py in ((1 - xi, yi), (xi, 1 - yi), (1 - xi, 1 - yi))]
    place = jnp.stack([chip, lax.axis_index("c"), *peers]).astype(jnp.int32)

    t_in, t_m_in, t_v_in = (jnp.transpose(w[0]) for w in (w_in, m_w_in, v_w_in))

    def row_shards(a):
        return a.reshape(N_CHIPS, a.shape[0] // N_CHIPS, a.shape[1])

    def stacked(a):
        return a.reshape(N_CHIPS * a.shape[1], a.shape[2])

    a_in, a_cmw, a_cfw = _gather_weights([t_in.astype(BF16)], (1,), [conv_mix_w[0], conv_ffn_w[0]])
    late = [w[0].astype(BF16) for w in (w_out_conv, w_out_attn, w_o, w_up, w_down)]
    late_handles, late_token = _chips_start("gather_late_start", "gather", late)

    def late_weights(after):
        lands = _chips_wait("gather_late_wait", "gather", late_handles, after)
        a_oc, a_oa, a_o, a_up, a_down = (
            lax.dynamic_update_index_in_dim(buf, own, chip, 0) for buf, own in zip(lands, late))
        return _cat_cols(a_oc), _cat_cols(a_oa), stacked(a_o), a_up, stacked(a_down)

    pending = []

    def reduce_start(tag, names, grads, axes):
        got = _exchange_sibling("exchange_sibling_" + tag, grads, axes)
        sums = [_pair_sum("pair_sum_" + nm, place, g, r, ax) for nm, g, r, ax in zip(names, grads, got, axes)]
        handles, token = _chips_start("exchange_chips_start_" + tag, "reduce", sums)
        pending.append((tag, names, grads, axes, got, handles))
        return token

    def ffn_grads_ready(dw_up, dw_down):
        return reduce_start("ffn", ("w_up", "w_down"), [dw_up, row_shards(dw_down)], (0, 0))

    def mix_grads_ready(dw_int, dw_oc, dw_oa, dw_o):
        return reduce_start("mix", ("w_in", "w_out_conv", "w_out_attn", "w_o"),
                            [row_shards(dw_int), _split_cols(dw_oc), _split_cols(dw_oa), row_shards(dw_o)],
                            (1, 0, 0, 0))

    loss_local, grad_x, smalls = _local_step(
        x, loss_target, stacked(a_in), late_weights, _cat_cols(a_cmw),
        _cat_cols(a_cfw), norm_mix_g + late_token[0:1, 0:1], b_f, b_gate, norm_ffn_g, norm_f_g,
        ffn_grads_ready, mix_grads_ready)

    reduced = {}
    for tag, names, grads, axes, got, handles in pending:
        arrivals = _chips_wait("exchange_chips_wait_" + tag, "reduce", handles, grad_x)
        halves = [_chip_sum("chip_sum_" + nm, place, g, r, arr, ax)
                  for nm, g, r, arr, ax in zip(names, grads, got, arrivals, axes)]
        reduced.update(zip(names, _share_sibling("share_sibling_" + tag, halves, axes)))
    g_in, g_oc, g_oa, g_o, g_up, g_down = (
        reduced[nm] for nm in ("w_in", "w_out_conv", "w_out_attn", "w_o", "w_up", "w_down"))

    smalls = (*smalls, loss_local.reshape(1, 1))
    packed, offs = _pack_small(smalls)
    total = _device_sum("device_sum", _gather_small(packed)).reshape(-1)
    shapes = [s.shape for s in smalls]
    d_g1, d_g2, d_gf, d_bg, d_bf, d_cmw, d_cfw, loss = [
        total[o:o + math.prod(sh)].reshape(sh) for o, sh in zip(offs, shapes)]
    loss = loss[0, 0]
    d_bf = d_bf[:, :HEADS]
    cw_s, cf_s = conv_mix_w.shape[2], conv_ffn_w.shape[2]
    d_cmw = lax.dynamic_slice(d_cmw, (0, chip * cw_s), (3, cw_s))
    d_cfw = lax.dynamic_slice(d_cfw, (0, chip * cf_s), (3, cf_s))

    order = [
        ("norm_mix_g", norm_mix_g[0:1], d_g1, m_norm_mix_g, v_norm_mix_g),
        ("w_in", t_in, g_in, t_m_in, t_v_in),
        ("b_f", b_f, d_bf, m_b_f, v_b_f),
        ("b_gate", b_gate, d_bg, m_b_gate, v_b_gate),
        ("conv_mix_w", conv_mix_w[0], d_cmw, m_conv_mix_w[0], v_conv_mix_w[0]),
        ("w_out_conv", w_out_conv[0], g_oc, m_w_out_conv[0], v_w_out_conv[0]),
        ("w_out_attn", w_out_attn[0], g_oa, m_w_out_attn[0], v_w_out_attn[0]),
        ("w_o", w_o[0], g_o, m_w_o[0], v_w_o[0]),
        ("norm_ffn_g", norm_ffn_g, d_g2, m_norm_ffn_g, v_norm_ffn_g),
        ("w_up", w_up[0], g_up, m_w_up[0], v_w_up[0]),
        ("conv_ffn_w", conv_ffn_w[0], d_cfw, m_conv_ffn_w[0], v_conv_ffn_w[0]),
        ("w_down", w_down[0], g_down, m_w_down[0], v_w_down[0]),
        ("norm_f_g", norm_f_g.reshape(1, d), d_gf, m_norm_f_g.reshape(1, d), v_norm_f_g.reshape(1, d)),
    ]
    out_shapes = [norm_mix_g.shape, w_in.shape, b_f.shape, b_gate.shape, conv_mix_w.shape, w_out_conv.shape,
                  w_out_attn.shape, w_o.shape, norm_ffn_g.shape, w_up.shape, conv_ffn_w.shape, w_down.shape,
                  norm_f_g.shape]
    g_out, d_out, m_out, v_out = [], [], [], []
    for (nm, w, g, m, v), sh in zip(order, out_shapes):
        g = g.reshape(w.shape)
        delta, new_m, new_v = _adamw("adamw_" + nm, w, g, m.reshape(w.shape), v.reshape(w.shape))
        for dst, val in ((g_out, g), (d_out, delta), (m_out, new_m), (v_out, new_v)):
            dst.append((jnp.transpose(val) if nm == "w_in" else val).reshape(sh))
    return (loss, grad_x, *g_out, *d_out, *m_out, *v_out)
```

```python
import math

import jax
import jax.numpy as jnp
from jax import lax
from jax.experimental import pallas as pl
from jax.experimental.pallas import tpu as pltpu

F32 = jnp.float32
BF16 = jnp.bfloat16
MESH = pl.DeviceIdType.MESH

EPS = 1e-6
HEADS = 8
HEAD_DIM = 64
ATTN_WIDTH = HEADS * HEAD_DIM
HEAD_PAIRS = HEADS // 2
LANES = 128
F_PAD = 2 * LANES
NEG_BIG = -1e30
N_CHIPS = 4
N_DEV = 8

ADAM_LR = 0.001
ADAM_B1 = 0.9
ADAM_B2 = 0.999
ADAM_EPS = 1e-08
ADAM_WD = 0.01
ADAM_STEP = 10

_DIMS = {
    "nn": (((1,), (0,)), ((), ())),
    "nt": (((1,), (1,)), ((), ())),
    "tn": (((0,), (0,)), ((), ())),
}


def _tile(n, target, mult, also=()):
    best = None
    for t in range(mult, n + 1, mult):
        if n % t == 0 and t <= target and all(o % t == 0 for o in also):
            best = t
    if best is None:
        assert all(o == 0 for o in also), (n, target, mult, also)
        return n
    return best


def _sds(shape, dtype):
    return jax.ShapeDtypeStruct(shape, dtype)


def _mm(name, a, b, mode, out_dtype, *, m, n, k, out=None, o_off=0, o3=None, add=None, dep=None,
        tm=1024, tn=2048, tk=2048):
    if mode == "nn":
        tm, tk, tn = _tile(m, tm, 16), _tile(k, tk, LANES), _tile(n, tn, LANES, (o_off,))
        a_spec = pl.BlockSpec((tm, tk), lambda i, j, kk: (i, kk))
        b_spec = pl.BlockSpec((tk, tn), lambda i, j, kk: (kk, j))
    elif mode == "nt":
        tm, tk, tn = _tile(m, tm, 16), _tile(k, tk, LANES), _tile(n, tn, LANES, (o_off,))
        a_spec = pl.BlockSpec((tm, tk), lambda i, j, kk: (i, kk))
        b_spec = pl.BlockSpec((tn, tk), lambda i, j, kk: (j, kk))
    else:
        tm, tk, tn = _tile(m, tm, LANES), _tile(k, tk, 16), _tile(n, tn, LANES, (o_off,))
        a_spec = pl.BlockSpec((tk, tm), lambda i, j, kk: (kk, i))
        b_spec = pl.BlockSpec((tk, tn), lambda i, j, kk: (kk, j))
    assert m % tm == 0 and n % tn == 0 and k % tk == 0, (name, tm, tn, tk)
    nk = k // tk
    if o3 is not None:
        o_spec = pl.BlockSpec((None, tm, tn), lambda i, j, kk: (o_off // tn + j, i, 0))
        out_sds = _sds((o3, m, tn), out_dtype)
    else:
        assert out is None and o_off == 0, name
        o_spec = pl.BlockSpec((tm, tn), lambda i, j, kk: (i, j))
        out_sds = _sds((m, n), out_dtype)
    use_acc = nk > 1 and out_dtype != F32
    dims = _DIMS[mode]
    has_add, has_out = add is not None, out is not None

    def body(*refs):
        a_ref, b_ref = refs[0], refs[1]
        pos = 2
        add_ref = None
        if has_add:
            add_ref = refs[pos]
            pos += 1
        if has_out:
            pos += 1
        if dep is not None:
            pos += 1
        o_ref = refs[pos]
        acc_ref = refs[pos + 1] if use_acc else None
        part = lax.dot_general(a_ref[...].astype(BF16), b_ref[...].astype(BF16), dims,
                               preferred_element_type=F32)
        if nk == 1:
            if has_add:
                part = part + add_ref[...]
            o_ref[...] = part.astype(o_ref.dtype)
            return
        kk = pl.program_id(2)
        tgt = acc_ref if use_acc else o_ref

        @pl.when(kk == 0)
        def _():
            tgt[...] = part + add_ref[...] if has_add else part

        @pl.when(kk > 0)
        def _():
            tgt[...] += part

        if use_acc:
            @pl.when(kk == nk - 1)
            def _():
                o_ref[...] = acc_ref[...].astype(o_ref.dtype)

    operands, in_specs = [a, b], [a_spec, b_spec]
    if has_add:
        operands.append(add)
        in_specs.append(pl.BlockSpec((tm, tn), lambda i, j, kk: (i, j)))
    aliases = {}
    if has_out:
        aliases = {len(operands): 0}
        operands.append(out)
        in_specs.append(pl.BlockSpec(memory_space=pl.ANY))
    if dep is not None:
        operands.append(dep)
        in_specs.append(pl.BlockSpec(memory_space=pl.ANY))
    return pl.pallas_call(
        body,
        out_shape=out_sds,
        grid=(m // tm, n // tn, nk),
        in_specs=in_specs,
        out_specs=o_spec,
        scratch_shapes=[pltpu.VMEM((tm, tn), F32)] if use_acc else [],
        input_output_aliases=aliases,
        compiler_params=pltpu.CompilerParams(dimension_semantics=("parallel", "parallel", "arbitrary")),
        name=name,
    )(*operands)


def _project(name, h, w_t, groups):
    t, d = h.shape
    tm = _tile(t, 512, 16)
    offs = [sum(n for n, _ in groups[:i]) for i in range(len(groups))]

    def body(h_ref, w_ref, *o_refs):
        hv = h_ref[...]
        for (n, _), off, o_ref in zip(groups, offs, o_refs):
            o_ref[...] = _dot(hv, w_ref[off:off + n, :], "nt").astype(o_ref.dtype)

    return pl.pallas_call(
        body,
        out_shape=tuple(_sds((t, n), dt) for n, dt in groups),
        grid=(t // tm,),
        in_specs=[pl.BlockSpec((tm, d), lambda i: (i, 0)), pl.BlockSpec(w_t.shape, lambda i: (0, 0))],
        out_specs=tuple(pl.BlockSpec((tm, n), lambda i: (i, 0)) for n, _ in groups),
        compiler_params=pltpu.CompilerParams(dimension_semantics=("parallel",)),
        name=name,
    )(h, w_t)


def _branch_out(name, a1, w1, a2, w2):
    t = a1.shape[0]
    n1, n2 = w1.shape[1], w2.shape[1]
    tm = _tile(t, 1024, 16)

    def body(a1_ref, w1_ref, a2_ref, w2_ref, o_ref):
        o_ref[:, 0:n1] = _dot(a1_ref[...], w1_ref[...], "nn").astype(o_ref.dtype)
        o_ref[:, n1:n1 + n2] = _dot(a2_ref[...], w2_ref[...], "nn").astype(o_ref.dtype)

    whole = lambda w: pl.BlockSpec(w.shape, lambda i: (0, 0))
    rows = lambda a: pl.BlockSpec((tm, a.shape[1]), lambda i: (i, 0))
    return pl.pallas_call(
        body,
        out_shape=_sds((t, n1 + n2), BF16),
        grid=(t // tm,),
        in_specs=[rows(a1), whole(w1), rows(a2), whole(w2)],
        out_specs=pl.BlockSpec((tm, n1 + n2), lambda i: (i, 0)),
        compiler_params=pltpu.CompilerParams(dimension_semantics=("parallel",)),
        name=name,
    )(a1, w1, a2, w2)


def _branch_out_bwd(name, dy, w1, w2):
    t, d2 = dy.shape
    d = d2 // 2
    tm = _tile(t, 1024, 16)

    def body(dy_ref, w1_ref, w2_ref, o1_ref, o2_ref):
        o1_ref[...] = _dot(dy_ref[:, 0:d], w1_ref[...], "nt").astype(o1_ref.dtype)
        o2_ref[...] = _dot(dy_ref[:, d:d2], w2_ref[...], "nt").astype(o2_ref.dtype)

    whole = lambda w: pl.BlockSpec(w.shape, lambda i: (0, 0))
    return pl.pallas_call(
        body,
        out_shape=(_sds((t, w1.shape[0]), BF16), _sds((t, w2.shape[0]), BF16)),
        grid=(t // tm,),
        in_specs=[pl.BlockSpec((tm, d2), lambda i: (i, 0)), whole(w1), whole(w2)],
        out_specs=(pl.BlockSpec((tm, w1.shape[0]), lambda i: (i, 0)), pl.BlockSpec((tm, w2.shape[0]), lambda i: (i, 0))),
        compiler_params=pltpu.CompilerParams(dimension_semantics=("parallel",)),
        name=name,
    )(dy, w1, w2)


def _ffn_up_bwd(name, du_a, du_b, w_up):
    t, fh = du_a.shape
    d = w_up.shape[0]
    tm = _tile(t, 512, 16)

    def body(da_ref, db_ref, wa_ref, wb_ref, o_ref):
        acc = _dot(da_ref[...], wa_ref[...], "nt") + _dot(db_ref[...], wb_ref[...], "nt")
        o_ref[...] = acc.astype(o_ref.dtype)

    rows = pl.BlockSpec((tm, fh), lambda i: (i, 0))
    return pl.pallas_call(
        body,
        out_shape=_sds((t, d), BF16),
        grid=(t // tm,),
        in_specs=[rows, rows, pl.BlockSpec((d, fh), lambda i: (0, 0)), pl.BlockSpec((d, fh), lambda i: (0, 1))],
        out_specs=pl.BlockSpec((tm, d), lambda i: (i, 0)),
        compiler_params=pltpu.CompilerParams(dimension_semantics=("parallel",)),
        name=name,
    )(du_a, du_b, w_up, w_up)


def _branch_out_dw(name, a1, a2, dy):
    t, d2 = dy.shape
    d = d2 // 2
    tk = _tile(t, 2048, 16)

    def body(a1_ref, a2_ref, dy_ref, o1_ref, o2_ref):
        @pl.when(pl.program_id(0) == 0)
        def _():
            o1_ref[...] = jnp.zeros_like(o1_ref)
            o2_ref[...] = jnp.zeros_like(o2_ref)

        o1_ref[...] += _dot(a1_ref[...], dy_ref[:, 0:d], "tn")
        o2_ref[...] += _dot(a2_ref[...], dy_ref[:, d:d2], "tn")

    rows = lambda a: pl.BlockSpec((tk, a.shape[1]), lambda k: (k, 0))
    acc = lambda a: pl.BlockSpec((a.shape[1], d), lambda k: (0, 0))
    return pl.pallas_call(
        body,
        out_shape=(_sds((a1.shape[1], d), F32), _sds((a2.shape[1], d), F32)),
        grid=(t // tk,),
        in_specs=[rows(a1), rows(a2), rows(dy)],
        out_specs=(acc(a1), acc(a2)),
        compiler_params=pltpu.CompilerParams(dimension_semantics=("arbitrary",)),
        name=name,
    )(a1, a2, dy)


def _rms_fwd(name, x, g):
    t, d = x.shape
    tm = _tile(t, 512, 16)

    def body(x_ref, g_ref, o_ref):
        xv = x_ref[...]
        r = lax.rsqrt(jnp.mean(xv * xv, axis=-1, keepdims=True) + EPS)
        o_ref[...] = ((xv * r) * g_ref[...]).astype(o_ref.dtype)

    return pl.pallas_call(
        body,
        out_shape=_sds((t, d), BF16),
        grid=(t // tm,),
        in_specs=[pl.BlockSpec((tm, d), lambda i: (i, 0)), pl.BlockSpec((1, d), lambda i: (0, 0))],
        out_specs=pl.BlockSpec((tm, d), lambda i: (i, 0)),
        compiler_params=pltpu.CompilerParams(dimension_semantics=("parallel",)),
        name=name,
    )(x, g)


def _rms_bwd(name, x, dh, g, res, bf16_copy):
    t, d = x.shape
    tm = _tile(t, 512, 16)

    def body(x_ref, dh_ref, g_ref, res_ref, dx_ref, *rest):
        dg_ref = rest[-1]
        xv = x_ref[...]
        r = lax.rsqrt(jnp.mean(xv * xv, axis=-1, keepdims=True) + EPS)
        xh = xv * r
        dhv = dh_ref[...].astype(F32)
        dxh = dhv * g_ref[...]
        dx = res_ref[...] + r * (dxh - xh * jnp.mean(dxh * xh, axis=-1, keepdims=True))
        dx_ref[...] = dx
        if bf16_copy:
            rest[0][...] = dx.astype(BF16)

        @pl.when(pl.program_id(0) == 0)
        def _():
            dg_ref[...] = jnp.zeros_like(dg_ref)

        dg_ref[...] += jnp.sum(dhv * xh, axis=0, keepdims=True)

    row = pl.BlockSpec((tm, d), lambda i: (i, 0))
    vec = pl.BlockSpec((1, d), lambda i: (0, 0))
    copies = ((_sds((t, d), BF16),), (row,)) if bf16_copy else ((), ())
    return pl.pallas_call(
        body,
        out_shape=(_sds((t, d), F32), *copies[0], _sds((1, d), F32)),
        grid=(t // tm,),
        in_specs=[row, row, vec, row],
        out_specs=(row, *copies[1], vec),
        compiler_params=pltpu.CompilerParams(dimension_semantics=("arbitrary",)),
        name=name,
    )(x, dh, g, res)


def _final_loss(name, x, g, target):
    t, d = x.shape
    tm = _tile(t, 512, 16)

    def body(x_ref, g_ref, t_ref, dx_ref, dxb_ref, loss_ref, dg_ref):
        xv = x_ref[...]
        gv = g_ref[...]
        r = lax.rsqrt(jnp.mean(xv * xv, axis=-1, keepdims=True) + EPS)
        xh = xv * r
        err = xh * gv - t_ref[...]
        dy = err * (1.0 / d)
        dxh = dy * gv
        dx = r * (dxh - xh * jnp.mean(dxh * xh, axis=-1, keepdims=True))
        dx_ref[...] = dx
        dxb_ref[...] = dx.astype(dxb_ref.dtype)
        per_row = jnp.sum(err * err, axis=-1, keepdims=True) * (0.5 / d)

        @pl.when(pl.program_id(0) == 0)
        def _():
            dg_ref[...] = jnp.zeros_like(dg_ref)
            loss_ref[...] = jnp.zeros_like(loss_ref)

        dg_ref[...] += jnp.sum(dy * xh, axis=0, keepdims=True)
        loss_ref[...] += jnp.sum(per_row, axis=0, keepdims=True)

    row = pl.BlockSpec((tm, d), lambda i: (i, 0))
    vec = pl.BlockSpec((1, d), lambda i: (0, 0))
    return pl.pallas_call(
        body,
        out_shape=(_sds((t, d), F32), _sds((t, d), BF16), _sds((1, LANES), F32), _sds((1, d), F32)),
        grid=(t // tm,),
        in_specs=[row, vec, row],
        out_specs=(row, row, pl.BlockSpec((1, LANES), lambda i: (0, 0)), vec),
        compiler_params=pltpu.CompilerParams(dimension_semantics=("arbitrary",)),
        name=name,
    )(x, g, target)


def _shift_down(z, k):
    row = lax.broadcasted_iota(jnp.int32, z.shape, 0)
    return jnp.where(row >= k, pltpu.roll(z, k, axis=0), 0.0)


def _shift_up(z, k):
    s = z.shape[0]
    row = lax.broadcasted_iota(jnp.int32, z.shape, 0)
    return jnp.where(row < s - k, pltpu.roll(z, s - k, axis=0), 0.0)


def _conv3(z, w):
    return (w[2:3] * z + w[0:1] * _shift_down(z, 2)) + w[1:2] * _shift_down(z, 1)


def _conv3_t(dz, w):
    return (w[2:3] * dz + w[0:1] * _shift_up(dz, 2)) + w[1:2] * _shift_up(dz, 1)


def _conv_fwd(name, pc, w, batch, seq, tc):
    cw = w.shape[1]
    nct = cw // tc

    def body(pc_ref, w_ref, o_ref):
        cb = pc_ref[:, 0:tc].astype(F32)
        z = pc_ref[:, tc:2 * tc].astype(F32) * pc_ref[:, 2 * tc:3 * tc].astype(F32)
        o_ref[...] = (cb * _conv3(z, w_ref[...])).astype(o_ref.dtype)

    return pl.pallas_call(
        body,
        out_shape=_sds((batch * seq, cw), BF16),
        grid=(batch, nct),
        in_specs=[pl.BlockSpec((seq, 3 * tc), lambda b, j: (b, j)), pl.BlockSpec((3, tc), lambda b, j: (0, j))],
        out_specs=pl.BlockSpec((seq, tc), lambda b, j: (b, j)),
        compiler_params=pltpu.CompilerParams(dimension_semantics=("parallel", "parallel")),
        name=name,
    )(pc, w)


def _conv_bwd(name, da, pc, w, dproj, batch, seq, tc):
    cw = w.shape[1]
    nct = cw // tc

    def body(da_ref, pc_ref, w_ref, _, dpc_ref, dw_ref):
        wv = w_ref[...]
        cb = pc_ref[:, 0:tc].astype(F32)
        cc = pc_ref[:, tc:2 * tc].astype(F32)
        cin = pc_ref[:, 2 * tc:3 * tc].astype(F32)
        z = cc * cin
        dav = da_ref[...].astype(F32)
        du = dav * cb
        dz = _conv3_t(du, wv)
        dpc_ref[:, 0:tc] = (dav * _conv3(z, wv)).astype(dpc_ref.dtype)
        dpc_ref[:, tc:2 * tc] = (dz * cin).astype(dpc_ref.dtype)
        dpc_ref[:, 2 * tc:3 * tc] = (dz * cc).astype(dpc_ref.dtype)

        @pl.when(pl.program_id(1) == 0)
        def _():
            dw_ref[...] = jnp.zeros_like(dw_ref)

        dw_ref[0:1, :] += jnp.sum(du * _shift_down(z, 2), axis=0, keepdims=True)
        dw_ref[1:2, :] += jnp.sum(du * _shift_down(z, 1), axis=0, keepdims=True)
        dw_ref[2:3, :] += jnp.sum(du * z, axis=0, keepdims=True)

    return pl.pallas_call(
        body,
        out_shape=(_sds(dproj.shape, dproj.dtype), _sds((3, cw), F32)),
        grid=(nct, batch),
        in_specs=[
            pl.BlockSpec((seq, tc), lambda j, b: (b, j)),
            pl.BlockSpec((seq, 3 * tc), lambda j, b: (b, j)),
            pl.BlockSpec((3, tc), lambda j, b: (0, j)),
            pl.BlockSpec(memory_space=pl.ANY),
        ],
        out_specs=(pl.BlockSpec((seq, 3 * tc), lambda j, b: (b, j)), pl.BlockSpec((3, tc), lambda j, b: (0, j))),
        input_output_aliases={3: 0},
        compiler_params=pltpu.CompilerParams(dimension_semantics=("parallel", "arbitrary")),
        name=name,
    )(da, pc, w, dproj)


def _ffn_up_act(name, h2, w_up, w, batch, seq, tc):
    d = h2.shape[1]
    fh = w.shape[1] // 2
    nf = fh // tc

    def body(h_ref, ma_ref, mb_ref, wa_ref, wb_ref, o_ref, ua_ref, ub_ref, a_ref, b_ref):
        hv = h_ref[...]
        ua = _dot(hv, ma_ref[...], "nn")
        ub = _dot(hv, mb_ref[...], "nn")
        ua_ref[...] = ua.astype(ua_ref.dtype)
        ub_ref[...] = ub.astype(ub_ref.dtype)
        a = _conv3(ua, wa_ref[...])
        b = _conv3(ub, wb_ref[...])
        a_ref[...] = a.astype(a_ref.dtype)
        b_ref[...] = b.astype(b_ref.dtype)
        o_ref[...] = (a * jax.nn.sigmoid(a) * b).astype(o_ref.dtype)

    act = pl.BlockSpec((seq, tc), lambda b, j: (b, j))
    shape = _sds((batch * seq, fh), BF16)
    return pl.pallas_call(
        body,
        out_shape=(shape,) * 5,
        grid=(batch, nf),
        in_specs=[
            pl.BlockSpec((seq, d), lambda b, j: (b, 0)),
            pl.BlockSpec((d, tc), lambda b, j: (0, j)),
            pl.BlockSpec((d, tc), lambda b, j: (0, nf + j)),
            pl.BlockSpec((3, tc), lambda b, j: (0, j)),
            pl.BlockSpec((3, tc), lambda b, j: (0, nf + j)),
        ],
        out_specs=(act,) * 5,
        compiler_params=pltpu.CompilerParams(dimension_semantics=("parallel", "parallel")),
        name=name,
    )(h2, w_up, w_up, w, w)


def _ffn_bwd(name, dx, w_down, ua, ub, av, bv, w, batch, seq, tc):
    d = dx.shape[1]
    fh = w.shape[1] // 2
    nf = fh // tc

    rb = _tile(seq, 128, 8)
    halo = 8

    def body(dx_ref, md_ref, ua_ref, ub_ref, a_ref, b_ref, wa_ref, wb_ref, dua_ref, dub_ref, dw_ref,
             dh_scr, da_scr, db_scr):
        j = pl.program_id(1)
        dh_scr[...] = _dot(dx_ref[...].astype(BF16), md_ref[...], "nt")
        da_scr[seq:seq + halo, :] = jnp.zeros((halo, tc), F32)
        db_scr[seq:seq + halo, :] = jnp.zeros((halo, tc), F32)

        def silu_bwd(r, carry):
            rows = pl.ds(pl.multiple_of(r * rb, rb), rb)
            a, b, dhv = a_ref[rows, :].astype(F32), b_ref[rows, :].astype(F32), dh_scr[rows, :]
            sg = jax.nn.sigmoid(a)
            da_scr[rows, :] = dhv * b * (sg * (1.0 + a * (1.0 - sg)))
            db_scr[rows, :] = dhv * (a * sg)
            return carry

        lax.fori_loop(0, seq // rb, silu_bwd, 0)
        wa, wb = wa_ref[...], wb_ref[...]

        def conv_bwd(r, sums):
            r0 = pl.multiple_of(r * rb, rb)
            rows = pl.ds(r0, rb)
            out = []
            for d_scr, u_ref, wv, du_ref, acc in ((da_scr, ua_ref, wa, dua_ref, sums[0:3]),
                                                  (db_scr, ub_ref, wb, dub_ref, sums[3:6])):
                x = d_scr[pl.ds(r0, rb + halo), :]
                dv = x[0:rb]
                up1 = pltpu.roll(x, rb + halo - 1, axis=0)[0:rb]
                up2 = pltpu.roll(x, rb + halo - 2, axis=0)[0:rb]
                du_ref[rows, :] = ((wv[2:3] * dv + wv[0:1] * up2) + wv[1:2] * up1).astype(du_ref.dtype)
                uv = u_ref[rows, :].astype(F32)
                out += [acc[0] + jnp.sum(up2 * uv, axis=0, keepdims=True),
                        acc[1] + jnp.sum(up1 * uv, axis=0, keepdims=True),
                        acc[2] + jnp.sum(dv * uv, axis=0, keepdims=True)]
            return tuple(out)

        sums = lax.fori_loop(0, seq // rb, conv_bwd, (jnp.zeros((1, tc), F32),) * 6)

        @pl.when((pl.program_id(0) == 0) & (j == 0))
        def _():
            dw_ref[...] = jnp.zeros_like(dw_ref)

        for half, off in enumerate((0, fh)):
            cols = pl.ds(pl.multiple_of(off + j * tc, LANES), tc)
            for k in range(3):
                dw_ref[k:k + 1, cols] += sums[3 * half + k]

    act = pl.BlockSpec((seq, tc), lambda b, j: (b, j))
    shape = _sds((batch * seq, fh), BF16)
    return pl.pallas_call(
        body,
        out_shape=(shape, shape, _sds((3, 2 * fh), F32)),
        grid=(batch, nf),
        in_specs=[
            pl.BlockSpec((seq, d), lambda b, j: (b, 0)),
            pl.BlockSpec((tc, d), lambda b, j: (j, 0)),
            act,
            act,
            act,
            act,
            pl.BlockSpec((3, tc), lambda b, j: (0, j)),
            pl.BlockSpec((3, tc), lambda b, j: (0, nf + j)),
        ],
        out_specs=(act, act, pl.BlockSpec((3, 2 * fh), lambda b, j: (0, 0))),
        scratch_shapes=[pltpu.VMEM((seq, tc), F32), pltpu.VMEM((seq + halo, tc), F32),
                        pltpu.VMEM((seq + halo, tc), F32)],
        compiler_params=pltpu.CompilerParams(dimension_semantics=("arbitrary", "arbitrary")),
        name=name,
    )(dx, w_down, ua, ub, av, bv, w, w)


def _merge_fwd(name, ycat, gl, bg):
    t, d2 = ycat.shape
    d = d2 // 2
    tm = _tile(t, 1024, 16)

    def body(y_ref, gl_ref, bg_ref, o_ref):
        g = jax.nn.sigmoid(gl_ref[...].astype(F32) + bg_ref[...])
        prod = g * y_ref[...].astype(F32)
        o_ref[...] = (prod[:, 0:d] + prod[:, d:d2]).astype(o_ref.dtype)

    row = pl.BlockSpec((tm, d2), lambda i: (i, 0))
    return pl.pallas_call(
        body,
        out_shape=_sds((t, d), BF16),
        grid=(t // tm,),
        in_specs=[row, row, pl.BlockSpec((1, d2), lambda i: (0, 0))],
        out_specs=pl.BlockSpec((tm, d), lambda i: (i, 0)),
        compiler_params=pltpu.CompilerParams(dimension_semantics=("parallel",)),
        name=name,
    )(ycat, gl, bg)


def _merge_bwd(name, dm, ycat, gl, bg, width, gl_off):
    t, d2 = ycat.shape
    d = d2 // 2
    tm = _tile(t, 1024, 16)
    wb = math.gcd(gl_off, d)
    nw = d // wb

    def body(dm_ref, y_ref, gl_ref, bg_ref, dgl_ref, dy_ref, dbg_ref):
        g = jax.nn.sigmoid(gl_ref[...].astype(F32) + bg_ref[...])
        dmv = dm_ref[...].astype(F32)
        dgl = dmv * y_ref[...].astype(F32) * (g * (1.0 - g))
        dgl_ref[...] = dgl.astype(dgl_ref.dtype)
        dy_ref[...] = (dmv * g).astype(dy_ref.dtype)

        @pl.when(pl.program_id(2) == 0)
        def _():
            dbg_ref[...] = jnp.zeros_like(dbg_ref)

        dbg_ref[...] += jnp.sum(dgl, axis=0, keepdims=True)

    half = pl.BlockSpec((tm, wb), lambda h, j, i: (i, h * nw + j))
    vec = pl.BlockSpec((1, wb), lambda h, j, i: (0, h * nw + j))
    return pl.pallas_call(
        body,
        out_shape=(_sds((t, width), BF16), _sds((t, d2), BF16), _sds((1, d2), F32)),
        grid=(2, nw, t // tm),
        in_specs=[pl.BlockSpec((tm, wb), lambda h, j, i: (i, j)), half, half, vec],
        out_specs=(pl.BlockSpec((tm, wb), lambda h, j, i: (i, gl_off // wb + h * nw + j)), half, vec),
        compiler_params=pltpu.CompilerParams(dimension_semantics=("parallel", "parallel", "arbitrary")),
        name=name,
    )(dm, ycat, gl, bg)


def _log_sigmoid(z):
    return jnp.minimum(z, 0.0) - jnp.log1p(jnp.exp(-jnp.abs(z)))


def _forget_fwd(name, fl, bf, batch, seq):
    def body(fl_ref, bf_ref, o_ref):
        lf = _log_sigmoid(fl_ref[:, 0:LANES] + bf_ref[:, 0:LANES])
        acc = lf.T[0:HEADS, :]
        lane = lax.broadcasted_iota(jnp.int32, acc.shape, 1)
        k = 1
        while k < seq:
            acc = acc + jnp.where(lane >= k, pltpu.roll(acc, k, axis=1), 0.0)
            k *= 2
        o_ref[...] = acc

    return pl.pallas_call(
        body,
        out_shape=_sds((batch, HEADS, seq), F32),
        grid=(batch,),
        in_specs=[pl.BlockSpec((seq, F_PAD), lambda b: (b, 0)), pl.BlockSpec((1, F_PAD), lambda b: (0, 0))],
        out_specs=pl.BlockSpec((None, HEADS, seq), lambda b: (b, 0, 0)),
        compiler_params=pltpu.CompilerParams(dimension_semantics=("parallel",)),
        name=name,
    )(fl, bf)


def _forget_bwd(name, d_key, d_query, fl, bf, dproj, f_off, batch, seq):
    nfb = F_PAD // LANES

    def body(dk_ref, dq_ref, fl_ref, bf_ref, _, df_ref, dbf_ref):
        jj = pl.program_id(1)
        key_t = jnp.concatenate([dk_ref[...], jnp.zeros((LANES - HEADS, seq), F32)], axis=0).T
        acc = dq_ref[...] - key_t
        row = lax.broadcasted_iota(jnp.int32, acc.shape, 0)
        k = 1
        while k < seq:
            acc = acc + jnp.where(row < seq - k, pltpu.roll(acc, seq - k, axis=0), 0.0)
            k *= 2
        z = fl_ref[:, 0:LANES] + bf_ref[:, 0:LANES]
        col = lax.broadcasted_iota(jnp.int32, acc.shape, 1)
        df = jnp.where(col < HEADS, acc * jax.nn.sigmoid(-z), 0.0)
        df = jnp.where(jj == 0, df, 0.0)
        df_ref[...] = df.astype(df_ref.dtype)

        @pl.when((pl.program_id(0) == 0) & (jj == 0))
        def _():
            dbf_ref[...] = jnp.zeros_like(dbf_ref)

        dbf_ref[...] += jnp.sum(df, axis=0, keepdims=True)

    return pl.pallas_call(
        body,
        out_shape=(_sds(dproj.shape, dproj.dtype), _sds((1, LANES), F32)),
        grid=(batch, nfb),
        in_specs=[
            pl.BlockSpec((None, HEADS, seq), lambda b, j: (b, 0, 0)),
            pl.BlockSpec((seq, LANES), lambda b, j: (b, 0)),
            pl.BlockSpec((seq, F_PAD), lambda b, j: (b, 0)),
            pl.BlockSpec((1, F_PAD), lambda b, j: (0, 0)),
            pl.BlockSpec(memory_space=pl.ANY),
        ],
        out_specs=(pl.BlockSpec((seq, LANES), lambda b, j: (b, f_off // LANES + j)),
                   pl.BlockSpec((1, LANES), lambda b, j: (0, 0))),
        input_output_aliases={4: 0},
        compiler_params=pltpu.CompilerParams(dimension_semantics=("arbitrary", "arbitrary")),
        name=name,
    )(d_key, d_query, fl, bf, dproj)


def _dot(a, b, mode):
    return lax.dot_general(a, b, _DIMS[mode], preferred_element_type=F32)


def _attn_fwd(name, qkv, frow, batch, seq, tq):
    nq = seq // tq
    scale = 1.0 / math.sqrt(HEAD_DIM)

    def body(q_ref, k_ref, v_ref, f_ref, o_ref, lse_ref):
        i = pl.program_id(2)
        lane = lax.broadcasted_iota(jnp.int32, (1, LANES), 1)
        lo = lane < HEAD_DIM
        qs = q_ref[...] * scale
        qh = (jnp.where(lo, qs, 0.0).astype(BF16), jnp.where(lo, 0.0, qs).astype(BF16))
        row = lax.broadcasted_iota(jnp.int32, (tq, tq), 0)
        col = lax.broadcasted_iota(jnp.int32, (tq, tq), 1)

        def step(j, carry, diag):
            m0, l0, m1, l1, acc = carry
            start = pl.multiple_of(j * tq, tq)
            kj = k_ref[pl.ds(start, tq), :]
            vj = v_ref[pl.ds(start, tq), :]
            ms, ls, pvs, alphas = [], [], [], []
            for h, (m_old, l_old) in enumerate(((m0, l0), (m1, l1))):
                s = _dot(qh[h], kj, "nt") - f_ref[h:h + 1, pl.ds(start, tq)]
                if diag:
                    s = jnp.where(col <= row, s, NEG_BIG)
                m_new = jnp.maximum(m_old, jnp.max(s, axis=1, keepdims=True))
                p = jnp.exp(s - m_new)
                alpha = jnp.exp(m_old - m_new)
                ls.append(alpha * l_old + jnp.sum(p, axis=1, keepdims=True))
                ms.append(m_new)
                alphas.append(alpha)
                vh = jnp.where(lo, vj, 0.0) if h == 0 else jnp.where(lo, 0.0, vj)
                pvs.append(_dot(p.astype(BF16), vh.astype(BF16), "nn"))
            acc = acc * jnp.where(lo, alphas[0], alphas[1]) + (pvs[0] + pvs[1])
            return ms[0], ls[0], ms[1], ls[1], acc

        neg = jnp.full((tq, 1), NEG_BIG, F32)
        zero = jnp.zeros((tq, 1), F32)
        init = (neg, zero, neg, zero, jnp.zeros((tq, LANES), F32))
        carry = lax.fori_loop(0, i, lambda j, c: step(j, c, False), init)
        m0, l0, m1, l1, acc = step(i, carry, True)
        o_ref[...] = (acc / jnp.where(lo, l0, l1)).astype(o_ref.dtype)
        lse_ref[:, 0:1] = m0 + jnp.log(l0)
        lse_ref[:, 1:2] = m1 + jnp.log(l1)

    return pl.pallas_call(
        body,
        out_shape=(_sds((batch * seq, ATTN_WIDTH), BF16), _sds((HEAD_PAIRS, batch * seq, 2), F32)),
        grid=(batch, HEAD_PAIRS, nq),
        in_specs=[
            pl.BlockSpec((tq, LANES), lambda b, hp, i: (b * nq + i, 3 * hp)),
            pl.BlockSpec((seq, LANES), lambda b, hp, i: (b, 3 * hp + 1)),
            pl.BlockSpec((seq, LANES), lambda b, hp, i: (b, 3 * hp + 2)),
            pl.BlockSpec((None, None, 2, seq), lambda b, hp, i: (b, hp, 0, 0)),
        ],
        out_specs=(
            pl.BlockSpec((tq, LANES), lambda b, hp, i: (b * nq + i, hp)),
            pl.BlockSpec((None, tq, 2), lambda b, hp, i: (hp, b * nq + i, 0)),
        ),
        compiler_params=pltpu.CompilerParams(dimension_semantics=("parallel", "parallel", "parallel")),
        name=name,
    )(qkv, qkv, qkv, frow)


def _attn_bwd(name, qkv, do, o, lse, frow, dproj, qkv_off, batch, seq, tq):
    nq = seq // tq
    scale = 1.0 / math.sqrt(HEAD_DIM)

    def body(q_ref, k_ref, v_ref, do_ref, o_ref, lse_ref, f_ref, _, dqkv_ref, df_ref, drow_ref,
             dq_acc, dk_acc, dv_acc, df_acc):
        j = pl.program_id(2)
        lane = lax.broadcasted_iota(jnp.int32, (1, LANES), 1)
        lo = lane < HEAD_DIM
        masks = (lo, jnp.logical_not(lo))
        row = lax.broadcasted_iota(jnp.int32, (tq, tq), 0)
        col = lax.broadcasted_iota(jnp.int32, (tq, tq), 1)

        @pl.when(j == 0)
        def _():
            dq_acc[...] = jnp.zeros_like(dq_acc)
            drow_ref[...] = jnp.zeros_like(drow_ref)

        dk_acc[...] = jnp.zeros_like(dk_acc)
        dv_acc[...] = jnp.zeros_like(dv_acc)
        df_acc[...] = jnp.zeros_like(df_acc)
        kj = k_ref[...]
        vj = v_ref[...]
        kstart = pl.multiple_of(j * tq, tq)
        kh = tuple(jnp.where(mk, kj, 0.0).astype(BF16) for mk in masks)

        def step(i, diag):
            start = pl.multiple_of(i * tq, tq)
            rows = pl.ds(start, tq)
            qi = q_ref[rows, :] * scale
            doi = do_ref[rows, :]
            prod = doi.astype(F32) * o_ref[rows, :].astype(F32)
            lse_i = lse_ref[rows, :]
            dq_i = jnp.zeros((tq, LANES), F32)
            for h, mk in enumerate(masks):
                q_h = jnp.where(mk, qi, 0.0).astype(BF16)
                do_h = jnp.where(mk, doi, 0.0).astype(BF16)
                delta = jnp.sum(jnp.where(mk, prod, 0.0), axis=1, keepdims=True)
                s = _dot(q_h, kj, "nt") - f_ref[h:h + 1, pl.ds(kstart, tq)]
                p = jnp.exp(s - lse_i[:, h:h + 1])
                if diag:
                    p = jnp.where(col <= row, p, 0.0)
                ds = p * (_dot(do_h, vj, "nt") - delta)
                df_acc[h:h + 1, :] += jnp.sum(ds, axis=0, keepdims=True)
                drow_ref[rows, h:h + 1] += jnp.sum(ds, axis=1, keepdims=True)
                dsb = ds.astype(BF16)
                dv_acc[...] += _dot(p.astype(BF16), do_h, "tn")
                dk_acc[...] += _dot(dsb, q_h, "tn")
                dq_i = dq_i + _dot(dsb, kh[h], "nn")
            dq_acc[rows, :] += dq_i

        step(j, True)
        lax.fori_loop(j + 1, nq, lambda i, c: (step(i, False), c)[1], 0)
        dqkv_ref[:, 0:LANES] = (dq_acc[pl.ds(kstart, tq), :] * scale).astype(dqkv_ref.dtype)
        dqkv_ref[:, LANES:2 * LANES] = dk_acc[...].astype(dqkv_ref.dtype)
        dqkv_ref[:, 2 * LANES:3 * LANES] = dv_acc[...].astype(dqkv_ref.dtype)
        df_ref[...] = df_acc[...]

    full = lambda c: pl.BlockSpec((seq, LANES), lambda b, hp, j: (b, c(hp)))
    blk = lambda c: pl.BlockSpec((tq, LANES), lambda b, hp, j: (b * nq + j, c(hp)))
    return pl.pallas_call(
        body,
        out_shape=(_sds(dproj.shape, dproj.dtype), _sds((batch, HEAD_PAIRS, 2, seq), F32),
                   _sds((HEAD_PAIRS, batch * seq, 2), F32)),
        grid=(batch, HEAD_PAIRS, nq),
        in_specs=[
            full(lambda hp: 3 * hp),
            blk(lambda hp: 3 * hp + 1),
            blk(lambda hp: 3 * hp + 2),
            full(lambda hp: hp),
            full(lambda hp: hp),
            pl.BlockSpec((None, seq, 2), lambda b, hp, j: (hp, b, 0)),
            pl.BlockSpec((None, None, 2, seq), lambda b, hp, j: (b, hp, 0, 0)),
            pl.BlockSpec(memory_space=pl.ANY),
        ],
        out_specs=(
            pl.BlockSpec((tq, 3 * LANES), lambda b, hp, j: (b * nq + j, qkv_off // (3 * LANES) + hp)),
            pl.BlockSpec((None, None, 2, tq), lambda b, hp, j: (b, hp, 0, j)),
            pl.BlockSpec((None, seq, 2), lambda b, hp, j: (hp, b, 0)),
        ),
        scratch_shapes=[
            pltpu.VMEM((seq, LANES), F32),
            pltpu.VMEM((tq, LANES), F32),
            pltpu.VMEM((tq, LANES), F32),
            pltpu.VMEM((2, tq), F32),
        ],
        input_output_aliases={7: 0},
        compiler_params=pltpu.CompilerParams(dimension_semantics=("parallel", "parallel", "arbitrary")),
        name=name,
    )(qkv, qkv, qkv, do, o, lse, frow, dproj)


def _mesh_place():
    x, y, c = lax.axis_index("x"), lax.axis_index("y"), lax.axis_index("c")
    chips = [(1 - x, y), (x, 1 - y), (1 - x, 1 - y)]
    return x, y, c, chips


def _hbm_specs(n):
    return [pl.BlockSpec(memory_space=pl.ANY)] * n


def _half(shape2d, axis, which):
    size = shape2d[axis] // 2
    sl = pl.ds(pl.multiple_of(which * size, 16 if axis == 0 else LANES), size)
    return (sl, slice(None)) if axis == 0 else (slice(None), sl)


def _gather_weights(bigs, axes, smalls):
    nb, ns = len(bigs), len(smalls)
    arrays = list(bigs) + list(smalls)
    n = nb + ns

    def body(*refs):
        ins, outs = refs[:n], refs[n:2 * n]
        send_sems, recv_sems = refs[2 * n:]
        x, y, c, chips = _mesh_place()
        me = 2 * x + y
        sibling = (x, y, 1 - c)

        def half(a, which):
            return _half(arrays[a].shape, axes[a], which)

        def copy(a, k, src, dst, to):
            return pltpu.make_async_remote_copy(src_ref=src, dst_ref=dst, send_sem=send_sems.at[a, k],
                                                recv_sem=recv_sems.at[a, k], device_id=to, device_id_type=MESH)

        sends = []
        for a in range(n):
            for j, chip in enumerate(chips):
                if a < nb:
                    cp = copy(a, j, ins[a].at[half(a, c)], outs[a].at[(me,) + half(a, c)], (*chip, c))
                else:
                    cp = copy(a, j, ins[a], outs[a].at[me], (*chip, c))
                cp.start()
                sends.append(cp)
        for a in range(nb):
            for j, (px, py) in enumerate(chips):
                blk = outs[a].at[(2 * px + py,) + half(a, c)]
                copy(a, j, blk, blk, (px, py, c)).wait_recv()
                fwd = copy(a, 3 + j, blk, blk, sibling)
                fwd.start()
                sends.append(fwd)
        for a in range(nb, n):
            for j, (px, py) in enumerate(chips):
                blk = outs[a].at[2 * px + py]
                copy(a, j, blk, blk, (px, py, c)).wait_recv()
        for a in range(nb):
            for j, (px, py) in enumerate(chips):
                blk = outs[a].at[(2 * px + py,) + half(a, 1 - c)]
                copy(a, 3 + j, blk, blk, sibling).wait_recv()
        for cp in sends:
            cp.wait_send()

    outs = pl.pallas_call(
        body,
        out_shape=tuple(_sds((N_CHIPS,) + a.shape, a.dtype) for a in arrays),
        in_specs=_hbm_specs(n),
        out_specs=tuple(_hbm_specs(n)),
        scratch_shapes=[pltpu.SemaphoreType.DMA((n, 6)), pltpu.SemaphoreType.DMA((n, 6))],
        name="gather_weights",
    )(*arrays)
    me = 2 * lax.axis_index("x") + lax.axis_index("y")
    return tuple(lax.dynamic_update_index_in_dim(o, a, me, 0) for o, a in zip(outs, arrays))


def _gather_small(v):
    m_per, ncol = v.shape

    def body(x_ref, out_ref, send_sems, recv_sems, local_sem):
        x, y, c, chips = _mesh_place()
        me, sibling = (x, y, c), (x, y, 1 - c)

        def rows(px, py, pc):
            return out_ref.at[pl.ds((4 * px + 2 * py + pc) * m_per, m_per), :]

        def copy(k, block, to, src=None):
            return pltpu.make_async_remote_copy(src_ref=rows(*block) if src is None else src, dst_ref=rows(*block),
                                                send_sem=send_sems.at[k], recv_sem=recv_sems.at[k],
                                                device_id=to, device_id_type=MESH)

        mine = pltpu.make_async_copy(x_ref, rows(*me), local_sem)
        mine.start()
        first = [copy(0, me, sibling, src=x_ref)]
        first += [copy(1 + j, me, (*chip, c), src=x_ref) for j, chip in enumerate(chips)]
        for cp in first:
            cp.start()
        passed = [copy(4 + j, (*chip, c), sibling) for j, chip in enumerate(chips)]
        for j, chip in enumerate(chips):
            copy(1 + j, (*chip, c), me).wait_recv()
            passed[j].start()
        copy(0, sibling, me).wait_recv()
        for j, chip in enumerate(chips):
            copy(4 + j, (*chip, 1 - c), me).wait_recv()
        for cp in first + passed:
            cp.wait_send()
        mine.wait()

    return pl.pallas_call(
        body,
        out_shape=_sds((N_DEV * m_per, ncol), v.dtype),
        in_specs=[pl.BlockSpec(memory_space=pltpu.VMEM)],
        out_specs=pl.BlockSpec(memory_space=pltpu.VMEM),
        scratch_shapes=[pltpu.SemaphoreType.DMA((7,)), pltpu.SemaphoreType.DMA((7,)), pltpu.SemaphoreType.DMA],
        name="gather_small",
    )(v)


def _half_shape(shape2d, axis):
    return (shape2d[0] // 2, shape2d[1]) if axis == 0 else (shape2d[0], shape2d[1] // 2)


def _exchange_sibling(name, grads, axes):
    n = len(grads)

    def body(*refs):
        ins, outs = refs[:n], refs[n:2 * n]
        send_sems, recv_sems = refs[2 * n:]
        x, y, c, _ = _mesh_place()
        copies = []
        for a in range(n):
            src = ins[a].at[(slice(None),) + _half(grads[a].shape[1:], axes[a], 1 - c)]
            cp = pltpu.make_async_remote_copy(src_ref=src, dst_ref=outs[a], send_sem=send_sems.at[a],
                                              recv_sem=recv_sems.at[a], device_id=(x, y, 1 - c), device_id_type=MESH)
            cp.start()
            copies.append(cp)
        for cp in copies:
            cp.wait()

    return pl.pallas_call(
        body,
        out_shape=tuple(_sds((N_CHIPS,) + _half_shape(g.shape[1:], ax), g.dtype) for g, ax in zip(grads, axes)),
        in_specs=_hbm_specs(n),
        out_specs=tuple(_hbm_specs(n)),
        scratch_shapes=[pltpu.SemaphoreType.DMA((n,)), pltpu.SemaphoreType.DMA((n,))],
        name=name,
    )(*grads)


_HBM = pl.BlockSpec(memory_space=pltpu.HBM)
_SEM = pl.BlockSpec(memory_space=pltpu.SEMAPHORE)
_EFFECT = pltpu.SideEffectType.DATAFLOW_SIDE_EFFECTING


def _chip_copies(kind, srcs, lands, send_sems, recv_sems):
    x, y, c, chips = _mesh_place()
    copies = []
    for a in range(len(srcs)):
        for j, (px, py) in enumerate(chips):
            if kind == "gather":
                src, dst = srcs[a], lands[a].at[2 * x + y]
            else:
                src, dst = srcs[a].at[j], lands[a].at[j]
            copies.append(pltpu.make_async_remote_copy(src_ref=src, dst_ref=dst, send_sem=send_sems.at[3 * a + j],
                                                       recv_sem=recv_sems.at[3 * a + j], device_id=(px, py, c),
                                                       device_id_type=MESH))
    return copies


def _chips_start(name, kind, srcs):
    n = len(srcs)
    slots = N_CHIPS if kind == "gather" else 3
    lands = [lax.empty((slots,) + (s.shape if kind == "gather" else s.shape[1:]), s.dtype) for s in srcs]

    def body(*refs):
        for cp in _chip_copies(kind, refs[:n], refs[n:2 * n], refs[2 * n], refs[2 * n + 1]):
            cp.start()
        refs[-1][...] = jnp.zeros_like(refs[-1])

    outs = pl.pallas_call(
        body,
        out_shape=(pltpu.SemaphoreType.DMA((3 * n,)), pltpu.SemaphoreType.DMA((3 * n,)),
                   *[pltpu.HBM(v.shape, v.dtype) for v in (*srcs, *lands)], _sds((8, LANES), F32)),
        in_specs=[_HBM] * (2 * n),
        out_specs=(_SEM, _SEM, *[_HBM] * (2 * n), pl.BlockSpec(memory_space=pltpu.VMEM)),
        input_output_aliases={i: 2 + i for i in range(2 * n)},
        compiler_params=pltpu.CompilerParams(has_side_effects=_EFFECT),
        name=name,
    )(*[pltpu.with_memory_space_constraint(v, pltpu.HBM) for v in (*srcs, *lands)])
    return outs[:-1], outs[-1]


def _chips_wait(name, kind, handles, after):
    send_sems, recv_sems, *thru = handles
    n = len(thru) // 2

    def body(*refs):
        for cp in _chip_copies(kind, refs[:n], refs[n:2 * n], refs[2 * n], refs[2 * n + 1]):
            cp.wait_send()
            cp.wait_recv()

    outs = pl.pallas_call(
        body,
        out_shape=tuple(pltpu.HBM(v.shape, v.dtype) for v in thru),
        in_specs=[_HBM] * (2 * n) + [_SEM, _SEM, pl.BlockSpec(memory_space=pl.ANY)],
        out_specs=tuple([_HBM] * (2 * n)),
        input_output_aliases={i: i for i in range(2 * n)},
        compiler_params=pltpu.CompilerParams(has_side_effects=_EFFECT),
        name=name,
    )(*thru, send_sems, recv_sems, after)
    return outs[n:]


def _share_sibling(name, shards, axes):
    n = len(shards)

    def body(*refs):
        ins, outs = refs[:n], refs[n:2 * n]
        send_sems, recv_sems = refs[2 * n:]
        x, y, c, _ = _mesh_place()
        started = []
        for a in range(n):
            mine = _half(shards[a].shape, axes[a], c)
            theirs = _half(shards[a].shape, axes[a], 1 - c)
            cp = pltpu.make_async_remote_copy(src_ref=ins[a].at[mine], dst_ref=outs[a].at[mine],
                                              send_sem=send_sems.at[a], recv_sem=recv_sems.at[a],
                                              device_id=(x, y, 1 - c), device_id_type=MESH)
            cp.start()
            arrival = pltpu.make_async_remote_copy(src_ref=ins[a].at[theirs], dst_ref=outs[a].at[theirs],
                                                   send_sem=send_sems.at[a], recv_sem=recv_sems.at[a],
                                                   device_id=(x, y, 1 - c), device_id_type=MESH)
            started.append((cp, arrival))
        for cp, arrival in started:
            arrival.wait_recv()
            cp.wait_send()

    return pl.pallas_call(
        body,
        out_shape=tuple(_sds(s.shape, s.dtype) for s in shards),
        in_specs=_hbm_specs(n),
        out_specs=tuple(_hbm_specs(n)),
        scratch_shapes=[pltpu.SemaphoreType.DMA((n,)), pltpu.SemaphoreType.DMA((n,))],
        input_output_aliases={a: a for a in range(n)},
        name=name,
    )(*shards)


def _pair_sum(name, place, g, got, axis):
    hr, hc = got.shape[1:]

    def body(place_ref, g_ref, got_ref, o_ref):
        o_ref[...] = (g_ref[...] + got_ref[...]).astype(o_ref.dtype)

    blk = (None, hr, hc)
    mine = (lambda j, pr: (pr[2 + j], pr[1], 0)) if axis == 0 else (lambda j, pr: (pr[2 + j], 0, pr[1]))
    return pl.pallas_call(
        body,
        out_shape=_sds((N_CHIPS - 1, hr, hc), BF16),
        grid_spec=pltpu.PrefetchScalarGridSpec(
            num_scalar_prefetch=1,
            grid=(N_CHIPS - 1,),
            in_specs=[pl.BlockSpec(blk, mine), pl.BlockSpec(blk, lambda j, pr: (pr[2 + j], 0, 0))],
            out_specs=pl.BlockSpec(blk, lambda j, pr: (j, 0, 0)),
        ),
        compiler_params=pltpu.CompilerParams(dimension_semantics=("parallel",)),
        name=name,
    )(place, g, got)


def _chip_sum(name, place, g, got, arrivals, axis):
    _, r, cdim = g.shape
    hr, hc = got.shape[1:]

    def body(place_ref, g_ref, got_ref, arr_ref, o_ref):
        acc = g_ref[...] + got_ref[...]
        for j in range(3):
            acc = acc + arr_ref[j].astype(F32)
        o_ref[...] = acc

    blk = (None, hr, hc)
    mine = (lambda i, pr: (pr[0], pr[1], 0)) if axis == 0 else (lambda i, pr: (pr[0], 0, pr[1]))
    dest = (lambda i, pr: (pr[1], 0)) if axis == 0 else (lambda i, pr: (0, pr[1]))
    return pl.pallas_call(
        body,
        out_shape=_sds((r, cdim), F32),
        grid_spec=pltpu.PrefetchScalarGridSpec(
            num_scalar_prefetch=1,
            grid=(1,),
            in_specs=[
                pl.BlockSpec(blk, mine),
                pl.BlockSpec(blk, lambda i, pr: (pr[0], 0, 0)),
                pl.BlockSpec((3, hr, hc), lambda i, pr: (0, 0, 0)),
            ],
            out_specs=pl.BlockSpec((hr, hc), dest),
        ),
        compiler_params=pltpu.CompilerParams(dimension_semantics=("arbitrary",)),
        name=name,
    )(place, g, got, arrivals)


def _device_sum(name, gathered):
    m_per = gathered.shape[0] // N_DEV

    def body(g_ref, o_ref):
        acc = g_ref[0:m_per, :]
        for dev in range(1, N_DEV):
            acc = acc + g_ref[dev * m_per:(dev + 1) * m_per, :]
        o_ref[...] = acc

    return pl.pallas_call(body, out_shape=_sds((m_per, gathered.shape[1]), F32), name=name)(gathered)


def _adamw(name, w, g, m, v):
    r, cdim = w.shape
    if r % 8 == 0:
        tr, tcol = _tile(r, 256, 8), cdim
    else:
        tr, tcol = r, (_tile(cdim, 256, LANES) if cdim % LANES == 0 else cdim)
    blk = pl.BlockSpec((tr, tcol), lambda i, j: (i, j))
    grid = (r // tr, cdim // tcol)
    bc1 = 1.0 - ADAM_B1 ** ADAM_STEP
    bc2 = 1.0 - ADAM_B2 ** ADAM_STEP

    def body(w_ref, g_ref, m_ref, v_ref, d_ref, nm_ref, nv_ref):
        gv = g_ref[...]
        nm = ADAM_B1 * m_ref[...] + (1.0 - ADAM_B1) * gv
        nv = ADAM_B2 * v_ref[...] + (1.0 - ADAM_B2) * (gv * gv)
        d_ref[...] = -ADAM_LR * ((nm / bc1) / (jnp.sqrt(nv / bc2) + ADAM_EPS) + ADAM_WD * w_ref[...])
        nm_ref[...] = nm
        nv_ref[...] = nv

    shape = _sds(w.shape, F32)
    return pl.pallas_call(
        body,
        out_shape=(shape, shape, shape),
        grid=grid,
        in_specs=[blk] * 4,
        out_specs=(blk, blk, blk),
        compiler_params=pltpu.CompilerParams(dimension_semantics=("parallel", "parallel")),
        name=name,
    )(w, g, m, v)


def _cat_cols(g):
    return jnp.transpose(g, (1, 0, 2)).reshape(g.shape[1], N_CHIPS * g.shape[2])


def _split_cols(a):
    r, c4 = a.shape
    return jnp.transpose(a.reshape(r, N_CHIPS, c4 // N_CHIPS), (1, 0, 2))


def _local_step(x, target, w_int, late_weights, cmw, cfw, g1, b_f, b_gate, g2, gf,
                ffn_grads_ready, mix_grads_ready):
    batch, seq, d = x.shape
    t = batch * seq
    cw = d // 2
    fh = cfw.shape[1] // 2
    tc = LANES
    nct = cw // tc
    tq = min(512, seq)
    pc_w, qkv_w, gl_w = 3 * cw, 3 * ATTN_WIDTH, 2 * d
    qkv_off, gl_off, f_off = pc_w, pc_w + qkv_w, pc_w + qkv_w + gl_w
    width = f_off + F_PAD
    f_col = pc_w + qkv_w

    w_pc = w_int[:pc_w].reshape(3, nct, tc, d).transpose(1, 0, 2, 3).reshape(pc_w, d)
    w_qkv = w_int[pc_w:f_col].reshape(3, HEAD_PAIRS, LANES, d).transpose(1, 0, 2, 3).reshape(qkv_w, d)
    w_f = jnp.pad(w_int[f_col:f_col + HEADS], ((0, F_PAD - HEADS), (0, 0)))
    w_inp = jnp.concatenate([w_pc, w_qkv, w_int[f_col + HEADS:], w_f], axis=0)
    bf_pad = jnp.pad(b_f, ((0, 0), (0, F_PAD - HEADS)))

    x2d = x.reshape(t, d)
    tgt2d = target.reshape(t, d)

    h1 = _rms_fwd("norm_mix", x2d, g1)
    pc, qkv, gl, fl = _project("proj_in", h1, w_inp, [(pc_w, BF16), (qkv_w, BF16), (gl_w, BF16), (F_PAD, F32)])
    a_c = _conv_fwd("conv_mix", pc, cmw, batch, seq, tc)
    f_cum = _forget_fwd("forget_cumsum", fl, bf_pad, batch, seq)
    frow = f_cum.reshape(batch, HEAD_PAIRS, 2, seq)
    o, lse = _attn_fwd("attn_fwd", qkv, frow, batch, seq, tq)
    w_oc, w_oa, w_o, w_up, w_down = late_weights(o)
    ycat = _branch_out("branch_out", a_c, w_oc, o, w_oa)
    mg = _merge_fwd("gate_merge", ycat, gl, b_gate)
    x2 = _mm("mix_out", mg, w_o, "nn", F32, m=t, n=d, k=d, add=x2d)
    h2 = _rms_fwd("norm_ffn", x2, g2)
    tcf = min(2 * LANES, fh)
    w_up2 = _cat_cols(w_up)
    hmid, ua, ub, ffn_a, ffn_b = _ffn_up_act("ffn_up_act", h2, w_up2, cfw, batch, seq, tcf)
    x3 = _mm("ffn_down", hmid, w_down, "nn", F32, m=t, n=d, k=fh, add=x2, tk=4096)

    dx3, dx3b, loss_row, d_gf = _final_loss("final_loss", x3, gf.reshape(1, d), tgt2d)
    dw_down = _mm("dw_down", hmid, dx3b, "tn", F32, m=fh, n=d, k=t, tm=256, tk=8192)
    du_a, du_b, d_cfw = _ffn_bwd("d_ffn", dx3b, w_down, ua, ub, ffn_a, ffn_b, cfw, batch, seq, tcf)
    ws = w_up.shape[2]
    dh2 = _ffn_up_bwd("d_ffn_up", du_a, du_b, w_up2)
    dw_up = _mm("dw_up_a", h2, du_a, "tn", F32, m=d, n=fh, k=t, tm=512, tn=ws, tk=4096, o3=N_CHIPS)
    dw_up = _mm("dw_up_b", h2, du_b, "tn", F32, m=d, n=fh, k=t, tm=512, tn=ws, tk=4096, o3=N_CHIPS, out=dw_up,
                o_off=fh)
    token = ffn_grads_ready(dw_up, dw_down)
    if token is not None:
        g2 = g2 + token[0:1, 0:1]
    dx2, dx2b, d_g2 = _rms_bwd("d_norm_ffn", x2, dh2, g2, dx3, True)
    dm = _mm("d_merge", dx2b, w_o, "nt", BF16, m=t, n=d, k=d)
    dw_o = _mm("dw_o", mg, dx2b, "tn", F32, m=d, n=d, k=t, tm=256, tk=8192)
    dproj, dycat, d_bg = _merge_bwd("d_gate_merge", dm, ycat, gl, b_gate, width, gl_off)
    da_c, do = _branch_out_bwd("d_branch_out", dycat, w_oc, w_oa)
    dw_oc, dw_oa = _branch_out_dw("dw_branch_out", a_c, o, dycat)
    dproj, d_cmw = _conv_bwd("d_conv_mix", da_c, pc, cmw, dproj, batch, seq, tc)
    dproj, d_fkey, d_fquery = _attn_bwd("attn_bwd", qkv, do, o, lse, frow, dproj, qkv_off, batch, seq, tq)
    d_fquery = jnp.pad(jnp.transpose(d_fquery, (1, 0, 2)).reshape(t, HEADS), ((0, 0), (0, LANES - HEADS)))
    dproj, d_bf = _forget_bwd("d_forget", d_fkey.reshape(batch, HEADS, seq), d_fquery, fl, bf_pad, dproj, f_off,
                              batch, seq)
    dw_inp = _mm("dw_in", dproj, h1, "tn", F32, m=width, n=d, k=t, tm=256, tk=8192)
    d_pc = dw_inp[:pc_w].reshape(nct, 3, tc, d).transpose(1, 0, 2, 3).reshape(pc_w, d)
    d_qkv = dw_inp[qkv_off:gl_off].reshape(HEAD_PAIRS, 3, LANES, d).transpose(1, 0, 2, 3).reshape(qkv_w, d)
    dw_int = jnp.concatenate([d_pc, d_qkv, dw_inp[f_off:f_off + HEADS], dw_inp[gl_off:f_off]], axis=0)
    token = mix_grads_ready(dw_int, dw_oc, dw_oa, dw_o)
    dh1 = _mm("d_norm_mix", dproj, w_inp, "nn", BF16, m=t, n=d, k=width, tm=512, tk=8192, dep=token)
    grad_x, d_g1 = _rms_bwd("d_norm_mix_x", x2d, dh1, g1, dx2, False)
    smalls = (d_g1, d_g2, d_gf, d_bg, d_bf, d_cmw, d_cfw)
    return loss_row[0, 0], grad_x.reshape(batch, seq, d), smalls


def _pack_small(parts):
    flat = [p.reshape(-1) for p in parts]
    sizes = [f.shape[0] for f in flat]
    total = sum(sizes)
    padded = -(-total // (8 * LANES)) * (8 * LANES)
    vec = jnp.concatenate(flat + [jnp.zeros((padded - total,), F32)])
    offsets = [sum(sizes[:i]) for i in range(len(sizes))]
    return vec.reshape(padded // LANES, LANES), offsets


def kernel(x, norm_mix_g, w_in, b_f, b_gate, conv_mix_w, w_out_conv, w_out_attn, w_o, norm_ffn_g, w_up, conv_ffn_w, w_down, norm_f_g, loss_target, m_norm_mix_g, m_w_in, m_b_f, m_b_gate, m_conv_mix_w, m_w_out_conv, m_w_out_attn, m_w_o, m_norm_ffn_g, m_w_up, m_conv_ffn_w, m_w_down, m_norm_f_g, v_norm_mix_g, v_w_in, v_b_f, v_b_gate, v_conv_mix_w, v_w_out_conv, v_w_out_attn, v_w_o, v_norm_ffn_g, v_w_up, v_conv_ffn_w, v_w_down, v_norm_f_g):
    d = x.shape[-1]
    chip = 2 * lax.axis_index("x") + lax.axis_index("y")
    xi, yi = lax.axis_index("x"), lax.axis_index("y")
    peers = [2 * px + py for px, py in ((1 - xi, yi), (xi, 1 - yi), (1 - xi, 1 - yi))]
    place = jnp.stack([chip, lax.axis_index("c"), *peers]).astype(jnp.int32)

    t_in, t_m_in, t_v_in = (jnp.transpose(w[0]) for w in (w_in, m_w_in, v_w_in))

    def row_shards(a):
        return a.reshape(N_CHIPS, a.shape[0] // N_CHIPS, a.shape[1])

    def stacked(a):
        return a.reshape(N_CHIPS * a.shape[1], a.shape[2])

    a_in, a_cmw, a_cfw = _gather_weights([t_in.astype(BF16)], (1,), [conv_mix_w[0], conv_ffn_w[0]])
    late = [w[0].astype(BF16) for w in (w_out_conv, w_out_attn, w_o, w_up, w_down)]
    late_handles, late_token = _chips_start("gather_late_start", "gather", late)

    def late_weights(after):
        lands = _chips_wait("gather_late_wait", "gather", late_handles, after)
        a_oc, a_oa, a_o, a_up, a_down = (
            lax.dynamic_update_index_in_dim(buf, own, chip, 0) for buf, own in zip(lands, late))
        return _cat_cols(a_oc), _cat_cols(a_oa), stacked(a_o), a_up, stacked(a_down)

    pending = []

    def reduce_start(tag, names, grads, axes):
        got = _exchange_sibling("exchange_sibling_" + tag, grads, axes)
        sums = [_pair_sum("pair_sum_" + nm, place, g, r, ax) for nm, g, r, ax in zip(names, grads, got, axes)]
        handles, token = _chips_start("exchange_chips_start_" + tag, "reduce", sums)
        pending.append((tag, names, grads, axes, got, handles))
        return token

    def ffn_grads_ready(dw_up, dw_down):
        return reduce_start("ffn", ("w_up", "w_down"), [dw_up, row_shards(dw_down)], (0, 0))

    def mix_grads_ready(dw_int, dw_oc, dw_oa, dw_o):
        return reduce_start("mix", ("w_in", "w_out_conv", "w_out_attn", "w_o"),
                            [row_shards(dw_int), _split_cols(dw_oc), _split_cols(dw_oa), row_shards(dw_o)],
                            (1, 0, 0, 0))

    loss_local, grad_x, smalls = _local_step(
        x, loss_target, stacked(a_in), late_weights, _cat_cols(a_cmw),
        _cat_cols(a_cfw), norm_mix_g + late_token[0:1, 0:1], b_f, b_gate, norm_ffn_g, norm_f_g,
        ffn_grads_ready, mix_grads_ready)

    reduced = {}
    for tag, names, grads, axes, got, handles in pending:
        arrivals = _chips_wait("exchange_chips_wait_" + tag, "reduce", handles, grad_x)
        halves = [_chip_sum("chip_sum_" + nm, place, g, r, arr, ax)
                  for nm, g, r, arr, ax in zip(names, grads, got, arrivals, axes)]
        reduced.update(zip(names, _share_sibling("share_sibling_" + tag, halves, axes)))
    g_in, g_oc, g_oa, g_o, g_up, g_down = (
        reduced[nm] for nm in ("w_in", "w_out_conv", "w_out_attn", "w_o", "w_up", "w_down"))

    smalls = (*smalls, loss_local.reshape(1, 1))
    packed, offs = _pack_small(smalls)
    total = _device_sum("device_sum", _gather_small(packed)).reshape(-1)
    shapes = [s.shape for s in smalls]
    d_g1, d_g2, d_gf, d_bg, d_bf, d_cmw, d_cfw, loss = [
        total[o:o + math.prod(sh)].reshape(sh) for o, sh in zip(offs, shapes)]
    loss = loss[0, 0]
    d_bf = d_bf[:, :HEADS]
    cw_s, cf_s = conv_mix_w.shape[2], conv_ffn_w.shape[2]
    d_cmw = lax.dynamic_slice(d_cmw, (0, chip * cw_s), (3, cw_s))
    d_cfw = lax.dynamic_slice(d_cfw, (0, chip * cf_s), (3, cf_s))

    order = [
        ("norm_mix_g", norm_mix_g[0:1], d_g1, m_norm_mix_g, v_norm_mix_g),
        ("w_in", t_in, g_in, t_m_in, t_v_in),
        ("b_f", b_f, d_bf, m_b_f, v_b_f),
        ("b_gate", b_gate, d_bg, m_b_gate, v_b_gate),
        ("conv_mix_w", conv_mix_w[0], d_cmw, m_conv_mix_w[0], v_conv_mix_w[0]),
        ("w_out_conv", w_out_conv[0], g_oc, m_w_out_conv[0], v_w_out_conv[0]),
        ("w_out_attn", w_out_attn[0], g_oa, m_w_out_attn[0], v_w_out_attn[0]),
        ("w_o", w_o[0], g_o, m_w_o[0], v_w_o[0]),
        ("norm_ffn_g", norm_ffn_g, d_g2, m_norm_ffn_g, v_norm_ffn_g),
        ("w_up", w_up[0], g_up, m_w_up[0], v_w_up[0]),
        ("conv_ffn_w", conv_ffn_w[0], d_cfw, m_conv_ffn_w[0], v_conv_ffn_w[0]),
        ("w_down", w_down[0], g_down, m_w_down[0], v_w_down[0]),
        ("norm_f_g", norm_f_g.reshape(1, d), d_gf, m_norm_f_g.reshape(1, d), v_norm_f_g.reshape(1, d)),
    ]
    out_shapes = [norm_mix_g.shape, w_in.shape, b_f.shape, b_gate.shape, conv_mix_w.shape, w_out_conv.shape,
                  w_out_attn.shape, w_o.shape, norm_ffn_g.shape, w_up.shape, conv_ffn_w.shape, w_down.shape,
                  norm_f_g.shape]
    g_out, d_out, m_out, v_out = [], [], [], []
    for (nm, w, g, m, v), sh in zip(order, out_shapes):
        g = g.reshape(w.shape)
        delta, new_m, new_v = _adamw("adamw_" + nm, w, g, m.reshape(w.shape), v.reshape(w.shape))
        for dst, val in ((g_out, g), (d_out, delta), (m_out, new_m), (v_out, new_v)):
            dst.append((jnp.transpose(val) if nm == "w_in" else val).reshape(sh))
    return (loss, grad_x, *g_out, *d_out, *m_out, *v_out)
```

```python
import math

import jax
import jax.numpy as jnp
from jax import lax
from jax.experimental import pallas as pl
from jax.experimental.pallas import tpu as pltpu

F32 = jnp.float32
BF16 = jnp.bfloat16
MESH = pl.DeviceIdType.MESH

EPS = 1e-6
HEADS = 8
HEAD_DIM = 64
ATTN_WIDTH = HEADS * HEAD_DIM
HEAD_PAIRS = HEADS // 2
LANES = 128
F_PAD = 2 * LANES
NEG_BIG = -1e30
N_CHIPS = 4
N_DEV = 8

ADAM_LR = 0.001
ADAM_B1 = 0.9
ADAM_B2 = 0.999
ADAM_EPS = 1e-08
ADAM_WD = 0.01
ADAM_STEP = 10

_DIMS = {
    "nn": (((1,), (0,)), ((), ())),
    "nt": (((1,), (1,)), ((), ())),
    "tn": (((0,), (0,)), ((), ())),
}


def _tile(n, target, mult, also=()):
    best = None
    for t in range(mult, n + 1, mult):
        if n % t == 0 and t <= target and all(o % t == 0 for o in also):
            best = t
    if best is None:
        assert all(o == 0 for o in also), (n, target, mult, also)
        return n
    return best


def _sds(shape, dtype):
    return jax.ShapeDtypeStruct(shape, dtype)


def _mm(name, a, b, mode, out_dtype, *, m, n, k, a_off=0, b_off=0, out=None, o_off=0,
        o_width=None, o3=None, add=None, dep=None, tm=1024, tn=2048, tk=2048):
    if mode == "nn":
        tm = _tile(m, tm, 16)
        tk = _tile(k, tk, LANES, (a_off,))
        tn = _tile(n, tn, LANES, (b_off, o_off))
        a_spec = pl.BlockSpec((tm, tk), lambda i, j, kk: (i, a_off // tk + kk))
        b_spec = pl.BlockSpec((tk, tn), lambda i, j, kk: (kk, b_off // tn + j))
    elif mode == "nt":
        tm = _tile(m, tm, 16)
        tk = _tile(k, tk, LANES, (a_off, b_off))
        tn = _tile(n, tn, LANES, (o_off,))
        a_spec = pl.BlockSpec((tm, tk), lambda i, j, kk: (i, a_off // tk + kk))
        b_spec = pl.BlockSpec((tn, tk), lambda i, j, kk: (j, b_off // tk + kk))
    else:
        tm = _tile(m, tm, LANES, (a_off,))
        tk = _tile(k, tk, 16)
        tn = _tile(n, tn, LANES, (b_off, o_off))
        a_spec = pl.BlockSpec((tk, tm), lambda i, j, kk: (kk, a_off // tm + i))
        b_spec = pl.BlockSpec((tk, tn), lambda i, j, kk: (kk, b_off // tn + j))
    assert m % tm == 0 and n % tn == 0 and k % tk == 0, (name, tm, tn, tk)
    nk = k // tk
    if o3 is not None:
        o_spec = pl.BlockSpec((None, tm, tn), lambda i, j, kk: (o_off // tn + j, i, 0))
        out_sds = _sds((o3, m, tn), out_dtype)
    else:
        o_spec = pl.BlockSpec((tm, tn), lambda i, j, kk: (i, o_off // tn + j))
        width = o_width if o_width is not None else (out.shape[1] if out is not None else n)
        out_sds = _sds((m, width), out_dtype)
    use_acc = nk > 1 and out_dtype != F32
    dims = _DIMS[mode]
    has_add, has_out = add is not None, out is not None

    def body(*refs):
        a_ref, b_ref = refs[0], refs[1]
        pos = 2
        add_ref = None
        if has_add:
            add_ref = refs[pos]
            pos += 1
        if has_out:
            pos += 1
        if dep is not None:
            pos += 1
        o_ref = refs[pos]
        acc_ref = refs[pos + 1] if use_acc else None
        part = lax.dot_general(a_ref[...].astype(BF16), b_ref[...].astype(BF16), dims,
                               preferred_element_type=F32)
        if nk == 1:
            if has_add:
                part = part + add_ref[...]
            o_ref[...] = part.astype(o_ref.dtype)
            return
        kk = pl.program_id(2)
        tgt = acc_ref if use_acc else o_ref

        @pl.when(kk == 0)
        def _():
            tgt[...] = part + add_ref[...] if has_add else part

        @pl.when(kk > 0)
        def _():
            tgt[...] += part

        if use_acc:
            @pl.when(kk == nk - 1)
            def _():
                o_ref[...] = acc_ref[...].astype(o_ref.dtype)

    operands, in_specs = [a, b], [a_spec, b_spec]
    if has_add:
        operands.append(add)
        in_specs.append(pl.BlockSpec((tm, tn), lambda i, j, kk: (i, j)))
    aliases = {}
    if has_out:
        aliases = {len(operands): 0}
        operands.append(out)
        in_specs.append(pl.BlockSpec(memory_space=pl.ANY))
    if dep is not None:
        operands.append(dep)
        in_specs.append(pl.BlockSpec(memory_space=pl.ANY))
    return pl.pallas_call(
        body,
        out_shape=out_sds,
        grid=(m // tm, n // tn, nk),
        in_specs=in_specs,
        out_specs=o_spec,
        scratch_shapes=[pltpu.VMEM((tm, tn), F32)] if use_acc else [],
        input_output_aliases=aliases,
        compiler_params=pltpu.CompilerParams(dimension_semantics=("parallel", "parallel", "arbitrary")),
        name=name,
    )(*operands)


def _project(name, h, w_t, groups):
    t, d = h.shape
    tm = _tile(t, 512, 16)
    offs = [sum(n for n, _ in groups[:i]) for i in range(len(groups))]

    def body(h_ref, w_ref, *o_refs):
        hv = h_ref[...]
        for (n, _), off, o_ref in zip(groups, offs, o_refs):
            o_ref[...] = _dot(hv, w_ref[off:off + n, :], "nt").astype(o_ref.dtype)

    return pl.pallas_call(
        body,
        out_shape=tuple(_sds((t, n), dt) for n, dt in groups),
        grid=(t // tm,),
        in_specs=[pl.BlockSpec((tm, d), lambda i: (i, 0)), pl.BlockSpec(w_t.shape, lambda i: (0, 0))],
        out_specs=tuple(pl.BlockSpec((tm, n), lambda i: (i, 0)) for n, _ in groups),
        compiler_params=pltpu.CompilerParams(dimension_semantics=("parallel",)),
        name=name,
    )(h, w_t)


def _branch_out(name, a1, w1, a2, w2):
    t = a1.shape[0]
    n1, n2 = w1.shape[1], w2.shape[1]
    tm = _tile(t, 1024, 16)

    def body(a1_ref, w1_ref, a2_ref, w2_ref, o_ref):
        o_ref[:, 0:n1] = _dot(a1_ref[...], w1_ref[...], "nn").astype(o_ref.dtype)
        o_ref[:, n1:n1 + n2] = _dot(a2_ref[...], w2_ref[...], "nn").astype(o_ref.dtype)

    whole = lambda w: pl.BlockSpec(w.shape, lambda i: (0, 0))
    rows = lambda a: pl.BlockSpec((tm, a.shape[1]), lambda i: (i, 0))
    return pl.pallas_call(
        body,
        out_shape=_sds((t, n1 + n2), BF16),
        grid=(t // tm,),
        in_specs=[rows(a1), whole(w1), rows(a2), whole(w2)],
        out_specs=pl.BlockSpec((tm, n1 + n2), lambda i: (i, 0)),
        compiler_params=pltpu.CompilerParams(dimension_semantics=("parallel",)),
        name=name,
    )(a1, w1, a2, w2)


def _branch_out_bwd(name, dy, w1, w2):
    t, d2 = dy.shape
    d = d2 // 2
    tm = _tile(t, 1024, 16)

    def body(dy_ref, w1_ref, w2_ref, o1_ref, o2_ref):
        o1_ref[...] = _dot(dy_ref[:, 0:d], w1_ref[...], "nt").astype(o1_ref.dtype)
        o2_ref[...] = _dot(dy_ref[:, d:d2], w2_ref[...], "nt").astype(o2_ref.dtype)

    whole = lambda w: pl.BlockSpec(w.shape, lambda i: (0, 0))
    return pl.pallas_call(
        body,
        out_shape=(_sds((t, w1.shape[0]), BF16), _sds((t, w2.shape[0]), BF16)),
        grid=(t // tm,),
        in_specs=[pl.BlockSpec((tm, d2), lambda i: (i, 0)), whole(w1), whole(w2)],
        out_specs=(pl.BlockSpec((tm, w1.shape[0]), lambda i: (i, 0)), pl.BlockSpec((tm, w2.shape[0]), lambda i: (i, 0))),
        compiler_params=pltpu.CompilerParams(dimension_semantics=("parallel",)),
        name=name,
    )(dy, w1, w2)


def _ffn_up_bwd(name, du_a, du_b, w_up):
    t, fh = du_a.shape
    d = w_up.shape[0]
    tm = _tile(t, 512, 16)

    def body(da_ref, db_ref, wa_ref, wb_ref, o_ref):
        acc = _dot(da_ref[...], wa_ref[...], "nt") + _dot(db_ref[...], wb_ref[...], "nt")
        o_ref[...] = acc.astype(o_ref.dtype)

    rows = pl.BlockSpec((tm, fh), lambda i: (i, 0))
    return pl.pallas_call(
        body,
        out_shape=_sds((t, d), BF16),
        grid=(t // tm,),
        in_specs=[rows, rows, pl.BlockSpec((d, fh), lambda i: (0, 0)), pl.BlockSpec((d, fh), lambda i: (0, 1))],
        out_specs=pl.BlockSpec((tm, d), lambda i: (i, 0)),
        compiler_params=pltpu.CompilerParams(dimension_semantics=("parallel",)),
        name=name,
    )(du_a, du_b, w_up, w_up)


def _branch_out_dw(name, a1, a2, dy):
    t, d2 = dy.shape
    d = d2 // 2
    tk = _tile(t, 2048, 16)

    def body(a1_ref, a2_ref, dy_ref, o1_ref, o2_ref):
        @pl.when(pl.program_id(0) == 0)
        def _():
            o1_ref[...] = jnp.zeros_like(o1_ref)
            o2_ref[...] = jnp.zeros_like(o2_ref)

        o1_ref[...] += _dot(a1_ref[...], dy_ref[:, 0:d], "tn")
        o2_ref[...] += _dot(a2_ref[...], dy_ref[:, d:d2], "tn")

    rows = lambda a: pl.BlockSpec((tk, a.shape[1]), lambda k: (k, 0))
    acc = lambda a: pl.BlockSpec((a.shape[1], d), lambda k: (0, 0))
    return pl.pallas_call(
        body,
        out_shape=(_sds((a1.shape[1], d), F32), _sds((a2.shape[1], d), F32)),
        grid=(t // tk,),
        in_specs=[rows(a1), rows(a2), rows(dy)],
        out_specs=(acc(a1), acc(a2)),
        compiler_params=pltpu.CompilerParams(dimension_semantics=("arbitrary",)),
        name=name,
    )(a1, a2, dy)


def _rms_fwd(name, x, g):
    t, d = x.shape
    tm = _tile(t, 512, 16)

    def body(x_ref, g_ref, o_ref):
        xv = x_ref[...]
        r = lax.rsqrt(jnp.mean(xv * xv, axis=-1, keepdims=True) + EPS)
        o_ref[...] = ((xv * r) * g_ref[...]).astype(o_ref.dtype)

    return pl.pallas_call(
        body,
        out_shape=_sds((t, d), BF16),
        grid=(t // tm,),
        in_specs=[pl.BlockSpec((tm, d), lambda i: (i, 0)), pl.BlockSpec((1, d), lambda i: (0, 0))],
        out_specs=pl.BlockSpec((tm, d), lambda i: (i, 0)),
        compiler_params=pltpu.CompilerParams(dimension_semantics=("parallel",)),
        name=name,
    )(x, g)


def _rms_bwd(name, x, dh, g, res):
    t, d = x.shape
    tm = _tile(t, 512, 16)

    def body(x_ref, dh_ref, g_ref, res_ref, dx_ref, dg_ref):
        xv = x_ref[...]
        r = lax.rsqrt(jnp.mean(xv * xv, axis=-1, keepdims=True) + EPS)
        xh = xv * r
        dhv = dh_ref[...].astype(F32)
        dxh = dhv * g_ref[...]
        dx_ref[...] = res_ref[...] + r * (dxh - xh * jnp.mean(dxh * xh, axis=-1, keepdims=True))

        @pl.when(pl.program_id(0) == 0)
        def _():
            dg_ref[...] = jnp.zeros_like(dg_ref)

        dg_ref[...] += jnp.sum(dhv * xh, axis=0, keepdims=True)

    row = pl.BlockSpec((tm, d), lambda i: (i, 0))
    vec = pl.BlockSpec((1, d), lambda i: (0, 0))
    return pl.pallas_call(
        body,
        out_shape=(_sds((t, d), F32), _sds((1, d), F32)),
        grid=(t // tm,),
        in_specs=[row, row, vec, row],
        out_specs=(row, vec),
        compiler_params=pltpu.CompilerParams(dimension_semantics=("arbitrary",)),
        name=name,
    )(x, dh, g, res)


def _final_loss(name, x, g, target):
    t, d = x.shape
    tm = _tile(t, 512, 16)

    def body(x_ref, g_ref, t_ref, dx_ref, dxb_ref, loss_ref, dg_ref):
        xv = x_ref[...]
        gv = g_ref[...]
        r = lax.rsqrt(jnp.mean(xv * xv, axis=-1, keepdims=True) + EPS)
        xh = xv * r
        err = xh * gv - t_ref[...]
        dy = err * (1.0 / d)
        dxh = dy * gv
        dx = r * (dxh - xh * jnp.mean(dxh * xh, axis=-1, keepdims=True))
        dx_ref[...] = dx
        dxb_ref[...] = dx.astype(dxb_ref.dtype)
        per_row = jnp.sum(err * err, axis=-1, keepdims=True) * (0.5 / d)

        @pl.when(pl.program_id(0) == 0)
        def _():
            dg_ref[...] = jnp.zeros_like(dg_ref)
            loss_ref[...] = jnp.zeros_like(loss_ref)

        dg_ref[...] += jnp.sum(dy * xh, axis=0, keepdims=True)
        loss_ref[...] += jnp.sum(per_row, axis=0, keepdims=True)

    row = pl.BlockSpec((tm, d), lambda i: (i, 0))
    vec = pl.BlockSpec((1, d), lambda i: (0, 0))
    return pl.pallas_call(
        body,
        out_shape=(_sds((t, d), F32), _sds((t, d), BF16), _sds((1, LANES), F32), _sds((1, d), F32)),
        grid=(t // tm,),
        in_specs=[row, vec, row],
        out_specs=(row, row, pl.BlockSpec((1, LANES), lambda i: (0, 0)), vec),
        compiler_params=pltpu.CompilerParams(dimension_semantics=("arbitrary",)),
        name=name,
    )(x, g, target)


def _shift_down(z, k):
    row = lax.broadcasted_iota(jnp.int32, z.shape, 0)
    return jnp.where(row >= k, pltpu.roll(z, k, axis=0), 0.0)


def _shift_up(z, k):
    s = z.shape[0]
    row = lax.broadcasted_iota(jnp.int32, z.shape, 0)
    return jnp.where(row < s - k, pltpu.roll(z, s - k, axis=0), 0.0)


def _conv3(z, w):
    return (w[2:3] * z + w[0:1] * _shift_down(z, 2)) + w[1:2] * _shift_down(z, 1)


def _conv3_t(dz, w):
    return (w[2:3] * dz + w[0:1] * _shift_up(dz, 2)) + w[1:2] * _shift_up(dz, 1)


def _conv_fwd(name, pc, w, batch, seq, tc):
    cw = w.shape[1]
    nct = cw // tc

    def body(pc_ref, w_ref, o_ref):
        cb = pc_ref[:, 0:tc].astype(F32)
        z = pc_ref[:, tc:2 * tc].astype(F32) * pc_ref[:, 2 * tc:3 * tc].astype(F32)
        o_ref[...] = (cb * _conv3(z, w_ref[...])).astype(o_ref.dtype)

    return pl.pallas_call(
        body,
        out_shape=_sds((batch * seq, cw), BF16),
        grid=(batch, nct),
        in_specs=[pl.BlockSpec((seq, 3 * tc), lambda b, j: (b, j)), pl.BlockSpec((3, tc), lambda b, j: (0, j))],
        out_specs=pl.BlockSpec((seq, tc), lambda b, j: (b, j)),
        compiler_params=pltpu.CompilerParams(dimension_semantics=("parallel", "parallel")),
        name=name,
    )(pc, w)


def _conv_bwd(name, da, pc, w, dproj, batch, seq, tc):
    cw = w.shape[1]
    nct = cw // tc

    def body(da_ref, pc_ref, w_ref, _, dpc_ref, dw_ref):
        wv = w_ref[...]
        cb = pc_ref[:, 0:tc].astype(F32)
        cc = pc_ref[:, tc:2 * tc].astype(F32)
        cin = pc_ref[:, 2 * tc:3 * tc].astype(F32)
        z = cc * cin
        dav = da_ref[...].astype(F32)
        du = dav * cb
        dz = _conv3_t(du, wv)
        dpc_ref[:, 0:tc] = (dav * _conv3(z, wv)).astype(dpc_ref.dtype)
        dpc_ref[:, tc:2 * tc] = (dz * cin).astype(dpc_ref.dtype)
        dpc_ref[:, 2 * tc:3 * tc] = (dz * cc).astype(dpc_ref.dtype)

        @pl.when(pl.program_id(1) == 0)
        def _():
            dw_ref[...] = jnp.zeros_like(dw_ref)

        dw_ref[0:1, :] += jnp.sum(du * _shift_down(z, 2), axis=0, keepdims=True)
        dw_ref[1:2, :] += jnp.sum(du * _shift_down(z, 1), axis=0, keepdims=True)
        dw_ref[2:3, :] += jnp.sum(du * z, axis=0, keepdims=True)

    return pl.pallas_call(
        body,
        out_shape=(_sds(dproj.shape, dproj.dtype), _sds((3, cw), F32)),
        grid=(nct, batch),
        in_specs=[
            pl.BlockSpec((seq, tc), lambda j, b: (b, j)),
            pl.BlockSpec((seq, 3 * tc), lambda j, b: (b, j)),
            pl.BlockSpec((3, tc), lambda j, b: (0, j)),
            pl.BlockSpec(memory_space=pl.ANY),
        ],
        out_specs=(pl.BlockSpec((seq, 3 * tc), lambda j, b: (b, j)), pl.BlockSpec((3, tc), lambda j, b: (0, j))),
        input_output_aliases={3: 0},
        compiler_params=pltpu.CompilerParams(dimension_semantics=("parallel", "arbitrary")),
        name=name,
    )(da, pc, w, dproj)


def _ffn_up_act(name, h2, w_up, w, batch, seq, tc):
    d = h2.shape[1]
    fh = w.shape[1] // 2
    nf = fh // tc

    def body(h_ref, ma_ref, mb_ref, wa_ref, wb_ref, o_ref, ua_ref, ub_ref, a_ref, b_ref):
        hv = h_ref[...]
        ua = _dot(hv, ma_ref[...], "nn")
        ub = _dot(hv, mb_ref[...], "nn")
        ua_ref[...] = ua.astype(ua_ref.dtype)
        ub_ref[...] = ub.astype(ub_ref.dtype)
        a = _conv3(ua, wa_ref[...])
        b = _conv3(ub, wb_ref[...])
        a_ref[...] = a.astype(a_ref.dtype)
        b_ref[...] = b.astype(b_ref.dtype)
        o_ref[...] = (a * jax.nn.sigmoid(a) * b).astype(o_ref.dtype)

    act = pl.BlockSpec((seq, tc), lambda b, j: (b, j))
    shape = _sds((batch * seq, fh), BF16)
    return pl.pallas_call(
        body,
        out_shape=(shape,) * 5,
        grid=(batch, nf),
        in_specs=[
            pl.BlockSpec((seq, d), lambda b, j: (b, 0)),
            pl.BlockSpec((d, tc), lambda b, j: (0, j)),
            pl.BlockSpec((d, tc), lambda b, j: (0, nf + j)),
            pl.BlockSpec((3, tc), lambda b, j: (0, j)),
            pl.BlockSpec((3, tc), lambda b, j: (0, nf + j)),
        ],
        out_specs=(act,) * 5,
        compiler_params=pltpu.CompilerParams(dimension_semantics=("parallel", "parallel")),
        name=name,
    )(h2, w_up, w_up, w, w)


def _ffn_bwd(name, dx, w_down, ua, ub, av, bv, w, batch, seq, tc):
    d = dx.shape[1]
    fh = w.shape[1] // 2
    nf = fh // tc

    rb = _tile(seq, 128, 8)
    halo = 8

    def body(dx_ref, md_ref, ua_ref, ub_ref, a_ref, b_ref, wa_ref, wb_ref, dua_ref, dub_ref, dw_ref,
             dh_scr, da_scr, db_scr):
        j = pl.program_id(1)
        dh_scr[...] = _dot(dx_ref[...].astype(BF16), md_ref[...], "nt")
        da_scr[seq:seq + halo, :] = jnp.zeros((halo, tc), F32)
        db_scr[seq:seq + halo, :] = jnp.zeros((halo, tc), F32)

        def silu_bwd(r, carry):
            rows = pl.ds(pl.multiple_of(r * rb, rb), rb)
            a, b, dhv = a_ref[rows, :].astype(F32), b_ref[rows, :].astype(F32), dh_scr[rows, :]
            sg = jax.nn.sigmoid(a)
            da_scr[rows, :] = dhv * b * (sg * (1.0 + a * (1.0 - sg)))
            db_scr[rows, :] = dhv * (a * sg)
            return carry

        lax.fori_loop(0, seq // rb, silu_bwd, 0)
        wa, wb = wa_ref[...], wb_ref[...]

        def conv_bwd(r, sums):
            r0 = pl.multiple_of(r * rb, rb)
            rows = pl.ds(r0, rb)
            out = []
            for d_scr, u_ref, wv, du_ref, acc in ((da_scr, ua_ref, wa, dua_ref, sums[0:3]),
                                                  (db_scr, ub_ref, wb, dub_ref, sums[3:6])):
                x = d_scr[pl.ds(r0, rb + halo), :]
                dv = x[0:rb]
                up1 = pltpu.roll(x, rb + halo - 1, axis=0)[0:rb]
                up2 = pltpu.roll(x, rb + halo - 2, axis=0)[0:rb]
                du_ref[rows, :] = ((wv[2:3] * dv + wv[0:1] * up2) + wv[1:2] * up1).astype(du_ref.dtype)
                uv = u_ref[rows, :].astype(F32)
                out += [acc[0] + jnp.sum(up2 * uv, axis=0, keepdims=True),
                        acc[1] + jnp.sum(up1 * uv, axis=0, keepdims=True),
                        acc[2] + jnp.sum(dv * uv, axis=0, keepdims=True)]
            return tuple(out)

        sums = lax.fori_loop(0, seq // rb, conv_bwd, (jnp.zeros((1, tc), F32),) * 6)

        @pl.when((pl.program_id(0) == 0) & (j == 0))
        def _():
            dw_ref[...] = jnp.zeros_like(dw_ref)

        for half, off in enumerate((0, fh)):
            cols = pl.ds(pl.multiple_of(off + j * tc, LANES), tc)
            for k in range(3):
                dw_ref[k:k + 1, cols] += sums[3 * half + k]

    act = pl.BlockSpec((seq, tc), lambda b, j: (b, j))
    shape = _sds((batch * seq, fh), BF16)
    return pl.pallas_call(
        body,
        out_shape=(shape, shape, _sds((3, 2 * fh), F32)),
        grid=(batch, nf),
        in_specs=[
            pl.BlockSpec((seq, d), lambda b, j: (b, 0)),
            pl.BlockSpec((tc, d), lambda b, j: (j, 0)),
            act,
            act,
            act,
            act,
            pl.BlockSpec((3, tc), lambda b, j: (0, j)),
            pl.BlockSpec((3, tc), lambda b, j: (0, nf + j)),
        ],
        out_specs=(act, act, pl.BlockSpec((3, 2 * fh), lambda b, j: (0, 0))),
        scratch_shapes=[pltpu.VMEM((seq, tc), F32), pltpu.VMEM((seq + halo, tc), F32),
                        pltpu.VMEM((seq + halo, tc), F32)],
        compiler_params=pltpu.CompilerParams(dimension_semantics=("arbitrary", "arbitrary")),
        name=name,
    )(dx, w_down, ua, ub, av, bv, w, w)


def _merge_fwd(name, ycat, gl, bg):
    t, d2 = ycat.shape
    d = d2 // 2
    tm = _tile(t, 1024, 16)

    def body(y_ref, gl_ref, bg_ref, o_ref, g_ref):
        g = jax.nn.sigmoid(gl_ref[...].astype(F32) + bg_ref[...])
        g_ref[...] = g.astype(g_ref.dtype)
        prod = g * y_ref[...].astype(F32)
        o_ref[...] = (prod[:, 0:d] + prod[:, d:d2]).astype(o_ref.dtype)

    row = pl.BlockSpec((tm, d2), lambda i: (i, 0))
    return pl.pallas_call(
        body,
        out_shape=(_sds((t, d), BF16), _sds((t, d2), BF16)),
        grid=(t // tm,),
        in_specs=[row, row, pl.BlockSpec((1, d2), lambda i: (0, 0))],
        out_specs=(pl.BlockSpec((tm, d), lambda i: (i, 0)), row),
        compiler_params=pltpu.CompilerParams(dimension_semantics=("parallel",)),
        name=name,
    )(ycat, gl, bg)


def _merge_bwd(name, dm, ycat, gates, width, gl_off):
    t, d2 = ycat.shape
    d = d2 // 2
    tm = _tile(t, 1024, 16)
    wb = math.gcd(gl_off, d)
    nw = d // wb

    def body(dm_ref, y_ref, g_ref, dgl_ref, dy_ref, dbg_ref):
        g = g_ref[...].astype(F32)
        dmv = dm_ref[...].astype(F32)
        dgl = dmv * y_ref[...].astype(F32) * (g * (1.0 - g))
        dgl_ref[...] = dgl.astype(dgl_ref.dtype)
        dy_ref[...] = (dmv * g).astype(dy_ref.dtype)

        @pl.when(pl.program_id(2) == 0)
        def _():
            dbg_ref[...] = jnp.zeros_like(dbg_ref)

        dbg_ref[...] += jnp.sum(dgl, axis=0, keepdims=True)

    half = pl.BlockSpec((tm, wb), lambda h, j, i: (i, h * nw + j))
    vec = pl.BlockSpec((1, wb), lambda h, j, i: (0, h * nw + j))
    return pl.pallas_call(
        body,
        out_shape=(_sds((t, width), BF16), _sds((t, d2), BF16), _sds((1, d2), F32)),
        grid=(2, nw, t // tm),
        in_specs=[pl.BlockSpec((tm, wb), lambda h, j, i: (i, j)), half, half],
        out_specs=(pl.BlockSpec((tm, wb), lambda h, j, i: (i, gl_off // wb + h * nw + j)), half, vec),
        compiler_params=pltpu.CompilerParams(dimension_semantics=("parallel", "parallel", "arbitrary")),
        name=name,
    )(dm, ycat, gates)


def _log_sigmoid(z):
    return jnp.minimum(z, 0.0) - jnp.log1p(jnp.exp(-jnp.abs(z)))


def _forget_fwd(name, fl, bf, batch, seq):
    def body(fl_ref, bf_ref, o_ref):
        lf = _log_sigmoid(fl_ref[:, 0:LANES] + bf_ref[:, 0:LANES])
        acc = lf.T[0:HEADS, :]
        lane = lax.broadcasted_iota(jnp.int32, acc.shape, 1)
        k = 1
        while k < seq:
            acc = acc + jnp.where(lane >= k, pltpu.roll(acc, k, axis=1), 0.0)
            k *= 2
        o_ref[...] = acc

    return pl.pallas_call(
        body,
        out_shape=_sds((batch, HEADS, seq), F32),
        grid=(batch,),
        in_specs=[pl.BlockSpec((seq, F_PAD), lambda b: (b, 0)), pl.BlockSpec((1, F_PAD), lambda b: (0, 0))],
        out_specs=pl.BlockSpec((None, HEADS, seq), lambda b: (b, 0, 0)),
        compiler_params=pltpu.CompilerParams(dimension_semantics=("parallel",)),
        name=name,
    )(fl, bf)


def _forget_bwd(name, d_key, d_query, fl, bf, dproj, f_off, batch, seq):
    nfb = F_PAD // LANES

    def body(dk_ref, dq_ref, fl_ref, bf_ref, _, df_ref, dbf_ref):
        jj = pl.program_id(1)
        key_t = jnp.concatenate([dk_ref[...], jnp.zeros((LANES - HEADS, seq), F32)], axis=0).T
        acc = dq_ref[...] - key_t
        row = lax.broadcasted_iota(jnp.int32, acc.shape, 0)
        k = 1
        while k < seq:
            acc = acc + jnp.where(row < seq - k, pltpu.roll(acc, seq - k, axis=0), 0.0)
            k *= 2
        z = fl_ref[:, 0:LANES] + bf_ref[:, 0:LANES]
        col = lax.broadcasted_iota(jnp.int32, acc.shape, 1)
        df = jnp.where(col < HEADS, acc * jax.nn.sigmoid(-z), 0.0)
        df = jnp.where(jj == 0, df, 0.0)
        df_ref[...] = df.astype(df_ref.dtype)

        @pl.when((pl.program_id(0) == 0) & (jj == 0))
        def _():
            dbf_ref[...] = jnp.zeros_like(dbf_ref)

        dbf_ref[...] += jnp.sum(df, axis=0, keepdims=True)

    return pl.pallas_call(
        body,
        out_shape=(_sds(dproj.shape, dproj.dtype), _sds((1, LANES), F32)),
        grid=(batch, nfb),
        in_specs=[
            pl.BlockSpec((None, HEADS, seq), lambda b, j: (b, 0, 0)),
            pl.BlockSpec((seq, LANES), lambda b, j: (b, 0)),
            pl.BlockSpec((seq, F_PAD), lambda b, j: (b, 0)),
            pl.BlockSpec((1, F_PAD), lambda b, j: (0, 0)),
            pl.BlockSpec(memory_space=pl.ANY),
        ],
        out_specs=(pl.BlockSpec((seq, LANES), lambda b, j: (b, f_off // LANES + j)),
                   pl.BlockSpec((1, LANES), lambda b, j: (0, 0))),
        input_output_aliases={4: 0},
        compiler_params=pltpu.CompilerParams(dimension_semantics=("arbitrary", "arbitrary")),
        name=name,
    )(d_key, d_query, fl, bf, dproj)


def _dot(a, b, mode):
    return lax.dot_general(a, b, _DIMS[mode], preferred_element_type=F32)


def _attn_fwd(name, qkv, frow, batch, seq, tq):
    nq = seq // tq
    scale = 1.0 / math.sqrt(HEAD_DIM)

    def body(q_ref, k_ref, v_ref, f_ref, o_ref, lse_ref):
        i = pl.program_id(2)
        lane = lax.broadcasted_iota(jnp.int32, (1, LANES), 1)
        lo = lane < HEAD_DIM
        qs = q_ref[...] * scale
        qh = (jnp.where(lo, qs, 0.0).astype(BF16), jnp.where(lo, 0.0, qs).astype(BF16))
        row = lax.broadcasted_iota(jnp.int32, (tq, tq), 0)
        col = lax.broadcasted_iota(jnp.int32, (tq, tq), 1)

        def step(j, carry, diag):
            m0, l0, m1, l1, acc = carry
            start = pl.multiple_of(j * tq, tq)
            kj = k_ref[pl.ds(start, tq), :]
            vj = v_ref[pl.ds(start, tq), :]
            ms, ls, pvs, alphas = [], [], [], []
            for h, (m_old, l_old) in enumerate(((m0, l0), (m1, l1))):
                s = _dot(qh[h], kj, "nt") - f_ref[h:h + 1, pl.ds(start, tq)]
                if diag:
                    s = jnp.where(col <= row, s, NEG_BIG)
                m_new = jnp.maximum(m_old, jnp.max(s, axis=1, keepdims=True))
                p = jnp.exp(s - m_new)
                alpha = jnp.exp(m_old - m_new)
                ls.append(alpha * l_old + jnp.sum(p, axis=1, keepdims=True))
                ms.append(m_new)
                alphas.append(alpha)
                vh = jnp.where(lo, vj, 0.0) if h == 0 else jnp.where(lo, 0.0, vj)
                pvs.append(_dot(p.astype(BF16), vh.astype(BF16), "nn"))
            acc = acc * jnp.where(lo, alphas[0], alphas[1]) + (pvs[0] + pvs[1])
            return ms[0], ls[0], ms[1], ls[1], acc

        neg = jnp.full((tq, 1), NEG_BIG, F32)
        zero = jnp.zeros((tq, 1), F32)
        init = (neg, zero, neg, zero, jnp.zeros((tq, LANES), F32))
        carry = lax.fori_loop(0, i, lambda j, c: step(j, c, False), init)
        m0, l0, m1, l1, acc = step(i, carry, True)
        o_ref[...] = (acc / jnp.where(lo, l0, l1)).astype(o_ref.dtype)
        lse_ref[:, 0:1] = m0 + jnp.log(l0)
        lse_ref[:, 1:2] = m1 + jnp.log(l1)

    return pl.pallas_call(
        body,
        out_shape=(_sds((batch * seq, ATTN_WIDTH), BF16), _sds((HEAD_PAIRS, batch * seq, 2), F32)),
        grid=(batch, HEAD_PAIRS, nq),
        in_specs=[
            pl.BlockSpec((tq, LANES), lambda b, hp, i: (b * nq + i, 3 * hp)),
            pl.BlockSpec((seq, LANES), lambda b, hp, i: (b, 3 * hp + 1)),
            pl.BlockSpec((seq, LANES), lambda b, hp, i: (b, 3 * hp + 2)),
            pl.BlockSpec((None, None, 2, seq), lambda b, hp, i: (b, hp, 0, 0)),
        ],
        out_specs=(
            pl.BlockSpec((tq, LANES), lambda b, hp, i: (b * nq + i, hp)),
            pl.BlockSpec((None, tq, 2), lambda b, hp, i: (hp, b * nq + i, 0)),
        ),
        compiler_params=pltpu.CompilerParams(dimension_semantics=("parallel", "parallel", "parallel")),
        name=name,
    )(qkv, qkv, qkv, frow)


def _attn_bwd(name, qkv, do, o, lse, frow, dproj, qkv_off, batch, seq, tq):
    nq = seq // tq
    scale = 1.0 / math.sqrt(HEAD_DIM)

    def body(q_ref, k_ref, v_ref, do_ref, o_ref, lse_ref, f_ref, _, dqkv_ref, df_ref, drow_ref,
             dq_acc, dk_acc, dv_acc, df_acc):
        j = pl.program_id(2)
        lane = lax.broadcasted_iota(jnp.int32, (1, LANES), 1)
        lo = lane < HEAD_DIM
        masks = (lo, jnp.logical_not(lo))
        row = lax.broadcasted_iota(jnp.int32, (tq, tq), 0)
        col = lax.broadcasted_iota(jnp.int32, (tq, tq), 1)

        @pl.when(j == 0)
        def _():
            dq_acc[...] = jnp.zeros_like(dq_acc)
            drow_ref[...] = jnp.zeros_like(drow_ref)

        dk_acc[...] = jnp.zeros_like(dk_acc)
        dv_acc[...] = jnp.zeros_like(dv_acc)
        df_acc[...] = jnp.zeros_like(df_acc)
        kj = k_ref[...]
        vj = v_ref[...]
        kstart = pl.multiple_of(j * tq, tq)
        kh = tuple(jnp.where(mk, kj, 0.0).astype(BF16) for mk in masks)

        def step(i, diag):
            start = pl.multiple_of(i * tq, tq)
            rows = pl.ds(start, tq)
            qi = q_ref[rows, :] * scale
            doi = do_ref[rows, :]
            prod = doi.astype(F32) * o_ref[rows, :].astype(F32)
            lse_i = lse_ref[rows, :]
            dq_i = jnp.zeros((tq, LANES), F32)
            for h, mk in enumerate(masks):
                q_h = jnp.where(mk, qi, 0.0).astype(BF16)
                do_h = jnp.where(mk, doi, 0.0).astype(BF16)
                delta = jnp.sum(jnp.where(mk, prod, 0.0), axis=1, keepdims=True)
                s = _dot(q_h, kj, "nt") - f_ref[h:h + 1, pl.ds(kstart, tq)]
                p = jnp.exp(s - lse_i[:, h:h + 1])
                if diag:
                    p = jnp.where(col <= row, p, 0.0)
                ds = p * (_dot(do_h, vj, "nt") - delta)
                df_acc[h:h + 1, :] += jnp.sum(ds, axis=0, keepdims=True)
                drow_ref[rows, h:h + 1] += jnp.sum(ds, axis=1, keepdims=True)
                dsb = ds.astype(BF16)
                dv_acc[...] += _dot(p.astype(BF16), do_h, "tn")
                dk_acc[...] += _dot(dsb, q_h, "tn")
                dq_i = dq_i + _dot(dsb, kh[h], "nn")
            dq_acc[rows, :] += dq_i

        step(j, True)
        lax.fori_loop(j + 1, nq, lambda i, c: (step(i, False), c)[1], 0)
        dqkv_ref[:, 0:LANES] = (dq_acc[pl.ds(kstart, tq), :] * scale).astype(dqkv_ref.dtype)
        dqkv_ref[:, LANES:2 * LANES] = dk_acc[...].astype(dqkv_ref.dtype)
        dqkv_ref[:, 2 * LANES:3 * LANES] = dv_acc[...].astype(dqkv_ref.dtype)
        df_ref[...] = df_acc[...]

    full = lambda c: pl.BlockSpec((seq, LANES), lambda b, hp, j: (b, c(hp)))
    blk = lambda c: pl.BlockSpec((tq, LANES), lambda b, hp, j: (b * nq + j, c(hp)))
    return pl.pallas_call(
        body,
        out_shape=(_sds(dproj.shape, dproj.dtype), _sds((batch, HEAD_PAIRS, 2, seq), F32),
                   _sds((HEAD_PAIRS, batch * seq, 2), F32)),
        grid=(batch, HEAD_PAIRS, nq),
        in_specs=[
            full(lambda hp: 3 * hp),
            blk(lambda hp: 3 * hp + 1),
            blk(lambda hp: 3 * hp + 2),
            full(lambda hp: hp),
            full(lambda hp: hp),
            pl.BlockSpec((None, seq, 2), lambda b, hp, j: (hp, b, 0)),
            pl.BlockSpec((None, None, 2, seq), lambda b, hp, j: (b, hp, 0, 0)),
            pl.BlockSpec(memory_space=pl.ANY),
        ],
        out_specs=(
            pl.BlockSpec((tq, 3 * LANES), lambda b, hp, j: (b * nq + j, qkv_off // (3 * LANES) + hp)),
            pl.BlockSpec((None, None, 2, tq), lambda b, hp, j: (b, hp, 0, j)),
            pl.BlockSpec((None, seq, 2), lambda b, hp, j: (hp, b, 0)),
        ),
        scratch_shapes=[
            pltpu.VMEM((seq, LANES), F32),
            pltpu.VMEM((tq, LANES), F32),
            pltpu.VMEM((tq, LANES), F32),
            pltpu.VMEM((2, tq), F32),
        ],
        input_output_aliases={7: 0},
        compiler_params=pltpu.CompilerParams(dimension_semantics=("parallel", "parallel", "arbitrary")),
        name=name,
    )(qkv, qkv, qkv, do, o, lse, frow, dproj)


def _mesh_place():
    x, y, c = lax.axis_index("x"), lax.axis_index("y"), lax.axis_index("c")
    chips = [(1 - x, y), (x, 1 - y), (1 - x, 1 - y)]
    return x, y, c, chips


def _hbm_specs(n):
    return [pl.BlockSpec(memory_space=pl.ANY)] * n


def _half(shape2d, axis, which):
    size = shape2d[axis] // 2
    sl = pl.ds(pl.multiple_of(which * size, 16 if axis == 0 else LANES), size)
    return (sl, slice(None)) if axis == 0 else (slice(None), sl)


def _gather_weights(bigs, axes, smalls):
    nb, ns = len(bigs), len(smalls)
    arrays = list(bigs) + list(smalls)
    n = nb + ns

    def body(*refs):
        ins, outs = refs[:n], refs[n:2 * n]
        send_sems, recv_sems = refs[2 * n:]
        x, y, c, chips = _mesh_place()
        me = 2 * x + y
        sibling = (x, y, 1 - c)

        def half(a, which):
            return _half(arrays[a].shape, axes[a], which)

        def copy(a, k, src, dst, to):
            return pltpu.make_async_remote_copy(src_ref=src, dst_ref=dst, send_sem=send_sems.at[a, k],
                                                recv_sem=recv_sems.at[a, k], device_id=to, device_id_type=MESH)

        sends = []
        for a in range(n):
            for j, chip in enumerate(chips):
                if a < nb:
                    cp = copy(a, j, ins[a].at[half(a, c)], outs[a].at[(me,) + half(a, c)], (*chip, c))
                else:
                    cp = copy(a, j, ins[a], outs[a].at[me], (*chip, c))
                cp.start()
                sends.append(cp)
        for a in range(nb):
            for j, (px, py) in enumerate(chips):
                blk = outs[a].at[(2 * px + py,) + half(a, c)]
                copy(a, j, blk, blk, (px, py, c)).wait_recv()
                fwd = copy(a, 3 + j, blk, blk, sibling)
                fwd.start()
                sends.append(fwd)
        for a in range(nb, n):
            for j, (px, py) in enumerate(chips):
                blk = outs[a].at[2 * px + py]
                copy(a, j, blk, blk, (px, py, c)).wait_recv()
        for a in range(nb):
            for j, (px, py) in enumerate(chips):
                blk = outs[a].at[(2 * px + py,) + half(a, 1 - c)]
                copy(a, 3 + j, blk, blk, sibling).wait_recv()
        for cp in sends:
            cp.wait_send()

    outs = pl.pallas_call(
        body,
        out_shape=tuple(_sds((N_CHIPS,) + a.shape, a.dtype) for a in arrays),
        in_specs=_hbm_specs(n),
        out_specs=tuple(_hbm_specs(n)),
        scratch_shapes=[pltpu.SemaphoreType.DMA((n, 6)), pltpu.SemaphoreType.DMA((n, 6))],
        name="gather_weights",
    )(*arrays)
    me = 2 * lax.axis_index("x") + lax.axis_index("y")
    return tuple(lax.dynamic_update_index_in_dim(o, a, me, 0) for o, a in zip(outs, arrays))


def _gather_small(v):
    m_per, ncol = v.shape

    def body(x_ref, out_ref, send_sems, recv_sems, local_sem):
        x, y, c, chips = _mesh_place()
        me, sibling = (x, y, c), (x, y, 1 - c)

        def rows(px, py, pc):
            return out_ref.at[pl.ds((4 * px + 2 * py + pc) * m_per, m_per), :]

        def copy(k, block, to, src=None):
            return pltpu.make_async_remote_copy(src_ref=rows(*block) if src is None else src, dst_ref=rows(*block),
                                                send_sem=send_sems.at[k], recv_sem=recv_sems.at[k],
                                                device_id=to, device_id_type=MESH)

        mine = pltpu.make_async_copy(x_ref, rows(*me), local_sem)
        mine.start()
        first = [copy(0, me, sibling, src=x_ref)]
        first += [copy(1 + j, me, (*chip, c), src=x_ref) for j, chip in enumerate(chips)]
        for cp in first:
            cp.start()
        passed = [copy(4 + j, (*chip, c), sibling) for j, chip in enumerate(chips)]
        for j, chip in enumerate(chips):
            copy(1 + j, (*chip, c), me).wait_recv()
            passed[j].start()
        copy(0, sibling, me).wait_recv()
        for j, chip in enumerate(chips):
            copy(4 + j, (*chip, 1 - c), me).wait_recv()
        for cp in first + passed:
            cp.wait_send()
        mine.wait()

    return pl.pallas_call(
        body,
        out_shape=_sds((N_DEV * m_per, ncol), v.dtype),
        in_specs=[pl.BlockSpec(memory_space=pltpu.VMEM)],
        out_specs=pl.BlockSpec(memory_space=pltpu.VMEM),
        scratch_shapes=[pltpu.SemaphoreType.DMA((7,)), pltpu.SemaphoreType.DMA((7,)), pltpu.SemaphoreType.DMA],
        name="gather_small",
    )(v)


def _half_shape(shape2d, axis):
    return (shape2d[0] // 2, shape2d[1]) if axis == 0 else (shape2d[0], shape2d[1] // 2)


def _exchange_sibling(name, grads, axes):
    n = len(grads)

    def body(*refs):
        ins, outs = refs[:n], refs[n:2 * n]
        send_sems, recv_sems = refs[2 * n:]
        x, y, c, _ = _mesh_place()
        copies = []
        for a in range(n):
            src = ins[a].at[(slice(None),) + _half(grads[a].shape[1:], axes[a], 1 - c)]
            cp = pltpu.make_async_remote_copy(src_ref=src, dst_ref=outs[a], send_sem=send_sems.at[a],
                                              recv_sem=recv_sems.at[a], device_id=(x, y, 1 - c), device_id_type=MESH)
            cp.start()
            copies.append(cp)
        for cp in copies:
            cp.wait()

    return pl.pallas_call(
        body,
        out_shape=tuple(_sds((N_CHIPS,) + _half_shape(g.shape[1:], ax), g.dtype) for g, ax in zip(grads, axes)),
        in_specs=_hbm_specs(n),
        out_specs=tuple(_hbm_specs(n)),
        scratch_shapes=[pltpu.SemaphoreType.DMA((n,)), pltpu.SemaphoreType.DMA((n,))],
        name=name,
    )(*grads)


_HBM = pl.BlockSpec(memory_space=pltpu.HBM)
_SEM = pl.BlockSpec(memory_space=pltpu.SEMAPHORE)
_EFFECT = pltpu.SideEffectType.DATAFLOW_SIDE_EFFECTING


def _chip_copies(kind, srcs, lands, send_sems, recv_sems):
    x, y, c, chips = _mesh_place()
    copies = []
    for a in range(len(srcs)):
        for j, (px, py) in enumerate(chips):
            if kind == "gather":
                src, dst = srcs[a], lands[a].at[2 * x + y]
            else:
                src, dst = srcs[a].at[j], lands[a].at[j]
            copies.append(pltpu.make_async_remote_copy(src_ref=src, dst_ref=dst, send_sem=send_sems.at[3 * a + j],
                                                       recv_sem=recv_sems.at[3 * a + j], device_id=(px, py, c),
                                                       device_id_type=MESH))
    return copies


def _chips_start(name, kind, srcs):
    n = len(srcs)
    slots = N_CHIPS if kind == "gather" else 3
    lands = [lax.empty((slots,) + (s.shape if kind == "gather" else s.shape[1:]), s.dtype) for s in srcs]

    def body(*refs):
        for cp in _chip_copies(kind, refs[:n], refs[n:2 * n], refs[2 * n], refs[2 * n + 1]):
            cp.start()
        refs[-1][...] = jnp.zeros_like(refs[-1])

    outs = pl.pallas_call(
        body,
        out_shape=(pltpu.SemaphoreType.DMA((3 * n,)), pltpu.SemaphoreType.DMA((3 * n,)),
                   *[pltpu.HBM(v.shape, v.dtype) for v in (*srcs, *lands)], _sds((8, LANES), F32)),
        in_specs=[_HBM] * (2 * n),
        out_specs=(_SEM, _SEM, *[_HBM] * (2 * n), pl.BlockSpec(memory_space=pltpu.VMEM)),
        input_output_aliases={i: 2 + i for i in range(2 * n)},
        compiler_params=pltpu.CompilerParams(has_side_effects=_EFFECT),
        name=name,
    )(*[pltpu.with_memory_space_constraint(v, pltpu.HBM) for v in (*srcs, *lands)])
    return outs[:-1], outs[-1]


def _chips_wait(name, kind, handles, after):
    send_sems, recv_sems, *thru = handles
    n = len(thru) // 2

    def body(*refs):
        for cp in _chip_copies(kind, refs[:n], refs[n:2 * n], refs[2 * n], refs[2 * n + 1]):
            cp.wait_send()
            cp.wait_recv()

    outs = pl.pallas_call(
        body,
        out_shape=tuple(pltpu.HBM(v.shape, v.dtype) for v in thru),
        in_specs=[_HBM] * (2 * n) + [_SEM, _SEM, pl.BlockSpec(memory_space=pl.ANY)],
        out_specs=tuple([_HBM] * (2 * n)),
        input_output_aliases={i: i for i in range(2 * n)},
        compiler_params=pltpu.CompilerParams(has_side_effects=_EFFECT),
        name=name,
    )(*thru, send_sems, recv_sems, after)
    return outs[n:]


def _share_sibling(name, shards, axes):
    n = len(shards)

    def body(*refs):
        ins, outs = refs[:n], refs[n:2 * n]
        send_sems, recv_sems = refs[2 * n:]
        x, y, c, _ = _mesh_place()
        started = []
        for a in range(n):
            mine = _half(shards[a].shape, axes[a], c)
            theirs = _half(shards[a].shape, axes[a], 1 - c)
            cp = pltpu.make_async_remote_copy(src_ref=ins[a].at[mine], dst_ref=outs[a].at[mine],
                                              send_sem=send_sems.at[a], recv_sem=recv_sems.at[a],
                                              device_id=(x, y, 1 - c), device_id_type=MESH)
            cp.start()
            arrival = pltpu.make_async_remote_copy(src_ref=ins[a].at[theirs], dst_ref=outs[a].at[theirs],
                                                   send_sem=send_sems.at[a], recv_sem=recv_sems.at[a],
                                                   device_id=(x, y, 1 - c), device_id_type=MESH)
            started.append((cp, arrival))
        for cp, arrival in started:
            arrival.wait_recv()
            cp.wait_send()

    return pl.pallas_call(
        body,
        out_shape=tuple(_sds(s.shape, s.dtype) for s in shards),
        in_specs=_hbm_specs(n),
        out_specs=tuple(_hbm_specs(n)),
        scratch_shapes=[pltpu.SemaphoreType.DMA((n,)), pltpu.SemaphoreType.DMA((n,))],
        input_output_aliases={a: a for a in range(n)},
        name=name,
    )(*shards)


def _pair_sum(name, place, g, got, axis):
    hr, hc = got.shape[1:]

    def body(place_ref, g_ref, got_ref, o_ref):
        o_ref[...] = (g_ref[...] + got_ref[...]).astype(o_ref.dtype)

    blk = (None, hr, hc)
    mine = (lambda j, pr: (pr[2 + j], pr[1], 0)) if axis == 0 else (lambda j, pr: (pr[2 + j], 0, pr[1]))
    return pl.pallas_call(
        body,
        out_shape=_sds((N_CHIPS - 1, hr, hc), BF16),
        grid_spec=pltpu.PrefetchScalarGridSpec(
            num_scalar_prefetch=1,
            grid=(N_CHIPS - 1,),
            in_specs=[pl.BlockSpec(blk, mine), pl.BlockSpec(blk, lambda j, pr: (pr[2 + j], 0, 0))],
            out_specs=pl.BlockSpec(blk, lambda j, pr: (j, 0, 0)),
        ),
        compiler_params=pltpu.CompilerParams(dimension_semantics=("parallel",)),
        name=name,
    )(place, g, got)


def _chip_sum(name, place, g, got, arrivals, axis):
    _, r, cdim = g.shape
    hr, hc = got.shape[1:]

    def body(place_ref, g_ref, got_ref, arr_ref, o_ref):
        acc = g_ref[...] + got_ref[...]
        for j in range(3):
            acc = acc + arr_ref[j].astype(F32)
        o_ref[...] = acc

    blk = (None, hr, hc)
    mine = (lambda i, pr: (pr[0], pr[1], 0)) if axis == 0 else (lambda i, pr: (pr[0], 0, pr[1]))
    dest = (lambda i, pr: (pr[1], 0)) if axis == 0 else (lambda i, pr: (0, pr[1]))
    return pl.pallas_call(
        body,
        out_shape=_sds((r, cdim), F32),
        grid_spec=pltpu.PrefetchScalarGridSpec(
            num_scalar_prefetch=1,
            grid=(1,),
            in_specs=[
                pl.BlockSpec(blk, mine),
                pl.BlockSpec(blk, lambda i, pr: (pr[0], 0, 0)),
                pl.BlockSpec((3, hr, hc), lambda i, pr: (0, 0, 0)),
            ],
            out_specs=pl.BlockSpec((hr, hc), dest),
        ),
        compiler_params=pltpu.CompilerParams(dimension_semantics=("arbitrary",)),
        name=name,
    )(place, g, got, arrivals)


def _device_sum(name, gathered):
    m_per = gathered.shape[0] // N_DEV

    def body(g_ref, o_ref):
        acc = g_ref[0:m_per, :]
        for dev in range(1, N_DEV):
            acc = acc + g_ref[dev * m_per:(dev + 1) * m_per, :]
        o_ref[...] = acc

    return pl.pallas_call(body, out_shape=_sds((m_per, gathered.shape[1]), F32), name=name)(gathered)


def _adamw(name, w, g, m, v):
    r, cdim = w.shape
    if r % 8 == 0:
        tr, tcol = _tile(r, 256, 8), cdim
    else:
        tr, tcol = r, (_tile(cdim, 256, LANES) if cdim % LANES == 0 else cdim)
    blk = pl.BlockSpec((tr, tcol), lambda i, j: (i, j))
    grid = (r // tr, cdim // tcol)
    bc1 = 1.0 - ADAM_B1 ** ADAM_STEP
    bc2 = 1.0 - ADAM_B2 ** ADAM_STEP

    def body(w_ref, g_ref, m_ref, v_ref, d_ref, nm_ref, nv_ref):
        gv = g_ref[...]
        nm = ADAM_B1 * m_ref[...] + (1.0 - ADAM_B1) * gv
        nv = ADAM_B2 * v_ref[...] + (1.0 - ADAM_B2) * (gv * gv)
        d_ref[...] = -ADAM_LR * ((nm / bc1) / (jnp.sqrt(nv / bc2) + ADAM_EPS) + ADAM_WD * w_ref[...])
        nm_ref[...] = nm
        nv_ref[...] = nv

    shape = _sds(w.shape, F32)
    return pl.pallas_call(
        body,
        out_shape=(shape, shape, shape),
        grid=grid,
        in_specs=[blk] * 4,
        out_specs=(blk, blk, blk),
        compiler_params=pltpu.CompilerParams(dimension_semantics=("parallel", "parallel")),
        name=name,
    )(w, g, m, v)


def _cat_cols(g):
    return jnp.transpose(g, (1, 0, 2)).reshape(g.shape[1], N_CHIPS * g.shape[2])


def _split_cols(a):
    r, c4 = a.shape
    return jnp.transpose(a.reshape(r, N_CHIPS, c4 // N_CHIPS), (1, 0, 2))


def _local_step(x, target, w_int, late_weights, cmw, cfw, g1, b_f, b_gate, g2, gf,
                ffn_grads_ready, mix_grads_ready):
    batch, seq, d = x.shape
    t = batch * seq
    cw = d // 2
    fh = cfw.shape[1] // 2
    tc = LANES
    nct = cw // tc
    tq = min(512, seq)
    pc_w, qkv_w, gl_w = 3 * cw, 3 * ATTN_WIDTH, 2 * d
    qkv_off, gl_off, f_off = pc_w, pc_w + qkv_w, pc_w + qkv_w + gl_w
    width = f_off + F_PAD
    f_col = pc_w + qkv_w

    w_pc = w_int[:pc_w].reshape(3, nct, tc, d).transpose(1, 0, 2, 3).reshape(pc_w, d)
    w_qkv = w_int[pc_w:f_col].reshape(3, HEAD_PAIRS, LANES, d).transpose(1, 0, 2, 3).reshape(qkv_w, d)
    w_f = jnp.pad(w_int[f_col:f_col + HEADS], ((0, F_PAD - HEADS), (0, 0)))
    w_inp = jnp.concatenate([w_pc, w_qkv, w_int[f_col + HEADS:], w_f], axis=0)
    bf_pad = jnp.pad(b_f, ((0, 0), (0, F_PAD - HEADS)))

    x2d = x.reshape(t, d)
    tgt2d = target.reshape(t, d)

    h1 = _rms_fwd("norm_mix", x2d, g1)
    pc, qkv, gl, fl = _project("proj_in", h1, w_inp, [(pc_w, BF16), (qkv_w, BF16), (gl_w, BF16), (F_PAD, F32)])
    a_c = _conv_fwd("conv_mix", pc, cmw, batch, seq, tc)
    f_cum = _forget_fwd("forget_cumsum", fl, bf_pad, batch, seq)
    frow = f_cum.reshape(batch, HEAD_PAIRS, 2, seq)
    o, lse = _attn_fwd("attn_fwd", qkv, frow, batch, seq, tq)
    w_oc, w_oa, w_o, w_up, w_down = late_weights(o)
    ycat = _branch_out("branch_out", a_c, w_oc, o, w_oa)
    mg, gates = _merge_fwd("gate_merge", ycat, gl, b_gate)
    x2 = _mm("mix_out", mg, w_o, "nn", F32, m=t, n=d, k=d, add=x2d)
    h2 = _rms_fwd("norm_ffn", x2, g2)
    tcf = min(2 * LANES, fh)
    w_up2 = _cat_cols(w_up)
    hmid, ua, ub, ffn_a, ffn_b = _ffn_up_act("ffn_up_act", h2, w_up2, cfw, batch, seq, tcf)
    x3 = _mm("ffn_down", hmid, w_down, "nn", F32, m=t, n=d, k=fh, add=x2, tk=4096)

    dx3, dx3b, loss_row, d_gf = _final_loss("final_loss", x3, gf.reshape(1, d), tgt2d)
    dw_down = _mm("dw_down", hmid, dx3b, "tn", F32, m=fh, n=d, k=t, tm=256, tk=8192)
    du_a, du_b, d_cfw = _ffn_bwd("d_ffn", dx3b, w_down, ua, ub, ffn_a, ffn_b, cfw, batch, seq, tcf)
    ws = w_up.shape[2]
    dh2 = _ffn_up_bwd("d_ffn_up", du_a, du_b, w_up2)
    dw_up = _mm("dw_up_a", h2, du_a, "tn", F32, m=d, n=fh, k=t, tm=512, tn=ws, tk=4096, o3=N_CHIPS)
    dw_up = _mm("dw_up_b", h2, du_b, "tn", F32, m=d, n=fh, k=t, tm=512, tn=ws, tk=4096, o3=N_CHIPS, out=dw_up,
                o_off=fh)
    token = ffn_grads_ready(dw_up, dw_down)
    if token is not None:
        g2 = g2 + token[0:1, 0:1]
    dx2, d_g2 = _rms_bwd("d_norm_ffn", x2, dh2, g2, dx3)
    dm = _mm("d_merge", dx2, w_o, "nt", BF16, m=t, n=d, k=d)
    dw_o = _mm("dw_o", mg, dx2, "tn", F32, m=d, n=d, k=t, tk=2048)
    dproj, dycat, d_bg = _merge_bwd("d_gate_merge", dm, ycat, gates, width, gl_off)
    da_c, do = _branch_out_bwd("d_branch_out", dycat, w_oc, w_oa)
    dw_oc, dw_oa = _branch_out_dw("dw_branch_out", a_c, o, dycat)
    dproj, d_cmw = _conv_bwd("d_conv_mix", da_c, pc, cmw, dproj, batch, seq, tc)
    dproj, d_fkey, d_fquery = _attn_bwd("attn_bwd", qkv, do, o, lse, frow, dproj, qkv_off, batch, seq, tq)
    d_fquery = jnp.pad(jnp.transpose(d_fquery, (1, 0, 2)).reshape(t, HEADS), ((0, 0), (0, LANES - HEADS)))
    dproj, d_bf = _forget_bwd("d_forget", d_fkey.reshape(batch, HEADS, seq), d_fquery, fl, bf_pad, dproj, f_off,
                              batch, seq)
    dw_inp = _mm("dw_in", dproj, h1, "tn", F32, m=width, n=d, k=t, tm=256, tk=8192)
    d_pc = dw_inp[:pc_w].reshape(nct, 3, tc, d).transpose(1, 0, 2, 3).reshape(pc_w, d)
    d_qkv = dw_inp[qkv_off:gl_off].reshape(HEAD_PAIRS, 3, LANES, d).transpose(1, 0, 2, 3).reshape(qkv_w, d)
    dw_int = jnp.concatenate([d_pc, d_qkv, dw_inp[f_off:f_off + HEADS], dw_inp[gl_off:f_off]], axis=0)
    token = mix_grads_ready(dw_int, dw_oc, dw_oa, dw_o)
    dh1 = _mm("d_norm_mix", dproj, w_inp, "nn", BF16, m=t, n=d, k=width, tm=512, tk=8192, dep=token)
    grad_x, d_g1 = _rms_bwd("d_norm_mix_x", x2d, dh1, g1, dx2)
    smalls = (d_g1, d_g2, d_gf, d_bg, d_bf, d_cmw, d_cfw)
    return loss_row[0, 0], grad_x.reshape(batch, seq, d), smalls


def _pack_small(parts):
    flat = [p.reshape(-1) for p in parts]
    sizes = [f.shape[0] for f in flat]
    total = sum(sizes)
    padded = -(-total // (8 * LANES)) * (8 * LANES)
    vec = jnp.concatenate(flat + [jnp.zeros((padded - total,), F32)])
    offsets = [sum(sizes[:i]) for i in range(len(sizes))]
    return vec.reshape(padded // LANES, LANES), offsets


def kernel(x, norm_mix_g, w_in, b_f, b_gate, conv_mix_w, w_out_conv, w_out_attn, w_o, norm_ffn_g, w_up, conv_ffn_w, w_down, norm_f_g, loss_target, m_norm_mix_g, m_w_in, m_b_f, m_b_gate, m_conv_mix_w, m_w_out_conv, m_w_out_attn, m_w_o, m_norm_ffn_g, m_w_up, m_conv_ffn_w, m_w_down, m_norm_f_g, v_norm_mix_g, v_w_in, v_b_f, v_b_gate, v_conv_mix_w, v_w_out_conv, v_w_out_attn, v_w_o, v_norm_ffn_g, v_w_up, v_conv_ffn_w, v_w_down, v_norm_f_g):
    d = x.shape[-1]
    chip = 2 * lax.axis_index("x") + lax.axis_index("y")
    xi, yi = lax.axis_index("x"), lax.axis_index("y")
    peers = [2 * px + py for px, py in ((1 - xi, yi), (xi, 1 - yi), (1 - xi, 1 - yi))]
    place = jnp.stack([chip, lax.axis_index("c"), *peers]).astype(jnp.int32)

    t_in, t_m_in, t_v_in = (jnp.transpose(w[0]) for w in (w_in, m_w_in, v_w_in))

    def row_shards(a):
        return a.reshape(N_CHIPS, a.shape[0] // N_CHIPS, a.shape[1])

    def stacked(a):
        return a.reshape(N_CHIPS * a.shape[1], a.shape[2])

    a_in, a_cmw, a_cfw = _gather_weights([t_in.astype(BF16)], (1,), [conv_mix_w[0], conv_ffn_w[0]])
    late = [w[0].astype(BF16) for w in (w_out_conv, w_out_attn, w_o, w_up, w_down)]
    late_handles, late_token = _chips_start("gather_late_start", "gather", late)

    def late_weights(after):
        lands = _chips_wait("gather_late_wait", "gather", late_handles, after)
        a_oc, a_oa, a_o, a_up, a_down = (
            lax.dynamic_update_index_in_dim(buf, own, chip, 0) for buf, own in zip(lands, late))
        return _cat_cols(a_oc), _cat_cols(a_oa), stacked(a_o), a_up, stacked(a_down)

    pending = []

    def reduce_start(tag, names, grads, axes):
        got = _exchange_sibling("exchange_sibling_" + tag, grads, axes)
        sums = [_pair_sum("pair_sum_" + nm, place, g, r, ax) for nm, g, r, ax in zip(names, grads, got, axes)]
        handles, token = _chips_start("exchange_chips_start_" + tag, "reduce", sums)
        pending.append((tag, names, grads, axes, got, handles))
        return token

    def ffn_grads_ready(dw_up, dw_down):
        return reduce_start("ffn", ("w_up", "w_down"), [dw_up, row_shards(dw_down)], (0, 0))

    def mix_grads_ready(dw_int, dw_oc, dw_oa, dw_o):
        return reduce_start("mix", ("w_in", "w_out_conv", "w_out_attn", "w_o"),
                            [row_shards(dw_int), _split_cols(dw_oc), _split_cols(dw_oa), row_shards(dw_o)],
                            (1, 0, 0, 0))

    loss_local, grad_x, smalls = _local_step(
        x, loss_target, stacked(a_in), late_weights, _cat_cols(a_cmw),
        _cat_cols(a_cfw), norm_mix_g + late_token[0:1, 0:1], b_f, b_gate, norm_ffn_g, norm_f_g,
        ffn_grads_ready, mix_grads_ready)

    reduced = {}
    for tag, names, grads, axes, got, handles in pending:
        arrivals = _chips_wait("exchange_chips_wait_" + tag, "reduce", handles, grad_x)
        halves = [_chip_sum("chip_sum_" + nm, place, g, r, arr, ax)
                  for nm, g, r, arr, ax in zip(names, grads, got, arrivals, axes)]
        reduced.update(zip(names, _share_sibling("share_sibling_" + tag, halves, axes)))
    g_in, g_oc, g_oa, g_o, g_up, g_down = (
        reduced[nm] for nm in ("w_in", "w_out_conv", "w_out_attn", "w_o", "w_up", "w_down"))

    smalls = (*smalls, loss_local.reshape(1, 1))
    packed, offs = _pack_small(smalls)
    total = _device_sum("device_sum", _gather_small(packed)).reshape(-1)
    shapes = [s.shape for s in smalls]
    d_g1, d_g2, d_gf, d_bg, d_bf, d_cmw, d_cfw, loss = [
        total[o:o + math.prod(sh)].reshape(sh) for o, sh in zip(offs, shapes)]
    loss = loss[0, 0]
    d_bf = d_bf[:, :HEADS]
    cw_s, cf_s = conv_mix_w.shape[2], conv_ffn_w.shape[2]
    d_cmw = lax.dynamic_slice(d_cmw, (0, chip * cw_s), (3, cw_s))
    d_cfw = lax.dynamic_slice(d_cfw, (0, chip * cf_s), (3, cf_s))

    order = [
        ("norm_mix_g", norm_mix_g[0:1], d_g1, m_norm_mix_g, v_norm_mix_g),
        ("w_in", t_in, g_in, t_m_in, t_v_in),
        ("b_f", b_f, d_bf, m_b_f, v_b_f),
        ("b_gate", b_gate, d_bg, m_b_gate, v_b_gate),
        ("conv_mix_w", conv_mix_w[0], d_cmw, m_conv_mix_w[0], v_conv_mix_w[0]),
        ("w_out_conv", w_out_conv[0], g_oc, m_w_out_conv[0], v_w_out_conv[0]),
        ("w_out_attn", w_out_attn[0], g_oa, m_w_out_attn[0], v_w_out_attn[0]),
        ("w_o", w_o[0], g_o, m_w_o[0], v_w_o[0]),
        ("norm_ffn_g", norm_ffn_g, d_g2, m_norm_ffn_g, v_norm_ffn_g),
        ("w_up", w_up[0], g_up, m_w_up[0], v_w_up[0]),
        ("conv_ffn_w", conv_ffn_w[0], d_cfw, m_conv_ffn_w[0], v_conv_ffn_w[0]),
        ("w_down", w_down[0], g_down, m_w_down[0], v_w_down[0]),
        ("norm_f_g", norm_f_g.reshape(1, d), d_gf, m_norm_f_g.reshape(1, d), v_norm_f_g.reshape(1, d)),
    ]
    out_shapes = [norm_mix_g.shape, w_in.shape, b_f.shape, b_gate.shape, conv_mix_w.shape, w_out_conv.shape,
                  w_out_attn.shape, w_o.shape, norm_ffn_g.shape, w_up.shape, conv_ffn_w.shape, w_down.shape,
                  norm_f_g.shape]
    g_out, d_out, m_out, v_out = [], [], [], []
    for (nm, w, g, m, v), sh in zip(order, out_shapes):
        g = g.reshape(w.shape)
        delta, new_m, new_v = _adamw("adamw_" + nm, w, g, m.reshape(w.shape), v.reshape(w.shape))
        for dst, val in ((g_out, g), (d_out, delta), (m_out, new_m), (v_out, new_v)):
            dst.append((jnp.transpose(val) if nm == "w_in" else val).reshape(sh))
    return (loss, grad_x, *g_out, *d_out, *m_out, *v_out)
```
